```python
import numpy as np
import jax
import jax.numpy as jnp
from jax import lax

D_MODEL = 1024
BATCH = 8
SEQ = 4096
DEPTH = 4

CHUNK = 64
MEM_LEN = 256
EPS = 1e-6
RET_HEADS = 4
RET_QK_DIM = 128
RET_V_DIM = 256
RET_QK = RET_HEADS * RET_QK_DIM
RET_V = RET_HEADS * RET_V_DIM
ROPE_THETA = 10000.0
SSM_INNER = 2 * D_MODEL
SSM_HEAD_DIM = 64
SSM_HEADS = SSM_INNER // SSM_HEAD_DIM
SSM_GROUPS = 8
SSM_HEADS_PER_GROUP = SSM_HEADS // SSM_GROUPS
SSM_STATE = 128
SSM_CONV = 4
SSM_BC = SSM_GROUPS * SSM_STATE
SSM_CONV_DIM = SSM_INNER + 2 * SSM_BC
N_BRANCH = 2
IN_SIZES = (RET_QK, RET_QK, RET_V, RET_V, SSM_INNER, SSM_CONV_DIM, SSM_HEADS, N_BRANCH * D_MODEL)
IN_DIM = RET_QK + RET_QK + RET_V + RET_V + SSM_INNER + SSM_CONV_DIM + SSM_HEADS + N_BRANCH * D_MODEL
XA_HEADS = 4
XA_HEAD_DIM = D_MODEL // XA_HEADS
D_FF = 4 * D_MODEL

kernel_name = 'hybrid_retention_ssd_xattn_trunk'


def _rms(x):
    xf = x.astype(jnp.float32)
    return (xf * lax.rsqrt(jnp.mean(xf * xf, axis=-1, keepdims=True) + EPS)).astype(x.dtype)


def _rmsnorm(x, w):
    return _rms(x) * w


def _rope(t, cos, sin):
    t1, t2 = jnp.split(t, 2, axis=-1)
    return jnp.concatenate([t1 * cos - t2 * sin, t1 * sin + t2 * cos], axis=-1)


def _causal_conv(x, w, bias):
    c = x.shape[-1]
    out = lax.conv_general_dilated(
        x, w[:, None, :].astype(x.dtype), window_strides=(1,), padding=[(SSM_CONV - 1, 0)],
        dimension_numbers=('NWC', 'WIO', 'NWC'), feature_group_count=c)
    return out + bias


def _chunk_scan(ret_q, ret_k, ret_v, ssm_x, ssm_dt, ssm_b, ssm_c, ssm_a):
    b, s = ret_q.shape[:2]
    n_chunks = s // CHUNK

    def to_chunks(t):
        t = t.astype(jnp.float32).reshape((b, n_chunks, CHUNK) + t.shape[2:])
        return jnp.moveaxis(t, 1, 0)

    idx = jnp.arange(CHUNK, dtype=jnp.float32)
    log_gamma = jnp.log1p(-(2.0 ** (-5.0 - jnp.arange(RET_HEADS, dtype=jnp.float32))))
    rel = jnp.abs(idx[:, None] - idx[None, :])
    ret_intra = jnp.exp(log_gamma[:, None, None] * rel)
    ret_q_decay = jnp.exp(log_gamma[None, :] * (idx[:, None] + 1.0))[..., None]
    ret_k_decay = jnp.exp(log_gamma[None, :] * (CHUNK - 1.0 - idx[:, None]))[..., None]
    ret_chunk_decay = jnp.exp(log_gamma * CHUNK)
    a = ssm_a.astype(jnp.float32)

    def step(carry, inp):
        s_ret, h_ssm = carry
        q, k, v, xs, dt, bm, cm = inp
        sc = jnp.einsum('blhd,bmhd->bhlm', q, k) * ret_intra
        y_ret = (jnp.einsum('bhlm,bmhe->blhe', sc, v)
                 + jnp.einsum('blhd,bhde->blhe', q * ret_q_decay, s_ret))
        s_ret = (s_ret * ret_chunk_decay[None, :, None, None]
                 + jnp.einsum('blhd,blhe->bhde', k * ret_k_decay, v))
        cum = jnp.cumsum(dt * a, axis=1)
        cum_h = jnp.moveaxis(cum, 1, -1)
        seg = jnp.exp(-jnp.abs(cum_h[..., :, None] - cum_h[..., None, :]))
        cb = jnp.einsum('blgn,bmgn->bglm', cm, bm)
        xdt = xs * dt[..., None]
        y_ssm = (jnp.einsum('bghlm,bmghp->blghp', cb[:, :, None] * seg, xdt)
                 + jnp.einsum('blgn,bghpn->blghp', cm, h_ssm) * jnp.exp(cum)[..., None])
        cum_last = cum[:, -1]
        h_ssm = (h_ssm * jnp.exp(cum_last)[..., None, None]
                 + jnp.einsum('bmgn,bmghp->bghpn', bm,
                              xdt * jnp.exp(cum_last[:, None] - cum)[..., None]))
        return (s_ret, h_ssm), (y_ret, y_ssm)

    init = (jnp.zeros((b, RET_HEADS, RET_QK_DIM, RET_V_DIM), jnp.float32),
            jnp.zeros((b, SSM_GROUPS, SSM_HEADS_PER_GROUP, SSM_HEAD_DIM, SSM_STATE), jnp.float32))
    xs_in = tuple(to_chunks(t) for t in (ret_q, ret_k, ret_v, ssm_x, ssm_dt, ssm_b, ssm_c))
    _, (y_ret, y_ssm) = lax.scan(step, init, xs_in)

    def from_chunks(t):
        t = jnp.moveaxis(t, 0, 1)
        return t.reshape((b, s) + t.shape[3:])

    return from_chunks(y_ret), from_chunks(y_ssm)


def _hybrid_mixer(u, cos, sin, w_in, b_gate, conv_w, conv_b, dt_bias, a_log, d_skip,
                  ssm_norm, w_br_ret, w_br_ssm, w_out):
    b, s, _ = u.shape
    proj = u @ w_in
    offsets = [int(o) for o in np.cumsum(IN_SIZES)[:-1]]
    q, k, v, g, z, xbc, dt, gates = jnp.split(proj, offsets, axis=-1)
    q = _rope(q.reshape(b, s, RET_HEADS, RET_QK_DIM), cos, sin)
    k = _rope(k.reshape(b, s, RET_HEADS, RET_QK_DIM), cos, sin) * (RET_QK_DIM ** -0.5)
    v = v.reshape(b, s, RET_HEADS, RET_V_DIM)
    xbc = jax.nn.silu(_causal_conv(xbc, conv_w, conv_b))
    xs, bm, cm = jnp.split(xbc, [SSM_INNER, SSM_INNER + SSM_BC], axis=-1)
    xs = xs.reshape(b, s, SSM_GROUPS, SSM_HEADS_PER_GROUP, SSM_HEAD_DIM)
    bm = bm.reshape(b, s, SSM_GROUPS, SSM_STATE)
    cm = cm.reshape(b, s, SSM_GROUPS, SSM_STATE)
    dt = jax.nn.softplus(dt.astype(jnp.float32) + dt_bias.astype(jnp.float32))
    dt = dt.reshape(b, s, SSM_GROUPS, SSM_HEADS_PER_GROUP)
    ssm_a = -jnp.exp(a_log.astype(jnp.float32)).reshape(SSM_GROUPS, SSM_HEADS_PER_GROUP)
    y_ret, y_ssm = _chunk_scan(q, k, v, xs, dt, bm, cm, ssm_a)
    y_ret = _rms(y_ret.astype(u.dtype)).reshape(b, s, RET_V) * jax.nn.silu(g)
    y_ssm = y_ssm.astype(u.dtype) + xs * d_skip.reshape(SSM_GROUPS, SSM_HEADS_PER_GROUP, 1)
    y_ssm = y_ssm.reshape(b, s, SSM_INNER) * jax.nn.silu(z)
    y_ssm = _rms(y_ssm.reshape(b, s, SSM_GROUPS, SSM_INNER // SSM_GROUPS)).reshape(b, s, SSM_INNER) * ssm_norm
    gate_ret, gate_ssm = jnp.split(jax.nn.sigmoid(gates + b_gate), N_BRANCH, axis=-1)
    merged = gate_ret * (y_ret @ w_br_ret) + gate_ssm * (y_ssm @ w_br_ssm)
    return merged @ w_out


def _cross_attention(h, mem_n, wq, wkv, wo):
    b, s, _ = h.shape
    m = mem_n.shape[1]
    q = (h @ wq).reshape(b, s, XA_HEADS, XA_HEAD_DIM)
    k, v = jnp.split(mem_n @ wkv, 2, axis=-1)
    k = k.reshape(b, m, XA_HEADS, XA_HEAD_DIM)
    v = v.reshape(b, m, XA_HEADS, XA_HEAD_DIM)
    scores = jnp.einsum('bshd,bmhd->bhsm', q, k).astype(jnp.float32) * (XA_HEAD_DIM ** -0.5)
    p = jax.nn.softmax(scores, axis=-1).astype(v.dtype)
    o = jnp.einsum('bhsm,bmhd->bshd', p, v).reshape(b, s, D_MODEL)
    return o @ wo


def _sq_relu_mlp(h, w1, w2):
    return jnp.square(jax.nn.relu(h @ w1)) @ w2


def _fwd_setup_inputs(seed: int = 0) -> dict:
    key = jax.random.key(seed)
    ks = jax.random.split(key, 32)

    def nrm(k, shape, scale):
        return jax.random.normal(k, shape, jnp.float32) * scale

    def gain(k, shape):
        return 1.0 + nrm(k, shape, 0.02)

    L = DEPTH
    x = nrm(ks[0], (BATCH, SEQ, D_MODEL), 1.0)
    mem = nrm(ks[1], (BATCH, MEM_LEN, D_MODEL), 1.0)
    offset = jax.random.randint(ks[2], (BATCH, 1), 0, 8192, dtype=jnp.int32)
    positions = offset + jnp.arange(SEQ, dtype=jnp.int32)[None, :]
    dt0 = jnp.exp(jax.random.uniform(ks[3], (L, SSM_HEADS), jnp.float32)
                  * (jnp.log(0.1) - jnp.log(0.001)) + jnp.log(0.001))
    dt_bias = dt0 + jnp.log(-jnp.expm1(-dt0))
    a_log = jnp.log(jax.random.uniform(ks[4], (L, SSM_HEADS), jnp.float32, 1.0, 16.0))
    return {
        'x': x,
        'mem': mem,
        'positions': positions,
        'norm_mix': gain(ks[5], (L, D_MODEL)),
        'w_in': nrm(ks[6], (L, D_MODEL, IN_DIM), D_MODEL ** -0.5),
        'b_gate': nrm(ks[7], (L, N_BRANCH * D_MODEL), 0.01),
        'conv_w': nrm(ks[8], (L, SSM_CONV, SSM_CONV_DIM), SSM_CONV ** -0.5),
        'conv_b': nrm(ks[9], (L, SSM_CONV_DIM), 0.01),
        'dt_bias': dt_bias,
        'a_log': a_log,
        'd_skip': gain(ks[10], (L, SSM_HEADS)),
        'ssm_norm': gain(ks[11], (L, SSM_INNER)),
        'w_br_ret': nrm(ks[12], (L, RET_V, D_MODEL), RET_V ** -0.5),
        'w_br_ssm': nrm(ks[13], (L, SSM_INNER, D_MODEL), SSM_INNER ** -0.5),
        'w_out': nrm(ks[14], (L, D_MODEL, D_MODEL), D_MODEL ** -0.5),
        'norm_xa': gain(ks[15], (L, D_MODEL)),
        'norm_mem': gain(ks[16], (L, D_MODEL)),
        'xa_wq': nrm(ks[17], (L, D_MODEL, D_MODEL), D_MODEL ** -0.5),
        'xa_wkv': nrm(ks[18], (L, D_MODEL, 2 * D_MODEL), D_MODEL ** -0.5),
        'xa_wo': nrm(ks[19], (L, D_MODEL, D_MODEL), D_MODEL ** -0.5),
        'norm_mlp': gain(ks[20], (L, D_MODEL)),
        'mlp_w1': nrm(ks[21], (L, D_MODEL, D_FF), D_MODEL ** -0.5),
        'mlp_w2': nrm(ks[22], (L, D_FF, D_MODEL), D_FF ** -0.5),
        'norm_final': gain(ks[23], (D_MODEL,)),
    }


def _fwd_reference(x, mem, positions, norm_mix, w_in, b_gate, conv_w, conv_b, dt_bias, a_log,
              d_skip, ssm_norm, w_br_ret, w_br_ssm, w_out, norm_xa, norm_mem, xa_wq, xa_wkv,
              xa_wo, norm_mlp, mlp_w1, mlp_w2, norm_final):
    inv_freq = ROPE_THETA ** (-jnp.arange(0, RET_QK_DIM, 2, dtype=jnp.float32) / RET_QK_DIM)
    ang = positions.astype(jnp.float32)[..., None] * inv_freq
    cos = jnp.cos(ang)[:, :, None, :].astype(x.dtype)
    sin = jnp.sin(ang)[:, :, None, :].astype(x.dtype)
    for l in range(DEPTH):
        x = x + _hybrid_mixer(_rmsnorm(x, norm_mix[l]), cos, sin, w_in[l], b_gate[l],
                              conv_w[l], conv_b[l], dt_bias[l], a_log[l], d_skip[l],
                              ssm_norm[l], w_br_ret[l], w_br_ssm[l], w_out[l])
        x = x + _cross_attention(_rmsnorm(x, norm_xa[l]), _rmsnorm(mem, norm_mem[l]),
                                 xa_wq[l], xa_wkv[l], xa_wo[l])
        x = x + _sq_relu_mlp(_rmsnorm(x, norm_mlp[l]), mlp_w1[l], mlp_w2[l])
    return _rmsnorm(x, norm_final)


import jax as _jax
import jax.numpy as _jnp

TWIN_FORMAT = 'train_step'
FWD_PARAMS = ['x', 'mem', 'positions', 'norm_mix', 'w_in', 'b_gate', 'conv_w', 'conv_b', 'dt_bias', 'a_log', 'd_skip', 'ssm_norm', 'w_br_ret', 'w_br_ssm', 'w_out', 'norm_xa', 'norm_mem', 'xa_wq', 'xa_wkv', 'xa_wo', 'norm_mlp', 'mlp_w1', 'mlp_w2', 'norm_final']
TWIN_WEIGHTS = ['norm_mix', 'w_in', 'b_gate', 'conv_w', 'conv_b', 'dt_bias', 'a_log', 'd_skip', 'ssm_norm', 'w_br_ret', 'w_br_ssm', 'w_out', 'norm_xa', 'norm_mem', 'xa_wq', 'xa_wkv', 'xa_wo', 'norm_mlp', 'mlp_w1', 'mlp_w2', 'norm_final']
TWIN_DIFF_INPUT = 'x'
TWIN_INPUTS = ['x', 'mem', 'positions', 'norm_mix', 'w_in', 'b_gate', 'conv_w', 'conv_b', 'dt_bias', 'a_log', 'd_skip', 'ssm_norm', 'w_br_ret', 'w_br_ssm', 'w_out', 'norm_xa', 'norm_mem', 'xa_wq', 'xa_wkv', 'xa_wo', 'norm_mlp', 'mlp_w1', 'mlp_w2', 'norm_final', 'loss_target', 'm_norm_mix', 'm_w_in', 'm_b_gate', 'm_conv_w', 'm_conv_b', 'm_dt_bias', 'm_a_log', 'm_d_skip', 'm_ssm_norm', 'm_w_br_ret', 'm_w_br_ssm', 'm_w_out', 'm_norm_xa', 'm_norm_mem', 'm_xa_wq', 'm_xa_wkv', 'm_xa_wo', 'm_norm_mlp', 'm_mlp_w1', 'm_mlp_w2', 'm_norm_final', 'v_norm_mix', 'v_w_in', 'v_b_gate', 'v_conv_w', 'v_conv_b', 'v_dt_bias', 'v_a_log', 'v_d_skip', 'v_ssm_norm', 'v_w_br_ret', 'v_w_br_ssm', 'v_w_out', 'v_norm_xa', 'v_norm_mem', 'v_xa_wq', 'v_xa_wkv', 'v_xa_wo', 'v_norm_mlp', 'v_mlp_w1', 'v_mlp_w2', 'v_norm_final']
TWIN_OUTPUTS = ['loss', 'grad_x', 'grad_norm_mix', 'grad_w_in', 'grad_b_gate', 'grad_conv_w', 'grad_conv_b', 'grad_dt_bias', 'grad_a_log', 'grad_d_skip', 'grad_ssm_norm', 'grad_w_br_ret', 'grad_w_br_ssm', 'grad_w_out', 'grad_norm_xa', 'grad_norm_mem', 'grad_xa_wq', 'grad_xa_wkv', 'grad_xa_wo', 'grad_norm_mlp', 'grad_mlp_w1', 'grad_mlp_w2', 'grad_norm_final', 'delta_norm_mix', 'delta_w_in', 'delta_b_gate', 'delta_conv_w', 'delta_conv_b', 'delta_dt_bias', 'delta_a_log', 'delta_d_skip', 'delta_ssm_norm', 'delta_w_br_ret', 'delta_w_br_ssm', 'delta_w_out', 'delta_norm_xa', 'delta_norm_mem', 'delta_xa_wq', 'delta_xa_wkv', 'delta_xa_wo', 'delta_norm_mlp', 'delta_mlp_w1', 'delta_mlp_w2', 'delta_norm_final', 'new_m_norm_mix', 'new_m_w_in', 'new_m_b_gate', 'new_m_conv_w', 'new_m_conv_b', 'new_m_dt_bias', 'new_m_a_log', 'new_m_d_skip', 'new_m_ssm_norm', 'new_m_w_br_ret', 'new_m_w_br_ssm', 'new_m_w_out', 'new_m_norm_xa', 'new_m_norm_mem', 'new_m_xa_wq', 'new_m_xa_wkv', 'new_m_xa_wo', 'new_m_norm_mlp', 'new_m_mlp_w1', 'new_m_mlp_w2', 'new_m_norm_final', 'new_v_norm_mix', 'new_v_w_in', 'new_v_b_gate', 'new_v_conv_w', 'new_v_conv_b', 'new_v_dt_bias', 'new_v_a_log', 'new_v_d_skip', 'new_v_ssm_norm', 'new_v_w_br_ret', 'new_v_w_br_ssm', 'new_v_w_out', 'new_v_norm_xa', 'new_v_norm_mem', 'new_v_xa_wq', 'new_v_xa_wkv', 'new_v_xa_wo', 'new_v_norm_mlp', 'new_v_mlp_w1', 'new_v_mlp_w2', 'new_v_norm_final']
TWIN_LEAF_KINDS = {'loss': 'loss', 'grad_x': 'grad_x', 'grad_norm_mix': 'grad_w', 'grad_w_in': 'grad_w', 'grad_b_gate': 'grad_w', 'grad_conv_w': 'grad_w', 'grad_conv_b': 'grad_w', 'grad_dt_bias': 'grad_w', 'grad_a_log': 'grad_w', 'grad_d_skip': 'grad_w', 'grad_ssm_norm': 'grad_w', 'grad_w_br_ret': 'grad_w', 'grad_w_br_ssm': 'grad_w', 'grad_w_out': 'grad_w', 'grad_norm_xa': 'grad_w', 'grad_norm_mem': 'grad_w', 'grad_xa_wq': 'grad_w', 'grad_xa_wkv': 'grad_w', 'grad_xa_wo': 'grad_w', 'grad_norm_mlp': 'grad_w', 'grad_mlp_w1': 'grad_w', 'grad_mlp_w2': 'grad_w', 'grad_norm_final': 'grad_w', 'delta_norm_mix': 'delta_w', 'delta_w_in': 'delta_w', 'delta_b_gate': 'delta_w', 'delta_conv_w': 'delta_w', 'delta_conv_b': 'delta_w', 'delta_dt_bias': 'delta_w', 'delta_a_log': 'delta_w', 'delta_d_skip': 'delta_w', 'delta_ssm_norm': 'delta_w', 'delta_w_br_ret': 'delta_w', 'delta_w_br_ssm': 'delta_w', 'delta_w_out': 'delta_w', 'delta_norm_xa': 'delta_w', 'delta_norm_mem': 'delta_w', 'delta_xa_wq': 'delta_w', 'delta_xa_wkv': 'delta_w', 'delta_xa_wo': 'delta_w', 'delta_norm_mlp': 'delta_w', 'delta_mlp_w1': 'delta_w', 'delta_mlp_w2': 'delta_w', 'delta_norm_final': 'delta_w', 'new_m_norm_mix': 'new_m', 'new_m_w_in': 'new_m', 'new_m_b_gate': 'new_m', 'new_m_conv_w': 'new_m', 'new_m_conv_b': 'new_m', 'new_m_dt_bias': 'new_m', 'new_m_a_log': 'new_m', 'new_m_d_skip': 'new_m', 'new_m_ssm_norm': 'new_m', 'new_m_w_br_ret': 'new_m', 'new_m_w_br_ssm': 'new_m', 'new_m_w_out': 'new_m', 'new_m_norm_xa': 'new_m', 'new_m_norm_mem': 'new_m', 'new_m_xa_wq': 'new_m', 'new_m_xa_wkv': 'new_m', 'new_m_xa_wo': 'new_m', 'new_m_norm_mlp': 'new_m', 'new_m_mlp_w1': 'new_m', 'new_m_mlp_w2': 'new_m', 'new_m_norm_final': 'new_m', 'new_v_norm_mix': 'new_v', 'new_v_w_in': 'new_v', 'new_v_b_gate': 'new_v', 'new_v_conv_w': 'new_v', 'new_v_conv_b': 'new_v', 'new_v_dt_bias': 'new_v', 'new_v_a_log': 'new_v', 'new_v_d_skip': 'new_v', 'new_v_ssm_norm': 'new_v', 'new_v_w_br_ret': 'new_v', 'new_v_w_br_ssm': 'new_v', 'new_v_w_out': 'new_v', 'new_v_norm_xa': 'new_v', 'new_v_norm_mem': 'new_v', 'new_v_xa_wq': 'new_v', 'new_v_xa_wkv': 'new_v', 'new_v_xa_wo': 'new_v', 'new_v_norm_mlp': 'new_v', 'new_v_mlp_w1': 'new_v', 'new_v_mlp_w2': 'new_v', 'new_v_norm_final': 'new_v'}


def _forward(args):
    return _fwd_reference(*[args[k] for k in FWD_PARAMS])


def _output_shape():
    out = _jax.eval_shape(lambda: _forward(_fwd_setup_inputs(0)))
    return out.shape, out.dtype

N_MICROBATCH = 1
ADAM_LR = 0.001
ADAM_B1 = 0.9
ADAM_B2 = 0.999
ADAM_EPS = 1e-08
ADAM_WD = 0.01
ADAM_STEP = 10
PER_EXAMPLE_BATCH_AXIS = {'x': 0, 'mem': 0, 'positions': 0, 'loss_target': 0}
SHARED_INPUTS = []
_WEIGHT_DTYPES = {'norm_mix': _jnp.float32, 'w_in': _jnp.float32, 'b_gate': _jnp.float32, 'conv_w': _jnp.float32, 'conv_b': _jnp.float32, 'dt_bias': _jnp.float32, 'a_log': _jnp.float32, 'd_skip': _jnp.float32, 'ssm_norm': _jnp.float32, 'w_br_ret': _jnp.float32, 'w_br_ssm': _jnp.float32, 'w_out': _jnp.float32, 'norm_xa': _jnp.float32, 'norm_mem': _jnp.float32, 'xa_wq': _jnp.float32, 'xa_wkv': _jnp.float32, 'xa_wo': _jnp.float32, 'norm_mlp': _jnp.float32, 'mlp_w1': _jnp.float32, 'mlp_w2': _jnp.float32, 'norm_final': _jnp.float32}
MOMENT_SCALE = {'norm_mix': 1.476647e-01, 'w_in': 4.418009e-02, 'b_gate': 2.513955e-02, 'conv_w': 4.005171e-02, 'conv_b': 5.995613e-02, 'dt_bias': 9.303327e-02, 'a_log': 1.666836e-01, 'd_skip': 3.859154e-01, 'ssm_norm': 5.429911e-02, 'w_br_ret': 4.402864e-02, 'w_br_ssm': 7.593883e-02, 'w_out': 8.767145e-02, 'norm_xa': 1.487564e-02, 'norm_mem': 2.169301e-02, 'xa_wq': 1.429354e-02, 'xa_wkv': 1.497800e-02, 'xa_wo': 1.561274e-02, 'norm_mlp': 1.427951e-01, 'mlp_w1': 6.975697e-02, 'mlp_w2': 1.346259e-01, 'norm_final': 3.272091e+01}


def _to_microbatches(a, axis):
    t = _jnp.moveaxis(a, axis, 0)
    t = t.reshape((N_MICROBATCH, t.shape[0] // N_MICROBATCH) + t.shape[1:])
    return _jnp.moveaxis(t, 1, axis + 1)


def setup_inputs(seed: int = 0) -> dict:
    inp = _fwd_setup_inputs(seed)
    key = _jax.random.fold_in(_jax.random.key(seed), 7919)
    shape, _ = _output_shape()
    out = dict(inp)
    out["loss_target"] = _jax.random.normal(_jax.random.fold_in(key, 0), shape, _jnp.float32)
    for i, name in enumerate(TWIN_WEIGHTS):
        w = inp[name].astype(_jnp.float32)
        if MOMENT_SCALE is None:
            s = _jnp.sqrt(_jnp.mean(_jnp.square(w)) + 1e-30)
        else:
            s = MOMENT_SCALE[name]
        km, kv = _jax.random.split(_jax.random.fold_in(key, i + 1))
        out[name] = w
        out["m_" + name] = s * _jax.random.normal(km, w.shape, _jnp.float32)
        out["v_" + name] = (s * s) * _jax.random.uniform(kv, w.shape, _jnp.float32, 0.5, 1.5)
    if N_MICROBATCH > 1:
        for name, axis in PER_EXAMPLE_BATCH_AXIS.items():
            out[name] = _to_microbatches(out[name], axis)
    return {'x': out['x'], 'mem': out['mem'], 'positions': out['positions'], 'norm_mix': out['norm_mix'], 'w_in': out['w_in'], 'b_gate': out['b_gate'], 'conv_w': out['conv_w'], 'conv_b': out['conv_b'], 'dt_bias': out['dt_bias'], 'a_log': out['a_log'], 'd_skip': out['d_skip'], 'ssm_norm': out['ssm_norm'], 'w_br_ret': out['w_br_ret'], 'w_br_ssm': out['w_br_ssm'], 'w_out': out['w_out'], 'norm_xa': out['norm_xa'], 'norm_mem': out['norm_mem'], 'xa_wq': out['xa_wq'], 'xa_wkv': out['xa_wkv'], 'xa_wo': out['xa_wo'], 'norm_mlp': out['norm_mlp'], 'mlp_w1': out['mlp_w1'], 'mlp_w2': out['mlp_w2'], 'norm_final': out['norm_final'], 'loss_target': out['loss_target'], 'm_norm_mix': out['m_norm_mix'], 'm_w_in': out['m_w_in'], 'm_b_gate': out['m_b_gate'], 'm_conv_w': out['m_conv_w'], 'm_conv_b': out['m_conv_b'], 'm_dt_bias': out['m_dt_bias'], 'm_a_log': out['m_a_log'], 'm_d_skip': out['m_d_skip'], 'm_ssm_norm': out['m_ssm_norm'], 'm_w_br_ret': out['m_w_br_ret'], 'm_w_br_ssm': out['m_w_br_ssm'], 'm_w_out': out['m_w_out'], 'm_norm_xa': out['m_norm_xa'], 'm_norm_mem': out['m_norm_mem'], 'm_xa_wq': out['m_xa_wq'], 'm_xa_wkv': out['m_xa_wkv'], 'm_xa_wo': out['m_xa_wo'], 'm_norm_mlp': out['m_norm_mlp'], 'm_mlp_w1': out['m_mlp_w1'], 'm_mlp_w2': out['m_mlp_w2'], 'm_norm_final': out['m_norm_final'], 'v_norm_mix': out['v_norm_mix'], 'v_w_in': out['v_w_in'], 'v_b_gate': out['v_b_gate'], 'v_conv_w': out['v_conv_w'], 'v_conv_b': out['v_conv_b'], 'v_dt_bias': out['v_dt_bias'], 'v_a_log': out['v_a_log'], 'v_d_skip': out['v_d_skip'], 'v_ssm_norm': out['v_ssm_norm'], 'v_w_br_ret': out['v_w_br_ret'], 'v_w_br_ssm': out['v_w_br_ssm'], 'v_w_out': out['v_w_out'], 'v_norm_xa': out['v_norm_xa'], 'v_norm_mem': out['v_norm_mem'], 'v_xa_wq': out['v_xa_wq'], 'v_xa_wkv': out['v_xa_wkv'], 'v_xa_wo': out['v_xa_wo'], 'v_norm_mlp': out['v_norm_mlp'], 'v_mlp_w1': out['v_mlp_w1'], 'v_mlp_w2': out['v_mlp_w2'], 'v_norm_final': out['v_norm_final']}


def _loss(weights, diff, rest, loss_target):
    with _jax.named_scope("forward"):
        args = {**rest, TWIN_DIFF_INPUT: diff, **{k: w.astype(_WEIGHT_DTYPES[k]) for k, w in weights.items()}}
        y = _forward(args)
    with _jax.named_scope("loss_head"):
        err = _jnp.square(y.astype(_jnp.float32) - loss_target)
        return 0.5 * _jnp.sum(_jnp.mean(err, axis=-1)) if err.ndim else 0.5 * err


def _adamw(w, g, m, v):
    m = ADAM_B1 * m + (1.0 - ADAM_B1) * g
    v = ADAM_B2 * v + (1.0 - ADAM_B2) * _jnp.square(g)
    m_hat = m / (1.0 - ADAM_B1 ** ADAM_STEP)
    v_hat = v / (1.0 - ADAM_B2 ** ADAM_STEP)
    delta = -ADAM_LR * (m_hat / (_jnp.sqrt(v_hat) + ADAM_EPS) + ADAM_WD * w)
    return delta, m, v


def reference(x, mem, positions, norm_mix, w_in, b_gate, conv_w, conv_b, dt_bias, a_log, d_skip, ssm_norm, w_br_ret, w_br_ssm, w_out, norm_xa, norm_mem, xa_wq, xa_wkv, xa_wo, norm_mlp, mlp_w1, mlp_w2, norm_final, loss_target, m_norm_mix, m_w_in, m_b_gate, m_conv_w, m_conv_b, m_dt_bias, m_a_log, m_d_skip, m_ssm_norm, m_w_br_ret, m_w_br_ssm, m_w_out, m_norm_xa, m_norm_mem, m_xa_wq, m_xa_wkv, m_xa_wo, m_norm_mlp, m_mlp_w1, m_mlp_w2, m_norm_final, v_norm_mix, v_w_in, v_b_gate, v_conv_w, v_conv_b, v_dt_bias, v_a_log, v_d_skip, v_ssm_norm, v_w_br_ret, v_w_br_ssm, v_w_out, v_norm_xa, v_norm_mem, v_xa_wq, v_xa_wkv, v_xa_wo, v_norm_mlp, v_mlp_w1, v_mlp_w2, v_norm_final):
    given = dict(x=x, mem=mem, positions=positions, norm_mix=norm_mix, w_in=w_in, b_gate=b_gate, conv_w=conv_w, conv_b=conv_b, dt_bias=dt_bias, a_log=a_log, d_skip=d_skip, ssm_norm=ssm_norm, w_br_ret=w_br_ret, w_br_ssm=w_br_ssm, w_out=w_out, norm_xa=norm_xa, norm_mem=norm_mem, xa_wq=xa_wq, xa_wkv=xa_wkv, xa_wo=xa_wo, norm_mlp=norm_mlp, mlp_w1=mlp_w1, mlp_w2=mlp_w2, norm_final=norm_final, loss_target=loss_target, m_norm_mix=m_norm_mix, m_w_in=m_w_in, m_b_gate=m_b_gate, m_conv_w=m_conv_w, m_conv_b=m_conv_b, m_dt_bias=m_dt_bias, m_a_log=m_a_log, m_d_skip=m_d_skip, m_ssm_norm=m_ssm_norm, m_w_br_ret=m_w_br_ret, m_w_br_ssm=m_w_br_ssm, m_w_out=m_w_out, m_norm_xa=m_norm_xa, m_norm_mem=m_norm_mem, m_xa_wq=m_xa_wq, m_xa_wkv=m_xa_wkv, m_xa_wo=m_xa_wo, m_norm_mlp=m_norm_mlp, m_mlp_w1=m_mlp_w1, m_mlp_w2=m_mlp_w2, m_norm_final=m_norm_final, v_norm_mix=v_norm_mix, v_w_in=v_w_in, v_b_gate=v_b_gate, v_conv_w=v_conv_w, v_conv_b=v_conv_b, v_dt_bias=v_dt_bias, v_a_log=v_a_log, v_d_skip=v_d_skip, v_ssm_norm=v_ssm_norm, v_w_br_ret=v_w_br_ret, v_w_br_ssm=v_w_br_ssm, v_w_out=v_w_out, v_norm_xa=v_norm_xa, v_norm_mem=v_norm_mem, v_xa_wq=v_xa_wq, v_xa_wkv=v_xa_wkv, v_xa_wo=v_xa_wo, v_norm_mlp=v_norm_mlp, v_mlp_w1=v_mlp_w1, v_mlp_w2=v_mlp_w2, v_norm_final=v_norm_final)
    weights = {n: given[n] for n in TWIN_WEIGHTS}
    shared = {n: given[n] for n in SHARED_INPUTS}
    per_example = {n: given[n] for n in ['x', 'mem', 'positions']}
    grad_fn = _jax.value_and_grad(_loss, argnums=(0, 1))

    def one_microbatch(ex, loss_target):
        ex = dict(ex)
        diff = ex.pop(TWIN_DIFF_INPUT)
        return grad_fn(weights, diff, {**shared, **ex}, loss_target)

    if N_MICROBATCH == 1:
        loss, (grad_w, grad_x) = one_microbatch(per_example, given["loss_target"])
    else:
        def body(carry, xs):
            loss_sum, grad_sum = carry
            l_k, (gw_k, gx_k) = one_microbatch(xs[0], xs[1])
            with _jax.named_scope("update"):
                return (loss_sum + l_k, _jax.tree.map(_jnp.add, grad_sum, gw_k)), gx_k

        init = (_jnp.zeros((), _jnp.float32), _jax.tree.map(_jnp.zeros_like, weights))
        (loss, grad_w), grad_x = _jax.lax.scan(body, init, (per_example, given["loss_target"]))
    with _jax.named_scope("update"):
        delta_w, new_m, new_v = {}, {}, {}
        for n in TWIN_WEIGHTS:
            delta_w[n], new_m[n], new_v[n] = _adamw(weights[n], grad_w[n], given["m_" + n], given["v_" + n])
    return (loss, grad_x, *[grad_w[n] for n in TWIN_WEIGHTS], *[delta_w[n] for n in TWIN_WEIGHTS],
            *[new_m[n] for n in TWIN_WEIGHTS], *[new_v[n] for n in TWIN_WEIGHTS])
```

```python
import functools

import numpy as np
import jax
import jax.numpy as jnp
from jax import lax
from jax.experimental import pallas as pl
from jax.experimental.pallas import tpu as pltpu

F32 = jnp.float32
MXU_DTYPE = jnp.bfloat16
VMEM_LIMIT_BYTES = 56 * 1024 * 1024
LANES = 128
N_DEV = 8

D_MODEL = 1024
DEPTH = 4
CHUNK = 64
EPS = 1e-6
RET_HEADS, RET_QK_DIM, RET_V_DIM = 4, 128, 256
RET_QK, RET_V = 512, 1024
ROPE_THETA = 10000.0
SSM_INNER, SSM_HEAD_DIM, SSM_HEADS, SSM_GROUPS, SSM_STATE, SSM_CONV = 2048, 64, 32, 8, 128, 4
SSM_BC = 1024
SSM_CONV_DIM = 4096
IN_DIM = 11296
XA_HEADS, XA_HEAD_DIM = 4, 256
D_FF = 4096
ADAM_LR, ADAM_B1, ADAM_B2, ADAM_EPS, ADAM_WD, ADAM_STEP = 0.001, 0.9, 0.999, 1e-08, 0.01, 10

PROJ_W = 11264
COL_Z, COL_XBC, COL_GATES, COL_V, COL_G, COL_Q, COL_K = 0, 2048, 6144, 8192, 9216, 10240, 10752
DT_PAD = 128
N_LTILE = SSM_INNER // LANES

WEIGHTS = ['norm_mix', 'w_in', 'b_gate', 'conv_w', 'conv_b', 'dt_bias', 'a_log', 'd_skip', 'ssm_norm',
           'w_br_ret', 'w_br_ssm', 'w_out', 'norm_xa', 'norm_mem', 'xa_wq', 'xa_wkv', 'xa_wo', 'norm_mlp',
           'mlp_w1', 'mlp_w2', 'norm_final']
COL_SHARDED = ['w_in', 'conv_w', 'xa_wkv', 'mlp_w1']
ROW_SHARDED = ['w_br_ret', 'w_br_ssm', 'w_out', 'xa_wq', 'xa_wo', 'mlp_w2']
SHARDED = COL_SHARDED + ROW_SHARDED
SMALL = [n for n in WEIGHTS if n not in SHARDED]


def _cparams(sem=None):
    return pltpu.CompilerParams(dimension_semantics=sem, vmem_limit_bytes=VMEM_LIMIT_BYTES)


def _sds(shape, dtype):
    return jax.ShapeDtypeStruct(shape, dtype)


def _full(shape):
    nd = len(shape)
    return pl.BlockSpec(shape, lambda *_: (0,) * nd)


_DIMS = {'nn': (((1,), (0,)), ((), ())), 'nt': (((1,), (1,)), ((), ())), 'tn': (((0,), (0,)), ((), ()))}


def _dot(a, b, mode='nn'):
    return lax.dot_general(a.astype(MXU_DTYPE), b.astype(MXU_DTYPE), _DIMS[mode], preferred_element_type=F32)


@functools.partial(jax.custom_vjp, nondiff_argnums=(2,))
def _mm(a, b, mode):
    return _dot(a, b, mode)


def _mm_fwd(a, b, mode):
    return _dot(a, b, mode), (a, b)


def _mm_bwd(mode, res, g):
    a, b = res
    if mode == 'nn':
        return _dot(g, b, 'nt'), _dot(a, g, 'tn')
    if mode == 'nt':
        return _dot(g, b, 'nn'), _dot(g, a, 'tn')
    return _dot(b, g, 'nt'), _dot(a, g, 'nn')


_mm.defvjp(_mm_fwd, _mm_bwd)


def _dot_hi(a, b):
    return jnp.dot(a, b, precision=lax.Precision.HIGHEST, preferred_element_type=F32)


@jax.custom_vjp
def _rmul_hi(x, c, c_t):
    return _dot_hi(x, c)


_rmul_hi.defvjp(lambda x, c, c_t: (_dot_hi(x, c), (c, c_t)),
                lambda res, g: (_dot_hi(g, res[1]), jnp.zeros_like(res[0]), jnp.zeros_like(res[1])))


@jax.custom_vjp
def _lmul_hi(c, c_t, x):
    return _dot_hi(c, x)


_lmul_hi.defvjp(lambda c, c_t, x: (_dot_hi(c, x), (c, c_t)),
                lambda res, g: (jnp.zeros_like(res[0]), jnp.zeros_like(res[1]), _dot_hi(res[1], g)))


def _silu(x):
    return x * jax.nn.sigmoid(x)


def _softplus(x):
    pos = x > 0.0
    return jnp.where(pos, x, 0.0) + jnp.log1p(jnp.exp(jnp.where(pos, -x, x)))


def _rms(x):
    return x * lax.rsqrt(jnp.mean(x * x, axis=-1, keepdims=True) + EPS)


def _pick(n, pref):
    t = min(n, pref)
    while n % t:
        t //= 2
    return t


def _matmul(a, b, mode, name, *, extras=(), epi=None, out_dtypes=(F32,), tm=1024, tn=512, tk=1024):
    if mode == 'nn':
        (m, k), (k2, n) = a.shape, b.shape
    elif mode == 'nt':
        (m, k), (n, k2) = a.shape, b.shape
    else:
        (k, m), (k2, n) = a.shape, b.shape
    assert k == k2, (a.shape, b.shape, mode)
    tm, tn, tk = _pick(m, tm), _pick(n, tn), _pick(k, tk)
    nk = k // tk
    n_ex, n_out = len(extras), len(out_dtypes)

    def body(*refs):
        a_ref, b_ref = refs[0], refs[1]
        ex_refs = refs[2:2 + n_ex]
        o_refs = refs[2 + n_ex:2 + n_ex + n_out]
        acc_ref = refs[-1]
        kk = pl.program_id(2)

        @pl.when(kk == 0)
        def _():
            acc_ref[...] = jnp.zeros_like(acc_ref)

        acc_ref[...] += _dot(a_ref[...], b_ref[...], mode)

        @pl.when(kk == nk - 1)
        def _():
            acc = acc_ref[...]
            outs = epi(acc, *[r[...] for r in ex_refs]) if epi is not None else (acc,)
            for o_ref, o in zip(o_refs, outs):
                o_ref[...] = o.astype(o_ref.dtype)

    if mode == 'nn':
        a_spec = pl.BlockSpec((tm, tk), lambda i, j, kk: (i, kk))
        b_spec = pl.BlockSpec((tk, tn), lambda i, j, kk: (kk, j))
    elif mode == 'nt':
        a_spec = pl.BlockSpec((tm, tk), lambda i, j, kk: (i, kk))
        b_spec = pl.BlockSpec((tn, tk), lambda i, j, kk: (j, kk))
    else:
        a_spec = pl.BlockSpec((tk, tm), lambda i, j, kk: (kk, i))
        b_spec = pl.BlockSpec((tk, tn), lambda i, j, kk: (kk, j))
    mn_spec = pl.BlockSpec((tm, tn), lambda i, j, kk: (i, j))
    outs = pl.pallas_call(
        body, grid=(m // tm, n // tn, nk),
        in_specs=[a_spec, b_spec] + [mn_spec] * n_ex,
        out_specs=[mn_spec] * n_out,
        out_shape=[_sds((m, n), dt) for dt in out_dtypes],
        scratch_shapes=[pltpu.VMEM((tm, tn), F32)],
        compiler_params=_cparams(("parallel", "parallel", "arbitrary")),
        name=name,
    )(a, b, *extras)
    return outs[0] if n_out == 1 else outs


def _epi_add(acc, r):
    return (acc + r,)


def _rmsnorm_fn(x, w):
    return _rms(x) * w


def _rmsnorm(x, w, name):
    s, d = x.shape
    t = _pick(s, 512)

    def body(x_ref, w_ref, o_ref):
        o_ref[...] = _rmsnorm_fn(x_ref[...], w_ref[...]).astype(o_ref.dtype)

    return pl.pallas_call(
        body, grid=(s // t,),
        in_specs=[pl.BlockSpec((t, d), lambda i: (i, 0)), _full((1, d))],
        out_specs=pl.BlockSpec((t, d), lambda i: (i, 0)),
        out_shape=_sds((s, d), MXU_DTYPE), compiler_params=_cparams(("parallel",)), name=name,
    )(x, w.reshape(1, d))


def _rmsnorm_bwd(x, w, du, dres, name):
    s, d = x.shape
    t = _pick(s, 512)
    has_res = dres is not None

    def body(*refs):
        if has_res:
            x_ref, w_ref, du_ref, dres_ref, dx_ref, dw_ref = refs
        else:
            x_ref, w_ref, du_ref, dx_ref, dw_ref = refs
        _, vjp = jax.vjp(_rmsnorm_fn, x_ref[...], w_ref[...])
        dx, dw = vjp(du_ref[...])
        dx_ref[...] = dx + dres_ref[...] if has_res else dx

        @pl.when(pl.program_id(0) == 0)
        def _():
            dw_ref[...] = jnp.zeros_like(dw_ref)

        dw_ref[...] += dw

    row = pl.BlockSpec((t, d), lambda i: (i, 0))
    return pl.pallas_call(
        body, grid=(s // t,),
        in_specs=[row, _full((1, d)), row] + ([row] if has_res else []),
        out_specs=[row, _full((1, d))],
        out_shape=[_sds((s, d), F32), _sds((1, d), F32)],
        compiler_params=_cparams(("arbitrary",)), name=name,
    )(x, w.reshape(1, d), du, *([dres] if has_res else []))


CONV_CW = 2048
CONV_HALO = 8


def _conv_taps(cat, w, n_rows, off):
    acc = cat[off:off + n_rows, :] * w[3:4, :]
    for sft in (1, 2, 3):
        acc = acc + pltpu.roll(cat, sft, axis=0)[off:off + n_rows, :] * w[3 - sft:4 - sft, :]
    return acc


def _conv_fwd(proj, conv_w, conv_b, name):
    s = proj.shape[0]
    tr = _pick(s, 256)
    hb = tr // CONV_HALO
    col0 = COL_XBC // CONV_CW

    def body(prev_ref, x_ref, w_ref, b_ref, o_ref):
        i = pl.program_id(1)
        prev = jnp.where(i == 0, 0.0, prev_ref[...])
        cat = jnp.concatenate([prev, x_ref[...]], axis=0)
        o_ref[...] = _silu(_conv_taps(cat, w_ref[...], tr, CONV_HALO) + b_ref[...])

    return pl.pallas_call(
        body, grid=(SSM_CONV_DIM // CONV_CW, s // tr),
        in_specs=[pl.BlockSpec((CONV_HALO, CONV_CW), lambda j, i: (jnp.maximum(i * hb - 1, 0), j + col0)),
                  pl.BlockSpec((tr, CONV_CW), lambda j, i: (i, j + col0)),
                  pl.BlockSpec((SSM_CONV, CONV_CW), lambda j, i: (0, j)),
                  pl.BlockSpec((1, CONV_CW), lambda j, i: (0, j))],
        out_specs=pl.BlockSpec((tr, CONV_CW), lambda j, i: (i, j)),
        out_shape=_sds((s, SSM_CONV_DIM), F32),
        compiler_params=_cparams(("parallel", "parallel")), name=name,
    )(proj, proj, conv_w, conv_b.reshape(1, SSM_CONV_DIM))


def _conv_bwd(proj, conv_w, conv_b, dact, name):
    s = proj.shape[0]
    tr = _pick(s, 256)
    hb = tr // CONV_HALO
    nb = s // CONV_HALO
    nt = s // tr
    col0 = COL_XBC // CONV_CW
    h = CONV_HALO

    def body(prev_ref, x_ref, next_ref, w_ref, b_ref, da_ref, dan_ref, dx_ref, dw_ref, db_ref):
        i = pl.program_id(1)
        w = w_ref[...]
        prev = jnp.where(i == 0, 0.0, prev_ref[...])
        cat = jnp.concatenate([prev, x_ref[...], next_ref[...]], axis=0)
        pre = _conv_taps(cat, w, tr + h, h) + b_ref[...]
        dact_n = jnp.where(i == nt - 1, 0.0, dan_ref[...])
        dact_ext = jnp.concatenate([da_ref[...], dact_n], axis=0)
        sg = jax.nn.sigmoid(pre)
        dpre = dact_ext * (sg * (1.0 + pre * (1.0 - sg)))
        dx = dpre[:tr, :] * w[3:4, :]
        for sft in (1, 2, 3):
            dx = dx + pltpu.roll(dpre, tr + h - sft, axis=0)[:tr, :] * w[3 - sft:4 - sft, :]
        dx_ref[...] = dx

        @pl.when(i == 0)
        def _():
            dw_ref[...] = jnp.zeros_like(dw_ref)
            db_ref[...] = jnp.zeros_like(db_ref)

        dp = dpre[:tr, :]
        db_ref[...] += jnp.sum(dp, axis=0, keepdims=True)
        rows = [jnp.sum(dp * pltpu.roll(cat, sft, axis=0)[h:h + tr, :], axis=0, keepdims=True) for sft in (3, 2, 1)]
        rows.append(jnp.sum(dp * cat[h:h + tr, :], axis=0, keepdims=True))
        for r, row in enumerate(rows):
            dw_ref[r:r + 1, :] += row

    return pl.pallas_call(
        body, grid=(SSM_CONV_DIM // CONV_CW, nt),
        in_specs=[pl.BlockSpec((h, CONV_CW), lambda j, i: (jnp.maximum(i * hb - 1, 0), j + col0)),
                  pl.BlockSpec((tr, CONV_CW), lambda j, i: (i, j + col0)),
                  pl.BlockSpec((h, CONV_CW), lambda j, i: (jnp.minimum((i + 1) * hb, nb - 1), j + col0)),
                  pl.BlockSpec((SSM_CONV, CONV_CW), lambda j, i: (0, j)),
                  pl.BlockSpec((1, CONV_CW), lambda j, i: (0, j)),
                  pl.BlockSpec((tr, CONV_CW), lambda j, i: (i, j)),
                  pl.BlockSpec((h, CONV_CW), lambda j, i: (jnp.minimum((i + 1) * hb, nb - 1), j))],
        out_specs=[pl.BlockSpec((tr, CONV_CW), lambda j, i: (i, j)),
                   pl.BlockSpec((SSM_CONV, CONV_CW), lambda j, i: (0, j)),
                   pl.BlockSpec((1, CONV_CW), lambda j, i: (0, j))],
        out_shape=[_sds((s, SSM_CONV_DIM), F32), _sds((SSM_CONV, SSM_CONV_DIM), F32), _sds((1, SSM_CONV_DIM), F32)],
        compiler_params=_cparams(("parallel", "arbitrary")), name=name,
    )(proj, proj, proj, conv_w, conv_b.reshape(1, SSM_CONV_DIM), dact, dact)


def _scan_tables():
    idx = np.arange(CHUNK, dtype=np.float32)
    lg = np.log1p(-(2.0 ** (-5.0 - np.arange(RET_HEADS, dtype=np.float32)))).astype(np.float32)
    rel = np.abs(idx[:, None] - idx[None, :])
    r_intra = np.exp(lg[:, None, None] * rel).astype(np.float32)
    qd = np.exp(lg[None, :] * (idx[:, None] + 1.0)).astype(np.float32)
    kd = np.exp(lg[None, :] * (CHUNK - 1.0 - idx[:, None])).astype(np.float32)
    gam = [float(v) for v in np.exp(lg * CHUNK).astype(np.float32)]
    qd_e = np.repeat(qd, RET_QK_DIM, axis=1)
    kd_e = np.repeat(kd, RET_QK_DIM, axis=1)
    e = np.zeros((DT_PAD, SSM_INNER), np.float32)
    for hh in range(SSM_HEADS):
        e[hh, hh * SSM_HEAD_DIM:(hh + 1) * SSM_HEAD_DIM] = 1.0
    tri = np.tril(np.ones((CHUNK, CHUNK), np.float32))
    eye2 = np.concatenate([np.eye(CHUNK, dtype=np.float32)] * 2, axis=1)
    bdm = np.kron(np.eye(2, dtype=np.float32), np.ones((CHUNK, CHUNK), np.float32))
    last = np.zeros((CHUNK, LANES), np.float32)
    last[CHUNK - 1, :] = 1.0
    consts = [r_intra, qd_e, kd_e, e, e.T.copy(), tri, tri.T.copy(), eye2, bdm, last]
    return [jnp.asarray(c) for c in consts], gam


def _rope(t, cos2, sin2):
    return t * cos2 + pltpu.roll(t, RET_QK_DIM // 2, axis=1) * sin2


def _rope_t(d, cos2, sin2):
    return d * cos2 + pltpu.roll(d * sin2, RET_QK_DIM // 2, axis=1)


def _ret_step(q, k, v, st, r_intra, qd, kd, gamma):
    k = k * (RET_QK_DIM ** -0.5)
    sc = _mm(q, k, 'nt') * r_intra
    y = _mm(sc, v, 'nn') + _mm(q * qd, st, 'nn')
    st_new = st * gamma + _mm(k * kd, v, 'tn')
    return y, st_new


def _ssd_step(dtraw, dtb, a_e, xs, bm, cm, ht, e_t, et_t, tri, tri_t, eye2, bdm, last):
    dt = _softplus(dtraw + dtb)
    dte = _rmul_hi(dt, e_t, et_t)
    cum = _lmul_hi(tri, tri_t, dte * a_e)
    r = jnp.sum(cum * eye2, axis=0, keepdims=True)
    dlt = cum - r
    seg = jnp.exp(jnp.where(dlt > 0.0, -dlt, dlt))
    xdt = xs * dte
    b2 = jnp.concatenate([bm, bm], axis=0)
    cbp = _mm(cm, b2, 'nt')
    bd = jnp.concatenate([xdt, xdt], axis=0) * bdm
    clast = jnp.sum(cum * last, axis=0, keepdims=True)
    y = _mm(cbp * seg, bd, 'nn') + jnp.exp(cum) * _mm(cm, ht, 'nn')
    ht_new = jnp.exp(clast) * ht + _mm(bm, xdt * jnp.exp(clast - cum), 'tn')
    return y, ht_new


def _scan_in_specs(nc, rev):
    ch = (lambda c: nc - 1 - c) if rev else (lambda c: c)
    col = lambda w, blk: pl.BlockSpec((CHUNK, w), lambda c: (ch(c), blk))
    return [col(RET_QK, COL_Q // RET_QK), col(RET_QK, COL_K // RET_QK), col(RET_V, COL_V // RET_V),
            col(SSM_INNER, 0), col(SSM_BC, 2), col(SSM_BC, 3),
            col(DT_PAD, 0), col(LANES, 0), col(LANES, 0)]


def _const_specs(consts):
    return [_full(c.shape) for c in consts]


def _scan_fwd(proj, xbc, dtraw, cos2, sin2, a_e, dtb, name):
    s = proj.shape[0]
    nc = s // CHUNK
    consts, gam = _scan_tables()

    def body(q_ref, k_ref, v_ref, xs_ref, bm_ref, cm_ref, dt_ref, cos_ref, sin_ref, ae_ref, dtb_ref,
             ri_ref, qd_ref, kd_ref, e_ref, et_ref, tri_ref, trit_ref, eye_ref, bdm_ref, last_ref,
             yr_ref, ys_ref, sh_ref, hh_ref, st_sc, ht_sc):
        @pl.when(pl.program_id(0) == 0)
        def _():
            st_sc[...] = jnp.zeros_like(st_sc)
            ht_sc[...] = jnp.zeros_like(ht_sc)

        sh_ref[0] = st_sc[...]
        hh_ref[0] = ht_sc[...]
        cos2, sin2 = cos_ref[...], sin_ref[...]
        for h in range(RET_HEADS):
            ql = slice(h * RET_QK_DIM, (h + 1) * RET_QK_DIM)
            vl = slice(h * RET_V_DIM, (h + 1) * RET_V_DIM)
            y, st_new = _ret_step(_rope(q_ref[:, ql], cos2, sin2), _rope(k_ref[:, ql], cos2, sin2), v_ref[:, vl],
                                  st_sc[ql, :], ri_ref[h], qd_ref[:, ql], kd_ref[:, ql], gam[h])
            yr_ref[:, vl] = y
            st_sc[ql, :] = st_new
        dtraw_v, dtb_v = dt_ref[...], dtb_ref[...]
        tri, tri_t, eye2, bdm, last = tri_ref[...], trit_ref[...], eye_ref[...], bdm_ref[...], last_ref[...]
        for t in range(N_LTILE):
            tl = slice(t * LANES, (t + 1) * LANES)
            gl = slice((t // 2) * SSM_STATE, (t // 2 + 1) * SSM_STATE)
            y, ht_new = _ssd_step(dtraw_v, dtb_v, ae_ref[:, tl], xs_ref[:, tl], bm_ref[:, gl], cm_ref[:, gl],
                                  ht_sc[:, tl], e_ref[:, tl], et_ref[tl, :], tri, tri_t, eye2, bdm, last)
            ys_ref[:, tl] = y
            ht_sc[:, tl] = ht_new

    return pl.pallas_call(
        body, grid=(nc,),
        in_specs=_scan_in_specs(nc, False) + [_full((1, SSM_INNER)), _full((1, DT_PAD))] + _const_specs(consts),
        out_specs=[pl.BlockSpec((CHUNK, RET_V), lambda c: (c, 0)),
                   pl.BlockSpec((CHUNK, SSM_INNER), lambda c: (c, 0)),
                   pl.BlockSpec((1, RET_QK, RET_V_DIM), lambda c: (c, 0, 0)),
                   pl.BlockSpec((1, SSM_STATE, SSM_INNER), lambda c: (c, 0, 0))],
        out_shape=[_sds((s, RET_V), F32), _sds((s, SSM_INNER), F32),
                   _sds((nc, RET_QK, RET_V_DIM), F32), _sds((nc, SSM_STATE, SSM_INNER), F32)],
        scratch_shapes=[pltpu.VMEM((RET_QK, RET_V_DIM), F32), pltpu.VMEM((SSM_STATE, SSM_INNER), F32)],
        compiler_params=_cparams(("arbitrary",)), name=name,
    )(proj, proj, proj, xbc, xbc, xbc, dtraw, cos2, sin2, a_e, dtb, *consts)


def _scan_bwd(proj, xbc, dtraw, cos2, sin2, a_e, dtb, s_hist, h_hist, dyr, dys, dxs_skip, name):
    s = proj.shape[0]
    nc = s // CHUNK
    consts, gam = _scan_tables()
    rv = lambda c: nc - 1 - c

    def body(q_ref, k_ref, v_ref, xs_ref, bm_ref, cm_ref, dt_ref, cos_ref, sin_ref, ae_ref, dtb_ref,
             ri_ref, qd_ref, kd_ref, e_ref, et_ref, tri_ref, trit_ref, eye_ref, bdm_ref, last_ref,
             sh_ref, hh_ref, dyr_ref, dys_ref, dsk_ref,
             dq_ref, dk_ref, dv_ref, dxbc_ref, ddt_ref, dae_ref, ddtb_ref, dst_sc, dht_sc):
        @pl.when(pl.program_id(0) == 0)
        def _():
            dst_sc[...] = jnp.zeros_like(dst_sc)
            dht_sc[...] = jnp.zeros_like(dht_sc)
            dae_ref[...] = jnp.zeros_like(dae_ref)
            ddtb_ref[...] = jnp.zeros_like(ddtb_ref)

        cos2, sin2 = cos_ref[...], sin_ref[...]
        for h in range(RET_HEADS):
            ql = slice(h * RET_QK_DIM, (h + 1) * RET_QK_DIM)
            vl = slice(h * RET_V_DIM, (h + 1) * RET_V_DIM)
            step = functools.partial(_ret_step, r_intra=ri_ref[h], qd=qd_ref[:, ql], kd=kd_ref[:, ql], gamma=gam[h])
            _, vjp = jax.vjp(step, _rope(q_ref[:, ql], cos2, sin2), _rope(k_ref[:, ql], cos2, sin2), v_ref[:, vl],
                             sh_ref[0, ql, :])
            dq, dk, dv, dst = vjp((dyr_ref[:, vl], dst_sc[ql, :]))
            dq_ref[:, ql] = _rope_t(dq, cos2, sin2)
            dk_ref[:, ql] = _rope_t(dk, cos2, sin2)
            dv_ref[:, vl] = dv
            dst_sc[ql, :] = dst
        dtraw_v, dtb_v = dt_ref[...], dtb_ref[...]
        tri, tri_t, eye2, bdm, last = tri_ref[...], trit_ref[...], eye_ref[...], bdm_ref[...], last_ref[...]
        ddt = jnp.zeros((CHUNK, DT_PAD), F32)
        ddtb = jnp.zeros((1, DT_PAD), F32)
        for t in range(N_LTILE):
            tl = slice(t * LANES, (t + 1) * LANES)
            g = t // 2
            gl = slice(g * SSM_STATE, (g + 1) * SSM_STATE)
            step = functools.partial(_ssd_step, e_t=e_ref[:, tl], et_t=et_ref[tl, :], tri=tri, tri_t=tri_t,
                                     eye2=eye2, bdm=bdm, last=last)
            _, vjp = jax.vjp(step, dtraw_v, dtb_v, ae_ref[:, tl], xs_ref[:, tl], bm_ref[:, gl], cm_ref[:, gl],
                             hh_ref[0, :, tl])
            d_dt, d_dtb, d_ae, d_xs, d_bm, d_cm, d_ht = vjp((dys_ref[:, tl], dht_sc[:, tl]))
            ddt = ddt + d_dt
            ddtb = ddtb + d_dtb
            dae_ref[:, tl] += d_ae
            dxbc_ref[:, tl] = d_xs + dsk_ref[:, tl]
            bl = slice(SSM_INNER + g * SSM_STATE, SSM_INNER + (g + 1) * SSM_STATE)
            cl = slice(SSM_INNER + SSM_BC + g * SSM_STATE, SSM_INNER + SSM_BC + (g + 1) * SSM_STATE)
            if t % 2 == 0:
                dxbc_ref[:, bl] = d_bm
                dxbc_ref[:, cl] = d_cm
            else:
                dxbc_ref[:, bl] += d_bm
                dxbc_ref[:, cl] += d_cm
            dht_sc[:, tl] = d_ht
        ddt_ref[...] = ddt
        ddtb_ref[...] += ddtb

    return pl.pallas_call(
        body, grid=(nc,),
        in_specs=(_scan_in_specs(nc, True) + [_full((1, SSM_INNER)), _full((1, DT_PAD))] + _const_specs(consts)
                  + [pl.BlockSpec((1, RET_QK, RET_V_DIM), lambda c: (rv(c), 0, 0)),
                     pl.BlockSpec((1, SSM_STATE, SSM_INNER), lambda c: (rv(c), 0, 0)),
                     pl.BlockSpec((CHUNK, RET_V), lambda c: (rv(c), 0)),
                     pl.BlockSpec((CHUNK, SSM_INNER), lambda c: (rv(c), 0)),
                     pl.BlockSpec((CHUNK, SSM_INNER), lambda c: (rv(c), 0))]),
        out_specs=[pl.BlockSpec((CHUNK, RET_QK), lambda c: (rv(c), 0)),
                   pl.BlockSpec((CHUNK, RET_QK), lambda c: (rv(c), 0)),
                   pl.BlockSpec((CHUNK, RET_V), lambda c: (rv(c), 0)),
                   pl.BlockSpec((CHUNK, SSM_CONV_DIM), lambda c: (rv(c), 0)),
                   pl.BlockSpec((CHUNK, DT_PAD), lambda c: (rv(c), 0)),
                   _full((1, SSM_INNER)), _full((1, DT_PAD))],
        out_shape=[_sds((s, RET_QK), F32), _sds((s, RET_QK), F32), _sds((s, RET_V), F32),
                   _sds((s, SSM_CONV_DIM), F32), _sds((s, DT_PAD), F32),
                   _sds((1, SSM_INNER), F32), _sds((1, DT_PAD), F32)],
        scratch_shapes=[pltpu.VMEM((RET_QK, RET_V_DIM), F32), pltpu.VMEM((SSM_STATE, SSM_INNER), F32)],
        compiler_params=_cparams(("arbitrary",)), name=name,
    )(proj, proj, proj, xbc, xbc, xbc, dtraw, cos2, sin2, a_e, dtb, *consts, s_hist, h_hist, dyr, dys, dxs_skip)


POST_W = 256


def _post_ret(y, g):
    return _rms(y) * _silu(g)


def _post_ssm(y, xs, z, dsk, nw):
    return _rms((y + xs * dsk) * _silu(z)) * nw


def _post_specs(t):
    row = lambda w, blk0: (lambda j: pl.BlockSpec((t, w), lambda i: (i, blk0)))
    return [pl.BlockSpec((t, RET_V), lambda i: (i, 0)),
            pl.BlockSpec((t, RET_V), lambda i: (i, COL_G // RET_V)),
            pl.BlockSpec((t, SSM_INNER), lambda i: (i, 0)),
            pl.BlockSpec((t, SSM_INNER), lambda i: (i, 0)),
            pl.BlockSpec((t, SSM_INNER), lambda i: (i, COL_Z // SSM_INNER)),
            _full((1, SSM_INNER)), _full((1, SSM_INNER))]


def _post_fwd(y_ret, proj, y_ssm, xbc, dsk_e, ssm_norm, name):
    s = y_ret.shape[0]
    t = _pick(s, 256)

    def body(yr_ref, g_ref, ys_ref, xs_ref, z_ref, dsk_ref, nw_ref, or_ref, os_ref):
        for h in range(RET_V // POST_W):
            sl = slice(h * POST_W, (h + 1) * POST_W)
            or_ref[:, sl] = _post_ret(yr_ref[:, sl], g_ref[:, sl]).astype(or_ref.dtype)
        for g in range(SSM_INNER // POST_W):
            sl = slice(g * POST_W, (g + 1) * POST_W)
            os_ref[:, sl] = _post_ssm(ys_ref[:, sl], xs_ref[:, sl], z_ref[:, sl], dsk_ref[:, sl],
                                      nw_ref[:, sl]).astype(os_ref.dtype)

    return pl.pallas_call(
        body, grid=(s // t,), in_specs=_post_specs(t),
        out_specs=[pl.BlockSpec((t, RET_V), lambda i: (i, 0)), pl.BlockSpec((t, SSM_INNER), lambda i: (i, 0))],
        out_shape=[_sds((s, RET_V), MXU_DTYPE), _sds((s, SSM_INNER), MXU_DTYPE)],
        compiler_params=_cparams(("parallel",)), name=name,
    )(y_ret, proj, y_ssm, xbc, proj, dsk_e, ssm_norm.reshape(1, SSM_INNER))


def _post_bwd(y_ret, proj, y_ssm, xbc, dsk_e, ssm_norm, d_or, d_os, name):
    s = y_ret.shape[0]
    t = _pick(s, 256)

    def body(yr_ref, g_ref, ys_ref, xs_ref, z_ref, dsk_ref, nw_ref, dor_ref, dos_ref,
             dyr_ref, dg_ref, dys_ref, dxs_ref, dz_ref, ddsk_ref, dnw_ref):
        @pl.when(pl.program_id(0) == 0)
        def _():
            ddsk_ref[...] = jnp.zeros_like(ddsk_ref)
            dnw_ref[...] = jnp.zeros_like(dnw_ref)

        for h in range(RET_V // POST_W):
            sl = slice(h * POST_W, (h + 1) * POST_W)
            _, vjp = jax.vjp(_post_ret, yr_ref[:, sl], g_ref[:, sl])
            dyr_ref[:, sl], dg_ref[:, sl] = vjp(dor_ref[:, sl])
        for g in range(SSM_INNER // POST_W):
            sl = slice(g * POST_W, (g + 1) * POST_W)
            _, vjp = jax.vjp(_post_ssm, ys_ref[:, sl], xs_ref[:, sl], z_ref[:, sl], dsk_ref[:, sl], nw_ref[:, sl])
            dy, dxs, dz, ddsk, dnw = vjp(dos_ref[:, sl])
            dys_ref[:, sl] = dy
            dxs_ref[:, sl] = dxs
            dz_ref[:, sl] = dz
            ddsk_ref[:, sl] += ddsk
            dnw_ref[:, sl] += dnw

    rowv = pl.BlockSpec((t, RET_V), lambda i: (i, 0))
    rows = pl.BlockSpec((t, SSM_INNER), lambda i: (i, 0))
    return pl.pallas_call(
        body, grid=(s // t,), in_specs=_post_specs(t) + [rowv, rows],
        out_specs=[rowv, rowv, rows, rows, rows, _full((1, SSM_INNER)), _full((1, SSM_INNER))],
        out_shape=[_sds((s, RET_V), F32), _sds((s, RET_V), F32), _sds((s, SSM_INNER), F32),
                   _sds((s, SSM_INNER), F32), _sds((s, SSM_INNER), F32),
                   _sds((1, SSM_INNER), F32), _sds((1, SSM_INNER), F32)],
        compiler_params=_cparams(("arbitrary",)), name=name,
    )(y_ret, proj, y_ssm, xbc, proj, dsk_e, ssm_norm.reshape(1, SSM_INNER), d_or, d_os)


def _merge_fn(gr, gs, br, bs, yr, ys):
    return jax.nn.sigmoid(gr + br) * yr + jax.nn.sigmoid(gs + bs) * ys


def _merge_specs(t):
    row = pl.BlockSpec((t, D_MODEL), lambda i: (i, 0))
    return [pl.BlockSpec((t, D_MODEL), lambda i: (i, COL_GATES // D_MODEL)),
            pl.BlockSpec((t, D_MODEL), lambda i: (i, COL_GATES // D_MODEL + 1)),
            pl.BlockSpec((1, D_MODEL), lambda i: (0, 0)), pl.BlockSpec((1, D_MODEL), lambda i: (0, 1)), row, row]


def _merge_fwd(proj, b_gate, br_ret, br_ssm, name):
    s = proj.shape[0]
    t = _pick(s, 512)

    def body(gr_ref, gs_ref, br_ref, bs_ref, yr_ref, ys_ref, o_ref):
        o_ref[...] = _merge_fn(gr_ref[...], gs_ref[...], br_ref[...], bs_ref[...], yr_ref[...],
                               ys_ref[...]).astype(o_ref.dtype)

    bg = b_gate.reshape(1, 2 * D_MODEL)
    return pl.pallas_call(
        body, grid=(s // t,), in_specs=_merge_specs(t),
        out_specs=pl.BlockSpec((t, D_MODEL), lambda i: (i, 0)),
        out_shape=_sds((s, D_MODEL), MXU_DTYPE), compiler_params=_cparams(("parallel",)), name=name,
    )(proj, proj, bg, bg, br_ret, br_ssm)


def _merge_bwd(proj, b_gate, br_ret, br_ssm, dm, name):
    s = proj.shape[0]
    t = _pick(s, 512)

    def body(gr_ref, gs_ref, br_ref, bs_ref, yr_ref, ys_ref, dm_ref, dgt_ref, db_ref, dyr_ref, dys_ref):
        @pl.when(pl.program_id(0) == 0)
        def _():
            db_ref[...] = jnp.zeros_like(db_ref)

        _, vjp = jax.vjp(_merge_fn, gr_ref[...], gs_ref[...], br_ref[...], bs_ref[...], yr_ref[...], ys_ref[...])
        dgr, dgs, dbr, dbs, dyr, dys = vjp(dm_ref[...])
        dgt_ref[:, :D_MODEL] = dgr
        dgt_ref[:, D_MODEL:] = dgs
        db_ref[:, :D_MODEL] += dbr
        db_ref[:, D_MODEL:] += dbs
        dyr_ref[...] = dyr.astype(dyr_ref.dtype)
        dys_ref[...] = dys.astype(dys_ref.dtype)

    bg = b_gate.reshape(1, 2 * D_MODEL)
    row = pl.BlockSpec((t, D_MODEL), lambda i: (i, 0))
    return pl.pallas_call(
        body, grid=(s // t,), in_specs=_merge_specs(t) + [row],
        out_specs=[pl.BlockSpec((t, 2 * D_MODEL), lambda i: (i, 0)), _full((1, 2 * D_MODEL)), row, row],
        out_shape=[_sds((s, 2 * D_MODEL), F32), _sds((1, 2 * D_MODEL), F32),
                   _sds((s, D_MODEL), MXU_DTYPE), _sds((s, D_MODEL), MXU_DTYPE)],
        compiler_params=_cparams(("arbitrary",)), name=name,
    )(proj, proj, bg, bg, br_ret, br_ssm, dm)


def _attn_head(q, k, v):
    sc = _mm(q, k, 'nt') * (XA_HEAD_DIM ** -0.5)
    e = jnp.exp(sc - lax.stop_gradient(jnp.max(sc, axis=-1, keepdims=True)))
    p = e / jnp.sum(e, axis=-1, keepdims=True)
    return _mm(p, v, 'nn')


def _attn_fwd(q, kv, name):
    s = q.shape[0]
    m = kv.shape[0]
    t = _pick(s, 512)

    def body(q_ref, kv_ref, o_ref):
        for h in range(XA_HEADS):
            sl = slice(h * XA_HEAD_DIM, (h + 1) * XA_HEAD_DIM)
            vl = slice(D_MODEL + h * XA_HEAD_DIM, D_MODEL + (h + 1) * XA_HEAD_DIM)
            o_ref[:, sl] = _attn_head(q_ref[:, sl], kv_ref[:, sl], kv_ref[:, vl]).astype(o_ref.dtype)

    return pl.pallas_call(
        body, grid=(s // t,),
        in_specs=[pl.BlockSpec((t, D_MODEL), lambda i: (i, 0)), _full((m, 2 * D_MODEL))],
        out_specs=pl.BlockSpec((t, D_MODEL), lambda i: (i, 0)),
        out_shape=_sds((s, D_MODEL), MXU_DTYPE), compiler_params=_cparams(("parallel",)), name=name,
    )(q, kv)


def _attn_bwd(q, kv, d_o, name):
    s = q.shape[0]
    m = kv.shape[0]
    t = _pick(s, 512)

    def body(q_ref, kv_ref, do_ref, dq_ref, dkv_ref):
        @pl.when(pl.program_id(0) == 0)
        def _():
            dkv_ref[...] = jnp.zeros_like(dkv_ref)

        for h in range(XA_HEADS):
            sl = slice(h * XA_HEAD_DIM, (h + 1) * XA_HEAD_DIM)
            vl = slice(D_MODEL + h * XA_HEAD_DIM, D_MODEL + (h + 1) * XA_HEAD_DIM)
            _, vjp = jax.vjp(_attn_head, q_ref[:, sl], kv_ref[:, sl], kv_ref[:, vl])
            dq, dk, dv = vjp(do_ref[:, sl])
            dq_ref[:, sl] = dq.astype(dq_ref.dtype)
            dkv_ref[:, sl] += dk
            dkv_ref[:, vl] += dv

    row = pl.BlockSpec((t, D_MODEL), lambda i: (i, 0))
    return pl.pallas_call(
        body, grid=(s // t,), in_specs=[row, _full((m, 2 * D_MODEL)), row],
        out_specs=[row, _full((m, 2 * D_MODEL))],
        out_shape=[_sds((s, D_MODEL), MXU_DTYPE), _sds((m, 2 * D_MODEL), F32)],
        compiler_params=_cparams(("arbitrary",)), name=name,
    )(q, kv, d_o)


def _loss_head(x, w, target, name):
    s, d = x.shape
    t = _pick(s, 512)

    def body(x_ref, w_ref, t_ref, loss_ref, dx_ref, dw_ref):
        @pl.when(pl.program_id(0) == 0)
        def _():
            loss_ref[...] = jnp.zeros_like(loss_ref)
            dw_ref[...] = jnp.zeros_like(dw_ref)

        y, vjp = jax.vjp(_rmsnorm_fn, x_ref[...], w_ref[...])
        err = y - t_ref[...]
        loss_ref[...] += 0.5 * jnp.sum(jnp.sum(err * err, axis=-1, keepdims=True), axis=0, keepdims=True) / d
        dx, dw = vjp(err * (1.0 / d))
        dx_ref[...] = dx
        dw_ref[...] += dw

    row = pl.BlockSpec((t, d), lambda i: (i, 0))
    return pl.pallas_call(
        body, grid=(s // t,), in_specs=[row, _full((1, d)), row],
        out_specs=[_full((1, LANES)), row, _full((1, d))],
        out_shape=[_sds((1, LANES), F32), _sds((s, d), F32), _sds((1, d), F32)],
        compiler_params=_cparams(("arbitrary",)), name=name,
    )(x, w.reshape(1, d), target)


def _epi_sqrelu(acc):
    r = jnp.maximum(acc, 0.0)
    return acc, r * r


def _epi_sqrelu_bwd(acc, a):
    return (acc * (2.0 * jnp.maximum(a, 0.0)),)


def _rope_tables(positions):
    inv_freq = ROPE_THETA ** (-jnp.arange(0, RET_QK_DIM, 2, dtype=F32) / RET_QK_DIM)
    ang = positions.astype(F32)[:, None] * inv_freq
    cos, sin = jnp.cos(ang), jnp.sin(ang)
    return jnp.concatenate([cos, cos], axis=1), jnp.concatenate([-sin, sin], axis=1)


def _split_w_in(w_in):
    q, k, v, g = w_in[:, 0:512], w_in[:, 512:1024], w_in[:, 1024:2048], w_in[:, 2048:3072]
    z, xbc, dt, gates = w_in[:, 3072:5120], w_in[:, 5120:9216], w_in[:, 9216:9248], w_in[:, 9248:11296]
    main = jnp.concatenate([z, xbc, gates, v, g, q, k], axis=1)
    return main, jnp.pad(dt, ((0, 0), (0, DT_PAD - SSM_HEADS)))


def _merge_w_in(d_main, d_dt):
    z, xbc, gates = d_main[:, 0:2048], d_main[:, 2048:6144], d_main[:, 6144:8192]
    v, g, q, k = d_main[:, 8192:9216], d_main[:, 9216:10240], d_main[:, 10240:10752], d_main[:, 10752:11264]
    return jnp.concatenate([q, k, v, g, z, xbc, d_dt[:, :SSM_HEADS], gates], axis=1)


def _lanes_of_heads(v):
    return jnp.repeat(v, SSM_HEAD_DIM).reshape(1, SSM_INNER)


def _heads_of_lanes(v):
    return v.reshape(SSM_HEADS, SSM_HEAD_DIM).sum(axis=1)


def _layer_fwd(x, mem, cos2, sin2, p, l):
    n = lambda s: f"{s}_l{l}"
    sv = {'x0': x}
    u = _rmsnorm(x, p['norm_mix'], n("norm_mix"))
    w_main, w_dt = p['w_in_main'], p['w_in_dt']
    proj = _matmul(u, w_main, 'nn', n("in_proj"))
    dtraw = _matmul(u, w_dt, 'nn', n("in_proj_dt"))
    xbc = _conv_fwd(proj, p['conv_w'], p['conv_b'], n("conv"))
    a_e = _lanes_of_heads(-jnp.exp(p['a_log']))
    dtb = jnp.pad(p['dt_bias'], (0, DT_PAD - SSM_HEADS)).reshape(1, DT_PAD)
    y_ret, y_ssm, s_hist, h_hist = _scan_fwd(proj, xbc, dtraw, cos2, sin2, a_e, dtb, n("scan"))
    dsk_e = _lanes_of_heads(p['d_skip'])
    o_ret, o_ssm = _post_fwd(y_ret, proj, y_ssm, xbc, dsk_e, p['ssm_norm'], n("post"))
    br_ret = _matmul(o_ret, p['w_br_ret'], 'nn', n("br_ret"))
    br_ssm = _matmul(o_ssm, p['w_br_ssm'], 'nn', n("br_ssm"))
    merged = _merge_fwd(proj, p['b_gate'], br_ret, br_ssm, n("merge"))
    x1 = _matmul(merged, p['w_out'], 'nn', n("w_out"), extras=(x,), epi=_epi_add)
    sv.update(u=u, proj=proj, dtraw=dtraw, xbc=xbc, a_e=a_e, dtb=dtb, y_ret=y_ret, y_ssm=y_ssm, s_hist=s_hist,
              h_hist=h_hist, dsk_e=dsk_e, o_ret=o_ret, o_ssm=o_ssm, br_ret=br_ret, br_ssm=br_ssm, merged=merged, x1=x1)
    hq = _rmsnorm(x1, p['norm_xa'], n("norm_xa"))
    memn = _rmsnorm(mem, p['norm_mem'], n("norm_mem"))
    q = _matmul(hq, p['xa_wq'], 'nn', n("xa_q"))
    kv = _matmul(memn, p['xa_wkv'], 'nn', n("xa_kv"))
    o = _attn_fwd(q, kv, n("attn"))
    x2 = _matmul(o, p['xa_wo'], 'nn', n("xa_o"), extras=(x1,), epi=_epi_add)
    sv.update(hq=hq, memn=memn, q=q, kv=kv, o=o, x2=x2)
    hm = _rmsnorm(x2, p['norm_mlp'], n("norm_mlp"))
    a, act = _matmul(hm, p['mlp_w1'], 'nn', n("mlp_1"), epi=_epi_sqrelu, out_dtypes=(F32, MXU_DTYPE))
    x3 = _matmul(act, p['mlp_w2'], 'nn', n("mlp_2"), extras=(x2,), epi=_epi_add)
    sv.update(hm=hm, a=a, act=act)
    return x3, sv


def _layer_bwd(dx, mem, cos2, sin2, p, sv, l):
    n = lambda s: f"{s}_bwd_l{l}"
    g = {}
    g['mlp_w2'] = _matmul(sv['act'], dx, 'tn', n("mlp_2_dw"))
    da = _matmul(dx, p['mlp_w2'], 'nt', n("mlp_2_dx"), extras=(sv['a'],), epi=_epi_sqrelu_bwd, out_dtypes=(MXU_DTYPE,))
    g['mlp_w1'] = _matmul(sv['hm'], da, 'tn', n("mlp_1_dw"))
    dhm = _matmul(da, p['mlp_w1'], 'nt', n("mlp_1_dx"))
    dx2, g['norm_mlp'] = _rmsnorm_bwd(sv['x2'], p['norm_mlp'], dhm, dx, n("norm_mlp"))
    g['xa_wo'] = _matmul(sv['o'], dx2, 'tn', n("xa_o_dw"))
    d_o = _matmul(dx2, p['xa_wo'], 'nt', n("xa_o_dx"))
    dq, dkv = _attn_bwd(sv['q'], sv['kv'], d_o, n("attn"))
    g['xa_wq'] = _matmul(sv['hq'], dq, 'tn', n("xa_q_dw"))
    dhq = _matmul(dq, p['xa_wq'], 'nt', n("xa_q_dx"))
    g['xa_wkv'] = _matmul(sv['memn'], dkv, 'tn', n("xa_kv_dw"))
    dmemn = _matmul(dkv, p['xa_wkv'], 'nt', n("xa_kv_dx"))
    _, g['norm_mem'] = _rmsnorm_bwd(mem, p['norm_mem'], dmemn, None, n("norm_mem"))
    dx1, g['norm_xa'] = _rmsnorm_bwd(sv['x1'], p['norm_xa'], dhq, dx2, n("norm_xa"))
    g['w_out'] = _matmul(sv['merged'], dx1, 'tn', n("w_out_dw"))
    dmerged = _matmul(dx1, p['w_out'], 'nt', n("w_out_dx"))
    dgates, g['b_gate'], dbr_ret, dbr_ssm = _merge_bwd(sv['proj'], p['b_gate'], sv['br_ret'], sv['br_ssm'], dmerged, n("merge"))
    g['w_br_ret'] = _matmul(sv['o_ret'], dbr_ret, 'tn', n("br_ret_dw"))
    g['w_br_ssm'] = _matmul(sv['o_ssm'], dbr_ssm, 'tn', n("br_ssm_dw"))
    d_or = _matmul(dbr_ret, p['w_br_ret'], 'nt', n("br_ret_dx"))
    d_os = _matmul(dbr_ssm, p['w_br_ssm'], 'nt', n("br_ssm_dx"))
    dyr, dg, dys, dxs_skip, dz, ddsk_e, g['ssm_norm'] = _post_bwd(
        sv['y_ret'], sv['proj'], sv['y_ssm'], sv['xbc'], sv['dsk_e'], p['ssm_norm'], d_or, d_os, n("post"))
    g['d_skip'] = _heads_of_lanes(ddsk_e)
    dq_r, dk_r, dv_r, dxbc_act, ddtraw, dae, ddtb = _scan_bwd(
        sv['proj'], sv['xbc'], sv['dtraw'], cos2, sin2, sv['a_e'], sv['dtb'], sv['s_hist'], sv['h_hist'],
        dyr, dys, dxs_skip, n("scan"))
    g['a_log'] = _heads_of_lanes(dae) * (-jnp.exp(p['a_log']))
    g['dt_bias'] = ddtb[0, :SSM_HEADS]
    dxbc_raw, g['conv_w'], g['conv_b'] = _conv_bwd(sv['proj'], p['conv_w'], p['conv_b'], dxbc_act, n("conv"))
    dproj = jnp.concatenate([dz, dxbc_raw, dgates, dv_r, dg, dq_r, dk_r], axis=1)
    d_main = _matmul(sv['u'], dproj, 'tn', n("in_proj_dw"))
    d_dt = _matmul(sv['u'], ddtraw, 'tn', n("in_proj_dt_dw"))
    g['w_in'] = _merge_w_in(d_main, d_dt)
    du_dt = _matmul(ddtraw, p['w_in_dt'], 'nt', n("in_proj_dt_dx"))
    du = _matmul(dproj, p['w_in_main'], 'nt', n("in_proj_dx"), extras=(du_dt,), epi=_epi_add)
    dx0, g['norm_mix'] = _rmsnorm_bwd(sv['x0'], p['norm_mix'], du, dx1, n("norm_mix"))
    return dx0, g


def _local_step(x, mem, positions, w, loss_target):
    cos2, sin2 = _rope_tables(positions)
    saved, layers = [], []
    for l in range(DEPTH):
        p = {k: w[k][l] for k in WEIGHTS if k != 'norm_final'}
        p['w_in_main'], p['w_in_dt'] = _split_w_in(p['w_in'])
        x, sv = _layer_fwd(x, mem, cos2, sin2, p, l)
        saved.append(sv)
        layers.append(p)
    loss, dx, dnf = _loss_head(x, w['norm_final'], loss_target, "loss_head")
    grads = [None] * DEPTH
    for l in reversed(range(DEPTH)):
        dx, grads[l] = _layer_bwd(dx, mem, cos2, sin2, layers[l], saved[l], l)
    out = {}
    for k in WEIGHTS:
        if k == 'norm_final':
            out[k] = dnf.reshape(D_MODEL)
        else:
            out[k] = jnp.stack([grads[l][k].reshape(w[k].shape[1:]) for l in range(DEPTH)])
    return loss, dx, out


MESH = pl.DeviceIdType.MESH
ANY_SPEC = pl.BlockSpec(memory_space=pl.ANY)


def _mesh_pos():
    return lax.axis_index("x"), lax.axis_index("y"), lax.axis_index("c")


def _other_chips(x, y):
    return [(1 - x, y), (x, 1 - y), (1 - x, 1 - y)]


def _all_gather(arrs, name):
    na = len(arrs)

    def body(*refs):
        x_refs, o_refs = refs[:na], refs[na:2 * na]
        send_sems, recv_sems, local_sems = refs[2 * na:]
        x, y, c = _mesh_pos()
        me, sib = (x, y, c), (x, y, 1 - c)
        chips = _other_chips(x, y)

        def copy(a, k, block, to, src=None):
            dst = o_refs[a].at[4 * block[0] + 2 * block[1] + block[2]]
            return pltpu.make_async_remote_copy(src_ref=dst if src is None else src, dst_ref=dst,
                                                send_sem=send_sems.at[a, k], recv_sem=recv_sems.at[a, k],
                                                device_id=to, device_id_type=MESH)

        mine = [pltpu.make_async_copy(x_refs[a], o_refs[a].at[4 * x + 2 * y + c], local_sems.at[a]) for a in range(na)]
        for cp in mine:
            cp.start()
        first = []
        for a in range(na):
            first.append(copy(a, 0, me, sib, src=x_refs[a]))
            first += [copy(a, 1 + j, me, (*chip, c), src=x_refs[a]) for j, chip in enumerate(chips)]
        for cp in first:
            cp.start()
        passed = []
        for a in range(na):
            for j, chip in enumerate(chips):
                copy(a, 1 + j, (*chip, c), me).wait_recv()
                cp = copy(a, 4 + j, (*chip, c), sib)
                cp.start()
                passed.append(cp)
        for a in range(na):
            copy(a, 0, sib, me).wait_recv()
            for j, chip in enumerate(chips):
                copy(a, 4 + j, (*chip, 1 - c), me).wait_recv()
        for cp in first + passed:
            cp.wait_send()
        for cp in mine:
            cp.wait()

    return pl.pallas_call(
        body, in_specs=[ANY_SPEC] * na, out_specs=[ANY_SPEC] * na,
        out_shape=[_sds((N_DEV,) + a.shape, a.dtype) for a in arrs],
        scratch_shapes=[pltpu.SemaphoreType.DMA((na, 7)), pltpu.SemaphoreType.DMA((na, 7)), pltpu.SemaphoreType.DMA((na,))],
        name=name,
    )(*arrs)


def _exchange_cores(arrs, name):
    na = len(arrs)

    def body(*refs):
        a_refs, o_refs = refs[:na], refs[na:2 * na]
        send_sems, recv_sems = refs[2 * na:]
        x, y, c = _mesh_pos()
        cps = [pltpu.make_async_remote_copy(src_ref=a_refs[a].at[1 - c], dst_ref=o_refs[a], send_sem=send_sems.at[a],
                                            recv_sem=recv_sems.at[a], device_id=(x, y, 1 - c), device_id_type=MESH)
               for a in range(na)]
        for cp in cps:
            cp.start()
        for cp in cps:
            cp.wait()

    return pl.pallas_call(
        body, in_specs=[ANY_SPEC] * na, out_specs=[ANY_SPEC] * na,
        out_shape=[_sds(a.shape[1:], a.dtype) for a in arrs],
        scratch_shapes=[pltpu.SemaphoreType.DMA((na,)), pltpu.SemaphoreType.DMA((na,))],
        name=name,
    )(*arrs)


def _exchange_chips(arrs, name):
    na = len(arrs)

    def body(*refs):
        a_refs, o_refs = refs[:na], refs[na:2 * na]
        send_sems, recv_sems, local_sems = refs[2 * na:]
        x, y, c = _mesh_pos()
        my_chip = 2 * x + y
        chips = _other_chips(x, y)
        mine = [pltpu.make_async_copy(a_refs[a].at[my_chip], o_refs[a].at[my_chip], local_sems.at[a]) for a in range(na)]
        for cp in mine:
            cp.start()
        cps = [pltpu.make_async_remote_copy(src_ref=a_refs[a].at[2 * px + py], dst_ref=o_refs[a].at[my_chip],
                                            send_sem=send_sems.at[a, j], recv_sem=recv_sems.at[a, j],
                                            device_id=(px, py, c), device_id_type=MESH)
               for a in range(na) for j, (px, py) in enumerate(chips)]
        for cp in cps:
            cp.start()
        for a in range(na):
            for j, (px, py) in enumerate(chips):
                pltpu.make_async_remote_copy(src_ref=a_refs[a].at[2 * px + py], dst_ref=o_refs[a].at[2 * px + py],
                                             send_sem=send_sems.at[a, j], recv_sem=recv_sems.at[a, j],
                                             device_id=(px, py, c), device_id_type=MESH).wait_recv()
        for cp in cps:
            cp.wait_send()
        for cp in mine:
            cp.wait()

    return pl.pallas_call(
        body, in_specs=[ANY_SPEC] * na, out_specs=[ANY_SPEC] * na,
        out_shape=[_sds(a.shape, a.dtype) for a in arrs],
        scratch_shapes=[pltpu.SemaphoreType.DMA((na, 3)), pltpu.SemaphoreType.DMA((na, 3)), pltpu.SemaphoreType.DMA((na,))],
        name=name,
    )(*arrs)


def _as_rows(a, lead):
    return a.reshape(a.shape[:lead] + (-1, a.shape[-1]))


def _add_halves(a, other, c_idx, name):
    a3, o2 = _as_rows(a, 1), _as_rows(other, 0)
    rows, cols = o2.shape
    tr = _pick(rows, 256)

    def body(c_ref, a_ref, o_ref, out_ref):
        out_ref[...] = a_ref[0] + o_ref[...]

    out = pl.pallas_call(
        body,
        grid_spec=pltpu.PrefetchScalarGridSpec(
            num_scalar_prefetch=1, grid=(rows // tr,),
            in_specs=[pl.BlockSpec((1, tr, cols), lambda i, c_ref: (c_ref[0], i, 0)),
                      pl.BlockSpec((tr, cols), lambda i, c_ref: (i, 0))],
            out_specs=pl.BlockSpec((tr, cols), lambda i, c_ref: (i, 0))),
        out_shape=_sds((rows, cols), F32), compiler_params=_cparams(("parallel",)), name=name,
    )(c_idx, a3, o2)
    return out.reshape(other.shape)


def _all_reduce_small(v, name):
    r = v.shape[0]

    def body(v_ref, o_ref, slots, send_sems, recv_sems):
        x, y, c = _mesh_pos()
        me = 4 * x + 2 * y + c
        slots[me] = v_ref[...]
        cps = []
        for k in range(1, N_DEV):
            px = 1 - x if k & 4 else x
            py = 1 - y if k & 2 else y
            pc = 1 - c if k & 1 else c
            cps.append(pltpu.make_async_remote_copy(src_ref=v_ref, dst_ref=slots.at[me], send_sem=send_sems.at[k - 1],
                                                    recv_sem=recv_sems.at[k - 1], device_id=(px, py, pc), device_id_type=MESH))
        for cp in cps:
            cp.start()
        for cp in cps:
            cp.wait()
        acc = slots[0]
        for d in range(1, N_DEV):
            acc = acc + slots[d]
        o_ref[...] = acc

    vm = pl.BlockSpec(memory_space=pltpu.VMEM)
    return pl.pallas_call(
        body, in_specs=[vm], out_specs=vm, out_shape=_sds((r, LANES), F32),
        scratch_shapes=[pltpu.VMEM((N_DEV, r, LANES), F32), pltpu.SemaphoreType.DMA((N_DEV - 1,)),
                        pltpu.SemaphoreType.DMA((N_DEV - 1,))],
        compiler_params=pltpu.CompilerParams(vmem_limit_bytes=VMEM_LIMIT_BYTES), name=name,
    )(v)


def _adamw(w, g_slots, m, v, name):
    w2, m2, v2, g3 = _as_rows(w, 0), _as_rows(m, 0), _as_rows(v, 0), _as_rows(g_slots, 1)
    ns = g3.shape[0]
    rows, cols = w2.shape
    tr = _pick(rows, 256 if cols <= 1024 else 128)

    def body(w_ref, g_ref, m_ref, v_ref, go_ref, d_ref, mo_ref, vo_ref):
        g = g_ref[0]
        for i in range(1, ns):
            g = g + g_ref[i]
        m_new = ADAM_B1 * m_ref[...] + (1.0 - ADAM_B1) * g
        v_new = ADAM_B2 * v_ref[...] + (1.0 - ADAM_B2) * (g * g)
        m_hat = m_new / (1.0 - ADAM_B1 ** ADAM_STEP)
        v_hat = v_new / (1.0 - ADAM_B2 ** ADAM_STEP)
        go_ref[...] = g
        d_ref[...] = -ADAM_LR * (m_hat / (jnp.sqrt(v_hat) + ADAM_EPS) + ADAM_WD * w_ref[...])
        mo_ref[...] = m_new
        vo_ref[...] = v_new

    row = pl.BlockSpec((tr, cols), lambda i: (i, 0))
    outs = pl.pallas_call(
        body, grid=(rows // tr,),
        in_specs=[row, pl.BlockSpec((ns, tr, cols), lambda i: (0, i, 0)), row, row],
        out_specs=[row] * 4, out_shape=[_sds((rows, cols), F32)] * 4,
        compiler_params=_cparams(("parallel",)), name=name,
    )(w2, g3, m2, v2)
    return [o.reshape(w.shape) for o in outs]


_ARG_NAMES = (['x', 'mem', 'positions'] + WEIGHTS + ['loss_target'] + ['m_' + n for n in WEIGHTS]
              + ['v_' + n for n in WEIGHTS])


def _gathered_to_full(name, g):
    dev, depth, r, c = g.shape
    if name in COL_SHARDED:
        return jnp.transpose(g, (1, 2, 0, 3)).reshape(depth, r, dev * c)
    return jnp.transpose(g, (1, 0, 2, 3)).reshape(depth, dev * r, c)


def _full_to_scatter(name, g):
    depth = g.shape[0]
    if name in COL_SHARDED:
        r, c = g.shape[1], g.shape[2] // N_DEV
        return jnp.transpose(g.reshape(depth, r, 4, 2, c), (3, 2, 0, 1, 4))
    r, c = g.shape[1] // N_DEV, g.shape[2]
    return jnp.transpose(g.reshape(depth, 4, 2, r, c), (2, 1, 0, 3, 4))


def _pack_small(parts):
    flat = jnp.concatenate([p.reshape(-1) for p in parts])
    n = flat.shape[0]
    rows = -(-n // LANES)
    rows = -(-rows // 8) * 8
    return jnp.pad(flat, (0, rows * LANES - n)).reshape(rows, LANES)


def _unpack_small(packed, shapes, skip):
    flat = packed.reshape(-1)
    out, off = [], skip
    for shp in shapes:
        n = int(np.prod(shp))
        out.append(flat[off:off + n].reshape(shp))
        off += n
    return out


def kernel(x, mem, positions, norm_mix, w_in, b_gate, conv_w, conv_b, dt_bias, a_log, d_skip, ssm_norm, w_br_ret, w_br_ssm, w_out, norm_xa, norm_mem, xa_wq, xa_wkv, xa_wo, norm_mlp, mlp_w1, mlp_w2, norm_final, loss_target, m_norm_mix, m_w_in, m_b_gate, m_conv_w, m_conv_b, m_dt_bias, m_a_log, m_d_skip, m_ssm_norm, m_w_br_ret, m_w_br_ssm, m_w_out, m_norm_xa, m_norm_mem, m_xa_wq, m_xa_wkv, m_xa_wo, m_norm_mlp, m_mlp_w1, m_mlp_w2, m_norm_final, v_norm_mix, v_w_in, v_b_gate, v_conv_w, v_conv_b, v_dt_bias, v_a_log, v_d_skip, v_ssm_norm, v_w_br_ret, v_w_br_ssm, v_w_out, v_norm_xa, v_norm_mem, v_xa_wq, v_xa_wkv, v_xa_wo, v_norm_mlp, v_mlp_w1, v_mlp_w2, v_norm_final):
    d = dict(zip(_ARG_NAMES, (x, mem, positions, norm_mix, w_in, b_gate, conv_w, conv_b, dt_bias, a_log, d_skip, ssm_norm, w_br_ret, w_br_ssm, w_out, norm_xa, norm_mem, xa_wq, xa_wkv, xa_wo, norm_mlp, mlp_w1, mlp_w2, norm_final, loss_target, m_norm_mix, m_w_in, m_b_gate, m_conv_w, m_conv_b, m_dt_bias, m_a_log, m_d_skip, m_ssm_norm, m_w_br_ret, m_w_br_ssm, m_w_out, m_norm_xa, m_norm_mem, m_xa_wq, m_xa_wkv, m_xa_wo, m_norm_mlp, m_mlp_w1, m_mlp_w2, m_norm_final, v_norm_mix, v_w_in, v_b_gate, v_conv_w, v_conv_b, v_dt_bias, v_a_log, v_d_skip, v_ssm_norm, v_w_br_ret, v_w_br_ssm, v_w_out, v_norm_xa, v_norm_mem, v_xa_wq, v_xa_wkv, v_xa_wo, v_norm_mlp, v_mlp_w1, v_mlp_w2, v_norm_final)))
    blocks = [d[k] if k == 'conv_w' else d[k].astype(MXU_DTYPE) for k in SHARDED]
    gathered = _all_gather(blocks, "all_gather_weights")
    w = {k: d[k] for k in SMALL}
    for k, g in zip(SHARDED, gathered):
        w[k] = _gathered_to_full(k, g)
    loss, grad_x, grads = _local_step(d['x'][0], d['mem'][0], d['positions'][0], w, d['loss_target'][0])
    by_core = [_full_to_scatter(k, grads[k]) for k in SHARDED]
    from_sibling = _exchange_cores(by_core, "grad_exchange_cores")
    c_idx = lax.axis_index("c").astype(jnp.int32).reshape(1)
    chip_sums = [_add_halves(a, o, c_idx, f"grad_add_cores_{k}") for k, a, o in zip(SHARDED, by_core, from_sibling)]
    by_chip = _exchange_chips(chip_sums, "grad_exchange_chips")
    packed = _pack_small([loss] + [grads[k] for k in SMALL])
    total = _all_reduce_small(packed, "all_reduce_small")
    loss_out = total[0, 0]
    res = {}
    for k, g4 in zip(SHARDED, by_chip):
        res[k] = _adamw(d[k], g4, d['m_' + k], d['v_' + k], f"adamw_{k}")
    small_shapes = [d[k].shape for k in SMALL]
    g_small = total.reshape(-1)[LANES:]
    n_small = sum(int(np.prod(s)) for s in small_shapes)
    pk = lambda pre: _pack_small([d[pre + k] for k in SMALL])
    rows = pk('').shape[0]
    g_pk = jnp.pad(g_small, (0, rows * LANES - g_small.shape[0]))[:rows * LANES].reshape(1, rows, LANES)
    outs = _adamw(pk(''), g_pk, pk('m_'), pk('v_'), "adamw_small")
    unpacked = [_unpack_small(o, small_shapes, 0) for o in outs]
    for i, k in enumerate(SMALL):
        res[k] = [unpacked[j][i] for j in range(4)]
    del n_small
    return (loss_out, grad_x[None], *[res[k][0] for k in WEIGHTS], *[res[k][1] for k in WEIGHTS],
            *[res[k][2] for k in WEIGHTS], *[res[k][3] for k in WEIGHTS])
```

```python
import functools

import numpy as np
import jax
import jax.numpy as jnp
from jax import lax
from jax.experimental import pallas as pl
from jax.experimental.pallas import tpu as pltpu

F32 = jnp.float32
MXU_DTYPE = jnp.bfloat16
VMEM_LIMIT_BYTES = 56 * 1024 * 1024
LANES = 128
N_DEV = 8

D_MODEL = 1024
DEPTH = 4
CHUNK = 64
EPS = 1e-6
RET_HEADS, RET_QK_DIM, RET_V_DIM = 4, 128, 256
RET_QK, RET_V = 512, 1024
ROPE_THETA = 10000.0
SSM_INNER, SSM_HEAD_DIM, SSM_HEADS, SSM_GROUPS, SSM_STATE, SSM_CONV = 2048, 64, 32, 8, 128, 4
SSM_BC = 1024
SSM_CONV_DIM = 4096
IN_DIM = 11296
XA_HEADS, XA_HEAD_DIM = 4, 256
D_FF = 4096
ADAM_LR, ADAM_B1, ADAM_B2, ADAM_EPS, ADAM_WD, ADAM_STEP = 0.001, 0.9, 0.999, 1e-08, 0.01, 10

PROJ_W = 11264
COL_Z, COL_XBC, COL_GATES, COL_V, COL_G, COL_Q, COL_K = 0, 2048, 6144, 8192, 9216, 10240, 10752
DT_PAD = 128
N_LTILE = SSM_INNER // LANES

WEIGHTS = ['norm_mix', 'w_in', 'b_gate', 'conv_w', 'conv_b', 'dt_bias', 'a_log', 'd_skip', 'ssm_norm',
           'w_br_ret', 'w_br_ssm', 'w_out', 'norm_xa', 'norm_mem', 'xa_wq', 'xa_wkv', 'xa_wo', 'norm_mlp',
           'mlp_w1', 'mlp_w2', 'norm_final']
COL_SHARDED = ['w_in', 'conv_w', 'xa_wkv', 'mlp_w1']
ROW_SHARDED = ['w_br_ret', 'w_br_ssm', 'w_out', 'xa_wq', 'xa_wo', 'mlp_w2']
SHARDED = COL_SHARDED + ROW_SHARDED
SMALL = [n for n in WEIGHTS if n not in SHARDED]


def _cparams(sem=None):
    return pltpu.CompilerParams(dimension_semantics=sem, vmem_limit_bytes=VMEM_LIMIT_BYTES)


def _sds(shape, dtype):
    return jax.ShapeDtypeStruct(shape, dtype)


def _full(shape):
    nd = len(shape)
    return pl.BlockSpec(shape, lambda *_: (0,) * nd)


_DIMS = {'nn': (((1,), (0,)), ((), ())), 'nt': (((1,), (1,)), ((), ())), 'tn': (((0,), (0,)), ((), ()))}


def _dot(a, b, mode='nn'):
    return lax.dot_general(a.astype(MXU_DTYPE), b.astype(MXU_DTYPE), _DIMS[mode], preferred_element_type=F32)


@functools.partial(jax.custom_vjp, nondiff_argnums=(2,))
def _mm(a, b, mode):
    return _dot(a, b, mode)


def _mm_fwd(a, b, mode):
    return _dot(a, b, mode), (a, b)


def _mm_bwd(mode, res, g):
    a, b = res
    if mode == 'nn':
        return _dot(g, b, 'nt'), _dot(a, g, 'tn')
    if mode == 'nt':
        return _dot(g, b, 'nn'), _dot(g, a, 'tn')
    return _dot(b, g, 'nt'), _dot(a, g, 'nn')


_mm.defvjp(_mm_fwd, _mm_bwd)


def _split3(x):
    hi = x.astype(jnp.bfloat16)
    r1 = x - hi.astype(F32)
    mid = r1.astype(jnp.bfloat16)
    lo = (r1 - mid.astype(F32)).astype(jnp.bfloat16)
    return hi, mid, lo


def _dot_sel(x, c, left=False):
    dims = _DIMS['nn']
    parts = _split3(x)
    if left:
        outs = [lax.dot_general(c, p, dims, preferred_element_type=F32) for p in parts]
    else:
        outs = [lax.dot_general(p, c, dims, preferred_element_type=F32) for p in parts]
    return (outs[0] + outs[1]) + outs[2]


def _silu(x):
    return x * jax.nn.sigmoid(x)


def _softplus(x):
    pos = x > 0.0
    return jnp.where(pos, x, 0.0) + jnp.log1p(jnp.exp(jnp.where(pos, -x, x)))


def _rms(x):
    return x * lax.rsqrt(jnp.mean(x * x, axis=-1, keepdims=True) + EPS)


def _pick(n, pref):
    t = min(n, pref)
    while n % t:
        t //= 2
    return t


MATMUL_TK_MAX = 4096


def _tiles(mode, m, n, k):
    tm, tn = (512, 1024) if mode == 'nt' else (1024, 512)
    tk = k
    while tk > MATMUL_TK_MAX or k % tk or tk % LANES:
        tk -= LANES
    return _pick(m, tm), _pick(n, tn), tk


def _matmul(a, b, mode, name, *, extras=(), epi=None, out_dtypes=(F32,), out_t=None, tiles=None):
    if mode == 'nn':
        (m, k), (k2, n) = a.shape, b.shape
    elif mode == 'nt':
        (m, k), (n, k2) = a.shape, b.shape
    else:
        (k, m), (k2, n) = a.shape, b.shape
    assert k == k2, (a.shape, b.shape, mode)
    tm, tn, tk = tiles or _tiles(mode, m, n, k)
    nk = k // tk
    n_ex, n_out = len(extras), len(out_dtypes)
    out_t = out_t or (False,) * n_out

    def finish(acc, ex_refs, o_refs):
        outs = epi(acc, *[r[...] for r in ex_refs]) if epi is not None else (acc,)
        for o_ref, o, tr in zip(o_refs, outs, out_t):
            o_ref[...] = (o.T if tr else o).astype(o_ref.dtype)

    def body(*refs):
        a_ref, b_ref = refs[0], refs[1]
        ex_refs = refs[2:2 + n_ex]
        o_refs = refs[2 + n_ex:2 + n_ex + n_out]
        if nk == 1:
            finish(_dot(a_ref[...], b_ref[...], mode), ex_refs, o_refs)
            return
        acc_ref = refs[-1]
        kk = pl.program_id(2)

        @pl.when(kk == 0)
        def _():
            acc_ref[...] = jnp.zeros_like(acc_ref)

        acc_ref[...] += _dot(a_ref[...], b_ref[...], mode)

        @pl.when(kk == nk - 1)
        def _():
            finish(acc_ref[...], ex_refs, o_refs)

    if mode == 'nn':
        a_spec = pl.BlockSpec((tm, tk), lambda i, j, kk: (i, kk))
        b_spec = pl.BlockSpec((tk, tn), lambda i, j, kk: (kk, j))
    elif mode == 'nt':
        a_spec = pl.BlockSpec((tm, tk), lambda i, j, kk: (i, kk))
        b_spec = pl.BlockSpec((tn, tk), lambda i, j, kk: (j, kk))
    else:
        a_spec = pl.BlockSpec((tk, tm), lambda i, j, kk: (kk, i))
        b_spec = pl.BlockSpec((tk, tn), lambda i, j, kk: (kk, j))
    mn_spec = pl.BlockSpec((tm, tn), lambda i, j, kk: (i, j))
    nm_spec = pl.BlockSpec((tn, tm), lambda i, j, kk: (j, i))
    outs = pl.pallas_call(
        body, grid=(m // tm, n // tn, nk),
        in_specs=[a_spec, b_spec] + [mn_spec] * n_ex,
        out_specs=[nm_spec if tr else mn_spec for tr in out_t],
        out_shape=[_sds((n, m) if tr else (m, n), dt) for dt, tr in zip(out_dtypes, out_t)],
        scratch_shapes=[pltpu.VMEM((tm, tn), F32)] if nk > 1 else [],
        compiler_params=_cparams(("parallel", "parallel", "arbitrary")),
        name=name,
    )(a, b, *extras)
    return outs[0] if n_out == 1 else outs


def _epi_add(acc, r):
    return (acc + r,)


def _rmsnorm_fn(x, w):
    return _rms(x) * w


def _rmsnorm(x, w, name):
    s, d = x.shape
    t = _pick(s, 512)

    def body(x_ref, w_ref, o_ref, ot_ref):
        y = _rmsnorm_fn(x_ref[...], w_ref[...])
        o_ref[...] = y.astype(o_ref.dtype)
        ot_ref[...] = y.T.astype(ot_ref.dtype)

    return pl.pallas_call(
        body, grid=(s // t,),
        in_specs=[pl.BlockSpec((t, d), lambda i: (i, 0)), _full((1, d))],
        out_specs=[pl.BlockSpec((t, d), lambda i: (i, 0)), pl.BlockSpec((d, t), lambda i: (0, i))],
        out_shape=[_sds((s, d), MXU_DTYPE), _sds((d, s), MXU_DTYPE)],
        compiler_params=_cparams(("parallel",)), name=name,
    )(x, w.reshape(1, d))


def _rmsnorm_bwd(x, w, du, dres, name):
    s, d = x.shape
    t = _pick(s, 512)
    has_res = dres is not None

    def body(*refs):
        if has_res:
            x_ref, w_ref, du_ref, dres_ref, dx_ref, dxb_ref, dw_ref = refs
        else:
            x_ref, w_ref, du_ref, dx_ref, dxb_ref, dw_ref = refs
        _, vjp = jax.vjp(_rmsnorm_fn, x_ref[...], w_ref[...])
        dx, dw = vjp(du_ref[...])
        dx = dx + dres_ref[...] if has_res else dx
        dx_ref[...] = dx
        dxb_ref[...] = dx.astype(dxb_ref.dtype)

        @pl.when(pl.program_id(0) == 0)
        def _():
            dw_ref[...] = jnp.zeros_like(dw_ref)

        dw_ref[...] += dw

    row = pl.BlockSpec((t, d), lambda i: (i, 0))
    return pl.pallas_call(
        body, grid=(s // t,),
        in_specs=[row, _full((1, d)), row] + ([row] if has_res else []),
        out_specs=[row, row, _full((1, d))],
        out_shape=[_sds((s, d), F32), _sds((s, d), MXU_DTYPE), _sds((1, d), F32)],
        compiler_params=_cparams(("arbitrary",)), name=name,
    )(x, w.reshape(1, d), du, *([dres] if has_res else []))


CONV_CW = 2048
CONV_HALO = 8


def _conv_taps(cat, w, n_rows, off):
    acc = cat[off:off + n_rows, :] * w[3:4, :]
    for sft in (1, 2, 3):
        acc = acc + pltpu.roll(cat, sft, axis=0)[off:off + n_rows, :] * w[3 - sft:4 - sft, :]
    return acc


def _conv_fwd(proj, conv_w, conv_b, name):
    s = proj.shape[0]
    tr = _pick(s, 256)
    hb = tr // CONV_HALO
    col0 = COL_XBC // CONV_CW

    def body(prev_ref, x_ref, w_ref, b_ref, o_ref):
        i = pl.program_id(1)
        prev = jnp.where(i == 0, 0.0, prev_ref[...])
        cat = jnp.concatenate([prev, x_ref[...]], axis=0)
        o_ref[...] = _silu(_conv_taps(cat, w_ref[...], tr, CONV_HALO) + b_ref[...])

    return pl.pallas_call(
        body, grid=(SSM_CONV_DIM // CONV_CW, s // tr),
        in_specs=[pl.BlockSpec((CONV_HALO, CONV_CW), lambda j, i: (jnp.maximum(i * hb - 1, 0), j + col0)),
                  pl.BlockSpec((tr, CONV_CW), lambda j, i: (i, j + col0)),
                  pl.BlockSpec((SSM_CONV, CONV_CW), lambda j, i: (0, j)),
                  pl.BlockSpec((1, CONV_CW), lambda j, i: (0, j))],
        out_specs=pl.BlockSpec((tr, CONV_CW), lambda j, i: (i, j)),
        out_shape=_sds((s, SSM_CONV_DIM), F32),
        compiler_params=_cparams(("parallel", "parallel")), name=name,
    )(proj, proj, conv_w, conv_b.reshape(1, SSM_CONV_DIM))


def _conv_bwd(proj, conv_w, conv_b, dact, name):
    s = proj.shape[0]
    tr = _pick(s, 256)
    hb = tr // CONV_HALO
    nb = s // CONV_HALO
    nt = s // tr
    col0 = COL_XBC // CONV_CW
    h = CONV_HALO

    def body(prev_ref, x_ref, next_ref, w_ref, b_ref, da_ref, dan_ref, dx_ref, dw_ref, db_ref):
        i = pl.program_id(1)
        w = w_ref[...]
        prev = jnp.where(i == 0, 0.0, prev_ref[...])
        cat = jnp.concatenate([prev, x_ref[...], next_ref[...]], axis=0)
        pre = _conv_taps(cat, w, tr + h, h) + b_ref[...]
        dact_n = jnp.where(i == nt - 1, 0.0, dan_ref[...])
        dact_ext = jnp.concatenate([da_ref[...], dact_n], axis=0)
        sg = jax.nn.sigmoid(pre)
        dpre = dact_ext * (sg * (1.0 + pre * (1.0 - sg)))
        dx = dpre[:tr, :] * w[3:4, :]
        for sft in (1, 2, 3):
            dx = dx + pltpu.roll(dpre, tr + h - sft, axis=0)[:tr, :] * w[3 - sft:4 - sft, :]
        dx_ref[...] = dx.astype(dx_ref.dtype)

        @pl.when(i == 0)
        def _():
            dw_ref[...] = jnp.zeros_like(dw_ref)
            db_ref[...] = jnp.zeros_like(db_ref)

        dp = dpre[:tr, :]
        db_ref[...] += jnp.sum(dp, axis=0, keepdims=True)
        rows = [jnp.sum(dp * pltpu.roll(cat, sft, axis=0)[h:h + tr, :], axis=0, keepdims=True) for sft in (3, 2, 1)]
        rows.append(jnp.sum(dp * cat[h:h + tr, :], axis=0, keepdims=True))
        for r, row in enumerate(rows):
            dw_ref[r:r + 1, :] += row

    return pl.pallas_call(
        body, grid=(SSM_CONV_DIM // CONV_CW, nt),
        in_specs=[pl.BlockSpec((h, CONV_CW), lambda j, i: (jnp.maximum(i * hb - 1, 0), j + col0)),
                  pl.BlockSpec((tr, CONV_CW), lambda j, i: (i, j + col0)),
                  pl.BlockSpec((h, CONV_CW), lambda j, i: (jnp.minimum((i + 1) * hb, nb - 1), j + col0)),
                  pl.BlockSpec((SSM_CONV, CONV_CW), lambda j, i: (0, j)),
                  pl.BlockSpec((1, CONV_CW), lambda j, i: (0, j)),
                  pl.BlockSpec((tr, CONV_CW), lambda j, i: (i, j)),
                  pl.BlockSpec((h, CONV_CW), lambda j, i: (jnp.minimum((i + 1) * hb, nb - 1), j))],
        out_specs=[pl.BlockSpec((tr, CONV_CW), lambda j, i: (i, j)),
                   pl.BlockSpec((SSM_CONV, CONV_CW), lambda j, i: (0, j)),
                   pl.BlockSpec((1, CONV_CW), lambda j, i: (0, j))],
        out_shape=[_sds((s, SSM_CONV_DIM), MXU_DTYPE), _sds((SSM_CONV, SSM_CONV_DIM), F32), _sds((1, SSM_CONV_DIM), F32)],
        compiler_params=_cparams(("parallel", "arbitrary")), name=name,
    )(proj, proj, proj, conv_w, conv_b.reshape(1, SSM_CONV_DIM), dact, dact)


def _scan_tables():
    idx = np.arange(CHUNK, dtype=np.float32)
    lg = np.log1p(-(2.0 ** (-5.0 - np.arange(RET_HEADS, dtype=np.float32)))).astype(np.float32)
    rel = np.abs(idx[:, None] - idx[None, :])
    r_intra = np.exp(lg[:, None, None] * rel).astype(np.float32)
    qd = np.exp(lg[None, :] * (idx[:, None] + 1.0)).astype(np.float32)
    kd = np.exp(lg[None, :] * (CHUNK - 1.0 - idx[:, None])).astype(np.float32)
    gam = [float(v) for v in np.exp(lg * CHUNK).astype(np.float32)]
    qd_e = np.repeat(qd, RET_QK_DIM, axis=1)
    kd_e = np.repeat(kd, RET_QK_DIM, axis=1)
    e = np.zeros((DT_PAD, SSM_INNER), np.float32)
    for hh in range(SSM_HEADS):
        e[hh, hh * SSM_HEAD_DIM:(hh + 1) * SSM_HEAD_DIM] = 1.0
    tri = np.tril(np.ones((CHUNK, CHUNK), np.float32))
    eye2 = np.concatenate([np.eye(CHUNK, dtype=np.float32)] * 2, axis=1)
    bdm = np.kron(np.eye(2, dtype=np.float32), np.ones((CHUNK, CHUNK), np.float32))
    last = np.zeros((CHUNK, LANES), np.float32)
    last[CHUNK - 1, :] = 1.0
    f32c = [jnp.asarray(c) for c in (r_intra, qd_e, kd_e, eye2, bdm, last)]
    sel = [jnp.asarray(c, jnp.bfloat16) for c in (e, e.T.copy(), tri, tri.T.copy())]
    return f32c + sel, gam


def _rope(t, cos2, sin2):
    return t * cos2 + pltpu.roll(t, RET_QK_DIM // 2, axis=1) * sin2


def _rope_t(d, cos2, sin2):
    return d * cos2 + pltpu.roll(d * sin2, RET_QK_DIM // 2, axis=1)


def _ret_step(q, k, v, st, r_intra, qd, kd, gamma):
    k = k * (RET_QK_DIM ** -0.5)
    sc = _mm(q, k, 'nt') * r_intra
    y = _mm(sc, v, 'nn') + _mm(q * qd, st, 'nn')
    st_new = st * gamma + _mm(k * kd, v, 'tn')
    return y, st_new


def _ssd_heads(dtraw, dtb, a_c, tri):
    dt = _softplus(dtraw + dtb)
    return dt, _dot_sel(dt * a_c, tri, left=True)


def _ssd_group(dte0, dte1, cum0, cum1, xs0, xs1, bm, cm, ht0, ht1, eye2, bdm, last):
    cbp = _mm(cm, jnp.concatenate([bm, bm], axis=0), 'nt')
    outs = []
    for dte, cum, xs, ht in ((dte0, cum0, xs0, ht0), (dte1, cum1, xs1, ht1)):
        r = jnp.sum(cum * eye2, axis=0, keepdims=True)
        dlt = cum - r
        seg = jnp.exp(jnp.where(dlt > 0.0, -dlt, dlt))
        xdt = xs * dte
        bd = jnp.concatenate([xdt, xdt], axis=0) * bdm
        clast = jnp.sum(cum * last, axis=0, keepdims=True)
        y = _mm(cbp * seg, bd, 'nn') + jnp.exp(cum) * _mm(cm, ht, 'nn')
        ht_new = jnp.exp(clast) * ht + _mm(bm, xdt * jnp.exp(clast - cum), 'tn')
        outs += [y, ht_new]
    return tuple(outs)


def _scan_in_specs(nc, rev):
    ch = (lambda c: nc - 1 - c) if rev else (lambda c: c)
    col = lambda w, blk: pl.BlockSpec((CHUNK, w), lambda c: (ch(c), blk))
    return [col(RET_QK, COL_Q // RET_QK), col(RET_QK, COL_K // RET_QK), col(RET_V, COL_V // RET_V),
            col(SSM_INNER, 0), col(SSM_BC, 2), col(SSM_BC, 3),
            col(DT_PAD, 0), col(LANES, 0), col(LANES, 0)]


def _const_specs(consts):
    return [_full(c.shape) for c in consts]


def _tile(t):
    return slice(t * LANES, (t + 1) * LANES)


def _scan_fwd(proj, xbc, dtraw, cos2, sin2, a_c, dtb, name):
    s = proj.shape[0]
    nc = s // CHUNK
    consts, gam = _scan_tables()

    def body(q_ref, k_ref, v_ref, xs_ref, bm_ref, cm_ref, dt_ref, cos_ref, sin_ref, ac_ref, dtb_ref,
             ri_ref, qd_ref, kd_ref, eye_ref, bdm_ref, last_ref, e_ref, et_ref, tri_ref, trit_ref,
             yr_ref, ys_ref, sh_ref, hh_ref, st_sc, ht_sc):
        @pl.when(pl.program_id(0) == 0)
        def _():
            st_sc[...] = jnp.zeros_like(st_sc)
            ht_sc[...] = jnp.zeros_like(ht_sc)

        sh_ref[0] = st_sc[...]
        hh_ref[0] = ht_sc[...]
        cos2, sin2 = cos_ref[...], sin_ref[...]
        st_new = []
        for h in range(RET_HEADS):
            ql = slice(h * RET_QK_DIM, (h + 1) * RET_QK_DIM)
            vl = slice(h * RET_V_DIM, (h + 1) * RET_V_DIM)
            y, st_h = _ret_step(_rope(q_ref[:, ql], cos2, sin2), _rope(k_ref[:, ql], cos2, sin2), v_ref[:, vl],
                                st_sc[ql, :], ri_ref[h], qd_ref[:, ql], kd_ref[:, ql], gam[h])
            yr_ref[:, vl] = y
            st_new.append(st_h)
        dt, cum_c = _ssd_heads(dt_ref[...], dtb_ref[...], ac_ref[...], tri_ref[...])
        both = jnp.concatenate([dt, cum_c], axis=0)
        eye2, bdm, last = eye_ref[...], bdm_ref[...], last_ref[...]
        ht_new = []
        for g in range(SSM_GROUPS):
            t0, t1 = 2 * g, 2 * g + 1
            e0, e1 = _dot_sel(both, e_ref[:, _tile(t0)]), _dot_sel(both, e_ref[:, _tile(t1)])
            y0, h0, y1, h1 = _ssd_group(e0[:CHUNK], e1[:CHUNK], e0[CHUNK:], e1[CHUNK:],
                                        xs_ref[:, _tile(t0)], xs_ref[:, _tile(t1)], bm_ref[:, _tile(g)],
                                        cm_ref[:, _tile(g)], ht_sc[:, _tile(t0)], ht_sc[:, _tile(t1)], eye2, bdm, last)
            ys_ref[:, _tile(t0)] = y0
            ys_ref[:, _tile(t1)] = y1
            ht_new += [h0, h1]
        for h in range(RET_HEADS):
            st_sc[h * RET_QK_DIM:(h + 1) * RET_QK_DIM, :] = st_new[h]
        for t in range(N_LTILE):
            ht_sc[:, _tile(t)] = ht_new[t]

    return pl.pallas_call(
        body, grid=(nc,),
        in_specs=_scan_in_specs(nc, False) + [_full((1, DT_PAD)), _full((1, DT_PAD))] + _const_specs(consts),
        out_specs=[pl.BlockSpec((CHUNK, RET_V), lambda c: (c, 0)),
                   pl.BlockSpec((CHUNK, SSM_INNER), lambda c: (c, 0)),
                   pl.BlockSpec((1, RET_QK, RET_V_DIM), lambda c: (c, 0, 0)),
                   pl.BlockSpec((1, SSM_STATE, SSM_INNER), lambda c: (c, 0, 0))],
        out_shape=[_sds((s, RET_V), F32), _sds((s, SSM_INNER), F32),
                   _sds((nc, RET_QK, RET_V_DIM), F32), _sds((nc, SSM_STATE, SSM_INNER), F32)],
        scratch_shapes=[pltpu.VMEM((RET_QK, RET_V_DIM), F32), pltpu.VMEM((SSM_STATE, SSM_INNER), F32)],
        compiler_params=_cparams(("arbitrary",)), name=name,
    )(proj, proj, proj, xbc, xbc, xbc, dtraw, cos2, sin2, a_c, dtb, *consts)


def _scan_bwd(proj, xbc, dtraw, cos2, sin2, a_c, dtb, s_hist, h_hist, dyr, dys, dxs_skip, name):
    s = proj.shape[0]
    nc = s // CHUNK
    consts, gam = _scan_tables()
    rv = lambda c: nc - 1 - c

    def body(q_ref, k_ref, v_ref, xs_ref, bm_ref, cm_ref, dt_ref, cos_ref, sin_ref, ac_ref, dtb_ref,
             ri_ref, qd_ref, kd_ref, eye_ref, bdm_ref, last_ref, e_ref, et_ref, tri_ref, trit_ref,
             sh_ref, hh_ref, dyr_ref, dys_ref, dsk_ref,
             dq_ref, dk_ref, dv_ref, dxbc_ref, ddt_ref, dac_ref, ddtb_ref, dst_sc, dht_sc):
        @pl.when(pl.program_id(0) == 0)
        def _():
            dst_sc[...] = jnp.zeros_like(dst_sc)
            dht_sc[...] = jnp.zeros_like(dht_sc)
            dac_ref[...] = jnp.zeros_like(dac_ref)
            ddtb_ref[...] = jnp.zeros_like(ddtb_ref)

        cos2, sin2 = cos_ref[...], sin_ref[...]
        dst_new = []
        for h in range(RET_HEADS):
            ql = slice(h * RET_QK_DIM, (h + 1) * RET_QK_DIM)
            vl = slice(h * RET_V_DIM, (h + 1) * RET_V_DIM)
            step = functools.partial(_ret_step, r_intra=ri_ref[h], qd=qd_ref[:, ql], kd=kd_ref[:, ql], gamma=gam[h])
            _, vjp = jax.vjp(step, _rope(q_ref[:, ql], cos2, sin2), _rope(k_ref[:, ql], cos2, sin2), v_ref[:, vl],
                             sh_ref[0, ql, :])
            dq, dk, dv, dst = vjp((dyr_ref[:, vl], dst_sc[ql, :]))
            dq_ref[:, ql] = _rope_t(dq, cos2, sin2).astype(dq_ref.dtype)
            dk_ref[:, ql] = _rope_t(dk, cos2, sin2).astype(dk_ref.dtype)
            dv_ref[:, vl] = dv.astype(dv_ref.dtype)
            dst_new.append(dst)
        dtraw_v, dtb_v, a_c, tri = dt_ref[...], dtb_ref[...], ac_ref[...], tri_ref[...]
        dt, cum_c = _ssd_heads(dtraw_v, dtb_v, a_c, tri)
        both = jnp.concatenate([dt, cum_c], axis=0)
        eye2, bdm, last = eye_ref[...], bdm_ref[...], last_ref[...]
        group = functools.partial(_ssd_group, eye2=eye2, bdm=bdm, last=last)
        d_both = jnp.zeros((2 * CHUNK, LANES), F32)
        dht_new = []
        for g in range(SSM_GROUPS):
            t0, t1 = 2 * g, 2 * g + 1
            e0, e1 = _dot_sel(both, e_ref[:, _tile(t0)]), _dot_sel(both, e_ref[:, _tile(t1)])
            _, vjp = jax.vjp(group, e0[:CHUNK], e1[:CHUNK], e0[CHUNK:], e1[CHUNK:],
                             xs_ref[:, _tile(t0)], xs_ref[:, _tile(t1)], bm_ref[:, _tile(g)], cm_ref[:, _tile(g)],
                             hh_ref[0, :, _tile(t0)], hh_ref[0, :, _tile(t1)])
            (d_dte0, d_dte1, d_cum0, d_cum1, d_xs0, d_xs1, d_bm, d_cm, d_ht0, d_ht1) = vjp(
                (dys_ref[:, _tile(t0)], dht_sc[:, _tile(t0)], dys_ref[:, _tile(t1)], dht_sc[:, _tile(t1)]))
            d_both = d_both + _dot_sel(jnp.concatenate([d_dte0, d_cum0], axis=0), et_ref[_tile(t0), :])
            d_both = d_both + _dot_sel(jnp.concatenate([d_dte1, d_cum1], axis=0), et_ref[_tile(t1), :])
            dxbc_ref[:, _tile(t0)] = d_xs0 + dsk_ref[:, _tile(t0)]
            dxbc_ref[:, _tile(t1)] = d_xs1 + dsk_ref[:, _tile(t1)]
            dxbc_ref[:, _tile(N_LTILE + g)] = d_bm
            dxbc_ref[:, _tile(N_LTILE + SSM_GROUPS + g)] = d_cm
            dht_new += [d_ht0, d_ht1]
        d_da = _dot_sel(d_both[CHUNK:], trit_ref[...], left=True)
        d_dt = d_both[:CHUNK] + d_da * a_c
        d_pre = d_dt * jax.nn.sigmoid(dtraw_v + dtb_v)
        ddt_ref[...] = d_pre
        ddtb_ref[...] += jnp.sum(d_pre, axis=0, keepdims=True)
        dac_ref[...] += jnp.sum(d_da * dt, axis=0, keepdims=True)
        for h in range(RET_HEADS):
            dst_sc[h * RET_QK_DIM:(h + 1) * RET_QK_DIM, :] = dst_new[h]
        for t in range(N_LTILE):
            dht_sc[:, _tile(t)] = dht_new[t]

    return pl.pallas_call(
        body, grid=(nc,),
        in_specs=(_scan_in_specs(nc, True) + [_full((1, DT_PAD)), _full((1, DT_PAD))] + _const_specs(consts)
                  + [pl.BlockSpec((1, RET_QK, RET_V_DIM), lambda c: (rv(c), 0, 0)),
                     pl.BlockSpec((1, SSM_STATE, SSM_INNER), lambda c: (rv(c), 0, 0)),
                     pl.BlockSpec((CHUNK, RET_V), lambda c: (rv(c), 0)),
                     pl.BlockSpec((CHUNK, SSM_INNER), lambda c: (rv(c), 0)),
                     pl.BlockSpec((CHUNK, SSM_INNER), lambda c: (rv(c), 0))]),
        out_specs=[pl.BlockSpec((CHUNK, RET_QK), lambda c: (rv(c), 0)),
                   pl.BlockSpec((CHUNK, RET_QK), lambda c: (rv(c), 0)),
                   pl.BlockSpec((CHUNK, RET_V), lambda c: (rv(c), 0)),
                   pl.BlockSpec((CHUNK, SSM_CONV_DIM), lambda c: (rv(c), 0)),
                   pl.BlockSpec((CHUNK, DT_PAD), lambda c: (rv(c), 0)),
                   _full((1, DT_PAD)), _full((1, DT_PAD))],
        out_shape=[_sds((s, RET_QK), MXU_DTYPE), _sds((s, RET_QK), MXU_DTYPE), _sds((s, RET_V), MXU_DTYPE),
                   _sds((s, SSM_CONV_DIM), F32), _sds((s, DT_PAD), F32),
                   _sds((1, DT_PAD), F32), _sds((1, DT_PAD), F32)],
        scratch_shapes=[pltpu.VMEM((RET_QK, RET_V_DIM), F32), pltpu.VMEM((SSM_STATE, SSM_INNER), F32)],
        compiler_params=_cparams(("arbitrary",)), name=name,
    )(proj, proj, proj, xbc, xbc, xbc, dtraw, cos2, sin2, a_c, dtb, *consts, s_hist, h_hist, dyr, dys, dxs_skip)


POST_W = 256


def _post_ret(y, g):
    return _rms(y) * _silu(g)


def _post_ssm(y, xs, z, dsk, nw):
    return _rms((y + xs * dsk) * _silu(z)) * nw


def _post_specs(t):
    return [pl.BlockSpec((t, RET_V), lambda i: (i, 0)),
            pl.BlockSpec((t, RET_V), lambda i: (i, COL_G // RET_V)),
            pl.BlockSpec((t, SSM_INNER), lambda i: (i, 0)),
            pl.BlockSpec((t, SSM_INNER), lambda i: (i, 0)),
            pl.BlockSpec((t, SSM_INNER), lambda i: (i, COL_Z // SSM_INNER)),
            _full((1, SSM_INNER)), _full((1, SSM_INNER))]


def _post_fwd(y_ret, proj, y_ssm, xbc, dsk_e, ssm_norm, name):
    s = y_ret.shape[0]
    t = _pick(s, 256)

    def body(yr_ref, g_ref, ys_ref, xs_ref, z_ref, dsk_ref, nw_ref, or_ref, os_ref, ort_ref, ost_ref):
        for h in range(RET_V // POST_W):
            sl = slice(h * POST_W, (h + 1) * POST_W)
            o = _post_ret(yr_ref[:, sl], g_ref[:, sl])
            or_ref[:, sl] = o.astype(or_ref.dtype)
            ort_ref[sl, :] = o.T.astype(ort_ref.dtype)
        for g in range(SSM_INNER // POST_W):
            sl = slice(g * POST_W, (g + 1) * POST_W)
            o = _post_ssm(ys_ref[:, sl], xs_ref[:, sl], z_ref[:, sl], dsk_ref[:, sl], nw_ref[:, sl])
            os_ref[:, sl] = o.astype(os_ref.dtype)
            ost_ref[sl, :] = o.T.astype(ost_ref.dtype)

    return pl.pallas_call(
        body, grid=(s // t,), in_specs=_post_specs(t),
        out_specs=[pl.BlockSpec((t, RET_V), lambda i: (i, 0)), pl.BlockSpec((t, SSM_INNER), lambda i: (i, 0)),
                   pl.BlockSpec((RET_V, t), lambda i: (0, i)), pl.BlockSpec((SSM_INNER, t), lambda i: (0, i))],
        out_shape=[_sds((s, RET_V), MXU_DTYPE), _sds((s, SSM_INNER), MXU_DTYPE),
                   _sds((RET_V, s), MXU_DTYPE), _sds((SSM_INNER, s), MXU_DTYPE)],
        compiler_params=_cparams(("parallel",)), name=name,
    )(y_ret, proj, y_ssm, xbc, proj, dsk_e, ssm_norm.reshape(1, SSM_INNER))


def _post_bwd(y_ret, proj, y_ssm, xbc, dsk_e, ssm_norm, d_or, d_os, name):
    s = y_ret.shape[0]
    t = _pick(s, 256)

    def body(yr_ref, g_ref, ys_ref, xs_ref, z_ref, dsk_ref, nw_ref, dor_ref, dos_ref,
             dyr_ref, dg_ref, dys_ref, dxs_ref, dz_ref, ddsk_ref, dnw_ref):
        @pl.when(pl.program_id(0) == 0)
        def _():
            ddsk_ref[...] = jnp.zeros_like(ddsk_ref)
            dnw_ref[...] = jnp.zeros_like(dnw_ref)

        for h in range(RET_V // POST_W):
            sl = slice(h * POST_W, (h + 1) * POST_W)
            _, vjp = jax.vjp(_post_ret, yr_ref[:, sl], g_ref[:, sl])
            dyr, dg = vjp(dor_ref[:, sl])
            dyr_ref[:, sl] = dyr
            dg_ref[:, sl] = dg.astype(dg_ref.dtype)
        for g in range(SSM_INNER // POST_W):
            sl = slice(g * POST_W, (g + 1) * POST_W)
            _, vjp = jax.vjp(_post_ssm, ys_ref[:, sl], xs_ref[:, sl], z_ref[:, sl], dsk_ref[:, sl], nw_ref[:, sl])
            dy, dxs, dz, ddsk, dnw = vjp(dos_ref[:, sl])
            dys_ref[:, sl] = dy
            dxs_ref[:, sl] = dxs
            dz_ref[:, sl] = dz.astype(dz_ref.dtype)
            ddsk_ref[:, sl] += ddsk
            dnw_ref[:, sl] += dnw

    rowv = pl.BlockSpec((t, RET_V), lambda i: (i, 0))
    rows = pl.BlockSpec((t, SSM_INNER), lambda i: (i, 0))
    return pl.pallas_call(
        body, grid=(s // t,), in_specs=_post_specs(t) + [rowv, rows],
        out_specs=[rowv, rowv, rows, rows, rows, _full((1, SSM_INNER)), _full((1, SSM_INNER))],
        out_shape=[_sds((s, RET_V), F32), _sds((s, RET_V), MXU_DTYPE), _sds((s, SSM_INNER), F32),
                   _sds((s, SSM_INNER), F32), _sds((s, SSM_INNER), MXU_DTYPE),
                   _sds((1, SSM_INNER), F32), _sds((1, SSM_INNER), F32)],
        compiler_params=_cparams(("arbitrary",)), name=name,
    )(y_ret, proj, y_ssm, xbc, proj, dsk_e, ssm_norm.reshape(1, SSM_INNER), d_or, d_os)


def _merge_fn(gr, gs, br, bs, yr, ys):
    return jax.nn.sigmoid(gr + br) * yr + jax.nn.sigmoid(gs + bs) * ys


def _merge_specs(t):
    row = pl.BlockSpec((t, D_MODEL), lambda i: (i, 0))
    return [pl.BlockSpec((t, D_MODEL), lambda i: (i, COL_GATES // D_MODEL)),
            pl.BlockSpec((t, D_MODEL), lambda i: (i, COL_GATES // D_MODEL + 1)),
            pl.BlockSpec((1, D_MODEL), lambda i: (0, 0)), pl.BlockSpec((1, D_MODEL), lambda i: (0, 1)), row, row]


def _merge_fwd(proj, b_gate, br_ret, br_ssm, name):
    s = proj.shape[0]
    t = _pick(s, 512)

    def body(gr_ref, gs_ref, br_ref, bs_ref, yr_ref, ys_ref, o_ref, ot_ref):
        o = _merge_fn(gr_ref[...], gs_ref[...], br_ref[...], bs_ref[...], yr_ref[...], ys_ref[...])
        o_ref[...] = o.astype(o_ref.dtype)
        ot_ref[...] = o.T.astype(ot_ref.dtype)

    bg = b_gate.reshape(1, 2 * D_MODEL)
    return pl.pallas_call(
        body, grid=(s // t,), in_specs=_merge_specs(t),
        out_specs=[pl.BlockSpec((t, D_MODEL), lambda i: (i, 0)), pl.BlockSpec((D_MODEL, t), lambda i: (0, i))],
        out_shape=[_sds((s, D_MODEL), MXU_DTYPE), _sds((D_MODEL, s), MXU_DTYPE)],
        compiler_params=_cparams(("parallel",)), name=name,
    )(proj, proj, bg, bg, br_ret, br_ssm)


def _merge_bwd(proj, b_gate, br_ret, br_ssm, dm, name):
    s = proj.shape[0]
    t = _pick(s, 512)

    def body(gr_ref, gs_ref, br_ref, bs_ref, yr_ref, ys_ref, dm_ref, dgt_ref, db_ref, dyr_ref, dys_ref):
        @pl.when(pl.program_id(0) == 0)
        def _():
            db_ref[...] = jnp.zeros_like(db_ref)

        _, vjp = jax.vjp(_merge_fn, gr_ref[...], gs_ref[...], br_ref[...], bs_ref[...], yr_ref[...], ys_ref[...])
        dgr, dgs, dbr, dbs, dyr, dys = vjp(dm_ref[...])
        dgt_ref[:, :D_MODEL] = dgr.astype(dgt_ref.dtype)
        dgt_ref[:, D_MODEL:] = dgs.astype(dgt_ref.dtype)
        db_ref[:, :D_MODEL] += dbr
        db_ref[:, D_MODEL:] += dbs
        dyr_ref[...] = dyr.astype(dyr_ref.dtype)
        dys_ref[...] = dys.astype(dys_ref.dtype)

    bg = b_gate.reshape(1, 2 * D_MODEL)
    row = pl.BlockSpec((t, D_MODEL), lambda i: (i, 0))
    return pl.pallas_call(
        body, grid=(s // t,), in_specs=_merge_specs(t) + [row],
        out_specs=[pl.BlockSpec((t, 2 * D_MODEL), lambda i: (i, 0)), _full((1, 2 * D_MODEL)), row, row],
        out_shape=[_sds((s, 2 * D_MODEL), MXU_DTYPE), _sds((1, 2 * D_MODEL), F32),
                   _sds((s, D_MODEL), MXU_DTYPE), _sds((s, D_MODEL), MXU_DTYPE)],
        compiler_params=_cparams(("arbitrary",)), name=name,
    )(proj, proj, bg, bg, br_ret, br_ssm, dm)


def _attn_head(q, k, v):
    sc = _mm(q, k, 'nt') * (XA_HEAD_DIM ** -0.5)
    e = jnp.exp(sc - lax.stop_gradient(jnp.max(sc, axis=-1, keepdims=True)))
    p = e / jnp.sum(e, axis=-1, keepdims=True)
    return _mm(p, v, 'nn')


def _attn_fwd(q, kv, name):
    s = q.shape[0]
    m = kv.shape[0]
    t = _pick(s, 512)

    def body(q_ref, kv_ref, o_ref, ot_ref):
        for h in range(XA_HEADS):
            sl = slice(h * XA_HEAD_DIM, (h + 1) * XA_HEAD_DIM)
            vl = slice(D_MODEL + h * XA_HEAD_DIM, D_MODEL + (h + 1) * XA_HEAD_DIM)
            o = _attn_head(q_ref[:, sl], kv_ref[:, sl], kv_ref[:, vl])
            o_ref[:, sl] = o.astype(o_ref.dtype)
            ot_ref[sl, :] = o.T.astype(ot_ref.dtype)

    return pl.pallas_call(
        body, grid=(s // t,),
        in_specs=[pl.BlockSpec((t, D_MODEL), lambda i: (i, 0)), _full((m, 2 * D_MODEL))],
        out_specs=[pl.BlockSpec((t, D_MODEL), lambda i: (i, 0)), pl.BlockSpec((D_MODEL, t), lambda i: (0, i))],
        out_shape=[_sds((s, D_MODEL), MXU_DTYPE), _sds((D_MODEL, s), MXU_DTYPE)],
        compiler_params=_cparams(("parallel",)), name=name,
    )(q, kv)


def _attn_bwd(q, kv, d_o, name):
    s = q.shape[0]
    m = kv.shape[0]
    t = _pick(s, 512)

    def body(q_ref, kv_ref, do_ref, dq_ref, dkv_ref):
        @pl.when(pl.program_id(0) == 0)
        def _():
            dkv_ref[...] = jnp.zeros_like(dkv_ref)

        for h in range(XA_HEADS):
            sl = slice(h * XA_HEAD_DIM, (h + 1) * XA_HEAD_DIM)
            vl = slice(D_MODEL + h * XA_HEAD_DIM, D_MODEL + (h + 1) * XA_HEAD_DIM)
            _, vjp = jax.vjp(_attn_head, q_ref[:, sl], kv_ref[:, sl], kv_ref[:, vl])
            dq, dk, dv = vjp(do_ref[:, sl])
            dq_ref[:, sl] = dq.astype(dq_ref.dtype)
            dkv_ref[:, sl] += dk
            dkv_ref[:, vl] += dv

    row = pl.BlockSpec((t, D_MODEL), lambda i: (i, 0))
    return pl.pallas_call(
        body, grid=(s // t,), in_specs=[row, _full((m, 2 * D_MODEL)), row],
        out_specs=[row, _full((m, 2 * D_MODEL))],
        out_shape=[_sds((s, D_MODEL), MXU_DTYPE), _sds((m, 2 * D_MODEL), F32)],
        compiler_params=_cparams(("arbitrary",)), name=name,
    )(q, kv, d_o)


def _loss_head(x, w, target, name):
    s, d = x.shape
    t = _pick(s, 512)

    def body(x_ref, w_ref, t_ref, loss_ref, dx_ref, dxb_ref, dw_ref):
        @pl.when(pl.program_id(0) == 0)
        def _():
            loss_ref[...] = jnp.zeros_like(loss_ref)
            dw_ref[...] = jnp.zeros_like(dw_ref)

        y, vjp = jax.vjp(_rmsnorm_fn, x_ref[...], w_ref[...])
        err = y - t_ref[...]
        loss_ref[...] += 0.5 * jnp.sum(jnp.sum(err * err, axis=-1, keepdims=True), axis=0, keepdims=True) / d
        dx, dw = vjp(err * (1.0 / d))
        dx_ref[...] = dx
        dxb_ref[...] = dx.astype(dxb_ref.dtype)
        dw_ref[...] += dw

    row = pl.BlockSpec((t, d), lambda i: (i, 0))
    return pl.pallas_call(
        body, grid=(s // t,), in_specs=[row, _full((1, d)), row],
        out_specs=[_full((1, LANES)), row, row, _full((1, d))],
        out_shape=[_sds((1, LANES), F32), _sds((s, d), F32), _sds((s, d), MXU_DTYPE), _sds((1, d), F32)],
        compiler_params=_cparams(("arbitrary",)), name=name,
    )(x, w.reshape(1, d), target)


def _epi_sqrelu(acc):
    r = jnp.maximum(acc, 0.0)
    return acc, r * r, r * r


def _epi_sqrelu_bwd(acc, a):
    return (acc * (2.0 * jnp.maximum(a, 0.0)),)


def _rope_tables(positions):
    inv_freq = ROPE_THETA ** (-jnp.arange(0, RET_QK_DIM, 2, dtype=F32) / RET_QK_DIM)
    ang = positions.astype(F32)[:, None] * inv_freq
    cos, sin = jnp.cos(ang), jnp.sin(ang)
    return jnp.concatenate([cos, cos], axis=1), jnp.concatenate([-sin, sin], axis=1)


def _split_w_in(w_in):
    q, k, v, g = w_in[:, 0:512], w_in[:, 512:1024], w_in[:, 1024:2048], w_in[:, 2048:3072]
    z, xbc, dt, gates = w_in[:, 3072:5120], w_in[:, 5120:9216], w_in[:, 9216:9248], w_in[:, 9248:11296]
    main = jnp.concatenate([z, xbc, gates, v, g, q, k], axis=1)
    return main, jnp.pad(dt, ((0, 0), (0, DT_PAD - SSM_HEADS)))


def _merge_w_in(d_main, d_dt):
    z, xbc, gates = d_main[:, 0:2048], d_main[:, 2048:6144], d_main[:, 6144:8192]
    v, g, q, k = d_main[:, 8192:9216], d_main[:, 9216:10240], d_main[:, 10240:10752], d_main[:, 10752:11264]
    return jnp.concatenate([q, k, v, g, z, xbc, d_dt[:, :SSM_HEADS], gates], axis=1)


def _lanes_of_heads(v):
    return jnp.repeat(v, SSM_HEAD_DIM).reshape(1, SSM_INNER)


def _heads_of_lanes(v):
    return v.reshape(SSM_HEADS, SSM_HEAD_DIM).sum(axis=1)


def _layer_fwd(x, mem, cos2, sin2, p, l):
    n = lambda s: f"{s}_l{l}"
    sv = {'x0': x}
    u, u_t = _rmsnorm(x, p['norm_mix'], n("norm_mix"))
    proj = _matmul(u, p['w_in_main'], 'nn', n("in_proj"))
    dtraw = _matmul(u, p['w_in_dt'], 'nn', n("in_proj_dt"))
    xbc = _conv_fwd(proj, p['conv_w'], p['conv_b'], n("conv"))
    a_c = jnp.pad(-jnp.exp(p['a_log']), (0, DT_PAD - SSM_HEADS)).reshape(1, DT_PAD)
    dtb = jnp.pad(p['dt_bias'], (0, DT_PAD - SSM_HEADS)).reshape(1, DT_PAD)
    y_ret, y_ssm, s_hist, h_hist = _scan_fwd(proj, xbc, dtraw, cos2, sin2, a_c, dtb, n("scan"))
    dsk_e = _lanes_of_heads(p['d_skip'])
    o_ret, o_ssm, o_ret_t, o_ssm_t = _post_fwd(y_ret, proj, y_ssm, xbc, dsk_e, p['ssm_norm'], n("post"))
    br_ret = _matmul(o_ret, p['w_br_ret'], 'nn', n("br_ret"))
    br_ssm = _matmul(o_ssm, p['w_br_ssm'], 'nn', n("br_ssm"))
    merged, merged_t = _merge_fwd(proj, p['b_gate'], br_ret, br_ssm, n("merge"))
    x1 = _matmul(merged, p['w_out'], 'nn', n("w_out"), extras=(x,), epi=_epi_add)
    sv.update(u_t=u_t, proj=proj, dtraw=dtraw, xbc=xbc, a_c=a_c, dtb=dtb, y_ret=y_ret, y_ssm=y_ssm, s_hist=s_hist,
              h_hist=h_hist, dsk_e=dsk_e, o_ret_t=o_ret_t, o_ssm_t=o_ssm_t, br_ret=br_ret, br_ssm=br_ssm,
              merged_t=merged_t, x1=x1)
    hq, hq_t = _rmsnorm(x1, p['norm_xa'], n("norm_xa"))
    memn, _ = _rmsnorm(mem, p['norm_mem'], n("norm_mem"))
    q = _matmul(hq, p['xa_wq'], 'nn', n("xa_q"))
    kv = _matmul(memn, p['xa_wkv'], 'nn', n("xa_kv"))
    o, o_t = _attn_fwd(q, kv, n("attn"))
    x2 = _matmul(o, p['xa_wo'], 'nn', n("xa_o"), extras=(x1,), epi=_epi_add)
    sv.update(hq_t=hq_t, memn=memn, q=q, kv=kv, o_t=o_t, x2=x2)
    hm, hm_t = _rmsnorm(x2, p['norm_mlp'], n("norm_mlp"))
    a, act, act_t = _matmul(hm, p['mlp_w1'], 'nn', n("mlp_1"), epi=_epi_sqrelu, out_dtypes=(F32, MXU_DTYPE, MXU_DTYPE),
                            out_t=(False, False, True))
    x3 = _matmul(act, p['mlp_w2'], 'nn', n("mlp_2"), extras=(x2,), epi=_epi_add)
    sv.update(hm_t=hm_t, a=a, act_t=act_t)
    return x3, sv


def _layer_bwd(dx, dxb, mem, cos2, sin2, p, sv, l):
    n = lambda s: f"{s}_bwd_l{l}"
    gd = (MXU_DTYPE,)
    g = {}
    g['mlp_w2'] = _matmul(sv['act_t'], dxb, 'nn', n("mlp_2_dw"), out_dtypes=gd)
    da = _matmul(dxb, p['mlp_w2'], 'nt', n("mlp_2_dx"), extras=(sv['a'],), epi=_epi_sqrelu_bwd, out_dtypes=(MXU_DTYPE,))
    g['mlp_w1'] = _matmul(sv['hm_t'], da, 'nn', n("mlp_1_dw"), out_dtypes=gd)
    dhm = _matmul(da, p['mlp_w1'], 'nt', n("mlp_1_dx"))
    dx2, dx2b, g['norm_mlp'] = _rmsnorm_bwd(sv['x2'], p['norm_mlp'], dhm, dx, n("norm_mlp"))
    g['xa_wo'] = _matmul(sv['o_t'], dx2b, 'nn', n("xa_o_dw"), out_dtypes=gd)
    d_o = _matmul(dx2b, p['xa_wo'], 'nt', n("xa_o_dx"))
    dq, dkv = _attn_bwd(sv['q'], sv['kv'], d_o, n("attn"))
    g['xa_wq'] = _matmul(sv['hq_t'], dq, 'nn', n("xa_q_dw"), out_dtypes=gd)
    dhq = _matmul(dq, p['xa_wq'], 'nt', n("xa_q_dx"))
    g['xa_wkv'] = _matmul(sv['memn'], dkv, 'tn', n("xa_kv_dw"), out_dtypes=gd)
    dmemn = _matmul(dkv, p['xa_wkv'], 'nt', n("xa_kv_dx"))
    _, _, g['norm_mem'] = _rmsnorm_bwd(mem, p['norm_mem'], dmemn, None, n("norm_mem"))
    dx1, dx1b, g['norm_xa'] = _rmsnorm_bwd(sv['x1'], p['norm_xa'], dhq, dx2, n("norm_xa"))
    g['w_out'] = _matmul(sv['merged_t'], dx1b, 'nn', n("w_out_dw"), out_dtypes=gd)
    dmerged = _matmul(dx1b, p['w_out'], 'nt', n("w_out_dx"))
    dgates, g['b_gate'], dbr_ret, dbr_ssm = _merge_bwd(sv['proj'], p['b_gate'], sv['br_ret'], sv['br_ssm'], dmerged, n("merge"))
    g['w_br_ret'] = _matmul(sv['o_ret_t'], dbr_ret, 'nn', n("br_ret_dw"), out_dtypes=gd)
    g['w_br_ssm'] = _matmul(sv['o_ssm_t'], dbr_ssm, 'nn', n("br_ssm_dw"), out_dtypes=gd)
    d_or = _matmul(dbr_ret, p['w_br_ret'], 'nt', n("br_ret_dx"))
    d_os = _matmul(dbr_ssm, p['w_br_ssm'], 'nt', n("br_ssm_dx"))
    dyr, dg, dys, dxs_skip, dz, ddsk_e, g['ssm_norm'] = _post_bwd(
        sv['y_ret'], sv['proj'], sv['y_ssm'], sv['xbc'], sv['dsk_e'], p['ssm_norm'], d_or, d_os, n("post"))
    g['d_skip'] = _heads_of_lanes(ddsk_e)
    dq_r, dk_r, dv_r, dxbc_act, ddtraw, dac, ddtb = _scan_bwd(
        sv['proj'], sv['xbc'], sv['dtraw'], cos2, sin2, sv['a_c'], sv['dtb'], sv['s_hist'], sv['h_hist'],
        dyr, dys, dxs_skip, n("scan"))
    g['a_log'] = dac[0, :SSM_HEADS] * (-jnp.exp(p['a_log']))
    g['dt_bias'] = ddtb[0, :SSM_HEADS]
    dxbc_raw, g['conv_w'], g['conv_b'] = _conv_bwd(sv['proj'], p['conv_w'], p['conv_b'], dxbc_act, n("conv"))
    dproj = jnp.concatenate([dz, dxbc_raw, dgates, dv_r, dg, dq_r, dk_r], axis=1)
    d_main = _matmul(sv['u_t'], dproj, 'nn', n("in_proj_dw"), out_dtypes=gd)
    d_dt = _matmul(sv['u_t'], ddtraw, 'nn', n("in_proj_dt_dw"), out_dtypes=gd)
    g['w_in'] = _merge_w_in(d_main, d_dt)
    du_dt = _matmul(ddtraw, p['w_in_dt'], 'nt', n("in_proj_dt_dx"))
    du = _matmul(dproj, p['w_in_main'], 'nt', n("in_proj_dx"), extras=(du_dt,), epi=_epi_add)
    dx0, dx0b, g['norm_mix'] = _rmsnorm_bwd(sv['x0'], p['norm_mix'], du, dx1, n("norm_mix"))
    return dx0, dx0b, g


def _local_step(x, mem, positions, w, loss_target):
    cos2, sin2 = _rope_tables(positions)
    saved, layers = [], []
    for l in range(DEPTH):
        p = {k: w[k][l] for k in WEIGHTS if k != 'norm_final'}
        p['w_in_main'], p['w_in_dt'] = _split_w_in(p['w_in'])
        x, sv = _layer_fwd(x, mem, cos2, sin2, p, l)
        saved.append(sv)
        layers.append(p)
    loss, dx, dxb, dnf = _loss_head(x, w['norm_final'], loss_target, "loss_head")
    grads = [None] * DEPTH
    for l in reversed(range(DEPTH)):
        dx, dxb, grads[l] = _layer_bwd(dx, dxb, mem, cos2, sin2, layers[l], saved[l], l)
    out = {}
    for k in WEIGHTS:
        if k == 'norm_final':
            out[k] = dnf.reshape(D_MODEL)
        else:
            out[k] = [grads[l][k].reshape(w[k].shape[1:]) for l in range(DEPTH)]
    return loss, dx, out


MESH = pl.DeviceIdType.MESH
ANY_SPEC = pl.BlockSpec(memory_space=pl.ANY)


def _mesh_pos():
    return lax.axis_index("x"), lax.axis_index("y"), lax.axis_index("c")


def _other_chips(x, y):
    return [(1 - x, y), (x, 1 - y), (1 - x, 1 - y)]


def _all_gather(arrs, name):
    na = len(arrs)

    def body(*refs):
        x_refs, o_refs = refs[:na], refs[na:2 * na]
        send_sems, recv_sems, local_sems = refs[2 * na:]
        x, y, c = _mesh_pos()
        me, sib = (x, y, c), (x, y, 1 - c)
        chips = _other_chips(x, y)

        def copy(a, k, block, to, src=None):
            dst = o_refs[a].at[4 * block[0] + 2 * block[1] + block[2]]
            return pltpu.make_async_remote_copy(src_ref=dst if src is None else src, dst_ref=dst,
                                                send_sem=send_sems.at[a, k], recv_sem=recv_sems.at[a, k],
                                                device_id=to, device_id_type=MESH)

        mine = [pltpu.make_async_copy(x_refs[a], o_refs[a].at[4 * x + 2 * y + c], local_sems.at[a]) for a in range(na)]
        for cp in mine:
            cp.start()
        first = []
        for a in range(na):
            first.append(copy(a, 0, me, sib, src=x_refs[a]))
            first += [copy(a, 1 + j, me, (*chip, c), src=x_refs[a]) for j, chip in enumerate(chips)]
        for cp in first:
            cp.start()
        passed = []
        for a in range(na):
            for j, chip in enumerate(chips):
                copy(a, 1 + j, (*chip, c), me).wait_recv()
                cp = copy(a, 4 + j, (*chip, c), sib)
                cp.start()
                passed.append(cp)
        for a in range(na):
            copy(a, 0, sib, me).wait_recv()
            for j, chip in enumerate(chips):
                copy(a, 4 + j, (*chip, 1 - c), me).wait_recv()
        for cp in first + passed:
            cp.wait_send()
        for cp in mine:
            cp.wait()

    return pl.pallas_call(
        body, in_specs=[ANY_SPEC] * na, out_specs=[ANY_SPEC] * na,
        out_shape=[_sds((N_DEV,) + a.shape, a.dtype) for a in arrs],
        scratch_shapes=[pltpu.SemaphoreType.DMA((na, 7)), pltpu.SemaphoreType.DMA((na, 7)), pltpu.SemaphoreType.DMA((na,))],
        name=name,
    )(*arrs)


def _exchange_cores(arrs, name):
    na = len(arrs)

    def body(*refs):
        a_refs, o_refs = refs[:na], refs[na:2 * na]
        send_sems, recv_sems = refs[2 * na:]
        x, y, c = _mesh_pos()
        cps = [pltpu.make_async_remote_copy(src_ref=a_refs[a].at[1 - c], dst_ref=o_refs[a], send_sem=send_sems.at[a],
                                            recv_sem=recv_sems.at[a], device_id=(x, y, 1 - c), device_id_type=MESH)
               for a in range(na)]
        for cp in cps:
            cp.start()
        for cp in cps:
            cp.wait()

    return pl.pallas_call(
        body, in_specs=[ANY_SPEC] * na, out_specs=[ANY_SPEC] * na,
        out_shape=[_sds(a.shape[1:], a.dtype) for a in arrs],
        scratch_shapes=[pltpu.SemaphoreType.DMA((na,)), pltpu.SemaphoreType.DMA((na,))],
        name=name,
    )(*arrs)


def _exchange_chips(arrs, name):
    na = len(arrs)

    def body(*refs):
        a_refs, o_refs = refs[:na], refs[na:2 * na]
        send_sems, recv_sems, local_sems = refs[2 * na:]
        x, y, c = _mesh_pos()
        my_chip = 2 * x + y
        chips = _other_chips(x, y)
        mine = [pltpu.make_async_copy(a_refs[a].at[my_chip], o_refs[a].at[my_chip], local_sems.at[a]) for a in range(na)]
        for cp in mine:
            cp.start()
        cps = [pltpu.make_async_remote_copy(src_ref=a_refs[a].at[2 * px + py], dst_ref=o_refs[a].at[my_chip],
                                            send_sem=send_sems.at[a, j], recv_sem=recv_sems.at[a, j],
                                            device_id=(px, py, c), device_id_type=MESH)
               for a in range(na) for j, (px, py) in enumerate(chips)]
        for cp in cps:
            cp.start()
        for a in range(na):
            for j, (px, py) in enumerate(chips):
                pltpu.make_async_remote_copy(src_ref=a_refs[a].at[2 * px + py], dst_ref=o_refs[a].at[2 * px + py],
                                             send_sem=send_sems.at[a, j], recv_sem=recv_sems.at[a, j],
                                             device_id=(px, py, c), device_id_type=MESH).wait_recv()
        for cp in cps:
            cp.wait_send()
        for cp in mine:
            cp.wait()

    return pl.pallas_call(
        body, in_specs=[ANY_SPEC] * na, out_specs=[ANY_SPEC] * na,
        out_shape=[_sds(a.shape, a.dtype) for a in arrs],
        scratch_shapes=[pltpu.SemaphoreType.DMA((na, 3)), pltpu.SemaphoreType.DMA((na, 3)), pltpu.SemaphoreType.DMA((na,))],
        name=name,
    )(*arrs)


def _as_rows(a, lead):
    return a.reshape(a.shape[:lead] + (-1, a.shape[-1]))


def _add_halves(a, other, c_idx, name):
    a3, o2 = _as_rows(a, 1), _as_rows(other, 0)
    rows, cols = o2.shape
    tr = _pick(rows, 256)

    def body(c_ref, a_ref, o_ref, out_ref):
        out_ref[...] = (a_ref[0].astype(F32) + o_ref[...].astype(F32)).astype(out_ref.dtype)

    out = pl.pallas_call(
        body,
        grid_spec=pltpu.PrefetchScalarGridSpec(
            num_scalar_prefetch=1, grid=(rows // tr,),
            in_specs=[pl.BlockSpec((1, tr, cols), lambda i, c_ref: (c_ref[0], i, 0)),
                      pl.BlockSpec((tr, cols), lambda i, c_ref: (i, 0))],
            out_specs=pl.BlockSpec((tr, cols), lambda i, c_ref: (i, 0))),
        out_shape=_sds((rows, cols), a.dtype), compiler_params=_cparams(("parallel",)), name=name,
    )(c_idx, a3, o2)
    return out.reshape(other.shape)


def _all_reduce_small(v, name):
    r = v.shape[0]

    def body(v_ref, o_ref, slots, send_sems, recv_sems):
        x, y, c = _mesh_pos()
        me = 4 * x + 2 * y + c
        slots[me] = v_ref[...]
        cps = []
        for k in range(1, N_DEV):
            px = 1 - x if k & 4 else x
            py = 1 - y if k & 2 else y
            pc = 1 - c if k & 1 else c
            cps.append(pltpu.make_async_remote_copy(src_ref=v_ref, dst_ref=slots.at[me], send_sem=send_sems.at[k - 1],
                                                    recv_sem=recv_sems.at[k - 1], device_id=(px, py, pc), device_id_type=MESH))
        for cp in cps:
            cp.start()
        for cp in cps:
            cp.wait()
        acc = slots[0]
        for d in range(1, N_DEV):
            acc = acc + slots[d]
        o_ref[...] = acc

    vm = pl.BlockSpec(memory_space=pltpu.VMEM)
    return pl.pallas_call(
        body, in_specs=[vm], out_specs=vm, out_shape=_sds((r, LANES), F32),
        scratch_shapes=[pltpu.VMEM((N_DEV, r, LANES), F32), pltpu.SemaphoreType.DMA((N_DEV - 1,)),
                        pltpu.SemaphoreType.DMA((N_DEV - 1,))],
        compiler_params=pltpu.CompilerParams(vmem_limit_bytes=VMEM_LIMIT_BYTES), name=name,
    )(v)


def _adamw(w, g_slots, m, v, name):
    w2, m2, v2, g3 = _as_rows(w, 0), _as_rows(m, 0), _as_rows(v, 0), _as_rows(g_slots, 1)
    ns = g3.shape[0]
    rows, cols = w2.shape
    tr = _pick(rows, 256 if cols <= 1024 else 128)

    def body(w_ref, g_ref, m_ref, v_ref, go_ref, d_ref, mo_ref, vo_ref):
        g = g_ref[0].astype(F32)
        for i in range(1, ns):
            g = g + g_ref[i].astype(F32)
        m_new = ADAM_B1 * m_ref[...] + (1.0 - ADAM_B1) * g
        v_new = ADAM_B2 * v_ref[...] + (1.0 - ADAM_B2) * (g * g)
        m_hat = m_new / (1.0 - ADAM_B1 ** ADAM_STEP)
        v_hat = v_new / (1.0 - ADAM_B2 ** ADAM_STEP)
        go_ref[...] = g
        d_ref[...] = -ADAM_LR * (m_hat / (jnp.sqrt(v_hat) + ADAM_EPS) + ADAM_WD * w_ref[...])
        mo_ref[...] = m_new
        vo_ref[...] = v_new

    row = pl.BlockSpec((tr, cols), lambda i: (i, 0))
    outs = pl.pallas_call(
        body, grid=(rows // tr,),
        in_specs=[row, pl.BlockSpec((ns, tr, cols), lambda i: (0, i, 0)), row, row],
        out_specs=[row] * 4, out_shape=[_sds((rows, cols), F32)] * 4,
        compiler_params=_cparams(("parallel",)), name=name,
    )(w2, g3, m2, v2)
    return [o.reshape(w.shape) for o in outs]


_ARG_NAMES = (['x', 'mem', 'positions'] + WEIGHTS + ['loss_target'] + ['m_' + n for n in WEIGHTS]
              + ['v_' + n for n in WEIGHTS])


def _gathered_to_full(name, g):
    dev, depth, r, c = g.shape
    if name in COL_SHARDED:
        return jnp.transpose(g, (1, 2, 0, 3)).reshape(depth, r, dev * c)
    return jnp.transpose(g, (1, 0, 2, 3)).reshape(depth, dev * r, c)


def _full_to_scatter(name, g):
    depth = g.shape[0]
    if name in COL_SHARDED:
        r, c = g.shape[1], g.shape[2] // N_DEV
        return jnp.transpose(g.reshape(depth, r, 4, 2, c), (3, 2, 0, 1, 4))
    r, c = g.shape[1] // N_DEV, g.shape[2]
    return jnp.transpose(g.reshape(depth, 4, 2, r, c), (2, 1, 0, 3, 4))


PACK_TILE = 8 * LANES


def _pack_rows(parts):
    blocks = []
    for part in parts:
        flat = part.reshape(-1)
        pad = (-flat.shape[0]) % PACK_TILE
        blocks.append((jnp.pad(flat, (0, pad)) if pad else flat).reshape(-1, LANES))
    return jnp.concatenate(blocks, axis=0)


def _unpack_rows(packed, shapes):
    out, off = [], 0
    for shp in shapes:
        n = int(np.prod(shp))
        rows = -(-n // PACK_TILE) * 8
        out.append(packed[off:off + rows].reshape(-1)[:n].reshape(shp))
        off += rows
    return out


def kernel(x, mem, positions, norm_mix, w_in, b_gate, conv_w, conv_b, dt_bias, a_log, d_skip, ssm_norm, w_br_ret, w_br_ssm, w_out, norm_xa, norm_mem, xa_wq, xa_wkv, xa_wo, norm_mlp, mlp_w1, mlp_w2, norm_final, loss_target, m_norm_mix, m_w_in, m_b_gate, m_conv_w, m_conv_b, m_dt_bias, m_a_log, m_d_skip, m_ssm_norm, m_w_br_ret, m_w_br_ssm, m_w_out, m_norm_xa, m_norm_mem, m_xa_wq, m_xa_wkv, m_xa_wo, m_norm_mlp, m_mlp_w1, m_mlp_w2, m_norm_final, v_norm_mix, v_w_in, v_b_gate, v_conv_w, v_conv_b, v_dt_bias, v_a_log, v_d_skip, v_ssm_norm, v_w_br_ret, v_w_br_ssm, v_w_out, v_norm_xa, v_norm_mem, v_xa_wq, v_xa_wkv, v_xa_wo, v_norm_mlp, v_mlp_w1, v_mlp_w2, v_norm_final):
    d = dict(zip(_ARG_NAMES, (x, mem, positions, norm_mix, w_in, b_gate, conv_w, conv_b, dt_bias, a_log, d_skip, ssm_norm, w_br_ret, w_br_ssm, w_out, norm_xa, norm_mem, xa_wq, xa_wkv, xa_wo, norm_mlp, mlp_w1, mlp_w2, norm_final, loss_target, m_norm_mix, m_w_in, m_b_gate, m_conv_w, m_conv_b, m_dt_bias, m_a_log, m_d_skip, m_ssm_norm, m_w_br_ret, m_w_br_ssm, m_w_out, m_norm_xa, m_norm_mem, m_xa_wq, m_xa_wkv, m_xa_wo, m_norm_mlp, m_mlp_w1, m_mlp_w2, m_norm_final, v_norm_mix, v_w_in, v_b_gate, v_conv_w, v_conv_b, v_dt_bias, v_a_log, v_d_skip, v_ssm_norm, v_w_br_ret, v_w_br_ssm, v_w_out, v_norm_xa, v_norm_mem, v_xa_wq, v_xa_wkv, v_xa_wo, v_norm_mlp, v_mlp_w1, v_mlp_w2, v_norm_final)))
    blocks = [d[k] if k == 'conv_w' else d[k].astype(MXU_DTYPE) for k in SHARDED]
    gathered = _all_gather(blocks, "all_gather_weights")
    w = {k: d[k] for k in SMALL}
    for k, g in zip(SHARDED, gathered):
        w[k] = _gathered_to_full(k, g)
    loss, grad_x, grads = _local_step(d['x'][0], d['mem'][0], d['positions'][0], w, d['loss_target'][0])
    by_core = [_full_to_scatter(k, jnp.stack(grads[k])) for k in SHARDED]
    from_sibling = _exchange_cores(by_core, "grad_exchange_cores")
    c_idx = lax.axis_index("c").astype(jnp.int32).reshape(1)
    chip_sums = [_add_halves(a, o, c_idx, f"grad_add_cores_{k}") for k, a, o in zip(SHARDED, by_core, from_sibling)]
    by_chip = _exchange_chips(chip_sums, "grad_exchange_chips")
    small_g = [grads[k] if k == 'norm_final' else jnp.stack(grads[k]) for k in SMALL]
    total = _all_reduce_small(_pack_rows([loss] + small_g), "all_reduce_small")
    loss_out = total[0, 0]
    res = {}
    for k, g4 in zip(SHARDED, by_chip):
        res[k] = _adamw(d[k], g4, d['m_' + k], d['v_' + k], f"adamw_{k}")
    small_shapes = [d[k].shape for k in SMALL]
    pk = lambda pre: _pack_rows([d[pre + k] for k in SMALL])
    outs = _adamw(pk(''), total[8:][None], pk('m_'), pk('v_'), "adamw_small")
    unpacked = [_unpack_rows(o, small_shapes) for o in outs]
    for i, k in enumerate(SMALL):
        res[k] = [unpacked[j][i] for j in range(4)]
    return (loss_out, grad_x[None], *[res[k][0] for k in WEIGHTS], *[res[k][1] for k in WEIGHTS],
            *[res[k][2] for k in WEIGHTS], *[res[k][3] for k in WEIGHTS])
```

```python
import functools

import numpy as np
import jax
import jax.numpy as jnp
from jax import lax
from jax.experimental import pallas as pl
from jax.experimental.pallas import tpu as pltpu

F32 = jnp.float32
MXU_DTYPE = jnp.bfloat16
VMEM_LIMIT_BYTES = 56 * 1024 * 1024
LANES = 128
N_DEV = 8

D_MODEL = 1024
DEPTH = 4
CHUNK = 64
EPS = 1e-6
RET_HEADS, RET_QK_DIM, RET_V_DIM = 4, 128, 256
RET_QK, RET_V = 512, 1024
ROPE_THETA = 10000.0
SSM_INNER, SSM_HEAD_DIM, SSM_HEADS, SSM_GROUPS, SSM_STATE, SSM_CONV = 2048, 64, 32, 8, 128, 4
SSM_BC = 1024
SSM_CONV_DIM = 4096
IN_DIM = 11296
XA_HEADS, XA_HEAD_DIM = 4, 256
D_FF = 4096
ADAM_LR, ADAM_B1, ADAM_B2, ADAM_EPS, ADAM_WD, ADAM_STEP = 0.001, 0.9, 0.999, 1e-08, 0.01, 10

PROJ_W = 11264
COL_Z, COL_XBC, COL_GATES, COL_V, COL_G, COL_Q, COL_K = 0, 2048, 6144, 8192, 9216, 10240, 10752
DT_PAD = 128
N_LTILE = SSM_INNER // LANES

WEIGHTS = ['norm_mix', 'w_in', 'b_gate', 'conv_w', 'conv_b', 'dt_bias', 'a_log', 'd_skip', 'ssm_norm',
           'w_br_ret', 'w_br_ssm', 'w_out', 'norm_xa', 'norm_mem', 'xa_wq', 'xa_wkv', 'xa_wo', 'norm_mlp',
           'mlp_w1', 'mlp_w2', 'norm_final']
COL_SHARDED = ['w_in', 'conv_w', 'xa_wkv', 'mlp_w1']
ROW_SHARDED = ['w_br_ret', 'w_br_ssm', 'w_out', 'xa_wq', 'xa_wo', 'mlp_w2']
SHARDED = COL_SHARDED + ROW_SHARDED
SMALL = [n for n in WEIGHTS if n not in SHARDED]


def _cparams(sem=None):
    return pltpu.CompilerParams(dimension_semantics=sem, vmem_limit_bytes=VMEM_LIMIT_BYTES)


def _sds(shape, dtype):
    return jax.ShapeDtypeStruct(shape, dtype)


def _full(shape):
    nd = len(shape)
    return pl.BlockSpec(shape, lambda *_: (0,) * nd)


_DIMS = {'nn': (((1,), (0,)), ((), ())), 'nt': (((1,), (1,)), ((), ())), 'tn': (((0,), (0,)), ((), ()))}


def _dot(a, b, mode='nn'):
    return lax.dot_general(a.astype(MXU_DTYPE), b.astype(MXU_DTYPE), _DIMS[mode], preferred_element_type=F32)


@functools.partial(jax.custom_vjp, nondiff_argnums=(2,))
def _mm(a, b, mode):
    return _dot(a, b, mode)


def _mm_fwd(a, b, mode):
    return _dot(a, b, mode), (a, b)


def _mm_bwd(mode, res, g):
    a, b = res
    if mode == 'nn':
        return _dot(g, b, 'nt'), _dot(a, g, 'tn')
    if mode == 'nt':
        return _dot(g, b, 'nn'), _dot(g, a, 'tn')
    return _dot(b, g, 'nt'), _dot(a, g, 'nn')


_mm.defvjp(_mm_fwd, _mm_bwd)


def _split3(x):
    hi = x.astype(jnp.bfloat16)
    r1 = x - hi.astype(F32)
    mid = r1.astype(jnp.bfloat16)
    lo = (r1 - mid.astype(F32)).astype(jnp.bfloat16)
    return hi, mid, lo


def _dot_sel(x, c, left=False):
    dims = _DIMS['nn']
    parts = _split3(x)
    if left:
        outs = [lax.dot_general(c, p, dims, preferred_element_type=F32) for p in parts]
    else:
        outs = [lax.dot_general(p, c, dims, preferred_element_type=F32) for p in parts]
    return (outs[0] + outs[1]) + outs[2]


def _silu(x):
    return x * jax.nn.sigmoid(x)


def _softplus(x):
    pos = x > 0.0
    return jnp.where(pos, x, 0.0) + jnp.log1p(jnp.exp(jnp.where(pos, -x, x)))


def _rms(x):
    return x * lax.rsqrt(jnp.mean(x * x, axis=-1, keepdims=True) + EPS)


def _pick(n, pref):
    t = min(n, pref)
    while n % t:
        t //= 2
    return t


MATMUL_TK_MAX = 4096


def _tiles(mode, m, n, k):
    tm, tn = (512, 1024) if mode == 'nt' else (1024, 512)
    tk = k
    while tk > MATMUL_TK_MAX or k % tk or tk % LANES:
        tk -= LANES
    return _pick(m, tm), _pick(n, tn), tk


def _matmul(a, b, mode, name, *, extras=(), epi=None, out_dtypes=(F32,), out_t=None, tiles=None):
    if mode == 'nn':
        (m, k), (k2, n) = a.shape, b.shape
    elif mode == 'nt':
        (m, k), (n, k2) = a.shape, b.shape
    else:
        (k, m), (k2, n) = a.shape, b.shape
    assert k == k2, (a.shape, b.shape, mode)
    tm, tn, tk = tiles or _tiles(mode, m, n, k)
    nk = k // tk
    n_ex, n_out = len(extras), len(out_dtypes)
    out_t = out_t or (False,) * n_out

    def finish(acc, ex_refs, o_refs):
        outs = epi(acc, *[r[...] for r in ex_refs]) if epi is not None else (acc,)
        for o_ref, o, tr in zip(o_refs, outs, out_t):
            o_ref[...] = (o.T if tr else o).astype(o_ref.dtype)

    def body(*refs):
        a_ref, b_ref = refs[0], refs[1]
        ex_refs = refs[2:2 + n_ex]
        o_refs = refs[2 + n_ex:2 + n_ex + n_out]
        if nk == 1:
            finish(_dot(a_ref[...], b_ref[...], mode), ex_refs, o_refs)
            return
        acc_ref = refs[-1]
        kk = pl.program_id(2)

        @pl.when(kk == 0)
        def _():
            acc_ref[...] = jnp.zeros_like(acc_ref)

        acc_ref[...] += _dot(a_ref[...], b_ref[...], mode)

        @pl.when(kk == nk - 1)
        def _():
            finish(acc_ref[...], ex_refs, o_refs)

    if mode == 'nn':
        a_spec = pl.BlockSpec((tm, tk), lambda i, j, kk: (i, kk))
        b_spec = pl.BlockSpec((tk, tn), lambda i, j, kk: (kk, j))
    elif mode == 'nt':
        a_spec = pl.BlockSpec((tm, tk), lambda i, j, kk: (i, kk))
        b_spec = pl.BlockSpec((tn, tk), lambda i, j, kk: (j, kk))
    else:
        a_spec = pl.BlockSpec((tk, tm), lambda i, j, kk: (kk, i))
        b_spec = pl.BlockSpec((tk, tn), lambda i, j, kk: (kk, j))
    mn_spec = pl.BlockSpec((tm, tn), lambda i, j, kk: (i, j))
    nm_spec = pl.BlockSpec((tn, tm), lambda i, j, kk: (j, i))
    outs = pl.pallas_call(
        body, grid=(m // tm, n // tn, nk),
        in_specs=[a_spec, b_spec] + [mn_spec] * n_ex,
        out_specs=[nm_spec if tr else mn_spec for tr in out_t],
        out_shape=[_sds((n, m) if tr else (m, n), dt) for dt, tr in zip(out_dtypes, out_t)],
        scratch_shapes=[pltpu.VMEM((tm, tn), F32)] if nk > 1 else [],
        compiler_params=_cparams(("parallel", "parallel", "arbitrary")),
        name=name,
    )(a, b, *extras)
    return outs[0] if n_out == 1 else outs


def _epi_add(acc, r):
    return (acc + r,)


PIECE_TK = 1024


def _matmul_nt_pieces(pieces, b, name, *, extras=(), epi=None, out_dtypes=(F32,)):
    m, n = pieces[0].shape[0], b.shape[0]
    tm, tn, tk = _pick(m, 512), _pick(n, 1024), PIECE_TK
    steps = [pc.shape[1] // tk for pc in pieces]
    starts = [sum(steps[:i]) for i in range(len(pieces))]
    nk = sum(steps)
    assert b.shape[1] == nk * tk and all(pc.shape[1] % tk == 0 for pc in pieces)
    n_pc, n_ex, n_out = len(pieces), len(extras), len(out_dtypes)

    def body(*refs):
        pc_refs, b_ref = refs[:n_pc], refs[n_pc]
        ex_refs = refs[n_pc + 1:n_pc + 1 + n_ex]
        o_refs = refs[n_pc + 1 + n_ex:n_pc + 1 + n_ex + n_out]
        acc_ref = refs[-1]
        kk = pl.program_id(2)

        @pl.when(kk == 0)
        def _():
            acc_ref[...] = jnp.zeros_like(acc_ref)

        for pc_ref, st, ns in zip(pc_refs, starts, steps):
            @pl.when((kk >= st) & (kk < st + ns))
            def _(pc_ref=pc_ref):
                acc_ref[...] += _dot(pc_ref[...], b_ref[...], 'nt')

        @pl.when(kk == nk - 1)
        def _():
            acc = acc_ref[...]
            outs = epi(acc, *[r[...] for r in ex_refs]) if epi is not None else (acc,)
            for o_ref, o in zip(o_refs, outs):
                o_ref[...] = o.astype(o_ref.dtype)

    pc_specs = [pl.BlockSpec((tm, tk), lambda i, j, kk, st=st, ns=ns: (i, jnp.clip(kk - st, 0, ns - 1)))
                for st, ns in zip(starts, steps)]
    mn_spec = pl.BlockSpec((tm, tn), lambda i, j, kk: (i, j))
    outs = pl.pallas_call(
        body, grid=(m // tm, n // tn, nk),
        in_specs=pc_specs + [pl.BlockSpec((tn, tk), lambda i, j, kk: (j, kk))] + [mn_spec] * n_ex,
        out_specs=[mn_spec] * n_out, out_shape=[_sds((m, n), dt) for dt in out_dtypes],
        scratch_shapes=[pltpu.VMEM((tm, tn), F32)],
        compiler_params=_cparams(("parallel", "parallel", "arbitrary")), name=name,
    )(*pieces, b, *extras)
    return outs[0] if n_out == 1 else outs


def _rmsnorm_fn(x, w):
    return _rms(x) * w


def _rmsnorm(x, w, name):
    s, d = x.shape
    t = _pick(s, 512)

    def body(x_ref, w_ref, o_ref, ot_ref):
        y = _rmsnorm_fn(x_ref[...], w_ref[...])
        o_ref[...] = y.astype(o_ref.dtype)
        ot_ref[...] = y.T.astype(ot_ref.dtype)

    return pl.pallas_call(
        body, grid=(s // t,),
        in_specs=[pl.BlockSpec((t, d), lambda i: (i, 0)), _full((1, d))],
        out_specs=[pl.BlockSpec((t, d), lambda i: (i, 0)), pl.BlockSpec((d, t), lambda i: (0, i))],
        out_shape=[_sds((s, d), MXU_DTYPE), _sds((d, s), MXU_DTYPE)],
        compiler_params=_cparams(("parallel",)), name=name,
    )(x, w.reshape(1, d))


def _rmsnorm_bwd(x, w, du, dres, name):
    s, d = x.shape
    t = _pick(s, 512)
    has_res = dres is not None

    def body(*refs):
        if has_res:
            x_ref, w_ref, du_ref, dres_ref, dx_ref, dxb_ref, dw_ref = refs
        else:
            x_ref, w_ref, du_ref, dx_ref, dxb_ref, dw_ref = refs
        _, vjp = jax.vjp(_rmsnorm_fn, x_ref[...], w_ref[...])
        dx, dw = vjp(du_ref[...])
        dx = dx + dres_ref[...] if has_res else dx
        dx_ref[...] = dx
        dxb_ref[...] = dx.astype(dxb_ref.dtype)

        @pl.when(pl.program_id(0) == 0)
        def _():
            dw_ref[...] = jnp.zeros_like(dw_ref)

        dw_ref[...] += dw

    row = pl.BlockSpec((t, d), lambda i: (i, 0))
    return pl.pallas_call(
        body, grid=(s // t,),
        in_specs=[row, _full((1, d)), row] + ([row] if has_res else []),
        out_specs=[row, row, _full((1, d))],
        out_shape=[_sds((s, d), F32), _sds((s, d), MXU_DTYPE), _sds((1, d), F32)],
        compiler_params=_cparams(("arbitrary",)), name=name,
    )(x, w.reshape(1, d), du, *([dres] if has_res else []))


CONV_CW = 2048
CONV_HALO = 8


def _shifted(cat):
    return [cat] + [pltpu.roll(cat, sft, axis=0) for sft in (1, 2, 3)]


def _conv_taps(shifted, w, n_rows, off):
    acc = shifted[0][off:off + n_rows, :] * w[3:4, :]
    for sft in (1, 2, 3):
        acc = acc + shifted[sft][off:off + n_rows, :] * w[3 - sft:4 - sft, :]
    return acc


def _conv_fwd(proj, conv_w, conv_b, name):
    s = proj.shape[0]
    tr = _pick(s, 256)
    hb = tr // CONV_HALO
    col0 = COL_XBC // CONV_CW

    def body(prev_ref, x_ref, w_ref, b_ref, o_ref):
        i = pl.program_id(1)
        prev = jnp.where(i == 0, 0.0, prev_ref[...])
        cat = jnp.concatenate([prev, x_ref[...]], axis=0)
        o_ref[...] = _silu(_conv_taps(_shifted(cat), w_ref[...], tr, CONV_HALO) + b_ref[...])

    return pl.pallas_call(
        body, grid=(SSM_CONV_DIM // CONV_CW, s // tr),
        in_specs=[pl.BlockSpec((CONV_HALO, CONV_CW), lambda j, i: (jnp.maximum(i * hb - 1, 0), j + col0)),
                  pl.BlockSpec((tr, CONV_CW), lambda j, i: (i, j + col0)),
                  pl.BlockSpec((SSM_CONV, CONV_CW), lambda j, i: (0, j)),
                  pl.BlockSpec((1, CONV_CW), lambda j, i: (0, j))],
        out_specs=pl.BlockSpec((tr, CONV_CW), lambda j, i: (i, j)),
        out_shape=_sds((s, SSM_CONV_DIM), F32),
        compiler_params=_cparams(("parallel", "parallel")), name=name,
    )(proj, proj, conv_w, conv_b.reshape(1, SSM_CONV_DIM))


def _conv_bwd(proj, conv_w, conv_b, dact, name):
    s = proj.shape[0]
    tr = _pick(s, 256)
    hb = tr // CONV_HALO
    nb = s // CONV_HALO
    nt = s // tr
    col0 = COL_XBC // CONV_CW
    h = CONV_HALO

    def body(prev_ref, x_ref, next_ref, w_ref, b_ref, da_ref, dan_ref, dx_ref, dw_ref, db_ref):
        i = pl.program_id(1)
        w = w_ref[...]
        prev = jnp.where(i == 0, 0.0, prev_ref[...])
        cat = jnp.concatenate([prev, x_ref[...], next_ref[...]], axis=0)
        shifted = _shifted(cat)
        pre = _conv_taps(shifted, w, tr + h, h) + b_ref[...]
        dact_n = jnp.where(i == nt - 1, 0.0, dan_ref[...])
        dact_ext = jnp.concatenate([da_ref[...], dact_n], axis=0)
        sg = jax.nn.sigmoid(pre)
        dpre = dact_ext * (sg * (1.0 + pre * (1.0 - sg)))
        dx = dpre[:tr, :] * w[3:4, :]
        for sft in (1, 2, 3):
            dx = dx + pltpu.roll(dpre, tr + h - sft, axis=0)[:tr, :] * w[3 - sft:4 - sft, :]
        dx_ref[...] = dx.astype(dx_ref.dtype)

        @pl.when(i == 0)
        def _():
            dw_ref[...] = jnp.zeros_like(dw_ref)
            db_ref[...] = jnp.zeros_like(db_ref)

        dp = dpre[:tr, :]
        db_ref[...] += jnp.sum(dp, axis=0, keepdims=True)
        for r, sft in enumerate((3, 2, 1, 0)):
            dw_ref[r:r + 1, :] += jnp.sum(dp * shifted[sft][h:h + tr, :], axis=0, keepdims=True)

    return pl.pallas_call(
        body, grid=(SSM_CONV_DIM // CONV_CW, nt),
        in_specs=[pl.BlockSpec((h, CONV_CW), lambda j, i: (jnp.maximum(i * hb - 1, 0), j + col0)),
                  pl.BlockSpec((tr, CONV_CW), lambda j, i: (i, j + col0)),
                  pl.BlockSpec((h, CONV_CW), lambda j, i: (jnp.minimum((i + 1) * hb, nb - 1), j + col0)),
                  pl.BlockSpec((SSM_CONV, CONV_CW), lambda j, i: (0, j)),
                  pl.BlockSpec((1, CONV_CW), lambda j, i: (0, j)),
                  pl.BlockSpec((tr, CONV_CW), lambda j, i: (i, j)),
                  pl.BlockSpec((h, CONV_CW), lambda j, i: (jnp.minimum((i + 1) * hb, nb - 1), j))],
        out_specs=[pl.BlockSpec((tr, CONV_CW), lambda j, i: (i, j)),
                   pl.BlockSpec((SSM_CONV, CONV_CW), lambda j, i: (0, j)),
                   pl.BlockSpec((1, CONV_CW), lambda j, i: (0, j))],
        out_shape=[_sds((s, SSM_CONV_DIM), MXU_DTYPE), _sds((SSM_CONV, SSM_CONV_DIM), F32), _sds((1, SSM_CONV_DIM), F32)],
        compiler_params=_cparams(("parallel", "arbitrary")), name=name,
    )(proj, proj, proj, conv_w, conv_b.reshape(1, SSM_CONV_DIM), dact, dact)


def _scan_tables():
    idx = np.arange(CHUNK, dtype=np.float32)
    lg = np.log1p(-(2.0 ** (-5.0 - np.arange(RET_HEADS, dtype=np.float32)))).astype(np.float32)
    rel = np.abs(idx[:, None] - idx[None, :])
    r_intra = np.exp(lg[:, None, None] * rel).astype(np.float32)
    qd = np.exp(lg[None, :] * (idx[:, None] + 1.0)).astype(np.float32)
    kd = np.exp(lg[None, :] * (CHUNK - 1.0 - idx[:, None])).astype(np.float32)
    gam = [float(v) for v in np.exp(lg * CHUNK).astype(np.float32)]
    qd_e = np.repeat(qd, RET_QK_DIM, axis=1)
    kd_e = np.repeat(kd, RET_QK_DIM, axis=1)
    e = np.zeros((DT_PAD, SSM_INNER), np.float32)
    for hh in range(SSM_HEADS):
        e[hh, hh * SSM_HEAD_DIM:(hh + 1) * SSM_HEAD_DIM] = 1.0
    tri = np.tril(np.ones((CHUNK, CHUNK), np.float32))
    eye2 = np.concatenate([np.eye(CHUNK, dtype=np.float32)] * 2, axis=1)
    bdm = np.kron(np.eye(2, dtype=np.float32), np.ones((CHUNK, CHUNK), np.float32))
    last = np.zeros((CHUNK, LANES), np.float32)
    last[CHUNK - 1, :] = 1.0
    f32c = [jnp.asarray(c) for c in (r_intra, qd_e, kd_e, eye2, bdm, last)]
    sel = [jnp.asarray(c, jnp.bfloat16) for c in (e, e.T.copy(), tri, tri.T.copy())]
    return f32c + sel, gam


def _rope(t, cos2, sin2):
    return t * cos2 + pltpu.roll(t, RET_QK_DIM // 2, axis=1) * sin2


def _rope_t(d, cos2, sin2):
    return d * cos2 + pltpu.roll(d * sin2, RET_QK_DIM // 2, axis=1)


def _ret_step(q, k, v, st, r_intra, qd, kd, gamma):
    k = k * (RET_QK_DIM ** -0.5)
    sc = _mm(q, k, 'nt') * r_intra
    y = _mm(sc, v, 'nn') + _mm(q * qd, st, 'nn')
    st_new = st * gamma + _mm(k * kd, v, 'tn')
    return y, st_new


def _ssd_heads(dtraw, dtb, a_c, tri):
    dt = _softplus(dtraw + dtb)
    return dt, _dot_sel(dt * a_c, tri, left=True)


def _ssd_group(dte0, dte1, cum0, cum1, xs0, xs1, bm, cm, ht0, ht1, eye2, bdm, last):
    cbp = _mm(cm, jnp.concatenate([bm, bm], axis=0), 'nt')
    outs = []
    for dte, cum, xs, ht in ((dte0, cum0, xs0, ht0), (dte1, cum1, xs1, ht1)):
        r = jnp.sum(cum * eye2, axis=0, keepdims=True)
        dlt = cum - r
        seg = jnp.exp(jnp.where(dlt > 0.0, -dlt, dlt))
        xdt = xs * dte
        bd = jnp.concatenate([xdt, xdt], axis=0) * bdm
        clast = jnp.sum(cum * last, axis=0, keepdims=True)
        y = _mm(cbp * seg, bd, 'nn') + jnp.exp(cum) * _mm(cm, ht, 'nn')
        ht_new = jnp.exp(clast) * ht + _mm(bm, xdt * jnp.exp(clast - cum), 'tn')
        outs += [y, ht_new]
    return tuple(outs)


def _scan_in_specs(nc, rev):
    ch = (lambda c: nc - 1 - c) if rev else (lambda c: c)
    col = lambda w, blk: pl.BlockSpec((CHUNK, w), lambda c: (ch(c), blk))
    return [col(RET_QK, COL_Q // RET_QK), col(RET_QK, COL_K // RET_QK), col(RET_V, COL_V // RET_V),
            col(SSM_INNER, 0), col(SSM_BC, 2), col(SSM_BC, 3),
            col(DT_PAD, 0), col(LANES, 0), col(LANES, 0)]


def _const_specs(consts):
    return [_full(c.shape) for c in consts]


def _tile(t):
    return slice(t * LANES, (t + 1) * LANES)


def _scan_fwd(proj, xbc, dtraw, cos2, sin2, a_c, dtb, name):
    s = proj.shape[0]
    nc = s // CHUNK
    consts, gam = _scan_tables()

    def body(q_ref, k_ref, v_ref, xs_ref, bm_ref, cm_ref, dt_ref, cos_ref, sin_ref, ac_ref, dtb_ref,
             ri_ref, qd_ref, kd_ref, eye_ref, bdm_ref, last_ref, e_ref, et_ref, tri_ref, trit_ref,
             yr_ref, ys_ref, sh_ref, hh_ref, st_sc, ht_sc):
        @pl.when(pl.program_id(0) == 0)
        def _():
            st_sc[...] = jnp.zeros_like(st_sc)
            ht_sc[...] = jnp.zeros_like(ht_sc)

        sh_ref[0] = st_sc[...]
        hh_ref[0] = ht_sc[...]
        cos2, sin2 = cos_ref[...], sin_ref[...]
        st_new = []
        for h in range(RET_HEADS):
            ql = slice(h * RET_QK_DIM, (h + 1) * RET_QK_DIM)
            vl = slice(h * RET_V_DIM, (h + 1) * RET_V_DIM)
            y, st_h = _ret_step(_rope(q_ref[:, ql], cos2, sin2), _rope(k_ref[:, ql], cos2, sin2), v_ref[:, vl],
                                st_sc[ql, :], ri_ref[h], qd_ref[:, ql], kd_ref[:, ql], gam[h])
            yr_ref[:, vl] = y
            st_new.append(st_h)
        dt, cum_c = _ssd_heads(dt_ref[...], dtb_ref[...], ac_ref[...], tri_ref[...])
        both = jnp.concatenate([dt, cum_c], axis=0)
        eye2, bdm, last = eye_ref[...], bdm_ref[...], last_ref[...]
        ht_new = []
        for g in range(SSM_GROUPS):
            t0, t1 = 2 * g, 2 * g + 1
            e0, e1 = _dot_sel(both, e_ref[:, _tile(t0)]), _dot_sel(both, e_ref[:, _tile(t1)])
            y0, h0, y1, h1 = _ssd_group(e0[:CHUNK], e1[:CHUNK], e0[CHUNK:], e1[CHUNK:],
                                        xs_ref[:, _tile(t0)], xs_ref[:, _tile(t1)], bm_ref[:, _tile(g)],
                                        cm_ref[:, _tile(g)], ht_sc[:, _tile(t0)], ht_sc[:, _tile(t1)], eye2, bdm, last)
            ys_ref[:, _tile(t0)] = y0
            ys_ref[:, _tile(t1)] = y1
            ht_new += [h0, h1]
        for h in range(RET_HEADS):
            st_sc[h * RET_QK_DIM:(h + 1) * RET_QK_DIM, :] = st_new[h]
        for t in range(N_LTILE):
            ht_sc[:, _tile(t)] = ht_new[t]

    return pl.pallas_call(
        body, grid=(nc,),
        in_specs=_scan_in_specs(nc, False) + [_full((1, DT_PAD)), _full((1, DT_PAD))] + _const_specs(consts),
        out_specs=[pl.BlockSpec((CHUNK, RET_V), lambda c: (c, 0)),
                   pl.BlockSpec((CHUNK, SSM_INNER), lambda c: (c, 0)),
                   pl.BlockSpec((1, RET_QK, RET_V_DIM), lambda c: (c, 0, 0)),
                   pl.BlockSpec((1, SSM_STATE, SSM_INNER), lambda c: (c, 0, 0))],
        out_shape=[_sds((s, RET_V), F32), _sds((s, SSM_INNER), F32),
                   _sds((nc, RET_QK, RET_V_DIM), F32), _sds((nc, SSM_STATE, SSM_INNER), F32)],
        scratch_shapes=[pltpu.VMEM((RET_QK, RET_V_DIM), F32), pltpu.VMEM((SSM_STATE, SSM_INNER), F32)],
        compiler_params=_cparams(("arbitrary",)), name=name,
    )(proj, proj, proj, xbc, xbc, xbc, dtraw, cos2, sin2, a_c, dtb, *consts)


def _scan_bwd(proj, xbc, dtraw, cos2, sin2, a_c, dtb, s_hist, h_hist, dyr, dys, dxs_skip, name):
    s = proj.shape[0]
    nc = s // CHUNK
    consts, gam = _scan_tables()
    rv = lambda c: nc - 1 - c

    def body(q_ref, k_ref, v_ref, xs_ref, bm_ref, cm_ref, dt_ref, cos_ref, sin_ref, ac_ref, dtb_ref,
             ri_ref, qd_ref, kd_ref, eye_ref, bdm_ref, last_ref, e_ref, et_ref, tri_ref, trit_ref,
             sh_ref, hh_ref, dyr_ref, dys_ref, dsk_ref,
             dqk_ref, dv_ref, dxbc_ref, ddt_ref, dac_ref, ddtb_ref, dst_sc, dht_sc):
        @pl.when(pl.program_id(0) == 0)
        def _():
            dst_sc[...] = jnp.zeros_like(dst_sc)
            dht_sc[...] = jnp.zeros_like(dht_sc)
            dac_ref[...] = jnp.zeros_like(dac_ref)
            ddtb_ref[...] = jnp.zeros_like(ddtb_ref)

        cos2, sin2 = cos_ref[...], sin_ref[...]
        dst_new = []
        for h in range(RET_HEADS):
            ql = slice(h * RET_QK_DIM, (h + 1) * RET_QK_DIM)
            vl = slice(h * RET_V_DIM, (h + 1) * RET_V_DIM)
            step = functools.partial(_ret_step, r_intra=ri_ref[h], qd=qd_ref[:, ql], kd=kd_ref[:, ql], gamma=gam[h])
            _, vjp = jax.vjp(step, _rope(q_ref[:, ql], cos2, sin2), _rope(k_ref[:, ql], cos2, sin2), v_ref[:, vl],
                             sh_ref[0, ql, :])
            dq, dk, dv, dst = vjp((dyr_ref[:, vl], dst_sc[ql, :]))
            dqk_ref[:, ql] = _rope_t(dq, cos2, sin2).astype(dqk_ref.dtype)
            dqk_ref[:, slice(RET_QK + ql.start, RET_QK + ql.stop)] = _rope_t(dk, cos2, sin2).astype(dqk_ref.dtype)
            dv_ref[:, vl] = dv.astype(dv_ref.dtype)
            dst_new.append(dst)
        dtraw_v, dtb_v, a_c, tri = dt_ref[...], dtb_ref[...], ac_ref[...], tri_ref[...]
        dt, cum_c = _ssd_heads(dtraw_v, dtb_v, a_c, tri)
        both = jnp.concatenate([dt, cum_c], axis=0)
        eye2, bdm, last = eye_ref[...], bdm_ref[...], last_ref[...]
        group = functools.partial(_ssd_group, eye2=eye2, bdm=bdm, last=last)
        d_both = jnp.zeros((2 * CHUNK, LANES), F32)
        dht_new = []
        for g in range(SSM_GROUPS):
            t0, t1 = 2 * g, 2 * g + 1
            e0, e1 = _dot_sel(both, e_ref[:, _tile(t0)]), _dot_sel(both, e_ref[:, _tile(t1)])
            _, vjp = jax.vjp(group, e0[:CHUNK], e1[:CHUNK], e0[CHUNK:], e1[CHUNK:],
                             xs_ref[:, _tile(t0)], xs_ref[:, _tile(t1)], bm_ref[:, _tile(g)], cm_ref[:, _tile(g)],
                             hh_ref[0, :, _tile(t0)], hh_ref[0, :, _tile(t1)])
            (d_dte0, d_dte1, d_cum0, d_cum1, d_xs0, d_xs1, d_bm, d_cm, d_ht0, d_ht1) = vjp(
                (dys_ref[:, _tile(t0)], dht_sc[:, _tile(t0)], dys_ref[:, _tile(t1)], dht_sc[:, _tile(t1)]))
            d_both = d_both + _dot_sel(jnp.concatenate([d_dte0, d_cum0], axis=0), et_ref[_tile(t0), :])
            d_both = d_both + _dot_sel(jnp.concatenate([d_dte1, d_cum1], axis=0), et_ref[_tile(t1), :])
            dxbc_ref[:, _tile(t0)] = d_xs0 + dsk_ref[:, _tile(t0)]
            dxbc_ref[:, _tile(t1)] = d_xs1 + dsk_ref[:, _tile(t1)]
            dxbc_ref[:, _tile(N_LTILE + g)] = d_bm
            dxbc_ref[:, _tile(N_LTILE + SSM_GROUPS + g)] = d_cm
            dht_new += [d_ht0, d_ht1]
        d_da = _dot_sel(d_both[CHUNK:], trit_ref[...], left=True)
        d_dt = d_both[:CHUNK] + d_da * a_c
        d_pre = d_dt * jax.nn.sigmoid(dtraw_v + dtb_v)
        ddt_ref[...] = d_pre
        ddtb_ref[...] += jnp.sum(d_pre, axis=0, keepdims=True)
        dac_ref[...] += jnp.sum(d_da * dt, axis=0, keepdims=True)
        for h in range(RET_HEADS):
            dst_sc[h * RET_QK_DIM:(h + 1) * RET_QK_DIM, :] = dst_new[h]
        for t in range(N_LTILE):
            dht_sc[:, _tile(t)] = dht_new[t]

    return pl.pallas_call(
        body, grid=(nc,),
        in_specs=(_scan_in_specs(nc, True) + [_full((1, DT_PAD)), _full((1, DT_PAD))] + _const_specs(consts)
                  + [pl.BlockSpec((1, RET_QK, RET_V_DIM), lambda c: (rv(c), 0, 0)),
                     pl.BlockSpec((1, SSM_STATE, SSM_INNER), lambda c: (rv(c), 0, 0)),
                     pl.BlockSpec((CHUNK, RET_V), lambda c: (rv(c), 0)),
                     pl.BlockSpec((CHUNK, SSM_INNER), lambda c: (rv(c), 0)),
                     pl.BlockSpec((CHUNK, SSM_INNER), lambda c: (rv(c), 0))]),
        out_specs=[pl.BlockSpec((CHUNK, 2 * RET_QK), lambda c: (rv(c), 0)),
                   pl.BlockSpec((CHUNK, RET_V), lambda c: (rv(c), 0)),
                   pl.BlockSpec((CHUNK, SSM_CONV_DIM), lambda c: (rv(c), 0)),
                   pl.BlockSpec((CHUNK, DT_PAD), lambda c: (rv(c), 0)),
                   _full((1, DT_PAD)), _full((1, DT_PAD))],
        out_shape=[_sds((s, 2 * RET_QK), MXU_DTYPE), _sds((s, RET_V), MXU_DTYPE),
                   _sds((s, SSM_CONV_DIM), F32), _sds((s, DT_PAD), F32),
                   _sds((1, DT_PAD), F32), _sds((1, DT_PAD), F32)],
        scratch_shapes=[pltpu.VMEM((RET_QK, RET_V_DIM), F32), pltpu.VMEM((SSM_STATE, SSM_INNER), F32)],
        compiler_params=_cparams(("arbitrary",)), name=name,
    )(proj, proj, proj, xbc, xbc, xbc, dtraw, cos2, sin2, a_c, dtb, *consts, s_hist, h_hist, dyr, dys, dxs_skip)


POST_W = 256


def _post_ret(y, g):
    return _rms(y) * _silu(g)


def _post_ssm(y, xs, z, dsk, nw):
    return _rms((y + xs * dsk) * _silu(z)) * nw


def _post_specs(t):
    return [pl.BlockSpec((t, RET_V), lambda i: (i, 0)),
            pl.BlockSpec((t, RET_V), lambda i: (i, COL_G // RET_V)),
            pl.BlockSpec((t, SSM_INNER), lambda i: (i, 0)),
            pl.BlockSpec((t, SSM_INNER), lambda i: (i, 0)),
            pl.BlockSpec((t, SSM_INNER), lambda i: (i, COL_Z // SSM_INNER)),
            _full((1, SSM_INNER)), _full((1, SSM_INNER))]


def _post_fwd(y_ret, proj, y_ssm, xbc, dsk_e, ssm_norm, name):
    s = y_ret.shape[0]
    t = _pick(s, 256)

    def body(yr_ref, g_ref, ys_ref, xs_ref, z_ref, dsk_ref, nw_ref, or_ref, os_ref, ort_ref, ost_ref):
        for h in range(RET_V // POST_W):
            sl = slice(h * POST_W, (h + 1) * POST_W)
            o = _post_ret(yr_ref[:, sl], g_ref[:, sl])
            or_ref[:, sl] = o.astype(or_ref.dtype)
            ort_ref[sl, :] = o.T.astype(ort_ref.dtype)
        for g in range(SSM_INNER // POST_W):
            sl = slice(g * POST_W, (g + 1) * POST_W)
            o = _post_ssm(ys_ref[:, sl], xs_ref[:, sl], z_ref[:, sl], dsk_ref[:, sl], nw_ref[:, sl])
            os_ref[:, sl] = o.astype(os_ref.dtype)
            ost_ref[sl, :] = o.T.astype(ost_ref.dtype)

    return pl.pallas_call(
        body, grid=(s // t,), in_specs=_post_specs(t),
        out_specs=[pl.BlockSpec((t, RET_V), lambda i: (i, 0)), pl.BlockSpec((t, SSM_INNER), lambda i: (i, 0)),
                   pl.BlockSpec((RET_V, t), lambda i: (0, i)), pl.BlockSpec((SSM_INNER, t), lambda i: (0, i))],
        out_shape=[_sds((s, RET_V), MXU_DTYPE), _sds((s, SSM_INNER), MXU_DTYPE),
                   _sds((RET_V, s), MXU_DTYPE), _sds((SSM_INNER, s), MXU_DTYPE)],
        compiler_params=_cparams(("parallel",)), name=name,
    )(y_ret, proj, y_ssm, xbc, proj, dsk_e, ssm_norm.reshape(1, SSM_INNER))


def _post_bwd(y_ret, proj, y_ssm, xbc, dsk_e, ssm_norm, d_or, d_os, name):
    s = y_ret.shape[0]
    t = _pick(s, 256)

    def body(yr_ref, g_ref, ys_ref, xs_ref, z_ref, dsk_ref, nw_ref, dor_ref, dos_ref,
             dyr_ref, dg_ref, dys_ref, dxs_ref, dz_ref, ddsk_ref, dnw_ref):
        @pl.when(pl.program_id(0) == 0)
        def _():
            ddsk_ref[...] = jnp.zeros_like(ddsk_ref)
            dnw_ref[...] = jnp.zeros_like(dnw_ref)

        for h in range(RET_V // POST_W):
            sl = slice(h * POST_W, (h + 1) * POST_W)
            _, vjp = jax.vjp(_post_ret, yr_ref[:, sl], g_ref[:, sl])
            dyr, dg = vjp(dor_ref[:, sl])
            dyr_ref[:, sl] = dyr
            dg_ref[:, sl] = dg.astype(dg_ref.dtype)
        for g in range(SSM_INNER // POST_W):
            sl = slice(g * POST_W, (g + 1) * POST_W)
            _, vjp = jax.vjp(_post_ssm, ys_ref[:, sl], xs_ref[:, sl], z_ref[:, sl], dsk_ref[:, sl], nw_ref[:, sl])
            dy, dxs, dz, ddsk, dnw = vjp(dos_ref[:, sl])
            dys_ref[:, sl] = dy
            dxs_ref[:, sl] = dxs
            dz_ref[:, sl] = dz.astype(dz_ref.dtype)
            ddsk_ref[:, sl] += ddsk
            dnw_ref[:, sl] += dnw

    rowv = pl.BlockSpec((t, RET_V), lambda i: (i, 0))
    rows = pl.BlockSpec((t, SSM_INNER), lambda i: (i, 0))
    return pl.pallas_call(
        body, grid=(s // t,), in_specs=_post_specs(t) + [rowv, rows],
        out_specs=[rowv, rowv, rows, rows, rows, _full((1, SSM_INNER)), _full((1, SSM_INNER))],
        out_shape=[_sds((s, RET_V), F32), _sds((s, RET_V), MXU_DTYPE), _sds((s, SSM_INNER), F32),
                   _sds((s, SSM_INNER), F32), _sds((s, SSM_INNER), MXU_DTYPE),
                   _sds((1, SSM_INNER), F32), _sds((1, SSM_INNER), F32)],
        compiler_params=_cparams(("arbitrary",)), name=name,
    )(y_ret, proj, y_ssm, xbc, proj, dsk_e, ssm_norm.reshape(1, SSM_INNER), d_or, d_os)


def _merge_fn(gr, gs, br, bs, yr, ys):
    return jax.nn.sigmoid(gr + br) * yr + jax.nn.sigmoid(gs + bs) * ys


def _merge_specs(t):
    row = pl.BlockSpec((t, D_MODEL), lambda i: (i, 0))
    return [pl.BlockSpec((t, D_MODEL), lambda i: (i, COL_GATES // D_MODEL)),
            pl.BlockSpec((t, D_MODEL), lambda i: (i, COL_GATES // D_MODEL + 1)),
            pl.BlockSpec((1, D_MODEL), lambda i: (0, 0)), pl.BlockSpec((1, D_MODEL), lambda i: (0, 1)), row, row]


def _merge_fwd(proj, b_gate, br_ret, br_ssm, name):
    s = proj.shape[0]
    t = _pick(s, 512)

    def body(gr_ref, gs_ref, br_ref, bs_ref, yr_ref, ys_ref, o_ref, ot_ref):
        o = _merge_fn(gr_ref[...], gs_ref[...], br_ref[...], bs_ref[...], yr_ref[...], ys_ref[...])
        o_ref[...] = o.astype(o_ref.dtype)
        ot_ref[...] = o.T.astype(ot_ref.dtype)

    bg = b_gate.reshape(1, 2 * D_MODEL)
    return pl.pallas_call(
        body, grid=(s // t,), in_specs=_merge_specs(t),
        out_specs=[pl.BlockSpec((t, D_MODEL), lambda i: (i, 0)), pl.BlockSpec((D_MODEL, t), lambda i: (0, i))],
        out_shape=[_sds((s, D_MODEL), MXU_DTYPE), _sds((D_MODEL, s), MXU_DTYPE)],
        compiler_params=_cparams(("parallel",)), name=name,
    )(proj, proj, bg, bg, br_ret, br_ssm)


def _merge_bwd(proj, b_gate, br_ret, br_ssm, dm, name):
    s = proj.shape[0]
    t = _pick(s, 512)

    def body(gr_ref, gs_ref, br_ref, bs_ref, yr_ref, ys_ref, dm_ref, dgt_ref, db_ref, dyr_ref, dys_ref):
        @pl.when(pl.program_id(0) == 0)
        def _():
            db_ref[...] = jnp.zeros_like(db_ref)

        _, vjp = jax.vjp(_merge_fn, gr_ref[...], gs_ref[...], br_ref[...], bs_ref[...], yr_ref[...], ys_ref[...])
        dgr, dgs, dbr, dbs, dyr, dys = vjp(dm_ref[...])
        dgt_ref[:, :D_MODEL] = dgr.astype(dgt_ref.dtype)
        dgt_ref[:, D_MODEL:] = dgs.astype(dgt_ref.dtype)
        db_ref[:, :D_MODEL] += dbr
        db_ref[:, D_MODEL:] += dbs
        dyr_ref[...] = dyr.astype(dyr_ref.dtype)
        dys_ref[...] = dys.astype(dys_ref.dtype)

    bg = b_gate.reshape(1, 2 * D_MODEL)
    row = pl.BlockSpec((t, D_MODEL), lambda i: (i, 0))
    return pl.pallas_call(
        body, grid=(s // t,), in_specs=_merge_specs(t) + [row],
        out_specs=[pl.BlockSpec((t, 2 * D_MODEL), lambda i: (i, 0)), _full((1, 2 * D_MODEL)), row, row],
        out_shape=[_sds((s, 2 * D_MODEL), MXU_DTYPE), _sds((1, 2 * D_MODEL), F32),
                   _sds((s, D_MODEL), MXU_DTYPE), _sds((s, D_MODEL), MXU_DTYPE)],
        compiler_params=_cparams(("arbitrary",)), name=name,
    )(proj, proj, bg, bg, br_ret, br_ssm, dm)


def _attn_head(q, k, v):
    sc = _mm(q, k, 'nt') * (XA_HEAD_DIM ** -0.5)
    e = jnp.exp(sc - lax.stop_gradient(jnp.max(sc, axis=-1, keepdims=True)))
    p = e / jnp.sum(e, axis=-1, keepdims=True)
    return _mm(p, v, 'nn')


def _attn_fwd(q, kv, name):
    s = q.shape[0]
    m = kv.shape[0]
    t = _pick(s, 512)

    def body(q_ref, kv_ref, o_ref, ot_ref):
        for h in range(XA_HEADS):
            sl = slice(h * XA_HEAD_DIM, (h + 1) * XA_HEAD_DIM)
            vl = slice(D_MODEL + h * XA_HEAD_DIM, D_MODEL + (h + 1) * XA_HEAD_DIM)
            o = _attn_head(q_ref[:, sl], kv_ref[:, sl], kv_ref[:, vl])
            o_ref[:, sl] = o.astype(o_ref.dtype)
            ot_ref[sl, :] = o.T.astype(ot_ref.dtype)

    return pl.pallas_call(
        body, grid=(s // t,),
        in_specs=[pl.BlockSpec((t, D_MODEL), lambda i: (i, 0)), _full((m, 2 * D_MODEL))],
        out_specs=[pl.BlockSpec((t, D_MODEL), lambda i: (i, 0)), pl.BlockSpec((D_MODEL, t), lambda i: (0, i))],
        out_shape=[_sds((s, D_MODEL), MXU_DTYPE), _sds((D_MODEL, s), MXU_DTYPE)],
        compiler_params=_cparams(("parallel",)), name=name,
    )(q, kv)


def _attn_bwd(q, kv, d_o, name):
    s = q.shape[0]
    m = kv.shape[0]
    t = _pick(s, 512)

    def body(q_ref, kv_ref, do_ref, dq_ref, dkv_ref):
        @pl.when(pl.program_id(0) == 0)
        def _():
            dkv_ref[...] = jnp.zeros_like(dkv_ref)

        for h in range(XA_HEADS):
            sl = slice(h * XA_HEAD_DIM, (h + 1) * XA_HEAD_DIM)
            vl = slice(D_MODEL + h * XA_HEAD_DIM, D_MODEL + (h + 1) * XA_HEAD_DIM)
            _, vjp = jax.vjp(_attn_head, q_ref[:, sl], kv_ref[:, sl], kv_ref[:, vl])
            dq, dk, dv = vjp(do_ref[:, sl])
            dq_ref[:, sl] = dq.astype(dq_ref.dtype)
            dkv_ref[:, sl] += dk
            dkv_ref[:, vl] += dv

    row = pl.BlockSpec((t, D_MODEL), lambda i: (i, 0))
    return pl.pallas_call(
        body, grid=(s // t,), in_specs=[row, _full((m, 2 * D_MODEL)), row],
        out_specs=[row, _full((m, 2 * D_MODEL))],
        out_shape=[_sds((s, D_MODEL), MXU_DTYPE), _sds((m, 2 * D_MODEL), F32)],
        compiler_params=_cparams(("arbitrary",)), name=name,
    )(q, kv, d_o)


def _loss_head(x, w, target, name):
    s, d = x.shape
    t = _pick(s, 512)

    def body(x_ref, w_ref, t_ref, loss_ref, dx_ref, dxb_ref, dw_ref):
        @pl.when(pl.program_id(0) == 0)
        def _():
            loss_ref[...] = jnp.zeros_like(loss_ref)
            dw_ref[...] = jnp.zeros_like(dw_ref)

        y, vjp = jax.vjp(_rmsnorm_fn, x_ref[...], w_ref[...])
        err = y - t_ref[...]
        loss_ref[...] += 0.5 * jnp.sum(jnp.sum(err * err, axis=-1, keepdims=True), axis=0, keepdims=True) / d
        dx, dw = vjp(err * (1.0 / d))
        dx_ref[...] = dx
        dxb_ref[...] = dx.astype(dxb_ref.dtype)
        dw_ref[...] += dw

    row = pl.BlockSpec((t, d), lambda i: (i, 0))
    return pl.pallas_call(
        body, grid=(s // t,), in_specs=[row, _full((1, d)), row],
        out_specs=[_full((1, LANES)), row, row, _full((1, d))],
        out_shape=[_sds((1, LANES), F32), _sds((s, d), F32), _sds((s, d), MXU_DTYPE), _sds((1, d), F32)],
        compiler_params=_cparams(("arbitrary",)), name=name,
    )(x, w.reshape(1, d), target)


def _epi_sqrelu(acc):
    r = jnp.maximum(acc, 0.0)
    return acc, r * r, r * r


def _epi_sqrelu_bwd(acc, a):
    return (acc * (2.0 * jnp.maximum(a, 0.0)),)


def _rope_tables(positions):
    inv_freq = ROPE_THETA ** (-jnp.arange(0, RET_QK_DIM, 2, dtype=F32) / RET_QK_DIM)
    ang = positions.astype(F32)[:, None] * inv_freq
    cos, sin = jnp.cos(ang), jnp.sin(ang)
    return jnp.concatenate([cos, cos], axis=1), jnp.concatenate([-sin, sin], axis=1)


W_IN_ORIG = (('q', 0, 512), ('k', 512, 1024), ('v', 1024, 2048), ('g', 2048, 3072), ('z', 3072, 5120),
             ('xbc', 5120, 9216), ('dt', 9216, 9248), ('gates', 9248, 11296))
W_IN_MAIN_ORDER = ('z', 'xbc', 'gates', 'v', 'g', 'q', 'k')
W_IN_SHARD = IN_DIM // N_DEV


def _shard_segments(lo, hi):
    segs = []
    for j in range(lo // W_IN_SHARD, (hi - 1) // W_IN_SHARD + 1):
        segs.append((j, max(lo, j * W_IN_SHARD) - j * W_IN_SHARD, min(hi, (j + 1) * W_IN_SHARD) - j * W_IN_SHARD))
    return segs


def _w_in_from_shards(g):
    rng = {name: (lo, hi) for name, lo, hi in W_IN_ORIG}
    cols = [g[j][:, a:b] for name in W_IN_MAIN_ORDER for j, a, b in _shard_segments(*rng[name])]
    (j, a, b), = _shard_segments(*rng['dt'])
    return jnp.concatenate(cols, axis=1), jnp.pad(g[j][:, a:b], ((0, 0), (0, DT_PAD - SSM_HEADS)))


def _w_in_grad_blocks(d, d_dt):
    src_of = {'q': ('qk', 0), 'k': ('qk', RET_QK)}
    blocks = []
    for j in range(N_DEV):
        lo_j, hi_j = j * W_IN_SHARD, (j + 1) * W_IN_SHARD
        cols = []
        for name, lo, hi in W_IN_ORIG:
            a, b = max(lo, lo_j), min(hi, hi_j)
            if a >= b:
                continue
            if name == 'dt':
                cols.append(d_dt[:, a - lo:b - lo])
            else:
                key, off = src_of.get(name, (name, 0))
                cols.append(d[key][:, off + a - lo:off + b - lo])
        blocks.append(jnp.concatenate(cols, axis=1))
    return blocks


def _lanes_of_heads(v):
    return jnp.repeat(v, SSM_HEAD_DIM).reshape(1, SSM_INNER)


def _heads_of_lanes(v):
    return v.reshape(SSM_HEADS, SSM_HEAD_DIM).sum(axis=1)


def _layer_fwd(x, mem, cos2, sin2, p, l):
    n = lambda s: f"{s}_l{l}"
    sv = {'x0': x}
    u, u_t = _rmsnorm(x, p['norm_mix'], n("norm_mix"))
    proj = _matmul(u, p['w_in_main'], 'nn', n("in_proj"))
    dtraw = _matmul(u, p['w_in_dt'], 'nn', n("in_proj_dt"))
    xbc = _conv_fwd(proj, p['conv_w'], p['conv_b'], n("conv"))
    a_c = jnp.pad(-jnp.exp(p['a_log']), (0, DT_PAD - SSM_HEADS)).reshape(1, DT_PAD)
    dtb = jnp.pad(p['dt_bias'], (0, DT_PAD - SSM_HEADS)).reshape(1, DT_PAD)
    y_ret, y_ssm, s_hist, h_hist = _scan_fwd(proj, xbc, dtraw, cos2, sin2, a_c, dtb, n("scan"))
    dsk_e = _lanes_of_heads(p['d_skip'])
    o_ret, o_ssm, o_ret_t, o_ssm_t = _post_fwd(y_ret, proj, y_ssm, xbc, dsk_e, p['ssm_norm'], n("post"))
    br_ret = _matmul(o_ret, p['w_br_ret'], 'nn', n("br_ret"))
    br_ssm = _matmul(o_ssm, p['w_br_ssm'], 'nn', n("br_ssm"))
    merged, merged_t = _merge_fwd(proj, p['b_gate'], br_ret, br_ssm, n("merge"))
    x1 = _matmul(merged, p['w_out'], 'nn', n("w_out"), extras=(x,), epi=_epi_add)
    sv.update(u_t=u_t, proj=proj, dtraw=dtraw, xbc=xbc, a_c=a_c, dtb=dtb, y_ret=y_ret, y_ssm=y_ssm, s_hist=s_hist,
              h_hist=h_hist, dsk_e=dsk_e, o_ret_t=o_ret_t, o_ssm_t=o_ssm_t, br_ret=br_ret, br_ssm=br_ssm,
              merged_t=merged_t, x1=x1)
    hq, hq_t = _rmsnorm(x1, p['norm_xa'], n("norm_xa"))
    memn, _ = _rmsnorm(mem, p['norm_mem'], n("norm_mem"))
    q = _matmul(hq, p['xa_wq'], 'nn', n("xa_q"))
    kv = _matmul(memn, p['xa_wkv'], 'nn', n("xa_kv"))
    o, o_t = _attn_fwd(q, kv, n("attn"))
    x2 = _matmul(o, p['xa_wo'], 'nn', n("xa_o"), extras=(x1,), epi=_epi_add)
    sv.update(hq_t=hq_t, memn=memn, q=q, kv=kv, o_t=o_t, x2=x2)
    hm, hm_t = _rmsnorm(x2, p['norm_mlp'], n("norm_mlp"))
    a, act, act_t = _matmul(hm, p['mlp_w1'], 'nn', n("mlp_1"), epi=_epi_sqrelu, out_dtypes=(F32, MXU_DTYPE, MXU_DTYPE),
                            out_t=(False, False, True))
    x3 = _matmul(act, p['mlp_w2'], 'nn', n("mlp_2"), extras=(x2,), epi=_epi_add)
    sv.update(hm_t=hm_t, a=a, act_t=act_t)
    return x3, sv


def _layer_bwd(dx, dxb, mem, cos2, sin2, p, sv, l):
    n = lambda s: f"{s}_bwd_l{l}"
    gd = (MXU_DTYPE,)
    g = {}
    g['mlp_w2'] = _matmul(sv['act_t'], dxb, 'nn', n("mlp_2_dw"), out_dtypes=gd)
    da = _matmul(dxb, p['mlp_w2'], 'nt', n("mlp_2_dx"), extras=(sv['a'],), epi=_epi_sqrelu_bwd, out_dtypes=(MXU_DTYPE,))
    g['mlp_w1'] = _matmul(sv['hm_t'], da, 'nn', n("mlp_1_dw"), out_dtypes=gd)
    dhm = _matmul(da, p['mlp_w1'], 'nt', n("mlp_1_dx"))
    dx2, dx2b, g['norm_mlp'] = _rmsnorm_bwd(sv['x2'], p['norm_mlp'], dhm, dx, n("norm_mlp"))
    g['xa_wo'] = _matmul(sv['o_t'], dx2b, 'nn', n("xa_o_dw"), out_dtypes=gd)
    d_o = _matmul(dx2b, p['xa_wo'], 'nt', n("xa_o_dx"))
    dq, dkv = _attn_bwd(sv['q'], sv['kv'], d_o, n("attn"))
    g['xa_wq'] = _matmul(sv['hq_t'], dq, 'nn', n("xa_q_dw"), out_dtypes=gd)
    dhq = _matmul(dq, p['xa_wq'], 'nt', n("xa_q_dx"))
    g['xa_wkv'] = _matmul(sv['memn'], dkv, 'tn', n("xa_kv_dw"), out_dtypes=gd)
    dmemn = _matmul(dkv, p['xa_wkv'], 'nt', n("xa_kv_dx"))
    _, _, g['norm_mem'] = _rmsnorm_bwd(mem, p['norm_mem'], dmemn, None, n("norm_mem"))
    dx1, dx1b, g['norm_xa'] = _rmsnorm_bwd(sv['x1'], p['norm_xa'], dhq, dx2, n("norm_xa"))
    g['w_out'] = _matmul(sv['merged_t'], dx1b, 'nn', n("w_out_dw"), out_dtypes=gd)
    dmerged = _matmul(dx1b, p['w_out'], 'nt', n("w_out_dx"))
    dgates, g['b_gate'], dbr_ret, dbr_ssm = _merge_bwd(sv['proj'], p['b_gate'], sv['br_ret'], sv['br_ssm'], dmerged, n("merge"))
    g['w_br_ret'] = _matmul(sv['o_ret_t'], dbr_ret, 'nn', n("br_ret_dw"), out_dtypes=gd)
    g['w_br_ssm'] = _matmul(sv['o_ssm_t'], dbr_ssm, 'nn', n("br_ssm_dw"), out_dtypes=gd)
    d_or = _matmul(dbr_ret, p['w_br_ret'], 'nt', n("br_ret_dx"))
    d_os = _matmul(dbr_ssm, p['w_br_ssm'], 'nt', n("br_ssm_dx"))
    dyr, dg, dys, dxs_skip, dz, ddsk_e, g['ssm_norm'] = _post_bwd(
        sv['y_ret'], sv['proj'], sv['y_ssm'], sv['xbc'], sv['dsk_e'], p['ssm_norm'], d_or, d_os, n("post"))
    g['d_skip'] = _heads_of_lanes(ddsk_e)
    dqk_r, dv_r, dxbc_act, ddtraw, dac, ddtb = _scan_bwd(
        sv['proj'], sv['xbc'], sv['dtraw'], cos2, sin2, sv['a_c'], sv['dtb'], sv['s_hist'], sv['h_hist'],
        dyr, dys, dxs_skip, n("scan"))
    g['a_log'] = dac[0, :SSM_HEADS] * (-jnp.exp(p['a_log']))
    g['dt_bias'] = ddtb[0, :SSM_HEADS]
    dxbc_raw, g['conv_w'], g['conv_b'] = _conv_bwd(sv['proj'], p['conv_w'], p['conv_b'], dxbc_act, n("conv"))
    pieces = {'z': dz, 'xbc': dxbc_raw, 'gates': dgates, 'v': dv_r, 'g': dg, 'qk': dqk_r}
    d_w = {k: _matmul(sv['u_t'], pc, 'nn', n(f"in_proj_dw_{k}"), out_dtypes=gd) for k, pc in pieces.items()}
    d_dt = _matmul(sv['u_t'], ddtraw, 'nn', n("in_proj_dt_dw"), out_dtypes=gd)
    g['w_in'] = _w_in_grad_blocks(d_w, d_dt)
    du_dt = _matmul(ddtraw, p['w_in_dt'], 'nt', n("in_proj_dt_dx"))
    du = _matmul_nt_pieces(list(pieces.values()), p['w_in_main'], n("in_proj_dx"), extras=(du_dt,), epi=_epi_add)
    dx0, dx0b, g['norm_mix'] = _rmsnorm_bwd(sv['x0'], p['norm_mix'], du, dx1, n("norm_mix"))
    return dx0, dx0b, g


def _local_step(x, mem, positions, w, loss_target):
    cos2, sin2 = _rope_tables(positions)
    saved, layers = [], []
    for l in range(DEPTH):
        p = {k: w[k][l] for k in WEIGHTS if k not in ('norm_final', 'w_in')}
        p['w_in_main'], p['w_in_dt'] = _w_in_from_shards(w['w_in'][:, l])
        x, sv = _layer_fwd(x, mem, cos2, sin2, p, l)
        saved.append(sv)
        layers.append(p)
    loss, dx, dxb, dnf = _loss_head(x, w['norm_final'], loss_target, "loss_head")
    grads = [None] * DEPTH
    for l in reversed(range(DEPTH)):
        dx, dxb, grads[l] = _layer_bwd(dx, dxb, mem, cos2, sin2, layers[l], saved[l], l)
    out = {}
    for k in WEIGHTS:
        if k == 'norm_final':
            out[k] = dnf.reshape(D_MODEL)
        else:
            out[k] = [grads[l][k] if k == 'w_in' else grads[l][k].reshape(w[k].shape[1:]) for l in range(DEPTH)]
    return loss, dx, out


MESH = pl.DeviceIdType.MESH
ANY_SPEC = pl.BlockSpec(memory_space=pl.ANY)


def _mesh_pos():
    return lax.axis_index("x"), lax.axis_index("y"), lax.axis_index("c")


def _other_chips(x, y):
    return [(1 - x, y), (x, 1 - y), (1 - x, 1 - y)]


def _all_gather(arrs, name):
    na = len(arrs)

    def body(*refs):
        x_refs, o_refs = refs[:na], refs[na:2 * na]
        send_sems, recv_sems, local_sems = refs[2 * na:]
        x, y, c = _mesh_pos()
        me, sib = (x, y, c), (x, y, 1 - c)
        chips = _other_chips(x, y)

        def copy(a, k, block, to, src=None):
            dst = o_refs[a].at[4 * block[0] + 2 * block[1] + block[2]]
            return pltpu.make_async_remote_copy(src_ref=dst if src is None else src, dst_ref=dst,
                                                send_sem=send_sems.at[a, k], recv_sem=recv_sems.at[a, k],
                                                device_id=to, device_id_type=MESH)

        mine = [pltpu.make_async_copy(x_refs[a], o_refs[a].at[4 * x + 2 * y + c], local_sems.at[a]) for a in range(na)]
        for cp in mine:
            cp.start()
        first = []
        for a in range(na):
            first.append(copy(a, 0, me, sib, src=x_refs[a]))
            first += [copy(a, 1 + j, me, (*chip, c), src=x_refs[a]) for j, chip in enumerate(chips)]
        for cp in first:
            cp.start()
        passed = []
        for a in range(na):
            for j, chip in enumerate(chips):
                copy(a, 1 + j, (*chip, c), me).wait_recv()
                cp = copy(a, 4 + j, (*chip, c), sib)
                cp.start()
                passed.append(cp)
        for a in range(na):
            copy(a, 0, sib, me).wait_recv()
            for j, chip in enumerate(chips):
                copy(a, 4 + j, (*chip, 1 - c), me).wait_recv()
        for cp in first + passed:
            cp.wait_send()
        for cp in mine:
            cp.wait()

    return pl.pallas_call(
        body, in_specs=[ANY_SPEC] * na, out_specs=[ANY_SPEC] * na,
        out_shape=[_sds((N_DEV,) + a.shape, a.dtype) for a in arrs],
        scratch_shapes=[pltpu.SemaphoreType.DMA((na, 7)), pltpu.SemaphoreType.DMA((na, 7)), pltpu.SemaphoreType.DMA((na,))],
        name=name,
    )(*arrs)


def _exchange_cores(arrs, name):
    na = len(arrs)

    def body(*refs):
        a_refs, o_refs = refs[:na], refs[na:2 * na]
        send_sems, recv_sems = refs[2 * na:]
        x, y, c = _mesh_pos()
        cps = [pltpu.make_async_remote_copy(src_ref=a_refs[a].at[1 - c], dst_ref=o_refs[a], send_sem=send_sems.at[a],
                                            recv_sem=recv_sems.at[a], device_id=(x, y, 1 - c), device_id_type=MESH)
               for a in range(na)]
        for cp in cps:
            cp.start()
        for cp in cps:
            cp.wait()

    return pl.pallas_call(
        body, in_specs=[ANY_SPEC] * na, out_specs=[ANY_SPEC] * na,
        out_shape=[_sds(a.shape[1:], a.dtype) for a in arrs],
        scratch_shapes=[pltpu.SemaphoreType.DMA((na,)), pltpu.SemaphoreType.DMA((na,))],
        name=name,
    )(*arrs)


def _exchange_chips(arrs, name):
    na = len(arrs)

    def body(*refs):
        a_refs, o_refs = refs[:na], refs[na:2 * na]
        send_sems, recv_sems, local_sems = refs[2 * na:]
        x, y, c = _mesh_pos()
        my_chip = 2 * x + y
        chips = _other_chips(x, y)
        mine = [pltpu.make_async_copy(a_refs[a].at[my_chip], o_refs[a].at[my_chip], local_sems.at[a]) for a in range(na)]
        for cp in mine:
            cp.start()
        cps = [pltpu.make_async_remote_copy(src_ref=a_refs[a].at[2 * px + py], dst_ref=o_refs[a].at[my_chip],
                                            send_sem=send_sems.at[a, j], recv_sem=recv_sems.at[a, j],
                                            device_id=(px, py, c), device_id_type=MESH)
               for a in range(na) for j, (px, py) in enumerate(chips)]
        for cp in cps:
            cp.start()
        for a in range(na):
            for j, (px, py) in enumerate(chips):
                pltpu.make_async_remote_copy(src_ref=a_refs[a].at[2 * px + py], dst_ref=o_refs[a].at[2 * px + py],
                                             send_sem=send_sems.at[a, j], recv_sem=recv_sems.at[a, j],
                                             device_id=(px, py, c), device_id_type=MESH).wait_recv()
        for cp in cps:
            cp.wait_send()
        for cp in mine:
            cp.wait()

    return pl.pallas_call(
        body, in_specs=[ANY_SPEC] * na, out_specs=[ANY_SPEC] * na,
        out_shape=[_sds(a.shape, a.dtype) for a in arrs],
        scratch_shapes=[pltpu.SemaphoreType.DMA((na, 3)), pltpu.SemaphoreType.DMA((na, 3)), pltpu.SemaphoreType.DMA((na,))],
        name=name,
    )(*arrs)


def _as_rows(a, lead):
    return a.reshape(a.shape[:lead] + (-1, a.shape[-1]))


def _add_halves(a, other, c_idx, name):
    a3, o2 = _as_rows(a, 1), _as_rows(other, 0)
    rows, cols = o2.shape
    tr = _pick(rows, 256)

    def body(c_ref, a_ref, o_ref, out_ref):
        out_ref[...] = (a_ref[0].astype(F32) + o_ref[...].astype(F32)).astype(out_ref.dtype)

    out = pl.pallas_call(
        body,
        grid_spec=pltpu.PrefetchScalarGridSpec(
            num_scalar_prefetch=1, grid=(rows // tr,),
            in_specs=[pl.BlockSpec((1, tr, cols), lambda i, c_ref: (c_ref[0], i, 0)),
                      pl.BlockSpec((tr, cols), lambda i, c_ref: (i, 0))],
            out_specs=pl.BlockSpec((tr, cols), lambda i, c_ref: (i, 0))),
        out_shape=_sds((rows, cols), a.dtype), compiler_params=_cparams(("parallel",)), name=name,
    )(c_idx, a3, o2)
    return out.reshape(other.shape)


def _all_reduce_small(v, name):
    r = v.shape[0]

    def body(v_ref, o_ref, slots, send_sems, recv_sems):
        x, y, c = _mesh_pos()
        me = 4 * x + 2 * y + c
        slots[me] = v_ref[...]
        cps = []
        for k in range(1, N_DEV):
            px = 1 - x if k & 4 else x
            py = 1 - y if k & 2 else y
            pc = 1 - c if k & 1 else c
            cps.append(pltpu.make_async_remote_copy(src_ref=v_ref, dst_ref=slots.at[me], send_sem=send_sems.at[k - 1],
                                                    recv_sem=recv_sems.at[k - 1], device_id=(px, py, pc), device_id_type=MESH))
        for cp in cps:
            cp.start()
        for cp in cps:
            cp.wait()
        acc = slots[0]
        for d in range(1, N_DEV):
            acc = acc + slots[d]
        o_ref[...] = acc

    vm = pl.BlockSpec(memory_space=pltpu.VMEM)
    return pl.pallas_call(
        body, in_specs=[vm], out_specs=vm, out_shape=_sds((r, LANES), F32),
        scratch_shapes=[pltpu.VMEM((N_DEV, r, LANES), F32), pltpu.SemaphoreType.DMA((N_DEV - 1,)),
                        pltpu.SemaphoreType.DMA((N_DEV - 1,))],
        compiler_params=pltpu.CompilerParams(vmem_limit_bytes=VMEM_LIMIT_BYTES), name=name,
    )(v)


def _adamw(w, g_slots, m, v, name):
    depth, rows, cols = w.shape
    ns = g_slots.shape[0]
    tr = _pick(rows, 256 if cols <= 1024 else 128)

    def body(w_ref, g_ref, m_ref, v_ref, go_ref, d_ref, mo_ref, vo_ref):
        g = g_ref[0, 0].astype(F32)
        for i in range(1, ns):
            g = g + g_ref[i, 0].astype(F32)
        m_new = ADAM_B1 * m_ref[0] + (1.0 - ADAM_B1) * g
        v_new = ADAM_B2 * v_ref[0] + (1.0 - ADAM_B2) * (g * g)
        m_hat = m_new / (1.0 - ADAM_B1 ** ADAM_STEP)
        v_hat = v_new / (1.0 - ADAM_B2 ** ADAM_STEP)
        go_ref[0] = g
        d_ref[0] = -ADAM_LR * (m_hat / (jnp.sqrt(v_hat) + ADAM_EPS) + ADAM_WD * w_ref[0])
        mo_ref[0] = m_new
        vo_ref[0] = v_new

    blk = pl.BlockSpec((1, tr, cols), lambda l, i: (l, i, 0))
    return pl.pallas_call(
        body, grid=(depth, rows // tr),
        in_specs=[blk, pl.BlockSpec((ns, 1, tr, cols), lambda l, i: (0, l, i, 0)), blk, blk],
        out_specs=[blk] * 4, out_shape=[_sds(w.shape, F32)] * 4,
        compiler_params=_cparams(("parallel", "parallel")), name=name,
    )(w, g_slots, m, v)


_ARG_NAMES = (['x', 'mem', 'positions'] + WEIGHTS + ['loss_target'] + ['m_' + n for n in WEIGHTS]
              + ['v_' + n for n in WEIGHTS])


def _gathered_to_full(name, g):
    dev, depth, r, c = g.shape
    if name in COL_SHARDED:
        return jnp.transpose(g, (1, 2, 0, 3)).reshape(depth, r, dev * c)
    return jnp.transpose(g, (1, 0, 2, 3)).reshape(depth, dev * r, c)


def _full_to_scatter(name, g):
    depth = g.shape[0]
    if name in COL_SHARDED:
        r, c = g.shape[1], g.shape[2] // N_DEV
        return jnp.transpose(g.reshape(depth, r, 4, 2, c), (3, 2, 0, 1, 4))
    r, c = g.shape[1] // N_DEV, g.shape[2]
    return jnp.transpose(g.reshape(depth, 4, 2, r, c), (2, 1, 0, 3, 4))


def _blocks_to_scatter(blocks):
    return jnp.stack([jnp.stack([jnp.stack([layer[2 * chip + core] for layer in blocks]) for chip in range(4)])
                      for core in range(2)])


PACK_TILE = 8 * LANES


def _pack_rows(parts):
    blocks = []
    for part in parts:
        flat = part.reshape(-1)
        pad = (-flat.shape[0]) % PACK_TILE
        blocks.append((jnp.pad(flat, (0, pad)) if pad else flat).reshape(-1, LANES))
    return jnp.concatenate(blocks, axis=0)


def _unpack_rows(packed, shapes):
    out, off = [], 0
    for shp in shapes:
        n = int(np.prod(shp))
        rows = -(-n // PACK_TILE) * 8
        out.append(packed[off:off + rows].reshape(-1)[:n].reshape(shp))
        off += rows
    return out


def kernel(x, mem, positions, norm_mix, w_in, b_gate, conv_w, conv_b, dt_bias, a_log, d_skip, ssm_norm, w_br_ret, w_br_ssm, w_out, norm_xa, norm_mem, xa_wq, xa_wkv, xa_wo, norm_mlp, mlp_w1, mlp_w2, norm_final, loss_target, m_norm_mix, m_w_in, m_b_gate, m_conv_w, m_conv_b, m_dt_bias, m_a_log, m_d_skip, m_ssm_norm, m_w_br_ret, m_w_br_ssm, m_w_out, m_norm_xa, m_norm_mem, m_xa_wq, m_xa_wkv, m_xa_wo, m_norm_mlp, m_mlp_w1, m_mlp_w2, m_norm_final, v_norm_mix, v_w_in, v_b_gate, v_conv_w, v_conv_b, v_dt_bias, v_a_log, v_d_skip, v_ssm_norm, v_w_br_ret, v_w_br_ssm, v_w_out, v_norm_xa, v_norm_mem, v_xa_wq, v_xa_wkv, v_xa_wo, v_norm_mlp, v_mlp_w1, v_mlp_w2, v_norm_final):
    d = dict(zip(_ARG_NAMES, (x, mem, positions, norm_mix, w_in, b_gate, conv_w, conv_b, dt_bias, a_log, d_skip, ssm_norm, w_br_ret, w_br_ssm, w_out, norm_xa, norm_mem, xa_wq, xa_wkv, xa_wo, norm_mlp, mlp_w1, mlp_w2, norm_final, loss_target, m_norm_mix, m_w_in, m_b_gate, m_conv_w, m_conv_b, m_dt_bias, m_a_log, m_d_skip, m_ssm_norm, m_w_br_ret, m_w_br_ssm, m_w_out, m_norm_xa, m_norm_mem, m_xa_wq, m_xa_wkv, m_xa_wo, m_norm_mlp, m_mlp_w1, m_mlp_w2, m_norm_final, v_norm_mix, v_w_in, v_b_gate, v_conv_w, v_conv_b, v_dt_bias, v_a_log, v_d_skip, v_ssm_norm, v_w_br_ret, v_w_br_ssm, v_w_out, v_norm_xa, v_norm_mem, v_xa_wq, v_xa_wkv, v_xa_wo, v_norm_mlp, v_mlp_w1, v_mlp_w2, v_norm_final)))
    blocks = [d[k] if k == 'conv_w' else d[k].astype(MXU_DTYPE) for k in SHARDED]
    gathered = _all_gather(blocks, "all_gather_weights")
    w = {k: d[k] for k in SMALL}
    for k, g in zip(SHARDED, gathered):
        w[k] = g if k == 'w_in' else _gathered_to_full(k, g)
    loss, grad_x, grads = _local_step(d['x'][0], d['mem'][0], d['positions'][0], w, d['loss_target'][0])
    by_core = [_blocks_to_scatter(grads[k]) if k == 'w_in' else _full_to_scatter(k, jnp.stack(grads[k])) for k in SHARDED]
    from_sibling = _exchange_cores(by_core, "grad_exchange_cores")
    c_idx = lax.axis_index("c").astype(jnp.int32).reshape(1)
    chip_sums = [_add_halves(a, o, c_idx, f"grad_add_cores_{k}") for k, a, o in zip(SHARDED, by_core, from_sibling)]
    by_chip = _exchange_chips(chip_sums, "grad_exchange_chips")
    small_g = [grads[k] if k == 'norm_final' else jnp.stack(grads[k]) for k in SMALL]
    total = _all_reduce_small(_pack_rows([loss] + small_g), "all_reduce_small")
    loss_out = total[0, 0]
    res = {}
    for k, g4 in zip(SHARDED, by_chip):
        res[k] = _adamw(d[k], g4, d['m_' + k], d['v_' + k], f"adamw_{k}")
    small_shapes = [d[k].shape for k in SMALL]
    pk = lambda pre: _pack_rows([d[pre + k] for k in SMALL])
    outs = _adamw(pk('')[None], total[8:][None, None], pk('m_')[None], pk('v_')[None], "adamw_small")
    unpacked = [_unpack_rows(o[0], small_shapes) for o in outs]
    for i, k in enumerate(SMALL):
        res[k] = [unpacked[j][i] for j in range(4)]
    return (loss_out, grad_x[None], *[res[k][0] for k in WEIGHTS], *[res[k][1] for k in WEIGHTS],
            *[res[k][2] for k in WEIGHTS], *[res[k][3] for k in WEIGHTS])
```

```python
import functools

import numpy as np
import jax
import jax.numpy as jnp
from jax import lax
from jax.experimental import pallas as pl
from jax.experimental.pallas import tpu as pltpu

F32 = jnp.float32
MXU_DTYPE = jnp.bfloat16
VMEM_LIMIT_BYTES = 56 * 1024 * 1024
LANES = 128
N_DEV = 8

D_MODEL = 1024
DEPTH = 4
CHUNK = 64
EPS = 1e-6
RET_HEADS, RET_QK_DIM, RET_V_DIM = 4, 128, 256
RET_QK, RET_V = 512, 1024
ROPE_THETA = 10000.0
SSM_INNER, SSM_HEAD_DIM, SSM_HEADS, SSM_GROUPS, SSM_STATE, SSM_CONV = 2048, 64, 32, 8, 128, 4
SSM_BC = 1024
SSM_CONV_DIM = 4096
IN_DIM = 11296
XA_HEADS, XA_HEAD_DIM = 4, 256
D_FF = 4096
ADAM_LR, ADAM_B1, ADAM_B2, ADAM_EPS, ADAM_WD, ADAM_STEP = 0.001, 0.9, 0.999, 1e-08, 0.01, 10

PROJ_W = 11264
COL_Z, COL_XBC, COL_GATES, COL_V, COL_G, COL_Q, COL_K = 0, 2048, 6144, 8192, 9216, 10240, 10752
DT_PAD = 128
N_LTILE = SSM_INNER // LANES

WEIGHTS = ['norm_mix', 'w_in', 'b_gate', 'conv_w', 'conv_b', 'dt_bias', 'a_log', 'd_skip', 'ssm_norm',
           'w_br_ret', 'w_br_ssm', 'w_out', 'norm_xa', 'norm_mem', 'xa_wq', 'xa_wkv', 'xa_wo', 'norm_mlp',
           'mlp_w1', 'mlp_w2', 'norm_final']
COL_SHARDED = ['w_in', 'conv_w', 'xa_wkv', 'mlp_w1']
ROW_SHARDED = ['w_br_ret', 'w_br_ssm', 'w_out', 'xa_wq', 'xa_wo', 'mlp_w2']
SHARDED = COL_SHARDED + ROW_SHARDED
SMALL = [n for n in WEIGHTS if n not in SHARDED]


def _cparams(sem=None):
    return pltpu.CompilerParams(dimension_semantics=sem, vmem_limit_bytes=VMEM_LIMIT_BYTES)


def _sds(shape, dtype):
    return jax.ShapeDtypeStruct(shape, dtype)


def _full(shape):
    nd = len(shape)
    return pl.BlockSpec(shape, lambda *_: (0,) * nd)


_DIMS = {'nn': (((1,), (0,)), ((), ())), 'nt': (((1,), (1,)), ((), ())), 'tn': (((0,), (0,)), ((), ()))}


def _dot(a, b, mode='nn'):
    return lax.dot_general(a.astype(MXU_DTYPE), b.astype(MXU_DTYPE), _DIMS[mode], preferred_element_type=F32)


@functools.partial(jax.custom_vjp, nondiff_argnums=(2,))
def _mm(a, b, mode):
    return _dot(a, b, mode)


def _mm_fwd(a, b, mode):
    return _dot(a, b, mode), (a, b)


def _mm_bwd(mode, res, g):
    a, b = res
    if mode == 'nn':
        return _dot(g, b, 'nt'), _dot(a, g, 'tn')
    if mode == 'nt':
        return _dot(g, b, 'nn'), _dot(g, a, 'tn')
    return _dot(b, g, 'nt'), _dot(a, g, 'nn')


_mm.defvjp(_mm_fwd, _mm_bwd)


def _split3(x):
    hi = x.astype(jnp.bfloat16)
    r1 = x - hi.astype(F32)
    mid = r1.astype(jnp.bfloat16)
    lo = (r1 - mid.astype(F32)).astype(jnp.bfloat16)
    return hi, mid, lo


def _dot_sel(x, c, left=False):
    dims = _DIMS['nn']
    parts = _split3(x)
    if left:
        outs = [lax.dot_general(c, p, dims, preferred_element_type=F32) for p in parts]
    else:
        outs = [lax.dot_general(p, c, dims, preferred_element_type=F32) for p in parts]
    return (outs[0] + outs[1]) + outs[2]


def _silu(x):
    return x * jax.nn.sigmoid(x)


def _softplus(x):
    pos = x > 0.0
    return jnp.where(pos, x, 0.0) + jnp.log1p(jnp.exp(jnp.where(pos, -x, x)))


def _rms(x):
    return x * lax.rsqrt(jnp.mean(x * x, axis=-1, keepdims=True) + EPS)


def _pick(n, pref):
    t = min(n, pref)
    while n % t:
        t //= 2
    return t


MATMUL_TK_MAX = 4096


def _tiles(mode, m, n, k):
    tm, tn = (512, 1024) if mode == 'nt' else (1024, 512)
    tk = k
    while tk > MATMUL_TK_MAX or k % tk or tk % LANES:
        tk -= LANES
    return _pick(m, tm), _pick(n, tn), tk


def _matmul(a, b, mode, name, *, extras=(), epi=None, out_dtypes=(F32,), out_t=None, tiles=None):
    if mode == 'nn':
        (m, k), (k2, n) = a.shape, b.shape
    elif mode == 'nt':
        (m, k), (n, k2) = a.shape, b.shape
    else:
        (k, m), (k2, n) = a.shape, b.shape
    assert k == k2, (a.shape, b.shape, mode)
    tm, tn, tk = tiles or _tiles(mode, m, n, k)
    nk = k // tk
    n_ex, n_out = len(extras), len(out_dtypes)
    out_t = out_t or (False,) * n_out

    def finish(acc, ex_refs, o_refs):
        outs = epi(acc, *[r[...] for r in ex_refs]) if epi is not None else (acc,)
        for o_ref, o, tr in zip(o_refs, outs, out_t):
            o_ref[...] = (o.T if tr else o).astype(o_ref.dtype)

    def body(*refs):
        a_ref, b_ref = refs[0], refs[1]
        ex_refs = refs[2:2 + n_ex]
        o_refs = refs[2 + n_ex:2 + n_ex + n_out]
        if nk == 1:
            finish(_dot(a_ref[...], b_ref[...], mode), ex_refs, o_refs)
            return
        acc_ref = refs[-1]
        kk = pl.program_id(2)

        @pl.when(kk == 0)
        def _():
            acc_ref[...] = jnp.zeros_like(acc_ref)

        acc_ref[...] += _dot(a_ref[...], b_ref[...], mode)

        @pl.when(kk == nk - 1)
        def _():
            finish(acc_ref[...], ex_refs, o_refs)

    if mode == 'nn':
        a_spec = pl.BlockSpec((tm, tk), lambda i, j, kk: (i, kk))
        b_spec = pl.BlockSpec((tk, tn), lambda i, j, kk: (kk, j))
    elif mode == 'nt':
        a_spec = pl.BlockSpec((tm, tk), lambda i, j, kk: (i, kk))
        b_spec = pl.BlockSpec((tn, tk), lambda i, j, kk: (j, kk))
    else:
        a_spec = pl.BlockSpec((tk, tm), lambda i, j, kk: (kk, i))
        b_spec = pl.BlockSpec((tk, tn), lambda i, j, kk: (kk, j))
    mn_spec = pl.BlockSpec((tm, tn), lambda i, j, kk: (i, j))
    nm_spec = pl.BlockSpec((tn, tm), lambda i, j, kk: (j, i))
    outs = pl.pallas_call(
        body, grid=(m // tm, n // tn, nk),
        in_specs=[a_spec, b_spec] + [mn_spec] * n_ex,
        out_specs=[nm_spec if tr else mn_spec for tr in out_t],
        out_shape=[_sds((n, m) if tr else (m, n), dt) for dt, tr in zip(out_dtypes, out_t)],
        scratch_shapes=[pltpu.VMEM((tm, tn), F32)] if nk > 1 else [],
        compiler_params=_cparams(("parallel", "parallel", "arbitrary")),
        name=name,
    )(a, b, *extras)
    return outs[0] if n_out == 1 else outs


def _epi_add(acc, r):
    return (acc + r,)


PIECE_TK = 1024


def _matmul_nt_pieces(pieces, b, name, *, extras=(), epi=None, out_dtypes=(F32,)):
    m, n = pieces[0].shape[0], b.shape[0]
    tm, tn, tk = _pick(m, 512), _pick(n, 1024), PIECE_TK
    steps = [pc.shape[1] // tk for pc in pieces]
    starts = [sum(steps[:i]) for i in range(len(pieces))]
    nk = sum(steps)
    assert b.shape[1] == nk * tk and all(pc.shape[1] % tk == 0 for pc in pieces)
    n_pc, n_ex, n_out = len(pieces), len(extras), len(out_dtypes)

    def body(*refs):
        pc_refs, b_ref = refs[:n_pc], refs[n_pc]
        ex_refs = refs[n_pc + 1:n_pc + 1 + n_ex]
        o_refs = refs[n_pc + 1 + n_ex:n_pc + 1 + n_ex + n_out]
        acc_ref = refs[-1]
        kk = pl.program_id(2)

        @pl.when(kk == 0)
        def _():
            acc_ref[...] = jnp.zeros_like(acc_ref)

        for pc_ref, st, ns in zip(pc_refs, starts, steps):
            @pl.when((kk >= st) & (kk < st + ns))
            def _(pc_ref=pc_ref):
                acc_ref[...] += _dot(pc_ref[...], b_ref[...], 'nt')

        @pl.when(kk == nk - 1)
        def _():
            acc = acc_ref[...]
            outs = epi(acc, *[r[...] for r in ex_refs]) if epi is not None else (acc,)
            for o_ref, o in zip(o_refs, outs):
                o_ref[...] = o.astype(o_ref.dtype)

    pc_specs = [pl.BlockSpec((tm, tk), lambda i, j, kk, st=st, ns=ns: (i, jnp.clip(kk - st, 0, ns - 1)))
                for st, ns in zip(starts, steps)]
    mn_spec = pl.BlockSpec((tm, tn), lambda i, j, kk: (i, j))
    outs = pl.pallas_call(
        body, grid=(m // tm, n // tn, nk),
        in_specs=pc_specs + [pl.BlockSpec((tn, tk), lambda i, j, kk: (j, kk))] + [mn_spec] * n_ex,
        out_specs=[mn_spec] * n_out, out_shape=[_sds((m, n), dt) for dt in out_dtypes],
        scratch_shapes=[pltpu.VMEM((tm, tn), F32)],
        compiler_params=_cparams(("parallel", "parallel", "arbitrary")), name=name,
    )(*pieces, b, *extras)
    return outs[0] if n_out == 1 else outs


def _rmsnorm_fn(x, w):
    return _rms(x) * w


def _rmsnorm(x, w, name):
    s, d = x.shape
    t = _pick(s, 512)

    def body(x_ref, w_ref, o_ref, ot_ref):
        y = _rmsnorm_fn(x_ref[...], w_ref[...])
        o_ref[...] = y.astype(o_ref.dtype)
        ot_ref[...] = y.T.astype(ot_ref.dtype)

    return pl.pallas_call(
        body, grid=(s // t,),
        in_specs=[pl.BlockSpec((t, d), lambda i: (i, 0)), _full((1, d))],
        out_specs=[pl.BlockSpec((t, d), lambda i: (i, 0)), pl.BlockSpec((d, t), lambda i: (0, i))],
        out_shape=[_sds((s, d), MXU_DTYPE), _sds((d, s), MXU_DTYPE)],
        compiler_params=_cparams(("parallel",)), name=name,
    )(x, w.reshape(1, d))


def _rmsnorm_bwd(x, w, du, dres, name):
    s, d = x.shape
    t = _pick(s, 512)
    has_res = dres is not None

    def body(*refs):
        if has_res:
            x_ref, w_ref, du_ref, dres_ref, dx_ref, dxb_ref, dw_ref = refs
        else:
            x_ref, w_ref, du_ref, dx_ref, dxb_ref, dw_ref = refs
        _, vjp = jax.vjp(_rmsnorm_fn, x_ref[...], w_ref[...])
        dx, dw = vjp(du_ref[...])
        dx = dx + dres_ref[...] if has_res else dx
        dx_ref[...] = dx
        dxb_ref[...] = dx.astype(dxb_ref.dtype)

        @pl.when(pl.program_id(0) == 0)
        def _():
            dw_ref[...] = jnp.zeros_like(dw_ref)

        dw_ref[...] += dw

    row = pl.BlockSpec((t, d), lambda i: (i, 0))
    return pl.pallas_call(
        body, grid=(s // t,),
        in_specs=[row, _full((1, d)), row] + ([row] if has_res else []),
        out_specs=[row, row, _full((1, d))],
        out_shape=[_sds((s, d), F32), _sds((s, d), MXU_DTYPE), _sds((1, d), F32)],
        compiler_params=_cparams(("arbitrary",)), name=name,
    )(x, w.reshape(1, d), du, *([dres] if has_res else []))


CONV_CW = 2048
CONV_HALO = 8


def _shifted(cat):
    return [cat] + [pltpu.roll(cat, sft, axis=0) for sft in (1, 2, 3)]


def _conv_taps(shifted, w, n_rows, off):
    acc = shifted[0][off:off + n_rows, :] * w[3:4, :]
    for sft in (1, 2, 3):
        acc = acc + shifted[sft][off:off + n_rows, :] * w[3 - sft:4 - sft, :]
    return acc


def _conv_fwd(proj, conv_w, conv_b, name):
    s = proj.shape[0]
    tr = _pick(s, 256)
    hb = tr // CONV_HALO
    col0 = COL_XBC // CONV_CW

    def body(prev_ref, x_ref, w_ref, b_ref, o_ref):
        i = pl.program_id(1)
        prev = jnp.where(i == 0, 0.0, prev_ref[...])
        cat = jnp.concatenate([prev, x_ref[...]], axis=0)
        o_ref[...] = _silu(_conv_taps(_shifted(cat), w_ref[...], tr, CONV_HALO) + b_ref[...])

    return pl.pallas_call(
        body, grid=(SSM_CONV_DIM // CONV_CW, s // tr),
        in_specs=[pl.BlockSpec((CONV_HALO, CONV_CW), lambda j, i: (jnp.maximum(i * hb - 1, 0), j + col0)),
                  pl.BlockSpec((tr, CONV_CW), lambda j, i: (i, j + col0)),
                  pl.BlockSpec((SSM_CONV, CONV_CW), lambda j, i: (0, j)),
                  pl.BlockSpec((1, CONV_CW), lambda j, i: (0, j))],
        out_specs=pl.BlockSpec((tr, CONV_CW), lambda j, i: (i, j)),
        out_shape=_sds((s, SSM_CONV_DIM), F32),
        compiler_params=_cparams(("parallel", "parallel")), name=name,
    )(proj, proj, conv_w, conv_b.reshape(1, SSM_CONV_DIM))


def _conv_bwd(proj, conv_w, conv_b, dact, name):
    s = proj.shape[0]
    tr = _pick(s, 256)
    hb = tr // CONV_HALO
    nb = s // CONV_HALO
    nt = s // tr
    col0 = COL_XBC // CONV_CW
    h = CONV_HALO

    def body(prev_ref, x_ref, next_ref, w_ref, b_ref, da_ref, dan_ref, dx_ref, dw_ref, db_ref):
        i = pl.program_id(1)
        w = w_ref[...]
        prev = jnp.where(i == 0, 0.0, prev_ref[...])
        cat = jnp.concatenate([prev, x_ref[...], next_ref[...]], axis=0)
        shifted = _shifted(cat)
        pre = _conv_taps(shifted, w, tr + h, h) + b_ref[...]
        dact_n = jnp.where(i == nt - 1, 0.0, dan_ref[...])
        dact_ext = jnp.concatenate([da_ref[...], dact_n], axis=0)
        sg = jax.nn.sigmoid(pre)
        dpre = dact_ext * (sg * (1.0 + pre * (1.0 - sg)))
        dx = dpre[:tr, :] * w[3:4, :]
        for sft in (1, 2, 3):
            dx = dx + pltpu.roll(dpre, tr + h - sft, axis=0)[:tr, :] * w[3 - sft:4 - sft, :]
        dx_ref[...] = dx.astype(dx_ref.dtype)

        @pl.when(i == 0)
        def _():
            dw_ref[...] = jnp.zeros_like(dw_ref)
            db_ref[...] = jnp.zeros_like(db_ref)

        dp = dpre[:tr, :]
        db_ref[...] += jnp.sum(dp, axis=0, keepdims=True)
        for r, sft in enumerate((3, 2, 1, 0)):
            dw_ref[r:r + 1, :] += jnp.sum(dp * shifted[sft][h:h + tr, :], axis=0, keepdims=True)

    return pl.pallas_call(
        body, grid=(SSM_CONV_DIM // CONV_CW, nt),
        in_specs=[pl.BlockSpec((h, CONV_CW), lambda j, i: (jnp.maximum(i * hb - 1, 0), j + col0)),
                  pl.BlockSpec((tr, CONV_CW), lambda j, i: (i, j + col0)),
                  pl.BlockSpec((h, CONV_CW), lambda j, i: (jnp.minimum((i + 1) * hb, nb - 1), j + col0)),
                  pl.BlockSpec((SSM_CONV, CONV_CW), lambda j, i: (0, j)),
                  pl.BlockSpec((1, CONV_CW), lambda j, i: (0, j)),
                  pl.BlockSpec((tr, CONV_CW), lambda j, i: (i, j)),
                  pl.BlockSpec((h, CONV_CW), lambda j, i: (jnp.minimum((i + 1) * hb, nb - 1), j))],
        out_specs=[pl.BlockSpec((tr, CONV_CW), lambda j, i: (i, j)),
                   pl.BlockSpec((SSM_CONV, CONV_CW), lambda j, i: (0, j)),
                   pl.BlockSpec((1, CONV_CW), lambda j, i: (0, j))],
        out_shape=[_sds((s, SSM_CONV_DIM), MXU_DTYPE), _sds((SSM_CONV, SSM_CONV_DIM), F32), _sds((1, SSM_CONV_DIM), F32)],
        compiler_params=_cparams(("parallel", "arbitrary")), name=name,
    )(proj, proj, proj, conv_w, conv_b.reshape(1, SSM_CONV_DIM), dact, dact)


def _scan_tables():
    idx = np.arange(CHUNK, dtype=np.float32)
    lg = np.log1p(-(2.0 ** (-5.0 - np.arange(RET_HEADS, dtype=np.float32)))).astype(np.float32)
    rel = np.abs(idx[:, None] - idx[None, :])
    r_intra = np.exp(lg[:, None, None] * rel).astype(np.float32)
    qd = np.exp(lg[None, :] * (idx[:, None] + 1.0)).astype(np.float32)
    kd = np.exp(lg[None, :] * (CHUNK - 1.0 - idx[:, None])).astype(np.float32)
    gam = [float(v) for v in np.exp(lg * CHUNK).astype(np.float32)]
    qd_e = np.repeat(qd, RET_QK_DIM, axis=1)
    kd_e = np.repeat(kd, RET_QK_DIM, axis=1)
    e = np.zeros((DT_PAD, SSM_INNER), np.float32)
    for hh in range(SSM_HEADS):
        e[hh, hh * SSM_HEAD_DIM:(hh + 1) * SSM_HEAD_DIM] = 1.0
    tri = np.tril(np.ones((CHUNK, CHUNK), np.float32))
    eye2 = np.concatenate([np.eye(CHUNK, dtype=np.float32)] * 2, axis=1)
    bdm = np.kron(np.eye(2, dtype=np.float32), np.ones((CHUNK, CHUNK), np.float32))
    last = np.zeros((CHUNK, LANES), np.float32)
    last[CHUNK - 1, :] = 1.0
    f32c = [jnp.asarray(c) for c in (r_intra, qd_e, kd_e, eye2, bdm, last)]
    sel = [jnp.asarray(c, jnp.bfloat16) for c in (e, e.T.copy(), tri, tri.T.copy())]
    return f32c + sel, gam


def _rope(t, cos2, sin2):
    return t * cos2 + pltpu.roll(t, RET_QK_DIM // 2, axis=1) * sin2


def _rope_t(d, cos2, sin2):
    return d * cos2 + pltpu.roll(d * sin2, RET_QK_DIM // 2, axis=1)


def _ret_step(q, k, v, st, r_intra, qd, kd, gamma):
    k = k * (RET_QK_DIM ** -0.5)
    sc = _mm(q, k, 'nt') * r_intra
    y = _mm(sc, v, 'nn') + _mm(q * qd, st, 'nn')
    st_new = st * gamma + _mm(k * kd, v, 'tn')
    return y, st_new


def _ssd_heads(dtraw, dtb, a_c, tri):
    dt = _softplus(dtraw + dtb)
    return dt, _dot_sel(dt * a_c, tri, left=True)


def _ssd_group(dte0, dte1, cum0, cum1, xs0, xs1, bm, cm, ht0, ht1, eye2, bdm, last):
    cbp = _mm(cm, jnp.concatenate([bm, bm], axis=0), 'nt')
    outs = []
    for dte, cum, xs, ht in ((dte0, cum0, xs0, ht0), (dte1, cum1, xs1, ht1)):
        r = jnp.sum(cum * eye2, axis=0, keepdims=True)
        dlt = cum - r
        seg = jnp.exp(jnp.where(dlt > 0.0, -dlt, dlt))
        xdt = xs * dte
        bd = jnp.concatenate([xdt, xdt], axis=0) * bdm
        clast = jnp.sum(cum * last, axis=0, keepdims=True)
        y = _mm(cbp * seg, bd, 'nn') + jnp.exp(cum) * _mm(cm, ht, 'nn')
        ht_new = jnp.exp(clast) * ht + _mm(bm, xdt * jnp.exp(clast - cum), 'tn')
        outs += [y, ht_new]
    return tuple(outs)


def _scan_in_specs(nc, rev):
    ch = (lambda c: nc - 1 - c) if rev else (lambda c: c)
    col = lambda w, blk: pl.BlockSpec((CHUNK, w), lambda c: (ch(c), blk))
    return [col(RET_QK, COL_Q // RET_QK), col(RET_QK, COL_K // RET_QK), col(RET_V, COL_V // RET_V),
            col(SSM_INNER, 0), col(SSM_BC, 2), col(SSM_BC, 3),
            col(DT_PAD, 0), col(LANES, 0), col(LANES, 0)]


def _const_specs(consts):
    return [_full(c.shape) for c in consts]


def _tile(t):
    return slice(t * LANES, (t + 1) * LANES)


def _with_rider(core, n_in, n_out, n_scratch, rider, n_steps):
    if rider is None:
        return core
    na, nrs = rider.n, len(rider.scratch)

    def body(*refs):
        ci, ri = refs[:n_in], refs[n_in:n_in + na]
        co, ro = refs[n_in + na:n_in + na + n_out], refs[n_in + na + n_out:n_in + 2 * na + n_out]
        sc = refs[n_in + 2 * na + n_out:]
        cs, rs = sc[:n_scratch], sc[n_scratch:]
        assert len(rs) == nrs

        @pl.when(pl.program_id(0) == 0)
        def _():
            rider.start(ri, ro, rs)

        core(*ci, *co, *cs)

        @pl.when(pl.program_id(0) == n_steps - 1)
        def _():
            rider.finish(ri, ro, rs)

    return body


def _rider_args(rider):
    if rider is None:
        return [], [], [], [], []
    return list(rider.arrs), [ANY_SPEC] * rider.n, [ANY_SPEC] * rider.n, list(rider.out_shape), list(rider.scratch)


def _scan_fwd(proj, xbc, dtraw, cos2, sin2, a_c, dtb, name, rider=None):
    s = proj.shape[0]
    nc = s // CHUNK
    consts, gam = _scan_tables()
    r_arrs, r_in, r_out, r_shape, r_scratch = _rider_args(rider)

    def body(q_ref, k_ref, v_ref, xs_ref, bm_ref, cm_ref, dt_ref, cos_ref, sin_ref, ac_ref, dtb_ref,
             ri_ref, qd_ref, kd_ref, eye_ref, bdm_ref, last_ref, e_ref, et_ref, tri_ref, trit_ref,
             yr_ref, ys_ref, sh_ref, hh_ref, st_sc, ht_sc):
        @pl.when(pl.program_id(0) == 0)
        def _():
            st_sc[...] = jnp.zeros_like(st_sc)
            ht_sc[...] = jnp.zeros_like(ht_sc)

        sh_ref[0] = st_sc[...]
        hh_ref[0] = ht_sc[...]
        cos2, sin2 = cos_ref[...], sin_ref[...]
        st_new = []
        for h in range(RET_HEADS):
            ql = slice(h * RET_QK_DIM, (h + 1) * RET_QK_DIM)
            vl = slice(h * RET_V_DIM, (h + 1) * RET_V_DIM)
            y, st_h = _ret_step(_rope(q_ref[:, ql], cos2, sin2), _rope(k_ref[:, ql], cos2, sin2), v_ref[:, vl],
                                st_sc[ql, :], ri_ref[h], qd_ref[:, ql], kd_ref[:, ql], gam[h])
            yr_ref[:, vl] = y
            st_new.append(st_h)
        dt, cum_c = _ssd_heads(dt_ref[...], dtb_ref[...], ac_ref[...], tri_ref[...])
        both = jnp.concatenate([dt, cum_c], axis=0)
        eye2, bdm, last = eye_ref[...], bdm_ref[...], last_ref[...]
        ht_new = []
        for g in range(SSM_GROUPS):
            t0, t1 = 2 * g, 2 * g + 1
            e0, e1 = _dot_sel(both, e_ref[:, _tile(t0)]), _dot_sel(both, e_ref[:, _tile(t1)])
            y0, h0, y1, h1 = _ssd_group(e0[:CHUNK], e1[:CHUNK], e0[CHUNK:], e1[CHUNK:],
                                        xs_ref[:, _tile(t0)], xs_ref[:, _tile(t1)], bm_ref[:, _tile(g)],
                                        cm_ref[:, _tile(g)], ht_sc[:, _tile(t0)], ht_sc[:, _tile(t1)], eye2, bdm, last)
            ys_ref[:, _tile(t0)] = y0
            ys_ref[:, _tile(t1)] = y1
            ht_new += [h0, h1]
        for h in range(RET_HEADS):
            st_sc[h * RET_QK_DIM:(h + 1) * RET_QK_DIM, :] = st_new[h]
        for t in range(N_LTILE):
            ht_sc[:, _tile(t)] = ht_new[t]

    in_specs = _scan_in_specs(nc, False) + [_full((1, DT_PAD)), _full((1, DT_PAD))] + _const_specs(consts)
    return pl.pallas_call(
        _with_rider(body, len(in_specs), 4, 2, rider, nc), grid=(nc,),
        in_specs=in_specs + r_in,
        out_specs=[pl.BlockSpec((CHUNK, RET_V), lambda c: (c, 0)),
                   pl.BlockSpec((CHUNK, SSM_INNER), lambda c: (c, 0)),
                   pl.BlockSpec((1, RET_QK, RET_V_DIM), lambda c: (c, 0, 0)),
                   pl.BlockSpec((1, SSM_STATE, SSM_INNER), lambda c: (c, 0, 0))] + r_out,
        out_shape=[_sds((s, RET_V), F32), _sds((s, SSM_INNER), F32),
                   _sds((nc, RET_QK, RET_V_DIM), F32), _sds((nc, SSM_STATE, SSM_INNER), F32)] + r_shape,
        scratch_shapes=[pltpu.VMEM((RET_QK, RET_V_DIM), F32), pltpu.VMEM((SSM_STATE, SSM_INNER), F32)] + r_scratch,
        compiler_params=_cparams(("arbitrary",)), name=name,
    )(proj, proj, proj, xbc, xbc, xbc, dtraw, cos2, sin2, a_c, dtb, *consts, *r_arrs)


def _scan_bwd(proj, xbc, dtraw, cos2, sin2, a_c, dtb, s_hist, h_hist, dyr, dys, dxs_skip, name, rider=None):
    s = proj.shape[0]
    nc = s // CHUNK
    consts, gam = _scan_tables()
    rv = lambda c: nc - 1 - c
    r_arrs, r_in, r_out, r_shape, r_scratch = _rider_args(rider)

    def body(q_ref, k_ref, v_ref, xs_ref, bm_ref, cm_ref, dt_ref, cos_ref, sin_ref, ac_ref, dtb_ref,
             ri_ref, qd_ref, kd_ref, eye_ref, bdm_ref, last_ref, e_ref, et_ref, tri_ref, trit_ref,
             sh_ref, hh_ref, dyr_ref, dys_ref, dsk_ref,
             dqk_ref, dv_ref, dxbc_ref, ddt_ref, dac_ref, ddtb_ref, dst_sc, dht_sc):
        @pl.when(pl.program_id(0) == 0)
        def _():
            dst_sc[...] = jnp.zeros_like(dst_sc)
            dht_sc[...] = jnp.zeros_like(dht_sc)
            dac_ref[...] = jnp.zeros_like(dac_ref)
            ddtb_ref[...] = jnp.zeros_like(ddtb_ref)

        cos2, sin2 = cos_ref[...], sin_ref[...]
        dst_new = []
        for h in range(RET_HEADS):
            ql = slice(h * RET_QK_DIM, (h + 1) * RET_QK_DIM)
            vl = slice(h * RET_V_DIM, (h + 1) * RET_V_DIM)
            step = functools.partial(_ret_step, r_intra=ri_ref[h], qd=qd_ref[:, ql], kd=kd_ref[:, ql], gamma=gam[h])
            _, vjp = jax.vjp(step, _rope(q_ref[:, ql], cos2, sin2), _rope(k_ref[:, ql], cos2, sin2), v_ref[:, vl],
                             sh_ref[0, ql, :])
            dq, dk, dv, dst = vjp((dyr_ref[:, vl], dst_sc[ql, :]))
            dqk_ref[:, ql] = _rope_t(dq, cos2, sin2).astype(dqk_ref.dtype)
            dqk_ref[:, slice(RET_QK + ql.start, RET_QK + ql.stop)] = _rope_t(dk, cos2, sin2).astype(dqk_ref.dtype)
            dv_ref[:, vl] = dv.astype(dv_ref.dtype)
            dst_new.append(dst)
        dtraw_v, dtb_v, a_c, tri = dt_ref[...], dtb_ref[...], ac_ref[...], tri_ref[...]
        dt, cum_c = _ssd_heads(dtraw_v, dtb_v, a_c, tri)
        both = jnp.concatenate([dt, cum_c], axis=0)
        eye2, bdm, last = eye_ref[...], bdm_ref[...], last_ref[...]
        group = functools.partial(_ssd_group, eye2=eye2, bdm=bdm, last=last)
        d_both = jnp.zeros((2 * CHUNK, LANES), F32)
        dht_new = []
        for g in range(SSM_GROUPS):
            t0, t1 = 2 * g, 2 * g + 1
            e0, e1 = _dot_sel(both, e_ref[:, _tile(t0)]), _dot_sel(both, e_ref[:, _tile(t1)])
            _, vjp = jax.vjp(group, e0[:CHUNK], e1[:CHUNK], e0[CHUNK:], e1[CHUNK:],
                             xs_ref[:, _tile(t0)], xs_ref[:, _tile(t1)], bm_ref[:, _tile(g)], cm_ref[:, _tile(g)],
                             hh_ref[0, :, _tile(t0)], hh_ref[0, :, _tile(t1)])
            (d_dte0, d_dte1, d_cum0, d_cum1, d_xs0, d_xs1, d_bm, d_cm, d_ht0, d_ht1) = vjp(
                (dys_ref[:, _tile(t0)], dht_sc[:, _tile(t0)], dys_ref[:, _tile(t1)], dht_sc[:, _tile(t1)]))
            d_both = d_both + _dot_sel(jnp.concatenate([d_dte0, d_cum0], axis=0), et_ref[_tile(t0), :])
            d_both = d_both + _dot_sel(jnp.concatenate([d_dte1, d_cum1], axis=0), et_ref[_tile(t1), :])
            dxbc_ref[:, _tile(t0)] = d_xs0 + dsk_ref[:, _tile(t0)]
            dxbc_ref[:, _tile(t1)] = d_xs1 + dsk_ref[:, _tile(t1)]
            dxbc_ref[:, _tile(N_LTILE + g)] = d_bm
            dxbc_ref[:, _tile(N_LTILE + SSM_GROUPS + g)] = d_cm
            dht_new += [d_ht0, d_ht1]
        d_da = _dot_sel(d_both[CHUNK:], trit_ref[...], left=True)
        d_dt = d_both[:CHUNK] + d_da * a_c
        d_pre = d_dt * jax.nn.sigmoid(dtraw_v + dtb_v)
        ddt_ref[...] = d_pre
        ddtb_ref[...] += jnp.sum(d_pre, axis=0, keepdims=True)
        dac_ref[...] += jnp.sum(d_da * dt, axis=0, keepdims=True)
        for h in range(RET_HEADS):
            dst_sc[h * RET_QK_DIM:(h + 1) * RET_QK_DIM, :] = dst_new[h]
        for t in range(N_LTILE):
            dht_sc[:, _tile(t)] = dht_new[t]

    in_specs = (_scan_in_specs(nc, True) + [_full((1, DT_PAD)), _full((1, DT_PAD))] + _const_specs(consts)
                + [pl.BlockSpec((1, RET_QK, RET_V_DIM), lambda c: (rv(c), 0, 0)),
                   pl.BlockSpec((1, SSM_STATE, SSM_INNER), lambda c: (rv(c), 0, 0)),
                   pl.BlockSpec((CHUNK, RET_V), lambda c: (rv(c), 0)),
                   pl.BlockSpec((CHUNK, SSM_INNER), lambda c: (rv(c), 0)),
                   pl.BlockSpec((CHUNK, SSM_INNER), lambda c: (rv(c), 0))])
    return pl.pallas_call(
        _with_rider(body, len(in_specs), 6, 2, rider, nc), grid=(nc,),
        in_specs=in_specs + r_in,
        out_specs=[pl.BlockSpec((CHUNK, 2 * RET_QK), lambda c: (rv(c), 0)),
                   pl.BlockSpec((CHUNK, RET_V), lambda c: (rv(c), 0)),
                   pl.BlockSpec((CHUNK, SSM_CONV_DIM), lambda c: (rv(c), 0)),
                   pl.BlockSpec((CHUNK, DT_PAD), lambda c: (rv(c), 0)),
                   _full((1, DT_PAD)), _full((1, DT_PAD))] + r_out,
        out_shape=[_sds((s, 2 * RET_QK), MXU_DTYPE), _sds((s, RET_V), MXU_DTYPE),
                   _sds((s, SSM_CONV_DIM), F32), _sds((s, DT_PAD), F32),
                   _sds((1, DT_PAD), F32), _sds((1, DT_PAD), F32)] + r_shape,
        scratch_shapes=[pltpu.VMEM((RET_QK, RET_V_DIM), F32), pltpu.VMEM((SSM_STATE, SSM_INNER), F32)] + r_scratch,
        compiler_params=_cparams(("arbitrary",)), name=name,
    )(proj, proj, proj, xbc, xbc, xbc, dtraw, cos2, sin2, a_c, dtb, *consts, s_hist, h_hist, dyr, dys, dxs_skip, *r_arrs)


POST_W = 256


def _post_ret(y, g):
    return _rms(y) * _silu(g)


def _post_ssm(y, xs, z, dsk, nw):
    return _rms((y + xs * dsk) * _silu(z)) * nw


def _post_specs(t):
    return [pl.BlockSpec((t, RET_V), lambda i: (i, 0)),
            pl.BlockSpec((t, RET_V), lambda i: (i, COL_G // RET_V)),
            pl.BlockSpec((t, SSM_INNER), lambda i: (i, 0)),
            pl.BlockSpec((t, SSM_INNER), lambda i: (i, 0)),
            pl.BlockSpec((t, SSM_INNER), lambda i: (i, COL_Z // SSM_INNER)),
            _full((1, SSM_INNER)), _full((1, SSM_INNER))]


def _post_fwd(y_ret, proj, y_ssm, xbc, dsk_e, ssm_norm, name):
    s = y_ret.shape[0]
    t = _pick(s, 256)

    def body(yr_ref, g_ref, ys_ref, xs_ref, z_ref, dsk_ref, nw_ref, or_ref, os_ref, ort_ref, ost_ref):
        for h in range(RET_V // POST_W):
            sl = slice(h * POST_W, (h + 1) * POST_W)
            o = _post_ret(yr_ref[:, sl], g_ref[:, sl])
            or_ref[:, sl] = o.astype(or_ref.dtype)
            ort_ref[sl, :] = o.T.astype(ort_ref.dtype)
        for g in range(SSM_INNER // POST_W):
            sl = slice(g * POST_W, (g + 1) * POST_W)
            o = _post_ssm(ys_ref[:, sl], xs_ref[:, sl], z_ref[:, sl], dsk_ref[:, sl], nw_ref[:, sl])
            os_ref[:, sl] = o.astype(os_ref.dtype)
            ost_ref[sl, :] = o.T.astype(ost_ref.dtype)

    return pl.pallas_call(
        body, grid=(s // t,), in_specs=_post_specs(t),
        out_specs=[pl.BlockSpec((t, RET_V), lambda i: (i, 0)), pl.BlockSpec((t, SSM_INNER), lambda i: (i, 0)),
                   pl.BlockSpec((RET_V, t), lambda i: (0, i)), pl.BlockSpec((SSM_INNER, t), lambda i: (0, i))],
        out_shape=[_sds((s, RET_V), MXU_DTYPE), _sds((s, SSM_INNER), MXU_DTYPE),
                   _sds((RET_V, s), MXU_DTYPE), _sds((SSM_INNER, s), MXU_DTYPE)],
        compiler_params=_cparams(("parallel",)), name=name,
    )(y_ret, proj, y_ssm, xbc, proj, dsk_e, ssm_norm.reshape(1, SSM_INNER))


def _post_bwd(y_ret, proj, y_ssm, xbc, dsk_e, ssm_norm, d_or, d_os, name):
    s = y_ret.shape[0]
    t = _pick(s, 256)

    def body(yr_ref, g_ref, ys_ref, xs_ref, z_ref, dsk_ref, nw_ref, dor_ref, dos_ref,
             dyr_ref, dg_ref, dys_ref, dxs_ref, dz_ref, ddsk_ref, dnw_ref):
        @pl.when(pl.program_id(0) == 0)
        def _():
            ddsk_ref[...] = jnp.zeros_like(ddsk_ref)
            dnw_ref[...] = jnp.zeros_like(dnw_ref)

        for h in range(RET_V // POST_W):
            sl = slice(h * POST_W, (h + 1) * POST_W)
            _, vjp = jax.vjp(_post_ret, yr_ref[:, sl], g_ref[:, sl])
            dyr, dg = vjp(dor_ref[:, sl])
            dyr_ref[:, sl] = dyr
            dg_ref[:, sl] = dg.astype(dg_ref.dtype)
        for g in range(SSM_INNER // POST_W):
            sl = slice(g * POST_W, (g + 1) * POST_W)
            _, vjp = jax.vjp(_post_ssm, ys_ref[:, sl], xs_ref[:, sl], z_ref[:, sl], dsk_ref[:, sl], nw_ref[:, sl])
            dy, dxs, dz, ddsk, dnw = vjp(dos_ref[:, sl])
            dys_ref[:, sl] = dy
            dxs_ref[:, sl] = dxs
            dz_ref[:, sl] = dz.astype(dz_ref.dtype)
            ddsk_ref[:, sl] += ddsk
            dnw_ref[:, sl] += dnw

    rowv = pl.BlockSpec((t, RET_V), lambda i: (i, 0))
    rows = pl.BlockSpec((t, SSM_INNER), lambda i: (i, 0))
    return pl.pallas_call(
        body, grid=(s // t,), in_specs=_post_specs(t) + [rowv, rows],
        out_specs=[rowv, rowv, rows, rows, rows, _full((1, SSM_INNER)), _full((1, SSM_INNER))],
        out_shape=[_sds((s, RET_V), F32), _sds((s, RET_V), MXU_DTYPE), _sds((s, SSM_INNER), F32),
                   _sds((s, SSM_INNER), F32), _sds((s, SSM_INNER), MXU_DTYPE),
                   _sds((1, SSM_INNER), F32), _sds((1, SSM_INNER), F32)],
        compiler_params=_cparams(("arbitrary",)), name=name,
    )(y_ret, proj, y_ssm, xbc, proj, dsk_e, ssm_norm.reshape(1, SSM_INNER), d_or, d_os)


def _merge_fn(gr, gs, br, bs, yr, ys):
    return jax.nn.sigmoid(gr + br) * yr + jax.nn.sigmoid(gs + bs) * ys


def _merge_specs(t):
    row = pl.BlockSpec((t, D_MODEL), lambda i: (i, 0))
    return [pl.BlockSpec((t, D_MODEL), lambda i: (i, COL_GATES // D_MODEL)),
            pl.BlockSpec((t, D_MODEL), lambda i: (i, COL_GATES // D_MODEL + 1)),
            pl.BlockSpec((1, D_MODEL), lambda i: (0, 0)), pl.BlockSpec((1, D_MODEL), lambda i: (0, 1)), row, row]


def _merge_fwd(proj, b_gate, br_ret, br_ssm, name):
    s = proj.shape[0]
    t = _pick(s, 512)

    def body(gr_ref, gs_ref, br_ref, bs_ref, yr_ref, ys_ref, o_ref, ot_ref):
        o = _merge_fn(gr_ref[...], gs_ref[...], br_ref[...], bs_ref[...], yr_ref[...], ys_ref[...])
        o_ref[...] = o.astype(o_ref.dtype)
        ot_ref[...] = o.T.astype(ot_ref.dtype)

    bg = b_gate.reshape(1, 2 * D_MODEL)
    return pl.pallas_call(
        body, grid=(s // t,), in_specs=_merge_specs(t),
        out_specs=[pl.BlockSpec((t, D_MODEL), lambda i: (i, 0)), pl.BlockSpec((D_MODEL, t), lambda i: (0, i))],
        out_shape=[_sds((s, D_MODEL), MXU_DTYPE), _sds((D_MODEL, s), MXU_DTYPE)],
        compiler_params=_cparams(("parallel",)), name=name,
    )(proj, proj, bg, bg, br_ret, br_ssm)


def _merge_bwd(proj, b_gate, br_ret, br_ssm, dm, name):
    s = proj.shape[0]
    t = _pick(s, 512)

    def body(gr_ref, gs_ref, br_ref, bs_ref, yr_ref, ys_ref, dm_ref, dgt_ref, db_ref, dyr_ref, dys_ref):
        @pl.when(pl.program_id(0) == 0)
        def _():
            db_ref[...] = jnp.zeros_like(db_ref)

        _, vjp = jax.vjp(_merge_fn, gr_ref[...], gs_ref[...], br_ref[...], bs_ref[...], yr_ref[...], ys_ref[...])
        dgr, dgs, dbr, dbs, dyr, dys = vjp(dm_ref[...])
        dgt_ref[:, :D_MODEL] = dgr.astype(dgt_ref.dtype)
        dgt_ref[:, D_MODEL:] = dgs.astype(dgt_ref.dtype)
        db_ref[:, :D_MODEL] += dbr
        db_ref[:, D_MODEL:] += dbs
        dyr_ref[...] = dyr.astype(dyr_ref.dtype)
        dys_ref[...] = dys.astype(dys_ref.dtype)

    bg = b_gate.reshape(1, 2 * D_MODEL)
    row = pl.BlockSpec((t, D_MODEL), lambda i: (i, 0))
    return pl.pallas_call(
        body, grid=(s // t,), in_specs=_merge_specs(t) + [row],
        out_specs=[pl.BlockSpec((t, 2 * D_MODEL), lambda i: (i, 0)), _full((1, 2 * D_MODEL)), row, row],
        out_shape=[_sds((s, 2 * D_MODEL), MXU_DTYPE), _sds((1, 2 * D_MODEL), F32),
                   _sds((s, D_MODEL), MXU_DTYPE), _sds((s, D_MODEL), MXU_DTYPE)],
        compiler_params=_cparams(("arbitrary",)), name=name,
    )(proj, proj, bg, bg, br_ret, br_ssm, dm)


def _attn_head(q, k, v):
    sc = _mm(q, k, 'nt') * (XA_HEAD_DIM ** -0.5)
    e = jnp.exp(sc - lax.stop_gradient(jnp.max(sc, axis=-1, keepdims=True)))
    p = e / jnp.sum(e, axis=-1, keepdims=True)
    return _mm(p, v, 'nn')


def _attn_fwd(q, kv, name):
    s = q.shape[0]
    m = kv.shape[0]
    t = _pick(s, 512)

    def body(q_ref, kv_ref, o_ref, ot_ref):
        for h in range(XA_HEADS):
            sl = slice(h * XA_HEAD_DIM, (h + 1) * XA_HEAD_DIM)
            vl = slice(D_MODEL + h * XA_HEAD_DIM, D_MODEL + (h + 1) * XA_HEAD_DIM)
            o = _attn_head(q_ref[:, sl], kv_ref[:, sl], kv_ref[:, vl])
            o_ref[:, sl] = o.astype(o_ref.dtype)
            ot_ref[sl, :] = o.T.astype(ot_ref.dtype)

    return pl.pallas_call(
        body, grid=(s // t,),
        in_specs=[pl.BlockSpec((t, D_MODEL), lambda i: (i, 0)), _full((m, 2 * D_MODEL))],
        out_specs=[pl.BlockSpec((t, D_MODEL), lambda i: (i, 0)), pl.BlockSpec((D_MODEL, t), lambda i: (0, i))],
        out_shape=[_sds((s, D_MODEL), MXU_DTYPE), _sds((D_MODEL, s), MXU_DTYPE)],
        compiler_params=_cparams(("parallel",)), name=name,
    )(q, kv)


def _attn_bwd(q, kv, d_o, name):
    s = q.shape[0]
    m = kv.shape[0]
    t = _pick(s, 512)

    def body(q_ref, kv_ref, do_ref, dq_ref, dkv_ref):
        @pl.when(pl.program_id(0) == 0)
        def _():
            dkv_ref[...] = jnp.zeros_like(dkv_ref)

        for h in range(XA_HEADS):
            sl = slice(h * XA_HEAD_DIM, (h + 1) * XA_HEAD_DIM)
            vl = slice(D_MODEL + h * XA_HEAD_DIM, D_MODEL + (h + 1) * XA_HEAD_DIM)
            _, vjp = jax.vjp(_attn_head, q_ref[:, sl], kv_ref[:, sl], kv_ref[:, vl])
            dq, dk, dv = vjp(do_ref[:, sl])
            dq_ref[:, sl] = dq.astype(dq_ref.dtype)
            dkv_ref[:, sl] += dk
            dkv_ref[:, vl] += dv

    row = pl.BlockSpec((t, D_MODEL), lambda i: (i, 0))
    return pl.pallas_call(
        body, grid=(s // t,), in_specs=[row, _full((m, 2 * D_MODEL)), row],
        out_specs=[row, _full((m, 2 * D_MODEL))],
        out_shape=[_sds((s, D_MODEL), MXU_DTYPE), _sds((m, 2 * D_MODEL), F32)],
        compiler_params=_cparams(("arbitrary",)), name=name,
    )(q, kv, d_o)


def _loss_head(x, w, target, name):
    s, d = x.shape
    t = _pick(s, 512)

    def body(x_ref, w_ref, t_ref, loss_ref, dx_ref, dxb_ref, dw_ref):
        @pl.when(pl.program_id(0) == 0)
        def _():
            loss_ref[...] = jnp.zeros_like(loss_ref)
            dw_ref[...] = jnp.zeros_like(dw_ref)

        y, vjp = jax.vjp(_rmsnorm_fn, x_ref[...], w_ref[...])
        err = y - t_ref[...]
        loss_ref[...] += 0.5 * jnp.sum(jnp.sum(err * err, axis=-1, keepdims=True), axis=0, keepdims=True) / d
        dx, dw = vjp(err * (1.0 / d))
        dx_ref[...] = dx
        dxb_ref[...] = dx.astype(dxb_ref.dtype)
        dw_ref[...] += dw

    row = pl.BlockSpec((t, d), lambda i: (i, 0))
    return pl.pallas_call(
        body, grid=(s // t,), in_specs=[row, _full((1, d)), row],
        out_specs=[_full((1, LANES)), row, row, _full((1, d))],
        out_shape=[_sds((1, LANES), F32), _sds((s, d), F32), _sds((s, d), MXU_DTYPE), _sds((1, d), F32)],
        compiler_params=_cparams(("arbitrary",)), name=name,
    )(x, w.reshape(1, d), target)


def _epi_sqrelu(acc):
    r = jnp.maximum(acc, 0.0)
    return acc, r * r, r * r


def _epi_sqrelu_bwd(acc, a):
    return (acc * (2.0 * jnp.maximum(a, 0.0)),)


def _rope_tables(positions):
    inv_freq = ROPE_THETA ** (-jnp.arange(0, RET_QK_DIM, 2, dtype=F32) / RET_QK_DIM)
    ang = positions.astype(F32)[:, None] * inv_freq
    cos, sin = jnp.cos(ang), jnp.sin(ang)
    return jnp.concatenate([cos, cos], axis=1), jnp.concatenate([-sin, sin], axis=1)


W_IN_ORIG = (('q', 0, 512), ('k', 512, 1024), ('v', 1024, 2048), ('g', 2048, 3072), ('z', 3072, 5120),
             ('xbc', 5120, 9216), ('dt', 9216, 9248), ('gates', 9248, 11296))
W_IN_MAIN_ORDER = ('z', 'xbc', 'gates', 'v', 'g', 'q', 'k')
W_IN_SHARD = IN_DIM // N_DEV


def _shard_segments(lo, hi):
    segs = []
    for j in range(lo // W_IN_SHARD, (hi - 1) // W_IN_SHARD + 1):
        segs.append((j, max(lo, j * W_IN_SHARD) - j * W_IN_SHARD, min(hi, (j + 1) * W_IN_SHARD) - j * W_IN_SHARD))
    return segs


def _w_in_from_shards(g):
    rng = {name: (lo, hi) for name, lo, hi in W_IN_ORIG}
    cols = [g[j][:, a:b] for name in W_IN_MAIN_ORDER for j, a, b in _shard_segments(*rng[name])]
    (j, a, b), = _shard_segments(*rng['dt'])
    return jnp.concatenate(cols, axis=1), jnp.pad(g[j][:, a:b], ((0, 0), (0, DT_PAD - SSM_HEADS)))


def _w_in_grad_blocks(d, d_dt):
    src_of = {'q': ('qk', 0), 'k': ('qk', RET_QK)}
    blocks = []
    for j in range(N_DEV):
        lo_j, hi_j = j * W_IN_SHARD, (j + 1) * W_IN_SHARD
        cols = []
        for name, lo, hi in W_IN_ORIG:
            a, b = max(lo, lo_j), min(hi, hi_j)
            if a >= b:
                continue
            if name == 'dt':
                cols.append(d_dt[:, a - lo:b - lo])
            else:
                key, off = src_of.get(name, (name, 0))
                cols.append(d[key][:, off + a - lo:off + b - lo])
        blocks.append(jnp.concatenate(cols, axis=1))
    return blocks


def _lanes_of_heads(v):
    return jnp.repeat(v, SSM_HEAD_DIM).reshape(1, SSM_INNER)


def _heads_of_lanes(v):
    return v.reshape(SSM_HEADS, SSM_HEAD_DIM).sum(axis=1)


def _layer_fwd(x, mem, cos2, sin2, p, l, rider=None):
    n = lambda s: f"{s}_l{l}"
    sv = {'x0': x}
    u, u_t = _rmsnorm(x, p['norm_mix'], n("norm_mix"))
    proj = _matmul(u, p['w_in_main'], 'nn', n("in_proj"))
    dtraw = _matmul(u, p['w_in_dt'], 'nn', n("in_proj_dt"))
    xbc = _conv_fwd(proj, p['conv_w'], p['conv_b'], n("conv"))
    a_c = jnp.pad(-jnp.exp(p['a_log']), (0, DT_PAD - SSM_HEADS)).reshape(1, DT_PAD)
    dtb = jnp.pad(p['dt_bias'], (0, DT_PAD - SSM_HEADS)).reshape(1, DT_PAD)
    y_ret, y_ssm, s_hist, h_hist, *delivered = _scan_fwd(proj, xbc, dtraw, cos2, sin2, a_c, dtb, n("scan"), rider)
    dsk_e = _lanes_of_heads(p['d_skip'])
    o_ret, o_ssm, o_ret_t, o_ssm_t = _post_fwd(y_ret, proj, y_ssm, xbc, dsk_e, p['ssm_norm'], n("post"))
    br_ret = _matmul(o_ret, p['w_br_ret'], 'nn', n("br_ret"))
    br_ssm = _matmul(o_ssm, p['w_br_ssm'], 'nn', n("br_ssm"))
    merged, merged_t = _merge_fwd(proj, p['b_gate'], br_ret, br_ssm, n("merge"))
    x1 = _matmul(merged, p['w_out'], 'nn', n("w_out"), extras=(x,), epi=_epi_add)
    sv.update(u_t=u_t, proj=proj, dtraw=dtraw, xbc=xbc, a_c=a_c, dtb=dtb, y_ret=y_ret, y_ssm=y_ssm, s_hist=s_hist,
              h_hist=h_hist, dsk_e=dsk_e, o_ret_t=o_ret_t, o_ssm_t=o_ssm_t, br_ret=br_ret, br_ssm=br_ssm,
              merged_t=merged_t, x1=x1)
    hq, hq_t = _rmsnorm(x1, p['norm_xa'], n("norm_xa"))
    memn, _ = _rmsnorm(mem, p['norm_mem'], n("norm_mem"))
    q = _matmul(hq, p['xa_wq'], 'nn', n("xa_q"))
    kv = _matmul(memn, p['xa_wkv'], 'nn', n("xa_kv"))
    o, o_t = _attn_fwd(q, kv, n("attn"))
    x2 = _matmul(o, p['xa_wo'], 'nn', n("xa_o"), extras=(x1,), epi=_epi_add)
    sv.update(hq_t=hq_t, memn=memn, q=q, kv=kv, o_t=o_t, x2=x2)
    hm, hm_t = _rmsnorm(x2, p['norm_mlp'], n("norm_mlp"))
    a, act, act_t = _matmul(hm, p['mlp_w1'], 'nn', n("mlp_1"), epi=_epi_sqrelu, out_dtypes=(F32, MXU_DTYPE, MXU_DTYPE),
                            out_t=(False, False, True))
    x3 = _matmul(act, p['mlp_w2'], 'nn', n("mlp_2"), extras=(x2,), epi=_epi_add)
    sv.update(hm_t=hm_t, a=a, act_t=act_t)
    return x3, sv, delivered


def _layer_bwd(dx, dxb, mem, cos2, sin2, p, sv, l, rider=None):
    n = lambda s: f"{s}_bwd_l{l}"
    gd = (MXU_DTYPE,)
    g = {}
    g['mlp_w2'] = _matmul(sv['act_t'], dxb, 'nn', n("mlp_2_dw"), out_dtypes=gd)
    da = _matmul(dxb, p['mlp_w2'], 'nt', n("mlp_2_dx"), extras=(sv['a'],), epi=_epi_sqrelu_bwd, out_dtypes=(MXU_DTYPE,))
    g['mlp_w1'] = _matmul(sv['hm_t'], da, 'nn', n("mlp_1_dw"), out_dtypes=gd)
    dhm = _matmul(da, p['mlp_w1'], 'nt', n("mlp_1_dx"))
    dx2, dx2b, g['norm_mlp'] = _rmsnorm_bwd(sv['x2'], p['norm_mlp'], dhm, dx, n("norm_mlp"))
    g['xa_wo'] = _matmul(sv['o_t'], dx2b, 'nn', n("xa_o_dw"), out_dtypes=gd)
    d_o = _matmul(dx2b, p['xa_wo'], 'nt', n("xa_o_dx"))
    dq, dkv = _attn_bwd(sv['q'], sv['kv'], d_o, n("attn"))
    g['xa_wq'] = _matmul(sv['hq_t'], dq, 'nn', n("xa_q_dw"), out_dtypes=gd)
    dhq = _matmul(dq, p['xa_wq'], 'nt', n("xa_q_dx"))
    g['xa_wkv'] = _matmul(sv['memn'], dkv, 'tn', n("xa_kv_dw"), out_dtypes=gd)
    dmemn = _matmul(dkv, p['xa_wkv'], 'nt', n("xa_kv_dx"))
    _, _, g['norm_mem'] = _rmsnorm_bwd(mem, p['norm_mem'], dmemn, None, n("norm_mem"))
    dx1, dx1b, g['norm_xa'] = _rmsnorm_bwd(sv['x1'], p['norm_xa'], dhq, dx2, n("norm_xa"))
    g['w_out'] = _matmul(sv['merged_t'], dx1b, 'nn', n("w_out_dw"), out_dtypes=gd)
    dmerged = _matmul(dx1b, p['w_out'], 'nt', n("w_out_dx"))
    dgates, g['b_gate'], dbr_ret, dbr_ssm = _merge_bwd(sv['proj'], p['b_gate'], sv['br_ret'], sv['br_ssm'], dmerged, n("merge"))
    g['w_br_ret'] = _matmul(sv['o_ret_t'], dbr_ret, 'nn', n("br_ret_dw"), out_dtypes=gd)
    g['w_br_ssm'] = _matmul(sv['o_ssm_t'], dbr_ssm, 'nn', n("br_ssm_dw"), out_dtypes=gd)
    d_or = _matmul(dbr_ret, p['w_br_ret'], 'nt', n("br_ret_dx"))
    d_os = _matmul(dbr_ssm, p['w_br_ssm'], 'nt', n("br_ssm_dx"))
    dyr, dg, dys, dxs_skip, dz, ddsk_e, g['ssm_norm'] = _post_bwd(
        sv['y_ret'], sv['proj'], sv['y_ssm'], sv['xbc'], sv['dsk_e'], p['ssm_norm'], d_or, d_os, n("post"))
    g['d_skip'] = _heads_of_lanes(ddsk_e)
    dqk_r, dv_r, dxbc_act, ddtraw, dac, ddtb, *delivered = _scan_bwd(
        sv['proj'], sv['xbc'], sv['dtraw'], cos2, sin2, sv['a_c'], sv['dtb'], sv['s_hist'], sv['h_hist'],
        dyr, dys, dxs_skip, n("scan"), rider)
    g['a_log'] = dac[0, :SSM_HEADS] * (-jnp.exp(p['a_log']))
    g['dt_bias'] = ddtb[0, :SSM_HEADS]
    dxbc_raw, g['conv_w'], g['conv_b'] = _conv_bwd(sv['proj'], p['conv_w'], p['conv_b'], dxbc_act, n("conv"))
    pieces = {'z': dz, 'xbc': dxbc_raw, 'gates': dgates, 'v': dv_r, 'g': dg, 'qk': dqk_r}
    d_w = {k: _matmul(sv['u_t'], pc, 'nn', n(f"in_proj_dw_{k}"), out_dtypes=gd) for k, pc in pieces.items()}
    d_dt = _matmul(sv['u_t'], ddtraw, 'nn', n("in_proj_dt_dw"), out_dtypes=gd)
    g['w_in'] = _w_in_grad_blocks(d_w, d_dt)
    du_dt = _matmul(ddtraw, p['w_in_dt'], 'nt', n("in_proj_dt_dx"))
    du = _matmul_nt_pieces(list(pieces.values()), p['w_in_main'], n("in_proj_dx"), extras=(du_dt,), epi=_epi_add)
    dx0, dx0b, g['norm_mix'] = _rmsnorm_bwd(sv['x0'], p['norm_mix'], du, dx1, n("norm_mix"))
    return dx0, dx0b, g, delivered


def _layer_params(small, gathered, l):
    p = {k: small[k][l] for k in SMALL if k != 'norm_final'}
    for k, g in zip(SHARDED, gathered):
        if k == 'w_in':
            p['w_in_main'], p['w_in_dt'] = _w_in_from_shards(g)
        elif k in COL_SHARDED:
            p[k] = jnp.concatenate([g[j] for j in range(N_DEV)], axis=1)
        else:
            p[k] = g.reshape(-1, g.shape[-1])
    return p


def _grad_scatter(k, g):
    if k == 'w_in':
        blocks = g
    elif k in COL_SHARDED:
        c = g.shape[1] // N_DEV
        blocks = [g[:, j * c:(j + 1) * c] for j in range(N_DEV)]
    else:
        r = g.shape[0] // N_DEV
        blocks = [g[j * r:(j + 1) * r] for j in range(N_DEV)]
    return jnp.stack([jnp.stack([blocks[2 * chip + core] for chip in range(4)]) for core in range(2)])


def _step(x, mem, positions, small, blocks, loss_target):
    cos2, sin2 = _rope_tables(positions)
    gathered = _run_exchange(_AllGather(blocks[0]), "all_gather_l0")
    saved, layers = [], []
    for l in range(DEPTH):
        p = _layer_params(small, gathered, l)
        rider = _AllGather(blocks[l + 1]) if l + 1 < DEPTH else None
        x, sv, gathered = _layer_fwd(x, mem, cos2, sin2, p, l, rider)
        saved.append(sv)
        layers.append(p)
    loss, dx, dxb, dnf = _loss_head(x, small['norm_final'], loss_target, "loss_head")
    c_idx = lax.axis_index("c").astype(jnp.int32).reshape(1)
    grads, by_chip, chip_sums = [None] * DEPTH, [None] * DEPTH, None
    for l in reversed(range(DEPTH)):
        rider = _ExchangeChips(chip_sums) if chip_sums is not None else None
        dx, dxb, grads[l], delivered = _layer_bwd(dx, dxb, mem, cos2, sin2, layers[l], saved[l], l, rider)
        if rider is not None:
            by_chip[l + 1] = delivered
        by_core = [_grad_scatter(k, grads[l][k]) for k in SHARDED]
        from_sibling = _exchange_cores(by_core, f"grad_exchange_cores_l{l}")
        chip_sums = [_add_halves(a, o, c_idx, f"grad_add_cores_{k}_l{l}") for k, a, o in zip(SHARDED, by_core, from_sibling)]
    by_chip[0] = _run_exchange(_ExchangeChips(chip_sums), "grad_exchange_chips_l0")
    small_g = {}
    for k in SMALL:
        small_g[k] = dnf.reshape(D_MODEL) if k == 'norm_final' else [grads[l][k].reshape(small[k].shape[1:]) for l in range(DEPTH)]
    return loss, dx, small_g, by_chip


MESH = pl.DeviceIdType.MESH
ANY_SPEC = pl.BlockSpec(memory_space=pl.ANY)


def _mesh_pos():
    return lax.axis_index("x"), lax.axis_index("y"), lax.axis_index("c")


def _other_chips(x, y):
    return [(1 - x, y), (x, 1 - y), (1 - x, 1 - y)]


class _AllGather:
    def __init__(self, arrs):
        self.arrs = list(arrs)
        na = self.n = len(self.arrs)
        self.out_shape = [_sds((N_DEV,) + a.shape, a.dtype) for a in self.arrs]
        self.scratch = [pltpu.SemaphoreType.DMA((na, 7)), pltpu.SemaphoreType.DMA((na, 7)), pltpu.SemaphoreType.DMA((na,))]

    def _copies(self, x_refs, o_refs, sems):
        send_sems, recv_sems, local_sems = sems
        x, y, c = _mesh_pos()
        me, sib = (x, y, c), (x, y, 1 - c)
        chips = _other_chips(x, y)

        def copy(a, k, block, to, src=None):
            dst = o_refs[a].at[4 * block[0] + 2 * block[1] + block[2]]
            return pltpu.make_async_remote_copy(src_ref=dst if src is None else src, dst_ref=dst,
                                                send_sem=send_sems.at[a, k], recv_sem=recv_sems.at[a, k],
                                                device_id=to, device_id_type=MESH)

        mine = [pltpu.make_async_copy(x_refs[a], o_refs[a].at[4 * x + 2 * y + c], local_sems.at[a]) for a in range(self.n)]
        first = []
        for a in range(self.n):
            first.append(copy(a, 0, me, sib, src=x_refs[a]))
            first += [copy(a, 1 + j, me, (*chip, c), src=x_refs[a]) for j, chip in enumerate(chips)]
        return copy, mine, first, me, sib, chips, c

    def start(self, x_refs, o_refs, sems):
        _, mine, first, *_ = self._copies(x_refs, o_refs, sems)
        for cp in mine + first:
            cp.start()

    def finish(self, x_refs, o_refs, sems):
        copy, mine, first, me, sib, chips, c = self._copies(x_refs, o_refs, sems)
        passed = []
        for a in range(self.n):
            for j, chip in enumerate(chips):
                copy(a, 1 + j, (*chip, c), me).wait_recv()
                cp = copy(a, 4 + j, (*chip, c), sib)
                cp.start()
                passed.append(cp)
        for a in range(self.n):
            copy(a, 0, sib, me).wait_recv()
            for j, chip in enumerate(chips):
                copy(a, 4 + j, (*chip, 1 - c), me).wait_recv()
        for cp in first + passed:
            cp.wait_send()
        for cp in mine:
            cp.wait()


class _ExchangeChips:
    def __init__(self, arrs):
        self.arrs = list(arrs)
        na = self.n = len(self.arrs)
        self.out_shape = [_sds(a.shape, a.dtype) for a in self.arrs]
        self.scratch = [pltpu.SemaphoreType.DMA((na, 3)), pltpu.SemaphoreType.DMA((na, 3)), pltpu.SemaphoreType.DMA((na,))]

    def _copies(self, a_refs, o_refs, sems):
        send_sems, recv_sems, local_sems = sems
        x, y, c = _mesh_pos()
        my_chip = 2 * x + y
        chips = _other_chips(x, y)
        mine = [pltpu.make_async_copy(a_refs[a].at[my_chip], o_refs[a].at[my_chip], local_sems.at[a]) for a in range(self.n)]
        sends = [pltpu.make_async_remote_copy(src_ref=a_refs[a].at[2 * px + py], dst_ref=o_refs[a].at[my_chip],
                                              send_sem=send_sems.at[a, j], recv_sem=recv_sems.at[a, j],
                                              device_id=(px, py, c), device_id_type=MESH)
                 for a in range(self.n) for j, (px, py) in enumerate(chips)]
        recvs = [pltpu.make_async_remote_copy(src_ref=a_refs[a].at[2 * px + py], dst_ref=o_refs[a].at[2 * px + py],
                                              send_sem=send_sems.at[a, j], recv_sem=recv_sems.at[a, j],
                                              device_id=(px, py, c), device_id_type=MESH)
                 for a in range(self.n) for j, (px, py) in enumerate(chips)]
        return mine, sends, recvs

    def start(self, a_refs, o_refs, sems):
        mine, sends, _ = self._copies(a_refs, o_refs, sems)
        for cp in mine + sends:
            cp.start()

    def finish(self, a_refs, o_refs, sems):
        mine, sends, recvs = self._copies(a_refs, o_refs, sems)
        for cp in recvs:
            cp.wait_recv()
        for cp in sends:
            cp.wait_send()
        for cp in mine:
            cp.wait()


def _run_exchange(ex, name):
    na = ex.n

    def body(*refs):
        i_refs, o_refs, sems = refs[:na], refs[na:2 * na], refs[2 * na:]
        ex.start(i_refs, o_refs, sems)
        ex.finish(i_refs, o_refs, sems)

    return pl.pallas_call(body, in_specs=[ANY_SPEC] * na, out_specs=[ANY_SPEC] * na, out_shape=ex.out_shape,
                          scratch_shapes=ex.scratch, name=name)(*ex.arrs)


def _exchange_cores(arrs, name):
    na = len(arrs)

    def body(*refs):
        a_refs, o_refs = refs[:na], refs[na:2 * na]
        send_sems, recv_sems = refs[2 * na:]
        x, y, c = _mesh_pos()
        cps = [pltpu.make_async_remote_copy(src_ref=a_refs[a].at[1 - c], dst_ref=o_refs[a], send_sem=send_sems.at[a],
                                            recv_sem=recv_sems.at[a], device_id=(x, y, 1 - c), device_id_type=MESH)
               for a in range(na)]
        for cp in cps:
            cp.start()
        for cp in cps:
            cp.wait()

    return pl.pallas_call(
        body, in_specs=[ANY_SPEC] * na, out_specs=[ANY_SPEC] * na,
        out_shape=[_sds(a.shape[1:], a.dtype) for a in arrs],
        scratch_shapes=[pltpu.SemaphoreType.DMA((na,)), pltpu.SemaphoreType.DMA((na,))],
        name=name,
    )(*arrs)


def _as_rows(a, lead):
    return a.reshape(a.shape[:lead] + (-1, a.shape[-1]))


def _add_halves(a, other, c_idx, name):
    a3, o2 = _as_rows(a, 1), _as_rows(other, 0)
    rows, cols = o2.shape
    tr = _pick(rows, 256)

    def body(c_ref, a_ref, o_ref, out_ref):
        out_ref[...] = (a_ref[0].astype(F32) + o_ref[...].astype(F32)).astype(out_ref.dtype)

    out = pl.pallas_call(
        body,
        grid_spec=pltpu.PrefetchScalarGridSpec(
            num_scalar_prefetch=1, grid=(rows // tr,),
            in_specs=[pl.BlockSpec((1, tr, cols), lambda i, c_ref: (c_ref[0], i, 0)),
                      pl.BlockSpec((tr, cols), lambda i, c_ref: (i, 0))],
            out_specs=pl.BlockSpec((tr, cols), lambda i, c_ref: (i, 0))),
        out_shape=_sds((rows, cols), a.dtype), compiler_params=_cparams(("parallel",)), name=name,
    )(c_idx, a3, o2)
    return out.reshape(other.shape)


def _all_reduce_small(v, name):
    r = v.shape[0]

    def body(v_ref, o_ref, slots, send_sems, recv_sems):
        x, y, c = _mesh_pos()
        me = 4 * x + 2 * y + c
        slots[me] = v_ref[...]
        cps = []
        for k in range(1, N_DEV):
            px = 1 - x if k & 4 else x
            py = 1 - y if k & 2 else y
            pc = 1 - c if k & 1 else c
            cps.append(pltpu.make_async_remote_copy(src_ref=v_ref, dst_ref=slots.at[me], send_sem=send_sems.at[k - 1],
                                                    recv_sem=recv_sems.at[k - 1], device_id=(px, py, pc), device_id_type=MESH))
        for cp in cps:
            cp.start()
        for cp in cps:
            cp.wait()
        acc = slots[0]
        for d in range(1, N_DEV):
            acc = acc + slots[d]
        o_ref[...] = acc

    vm = pl.BlockSpec(memory_space=pltpu.VMEM)
    return pl.pallas_call(
        body, in_specs=[vm], out_specs=vm, out_shape=_sds((r, LANES), F32),
        scratch_shapes=[pltpu.VMEM((N_DEV, r, LANES), F32), pltpu.SemaphoreType.DMA((N_DEV - 1,)),
                        pltpu.SemaphoreType.DMA((N_DEV - 1,))],
        compiler_params=pltpu.CompilerParams(vmem_limit_bytes=VMEM_LIMIT_BYTES), name=name,
    )(v)


def _adamw(w, g_slots, m, v, name):
    depth, rows, cols = w.shape
    ns = g_slots.shape[0]
    tr = _pick(rows, 256 if cols <= 1024 else 128)

    def body(w_ref, g_ref, m_ref, v_ref, go_ref, d_ref, mo_ref, vo_ref):
        g = g_ref[0, 0].astype(F32)
        for i in range(1, ns):
            g = g + g_ref[i, 0].astype(F32)
        m_new = ADAM_B1 * m_ref[0] + (1.0 - ADAM_B1) * g
        v_new = ADAM_B2 * v_ref[0] + (1.0 - ADAM_B2) * (g * g)
        m_hat = m_new / (1.0 - ADAM_B1 ** ADAM_STEP)
        v_hat = v_new / (1.0 - ADAM_B2 ** ADAM_STEP)
        go_ref[0] = g
        d_ref[0] = -ADAM_LR * (m_hat / (jnp.sqrt(v_hat) + ADAM_EPS) + ADAM_WD * w_ref[0])
        mo_ref[0] = m_new
        vo_ref[0] = v_new

    blk = pl.BlockSpec((1, tr, cols), lambda l, i: (l, i, 0))
    return pl.pallas_call(
        body, grid=(depth, rows // tr),
        in_specs=[blk, pl.BlockSpec((ns, 1, tr, cols), lambda l, i: (0, l, i, 0)), blk, blk],
        out_specs=[blk] * 4, out_shape=[_sds(w.shape, F32)] * 4,
        compiler_params=_cparams(("parallel", "parallel")), name=name,
    )(w, g_slots, m, v)


_ARG_NAMES = (['x', 'mem', 'positions'] + WEIGHTS + ['loss_target'] + ['m_' + n for n in WEIGHTS]
              + ['v_' + n for n in WEIGHTS])


PACK_TILE = 8 * LANES


def _pack_rows(parts):
    blocks = []
    for part in parts:
        flat = part.reshape(-1)
        pad = (-flat.shape[0]) % PACK_TILE
        blocks.append((jnp.pad(flat, (0, pad)) if pad else flat).reshape(-1, LANES))
    return jnp.concatenate(blocks, axis=0)


def _unpack_rows(packed, shapes):
    out, off = [], 0
    for shp in shapes:
        n = int(np.prod(shp))
        rows = -(-n // PACK_TILE) * 8
        out.append(packed[off:off + rows].reshape(-1)[:n].reshape(shp))
        off += rows
    return out


def kernel(x, mem, positions, norm_mix, w_in, b_gate, conv_w, conv_b, dt_bias, a_log, d_skip, ssm_norm, w_br_ret, w_br_ssm, w_out, norm_xa, norm_mem, xa_wq, xa_wkv, xa_wo, norm_mlp, mlp_w1, mlp_w2, norm_final, loss_target, m_norm_mix, m_w_in, m_b_gate, m_conv_w, m_conv_b, m_dt_bias, m_a_log, m_d_skip, m_ssm_norm, m_w_br_ret, m_w_br_ssm, m_w_out, m_norm_xa, m_norm_mem, m_xa_wq, m_xa_wkv, m_xa_wo, m_norm_mlp, m_mlp_w1, m_mlp_w2, m_norm_final, v_norm_mix, v_w_in, v_b_gate, v_conv_w, v_conv_b, v_dt_bias, v_a_log, v_d_skip, v_ssm_norm, v_w_br_ret, v_w_br_ssm, v_w_out, v_norm_xa, v_norm_mem, v_xa_wq, v_xa_wkv, v_xa_wo, v_norm_mlp, v_mlp_w1, v_mlp_w2, v_norm_final):
    d = dict(zip(_ARG_NAMES, (x, mem, positions, norm_mix, w_in, b_gate, conv_w, conv_b, dt_bias, a_log, d_skip, ssm_norm, w_br_ret, w_br_ssm, w_out, norm_xa, norm_mem, xa_wq, xa_wkv, xa_wo, norm_mlp, mlp_w1, mlp_w2, norm_final, loss_target, m_norm_mix, m_w_in, m_b_gate, m_conv_w, m_conv_b, m_dt_bias, m_a_log, m_d_skip, m_ssm_norm, m_w_br_ret, m_w_br_ssm, m_w_out, m_norm_xa, m_norm_mem, m_xa_wq, m_xa_wkv, m_xa_wo, m_norm_mlp, m_mlp_w1, m_mlp_w2, m_norm_final, v_norm_mix, v_w_in, v_b_gate, v_conv_w, v_conv_b, v_dt_bias, v_a_log, v_d_skip, v_ssm_norm, v_w_br_ret, v_w_br_ssm, v_w_out, v_norm_xa, v_norm_mem, v_xa_wq, v_xa_wkv, v_xa_wo, v_norm_mlp, v_mlp_w1, v_mlp_w2, v_norm_final)))
    blocks = [[d[k][l] if k == 'conv_w' else d[k][l].astype(MXU_DTYPE) for k in SHARDED] for l in range(DEPTH)]
    small = {k: d[k] for k in SMALL}
    loss, grad_x, grads, by_chip_l = _step(d['x'][0], d['mem'][0], d['positions'][0], small, blocks, d['loss_target'][0])
    by_chip = [jnp.stack([by_chip_l[l][i] for l in range(DEPTH)], axis=1) for i in range(len(SHARDED))]
    small_g = [grads[k] if k == 'norm_final' else jnp.stack(grads[k]) for k in SMALL]
    total = _all_reduce_small(_pack_rows([loss] + small_g), "all_reduce_small")
    loss_out = total[0, 0]
    res = {}
    for k, g4 in zip(SHARDED, by_chip):
        res[k] = _adamw(d[k], g4, d['m_' + k], d['v_' + k], f"adamw_{k}")
    small_shapes = [d[k].shape for k in SMALL]
    pk = lambda pre: _pack_rows([d[pre + k] for k in SMALL])
    outs = _adamw(pk('')[None], total[8:][None, None], pk('m_')[None], pk('v_')[None], "adamw_small")
    unpacked = [_unpack_rows(o[0], small_shapes) for o in outs]
    for i, k in enumerate(SMALL):
        res[k] = [unpacked[j][i] for j in range(4)]
    return (loss_out, grad_x[None], *[res[k][0] for k in WEIGHTS], *[res[k][1] for k in WEIGHTS],
            *[res[k][2] for k in WEIGHTS], *[res[k][3] for k in WEIGHTS])
```

```python
import functools

import numpy as np
import jax
import jax.numpy as jnp
from jax import lax
from jax.experimental import pallas as pl
from jax.experimental.pallas import tpu as pltpu

F32 = jnp.float32
MXU_DTYPE = jnp.bfloat16
VMEM_LIMIT_BYTES = 56 * 1024 * 1024
LANES = 128
N_DEV = 8

D_MODEL = 1024
DEPTH = 4
CHUNK = 64
EPS = 1e-6
RET_HEADS, RET_QK_DIM, RET_V_DIM = 4, 128, 256
RET_QK, RET_V = 512, 1024
ROPE_THETA = 10000.0
SSM_INNER, SSM_HEAD_DIM, SSM_HEADS, SSM_GROUPS, SSM_STATE, SSM_CONV = 2048, 64, 32, 8, 128, 4
SSM_BC = 1024
SSM_CONV_DIM = 4096
IN_DIM = 11296
XA_HEADS, XA_HEAD_DIM = 4, 256
D_FF = 4096
ADAM_LR, ADAM_B1, ADAM_B2, ADAM_EPS, ADAM_WD, ADAM_STEP = 0.001, 0.9, 0.999, 1e-08, 0.01, 10

PROJ_W = 11264
COL_Z, COL_XBC, COL_GATES, COL_V, COL_G, COL_Q, COL_K = 0, 2048, 6144, 8192, 9216, 10240, 10752
DT_PAD = 128
N_LTILE = SSM_INNER // LANES

WEIGHTS = ['norm_mix', 'w_in', 'b_gate', 'conv_w', 'conv_b', 'dt_bias', 'a_log', 'd_skip', 'ssm_norm',
           'w_br_ret', 'w_br_ssm', 'w_out', 'norm_xa', 'norm_mem', 'xa_wq', 'xa_wkv', 'xa_wo', 'norm_mlp',
           'mlp_w1', 'mlp_w2', 'norm_final']
COL_SHARDED = ['w_in', 'conv_w', 'xa_wkv', 'mlp_w1']
ROW_SHARDED = ['w_br_ret', 'w_br_ssm', 'w_out', 'xa_wq', 'xa_wo', 'mlp_w2']
SHARDED = COL_SHARDED + ROW_SHARDED
FIRST = ['w_in', 'conv_w']
LATER = [n for n in SHARDED if n not in FIRST]
SMALL = [n for n in WEIGHTS if n not in SHARDED]


def _cparams(sem=None):
    return pltpu.CompilerParams(dimension_semantics=sem, vmem_limit_bytes=VMEM_LIMIT_BYTES)


def _sds(shape, dtype):
    return jax.ShapeDtypeStruct(shape, dtype)


def _full(shape):
    nd = len(shape)
    return pl.BlockSpec(shape, lambda *_: (0,) * nd)


_DIMS = {'nn': (((1,), (0,)), ((), ())), 'nt': (((1,), (1,)), ((), ())), 'tn': (((0,), (0,)), ((), ()))}


def _dot(a, b, mode='nn'):
    return lax.dot_general(a.astype(MXU_DTYPE), b.astype(MXU_DTYPE), _DIMS[mode], preferred_element_type=F32)


@functools.partial(jax.custom_vjp, nondiff_argnums=(2,))
def _mm(a, b, mode):
    return _dot(a, b, mode)


def _mm_fwd(a, b, mode):
    return _dot(a, b, mode), (a, b)


def _mm_bwd(mode, res, g):
    a, b = res
    if mode == 'nn':
        return _dot(g, b, 'nt'), _dot(a, g, 'tn')
    if mode == 'nt':
        return _dot(g, b, 'nn'), _dot(g, a, 'tn')
    return _dot(b, g, 'nt'), _dot(a, g, 'nn')


_mm.defvjp(_mm_fwd, _mm_bwd)


def _split3(x):
    hi = x.astype(jnp.bfloat16)
    r1 = x - hi.astype(F32)
    mid = r1.astype(jnp.bfloat16)
    lo = (r1 - mid.astype(F32)).astype(jnp.bfloat16)
    return hi, mid, lo


def _dot_sel(x, c, left=False):
    dims = _DIMS['nn']
    parts = _split3(x)
    if left:
        outs = [lax.dot_general(c, p, dims, preferred_element_type=F32) for p in parts]
    else:
        outs = [lax.dot_general(p, c, dims, preferred_element_type=F32) for p in parts]
    return (outs[0] + outs[1]) + outs[2]


def _silu(x):
    return x * jax.nn.sigmoid(x)


def _softplus(x):
    pos = x > 0.0
    return jnp.where(pos, x, 0.0) + jnp.log1p(jnp.exp(jnp.where(pos, -x, x)))


def _rms(x):
    return x * lax.rsqrt(jnp.mean(x * x, axis=-1, keepdims=True) + EPS)


def _pick(n, pref):
    t = min(n, pref)
    while n % t:
        t //= 2
    return t


def _with_rider(core, n_in, n_out, n_scratch, rider, grid):
    if rider is None:
        return core
    na, nrs = rider.n, len(rider.scratch)

    def at(step_of):
        cond = pl.program_id(0) == step_of(grid[0])
        for ax in range(1, len(grid)):
            cond = cond & (pl.program_id(ax) == step_of(grid[ax]))
        return cond

    def body(*refs):
        ci, ri = refs[:n_in], refs[n_in:n_in + na]
        co, ro = refs[n_in + na:n_in + na + n_out], refs[n_in + na + n_out:n_in + 2 * na + n_out]
        sc = refs[n_in + 2 * na + n_out:]
        cs, rs = sc[:n_scratch], sc[n_scratch:]
        assert len(rs) == nrs

        @pl.when(at(lambda n: 0))
        def _():
            rider.start(ri, ro, rs)

        core(*ci, *co, *cs)

        @pl.when(at(lambda n: n - 1))
        def _():
            rider.finish(ri, ro, rs)

    return body


def _rider_args(rider):
    if rider is None:
        return [], [], [], [], []
    return list(rider.arrs), [ANY_SPEC] * rider.n, [ANY_SPEC] * rider.n, list(rider.out_shape), list(rider.scratch)


MATMUL_TK_MAX = 4096


def _tiles(mode, m, n, k):
    tm, tn = (512, 1024) if mode == 'nt' else (1024, 512)
    tk = k
    while tk > MATMUL_TK_MAX or k % tk or tk % LANES:
        tk -= LANES
    return _pick(m, tm), _pick(n, tn), tk


def _matmul(a, b, mode, name, *, extras=(), epi=None, out_dtypes=(F32,), out_t=None, tiles=None, rider=None):
    if mode == 'nn':
        (m, k), (k2, n) = a.shape, b.shape
    elif mode == 'nt':
        (m, k), (n, k2) = a.shape, b.shape
    else:
        (k, m), (k2, n) = a.shape, b.shape
    assert k == k2, (a.shape, b.shape, mode)
    tm, tn, tk = tiles or _tiles(mode, m, n, k)
    nk = k // tk
    n_ex, n_out = len(extras), len(out_dtypes)
    out_t = out_t or (False,) * n_out

    def finish(acc, ex_refs, o_refs):
        outs = epi(acc, *[r[...] for r in ex_refs]) if epi is not None else (acc,)
        for o_ref, o, tr in zip(o_refs, outs, out_t):
            o_ref[...] = (o.T if tr else o).astype(o_ref.dtype)

    def body(*refs):
        a_ref, b_ref = refs[0], refs[1]
        ex_refs = refs[2:2 + n_ex]
        o_refs = refs[2 + n_ex:2 + n_ex + n_out]
        if nk == 1:
            finish(_dot(a_ref[...], b_ref[...], mode), ex_refs, o_refs)
            return
        acc_ref = refs[-1]
        kk = pl.program_id(2)

        @pl.when(kk == 0)
        def _():
            acc_ref[...] = jnp.zeros_like(acc_ref)

        acc_ref[...] += _dot(a_ref[...], b_ref[...], mode)

        @pl.when(kk == nk - 1)
        def _():
            finish(acc_ref[...], ex_refs, o_refs)

    if mode == 'nn':
        a_spec = pl.BlockSpec((tm, tk), lambda i, j, kk: (i, kk))
        b_spec = pl.BlockSpec((tk, tn), lambda i, j, kk: (kk, j))
    elif mode == 'nt':
        a_spec = pl.BlockSpec((tm, tk), lambda i, j, kk: (i, kk))
        b_spec = pl.BlockSpec((tn, tk), lambda i, j, kk: (j, kk))
    else:
        a_spec = pl.BlockSpec((tk, tm), lambda i, j, kk: (kk, i))
        b_spec = pl.BlockSpec((tk, tn), lambda i, j, kk: (kk, j))
    mn_spec = pl.BlockSpec((tm, tn), lambda i, j, kk: (i, j))
    nm_spec = pl.BlockSpec((tn, tm), lambda i, j, kk: (j, i))
    grid = (m // tm, n // tn, nk)
    r_arrs, r_in, r_out, r_shape, r_scratch = _rider_args(rider)
    outs = pl.pallas_call(
        _with_rider(body, 2 + n_ex, n_out, int(nk > 1), rider, grid), grid=grid,
        in_specs=[a_spec, b_spec] + [mn_spec] * n_ex + r_in,
        out_specs=[nm_spec if tr else mn_spec for tr in out_t] + r_out,
        out_shape=[_sds((n, m) if tr else (m, n), dt) for dt, tr in zip(out_dtypes, out_t)] + r_shape,
        scratch_shapes=([pltpu.VMEM((tm, tn), F32)] if nk > 1 else []) + r_scratch,
        compiler_params=_cparams(("arbitrary",) * 3 if rider is not None else ("parallel", "parallel", "arbitrary")),
        name=name,
    )(a, b, *extras, *r_arrs)
    res = outs[0] if n_out == 1 else outs[:n_out]
    return (res, outs[n_out:]) if rider is not None else res


def _epi_add(acc, r):
    return (acc + r,)


PIECE_TK = 1024


def _matmul_nt_pieces(pieces, b, name, *, extras=(), epi=None, out_dtypes=(F32,)):
    m, n = pieces[0].shape[0], b.shape[0]
    tm, tn, tk = _pick(m, 512), _pick(n, 1024), PIECE_TK
    steps = [pc.shape[1] // tk for pc in pieces]
    starts = [sum(steps[:i]) for i in range(len(pieces))]
    nk = sum(steps)
    assert b.shape[1] == nk * tk and all(pc.shape[1] % tk == 0 for pc in pieces)
    n_pc, n_ex, n_out = len(pieces), len(extras), len(out_dtypes)

    def body(*refs):
        pc_refs, b_ref = refs[:n_pc], refs[n_pc]
        ex_refs = refs[n_pc + 1:n_pc + 1 + n_ex]
        o_refs = refs[n_pc + 1 + n_ex:n_pc + 1 + n_ex + n_out]
        acc_ref = refs[-1]
        kk = pl.program_id(2)

        @pl.when(kk == 0)
        def _():
            acc_ref[...] = jnp.zeros_like(acc_ref)

        for pc_ref, st, ns in zip(pc_refs, starts, steps):
            @pl.when((kk >= st) & (kk < st + ns))
            def _(pc_ref=pc_ref):
                acc_ref[...] += _dot(pc_ref[...], b_ref[...], 'nt')

        @pl.when(kk == nk - 1)
        def _():
            acc = acc_ref[...]
            outs = epi(acc, *[r[...] for r in ex_refs]) if epi is not None else (acc,)
            for o_ref, o in zip(o_refs, outs):
                o_ref[...] = o.astype(o_ref.dtype)

    pc_specs = [pl.BlockSpec((tm, tk), lambda i, j, kk, st=st, ns=ns: (i, jnp.clip(kk - st, 0, ns - 1)))
                for st, ns in zip(starts, steps)]
    mn_spec = pl.BlockSpec((tm, tn), lambda i, j, kk: (i, j))
    outs = pl.pallas_call(
        body, grid=(m // tm, n // tn, nk),
        in_specs=pc_specs + [pl.BlockSpec((tn, tk), lambda i, j, kk: (j, kk))] + [mn_spec] * n_ex,
        out_specs=[mn_spec] * n_out, out_shape=[_sds((m, n), dt) for dt in out_dtypes],
        scratch_shapes=[pltpu.VMEM((tm, tn), F32)],
        compiler_params=_cparams(("parallel", "parallel", "arbitrary")), name=name,
    )(*pieces, b, *extras)
    return outs[0] if n_out == 1 else outs


def _rmsnorm_fn(x, w):
    return _rms(x) * w


def _rmsnorm(x, w, name):
    s, d = x.shape
    t = _pick(s, 512)

    def body(x_ref, w_ref, o_ref, ot_ref):
        y = _rmsnorm_fn(x_ref[...], w_ref[...])
        o_ref[...] = y.astype(o_ref.dtype)
        ot_ref[...] = y.T.astype(ot_ref.dtype)

    return pl.pallas_call(
        body, grid=(s // t,),
        in_specs=[pl.BlockSpec((t, d), lambda i: (i, 0)), _full((1, d))],
        out_specs=[pl.BlockSpec((t, d), lambda i: (i, 0)), pl.BlockSpec((d, t), lambda i: (0, i))],
        out_shape=[_sds((s, d), MXU_DTYPE), _sds((d, s), MXU_DTYPE)],
        compiler_params=_cparams(("parallel",)), name=name,
    )(x, w.reshape(1, d))


def _rmsnorm_bwd(x, w, du, dres, name):
    s, d = x.shape
    t = _pick(s, 512)
    has_res = dres is not None

    def body(*refs):
        if has_res:
            x_ref, w_ref, du_ref, dres_ref, dx_ref, dxb_ref, dw_ref = refs
        else:
            x_ref, w_ref, du_ref, dx_ref, dxb_ref, dw_ref = refs
        _, vjp = jax.vjp(_rmsnorm_fn, x_ref[...], w_ref[...])
        dx, dw = vjp(du_ref[...])
        dx = dx + dres_ref[...] if has_res else dx
        dx_ref[...] = dx
        dxb_ref[...] = dx.astype(dxb_ref.dtype)

        @pl.when(pl.program_id(0) == 0)
        def _():
            dw_ref[...] = jnp.zeros_like(dw_ref)

        dw_ref[...] += dw

    row = pl.BlockSpec((t, d), lambda i: (i, 0))
    return pl.pallas_call(
        body, grid=(s // t,),
        in_specs=[row, _full((1, d)), row] + ([row] if has_res else []),
        out_specs=[row, row, _full((1, d))],
        out_shape=[_sds((s, d), F32), _sds((s, d), MXU_DTYPE), _sds((1, d), F32)],
        compiler_params=_cparams(("arbitrary",)), name=name,
    )(x, w.reshape(1, d), du, *([dres] if has_res else []))


CONV_CW = 2048
CONV_HALO = 8


def _shifted(cat):
    return [cat] + [pltpu.roll(cat, sft, axis=0) for sft in (1, 2, 3)]


def _conv_taps(shifted, w, n_rows, off):
    acc = shifted[0][off:off + n_rows, :] * w[3:4, :]
    for sft in (1, 2, 3):
        acc = acc + shifted[sft][off:off + n_rows, :] * w[3 - sft:4 - sft, :]
    return acc


def _conv_fwd(proj, conv_w, conv_b, name):
    s = proj.shape[0]
    tr = _pick(s, 256)
    hb = tr // CONV_HALO
    col0 = COL_XBC // CONV_CW

    def body(prev_ref, x_ref, w_ref, b_ref, o_ref):
        i = pl.program_id(1)
        prev = jnp.where(i == 0, 0.0, prev_ref[...])
        cat = jnp.concatenate([prev, x_ref[...]], axis=0)
        o_ref[...] = _silu(_conv_taps(_shifted(cat), w_ref[...], tr, CONV_HALO) + b_ref[...])

    return pl.pallas_call(
        body, grid=(SSM_CONV_DIM // CONV_CW, s // tr),
        in_specs=[pl.BlockSpec((CONV_HALO, CONV_CW), lambda j, i: (jnp.maximum(i * hb - 1, 0), j + col0)),
                  pl.BlockSpec((tr, CONV_CW), lambda j, i: (i, j + col0)),
                  pl.BlockSpec((SSM_CONV, CONV_CW), lambda j, i: (0, j)),
                  pl.BlockSpec((1, CONV_CW), lambda j, i: (0, j))],
        out_specs=pl.BlockSpec((tr, CONV_CW), lambda j, i: (i, j)),
        out_shape=_sds((s, SSM_CONV_DIM), F32),
        compiler_params=_cparams(("parallel", "parallel")), name=name,
    )(proj, proj, conv_w, conv_b.reshape(1, SSM_CONV_DIM))


def _conv_bwd(proj, conv_w, conv_b, dact, name):
    s = proj.shape[0]
    tr = _pick(s, 256)
    hb = tr // CONV_HALO
    nb = s // CONV_HALO
    nt = s // tr
    col0 = COL_XBC // CONV_CW
    h = CONV_HALO

    def body(prev_ref, x_ref, next_ref, w_ref, b_ref, da_ref, dan_ref, dx_ref, dw_ref, db_ref):
        i = pl.program_id(1)
        w = w_ref[...]
        prev = jnp.where(i == 0, 0.0, prev_ref[...])
        cat = jnp.concatenate([prev, x_ref[...], next_ref[...]], axis=0)
        shifted = _shifted(cat)
        pre = _conv_taps(shifted, w, tr + h, h) + b_ref[...]
        dact_n = jnp.where(i == nt - 1, 0.0, dan_ref[...])
        dact_ext = jnp.concatenate([da_ref[...], dact_n], axis=0)
        sg = jax.nn.sigmoid(pre)
        dpre = dact_ext * (sg * (1.0 + pre * (1.0 - sg)))
        dx = dpre[:tr, :] * w[3:4, :]
        for sft in (1, 2, 3):
            dx = dx + pltpu.roll(dpre, tr + h - sft, axis=0)[:tr, :] * w[3 - sft:4 - sft, :]
        dx_ref[...] = dx.astype(dx_ref.dtype)

        @pl.when(i == 0)
        def _():
            dw_ref[...] = jnp.zeros_like(dw_ref)
            db_ref[...] = jnp.zeros_like(db_ref)

        dp = dpre[:tr, :]
        db_ref[...] += jnp.sum(dp, axis=0, keepdims=True)
        for r, sft in enumerate((3, 2, 1, 0)):
            dw_ref[r:r + 1, :] += jnp.sum(dp * shifted[sft][h:h + tr, :], axis=0, keepdims=True)

    return pl.pallas_call(
        body, grid=(SSM_CONV_DIM // CONV_CW, nt),
        in_specs=[pl.BlockSpec((h, CONV_CW), lambda j, i: (jnp.maximum(i * hb - 1, 0), j + col0)),
                  pl.BlockSpec((tr, CONV_CW), lambda j, i: (i, j + col0)),
                  pl.BlockSpec((h, CONV_CW), lambda j, i: (jnp.minimum((i + 1) * hb, nb - 1), j + col0)),
                  pl.BlockSpec((SSM_CONV, CONV_CW), lambda j, i: (0, j)),
                  pl.BlockSpec((1, CONV_CW), lambda j, i: (0, j)),
                  pl.BlockSpec((tr, CONV_CW), lambda j, i: (i, j)),
                  pl.BlockSpec((h, CONV_CW), lambda j, i: (jnp.minimum((i + 1) * hb, nb - 1), j))],
        out_specs=[pl.BlockSpec((tr, CONV_CW), lambda j, i: (i, j)),
                   pl.BlockSpec((SSM_CONV, CONV_CW), lambda j, i: (0, j)),
                   pl.BlockSpec((1, CONV_CW), lambda j, i: (0, j))],
        out_shape=[_sds((s, SSM_CONV_DIM), MXU_DTYPE), _sds((SSM_CONV, SSM_CONV_DIM), F32), _sds((1, SSM_CONV_DIM), F32)],
        compiler_params=_cparams(("parallel", "arbitrary")), name=name,
    )(proj, proj, proj, conv_w, conv_b.reshape(1, SSM_CONV_DIM), dact, dact)


def _scan_tables():
    idx = np.arange(CHUNK, dtype=np.float32)
    lg = np.log1p(-(2.0 ** (-5.0 - np.arange(RET_HEADS, dtype=np.float32)))).astype(np.float32)
    rel = np.abs(idx[:, None] - idx[None, :])
    r_intra = np.exp(lg[:, None, None] * rel).astype(np.float32)
    qd = np.exp(lg[None, :] * (idx[:, None] + 1.0)).astype(np.float32)
    kd = np.exp(lg[None, :] * (CHUNK - 1.0 - idx[:, None])).astype(np.float32)
    gam = [float(v) for v in np.exp(lg * CHUNK).astype(np.float32)]
    qd_e = np.repeat(qd, RET_QK_DIM, axis=1)
    kd_e = np.repeat(kd, RET_QK_DIM, axis=1)
    e = np.zeros((DT_PAD, SSM_INNER), np.float32)
    for hh in range(SSM_HEADS):
        e[hh, hh * SSM_HEAD_DIM:(hh + 1) * SSM_HEAD_DIM] = 1.0
    tri = np.tril(np.ones((CHUNK, CHUNK), np.float32))
    eye2 = np.concatenate([np.eye(CHUNK, dtype=np.float32)] * 2, axis=1)
    bdm = np.kron(np.eye(2, dtype=np.float32), np.ones((CHUNK, CHUNK), np.float32))
    last = np.zeros((CHUNK, LANES), np.float32)
    last[CHUNK - 1, :] = 1.0
    f32c = [jnp.asarray(c) for c in (r_intra, qd_e, kd_e, eye2, bdm, last)]
    sel = [jnp.asarray(c, jnp.bfloat16) for c in (e, e.T.copy(), tri, tri.T.copy())]
    return f32c + sel, gam


def _rope(t, cos2, sin2):
    return t * cos2 + pltpu.roll(t, RET_QK_DIM // 2, axis=1) * sin2


def _rope_t(d, cos2, sin2):
    return d * cos2 + pltpu.roll(d * sin2, RET_QK_DIM // 2, axis=1)


def _ret_step(q, k, v, st, r_intra, qd, kd, gamma):
    k = k * (RET_QK_DIM ** -0.5)
    sc = _mm(q, k, 'nt') * r_intra
    y = _mm(sc, v, 'nn') + _mm(q * qd, st, 'nn')
    st_new = st * gamma + _mm(k * kd, v, 'tn')
    return y, st_new


def _ssd_heads(dtraw, dtb, a_c, tri):
    dt = _softplus(dtraw + dtb)
    return dt, _dot_sel(dt * a_c, tri, left=True)


def _ssd_group(dte0, dte1, cum0, cum1, xs0, xs1, bm, cm, ht0, ht1, eye2, bdm, last):
    cbp = _mm(cm, jnp.concatenate([bm, bm], axis=0), 'nt')
    outs = []
    for dte, cum, xs, ht in ((dte0, cum0, xs0, ht0), (dte1, cum1, xs1, ht1)):
        r = jnp.sum(cum * eye2, axis=0, keepdims=True)
        dlt = cum - r
        seg = jnp.exp(jnp.where(dlt > 0.0, -dlt, dlt))
        xdt = xs * dte
        bd = jnp.concatenate([xdt, xdt], axis=0) * bdm
        clast = jnp.sum(cum * last, axis=0, keepdims=True)
        y = _mm(cbp * seg, bd, 'nn') + jnp.exp(cum) * _mm(cm, ht, 'nn')
        ht_new = jnp.exp(clast) * ht + _mm(bm, xdt * jnp.exp(clast - cum), 'tn')
        outs += [y, ht_new]
    return tuple(outs)


def _scan_in_specs(nc, rev):
    ch = (lambda c: nc - 1 - c) if rev else (lambda c: c)
    col = lambda w, blk: pl.BlockSpec((CHUNK, w), lambda c: (ch(c), blk))
    return [col(RET_QK, COL_Q // RET_QK), col(RET_QK, COL_K // RET_QK), col(RET_V, COL_V // RET_V),
            col(SSM_INNER, 0), col(SSM_BC, 2), col(SSM_BC, 3),
            col(DT_PAD, 0), col(LANES, 0), col(LANES, 0)]


def _const_specs(consts):
    return [_full(c.shape) for c in consts]


def _tile(t):
    return slice(t * LANES, (t + 1) * LANES)


def _scan_fwd(proj, xbc, dtraw, cos2, sin2, a_c, dtb, name, rider=None):
    s = proj.shape[0]
    nc = s // CHUNK
    consts, gam = _scan_tables()
    r_arrs, r_in, r_out, r_shape, r_scratch = _rider_args(rider)

    def body(q_ref, k_ref, v_ref, xs_ref, bm_ref, cm_ref, dt_ref, cos_ref, sin_ref, ac_ref, dtb_ref,
             ri_ref, qd_ref, kd_ref, eye_ref, bdm_ref, last_ref, e_ref, et_ref, tri_ref, trit_ref,
             yr_ref, ys_ref, sh_ref, hh_ref, st_sc, ht_sc):
        @pl.when(pl.program_id(0) == 0)
        def _():
            st_sc[...] = jnp.zeros_like(st_sc)
            ht_sc[...] = jnp.zeros_like(ht_sc)

        sh_ref[0] = st_sc[...]
        hh_ref[0] = ht_sc[...]
        cos2, sin2 = cos_ref[...], sin_ref[...]
        st_new = []
        for h in range(RET_HEADS):
            ql = slice(h * RET_QK_DIM, (h + 1) * RET_QK_DIM)
            vl = slice(h * RET_V_DIM, (h + 1) * RET_V_DIM)
            y, st_h = _ret_step(_rope(q_ref[:, ql], cos2, sin2), _rope(k_ref[:, ql], cos2, sin2), v_ref[:, vl],
                                st_sc[ql, :], ri_ref[h], qd_ref[:, ql], kd_ref[:, ql], gam[h])
            yr_ref[:, vl] = y
            st_new.append(st_h)
        dt, cum_c = _ssd_heads(dt_ref[...], dtb_ref[...], ac_ref[...], tri_ref[...])
        both = jnp.concatenate([dt, cum_c], axis=0)
        eye2, bdm, last = eye_ref[...], bdm_ref[...], last_ref[...]
        ht_new = []
        for g in range(SSM_GROUPS):
            t0, t1 = 2 * g, 2 * g + 1
            e0, e1 = _dot_sel(both, e_ref[:, _tile(t0)]), _dot_sel(both, e_ref[:, _tile(t1)])
            y0, h0, y1, h1 = _ssd_group(e0[:CHUNK], e1[:CHUNK], e0[CHUNK:], e1[CHUNK:],
                                        xs_ref[:, _tile(t0)], xs_ref[:, _tile(t1)], bm_ref[:, _tile(g)],
                                        cm_ref[:, _tile(g)], ht_sc[:, _tile(t0)], ht_sc[:, _tile(t1)], eye2, bdm, last)
            ys_ref[:, _tile(t0)] = y0
            ys_ref[:, _tile(t1)] = y1
            ht_new += [h0, h1]
        for h in range(RET_HEADS):
            st_sc[h * RET_QK_DIM:(h + 1) * RET_QK_DIM, :] = st_new[h]
        for t in range(N_LTILE):
            ht_sc[:, _tile(t)] = ht_new[t]

    in_specs = _scan_in_specs(nc, False) + [_full((1, DT_PAD)), _full((1, DT_PAD))] + _const_specs(consts)
    return pl.pallas_call(
        _with_rider(body, len(in_specs), 4, 2, rider, (nc,)), grid=(nc,),
        in_specs=in_specs + r_in,
        out_specs=[pl.BlockSpec((CHUNK, RET_V), lambda c: (c, 0)),
                   pl.BlockSpec((CHUNK, SSM_INNER), lambda c: (c, 0)),
                   pl.BlockSpec((1, RET_QK, RET_V_DIM), lambda c: (c, 0, 0)),
                   pl.BlockSpec((1, SSM_STATE, SSM_INNER), lambda c: (c, 0, 0))] + r_out,
        out_shape=[_sds((s, RET_V), F32), _sds((s, SSM_INNER), F32),
                   _sds((nc, RET_QK, RET_V_DIM), F32), _sds((nc, SSM_STATE, SSM_INNER), F32)] + r_shape,
        scratch_shapes=[pltpu.VMEM((RET_QK, RET_V_DIM), F32), pltpu.VMEM((SSM_STATE, SSM_INNER), F32)] + r_scratch,
        compiler_params=_cparams(("arbitrary",)), name=name,
    )(proj, proj, proj, xbc, xbc, xbc, dtraw, cos2, sin2, a_c, dtb, *consts, *r_arrs)


def _scan_bwd(proj, xbc, dtraw, cos2, sin2, a_c, dtb, s_hist, h_hist, dyr, dys, dxs_skip, name, rider=None):
    s = proj.shape[0]
    nc = s // CHUNK
    consts, gam = _scan_tables()
    rv = lambda c: nc - 1 - c
    r_arrs, r_in, r_out, r_shape, r_scratch = _rider_args(rider)

    def body(q_ref, k_ref, v_ref, xs_ref, bm_ref, cm_ref, dt_ref, cos_ref, sin_ref, ac_ref, dtb_ref,
             ri_ref, qd_ref, kd_ref, eye_ref, bdm_ref, last_ref, e_ref, et_ref, tri_ref, trit_ref,
             sh_ref, hh_ref, dyr_ref, dys_ref, dsk_ref,
             dqk_ref, dv_ref, dxbc_ref, ddt_ref, dac_ref, ddtb_ref, dst_sc, dht_sc):
        @pl.when(pl.program_id(0) == 0)
        def _():
            dst_sc[...] = jnp.zeros_like(dst_sc)
            dht_sc[...] = jnp.zeros_like(dht_sc)
            dac_ref[...] = jnp.zeros_like(dac_ref)
            ddtb_ref[...] = jnp.zeros_like(ddtb_ref)

        cos2, sin2 = cos_ref[...], sin_ref[...]
        dst_new = []
        for h in range(RET_HEADS):
            ql = slice(h * RET_QK_DIM, (h + 1) * RET_QK_DIM)
            vl = slice(h * RET_V_DIM, (h + 1) * RET_V_DIM)
            step = functools.partial(_ret_step, r_intra=ri_ref[h], qd=qd_ref[:, ql], kd=kd_ref[:, ql], gamma=gam[h])
            _, vjp = jax.vjp(step, _rope(q_ref[:, ql], cos2, sin2), _rope(k_ref[:, ql], cos2, sin2), v_ref[:, vl],
                             sh_ref[0, ql, :])
            dq, dk, dv, dst = vjp((dyr_ref[:, vl], dst_sc[ql, :]))
            dqk_ref[:, ql] = _rope_t(dq, cos2, sin2).astype(dqk_ref.dtype)
            dqk_ref[:, slice(RET_QK + ql.start, RET_QK + ql.stop)] = _rope_t(dk, cos2, sin2).astype(dqk_ref.dtype)
            dv_ref[:, vl] = dv.astype(dv_ref.dtype)
            dst_new.append(dst)
        dtraw_v, dtb_v, a_c, tri = dt_ref[...], dtb_ref[...], ac_ref[...], tri_ref[...]
        dt, cum_c = _ssd_heads(dtraw_v, dtb_v, a_c, tri)
        both = jnp.concatenate([dt, cum_c], axis=0)
        eye2, bdm, last = eye_ref[...], bdm_ref[...], last_ref[...]
        group = functools.partial(_ssd_group, eye2=eye2, bdm=bdm, last=last)
        d_both = jnp.zeros((2 * CHUNK, LANES), F32)
        dht_new = []
        for g in range(SSM_GROUPS):
            t0, t1 = 2 * g, 2 * g + 1
            e0, e1 = _dot_sel(both, e_ref[:, _tile(t0)]), _dot_sel(both, e_ref[:, _tile(t1)])
            _, vjp = jax.vjp(group, e0[:CHUNK], e1[:CHUNK], e0[CHUNK:], e1[CHUNK:],
                             xs_ref[:, _tile(t0)], xs_ref[:, _tile(t1)], bm_ref[:, _tile(g)], cm_ref[:, _tile(g)],
                             hh_ref[0, :, _tile(t0)], hh_ref[0, :, _tile(t1)])
            (d_dte0, d_dte1, d_cum0, d_cum1, d_xs0, d_xs1, d_bm, d_cm, d_ht0, d_ht1) = vjp(
                (dys_ref[:, _tile(t0)], dht_sc[:, _tile(t0)], dys_ref[:, _tile(t1)], dht_sc[:, _tile(t1)]))
            d_both = d_both + _dot_sel(jnp.concatenate([d_dte0, d_cum0], axis=0), et_ref[_tile(t0), :])
            d_both = d_both + _dot_sel(jnp.concatenate([d_dte1, d_cum1], axis=0), et_ref[_tile(t1), :])
            dxbc_ref[:, _tile(t0)] = d_xs0 + dsk_ref[:, _tile(t0)]
            dxbc_ref[:, _tile(t1)] = d_xs1 + dsk_ref[:, _tile(t1)]
            dxbc_ref[:, _tile(N_LTILE + g)] = d_bm
            dxbc_ref[:, _tile(N_LTILE + SSM_GROUPS + g)] = d_cm
            dht_new += [d_ht0, d_ht1]
        d_da = _dot_sel(d_both[CHUNK:], trit_ref[...], left=True)
        d_dt = d_both[:CHUNK] + d_da * a_c
        d_pre = d_dt * jax.nn.sigmoid(dtraw_v + dtb_v)
        ddt_ref[...] = d_pre
        ddtb_ref[...] += jnp.sum(d_pre, axis=0, keepdims=True)
        dac_ref[...] += jnp.sum(d_da * dt, axis=0, keepdims=True)
        for h in range(RET_HEADS):
            dst_sc[h * RET_QK_DIM:(h + 1) * RET_QK_DIM, :] = dst_new[h]
        for t in range(N_LTILE):
            dht_sc[:, _tile(t)] = dht_new[t]

    in_specs = (_scan_in_specs(nc, True) + [_full((1, DT_PAD)), _full((1, DT_PAD))] + _const_specs(consts)
                + [pl.BlockSpec((1, RET_QK, RET_V_DIM), lambda c: (rv(c), 0, 0)),
                   pl.BlockSpec((1, SSM_STATE, SSM_INNER), lambda c: (rv(c), 0, 0)),
                   pl.BlockSpec((CHUNK, RET_V), lambda c: (rv(c), 0)),
                   pl.BlockSpec((CHUNK, SSM_INNER), lambda c: (rv(c), 0)),
                   pl.BlockSpec((CHUNK, SSM_INNER), lambda c: (rv(c), 0))])
    return pl.pallas_call(
        _with_rider(body, len(in_specs), 6, 2, rider, (nc,)), grid=(nc,),
        in_specs=in_specs + r_in,
        out_specs=[pl.BlockSpec((CHUNK, 2 * RET_QK), lambda c: (rv(c), 0)),
                   pl.BlockSpec((CHUNK, RET_V), lambda c: (rv(c), 0)),
                   pl.BlockSpec((CHUNK, SSM_CONV_DIM), lambda c: (rv(c), 0)),
                   pl.BlockSpec((CHUNK, DT_PAD), lambda c: (rv(c), 0)),
                   _full((1, DT_PAD)), _full((1, DT_PAD))] + r_out,
        out_shape=[_sds((s, 2 * RET_QK), MXU_DTYPE), _sds((s, RET_V), MXU_DTYPE),
                   _sds((s, SSM_CONV_DIM), F32), _sds((s, DT_PAD), F32),
                   _sds((1, DT_PAD), F32), _sds((1, DT_PAD), F32)] + r_shape,
        scratch_shapes=[pltpu.VMEM((RET_QK, RET_V_DIM), F32), pltpu.VMEM((SSM_STATE, SSM_INNER), F32)] + r_scratch,
        compiler_params=_cparams(("arbitrary",)), name=name,
    )(proj, proj, proj, xbc, xbc, xbc, dtraw, cos2, sin2, a_c, dtb, *consts, s_hist, h_hist, dyr, dys, dxs_skip, *r_arrs)


POST_W = 256


def _post_ret(y, g):
    return _rms(y) * _silu(g)


def _post_ssm(y, xs, z, dsk, nw):
    return _rms((y + xs * dsk) * _silu(z)) * nw


def _post_specs(t):
    return [pl.BlockSpec((t, RET_V), lambda i: (i, 0)),
            pl.BlockSpec((t, RET_V), lambda i: (i, COL_G // RET_V)),
            pl.BlockSpec((t, SSM_INNER), lambda i: (i, 0)),
            pl.BlockSpec((t, SSM_INNER), lambda i: (i, 0)),
            pl.BlockSpec((t, SSM_INNER), lambda i: (i, COL_Z // SSM_INNER)),
            _full((1, SSM_INNER)), _full((1, SSM_INNER))]


def _post_fwd(y_ret, proj, y_ssm, xbc, dsk_e, ssm_norm, name):
    s = y_ret.shape[0]
    t = _pick(s, 256)

    def body(yr_ref, g_ref, ys_ref, xs_ref, z_ref, dsk_ref, nw_ref, or_ref, os_ref, ort_ref, ost_ref):
        for h in range(RET_V // POST_W):
            sl = slice(h * POST_W, (h + 1) * POST_W)
            o = _post_ret(yr_ref[:, sl], g_ref[:, sl])
            or_ref[:, sl] = o.astype(or_ref.dtype)
            ort_ref[sl, :] = o.T.astype(ort_ref.dtype)
        for g in range(SSM_INNER // POST_W):
            sl = slice(g * POST_W, (g + 1) * POST_W)
            o = _post_ssm(ys_ref[:, sl], xs_ref[:, sl], z_ref[:, sl], dsk_ref[:, sl], nw_ref[:, sl])
            os_ref[:, sl] = o.astype(os_ref.dtype)
            ost_ref[sl, :] = o.T.astype(ost_ref.dtype)

    return pl.pallas_call(
        body, grid=(s // t,), in_specs=_post_specs(t),
        out_specs=[pl.BlockSpec((t, RET_V), lambda i: (i, 0)), pl.BlockSpec((t, SSM_INNER), lambda i: (i, 0)),
                   pl.BlockSpec((RET_V, t), lambda i: (0, i)), pl.BlockSpec((SSM_INNER, t), lambda i: (0, i))],
        out_shape=[_sds((s, RET_V), MXU_DTYPE), _sds((s, SSM_INNER), MXU_DTYPE),
                   _sds((RET_V, s), MXU_DTYPE), _sds((SSM_INNER, s), MXU_DTYPE)],
        compiler_params=_cparams(("parallel",)), name=name,
    )(y_ret, proj, y_ssm, xbc, proj, dsk_e, ssm_norm.reshape(1, SSM_INNER))


def _post_bwd(y_ret, proj, y_ssm, xbc, dsk_e, ssm_norm, d_or, d_os, name):
    s = y_ret.shape[0]
    t = _pick(s, 256)

    def body(yr_ref, g_ref, ys_ref, xs_ref, z_ref, dsk_ref, nw_ref, dor_ref, dos_ref,
             dyr_ref, dg_ref, dys_ref, dxs_ref, dz_ref, ddsk_ref, dnw_ref):
        @pl.when(pl.program_id(0) == 0)
        def _():
            ddsk_ref[...] = jnp.zeros_like(ddsk_ref)
            dnw_ref[...] = jnp.zeros_like(dnw_ref)

        for h in range(RET_V // POST_W):
            sl = slice(h * POST_W, (h + 1) * POST_W)
            _, vjp = jax.vjp(_post_ret, yr_ref[:, sl], g_ref[:, sl])
            dyr, dg = vjp(dor_ref[:, sl])
            dyr_ref[:, sl] = dyr
            dg_ref[:, sl] = dg.astype(dg_ref.dtype)
        for g in range(SSM_INNER // POST_W):
            sl = slice(g * POST_W, (g + 1) * POST_W)
            _, vjp = jax.vjp(_post_ssm, ys_ref[:, sl], xs_ref[:, sl], z_ref[:, sl], dsk_ref[:, sl], nw_ref[:, sl])
            dy, dxs, dz, ddsk, dnw = vjp(dos_ref[:, sl])
            dys_ref[:, sl] = dy
            dxs_ref[:, sl] = dxs
            dz_ref[:, sl] = dz.astype(dz_ref.dtype)
            ddsk_ref[:, sl] += ddsk
            dnw_ref[:, sl] += dnw

    rowv = pl.BlockSpec((t, RET_V), lambda i: (i, 0))
    rows = pl.BlockSpec((t, SSM_INNER), lambda i: (i, 0))
    return pl.pallas_call(
        body, grid=(s // t,), in_specs=_post_specs(t) + [rowv, rows],
        out_specs=[rowv, rowv, rows, rows, rows, _full((1, SSM_INNER)), _full((1, SSM_INNER))],
        out_shape=[_sds((s, RET_V), F32), _sds((s, RET_V), MXU_DTYPE), _sds((s, SSM_INNER), F32),
                   _sds((s, SSM_INNER), F32), _sds((s, SSM_INNER), MXU_DTYPE),
                   _sds((1, SSM_INNER), F32), _sds((1, SSM_INNER), F32)],
        compiler_params=_cparams(("arbitrary",)), name=name,
    )(y_ret, proj, y_ssm, xbc, proj, dsk_e, ssm_norm.reshape(1, SSM_INNER), d_or, d_os)


def _merge_fn(gr, gs, br, bs, yr, ys):
    return jax.nn.sigmoid(gr + br) * yr + jax.nn.sigmoid(gs + bs) * ys


def _merge_specs(t):
    row = pl.BlockSpec((t, D_MODEL), lambda i: (i, 0))
    return [pl.BlockSpec((t, D_MODEL), lambda i: (i, COL_GATES // D_MODEL)),
            pl.BlockSpec((t, D_MODEL), lambda i: (i, COL_GATES // D_MODEL + 1)),
            pl.BlockSpec((1, D_MODEL), lambda i: (0, 0)), pl.BlockSpec((1, D_MODEL), lambda i: (0, 1)), row, row]


def _merge_fwd(proj, b_gate, br_ret, br_ssm, name):
    s = proj.shape[0]
    t = _pick(s, 512)

    def body(gr_ref, gs_ref, br_ref, bs_ref, yr_ref, ys_ref, o_ref, ot_ref):
        o = _merge_fn(gr_ref[...], gs_ref[...], br_ref[...], bs_ref[...], yr_ref[...], ys_ref[...])
        o_ref[...] = o.astype(o_ref.dtype)
        ot_ref[...] = o.T.astype(ot_ref.dtype)

    bg = b_gate.reshape(1, 2 * D_MODEL)
    return pl.pallas_call(
        body, grid=(s // t,), in_specs=_merge_specs(t),
        out_specs=[pl.BlockSpec((t, D_MODEL), lambda i: (i, 0)), pl.BlockSpec((D_MODEL, t), lambda i: (0, i))],
        out_shape=[_sds((s, D_MODEL), MXU_DTYPE), _sds((D_MODEL, s), MXU_DTYPE)],
        compiler_params=_cparams(("parallel",)), name=name,
    )(proj, proj, bg, bg, br_ret, br_ssm)


def _merge_bwd(proj, b_gate, br_ret, br_ssm, dm, name):
    s = proj.shape[0]
    t = _pick(s, 512)

    def body(gr_ref, gs_ref, br_ref, bs_ref, yr_ref, ys_ref, dm_ref, dgt_ref, db_ref, dyr_ref, dys_ref):
        @pl.when(pl.program_id(0) == 0)
        def _():
            db_ref[...] = jnp.zeros_like(db_ref)

        _, vjp = jax.vjp(_merge_fn, gr_ref[...], gs_ref[...], br_ref[...], bs_ref[...], yr_ref[...], ys_ref[...])
        dgr, dgs, dbr, dbs, dyr, dys = vjp(dm_ref[...])
        dgt_ref[:, :D_MODEL] = dgr.astype(dgt_ref.dtype)
        dgt_ref[:, D_MODEL:] = dgs.astype(dgt_ref.dtype)
        db_ref[:, :D_MODEL] += dbr
        db_ref[:, D_MODEL:] += dbs
        dyr_ref[...] = dyr.astype(dyr_ref.dtype)
        dys_ref[...] = dys.astype(dys_ref.dtype)

    bg = b_gate.reshape(1, 2 * D_MODEL)
    row = pl.BlockSpec((t, D_MODEL), lambda i: (i, 0))
    return pl.pallas_call(
        body, grid=(s // t,), in_specs=_merge_specs(t) + [row],
        out_specs=[pl.BlockSpec((t, 2 * D_MODEL), lambda i: (i, 0)), _full((1, 2 * D_MODEL)), row, row],
        out_shape=[_sds((s, 2 * D_MODEL), MXU_DTYPE), _sds((1, 2 * D_MODEL), F32),
                   _sds((s, D_MODEL), MXU_DTYPE), _sds((s, D_MODEL), MXU_DTYPE)],
        compiler_params=_cparams(("arbitrary",)), name=name,
    )(proj, proj, bg, bg, br_ret, br_ssm, dm)


def _attn_head(q, k, v):
    sc = _mm(q, k, 'nt') * (XA_HEAD_DIM ** -0.5)
    e = jnp.exp(sc - lax.stop_gradient(jnp.max(sc, axis=-1, keepdims=True)))
    p = e / jnp.sum(e, axis=-1, keepdims=True)
    return _mm(p, v, 'nn')


def _attn_fwd(q, kv, name):
    s = q.shape[0]
    m = kv.shape[0]
    t = _pick(s, 512)

    def body(q_ref, kv_ref, o_ref, ot_ref):
        for h in range(XA_HEADS):
            sl = slice(h * XA_HEAD_DIM, (h + 1) * XA_HEAD_DIM)
            vl = slice(D_MODEL + h * XA_HEAD_DIM, D_MODEL + (h + 1) * XA_HEAD_DIM)
            o = _attn_head(q_ref[:, sl], kv_ref[:, sl], kv_ref[:, vl])
            o_ref[:, sl] = o.astype(o_ref.dtype)
            ot_ref[sl, :] = o.T.astype(ot_ref.dtype)

    return pl.pallas_call(
        body, grid=(s // t,),
        in_specs=[pl.BlockSpec((t, D_MODEL), lambda i: (i, 0)), _full((m, 2 * D_MODEL))],
        out_specs=[pl.BlockSpec((t, D_MODEL), lambda i: (i, 0)), pl.BlockSpec((D_MODEL, t), lambda i: (0, i))],
        out_shape=[_sds((s, D_MODEL), MXU_DTYPE), _sds((D_MODEL, s), MXU_DTYPE)],
        compiler_params=_cparams(("parallel",)), name=name,
    )(q, kv)


def _attn_bwd(q, kv, d_o, name):
    s = q.shape[0]
    m = kv.shape[0]
    t = _pick(s, 512)

    def body(q_ref, kv_ref, do_ref, dq_ref, dkv_ref):
        @pl.when(pl.program_id(0) == 0)
        def _():
            dkv_ref[...] = jnp.zeros_like(dkv_ref)

        for h in range(XA_HEADS):
            sl = slice(h * XA_HEAD_DIM, (h + 1) * XA_HEAD_DIM)
            vl = slice(D_MODEL + h * XA_HEAD_DIM, D_MODEL + (h + 1) * XA_HEAD_DIM)
            _, vjp = jax.vjp(_attn_head, q_ref[:, sl], kv_ref[:, sl], kv_ref[:, vl])
            dq, dk, dv = vjp(do_ref[:, sl])
            dq_ref[:, sl] = dq.astype(dq_ref.dtype)
            dkv_ref[:, sl] += dk
            dkv_ref[:, vl] += dv

    row = pl.BlockSpec((t, D_MODEL), lambda i: (i, 0))
    return pl.pallas_call(
        body, grid=(s // t,), in_specs=[row, _full((m, 2 * D_MODEL)), row],
        out_specs=[row, _full((m, 2 * D_MODEL))],
        out_shape=[_sds((s, D_MODEL), MXU_DTYPE), _sds((m, 2 * D_MODEL), F32)],
        compiler_params=_cparams(("arbitrary",)), name=name,
    )(q, kv, d_o)


def _loss_head(x, w, target, name):
    s, d = x.shape
    t = _pick(s, 512)

    def body(x_ref, w_ref, t_ref, loss_ref, dx_ref, dxb_ref, dw_ref):
        @pl.when(pl.program_id(0) == 0)
        def _():
            loss_ref[...] = jnp.zeros_like(loss_ref)
            dw_ref[...] = jnp.zeros_like(dw_ref)

        y, vjp = jax.vjp(_rmsnorm_fn, x_ref[...], w_ref[...])
        err = y - t_ref[...]
        loss_ref[...] += 0.5 * jnp.sum(jnp.sum(err * err, axis=-1, keepdims=True), axis=0, keepdims=True) / d
        dx, dw = vjp(err * (1.0 / d))
        dx_ref[...] = dx
        dxb_ref[...] = dx.astype(dxb_ref.dtype)
        dw_ref[...] += dw

    row = pl.BlockSpec((t, d), lambda i: (i, 0))
    return pl.pallas_call(
        body, grid=(s // t,), in_specs=[row, _full((1, d)), row],
        out_specs=[_full((1, LANES)), row, row, _full((1, d))],
        out_shape=[_sds((1, LANES), F32), _sds((s, d), F32), _sds((s, d), MXU_DTYPE), _sds((1, d), F32)],
        compiler_params=_cparams(("arbitrary",)), name=name,
    )(x, w.reshape(1, d), target)


def _epi_sqrelu(acc):
    r = jnp.maximum(acc, 0.0)
    return acc, r * r, r * r


def _epi_sqrelu_bwd(acc, a):
    return (acc * (2.0 * jnp.maximum(a, 0.0)),)


def _rope_tables(positions):
    inv_freq = ROPE_THETA ** (-jnp.arange(0, RET_QK_DIM, 2, dtype=F32) / RET_QK_DIM)
    ang = positions.astype(F32)[:, None] * inv_freq
    cos, sin = jnp.cos(ang), jnp.sin(ang)
    return jnp.concatenate([cos, cos], axis=1), jnp.concatenate([-sin, sin], axis=1)


W_IN_ORIG = (('q', 0, 512), ('k', 512, 1024), ('v', 1024, 2048), ('g', 2048, 3072), ('z', 3072, 5120),
             ('xbc', 5120, 9216), ('dt', 9216, 9248), ('gates', 9248, 11296))
W_IN_MAIN_ORDER = ('z', 'xbc', 'gates', 'v', 'g', 'q', 'k')
W_IN_SHARD = IN_DIM // N_DEV


def _shard_segments(lo, hi):
    segs = []
    for j in range(lo // W_IN_SHARD, (hi - 1) // W_IN_SHARD + 1):
        segs.append((j, max(lo, j * W_IN_SHARD) - j * W_IN_SHARD, min(hi, (j + 1) * W_IN_SHARD) - j * W_IN_SHARD))
    return segs


def _w_in_from_shards(g):
    rng = {name: (lo, hi) for name, lo, hi in W_IN_ORIG}
    cols = [g[j][:, a:b] for name in W_IN_MAIN_ORDER for j, a, b in _shard_segments(*rng[name])]
    (j, a, b), = _shard_segments(*rng['dt'])
    return jnp.concatenate(cols, axis=1), jnp.pad(g[j][:, a:b], ((0, 0), (0, DT_PAD - SSM_HEADS)))


def _w_in_grad_blocks(d, d_dt):
    src_of = {'q': ('qk', 0), 'k': ('qk', RET_QK)}
    blocks = []
    for j in range(N_DEV):
        lo_j, hi_j = j * W_IN_SHARD, (j + 1) * W_IN_SHARD
        cols = []
        for name, lo, hi in W_IN_ORIG:
            a, b = max(lo, lo_j), min(hi, hi_j)
            if a >= b:
                continue
            if name == 'dt':
                cols.append(d_dt[:, a - lo:b - lo])
            else:
                key, off = src_of.get(name, (name, 0))
                cols.append(d[key][:, off + a - lo:off + b - lo])
        blocks.append(jnp.concatenate(cols, axis=1))
    return blocks


def _lanes_of_heads(v):
    return jnp.repeat(v, SSM_HEAD_DIM).reshape(1, SSM_INNER)


def _heads_of_lanes(v):
    return v.reshape(SSM_HEADS, SSM_HEAD_DIM).sum(axis=1)


def _layer_fwd(x, mem, cos2, sin2, p, l, later_blocks, rider=None):
    n = lambda s: f"{s}_l{l}"
    sv = {'x0': x}
    u, u_t = _rmsnorm(x, p['norm_mix'], n("norm_mix"))
    proj, gathered = _matmul(u, p['w_in_main'], 'nn', n("in_proj"), rider=_AllGather(later_blocks))
    p = {**p, **_full_weights(LATER, gathered)}
    dtraw = _matmul(u, p['w_in_dt'], 'nn', n("in_proj_dt"))
    xbc = _conv_fwd(proj, p['conv_w'], p['conv_b'], n("conv"))
    a_c = jnp.pad(-jnp.exp(p['a_log']), (0, DT_PAD - SSM_HEADS)).reshape(1, DT_PAD)
    dtb = jnp.pad(p['dt_bias'], (0, DT_PAD - SSM_HEADS)).reshape(1, DT_PAD)
    y_ret, y_ssm, s_hist, h_hist, *delivered = _scan_fwd(proj, xbc, dtraw, cos2, sin2, a_c, dtb, n("scan"), rider)
    dsk_e = _lanes_of_heads(p['d_skip'])
    o_ret, o_ssm, o_ret_t, o_ssm_t = _post_fwd(y_ret, proj, y_ssm, xbc, dsk_e, p['ssm_norm'], n("post"))
    br_ret = _matmul(o_ret, p['w_br_ret'], 'nn', n("br_ret"))
    br_ssm = _matmul(o_ssm, p['w_br_ssm'], 'nn', n("br_ssm"))
    merged, merged_t = _merge_fwd(proj, p['b_gate'], br_ret, br_ssm, n("merge"))
    x1 = _matmul(merged, p['w_out'], 'nn', n("w_out"), extras=(x,), epi=_epi_add)
    sv.update(u_t=u_t, proj=proj, dtraw=dtraw, xbc=xbc, a_c=a_c, dtb=dtb, y_ret=y_ret, y_ssm=y_ssm, s_hist=s_hist,
              h_hist=h_hist, dsk_e=dsk_e, o_ret_t=o_ret_t, o_ssm_t=o_ssm_t, br_ret=br_ret, br_ssm=br_ssm,
              merged_t=merged_t, x1=x1)
    hq, hq_t = _rmsnorm(x1, p['norm_xa'], n("norm_xa"))
    memn, _ = _rmsnorm(mem, p['norm_mem'], n("norm_mem"))
    q = _matmul(hq, p['xa_wq'], 'nn', n("xa_q"))
    kv = _matmul(memn, p['xa_wkv'], 'nn', n("xa_kv"))
    o, o_t = _attn_fwd(q, kv, n("attn"))
    x2 = _matmul(o, p['xa_wo'], 'nn', n("xa_o"), extras=(x1,), epi=_epi_add)
    sv.update(hq_t=hq_t, memn=memn, q=q, kv=kv, o_t=o_t, x2=x2)
    hm, hm_t = _rmsnorm(x2, p['norm_mlp'], n("norm_mlp"))
    a, act, act_t = _matmul(hm, p['mlp_w1'], 'nn', n("mlp_1"), epi=_epi_sqrelu, out_dtypes=(F32, MXU_DTYPE, MXU_DTYPE),
                            out_t=(False, False, True))
    x3 = _matmul(act, p['mlp_w2'], 'nn', n("mlp_2"), extras=(x2,), epi=_epi_add)
    sv.update(hm_t=hm_t, a=a, act_t=act_t)
    return x3, sv, p, delivered


def _layer_bwd(dx, dxb, mem, cos2, sin2, p, sv, l, pending, c_idx):
    n = lambda s: f"{s}_bwd_l{l}"
    gd = (MXU_DTYPE,)
    g = {}
    g['mlp_w2'] = _matmul(sv['act_t'], dxb, 'nn', n("mlp_2_dw"), out_dtypes=gd)
    da = _matmul(dxb, p['mlp_w2'], 'nt', n("mlp_2_dx"), extras=(sv['a'],), epi=_epi_sqrelu_bwd, out_dtypes=(MXU_DTYPE,))
    g['mlp_w1'] = _matmul(sv['hm_t'], da, 'nn', n("mlp_1_dw"), out_dtypes=gd)
    dhm = _matmul(da, p['mlp_w1'], 'nt', n("mlp_1_dx"))
    dx2, dx2b, g['norm_mlp'] = _rmsnorm_bwd(sv['x2'], p['norm_mlp'], dhm, dx, n("norm_mlp"))
    g['xa_wo'] = _matmul(sv['o_t'], dx2b, 'nn', n("xa_o_dw"), out_dtypes=gd)
    d_o = _matmul(dx2b, p['xa_wo'], 'nt', n("xa_o_dx"))
    dq, dkv = _attn_bwd(sv['q'], sv['kv'], d_o, n("attn"))
    g['xa_wq'] = _matmul(sv['hq_t'], dq, 'nn', n("xa_q_dw"), out_dtypes=gd)
    dhq = _matmul(dq, p['xa_wq'], 'nt', n("xa_q_dx"))
    g['xa_wkv'] = _matmul(sv['memn'], dkv, 'tn', n("xa_kv_dw"), out_dtypes=gd)
    dmemn = _matmul(dkv, p['xa_wkv'], 'nt', n("xa_kv_dx"))
    _, _, g['norm_mem'] = _rmsnorm_bwd(mem, p['norm_mem'], dmemn, None, n("norm_mem"))
    dx1, dx1b, g['norm_xa'] = _rmsnorm_bwd(sv['x1'], p['norm_xa'], dhq, dx2, n("norm_xa"))
    g['w_out'] = _matmul(sv['merged_t'], dx1b, 'nn', n("w_out_dw"), out_dtypes=gd)
    dmerged = _matmul(dx1b, p['w_out'], 'nt', n("w_out_dx"))
    dgates, g['b_gate'], dbr_ret, dbr_ssm = _merge_bwd(sv['proj'], p['b_gate'], sv['br_ret'], sv['br_ssm'], dmerged, n("merge"))
    g['w_br_ret'] = _matmul(sv['o_ret_t'], dbr_ret, 'nn', n("br_ret_dw"), out_dtypes=gd)
    g['w_br_ssm'] = _matmul(sv['o_ssm_t'], dbr_ssm, 'nn', n("br_ssm_dw"), out_dtypes=gd)
    d_or = _matmul(dbr_ret, p['w_br_ret'], 'nt', n("br_ret_dx"))
    d_os = _matmul(dbr_ssm, p['w_br_ssm'], 'nt', n("br_ssm_dx"))
    later_sums = _core_sums(LATER, g, c_idx, l)
    rider = _ExchangeChips(list(pending) + later_sums)
    dyr, dg, dys, dxs_skip, dz, ddsk_e, g['ssm_norm'] = _post_bwd(
        sv['y_ret'], sv['proj'], sv['y_ssm'], sv['xbc'], sv['dsk_e'], p['ssm_norm'], d_or, d_os, n("post"))
    g['d_skip'] = _heads_of_lanes(ddsk_e)
    dqk_r, dv_r, dxbc_act, ddtraw, dac, ddtb, *delivered = _scan_bwd(
        sv['proj'], sv['xbc'], sv['dtraw'], cos2, sin2, sv['a_c'], sv['dtb'], sv['s_hist'], sv['h_hist'],
        dyr, dys, dxs_skip, n("scan"), rider)
    g['a_log'] = dac[0, :SSM_HEADS] * (-jnp.exp(p['a_log']))
    g['dt_bias'] = ddtb[0, :SSM_HEADS]
    dxbc_raw, g['conv_w'], g['conv_b'] = _conv_bwd(sv['proj'], p['conv_w'], p['conv_b'], dxbc_act, n("conv"))
    pieces = {'z': dz, 'xbc': dxbc_raw, 'gates': dgates, 'v': dv_r, 'g': dg, 'qk': dqk_r}
    d_w = {k: _matmul(sv['u_t'], pc, 'nn', n(f"in_proj_dw_{k}"), out_dtypes=gd) for k, pc in pieces.items()}
    d_dt = _matmul(sv['u_t'], ddtraw, 'nn', n("in_proj_dt_dw"), out_dtypes=gd)
    g['w_in'] = _w_in_grad_blocks(d_w, d_dt)
    du_dt = _matmul(ddtraw, p['w_in_dt'], 'nt', n("in_proj_dt_dx"))
    du = _matmul_nt_pieces(list(pieces.values()), p['w_in_main'], n("in_proj_dx"), extras=(du_dt,), epi=_epi_add)
    dx0, dx0b, g['norm_mix'] = _rmsnorm_bwd(sv['x0'], p['norm_mix'], du, dx1, n("norm_mix"))
    return dx0, dx0b, g, delivered[:len(pending)], delivered[len(pending):], _core_sums(FIRST, g, c_idx, l)


def _full_weights(names, gathered):
    p = {}
    for k, g in zip(names, gathered):
        if k == 'w_in':
            p['w_in_main'], p['w_in_dt'] = _w_in_from_shards(g)
        elif k in COL_SHARDED:
            p[k] = jnp.concatenate([g[j] for j in range(N_DEV)], axis=1)
        else:
            p[k] = g.reshape(-1, g.shape[-1])
    return p


def _grad_scatter(k, g):
    if k == 'w_in':
        blocks = g
    elif k in COL_SHARDED:
        c = g.shape[1] // N_DEV
        blocks = [g[:, j * c:(j + 1) * c] for j in range(N_DEV)]
    else:
        r = g.shape[0] // N_DEV
        blocks = [g[j * r:(j + 1) * r] for j in range(N_DEV)]
    return jnp.stack([jnp.stack([blocks[2 * chip + core] for chip in range(4)]) for core in range(2)])


def _core_sums(names, g, c_idx, l):
    by_core = [_grad_scatter(k, g[k]) for k in names]
    from_sibling = _exchange_cores(by_core, f"grad_exchange_cores_{names[0]}_l{l}")
    return [_add_halves(a, o, c_idx, f"grad_add_cores_{k}_l{l}") for k, a, o in zip(names, by_core, from_sibling)]


def _step(x, mem, positions, small, blocks, loss_target):
    cos2, sin2 = _rope_tables(positions)
    first = _run_exchange(_AllGather([blocks[0][k] for k in FIRST]), "all_gather_first_l0")
    saved, layers = [], []
    for l in range(DEPTH):
        p = {k: small[k][l] for k in SMALL if k != 'norm_final'}
        p.update(_full_weights(FIRST, first))
        rider = _AllGather([blocks[l + 1][k] for k in FIRST]) if l + 1 < DEPTH else None
        x, sv, p, first = _layer_fwd(x, mem, cos2, sin2, p, l, [blocks[l][k] for k in LATER], rider)
        saved.append(sv)
        layers.append(p)
    loss, dx, dxb, dnf = _loss_head(x, small['norm_final'], loss_target, "loss_head")
    c_idx = lax.axis_index("c").astype(jnp.int32).reshape(1)
    grads, by_chip, pending = [None] * DEPTH, [dict() for _ in range(DEPTH)], []
    for l in reversed(range(DEPTH)):
        dx, dxb, grads[l], got_first, got_later, pending_next = _layer_bwd(
            dx, dxb, mem, cos2, sin2, layers[l], saved[l], l, pending, c_idx)
        if pending:
            by_chip[l + 1].update(zip(FIRST, got_first))
        by_chip[l].update(zip(LATER, got_later))
        pending = pending_next
    by_chip[0].update(zip(FIRST, _run_exchange(_ExchangeChips(pending), "grad_exchange_chips_first_l0")))
    small_g = {}
    for k in SMALL:
        small_g[k] = dnf.reshape(D_MODEL) if k == 'norm_final' else [grads[l][k].reshape(small[k].shape[1:]) for l in range(DEPTH)]
    return loss, dx, small_g, by_chip


MESH = pl.DeviceIdType.MESH
ANY_SPEC = pl.BlockSpec(memory_space=pl.ANY)


def _mesh_pos():
    return lax.axis_index("x"), lax.axis_index("y"), lax.axis_index("c")


def _other_chips(x, y):
    return [(1 - x, y), (x, 1 - y), (1 - x, 1 - y)]


class _AllGather:
    def __init__(self, arrs):
        self.arrs = list(arrs)
        na = self.n = len(self.arrs)
        self.out_shape = [_sds((N_DEV,) + a.shape, a.dtype) for a in self.arrs]
        self.scratch = [pltpu.SemaphoreType.DMA((na, 7)), pltpu.SemaphoreType.DMA((na, 7)), pltpu.SemaphoreType.DMA((na,))]

    def _copies(self, x_refs, o_refs, sems):
        send_sems, recv_sems, local_sems = sems
        x, y, c = _mesh_pos()
        me, sib = (x, y, c), (x, y, 1 - c)
        chips = _other_chips(x, y)

        def copy(a, k, block, to, src=None):
            dst = o_refs[a].at[4 * block[0] + 2 * block[1] + block[2]]
            return pltpu.make_async_remote_copy(src_ref=dst if src is None else src, dst_ref=dst,
                                                send_sem=send_sems.at[a, k], recv_sem=recv_sems.at[a, k],
                                                device_id=to, device_id_type=MESH)

        mine = [pltpu.make_async_copy(x_refs[a], o_refs[a].at[4 * x + 2 * y + c], local_sems.at[a]) for a in range(self.n)]
        first = []
        for a in range(self.n):
            first.append(copy(a, 0, me, sib, src=x_refs[a]))
            first += [copy(a, 1 + j, me, (*chip, c), src=x_refs[a]) for j, chip in enumerate(chips)]
        return copy, mine, first, me, sib, chips, c

    def start(self, x_refs, o_refs, sems):
        _, mine, first, *_ = self._copies(x_refs, o_refs, sems)
        for cp in mine + first:
            cp.start()

    def finish(self, x_refs, o_refs, sems):
        copy, mine, first, me, sib, chips, c = self._copies(x_refs, o_refs, sems)
        passed = []
        for a in range(self.n):
            for j, chip in enumerate(chips):
                copy(a, 1 + j, (*chip, c), me).wait_recv()
                cp = copy(a, 4 + j, (*chip, c), sib)
                cp.start()
                passed.append(cp)
        for a in range(self.n):
            copy(a, 0, sib, me).wait_recv()
            for j, chip in enumerate(chips):
                copy(a, 4 + j, (*chip, 1 - c), me).wait_recv()
        for cp in first + passed:
            cp.wait_send()
        for cp in mine:
            cp.wait()


class _ExchangeChips:
    def __init__(self, arrs):
        self.arrs = list(arrs)
        na = self.n = len(self.arrs)
        self.out_shape = [_sds(a.shape, a.dtype) for a in self.arrs]
        self.scratch = [pltpu.SemaphoreType.DMA((na, 3)), pltpu.SemaphoreType.DMA((na, 3)), pltpu.SemaphoreType.DMA((na,))]

    def _copies(self, a_refs, o_refs, sems):
        send_sems, recv_sems, local_sems = sems
        x, y, c = _mesh_pos()
        my_chip = 2 * x + y
        chips = _other_chips(x, y)
        mine = [pltpu.make_async_copy(a_refs[a].at[my_chip], o_refs[a].at[my_chip], local_sems.at[a]) for a in range(self.n)]
        sends = [pltpu.make_async_remote_copy(src_ref=a_refs[a].at[2 * px + py], dst_ref=o_refs[a].at[my_chip],
                                              send_sem=send_sems.at[a, j], recv_sem=recv_sems.at[a, j],
                                              device_id=(px, py, c), device_id_type=MESH)
                 for a in range(self.n) for j, (px, py) in enumerate(chips)]
        recvs = [pltpu.make_async_remote_copy(src_ref=a_refs[a].at[2 * px + py], dst_ref=o_refs[a].at[2 * px + py],
                                              send_sem=send_sems.at[a, j], recv_sem=recv_sems.at[a, j],
                                              device_id=(px, py, c), device_id_type=MESH)
                 for a in range(self.n) for j, (px, py) in enumerate(chips)]
        return mine, sends, recvs

    def start(self, a_refs, o_refs, sems):
        mine, sends, _ = self._copies(a_refs, o_refs, sems)
        for cp in mine + sends:
            cp.start()

    def finish(self, a_refs, o_refs, sems):
        mine, sends, recvs = self._copies(a_refs, o_refs, sems)
        for cp in recvs:
            cp.wait_recv()
        for cp in sends:
            cp.wait_send()
        for cp in mine:
            cp.wait()


def _run_exchange(ex, name):
    na = ex.n

    def body(*refs):
        i_refs, o_refs, sems = refs[:na], refs[na:2 * na], refs[2 * na:]
        ex.start(i_refs, o_refs, sems)
        ex.finish(i_refs, o_refs, sems)

    return pl.pallas_call(body, in_specs=[ANY_SPEC] * na, out_specs=[ANY_SPEC] * na, out_shape=ex.out_shape,
                          scratch_shapes=ex.scratch, name=name)(*ex.arrs)


def _exchange_cores(arrs, name):
    na = len(arrs)

    def body(*refs):
        a_refs, o_refs = refs[:na], refs[na:2 * na]
        send_sems, recv_sems = refs[2 * na:]
        x, y, c = _mesh_pos()
        cps = [pltpu.make_async_remote_copy(src_ref=a_refs[a].at[1 - c], dst_ref=o_refs[a], send_sem=send_sems.at[a],
                                            recv_sem=recv_sems.at[a], device_id=(x, y, 1 - c), device_id_type=MESH)
               for a in range(na)]
        for cp in cps:
            cp.start()
        for cp in cps:
            cp.wait()

    return pl.pallas_call(
        body, in_specs=[ANY_SPEC] * na, out_specs=[ANY_SPEC] * na,
        out_shape=[_sds(a.shape[1:], a.dtype) for a in arrs],
        scratch_shapes=[pltpu.SemaphoreType.DMA((na,)), pltpu.SemaphoreType.DMA((na,))],
        name=name,
    )(*arrs)


def _as_rows(a, lead):
    return a.reshape(a.shape[:lead] + (-1, a.shape[-1]))


def _add_halves(a, other, c_idx, name):
    a3, o2 = _as_rows(a, 1), _as_rows(other, 0)
    rows, cols = o2.shape
    tr = _pick(rows, 256)

    def body(c_ref, a_ref, o_ref, out_ref):
        out_ref[...] = (a_ref[0].astype(F32) + o_ref[...].astype(F32)).astype(out_ref.dtype)

    out = pl.pallas_call(
        body,
        grid_spec=pltpu.PrefetchScalarGridSpec(
            num_scalar_prefetch=1, grid=(rows // tr,),
            in_specs=[pl.BlockSpec((1, tr, cols), lambda i, c_ref: (c_ref[0], i, 0)),
                      pl.BlockSpec((tr, cols), lambda i, c_ref: (i, 0))],
            out_specs=pl.BlockSpec((tr, cols), lambda i, c_ref: (i, 0))),
        out_shape=_sds((rows, cols), a.dtype), compiler_params=_cparams(("parallel",)), name=name,
    )(c_idx, a3, o2)
    return out.reshape(other.shape)


def _all_reduce_small(v, name):
    r = v.shape[0]

    def body(v_ref, o_ref, slots, send_sems, recv_sems):
        x, y, c = _mesh_pos()
        me = 4 * x + 2 * y + c
        slots[me] = v_ref[...]
        cps = []
        for k in range(1, N_DEV):
            px = 1 - x if k & 4 else x
            py = 1 - y if k & 2 else y
            pc = 1 - c if k & 1 else c
            cps.append(pltpu.make_async_remote_copy(src_ref=v_ref, dst_ref=slots.at[me], send_sem=send_sems.at[k - 1],
                                                    recv_sem=recv_sems.at[k - 1], device_id=(px, py, pc), device_id_type=MESH))
        for cp in cps:
            cp.start()
        for cp in cps:
            cp.wait()
        acc = slots[0]
        for d in range(1, N_DEV):
            acc = acc + slots[d]
        o_ref[...] = acc

    vm = pl.BlockSpec(memory_space=pltpu.VMEM)
    return pl.pallas_call(
        body, in_specs=[vm], out_specs=vm, out_shape=_sds((r, LANES), F32),
        scratch_shapes=[pltpu.VMEM((N_DEV, r, LANES), F32), pltpu.SemaphoreType.DMA((N_DEV - 1,)),
                        pltpu.SemaphoreType.DMA((N_DEV - 1,))],
        compiler_params=pltpu.CompilerParams(vmem_limit_bytes=VMEM_LIMIT_BYTES), name=name,
    )(v)


def _adamw(w, g_slots, m, v, name):
    depth, rows, cols = w.shape
    ns = g_slots.shape[0]
    tr = _pick(rows, 256 if cols <= 1024 else 128)

    def body(w_ref, g_ref, m_ref, v_ref, go_ref, d_ref, mo_ref, vo_ref):
        g = g_ref[0, 0].astype(F32)
        for i in range(1, ns):
            g = g + g_ref[i, 0].astype(F32)
        m_new = ADAM_B1 * m_ref[0] + (1.0 - ADAM_B1) * g
        v_new = ADAM_B2 * v_ref[0] + (1.0 - ADAM_B2) * (g * g)
        m_hat = m_new / (1.0 - ADAM_B1 ** ADAM_STEP)
        v_hat = v_new / (1.0 - ADAM_B2 ** ADAM_STEP)
        go_ref[0] = g
        d_ref[0] = -ADAM_LR * (m_hat / (jnp.sqrt(v_hat) + ADAM_EPS) + ADAM_WD * w_ref[0])
        mo_ref[0] = m_new
        vo_ref[0] = v_new

    blk = pl.BlockSpec((1, tr, cols), lambda l, i: (l, i, 0))
    return pl.pallas_call(
        body, grid=(depth, rows // tr),
        in_specs=[blk, pl.BlockSpec((ns, 1, tr, cols), lambda l, i: (0, l, i, 0)), blk, blk],
        out_specs=[blk] * 4, out_shape=[_sds(w.shape, F32)] * 4,
        compiler_params=_cparams(("parallel", "parallel")), name=name,
    )(w, g_slots, m, v)


_ARG_NAMES = (['x', 'mem', 'positions'] + WEIGHTS + ['loss_target'] + ['m_' + n for n in WEIGHTS]
              + ['v_' + n for n in WEIGHTS])


PACK_TILE = 8 * LANES


def _pack_rows(parts):
    blocks = []
    for part in parts:
        flat = part.reshape(-1)
        pad = (-flat.shape[0]) % PACK_TILE
        blocks.append((jnp.pad(flat, (0, pad)) if pad else flat).reshape(-1, LANES))
    return jnp.concatenate(blocks, axis=0)


def _unpack_rows(packed, shapes):
    out, off = [], 0
    for shp in shapes:
        n = int(np.prod(shp))
        rows = -(-n // PACK_TILE) * 8
        out.append(packed[off:off + rows].reshape(-1)[:n].reshape(shp))
        off += rows
    return out


def kernel(x, mem, positions, norm_mix, w_in, b_gate, conv_w, conv_b, dt_bias, a_log, d_skip, ssm_norm, w_br_ret, w_br_ssm, w_out, norm_xa, norm_mem, xa_wq, xa_wkv, xa_wo, norm_mlp, mlp_w1, mlp_w2, norm_final, loss_target, m_norm_mix, m_w_in, m_b_gate, m_conv_w, m_conv_b, m_dt_bias, m_a_log, m_d_skip, m_ssm_norm, m_w_br_ret, m_w_br_ssm, m_w_out, m_norm_xa, m_norm_mem, m_xa_wq, m_xa_wkv, m_xa_wo, m_norm_mlp, m_mlp_w1, m_mlp_w2, m_norm_final, v_norm_mix, v_w_in, v_b_gate, v_conv_w, v_conv_b, v_dt_bias, v_a_log, v_d_skip, v_ssm_norm, v_w_br_ret, v_w_br_ssm, v_w_out, v_norm_xa, v_norm_mem, v_xa_wq, v_xa_wkv, v_xa_wo, v_norm_mlp, v_mlp_w1, v_mlp_w2, v_norm_final):
    d = dict(zip(_ARG_NAMES, (x, mem, positions, norm_mix, w_in, b_gate, conv_w, conv_b, dt_bias, a_log, d_skip, ssm_norm, w_br_ret, w_br_ssm, w_out, norm_xa, norm_mem, xa_wq, xa_wkv, xa_wo, norm_mlp, mlp_w1, mlp_w2, norm_final, loss_target, m_norm_mix, m_w_in, m_b_gate, m_conv_w, m_conv_b, m_dt_bias, m_a_log, m_d_skip, m_ssm_norm, m_w_br_ret, m_w_br_ssm, m_w_out, m_norm_xa, m_norm_mem, m_xa_wq, m_xa_wkv, m_xa_wo, m_norm_mlp, m_mlp_w1, m_mlp_w2, m_norm_final, v_norm_mix, v_w_in, v_b_gate, v_conv_w, v_conv_b, v_dt_bias, v_a_log, v_d_skip, v_ssm_norm, v_w_br_ret, v_w_br_ssm, v_w_out, v_norm_xa, v_norm_mem, v_xa_wq, v_xa_wkv, v_xa_wo, v_norm_mlp, v_mlp_w1, v_mlp_w2, v_norm_final)))
    blocks = [{k: d[k][l] if k == 'conv_w' else d[k][l].astype(MXU_DTYPE) for k in SHARDED} for l in range(DEPTH)]
    small = {k: d[k] for k in SMALL}
    loss, grad_x, grads, by_chip_l = _step(d['x'][0], d['mem'][0], d['positions'][0], small, blocks, d['loss_target'][0])
    by_chip = [jnp.stack([by_chip_l[l][k] for l in range(DEPTH)], axis=1) for k in SHARDED]
    small_g = [grads[k] if k == 'norm_final' else jnp.stack(grads[k]) for k in SMALL]
    total = _all_reduce_small(_pack_rows([loss] + small_g), "all_reduce_small")
    loss_out = total[0, 0]
    res = {}
    for k, g4 in zip(SHARDED, by_chip):
        res[k] = _adamw(d[k], g4, d['m_' + k], d['v_' + k], f"adamw_{k}")
    small_shapes = [d[k].shape for k in SMALL]
    pk = lambda pre: _pack_rows([d[pre + k] for k in SMALL])
    outs = _adamw(pk('')[None], total[8:][None, None], pk('m_')[None], pk('v_')[None], "adamw_small")
    unpacked = [_unpack_rows(o[0], small_shapes) for o in outs]
    for i, k in enumerate(SMALL):
        res[k] = [unpacked[j][i] for j in range(4)]
    return (loss_out, grad_x[None], *[res[k][0] for k in WEIGHTS], *[res[k][1] for k in WEIGHTS],
            *[res[k][2] for k in WEIGHTS], *[res[k][3] for k in WEIGHTS])
```

```python
import functools

import numpy as np
import jax
import jax.numpy as jnp
from jax import lax
from jax.experimental import pallas as pl
from jax.experimental.pallas import tpu as pltpu

F32 = jnp.float32
MXU_DTYPE = jnp.bfloat16
VMEM_LIMIT_BYTES = 56 * 1024 * 1024
LANES = 128
N_DEV = 8

D_MODEL = 1024
DEPTH = 4
CHUNK = 64
EPS = 1e-6
RET_HEADS, RET_QK_DIM, RET_V_DIM = 4, 128, 256
RET_QK, RET_V = 512, 1024
ROPE_THETA = 10000.0
SSM_INNER, SSM_HEAD_DIM, SSM_HEADS, SSM_GROUPS, SSM_STATE, SSM_CONV = 2048, 64, 32, 8, 128, 4
SSM_BC = 1024
SSM_CONV_DIM = 4096
IN_DIM = 11296
XA_HEADS, XA_HEAD_DIM = 4, 256
D_FF = 4096
ADAM_LR, ADAM_B1, ADAM_B2, ADAM_EPS, ADAM_WD, ADAM_STEP = 0.001, 0.9, 0.999, 1e-08, 0.01, 10

PROJ_W = 11264
COL_Z, COL_XBC, COL_GATES, COL_V, COL_G, COL_Q, COL_K = 0, 2048, 6144, 8192, 9216, 10240, 10752
DT_PAD = 128
N_LTILE = SSM_INNER // LANES

WEIGHTS = ['norm_mix', 'w_in', 'b_gate', 'conv_w', 'conv_b', 'dt_bias', 'a_log', 'd_skip', 'ssm_norm',
           'w_br_ret', 'w_br_ssm', 'w_out', 'norm_xa', 'norm_mem', 'xa_wq', 'xa_wkv', 'xa_wo', 'norm_mlp',
           'mlp_w1', 'mlp_w2', 'norm_final']
COL_SHARDED = ['w_in', 'conv_w', 'xa_wkv', 'mlp_w1']
ROW_SHARDED = ['w_br_ret', 'w_br_ssm', 'w_out', 'xa_wq', 'xa_wo', 'mlp_w2']
SHARDED = COL_SHARDED + ROW_SHARDED
FIRST = ['w_in', 'conv_w']
LATER = [n for n in SHARDED if n not in FIRST]
SMALL = [n for n in WEIGHTS if n not in SHARDED]


def _cparams(sem=None):
    return pltpu.CompilerParams(dimension_semantics=sem, vmem_limit_bytes=VMEM_LIMIT_BYTES)


def _sds(shape, dtype):
    return jax.ShapeDtypeStruct(shape, dtype)


def _full(shape):
    nd = len(shape)
    return pl.BlockSpec(shape, lambda *_: (0,) * nd)


_DIMS = {'nn': (((1,), (0,)), ((), ())), 'nt': (((1,), (1,)), ((), ())), 'tn': (((0,), (0,)), ((), ()))}


def _dot(a, b, mode='nn'):
    return lax.dot_general(a.astype(MXU_DTYPE), b.astype(MXU_DTYPE), _DIMS[mode], preferred_element_type=F32)


@functools.partial(jax.custom_vjp, nondiff_argnums=(2,))
def _mm(a, b, mode):
    return _dot(a, b, mode)


def _mm_fwd(a, b, mode):
    return _dot(a, b, mode), (a, b)


def _mm_bwd(mode, res, g):
    a, b = res
    if mode == 'nn':
        return _dot(g, b, 'nt'), _dot(a, g, 'tn')
    if mode == 'nt':
        return _dot(g, b, 'nn'), _dot(g, a, 'tn')
    return _dot(b, g, 'nt'), _dot(a, g, 'nn')


_mm.defvjp(_mm_fwd, _mm_bwd)


def _split3(x):
    hi = x.astype(jnp.bfloat16)
    r1 = x - hi.astype(F32)
    mid = r1.astype(jnp.bfloat16)
    lo = (r1 - mid.astype(F32)).astype(jnp.bfloat16)
    return hi, mid, lo


def _dot_sel(x, c, left=False):
    dims = _DIMS['nn']
    parts = _split3(x)
    if left:
        outs = [lax.dot_general(c, p, dims, preferred_element_type=F32) for p in parts]
    else:
        outs = [lax.dot_general(p, c, dims, preferred_element_type=F32) for p in parts]
    return (outs[0] + outs[1]) + outs[2]


def _silu(x):
    return x * jax.nn.sigmoid(x)


def _softplus(x):
    pos = x > 0.0
    return jnp.where(pos, x, 0.0) + jnp.log1p(jnp.exp(jnp.where(pos, -x, x)))


def _rms(x):
    return x * lax.rsqrt(jnp.mean(x * x, axis=-1, keepdims=True) + EPS)


def _pick(n, pref):
    t = min(n, pref)
    while n % t:
        t //= 2
    return t


def _with_rider(core, n_in, n_out, n_scratch, rider, grid):
    if rider is None:
        return core
    na, nrs = rider.n, len(rider.scratch)

    def at(step_of):
        cond = pl.program_id(0) == step_of(grid[0])
        for ax in range(1, len(grid)):
            cond = cond & (pl.program_id(ax) == step_of(grid[ax]))
        return cond

    def body(*refs):
        ci, ri = refs[:n_in], refs[n_in:n_in + na]
        co, ro = refs[n_in + na:n_in + na + n_out], refs[n_in + na + n_out:n_in + 2 * na + n_out]
        sc = refs[n_in + 2 * na + n_out:]
        cs, rs = sc[:n_scratch], sc[n_scratch:]
        assert len(rs) == nrs

        @pl.when(at(lambda n: 0))
        def _():
            rider.start(ri, ro, rs)

        core(*ci, *co, *cs)

        @pl.when(at(lambda n: n - 1))
        def _():
            rider.finish(ri, ro, rs)

    return body


def _rider_args(rider):
    if rider is None:
        return [], [], [], [], []
    return list(rider.arrs), [ANY_SPEC] * rider.n, [ANY_SPEC] * rider.n, list(rider.out_shape), list(rider.scratch)


MATMUL_TK_MAX = 4096


def _tiles(mode, m, n, k):
    tm, tn = (512, 1024) if mode == 'nt' else (1024, 512)
    tk = k
    while tk > MATMUL_TK_MAX or k % tk or tk % LANES:
        tk -= LANES
    return _pick(m, tm), _pick(n, tn), tk


def _matmul(a, b, mode, name, *, extras=(), epi=None, out_dtypes=(F32,), out_t=None, tiles=None, rider=None):
    if mode == 'nn':
        (m, k), (k2, n) = a.shape, b.shape
    elif mode == 'nt':
        (m, k), (n, k2) = a.shape, b.shape
    else:
        (k, m), (k2, n) = a.shape, b.shape
    assert k == k2, (a.shape, b.shape, mode)
    tm, tn, tk = tiles or _tiles(mode, m, n, k)
    nk = k // tk
    n_ex, n_out = len(extras), len(out_dtypes)
    out_t = out_t or (False,) * n_out

    def finish(acc, ex_refs, o_refs):
        outs = epi(acc, *[r[...] for r in ex_refs]) if epi is not None else (acc,)
        for o_ref, o, tr in zip(o_refs, outs, out_t):
            o_ref[...] = (o.T if tr else o).astype(o_ref.dtype)

    def body(*refs):
        a_ref, b_ref = refs[0], refs[1]
        ex_refs = refs[2:2 + n_ex]
        o_refs = refs[2 + n_ex:2 + n_ex + n_out]
        if nk == 1:
            finish(_dot(a_ref[...], b_ref[...], mode), ex_refs, o_refs)
            return
        acc_ref = refs[-1]
        kk = pl.program_id(2)

        @pl.when(kk == 0)
        def _():
            acc_ref[...] = jnp.zeros_like(acc_ref)

        acc_ref[...] += _dot(a_ref[...], b_ref[...], mode)

        @pl.when(kk == nk - 1)
        def _():
            finish(acc_ref[...], ex_refs, o_refs)

    if mode == 'nn':
        a_spec = pl.BlockSpec((tm, tk), lambda i, j, kk: (i, kk))
        b_spec = pl.BlockSpec((tk, tn), lambda i, j, kk: (kk, j))
    elif mode == 'nt':
        a_spec = pl.BlockSpec((tm, tk), lambda i, j, kk: (i, kk))
        b_spec = pl.BlockSpec((tn, tk), lambda i, j, kk: (j, kk))
    else:
        a_spec = pl.BlockSpec((tk, tm), lambda i, j, kk: (kk, i))
        b_spec = pl.BlockSpec((tk, tn), lambda i, j, kk: (kk, j))
    mn_spec = pl.BlockSpec((tm, tn), lambda i, j, kk: (i, j))
    nm_spec = pl.BlockSpec((tn, tm), lambda i, j, kk: (j, i))
    grid = (m // tm, n // tn, nk)
    r_arrs, r_in, r_out, r_shape, r_scratch = _rider_args(rider)
    outs = pl.pallas_call(
        _with_rider(body, 2 + n_ex, n_out, int(nk > 1), rider, grid), grid=grid,
        in_specs=[a_spec, b_spec] + [mn_spec] * n_ex + r_in,
        out_specs=[nm_spec if tr else mn_spec for tr in out_t] + r_out,
        out_shape=[_sds((n, m) if tr else (m, n), dt) for dt, tr in zip(out_dtypes, out_t)] + r_shape,
        scratch_shapes=([pltpu.VMEM((tm, tn), F32)] if nk > 1 else []) + r_scratch,
        compiler_params=_cparams(("arbitrary",) * 3 if rider is not None else ("parallel", "parallel", "arbitrary")),
        name=name,
    )(a, b, *extras, *r_arrs)
    res = outs[0] if n_out == 1 else outs[:n_out]
    return (res, outs[n_out:]) if rider is not None else res


def _epi_add(acc, r):
    return (acc + r,)


PIECE_TK = 1024


def _matmul_nt_pieces(pieces, b, name, *, extras=(), epi=None, out_dtypes=(F32,)):
    m, n = pieces[0].shape[0], b.shape[0]
    tm, tn, tk = _pick(m, 512), _pick(n, 1024), PIECE_TK
    steps = [pc.shape[1] // tk for pc in pieces]
    starts = [sum(steps[:i]) for i in range(len(pieces))]
    nk = sum(steps)
    assert b.shape[1] == nk * tk and all(pc.shape[1] % tk == 0 for pc in pieces)
    n_pc, n_ex, n_out = len(pieces), len(extras), len(out_dtypes)

    def body(*refs):
        pc_refs, b_ref = refs[:n_pc], refs[n_pc]
        ex_refs = refs[n_pc + 1:n_pc + 1 + n_ex]
        o_refs = refs[n_pc + 1 + n_ex:n_pc + 1 + n_ex + n_out]
        acc_ref = refs[-1]
        kk = pl.program_id(2)

        @pl.when(kk == 0)
        def _():
            acc_ref[...] = jnp.zeros_like(acc_ref)

        for pc_ref, st, ns in zip(pc_refs, starts, steps):
            @pl.when((kk >= st) & (kk < st + ns))
            def _(pc_ref=pc_ref):
                acc_ref[...] += _dot(pc_ref[...], b_ref[...], 'nt')

        @pl.when(kk == nk - 1)
        def _():
            acc = acc_ref[...]
            outs = epi(acc, *[r[...] for r in ex_refs]) if epi is not None else (acc,)
            for o_ref, o in zip(o_refs, outs):
                o_ref[...] = o.astype(o_ref.dtype)

    pc_specs = [pl.BlockSpec((tm, tk), lambda i, j, kk, st=st, ns=ns: (i, jnp.clip(kk - st, 0, ns - 1)))
                for st, ns in zip(starts, steps)]
    mn_spec = pl.BlockSpec((tm, tn), lambda i, j, kk: (i, j))
    outs = pl.pallas_call(
        body, grid=(m // tm, n // tn, nk),
        in_specs=pc_specs + [pl.BlockSpec((tn, tk), lambda i, j, kk: (j, kk))] + [mn_spec] * n_ex,
        out_specs=[mn_spec] * n_out, out_shape=[_sds((m, n), dt) for dt in out_dtypes],
        scratch_shapes=[pltpu.VMEM((tm, tn), F32)],
        compiler_params=_cparams(("parallel", "parallel", "arbitrary")), name=name,
    )(*pieces, b, *extras)
    return outs[0] if n_out == 1 else outs


def _rmsnorm_fn(x, w):
    return _rms(x) * w


def _rmsnorm(x, w, name):
    s, d = x.shape
    t = _pick(s, 512)

    def body(x_ref, w_ref, o_ref, ot_ref):
        y = _rmsnorm_fn(x_ref[...], w_ref[...])
        o_ref[...] = y.astype(o_ref.dtype)
        ot_ref[...] = y.T.astype(ot_ref.dtype)

    return pl.pallas_call(
        body, grid=(s // t,),
        in_specs=[pl.BlockSpec((t, d), lambda i: (i, 0)), _full((1, d))],
        out_specs=[pl.BlockSpec((t, d), lambda i: (i, 0)), pl.BlockSpec((d, t), lambda i: (0, i))],
        out_shape=[_sds((s, d), MXU_DTYPE), _sds((d, s), MXU_DTYPE)],
        compiler_params=_cparams(("parallel",)), name=name,
    )(x, w.reshape(1, d))


def _rmsnorm_bwd(x, w, du, dres, name):
    s, d = x.shape
    t = _pick(s, 512)
    has_res = dres is not None

    def body(*refs):
        if has_res:
            x_ref, w_ref, du_ref, dres_ref, dx_ref, dxb_ref, dw_ref = refs
        else:
            x_ref, w_ref, du_ref, dx_ref, dxb_ref, dw_ref = refs
        _, vjp = jax.vjp(_rmsnorm_fn, x_ref[...], w_ref[...])
        dx, dw = vjp(du_ref[...])
        dx = dx + dres_ref[...] if has_res else dx
        dx_ref[...] = dx
        dxb_ref[...] = dx.astype(dxb_ref.dtype)

        @pl.when(pl.program_id(0) == 0)
        def _():
            dw_ref[...] = jnp.zeros_like(dw_ref)

        dw_ref[...] += dw

    row = pl.BlockSpec((t, d), lambda i: (i, 0))
    return pl.pallas_call(
        body, grid=(s // t,),
        in_specs=[row, _full((1, d)), row] + ([row] if has_res else []),
        out_specs=[row, row, _full((1, d))],
        out_shape=[_sds((s, d), F32), _sds((s, d), MXU_DTYPE), _sds((1, d), F32)],
        compiler_params=_cparams(("arbitrary",)), name=name,
    )(x, w.reshape(1, d), du, *([dres] if has_res else []))


CONV_CW = 2048
CONV_HALO = 16


def _shifted(cat):
    return [cat] + [pltpu.roll(cat, sft, axis=0) for sft in (1, 2, 3)]


def _conv_taps(shifted, w, n_rows, off):
    acc = shifted[0][off:off + n_rows, :] * w[3:4, :]
    for sft in (1, 2, 3):
        acc = acc + shifted[sft][off:off + n_rows, :] * w[3 - sft:4 - sft, :]
    return acc


def _conv_fwd(proj, conv_w, conv_b, name):
    s = proj.shape[0]
    tr = _pick(s, 256)
    hb = tr // CONV_HALO
    col0 = COL_XBC // CONV_CW

    def body(prev_ref, x_ref, w_ref, b_ref, o_ref):
        i = pl.program_id(1)
        prev = jnp.where(i == 0, 0.0, prev_ref[...].astype(F32))
        cat = jnp.concatenate([prev, x_ref[...].astype(F32)], axis=0)
        o_ref[...] = _silu(_conv_taps(_shifted(cat), w_ref[...], tr, CONV_HALO) + b_ref[...])

    return pl.pallas_call(
        body, grid=(SSM_CONV_DIM // CONV_CW, s // tr),
        in_specs=[pl.BlockSpec((CONV_HALO, CONV_CW), lambda j, i: (jnp.maximum(i * hb - 1, 0), j + col0)),
                  pl.BlockSpec((tr, CONV_CW), lambda j, i: (i, j + col0)),
                  pl.BlockSpec((SSM_CONV, CONV_CW), lambda j, i: (0, j)),
                  pl.BlockSpec((1, CONV_CW), lambda j, i: (0, j))],
        out_specs=pl.BlockSpec((tr, CONV_CW), lambda j, i: (i, j)),
        out_shape=_sds((s, SSM_CONV_DIM), F32),
        compiler_params=_cparams(("parallel", "parallel")), name=name,
    )(proj, proj, conv_w, conv_b.reshape(1, SSM_CONV_DIM))


def _conv_bwd(proj, conv_w, conv_b, dact, name):
    s = proj.shape[0]
    tr = _pick(s, 256)
    hb = tr // CONV_HALO
    nb = s // CONV_HALO
    nt = s // tr
    col0 = COL_XBC // CONV_CW
    h = CONV_HALO

    def body(prev_ref, x_ref, next_ref, w_ref, b_ref, da_ref, dan_ref, dx_ref, dw_ref, db_ref):
        i = pl.program_id(1)
        w = w_ref[...]
        prev = jnp.where(i == 0, 0.0, prev_ref[...].astype(F32))
        cat = jnp.concatenate([prev, x_ref[...].astype(F32), next_ref[...].astype(F32)], axis=0)
        shifted = _shifted(cat)
        pre = _conv_taps(shifted, w, tr + h, h) + b_ref[...]
        dact_n = jnp.where(i == nt - 1, 0.0, dan_ref[...])
        dact_ext = jnp.concatenate([da_ref[...], dact_n], axis=0)
        sg = jax.nn.sigmoid(pre)
        dpre = dact_ext * (sg * (1.0 + pre * (1.0 - sg)))
        dx = dpre[:tr, :] * w[3:4, :]
        for sft in (1, 2, 3):
            dx = dx + pltpu.roll(dpre, tr + h - sft, axis=0)[:tr, :] * w[3 - sft:4 - sft, :]
        dx_ref[...] = dx.astype(dx_ref.dtype)

        @pl.when(i == 0)
        def _():
            dw_ref[...] = jnp.zeros_like(dw_ref)
            db_ref[...] = jnp.zeros_like(db_ref)

        dp = dpre[:tr, :]
        db_ref[...] += jnp.sum(dp, axis=0, keepdims=True)
        for r, sft in enumerate((3, 2, 1, 0)):
            dw_ref[r:r + 1, :] += jnp.sum(dp * shifted[sft][h:h + tr, :], axis=0, keepdims=True)

    return pl.pallas_call(
        body, grid=(SSM_CONV_DIM // CONV_CW, nt),
        in_specs=[pl.BlockSpec((h, CONV_CW), lambda j, i: (jnp.maximum(i * hb - 1, 0), j + col0)),
                  pl.BlockSpec((tr, CONV_CW), lambda j, i: (i, j + col0)),
                  pl.BlockSpec((h, CONV_CW), lambda j, i: (jnp.minimum((i + 1) * hb, nb - 1), j + col0)),
                  pl.BlockSpec((SSM_CONV, CONV_CW), lambda j, i: (0, j)),
                  pl.BlockSpec((1, CONV_CW), lambda j, i: (0, j)),
                  pl.BlockSpec((tr, CONV_CW), lambda j, i: (i, j)),
                  pl.BlockSpec((h, CONV_CW), lambda j, i: (jnp.minimum((i + 1) * hb, nb - 1), j))],
        out_specs=[pl.BlockSpec((tr, CONV_CW), lambda j, i: (i, j)),
                   pl.BlockSpec((SSM_CONV, CONV_CW), lambda j, i: (0, j)),
                   pl.BlockSpec((1, CONV_CW), lambda j, i: (0, j))],
        out_shape=[_sds((s, SSM_CONV_DIM), MXU_DTYPE), _sds((SSM_CONV, SSM_CONV_DIM), F32), _sds((1, SSM_CONV_DIM), F32)],
        compiler_params=_cparams(("parallel", "arbitrary")), name=name,
    )(proj, proj, proj, conv_w, conv_b.reshape(1, SSM_CONV_DIM), dact, dact)


def _scan_tables():
    idx = np.arange(CHUNK, dtype=np.float32)
    lg = np.log1p(-(2.0 ** (-5.0 - np.arange(RET_HEADS, dtype=np.float32)))).astype(np.float32)
    rel = np.abs(idx[:, None] - idx[None, :])
    r_intra = np.exp(lg[:, None, None] * rel).astype(np.float32)
    qd = np.exp(lg[None, :] * (idx[:, None] + 1.0)).astype(np.float32)
    kd = np.exp(lg[None, :] * (CHUNK - 1.0 - idx[:, None])).astype(np.float32)
    gam = [float(v) for v in np.exp(lg * CHUNK).astype(np.float32)]
    qd_e = np.repeat(qd, RET_QK_DIM, axis=1)
    kd_e = np.repeat(kd, RET_QK_DIM, axis=1)
    e = np.zeros((DT_PAD, SSM_INNER), np.float32)
    for hh in range(SSM_HEADS):
        e[hh, hh * SSM_HEAD_DIM:(hh + 1) * SSM_HEAD_DIM] = 1.0
    tri = np.tril(np.ones((CHUNK, CHUNK), np.float32))
    eye2 = np.concatenate([np.eye(CHUNK, dtype=np.float32)] * 2, axis=1)
    bdm = np.kron(np.eye(2, dtype=np.float32), np.ones((CHUNK, CHUNK), np.float32))
    last = np.zeros((CHUNK, LANES), np.float32)
    last[CHUNK - 1, :] = 1.0
    f32c = [jnp.asarray(c) for c in (r_intra, qd_e, kd_e, eye2, bdm, last)]
    sel = [jnp.asarray(c, jnp.bfloat16) for c in (e, e.T.copy(), tri, tri.T.copy())]
    return f32c + sel, gam


def _rope(t, cos2, sin2):
    return t * cos2 + pltpu.roll(t, RET_QK_DIM // 2, axis=1) * sin2


def _rope_t(d, cos2, sin2):
    return d * cos2 + pltpu.roll(d * sin2, RET_QK_DIM // 2, axis=1)


def _ret_step(q, k, v, st, r_intra, qd, kd, gamma):
    k = k * (RET_QK_DIM ** -0.5)
    sc = _mm(q, k, 'nt') * r_intra
    y = _mm(sc, v, 'nn') + _mm(q * qd, st, 'nn')
    st_new = st * gamma + _mm(k * kd, v, 'tn')
    return y, st_new


def _ssd_heads(dtraw, dtb, a_c, tri):
    dt = _softplus(dtraw + dtb)
    return dt, _dot_sel(dt * a_c, tri, left=True)


def _ssd_group(dte0, dte1, cum0, cum1, xs0, xs1, bm, cm, ht0, ht1, eye2, bdm, last):
    cbp = _mm(cm, jnp.concatenate([bm, bm], axis=0), 'nt')
    outs = []
    for dte, cum, xs, ht in ((dte0, cum0, xs0, ht0), (dte1, cum1, xs1, ht1)):
        r = jnp.sum(cum * eye2, axis=0, keepdims=True)
        dlt = cum - r
        seg = jnp.exp(jnp.where(dlt > 0.0, -dlt, dlt))
        xdt = xs * dte
        bd = jnp.concatenate([xdt, xdt], axis=0) * bdm
        clast = jnp.sum(cum * last, axis=0, keepdims=True)
        y = _mm(cbp * seg, bd, 'nn') + jnp.exp(cum) * _mm(cm, ht, 'nn')
        ht_new = jnp.exp(clast) * ht + _mm(bm, xdt * jnp.exp(clast - cum), 'tn')
        outs += [y, ht_new]
    return tuple(outs)


def _scan_in_specs(nc, rev):
    ch = (lambda c: nc - 1 - c) if rev else (lambda c: c)
    col = lambda w, blk: pl.BlockSpec((CHUNK, w), lambda c: (ch(c), blk))
    return [col(RET_QK, COL_Q // RET_QK), col(RET_QK, COL_K // RET_QK), col(RET_V, COL_V // RET_V),
            col(SSM_INNER, 0), col(SSM_BC, 2), col(SSM_BC, 3),
            col(DT_PAD, 0), col(LANES, 0), col(LANES, 0)]


def _const_specs(consts):
    return [_full(c.shape) for c in consts]


def _tile(t):
    return slice(t * LANES, (t + 1) * LANES)


def _scan_fwd(proj, xbc, dtraw, cos2, sin2, a_c, dtb, name, rider=None):
    s = proj.shape[0]
    nc = s // CHUNK
    consts, gam = _scan_tables()
    r_arrs, r_in, r_out, r_shape, r_scratch = _rider_args(rider)

    def body(q_ref, k_ref, v_ref, xs_ref, bm_ref, cm_ref, dt_ref, cos_ref, sin_ref, ac_ref, dtb_ref,
             ri_ref, qd_ref, kd_ref, eye_ref, bdm_ref, last_ref, e_ref, et_ref, tri_ref, trit_ref,
             yr_ref, ys_ref, sh_ref, hh_ref, st_sc, ht_sc):
        @pl.when(pl.program_id(0) == 0)
        def _():
            st_sc[...] = jnp.zeros_like(st_sc)
            ht_sc[...] = jnp.zeros_like(ht_sc)

        sh_ref[0] = st_sc[...]
        hh_ref[0] = ht_sc[...]
        cos2, sin2 = cos_ref[...], sin_ref[...]
        st_new = []
        for h in range(RET_HEADS):
            ql = slice(h * RET_QK_DIM, (h + 1) * RET_QK_DIM)
            vl = slice(h * RET_V_DIM, (h + 1) * RET_V_DIM)
            y, st_h = _ret_step(_rope(q_ref[:, ql].astype(F32), cos2, sin2), _rope(k_ref[:, ql].astype(F32), cos2, sin2),
                                v_ref[:, vl].astype(F32), st_sc[ql, :], ri_ref[h], qd_ref[:, ql], kd_ref[:, ql], gam[h])
            yr_ref[:, vl] = y
            st_new.append(st_h)
        dt, cum_c = _ssd_heads(dt_ref[...], dtb_ref[...], ac_ref[...], tri_ref[...])
        both = jnp.concatenate([dt, cum_c], axis=0)
        eye2, bdm, last = eye_ref[...], bdm_ref[...], last_ref[...]
        ht_new = []
        for g in range(SSM_GROUPS):
            t0, t1 = 2 * g, 2 * g + 1
            e0, e1 = _dot_sel(both, e_ref[:, _tile(t0)]), _dot_sel(both, e_ref[:, _tile(t1)])
            y0, h0, y1, h1 = _ssd_group(e0[:CHUNK], e1[:CHUNK], e0[CHUNK:], e1[CHUNK:],
                                        xs_ref[:, _tile(t0)], xs_ref[:, _tile(t1)], bm_ref[:, _tile(g)],
                                        cm_ref[:, _tile(g)], ht_sc[:, _tile(t0)], ht_sc[:, _tile(t1)], eye2, bdm, last)
            ys_ref[:, _tile(t0)] = y0
            ys_ref[:, _tile(t1)] = y1
            ht_new += [h0, h1]
        for h in range(RET_HEADS):
            st_sc[h * RET_QK_DIM:(h + 1) * RET_QK_DIM, :] = st_new[h]
        for t in range(N_LTILE):
            ht_sc[:, _tile(t)] = ht_new[t]

    in_specs = _scan_in_specs(nc, False) + [_full((1, DT_PAD)), _full((1, DT_PAD))] + _const_specs(consts)
    return pl.pallas_call(
        _with_rider(body, len(in_specs), 4, 2, rider, (nc,)), grid=(nc,),
        in_specs=in_specs + r_in,
        out_specs=[pl.BlockSpec((CHUNK, RET_V), lambda c: (c, 0)),
                   pl.BlockSpec((CHUNK, SSM_INNER), lambda c: (c, 0)),
                   pl.BlockSpec((1, RET_QK, RET_V_DIM), lambda c: (c, 0, 0)),
                   pl.BlockSpec((1, SSM_STATE, SSM_INNER), lambda c: (c, 0, 0))] + r_out,
        out_shape=[_sds((s, RET_V), F32), _sds((s, SSM_INNER), F32),
                   _sds((nc, RET_QK, RET_V_DIM), F32), _sds((nc, SSM_STATE, SSM_INNER), F32)] + r_shape,
        scratch_shapes=[pltpu.VMEM((RET_QK, RET_V_DIM), F32), pltpu.VMEM((SSM_STATE, SSM_INNER), F32)] + r_scratch,
        compiler_params=_cparams(("arbitrary",)), name=name,
    )(proj, proj, proj, xbc, xbc, xbc, dtraw, cos2, sin2, a_c, dtb, *consts, *r_arrs)


def _scan_bwd(proj, xbc, dtraw, cos2, sin2, a_c, dtb, s_hist, h_hist, dyr, dys, dxs_skip, name, rider=None):
    s = proj.shape[0]
    nc = s // CHUNK
    consts, gam = _scan_tables()
    rv = lambda c: nc - 1 - c
    r_arrs, r_in, r_out, r_shape, r_scratch = _rider_args(rider)

    def body(q_ref, k_ref, v_ref, xs_ref, bm_ref, cm_ref, dt_ref, cos_ref, sin_ref, ac_ref, dtb_ref,
             ri_ref, qd_ref, kd_ref, eye_ref, bdm_ref, last_ref, e_ref, et_ref, tri_ref, trit_ref,
             sh_ref, hh_ref, dyr_ref, dys_ref, dsk_ref,
             dqk_ref, dv_ref, dxbc_ref, ddt_ref, dac_ref, ddtb_ref, dst_sc, dht_sc):
        @pl.when(pl.program_id(0) == 0)
        def _():
            dst_sc[...] = jnp.zeros_like(dst_sc)
            dht_sc[...] = jnp.zeros_like(dht_sc)
            dac_ref[...] = jnp.zeros_like(dac_ref)
            ddtb_ref[...] = jnp.zeros_like(ddtb_ref)

        cos2, sin2 = cos_ref[...], sin_ref[...]
        dst_new = []
        for h in range(RET_HEADS):
            ql = slice(h * RET_QK_DIM, (h + 1) * RET_QK_DIM)
            vl = slice(h * RET_V_DIM, (h + 1) * RET_V_DIM)
            step = functools.partial(_ret_step, r_intra=ri_ref[h], qd=qd_ref[:, ql], kd=kd_ref[:, ql], gamma=gam[h])
            _, vjp = jax.vjp(step, _rope(q_ref[:, ql].astype(F32), cos2, sin2), _rope(k_ref[:, ql].astype(F32), cos2, sin2),
                             v_ref[:, vl].astype(F32), sh_ref[0, ql, :])
            dq, dk, dv, dst = vjp((dyr_ref[:, vl], dst_sc[ql, :]))
            dqk_ref[:, ql] = _rope_t(dq, cos2, sin2).astype(dqk_ref.dtype)
            dqk_ref[:, slice(RET_QK + ql.start, RET_QK + ql.stop)] = _rope_t(dk, cos2, sin2).astype(dqk_ref.dtype)
            dv_ref[:, vl] = dv.astype(dv_ref.dtype)
            dst_new.append(dst)
        dtraw_v, dtb_v, a_c, tri = dt_ref[...], dtb_ref[...], ac_ref[...], tri_ref[...]
        dt, cum_c = _ssd_heads(dtraw_v, dtb_v, a_c, tri)
        both = jnp.concatenate([dt, cum_c], axis=0)
        eye2, bdm, last = eye_ref[...], bdm_ref[...], last_ref[...]
        group = functools.partial(_ssd_group, eye2=eye2, bdm=bdm, last=last)
        d_both = jnp.zeros((2 * CHUNK, LANES), F32)
        dht_new = []
        for g in range(SSM_GROUPS):
            t0, t1 = 2 * g, 2 * g + 1
            e0, e1 = _dot_sel(both, e_ref[:, _tile(t0)]), _dot_sel(both, e_ref[:, _tile(t1)])
            _, vjp = jax.vjp(group, e0[:CHUNK], e1[:CHUNK], e0[CHUNK:], e1[CHUNK:],
                             xs_ref[:, _tile(t0)], xs_ref[:, _tile(t1)], bm_ref[:, _tile(g)], cm_ref[:, _tile(g)],
                             hh_ref[0, :, _tile(t0)], hh_ref[0, :, _tile(t1)])
            (d_dte0, d_dte1, d_cum0, d_cum1, d_xs0, d_xs1, d_bm, d_cm, d_ht0, d_ht1) = vjp(
                (dys_ref[:, _tile(t0)], dht_sc[:, _tile(t0)], dys_ref[:, _tile(t1)], dht_sc[:, _tile(t1)]))
            d_both = d_both + _dot_sel(jnp.concatenate([d_dte0, d_cum0], axis=0), et_ref[_tile(t0), :])
            d_both = d_both + _dot_sel(jnp.concatenate([d_dte1, d_cum1], axis=0), et_ref[_tile(t1), :])
            dxbc_ref[:, _tile(t0)] = d_xs0 + dsk_ref[:, _tile(t0)]
            dxbc_ref[:, _tile(t1)] = d_xs1 + dsk_ref[:, _tile(t1)]
            dxbc_ref[:, _tile(N_LTILE + g)] = d_bm
            dxbc_ref[:, _tile(N_LTILE + SSM_GROUPS + g)] = d_cm
            dht_new += [d_ht0, d_ht1]
        d_da = _dot_sel(d_both[CHUNK:], trit_ref[...], left=True)
        d_dt = d_both[:CHUNK] + d_da * a_c
        d_pre = d_dt * jax.nn.sigmoid(dtraw_v + dtb_v)
        ddt_ref[...] = d_pre
        ddtb_ref[...] += jnp.sum(d_pre, axis=0, keepdims=True)
        dac_ref[...] += jnp.sum(d_da * dt, axis=0, keepdims=True)
        for h in range(RET_HEADS):
            dst_sc[h * RET_QK_DIM:(h + 1) * RET_QK_DIM, :] = dst_new[h]
        for t in range(N_LTILE):
            dht_sc[:, _tile(t)] = dht_new[t]

    in_specs = (_scan_in_specs(nc, True) + [_full((1, DT_PAD)), _full((1, DT_PAD))] + _const_specs(consts)
                + [pl.BlockSpec((1, RET_QK, RET_V_DIM), lambda c: (rv(c), 0, 0)),
                   pl.BlockSpec((1, SSM_STATE, SSM_INNER), lambda c: (rv(c), 0, 0)),
                   pl.BlockSpec((CHUNK, RET_V), lambda c: (rv(c), 0)),
                   pl.BlockSpec((CHUNK, SSM_INNER), lambda c: (rv(c), 0)),
                   pl.BlockSpec((CHUNK, SSM_INNER), lambda c: (rv(c), 0))])
    return pl.pallas_call(
        _with_rider(body, len(in_specs), 6, 2, rider, (nc,)), grid=(nc,),
        in_specs=in_specs + r_in,
        out_specs=[pl.BlockSpec((CHUNK, 2 * RET_QK), lambda c: (rv(c), 0)),
                   pl.BlockSpec((CHUNK, RET_V), lambda c: (rv(c), 0)),
                   pl.BlockSpec((CHUNK, SSM_CONV_DIM), lambda c: (rv(c), 0)),
                   pl.BlockSpec((CHUNK, DT_PAD), lambda c: (rv(c), 0)),
                   _full((1, DT_PAD)), _full((1, DT_PAD))] + r_out,
        out_shape=[_sds((s, 2 * RET_QK), MXU_DTYPE), _sds((s, RET_V), MXU_DTYPE),
                   _sds((s, SSM_CONV_DIM), F32), _sds((s, DT_PAD), F32),
                   _sds((1, DT_PAD), F32), _sds((1, DT_PAD), F32)] + r_shape,
        scratch_shapes=[pltpu.VMEM((RET_QK, RET_V_DIM), F32), pltpu.VMEM((SSM_STATE, SSM_INNER), F32)] + r_scratch,
        compiler_params=_cparams(("arbitrary",)), name=name,
    )(proj, proj, proj, xbc, xbc, xbc, dtraw, cos2, sin2, a_c, dtb, *consts, s_hist, h_hist, dyr, dys, dxs_skip, *r_arrs)


POST_W = 256


def _post_ret(y, g):
    return _rms(y) * _silu(g)


def _post_ssm(y, xs, z, dsk, nw):
    return _rms((y + xs * dsk) * _silu(z)) * nw


def _post_specs(t):
    return [pl.BlockSpec((t, RET_V), lambda i: (i, 0)),
            pl.BlockSpec((t, RET_V), lambda i: (i, COL_G // RET_V)),
            pl.BlockSpec((t, SSM_INNER), lambda i: (i, 0)),
            pl.BlockSpec((t, SSM_INNER), lambda i: (i, 0)),
            pl.BlockSpec((t, SSM_INNER), lambda i: (i, COL_Z // SSM_INNER)),
            _full((1, SSM_INNER)), _full((1, SSM_INNER))]


def _post_fwd(y_ret, proj, y_ssm, xbc, dsk_e, ssm_norm, name):
    s = y_ret.shape[0]
    t = _pick(s, 256)

    def body(yr_ref, g_ref, ys_ref, xs_ref, z_ref, dsk_ref, nw_ref, or_ref, os_ref, ort_ref, ost_ref):
        for h in range(RET_V // POST_W):
            sl = slice(h * POST_W, (h + 1) * POST_W)
            o = _post_ret(yr_ref[:, sl], g_ref[:, sl].astype(F32))
            or_ref[:, sl] = o.astype(or_ref.dtype)
            ort_ref[sl, :] = o.T.astype(ort_ref.dtype)
        for g in range(SSM_INNER // POST_W):
            sl = slice(g * POST_W, (g + 1) * POST_W)
            o = _post_ssm(ys_ref[:, sl], xs_ref[:, sl], z_ref[:, sl].astype(F32), dsk_ref[:, sl], nw_ref[:, sl])
            os_ref[:, sl] = o.astype(os_ref.dtype)
            ost_ref[sl, :] = o.T.astype(ost_ref.dtype)

    return pl.pallas_call(
        body, grid=(s // t,), in_specs=_post_specs(t),
        out_specs=[pl.BlockSpec((t, RET_V), lambda i: (i, 0)), pl.BlockSpec((t, SSM_INNER), lambda i: (i, 0)),
                   pl.BlockSpec((RET_V, t), lambda i: (0, i)), pl.BlockSpec((SSM_INNER, t), lambda i: (0, i))],
        out_shape=[_sds((s, RET_V), MXU_DTYPE), _sds((s, SSM_INNER), MXU_DTYPE),
                   _sds((RET_V, s), MXU_DTYPE), _sds((SSM_INNER, s), MXU_DTYPE)],
        compiler_params=_cparams(("parallel",)), name=name,
    )(y_ret, proj, y_ssm, xbc, proj, dsk_e, ssm_norm.reshape(1, SSM_INNER))


def _post_bwd(y_ret, proj, y_ssm, xbc, dsk_e, ssm_norm, d_or, d_os, name):
    s = y_ret.shape[0]
    t = _pick(s, 256)

    def body(yr_ref, g_ref, ys_ref, xs_ref, z_ref, dsk_ref, nw_ref, dor_ref, dos_ref,
             dyr_ref, dg_ref, dys_ref, dxs_ref, dz_ref, ddsk_ref, dnw_ref):
        @pl.when(pl.program_id(0) == 0)
        def _():
            ddsk_ref[...] = jnp.zeros_like(ddsk_ref)
            dnw_ref[...] = jnp.zeros_like(dnw_ref)

        for h in range(RET_V // POST_W):
            sl = slice(h * POST_W, (h + 1) * POST_W)
            _, vjp = jax.vjp(_post_ret, yr_ref[:, sl], g_ref[:, sl].astype(F32))
            dyr, dg = vjp(dor_ref[:, sl])
            dyr_ref[:, sl] = dyr
            dg_ref[:, sl] = dg.astype(dg_ref.dtype)
        for g in range(SSM_INNER // POST_W):
            sl = slice(g * POST_W, (g + 1) * POST_W)
            _, vjp = jax.vjp(_post_ssm, ys_ref[:, sl], xs_ref[:, sl], z_ref[:, sl].astype(F32), dsk_ref[:, sl], nw_ref[:, sl])
            dy, dxs, dz, ddsk, dnw = vjp(dos_ref[:, sl])
            dys_ref[:, sl] = dy
            dxs_ref[:, sl] = dxs
            dz_ref[:, sl] = dz.astype(dz_ref.dtype)
            ddsk_ref[:, sl] += ddsk
            dnw_ref[:, sl] += dnw

    rowv = pl.BlockSpec((t, RET_V), lambda i: (i, 0))
    rows = pl.BlockSpec((t, SSM_INNER), lambda i: (i, 0))
    return pl.pallas_call(
        body, grid=(s // t,), in_specs=_post_specs(t) + [rowv, rows],
        out_specs=[rowv, rowv, rows, rows, rows, _full((1, SSM_INNER)), _full((1, SSM_INNER))],
        out_shape=[_sds((s, RET_V), F32), _sds((s, RET_V), MXU_DTYPE), _sds((s, SSM_INNER), F32),
                   _sds((s, SSM_INNER), F32), _sds((s, SSM_INNER), MXU_DTYPE),
                   _sds((1, SSM_INNER), F32), _sds((1, SSM_INNER), F32)],
        compiler_params=_cparams(("arbitrary",)), name=name,
    )(y_ret, proj, y_ssm, xbc, proj, dsk_e, ssm_norm.reshape(1, SSM_INNER), d_or, d_os)


def _merge_fn(gr, gs, br, bs, yr, ys):
    return jax.nn.sigmoid(gr + br) * yr + jax.nn.sigmoid(gs + bs) * ys


def _merge_specs(t):
    row = pl.BlockSpec((t, D_MODEL), lambda i: (i, 0))
    return [pl.BlockSpec((t, D_MODEL), lambda i: (i, COL_GATES // D_MODEL)),
            pl.BlockSpec((t, D_MODEL), lambda i: (i, COL_GATES // D_MODEL + 1)),
            pl.BlockSpec((1, D_MODEL), lambda i: (0, 0)), pl.BlockSpec((1, D_MODEL), lambda i: (0, 1)), row, row]


def _merge_fwd(proj, b_gate, br_ret, br_ssm, name):
    s = proj.shape[0]
    t = _pick(s, 512)

    def body(gr_ref, gs_ref, br_ref, bs_ref, yr_ref, ys_ref, o_ref, ot_ref):
        o = _merge_fn(gr_ref[...].astype(F32), gs_ref[...].astype(F32), br_ref[...], bs_ref[...], yr_ref[...], ys_ref[...])
        o_ref[...] = o.astype(o_ref.dtype)
        ot_ref[...] = o.T.astype(ot_ref.dtype)

    bg = b_gate.reshape(1, 2 * D_MODEL)
    return pl.pallas_call(
        body, grid=(s // t,), in_specs=_merge_specs(t),
        out_specs=[pl.BlockSpec((t, D_MODEL), lambda i: (i, 0)), pl.BlockSpec((D_MODEL, t), lambda i: (0, i))],
        out_shape=[_sds((s, D_MODEL), MXU_DTYPE), _sds((D_MODEL, s), MXU_DTYPE)],
        compiler_params=_cparams(("parallel",)), name=name,
    )(proj, proj, bg, bg, br_ret, br_ssm)


def _merge_bwd(proj, b_gate, br_ret, br_ssm, dm, name):
    s = proj.shape[0]
    t = _pick(s, 512)

    def body(gr_ref, gs_ref, br_ref, bs_ref, yr_ref, ys_ref, dm_ref, dgt_ref, db_ref, dyr_ref, dys_ref):
        @pl.when(pl.program_id(0) == 0)
        def _():
            db_ref[...] = jnp.zeros_like(db_ref)

        _, vjp = jax.vjp(_merge_fn, gr_ref[...].astype(F32), gs_ref[...].astype(F32), br_ref[...], bs_ref[...],
                         yr_ref[...], ys_ref[...])
        dgr, dgs, dbr, dbs, dyr, dys = vjp(dm_ref[...])
        dgt_ref[:, :D_MODEL] = dgr.astype(dgt_ref.dtype)
        dgt_ref[:, D_MODEL:] = dgs.astype(dgt_ref.dtype)
        db_ref[:, :D_MODEL] += dbr
        db_ref[:, D_MODEL:] += dbs
        dyr_ref[...] = dyr.astype(dyr_ref.dtype)
        dys_ref[...] = dys.astype(dys_ref.dtype)

    bg = b_gate.reshape(1, 2 * D_MODEL)
    row = pl.BlockSpec((t, D_MODEL), lambda i: (i, 0))
    return pl.pallas_call(
        body, grid=(s // t,), in_specs=_merge_specs(t) + [row],
        out_specs=[pl.BlockSpec((t, 2 * D_MODEL), lambda i: (i, 0)), _full((1, 2 * D_MODEL)), row, row],
        out_shape=[_sds((s, 2 * D_MODEL), MXU_DTYPE), _sds((1, 2 * D_MODEL), F32),
                   _sds((s, D_MODEL), MXU_DTYPE), _sds((s, D_MODEL), MXU_DTYPE)],
        compiler_params=_cparams(("arbitrary",)), name=name,
    )(proj, proj, bg, bg, br_ret, br_ssm, dm)


def _attn_head(q, k, v):
    sc = _mm(q, k, 'nt') * (XA_HEAD_DIM ** -0.5)
    e = jnp.exp(sc - lax.stop_gradient(jnp.max(sc, axis=-1, keepdims=True)))
    p = e / jnp.sum(e, axis=-1, keepdims=True)
    return _mm(p, v, 'nn')


def _attn_fwd(q, kv, name):
    s = q.shape[0]
    m = kv.shape[0]
    t = _pick(s, 512)

    def body(q_ref, kv_ref, o_ref, ot_ref):
        for h in range(XA_HEADS):
            sl = slice(h * XA_HEAD_DIM, (h + 1) * XA_HEAD_DIM)
            vl = slice(D_MODEL + h * XA_HEAD_DIM, D_MODEL + (h + 1) * XA_HEAD_DIM)
            o = _attn_head(q_ref[:, sl], kv_ref[:, sl], kv_ref[:, vl])
            o_ref[:, sl] = o.astype(o_ref.dtype)
            ot_ref[sl, :] = o.T.astype(ot_ref.dtype)

    return pl.pallas_call(
        body, grid=(s // t,),
        in_specs=[pl.BlockSpec((t, D_MODEL), lambda i: (i, 0)), _full((m, 2 * D_MODEL))],
        out_specs=[pl.BlockSpec((t, D_MODEL), lambda i: (i, 0)), pl.BlockSpec((D_MODEL, t), lambda i: (0, i))],
        out_shape=[_sds((s, D_MODEL), MXU_DTYPE), _sds((D_MODEL, s), MXU_DTYPE)],
        compiler_params=_cparams(("parallel",)), name=name,
    )(q, kv)


def _attn_bwd(q, kv, d_o, name):
    s = q.shape[0]
    m = kv.shape[0]
    t = _pick(s, 512)

    def body(q_ref, kv_ref, do_ref, dq_ref, dkv_ref):
        @pl.when(pl.program_id(0) == 0)
        def _():
            dkv_ref[...] = jnp.zeros_like(dkv_ref)

        for h in range(XA_HEADS):
            sl = slice(h * XA_HEAD_DIM, (h + 1) * XA_HEAD_DIM)
            vl = slice(D_MODEL + h * XA_HEAD_DIM, D_MODEL + (h + 1) * XA_HEAD_DIM)
            _, vjp = jax.vjp(_attn_head, q_ref[:, sl], kv_ref[:, sl], kv_ref[:, vl])
            dq, dk, dv = vjp(do_ref[:, sl])
            dq_ref[:, sl] = dq.astype(dq_ref.dtype)
            dkv_ref[:, sl] += dk
            dkv_ref[:, vl] += dv

    row = pl.BlockSpec((t, D_MODEL), lambda i: (i, 0))
    return pl.pallas_call(
        body, grid=(s // t,), in_specs=[row, _full((m, 2 * D_MODEL)), row],
        out_specs=[row, _full((m, 2 * D_MODEL))],
        out_shape=[_sds((s, D_MODEL), MXU_DTYPE), _sds((m, 2 * D_MODEL), F32)],
        compiler_params=_cparams(("arbitrary",)), name=name,
    )(q, kv, d_o)


def _loss_head(x, w, target, name):
    s, d = x.shape
    t = _pick(s, 512)

    def body(x_ref, w_ref, t_ref, loss_ref, dx_ref, dxb_ref, dw_ref):
        @pl.when(pl.program_id(0) == 0)
        def _():
            loss_ref[...] = jnp.zeros_like(loss_ref)
            dw_ref[...] = jnp.zeros_like(dw_ref)

        y, vjp = jax.vjp(_rmsnorm_fn, x_ref[...], w_ref[...])
        err = y - t_ref[...]
        loss_ref[...] += 0.5 * jnp.sum(jnp.sum(err * err, axis=-1, keepdims=True), axis=0, keepdims=True) / d
        dx, dw = vjp(err * (1.0 / d))
        dx_ref[...] = dx
        dxb_ref[...] = dx.astype(dxb_ref.dtype)
        dw_ref[...] += dw

    row = pl.BlockSpec((t, d), lambda i: (i, 0))
    return pl.pallas_call(
        body, grid=(s // t,), in_specs=[row, _full((1, d)), row],
        out_specs=[_full((1, LANES)), row, row, _full((1, d))],
        out_shape=[_sds((1, LANES), F32), _sds((s, d), F32), _sds((s, d), MXU_DTYPE), _sds((1, d), F32)],
        compiler_params=_cparams(("arbitrary",)), name=name,
    )(x, w.reshape(1, d), target)


def _epi_sqrelu(acc):
    r = jnp.maximum(acc, 0.0)
    return r * r, r * r


def _epi_sqrelu_bwd(acc, act):
    return (acc * (2.0 * jnp.sqrt(act.astype(F32))),)


def _rope_tables(positions):
    inv_freq = ROPE_THETA ** (-jnp.arange(0, RET_QK_DIM, 2, dtype=F32) / RET_QK_DIM)
    ang = positions.astype(F32)[:, None] * inv_freq
    cos, sin = jnp.cos(ang), jnp.sin(ang)
    return jnp.concatenate([cos, cos], axis=1), jnp.concatenate([-sin, sin], axis=1)


W_IN_ORIG = (('q', 0, 512), ('k', 512, 1024), ('v', 1024, 2048), ('g', 2048, 3072), ('z', 3072, 5120),
             ('xbc', 5120, 9216), ('dt', 9216, 9248), ('gates', 9248, 11296))
W_IN_MAIN_ORDER = ('z', 'xbc', 'gates', 'v', 'g', 'q', 'k')
W_IN_SHARD = IN_DIM // N_DEV


def _shard_segments(lo, hi):
    segs = []
    for j in range(lo // W_IN_SHARD, (hi - 1) // W_IN_SHARD + 1):
        segs.append((j, max(lo, j * W_IN_SHARD) - j * W_IN_SHARD, min(hi, (j + 1) * W_IN_SHARD) - j * W_IN_SHARD))
    return segs


def _w_in_from_shards(g):
    rng = {name: (lo, hi) for name, lo, hi in W_IN_ORIG}
    cols = [g[j][:, a:b] for name in W_IN_MAIN_ORDER for j, a, b in _shard_segments(*rng[name])]
    (j, a, b), = _shard_segments(*rng['dt'])
    return jnp.concatenate(cols, axis=1), jnp.pad(g[j][:, a:b], ((0, 0), (0, DT_PAD - SSM_HEADS)))


def _w_in_grad_blocks(d, d_dt):
    src_of = {'q': ('qk', 0), 'k': ('qk', RET_QK)}
    blocks = []
    for j in range(N_DEV):
        lo_j, hi_j = j * W_IN_SHARD, (j + 1) * W_IN_SHARD
        cols = []
        for name, lo, hi in W_IN_ORIG:
            a, b = max(lo, lo_j), min(hi, hi_j)
            if a >= b:
                continue
            if name == 'dt':
                cols.append(d_dt[:, a - lo:b - lo])
            else:
                key, off = src_of.get(name, (name, 0))
                cols.append(d[key][:, off + a - lo:off + b - lo])
        blocks.append(jnp.concatenate(cols, axis=1))
    return blocks


def _lanes_of_heads(v):
    return jnp.repeat(v, SSM_HEAD_DIM).reshape(1, SSM_INNER)


def _heads_of_lanes(v):
    return v.reshape(SSM_HEADS, SSM_HEAD_DIM).sum(axis=1)


def _layer_fwd(x, mem, cos2, sin2, p, l, later_blocks, rider=None):
    n = lambda s: f"{s}_l{l}"
    sv = {'x0': x}
    u, u_t = _rmsnorm(x, p['norm_mix'], n("norm_mix"))
    proj, gathered = _matmul(u, p['w_in_main'], 'nn', n("in_proj"), out_dtypes=(MXU_DTYPE,), rider=_AllGather(later_blocks))
    p = {**p, **_full_weights(LATER, gathered)}
    dtraw = _matmul(u, p['w_in_dt'], 'nn', n("in_proj_dt"))
    xbc = _conv_fwd(proj, p['conv_w'], p['conv_b'], n("conv"))
    a_c = jnp.pad(-jnp.exp(p['a_log']), (0, DT_PAD - SSM_HEADS)).reshape(1, DT_PAD)
    dtb = jnp.pad(p['dt_bias'], (0, DT_PAD - SSM_HEADS)).reshape(1, DT_PAD)
    y_ret, y_ssm, s_hist, h_hist, *delivered = _scan_fwd(proj, xbc, dtraw, cos2, sin2, a_c, dtb, n("scan"), rider)
    dsk_e = _lanes_of_heads(p['d_skip'])
    o_ret, o_ssm, o_ret_t, o_ssm_t = _post_fwd(y_ret, proj, y_ssm, xbc, dsk_e, p['ssm_norm'], n("post"))
    br_ret = _matmul(o_ret, p['w_br_ret'], 'nn', n("br_ret"))
    br_ssm = _matmul(o_ssm, p['w_br_ssm'], 'nn', n("br_ssm"))
    merged, merged_t = _merge_fwd(proj, p['b_gate'], br_ret, br_ssm, n("merge"))
    x1 = _matmul(merged, p['w_out'], 'nn', n("w_out"), extras=(x,), epi=_epi_add)
    sv.update(u_t=u_t, proj=proj, dtraw=dtraw, xbc=xbc, a_c=a_c, dtb=dtb, y_ret=y_ret, y_ssm=y_ssm, s_hist=s_hist,
              h_hist=h_hist, dsk_e=dsk_e, o_ret_t=o_ret_t, o_ssm_t=o_ssm_t, br_ret=br_ret, br_ssm=br_ssm,
              merged_t=merged_t, x1=x1)
    hq, hq_t = _rmsnorm(x1, p['norm_xa'], n("norm_xa"))
    memn, _ = _rmsnorm(mem, p['norm_mem'], n("norm_mem"))
    q = _matmul(hq, p['xa_wq'], 'nn', n("xa_q"))
    kv = _matmul(memn, p['xa_wkv'], 'nn', n("xa_kv"))
    o, o_t = _attn_fwd(q, kv, n("attn"))
    x2 = _matmul(o, p['xa_wo'], 'nn', n("xa_o"), extras=(x1,), epi=_epi_add)
    sv.update(hq_t=hq_t, memn=memn, q=q, kv=kv, o_t=o_t, x2=x2)
    hm, hm_t = _rmsnorm(x2, p['norm_mlp'], n("norm_mlp"))
    act, act_t = _matmul(hm, p['mlp_w1'], 'nn', n("mlp_1"), epi=_epi_sqrelu, out_dtypes=(MXU_DTYPE, MXU_DTYPE),
                         out_t=(False, True))
    x3 = _matmul(act, p['mlp_w2'], 'nn', n("mlp_2"), extras=(x2,), epi=_epi_add)
    sv.update(hm_t=hm_t, act=act, act_t=act_t)
    return x3, sv, p, delivered


def _layer_bwd(dx, dxb, mem, cos2, sin2, p, sv, l, pending, c_idx):
    n = lambda s: f"{s}_bwd_l{l}"
    gd = (MXU_DTYPE,)
    g = {}
    g['mlp_w2'] = _matmul(sv['act_t'], dxb, 'nn', n("mlp_2_dw"), out_dtypes=gd)
    da = _matmul(dxb, p['mlp_w2'], 'nt', n("mlp_2_dx"), extras=(sv['act'],), epi=_epi_sqrelu_bwd, out_dtypes=(MXU_DTYPE,))
    g['mlp_w1'] = _matmul(sv['hm_t'], da, 'nn', n("mlp_1_dw"), out_dtypes=gd)
    dhm = _matmul(da, p['mlp_w1'], 'nt', n("mlp_1_dx"))
    dx2, dx2b, g['norm_mlp'] = _rmsnorm_bwd(sv['x2'], p['norm_mlp'], dhm, dx, n("norm_mlp"))
    g['xa_wo'] = _matmul(sv['o_t'], dx2b, 'nn', n("xa_o_dw"), out_dtypes=gd)
    d_o = _matmul(dx2b, p['xa_wo'], 'nt', n("xa_o_dx"))
    dq, dkv = _attn_bwd(sv['q'], sv['kv'], d_o, n("attn"))
    g['xa_wq'] = _matmul(sv['hq_t'], dq, 'nn', n("xa_q_dw"), out_dtypes=gd)
    dhq = _matmul(dq, p['xa_wq'], 'nt', n("xa_q_dx"))
    g['xa_wkv'] = _matmul(sv['memn'], dkv, 'tn', n("xa_kv_dw"), out_dtypes=gd)
    dmemn = _matmul(dkv, p['xa_wkv'], 'nt', n("xa_kv_dx"))
    _, _, g['norm_mem'] = _rmsnorm_bwd(mem, p['norm_mem'], dmemn, None, n("norm_mem"))
    dx1, dx1b, g['norm_xa'] = _rmsnorm_bwd(sv['x1'], p['norm_xa'], dhq, dx2, n("norm_xa"))
    g['w_out'] = _matmul(sv['merged_t'], dx1b, 'nn', n("w_out_dw"), out_dtypes=gd)
    dmerged = _matmul(dx1b, p['w_out'], 'nt', n("w_out_dx"))
    dgates, g['b_gate'], dbr_ret, dbr_ssm = _merge_bwd(sv['proj'], p['b_gate'], sv['br_ret'], sv['br_ssm'], dmerged, n("merge"))
    g['w_br_ret'] = _matmul(sv['o_ret_t'], dbr_ret, 'nn', n("br_ret_dw"), out_dtypes=gd)
    g['w_br_ssm'] = _matmul(sv['o_ssm_t'], dbr_ssm, 'nn', n("br_ssm_dw"), out_dtypes=gd)
    d_or = _matmul(dbr_ret, p['w_br_ret'], 'nt', n("br_ret_dx"))
    d_os = _matmul(dbr_ssm, p['w_br_ssm'], 'nt', n("br_ssm_dx"))
    later_sums = _core_sums(LATER, g, c_idx, l)
    rider = _ExchangeChips(list(pending) + later_sums)
    dyr, dg, dys, dxs_skip, dz, ddsk_e, g['ssm_norm'] = _post_bwd(
        sv['y_ret'], sv['proj'], sv['y_ssm'], sv['xbc'], sv['dsk_e'], p['ssm_norm'], d_or, d_os, n("post"))
    g['d_skip'] = _heads_of_lanes(ddsk_e)
    dqk_r, dv_r, dxbc_act, ddtraw, dac, ddtb, *delivered = _scan_bwd(
        sv['proj'], sv['xbc'], sv['dtraw'], cos2, sin2, sv['a_c'], sv['dtb'], sv['s_hist'], sv['h_hist'],
        dyr, dys, dxs_skip, n("scan"), rider)
    g['a_log'] = dac[0, :SSM_HEADS] * (-jnp.exp(p['a_log']))
    g['dt_bias'] = ddtb[0, :SSM_HEADS]
    dxbc_raw, g['conv_w'], g['conv_b'] = _conv_bwd(sv['proj'], p['conv_w'], p['conv_b'], dxbc_act, n("conv"))
    pieces = {'z': dz, 'xbc': dxbc_raw, 'gates': dgates, 'v': dv_r, 'g': dg, 'qk': dqk_r}
    d_w = {k: _matmul(sv['u_t'], pc, 'nn', n(f"in_proj_dw_{k}"), out_dtypes=gd) for k, pc in pieces.items()}
    d_dt = _matmul(sv['u_t'], ddtraw, 'nn', n("in_proj_dt_dw"), out_dtypes=gd)
    g['w_in'] = _w_in_grad_blocks(d_w, d_dt)
    du_dt = _matmul(ddtraw, p['w_in_dt'], 'nt', n("in_proj_dt_dx"))
    du = _matmul_nt_pieces(list(pieces.values()), p['w_in_main'], n("in_proj_dx"), extras=(du_dt,), epi=_epi_add)
    dx0, dx0b, g['norm_mix'] = _rmsnorm_bwd(sv['x0'], p['norm_mix'], du, dx1, n("norm_mix"))
    return dx0, dx0b, g, delivered[:len(pending)], delivered[len(pending):], _core_sums(FIRST, g, c_idx, l)


def _full_weights(names, gathered):
    p = {}
    for k, g in zip(names, gathered):
        if k == 'w_in':
            p['w_in_main'], p['w_in_dt'] = _w_in_from_shards(g)
        elif k in COL_SHARDED:
            p[k] = jnp.concatenate([g[j] for j in range(N_DEV)], axis=1)
        else:
            p[k] = g.reshape(-1, g.shape[-1])
    return p


def _grad_scatter(k, g):
    if k == 'w_in':
        blocks = g
    elif k in COL_SHARDED:
        c = g.shape[1] // N_DEV
        blocks = [g[:, j * c:(j + 1) * c] for j in range(N_DEV)]
    else:
        r = g.shape[0] // N_DEV
        blocks = [g[j * r:(j + 1) * r] for j in range(N_DEV)]
    return jnp.stack([jnp.stack([blocks[2 * chip + core] for chip in range(4)]) for core in range(2)])


def _core_sums(names, g, c_idx, l):
    by_core = [_grad_scatter(k, g[k]) for k in names]
    from_sibling = _exchange_cores(by_core, f"grad_exchange_cores_{names[0]}_l{l}")
    return [_add_halves(a, o, c_idx, f"grad_add_cores_{k}_l{l}") for k, a, o in zip(names, by_core, from_sibling)]


def _step(x, mem, positions, small, blocks, loss_target):
    cos2, sin2 = _rope_tables(positions)
    first = _run_exchange(_AllGather([blocks[0][k] for k in FIRST]), "all_gather_first_l0")
    saved, layers = [], []
    for l in range(DEPTH):
        p = {k: small[k][l] for k in SMALL if k != 'norm_final'}
        p.update(_full_weights(FIRST, first))
        rider = _AllGather([blocks[l + 1][k] for k in FIRST]) if l + 1 < DEPTH else None
        x, sv, p, first = _layer_fwd(x, mem, cos2, sin2, p, l, [blocks[l][k] for k in LATER], rider)
        saved.append(sv)
        layers.append(p)
    loss, dx, dxb, dnf = _loss_head(x, small['norm_final'], loss_target, "loss_head")
    c_idx = lax.axis_index("c").astype(jnp.int32).reshape(1)
    grads, by_chip, pending = [None] * DEPTH, [dict() for _ in range(DEPTH)], []
    for l in reversed(range(DEPTH)):
        dx, dxb, grads[l], got_first, got_later, pending_next = _layer_bwd(
            dx, dxb, mem, cos2, sin2, layers[l], saved[l], l, pending, c_idx)
        if pending:
            by_chip[l + 1].update(zip(FIRST, got_first))
        by_chip[l].update(zip(LATER, got_later))
        pending = pending_next
    by_chip[0].update(zip(FIRST, _run_exchange(_ExchangeChips(pending), "grad_exchange_chips_first_l0")))
    small_g = {}
    for k in SMALL:
        small_g[k] = dnf.reshape(D_MODEL) if k == 'norm_final' else [grads[l][k].reshape(small[k].shape[1:]) for l in range(DEPTH)]
    return loss, dx, small_g, by_chip


MESH = pl.DeviceIdType.MESH
ANY_SPEC = pl.BlockSpec(memory_space=pl.ANY)


def _mesh_pos():
    return lax.axis_index("x"), lax.axis_index("y"), lax.axis_index("c")


def _other_chips(x, y):
    return [(1 - x, y), (x, 1 - y), (1 - x, 1 - y)]


class _AllGather:
    def __init__(self, arrs):
        self.arrs = list(arrs)
        na = self.n = len(self.arrs)
        self.out_shape = [_sds((N_DEV,) + a.shape, a.dtype) for a in self.arrs]
        self.scratch = [pltpu.SemaphoreType.DMA((na, 7)), pltpu.SemaphoreType.DMA((na, 7)), pltpu.SemaphoreType.DMA((na,))]

    def _copies(self, x_refs, o_refs, sems):
        send_sems, recv_sems, local_sems = sems
        x, y, c = _mesh_pos()
        me, sib = (x, y, c), (x, y, 1 - c)
        chips = _other_chips(x, y)

        def copy(a, k, block, to, src=None):
            dst = o_refs[a].at[4 * block[0] + 2 * block[1] + block[2]]
            return pltpu.make_async_remote_copy(src_ref=dst if src is None else src, dst_ref=dst,
                                                send_sem=send_sems.at[a, k], recv_sem=recv_sems.at[a, k],
                                                device_id=to, device_id_type=MESH)

        mine = [pltpu.make_async_copy(x_refs[a], o_refs[a].at[4 * x + 2 * y + c], local_sems.at[a]) for a in range(self.n)]
        first = []
        for a in range(self.n):
            first.append(copy(a, 0, me, sib, src=x_refs[a]))
            first += [copy(a, 1 + j, me, (*chip, c), src=x_refs[a]) for j, chip in enumerate(chips)]
        return copy, mine, first, me, sib, chips, c

    def start(self, x_refs, o_refs, sems):
        _, mine, first, *_ = self._copies(x_refs, o_refs, sems)
        for cp in mine + first:
            cp.start()

    def finish(self, x_refs, o_refs, sems):
        copy, mine, first, me, sib, chips, c = self._copies(x_refs, o_refs, sems)
        passed = []
        for a in range(self.n):
            for j, chip in enumerate(chips):
                copy(a, 1 + j, (*chip, c), me).wait_recv()
                cp = copy(a, 4 + j, (*chip, c), sib)
                cp.start()
                passed.append(cp)
        for a in range(self.n):
            copy(a, 0, sib, me).wait_recv()
            for j, chip in enumerate(chips):
                copy(a, 4 + j, (*chip, 1 - c), me).wait_recv()
        for cp in first + passed:
            cp.wait_send()
        for cp in mine:
            cp.wait()


class _ExchangeChips:
    def __init__(self, arrs):
        self.arrs = list(arrs)
        na = self.n = len(self.arrs)
        self.out_shape = [_sds(a.shape, a.dtype) for a in self.arrs]
        self.scratch = [pltpu.SemaphoreType.DMA((na, 3)), pltpu.SemaphoreType.DMA((na, 3)), pltpu.SemaphoreType.DMA((na,))]

    def _copies(self, a_refs, o_refs, sems):
        send_sems, recv_sems, local_sems = sems
        x, y, c = _mesh_pos()
        my_chip = 2 * x + y
        chips = _other_chips(x, y)
        mine = [pltpu.make_async_copy(a_refs[a].at[my_chip], o_refs[a].at[my_chip], local_sems.at[a]) for a in range(self.n)]
        sends = [pltpu.make_async_remote_copy(src_ref=a_refs[a].at[2 * px + py], dst_ref=o_refs[a].at[my_chip],
                                              send_sem=send_sems.at[a, j], recv_sem=recv_sems.at[a, j],
                                              device_id=(px, py, c), device_id_type=MESH)
                 for a in range(self.n) for j, (px, py) in enumerate(chips)]
        recvs = [pltpu.make_async_remote_copy(src_ref=a_refs[a].at[2 * px + py], dst_ref=o_refs[a].at[2 * px + py],
                                              send_sem=send_sems.at[a, j], recv_sem=recv_sems.at[a, j],
                                              device_id=(px, py, c), device_id_type=MESH)
                 for a in range(self.n) for j, (px, py) in enumerate(chips)]
        return mine, sends, recvs

    def start(self, a_refs, o_refs, sems):
        mine, sends, _ = self._copies(a_refs, o_refs, sems)
        for cp in mine + sends:
            cp.start()

    def finish(self, a_refs, o_refs, sems):
        mine, sends, recvs = self._copies(a_refs, o_refs, sems)
        for cp in recvs:
            cp.wait_recv()
        for cp in sends:
            cp.wait_send()
        for cp in mine:
            cp.wait()


def _run_exchange(ex, name):
    na = ex.n

    def body(*refs):
        i_refs, o_refs, sems = refs[:na], refs[na:2 * na], refs[2 * na:]
        ex.start(i_refs, o_refs, sems)
        ex.finish(i_refs, o_refs, sems)

    return pl.pallas_call(body, in_specs=[ANY_SPEC] * na, out_specs=[ANY_SPEC] * na, out_shape=ex.out_shape,
                          scratch_shapes=ex.scratch, name=name)(*ex.arrs)


def _exchange_cores(arrs, name):
    na = len(arrs)

    def body(*refs):
        a_refs, o_refs = refs[:na], refs[na:2 * na]
        send_sems, recv_sems = refs[2 * na:]
        x, y, c = _mesh_pos()
        cps = [pltpu.make_async_remote_copy(src_ref=a_refs[a].at[1 - c], dst_ref=o_refs[a], send_sem=send_sems.at[a],
                                            recv_sem=recv_sems.at[a], device_id=(x, y, 1 - c), device_id_type=MESH)
               for a in range(na)]
        for cp in cps:
            cp.start()
        for cp in cps:
            cp.wait()

    return pl.pallas_call(
        body, in_specs=[ANY_SPEC] * na, out_specs=[ANY_SPEC] * na,
        out_shape=[_sds(a.shape[1:], a.dtype) for a in arrs],
        scratch_shapes=[pltpu.SemaphoreType.DMA((na,)), pltpu.SemaphoreType.DMA((na,))],
        name=name,
    )(*arrs)


def _as_rows(a, lead):
    return a.reshape(a.shape[:lead] + (-1, a.shape[-1]))


def _add_halves(a, other, c_idx, name):
    a3, o2 = _as_rows(a, 1), _as_rows(other, 0)
    rows, cols = o2.shape
    tr = _pick(rows, 256)

    def body(c_ref, a_ref, o_ref, out_ref):
        out_ref[...] = (a_ref[0].astype(F32) + o_ref[...].astype(F32)).astype(out_ref.dtype)

    out = pl.pallas_call(
        body,
        grid_spec=pltpu.PrefetchScalarGridSpec(
            num_scalar_prefetch=1, grid=(rows // tr,),
            in_specs=[pl.BlockSpec((1, tr, cols), lambda i, c_ref: (c_ref[0], i, 0)),
                      pl.BlockSpec((tr, cols), lambda i, c_ref: (i, 0))],
            out_specs=pl.BlockSpec((tr, cols), lambda i, c_ref: (i, 0))),
        out_shape=_sds((rows, cols), a.dtype), compiler_params=_cparams(("parallel",)), name=name,
    )(c_idx, a3, o2)
    return out.reshape(other.shape)


def _all_reduce_small(v, name):
    r = v.shape[0]

    def body(v_ref, o_ref, slots, send_sems, recv_sems):
        x, y, c = _mesh_pos()
        me = 4 * x + 2 * y + c
        slots[me] = v_ref[...]
        cps = []
        for k in range(1, N_DEV):
            px = 1 - x if k & 4 else x
            py = 1 - y if k & 2 else y
            pc = 1 - c if k & 1 else c
            cps.append(pltpu.make_async_remote_copy(src_ref=v_ref, dst_ref=slots.at[me], send_sem=send_sems.at[k - 1],
                                                    recv_sem=recv_sems.at[k - 1], device_id=(px, py, pc), device_id_type=MESH))
        for cp in cps:
            cp.start()
        for cp in cps:
            cp.wait()
        acc = slots[0]
        for d in range(1, N_DEV):
            acc = acc + slots[d]
        o_ref[...] = acc

    vm = pl.BlockSpec(memory_space=pltpu.VMEM)
    return pl.pallas_call(
        body, in_specs=[vm], out_specs=vm, out_shape=_sds((r, LANES), F32),
        scratch_shapes=[pltpu.VMEM((N_DEV, r, LANES), F32), pltpu.SemaphoreType.DMA((N_DEV - 1,)),
                        pltpu.SemaphoreType.DMA((N_DEV - 1,))],
        compiler_params=pltpu.CompilerParams(vmem_limit_bytes=VMEM_LIMIT_BYTES), name=name,
    )(v)


def _adamw(w, g_slots, m, v, name):
    depth, rows, cols = w.shape
    ns = g_slots.shape[0]
    tr = _pick(rows, 256 if cols <= 1024 else 128)

    def body(w_ref, g_ref, m_ref, v_ref, go_ref, d_ref, mo_ref, vo_ref):
        g = g_ref[0, 0].astype(F32)
        for i in range(1, ns):
            g = g + g_ref[i, 0].astype(F32)
        m_new = ADAM_B1 * m_ref[0] + (1.0 - ADAM_B1) * g
        v_new = ADAM_B2 * v_ref[0] + (1.0 - ADAM_B2) * (g * g)
        m_hat = m_new / (1.0 - ADAM_B1 ** ADAM_STEP)
        v_hat = v_new / (1.0 - ADAM_B2 ** ADAM_STEP)
        go_ref[0] = g
        d_ref[0] = -ADAM_LR * (m_hat / (jnp.sqrt(v_hat) + ADAM_EPS) + ADAM_WD * w_ref[0])
        mo_ref[0] = m_new
        vo_ref[0] = v_new

    blk = pl.BlockSpec((1, tr, cols), lambda l, i: (l, i, 0))
    return pl.pallas_call(
        body, grid=(depth, rows // tr),
        in_specs=[blk, pl.BlockSpec((ns, 1, tr, cols), lambda l, i: (0, l, i, 0)), blk, blk],
        out_specs=[blk] * 4, out_shape=[_sds(w.shape, F32)] * 4,
        compiler_params=_cparams(("parallel", "parallel")), name=name,
    )(w, g_slots, m, v)


_ARG_NAMES = (['x', 'mem', 'positions'] + WEIGHTS + ['loss_target'] + ['m_' + n for n in WEIGHTS]
              + ['v_' + n for n in WEIGHTS])


PACK_TILE = 8 * LANES


def _pack_rows(parts):
    blocks = []
    for part in parts:
        flat = part.reshape(-1)
        pad = (-flat.shape[0]) % PACK_TILE
        blocks.append((jnp.pad(flat, (0, pad)) if pad else flat).reshape(-1, LANES))
    return jnp.concatenate(blocks, axis=0)


def _unpack_rows(packed, shapes):
    out, off = [], 0
    for shp in shapes:
        n = int(np.prod(shp))
        rows = -(-n // PACK_TILE) * 8
        out.append(packed[off:off + rows].reshape(-1)[:n].reshape(shp))
        off += rows
    return out


def kernel(x, mem, positions, norm_mix, w_in, b_gate, conv_w, conv_b, dt_bias, a_log, d_skip, ssm_norm, w_br_ret, w_br_ssm, w_out, norm_xa, norm_mem, xa_wq, xa_wkv, xa_wo, norm_mlp, mlp_w1, mlp_w2, norm_final, loss_target, m_norm_mix, m_w_in, m_b_gate, m_conv_w, m_conv_b, m_dt_bias, m_a_log, m_d_skip, m_ssm_norm, m_w_br_ret, m_w_br_ssm, m_w_out, m_norm_xa, m_norm_mem, m_xa_wq, m_xa_wkv, m_xa_wo, m_norm_mlp, m_mlp_w1, m_mlp_w2, m_norm_final, v_norm_mix, v_w_in, v_b_gate, v_conv_w, v_conv_b, v_dt_bias, v_a_log, v_d_skip, v_ssm_norm, v_w_br_ret, v_w_br_ssm, v_w_out, v_norm_xa, v_norm_mem, v_xa_wq, v_xa_wkv, v_xa_wo, v_norm_mlp, v_mlp_w1, v_mlp_w2, v_norm_final):
    d = dict(zip(_ARG_NAMES, (x, mem, positions, norm_mix, w_in, b_gate, conv_w, conv_b, dt_bias, a_log, d_skip, ssm_norm, w_br_ret, w_br_ssm, w_out, norm_xa, norm_mem, xa_wq, xa_wkv, xa_wo, norm_mlp, mlp_w1, mlp_w2, norm_final, loss_target, m_norm_mix, m_w_in, m_b_gate, m_conv_w, m_conv_b, m_dt_bias, m_a_log, m_d_skip, m_ssm_norm, m_w_br_ret, m_w_br_ssm, m_w_out, m_norm_xa, m_norm_mem, m_xa_wq, m_xa_wkv, m_xa_wo, m_norm_mlp, m_mlp_w1, m_mlp_w2, m_norm_final, v_norm_mix, v_w_in, v_b_gate, v_conv_w, v_conv_b, v_dt_bias, v_a_log, v_d_skip, v_ssm_norm, v_w_br_ret, v_w_br_ssm, v_w_out, v_norm_xa, v_norm_mem, v_xa_wq, v_xa_wkv, v_xa_wo, v_norm_mlp, v_mlp_w1, v_mlp_w2, v_norm_final)))
    blocks = [{k: d[k][l] if k == 'conv_w' else d[k][l].astype(MXU_DTYPE) for k in SHARDED} for l in range(DEPTH)]
    small = {k: d[k] for k in SMALL}
    loss, grad_x, grads, by_chip_l = _step(d['x'][0], d['mem'][0], d['positions'][0], small, blocks, d['loss_target'][0])
    by_chip = [jnp.stack([by_chip_l[l][k] for l in range(DEPTH)], axis=1) for k in SHARDED]
    small_g = [grads[k] if k == 'norm_final' else jnp.stack(grads[k]) for k in SMALL]
    total = _all_reduce_small(_pack_rows([loss] + small_g), "all_reduce_small")
    loss_out = total[0, 0]
    res = {}
    for k, g4 in zip(SHARDED, by_chip):
        res[k] = _adamw(d[k], g4, d['m_' + k], d['v_' + k], f"adamw_{k}")
    small_shapes = [d[k].shape for k in SMALL]
    pk = lambda pre: _pack_rows([d[pre + k] for k in SMALL])
    outs = _adamw(pk('')[None], total[8:][None, None], pk('m_')[None], pk('v_')[None], "adamw_small")
    unpacked = [_unpack_rows(o[0], small_shapes) for o in outs]
    for i, k in enumerate(SMALL):
        res[k] = [unpacked[j][i] for j in range(4)]
    return (loss_out, grad_x[None], *[res[k][0] for k in WEIGHTS], *[res[k][1] for k in WEIGHTS],
            *[res[k][2] for k in WEIGHTS], *[res[k][3] for k in WEIGHTS])
```

```python
import functools

import numpy as np
import jax
import jax.numpy as jnp
from jax import lax
from jax.experimental import pallas as pl
from jax.experimental.pallas import tpu as pltpu

F32 = jnp.float32
MXU_DTYPE = jnp.bfloat16
VMEM_LIMIT_BYTES = 56 * 1024 * 1024
LANES = 128
N_DEV = 8

D_MODEL = 1024
DEPTH = 4
CHUNK = 64
EPS = 1e-6
RET_HEADS, RET_QK_DIM, RET_V_DIM = 4, 128, 256
RET_QK, RET_V = 512, 1024
ROPE_THETA = 10000.0
SSM_INNER, SSM_HEAD_DIM, SSM_HEADS, SSM_GROUPS, SSM_STATE, SSM_CONV = 2048, 64, 32, 8, 128, 4
SSM_BC = 1024
SSM_CONV_DIM = 4096
IN_DIM = 11296
XA_HEADS, XA_HEAD_DIM = 4, 256
D_FF = 4096
ADAM_LR, ADAM_B1, ADAM_B2, ADAM_EPS, ADAM_WD, ADAM_STEP = 0.001, 0.9, 0.999, 1e-08, 0.01, 10

PROJ_W = 11264
COL_Z, COL_XBC, COL_GATES, COL_V, COL_G, COL_Q, COL_K = 0, 2048, 6144, 8192, 9216, 10240, 10752
DT_PAD = 128
N_LTILE = SSM_INNER // LANES

WEIGHTS = ['norm_mix', 'w_in', 'b_gate', 'conv_w', 'conv_b', 'dt_bias', 'a_log', 'd_skip', 'ssm_norm',
           'w_br_ret', 'w_br_ssm', 'w_out', 'norm_xa', 'norm_mem', 'xa_wq', 'xa_wkv', 'xa_wo', 'norm_mlp',
           'mlp_w1', 'mlp_w2', 'norm_final']
COL_SHARDED = ['w_in', 'conv_w', 'xa_wkv', 'mlp_w1']
ROW_SHARDED = ['w_br_ret', 'w_br_ssm', 'w_out', 'xa_wq', 'xa_wo', 'mlp_w2']
SHARDED = COL_SHARDED + ROW_SHARDED
FIRST = ['w_in', 'conv_w']
LATER = [n for n in SHARDED if n not in FIRST]
SMALL = [n for n in WEIGHTS if n not in SHARDED]


def _cparams(sem=None):
    return pltpu.CompilerParams(dimension_semantics=sem, vmem_limit_bytes=VMEM_LIMIT_BYTES)


def _sds(shape, dtype):
    return jax.ShapeDtypeStruct(shape, dtype)


def _full(shape):
    nd = len(shape)
    return pl.BlockSpec(shape, lambda *_: (0,) * nd)


_DIMS = {'nn': (((1,), (0,)), ((), ())), 'nt': (((1,), (1,)), ((), ())), 'tn': (((0,), (0,)), ((), ()))}


def _dot(a, b, mode='nn'):
    return lax.dot_general(a.astype(MXU_DTYPE), b.astype(MXU_DTYPE), _DIMS[mode], preferred_element_type=F32)


@functools.partial(jax.custom_vjp, nondiff_argnums=(2,))
def _mm(a, b, mode):
    return _dot(a, b, mode)


def _mm_fwd(a, b, mode):
    return _dot(a, b, mode), (a, b)


def _mm_bwd(mode, res, g):
    a, b = res
    if mode == 'nn':
        return _dot(g, b, 'nt'), _dot(a, g, 'tn')
    if mode == 'nt':
        return _dot(g, b, 'nn'), _dot(g, a, 'tn')
    return _dot(b, g, 'nt'), _dot(a, g, 'nn')


_mm.defvjp(_mm_fwd, _mm_bwd)


def _split3(x):
    hi = x.astype(jnp.bfloat16)
    r1 = x - hi.astype(F32)
    mid = r1.astype(jnp.bfloat16)
    lo = (r1 - mid.astype(F32)).astype(jnp.bfloat16)
    return hi, mid, lo


def _dot_sel(x, c, left=False):
    dims = _DIMS['nn']
    parts = _split3(x)
    if left:
        outs = [lax.dot_general(c, p, dims, preferred_element_type=F32) for p in parts]
    else:
        outs = [lax.dot_general(p, c, dims, preferred_element_type=F32) for p in parts]
    return (outs[0] + outs[1]) + outs[2]


def _silu(x):
    return x * jax.nn.sigmoid(x)


def _softplus(x):
    pos = x > 0.0
    return jnp.where(pos, x, 0.0) + jnp.log1p(jnp.exp(jnp.where(pos, -x, x)))


def _rms(x):
    return x * lax.rsqrt(jnp.mean(x * x, axis=-1, keepdims=True) + EPS)


def _pick(n, pref):
    t = min(n, pref)
    while n % t:
        t //= 2
    return t


def _with_rider(core, n_in, n_out, n_scratch, rider, grid):
    if rider is None:
        return core
    na, nrs = rider.n, len(rider.scratch)

    def at(step_of):
        cond = pl.program_id(0) == step_of(grid[0])
        for ax in range(1, len(grid)):
            cond = cond & (pl.program_id(ax) == step_of(grid[ax]))
        return cond

    def body(*refs):
        ci, ri = refs[:n_in], refs[n_in:n_in + na]
        co, ro = refs[n_in + na:n_in + na + n_out], refs[n_in + na + n_out:n_in + 2 * na + n_out]
        sc = refs[n_in + 2 * na + n_out:]
        cs, rs = sc[:n_scratch], sc[n_scratch:]
        assert len(rs) == nrs

        @pl.when(at(lambda n: 0))
        def _():
            rider.start(ri, ro, rs)

        core(*ci, *co, *cs)

        @pl.when(at(lambda n: n - 1))
        def _():
            rider.finish(ri, ro, rs)

    return body


def _rider_args(rider):
    if rider is None:
        return [], [], [], [], []
    return list(rider.arrs), [ANY_SPEC] * rider.n, [ANY_SPEC] * rider.n, list(rider.out_shape), list(rider.scratch)


MATMUL_TK_MAX = 4096


def _tiles(mode, m, n, k):
    tm, tn = (512, 1024) if mode == 'nt' else (1024, 512)
    tk = k
    while tk > MATMUL_TK_MAX or k % tk or tk % LANES:
        tk -= LANES
    return _pick(m, tm), _pick(n, tn), tk


def _matmul(a, b, mode, name, *, extras=(), epi=None, out_dtypes=(F32,), out_t=None, tiles=None, rider=None):
    if mode == 'nn':
        (m, k), (k2, n) = a.shape, b.shape
    elif mode == 'nt':
        (m, k), (n, k2) = a.shape, b.shape
    else:
        (k, m), (k2, n) = a.shape, b.shape
    assert k == k2, (a.shape, b.shape, mode)
    tm, tn, tk = tiles or _tiles(mode, m, n, k)
    nk = k // tk
    n_ex, n_out = len(extras), len(out_dtypes)
    out_t = out_t or (False,) * n_out

    def finish(acc, ex_refs, o_refs):
        outs = epi(acc, *[r[...] for r in ex_refs]) if epi is not None else (acc,)
        for o_ref, o, tr in zip(o_refs, outs, out_t):
            o_ref[...] = (o.T if tr else o).astype(o_ref.dtype)

    def body(*refs):
        a_ref, b_ref = refs[0], refs[1]
        ex_refs = refs[2:2 + n_ex]
        o_refs = refs[2 + n_ex:2 + n_ex + n_out]
        if nk == 1:
            finish(_dot(a_ref[...], b_ref[...], mode), ex_refs, o_refs)
            return
        acc_ref = refs[-1]
        kk = pl.program_id(2)

        @pl.when(kk == 0)
        def _():
            acc_ref[...] = jnp.zeros_like(acc_ref)

        acc_ref[...] += _dot(a_ref[...], b_ref[...], mode)

        @pl.when(kk == nk - 1)
        def _():
            finish(acc_ref[...], ex_refs, o_refs)

    if mode == 'nn':
        a_spec = pl.BlockSpec((tm, tk), lambda i, j, kk: (i, kk))
        b_spec = pl.BlockSpec((tk, tn), lambda i, j, kk: (kk, j))
    elif mode == 'nt':
        a_spec = pl.BlockSpec((tm, tk), lambda i, j, kk: (i, kk))
        b_spec = pl.BlockSpec((tn, tk), lambda i, j, kk: (j, kk))
    else:
        a_spec = pl.BlockSpec((tk, tm), lambda i, j, kk: (kk, i))
        b_spec = pl.BlockSpec((tk, tn), lambda i, j, kk: (kk, j))
    mn_spec = pl.BlockSpec((tm, tn), lambda i, j, kk: (i, j))
    nm_spec = pl.BlockSpec((tn, tm), lambda i, j, kk: (j, i))
    grid = (m // tm, n // tn, nk)
    r_arrs, r_in, r_out, r_shape, r_scratch = _rider_args(rider)
    outs = pl.pallas_call(
        _with_rider(body, 2 + n_ex, n_out, int(nk > 1), rider, grid), grid=grid,
        in_specs=[a_spec, b_spec] + [mn_spec] * n_ex + r_in,
        out_specs=[nm_spec if tr else mn_spec for tr in out_t] + r_out,
        out_shape=[_sds((n, m) if tr else (m, n), dt) for dt, tr in zip(out_dtypes, out_t)] + r_shape,
        scratch_shapes=([pltpu.VMEM((tm, tn), F32)] if nk > 1 else []) + r_scratch,
        compiler_params=_cparams(("arbitrary",) * 3 if rider is not None else ("parallel", "parallel", "arbitrary")),
        name=name,
    )(a, b, *extras, *r_arrs)
    res = outs[0] if n_out == 1 else outs[:n_out]
    return (res, outs[n_out:]) if rider is not None else res


def _epi_add(acc, r):
    return (acc + r,)


PIECE_TK = 1024


def _matmul_nt_pieces(pieces, b, name, *, extras=(), epi=None, out_dtypes=(F32,)):
    m, n = pieces[0].shape[0], b.shape[0]
    tm, tn, tk = _pick(m, 1024), _pick(n, 1024), PIECE_TK
    steps = [pc.shape[1] // tk for pc in pieces]
    starts = [sum(steps[:i]) for i in range(len(pieces))]
    nk = sum(steps)
    assert b.shape[1] == nk * tk and all(pc.shape[1] % tk == 0 for pc in pieces)
    n_pc, n_ex, n_out = len(pieces), len(extras), len(out_dtypes)

    def body(*refs):
        pc_refs, b_ref = refs[:n_pc], refs[n_pc]
        ex_refs = refs[n_pc + 1:n_pc + 1 + n_ex]
        o_refs = refs[n_pc + 1 + n_ex:n_pc + 1 + n_ex + n_out]
        acc_ref = refs[-1]
        kk = pl.program_id(2)

        @pl.when(kk == 0)
        def _():
            acc_ref[...] = jnp.zeros_like(acc_ref)

        for pc_ref, st, ns in zip(pc_refs, starts, steps):
            @pl.when((kk >= st) & (kk < st + ns))
            def _(pc_ref=pc_ref):
                acc_ref[...] += _dot(pc_ref[...], b_ref[...], 'nt')

        @pl.when(kk == nk - 1)
        def _():
            acc = acc_ref[...]
            outs = epi(acc, *[r[...] for r in ex_refs]) if epi is not None else (acc,)
            for o_ref, o in zip(o_refs, outs):
                o_ref[...] = o.astype(o_ref.dtype)

    pc_specs = [pl.BlockSpec((tm, tk), lambda i, j, kk, st=st, ns=ns: (i, jnp.clip(kk - st, 0, ns - 1)))
                for st, ns in zip(starts, steps)]
    mn_spec = pl.BlockSpec((tm, tn), lambda i, j, kk: (i, j))
    outs = pl.pallas_call(
        body, grid=(m // tm, n // tn, nk),
        in_specs=pc_specs + [pl.BlockSpec((tn, tk), lambda i, j, kk: (j, kk))] + [mn_spec] * n_ex,
        out_specs=[mn_spec] * n_out, out_shape=[_sds((m, n), dt) for dt in out_dtypes],
        scratch_shapes=[pltpu.VMEM((tm, tn), F32)],
        compiler_params=_cparams(("parallel", "parallel", "arbitrary")), name=name,
    )(*pieces, b, *extras)
    return outs[0] if n_out == 1 else outs


def _rmsnorm_fn(x, w):
    return _rms(x) * w


def _rmsnorm(x, w, name):
    s, d = x.shape
    t = _pick(s, 512)

    def body(x_ref, w_ref, o_ref, ot_ref):
        y = _rmsnorm_fn(x_ref[...], w_ref[...])
        o_ref[...] = y.astype(o_ref.dtype)
        ot_ref[...] = y.T.astype(ot_ref.dtype)

    return pl.pallas_call(
        body, grid=(s // t,),
        in_specs=[pl.BlockSpec((t, d), lambda i: (i, 0)), _full((1, d))],
        out_specs=[pl.BlockSpec((t, d), lambda i: (i, 0)), pl.BlockSpec((d, t), lambda i: (0, i))],
        out_shape=[_sds((s, d), MXU_DTYPE), _sds((d, s), MXU_DTYPE)],
        compiler_params=_cparams(("parallel",)), name=name,
    )(x, w.reshape(1, d))


def _rmsnorm_bwd(x, w, du, dres, name):
    s, d = x.shape
    t = _pick(s, 512)
    has_res = dres is not None

    def body(*refs):
        if has_res:
            x_ref, w_ref, du_ref, dres_ref, dx_ref, dxb_ref, dw_ref = refs
        else:
            x_ref, w_ref, du_ref, dx_ref, dxb_ref, dw_ref = refs
        _, vjp = jax.vjp(_rmsnorm_fn, x_ref[...], w_ref[...])
        dx, dw = vjp(du_ref[...])
        dx = dx + dres_ref[...] if has_res else dx
        dx_ref[...] = dx
        dxb_ref[...] = dx.astype(dxb_ref.dtype)

        @pl.when(pl.program_id(0) == 0)
        def _():
            dw_ref[...] = jnp.zeros_like(dw_ref)

        dw_ref[...] += dw

    row = pl.BlockSpec((t, d), lambda i: (i, 0))
    return pl.pallas_call(
        body, grid=(s // t,),
        in_specs=[row, _full((1, d)), row] + ([row] if has_res else []),
        out_specs=[row, row, _full((1, d))],
        out_shape=[_sds((s, d), F32), _sds((s, d), MXU_DTYPE), _sds((1, d), F32)],
        compiler_params=_cparams(("arbitrary",)), name=name,
    )(x, w.reshape(1, d), du, *([dres] if has_res else []))


CONV_CW = 2048
CONV_HALO = 16


def _shifted(cat):
    return [cat] + [pltpu.roll(cat, sft, axis=0) for sft in (1, 2, 3)]


def _conv_taps(shifted, w, n_rows, off):
    acc = shifted[0][off:off + n_rows, :] * w[3:4, :]
    for sft in (1, 2, 3):
        acc = acc + shifted[sft][off:off + n_rows, :] * w[3 - sft:4 - sft, :]
    return acc


def _conv_fwd(proj, conv_w, conv_b, name):
    s = proj.shape[0]
    tr = _pick(s, 256)
    hb = tr // CONV_HALO
    col0 = COL_XBC // CONV_CW

    def body(prev_ref, x_ref, w_ref, b_ref, o_ref):
        i = pl.program_id(1)
        prev = jnp.where(i == 0, 0.0, prev_ref[...].astype(F32))
        cat = jnp.concatenate([prev, x_ref[...].astype(F32)], axis=0)
        o_ref[...] = _silu(_conv_taps(_shifted(cat), w_ref[...], tr, CONV_HALO) + b_ref[...])

    return pl.pallas_call(
        body, grid=(SSM_CONV_DIM // CONV_CW, s // tr),
        in_specs=[pl.BlockSpec((CONV_HALO, CONV_CW), lambda j, i: (jnp.maximum(i * hb - 1, 0), j + col0)),
                  pl.BlockSpec((tr, CONV_CW), lambda j, i: (i, j + col0)),
                  pl.BlockSpec((SSM_CONV, CONV_CW), lambda j, i: (0, j)),
                  pl.BlockSpec((1, CONV_CW), lambda j, i: (0, j))],
        out_specs=pl.BlockSpec((tr, CONV_CW), lambda j, i: (i, j)),
        out_shape=_sds((s, SSM_CONV_DIM), F32),
        compiler_params=_cparams(("parallel", "parallel")), name=name,
    )(proj, proj, conv_w, conv_b.reshape(1, SSM_CONV_DIM))


def _conv_bwd(proj, conv_w, conv_b, dact, name):
    s = proj.shape[0]
    tr = _pick(s, 256)
    hb = tr // CONV_HALO
    nb = s // CONV_HALO
    nt = s // tr
    col0 = COL_XBC // CONV_CW
    h = CONV_HALO

    def body(prev_ref, x_ref, next_ref, w_ref, b_ref, da_ref, dan_ref, dx_ref, dw_ref, db_ref):
        i = pl.program_id(1)
        w = w_ref[...]
        prev = jnp.where(i == 0, 0.0, prev_ref[...].astype(F32))
        cat = jnp.concatenate([prev, x_ref[...].astype(F32), next_ref[...].astype(F32)], axis=0)
        shifted = _shifted(cat)
        pre = _conv_taps(shifted, w, tr + h, h) + b_ref[...]
        dact_n = jnp.where(i == nt - 1, 0.0, dan_ref[...])
        dact_ext = jnp.concatenate([da_ref[...], dact_n], axis=0)
        sg = jax.nn.sigmoid(pre)
        dpre = dact_ext * (sg * (1.0 + pre * (1.0 - sg)))
        dx = dpre[:tr, :] * w[3:4, :]
        for sft in (1, 2, 3):
            dx = dx + pltpu.roll(dpre, tr + h - sft, axis=0)[:tr, :] * w[3 - sft:4 - sft, :]
        dx_ref[...] = dx.astype(dx_ref.dtype)

        @pl.when(i == 0)
        def _():
            dw_ref[...] = jnp.zeros_like(dw_ref)
            db_ref[...] = jnp.zeros_like(db_ref)

        dp = dpre[:tr, :]
        db_ref[...] += jnp.sum(dp, axis=0, keepdims=True)
        for r, sft in enumerate((3, 2, 1, 0)):
            dw_ref[r:r + 1, :] += jnp.sum(dp * shifted[sft][h:h + tr, :], axis=0, keepdims=True)

    return pl.pallas_call(
        body, grid=(SSM_CONV_DIM // CONV_CW, nt),
        in_specs=[pl.BlockSpec((h, CONV_CW), lambda j, i: (jnp.maximum(i * hb - 1, 0), j + col0)),
                  pl.BlockSpec((tr, CONV_CW), lambda j, i: (i, j + col0)),
                  pl.BlockSpec((h, CONV_CW), lambda j, i: (jnp.minimum((i + 1) * hb, nb - 1), j + col0)),
                  pl.BlockSpec((SSM_CONV, CONV_CW), lambda j, i: (0, j)),
                  pl.BlockSpec((1, CONV_CW), lambda j, i: (0, j)),
                  pl.BlockSpec((tr, CONV_CW), lambda j, i: (i, j)),
                  pl.BlockSpec((h, CONV_CW), lambda j, i: (jnp.minimum((i + 1) * hb, nb - 1), j))],
        out_specs=[pl.BlockSpec((tr, CONV_CW), lambda j, i: (i, j)),
                   pl.BlockSpec((SSM_CONV, CONV_CW), lambda j, i: (0, j)),
                   pl.BlockSpec((1, CONV_CW), lambda j, i: (0, j))],
        out_shape=[_sds((s, SSM_CONV_DIM), MXU_DTYPE), _sds((SSM_CONV, SSM_CONV_DIM), F32), _sds((1, SSM_CONV_DIM), F32)],
        compiler_params=_cparams(("parallel", "arbitrary")), name=name,
    )(proj, proj, proj, conv_w, conv_b.reshape(1, SSM_CONV_DIM), dact, dact)


def _scan_tables():
    idx = np.arange(CHUNK, dtype=np.float32)
    lg = np.log1p(-(2.0 ** (-5.0 - np.arange(RET_HEADS, dtype=np.float32)))).astype(np.float32)
    rel = np.abs(idx[:, None] - idx[None, :])
    r_intra = np.exp(lg[:, None, None] * rel).astype(np.float32)
    qd = np.exp(lg[None, :] * (idx[:, None] + 1.0)).astype(np.float32)
    kd = np.exp(lg[None, :] * (CHUNK - 1.0 - idx[:, None])).astype(np.float32)
    gam = [float(v) for v in np.exp(lg * CHUNK).astype(np.float32)]
    qd_e = np.repeat(qd, RET_QK_DIM, axis=1)
    kd_e = np.repeat(kd, RET_QK_DIM, axis=1)
    e = np.zeros((DT_PAD, SSM_INNER), np.float32)
    for hh in range(SSM_HEADS):
        e[hh, hh * SSM_HEAD_DIM:(hh + 1) * SSM_HEAD_DIM] = 1.0
    tri = np.tril(np.ones((CHUNK, CHUNK), np.float32))
    eye2 = np.concatenate([np.eye(CHUNK, dtype=np.float32)] * 2, axis=1)
    bdm = np.kron(np.eye(2, dtype=np.float32), np.ones((CHUNK, CHUNK), np.float32))
    last = np.zeros((CHUNK, LANES), np.float32)
    last[CHUNK - 1, :] = 1.0
    f32c = [jnp.asarray(c) for c in (r_intra, qd_e, kd_e, eye2, bdm, last)]
    sel = [jnp.asarray(c, jnp.bfloat16) for c in (e, e.T.copy(), tri, tri.T.copy())]
    return f32c + sel, gam


def _rope(t, cos2, sin2):
    return t * cos2 + pltpu.roll(t, RET_QK_DIM // 2, axis=1) * sin2


def _rope_t(d, cos2, sin2):
    return d * cos2 + pltpu.roll(d * sin2, RET_QK_DIM // 2, axis=1)


def _ret_step(q, k, v, st, r_intra, qd, kd, gamma):
    k = k * (RET_QK_DIM ** -0.5)
    sc = _mm(q, k, 'nt') * r_intra
    y = _mm(sc, v, 'nn') + _mm(q * qd, st, 'nn')
    st_new = st * gamma + _mm(k * kd, v, 'tn')
    return y, st_new


def _ssd_heads(dtraw, dtb, a_c, tri):
    dt = _softplus(dtraw + dtb)
    return dt, _dot_sel(dt * a_c, tri, left=True)


def _ssd_group(dte0, dte1, cum0, cum1, xs0, xs1, bm, cm, ht0, ht1, eye2, bdm, last):
    cbp = _mm(cm, jnp.concatenate([bm, bm], axis=0), 'nt')
    outs = []
    for dte, cum, xs, ht in ((dte0, cum0, xs0, ht0), (dte1, cum1, xs1, ht1)):
        r = jnp.sum(cum * eye2, axis=0, keepdims=True)
        dlt = cum - r
        seg = jnp.exp(jnp.where(dlt > 0.0, -dlt, dlt))
        xdt = xs * dte
        bd = jnp.concatenate([xdt, xdt], axis=0) * bdm
        clast = jnp.sum(cum * last, axis=0, keepdims=True)
        y = _mm(cbp * seg, bd, 'nn') + jnp.exp(cum) * _mm(cm, ht, 'nn')
        ht_new = jnp.exp(clast) * ht + _mm(bm, xdt * jnp.exp(clast - cum), 'tn')
        outs += [y, ht_new]
    return tuple(outs)


def _scan_in_specs(nc, rev):
    ch = (lambda c: nc - 1 - c) if rev else (lambda c: c)
    col = lambda w, blk: pl.BlockSpec((CHUNK, w), lambda c: (ch(c), blk))
    return [col(RET_QK, COL_Q // RET_QK), col(RET_QK, COL_K // RET_QK), col(RET_V, COL_V // RET_V),
            col(SSM_INNER, 0), col(SSM_BC, 2), col(SSM_BC, 3),
            col(DT_PAD, 0), col(LANES, 0), col(LANES, 0)]


def _const_specs(consts):
    return [_full(c.shape) for c in consts]


def _tile(t):
    return slice(t * LANES, (t + 1) * LANES)


def _scan_fwd(proj, xbc, dtraw, cos2, sin2, a_c, dtb, name, rider=None):
    s = proj.shape[0]
    nc = s // CHUNK
    consts, gam = _scan_tables()
    r_arrs, r_in, r_out, r_shape, r_scratch = _rider_args(rider)

    def body(q_ref, k_ref, v_ref, xs_ref, bm_ref, cm_ref, dt_ref, cos_ref, sin_ref, ac_ref, dtb_ref,
             ri_ref, qd_ref, kd_ref, eye_ref, bdm_ref, last_ref, e_ref, et_ref, tri_ref, trit_ref,
             yr_ref, ys_ref, sh_ref, hh_ref, st_sc, ht_sc):
        @pl.when(pl.program_id(0) == 0)
        def _():
            st_sc[...] = jnp.zeros_like(st_sc)
            ht_sc[...] = jnp.zeros_like(ht_sc)

        sh_ref[0] = st_sc[...]
        hh_ref[0] = ht_sc[...]
        cos2, sin2 = cos_ref[...], sin_ref[...]
        st_new = []
        for h in range(RET_HEADS):
            ql = slice(h * RET_QK_DIM, (h + 1) * RET_QK_DIM)
            vl = slice(h * RET_V_DIM, (h + 1) * RET_V_DIM)
            y, st_h = _ret_step(_rope(q_ref[:, ql].astype(F32), cos2, sin2), _rope(k_ref[:, ql].astype(F32), cos2, sin2),
                                v_ref[:, vl].astype(F32), st_sc[ql, :], ri_ref[h], qd_ref[:, ql], kd_ref[:, ql], gam[h])
            yr_ref[:, vl] = y
            st_new.append(st_h)
        dt, cum_c = _ssd_heads(dt_ref[...], dtb_ref[...], ac_ref[...], tri_ref[...])
        both = jnp.concatenate([dt, cum_c], axis=0)
        eye2, bdm, last = eye_ref[...], bdm_ref[...], last_ref[...]
        ht_new = []
        for g in range(SSM_GROUPS):
            t0, t1 = 2 * g, 2 * g + 1
            e0, e1 = _dot_sel(both, e_ref[:, _tile(t0)]), _dot_sel(both, e_ref[:, _tile(t1)])
            y0, h0, y1, h1 = _ssd_group(e0[:CHUNK], e1[:CHUNK], e0[CHUNK:], e1[CHUNK:],
                                        xs_ref[:, _tile(t0)], xs_ref[:, _tile(t1)], bm_ref[:, _tile(g)],
                                        cm_ref[:, _tile(g)], ht_sc[:, _tile(t0)], ht_sc[:, _tile(t1)], eye2, bdm, last)
            ys_ref[:, _tile(t0)] = y0
            ys_ref[:, _tile(t1)] = y1
            ht_new += [h0, h1]
        for h in range(RET_HEADS):
            st_sc[h * RET_QK_DIM:(h + 1) * RET_QK_DIM, :] = st_new[h]
        for t in range(N_LTILE):
            ht_sc[:, _tile(t)] = ht_new[t]

    in_specs = _scan_in_specs(nc, False) + [_full((1, DT_PAD)), _full((1, DT_PAD))] + _const_specs(consts)
    return pl.pallas_call(
        _with_rider(body, len(in_specs), 4, 2, rider, (nc,)), grid=(nc,),
        in_specs=in_specs + r_in,
        out_specs=[pl.BlockSpec((CHUNK, RET_V), lambda c: (c, 0)),
                   pl.BlockSpec((CHUNK, SSM_INNER), lambda c: (c, 0)),
                   pl.BlockSpec((1, RET_QK, RET_V_DIM), lambda c: (c, 0, 0)),
                   pl.BlockSpec((1, SSM_STATE, SSM_INNER), lambda c: (c, 0, 0))] + r_out,
        out_shape=[_sds((s, RET_V), F32), _sds((s, SSM_INNER), F32),
                   _sds((nc, RET_QK, RET_V_DIM), F32), _sds((nc, SSM_STATE, SSM_INNER), F32)] + r_shape,
        scratch_shapes=[pltpu.VMEM((RET_QK, RET_V_DIM), F32), pltpu.VMEM((SSM_STATE, SSM_INNER), F32)] + r_scratch,
        compiler_params=_cparams(("arbitrary",)), name=name,
    )(proj, proj, proj, xbc, xbc, xbc, dtraw, cos2, sin2, a_c, dtb, *consts, *r_arrs)


def _scan_bwd(proj, xbc, dtraw, cos2, sin2, a_c, dtb, s_hist, h_hist, dyr, dys, dxs_skip, name, rider=None):
    s = proj.shape[0]
    nc = s // CHUNK
    consts, gam = _scan_tables()
    rv = lambda c: nc - 1 - c
    r_arrs, r_in, r_out, r_shape, r_scratch = _rider_args(rider)

    def body(q_ref, k_ref, v_ref, xs_ref, bm_ref, cm_ref, dt_ref, cos_ref, sin_ref, ac_ref, dtb_ref,
             ri_ref, qd_ref, kd_ref, eye_ref, bdm_ref, last_ref, e_ref, et_ref, tri_ref, trit_ref,
             sh_ref, hh_ref, dyr_ref, dys_ref, dsk_ref,
             dqk_ref, dv_ref, dxbc_ref, ddt_ref, dac_ref, ddtb_ref, dst_sc, dht_sc):
        @pl.when(pl.program_id(0) == 0)
        def _():
            dst_sc[...] = jnp.zeros_like(dst_sc)
            dht_sc[...] = jnp.zeros_like(dht_sc)
            dac_ref[...] = jnp.zeros_like(dac_ref)
            ddtb_ref[...] = jnp.zeros_like(ddtb_ref)

        cos2, sin2 = cos_ref[...], sin_ref[...]
        dst_new = []
        for h in range(RET_HEADS):
            ql = slice(h * RET_QK_DIM, (h + 1) * RET_QK_DIM)
            vl = slice(h * RET_V_DIM, (h + 1) * RET_V_DIM)
            step = functools.partial(_ret_step, r_intra=ri_ref[h], qd=qd_ref[:, ql], kd=kd_ref[:, ql], gamma=gam[h])
            _, vjp = jax.vjp(step, _rope(q_ref[:, ql].astype(F32), cos2, sin2), _rope(k_ref[:, ql].astype(F32), cos2, sin2),
                             v_ref[:, vl].astype(F32), sh_ref[0, ql, :])
            dq, dk, dv, dst = vjp((dyr_ref[:, vl], dst_sc[ql, :]))
            dqk_ref[:, ql] = _rope_t(dq, cos2, sin2).astype(dqk_ref.dtype)
            dqk_ref[:, slice(RET_QK + ql.start, RET_QK + ql.stop)] = _rope_t(dk, cos2, sin2).astype(dqk_ref.dtype)
            dv_ref[:, vl] = dv.astype(dv_ref.dtype)
            dst_new.append(dst)
        dtraw_v, dtb_v, a_c, tri = dt_ref[...], dtb_ref[...], ac_ref[...], tri_ref[...]
        dt, cum_c = _ssd_heads(dtraw_v, dtb_v, a_c, tri)
        both = jnp.concatenate([dt, cum_c], axis=0)
        eye2, bdm, last = eye_ref[...], bdm_ref[...], last_ref[...]
        group = functools.partial(_ssd_group, eye2=eye2, bdm=bdm, last=last)
        d_both = jnp.zeros((2 * CHUNK, LANES), F32)
        dht_new = []
        for g in range(SSM_GROUPS):
            t0, t1 = 2 * g, 2 * g + 1
            e0, e1 = _dot_sel(both, e_ref[:, _tile(t0)]), _dot_sel(both, e_ref[:, _tile(t1)])
            _, vjp = jax.vjp(group, e0[:CHUNK], e1[:CHUNK], e0[CHUNK:], e1[CHUNK:],
                             xs_ref[:, _tile(t0)], xs_ref[:, _tile(t1)], bm_ref[:, _tile(g)], cm_ref[:, _tile(g)],
                             hh_ref[0, :, _tile(t0)], hh_ref[0, :, _tile(t1)])
            (d_dte0, d_dte1, d_cum0, d_cum1, d_xs0, d_xs1, d_bm, d_cm, d_ht0, d_ht1) = vjp(
                (dys_ref[:, _tile(t0)], dht_sc[:, _tile(t0)], dys_ref[:, _tile(t1)], dht_sc[:, _tile(t1)]))
            d_both = d_both + _dot_sel(jnp.concatenate([d_dte0, d_cum0], axis=0), et_ref[_tile(t0), :])
            d_both = d_both + _dot_sel(jnp.concatenate([d_dte1, d_cum1], axis=0), et_ref[_tile(t1), :])
            dxbc_ref[:, _tile(t0)] = d_xs0 + dsk_ref[:, _tile(t0)]
            dxbc_ref[:, _tile(t1)] = d_xs1 + dsk_ref[:, _tile(t1)]
            dxbc_ref[:, _tile(N_LTILE + g)] = d_bm
            dxbc_ref[:, _tile(N_LTILE + SSM_GROUPS + g)] = d_cm
            dht_new += [d_ht0, d_ht1]
        d_da = _dot_sel(d_both[CHUNK:], trit_ref[...], left=True)
        d_dt = d_both[:CHUNK] + d_da * a_c
        d_pre = d_dt * jax.nn.sigmoid(dtraw_v + dtb_v)
        ddt_ref[...] = d_pre
        ddtb_ref[...] += jnp.sum(d_pre, axis=0, keepdims=True)
        dac_ref[...] += jnp.sum(d_da * dt, axis=0, keepdims=True)
        for h in range(RET_HEADS):
            dst_sc[h * RET_QK_DIM:(h + 1) * RET_QK_DIM, :] = dst_new[h]
        for t in range(N_LTILE):
            dht_sc[:, _tile(t)] = dht_new[t]

    in_specs = (_scan_in_specs(nc, True) + [_full((1, DT_PAD)), _full((1, DT_PAD))] + _const_specs(consts)
                + [pl.BlockSpec((1, RET_QK, RET_V_DIM), lambda c: (rv(c), 0, 0)),
                   pl.BlockSpec((1, SSM_STATE, SSM_INNER), lambda c: (rv(c), 0, 0)),
                   pl.BlockSpec((CHUNK, RET_V), lambda c: (rv(c), 0)),
                   pl.BlockSpec((CHUNK, SSM_INNER), lambda c: (rv(c), 0)),
                   pl.BlockSpec((CHUNK, SSM_INNER), lambda c: (rv(c), 0))])
    return pl.pallas_call(
        _with_rider(body, len(in_specs), 6, 2, rider, (nc,)), grid=(nc,),
        in_specs=in_specs + r_in,
        out_specs=[pl.BlockSpec((CHUNK, 2 * RET_QK), lambda c: (rv(c), 0)),
                   pl.BlockSpec((CHUNK, RET_V), lambda c: (rv(c), 0)),
                   pl.BlockSpec((CHUNK, SSM_CONV_DIM), lambda c: (rv(c), 0)),
                   pl.BlockSpec((CHUNK, DT_PAD), lambda c: (rv(c), 0)),
                   _full((1, DT_PAD)), _full((1, DT_PAD))] + r_out,
        out_shape=[_sds((s, 2 * RET_QK), MXU_DTYPE), _sds((s, RET_V), MXU_DTYPE),
                   _sds((s, SSM_CONV_DIM), F32), _sds((s, DT_PAD), F32),
                   _sds((1, DT_PAD), F32), _sds((1, DT_PAD), F32)] + r_shape,
        scratch_shapes=[pltpu.VMEM((RET_QK, RET_V_DIM), F32), pltpu.VMEM((SSM_STATE, SSM_INNER), F32)] + r_scratch,
        compiler_params=_cparams(("arbitrary",)), name=name,
    )(proj, proj, proj, xbc, xbc, xbc, dtraw, cos2, sin2, a_c, dtb, *consts, s_hist, h_hist, dyr, dys, dxs_skip, *r_arrs)


POST_W = 256


def _post_ret(y, g):
    return _rms(y) * _silu(g)


def _post_ssm(y, xs, z, dsk, nw):
    return _rms((y + xs * dsk) * _silu(z)) * nw


def _post_specs(t):
    return [pl.BlockSpec((t, RET_V), lambda i: (i, 0)),
            pl.BlockSpec((t, RET_V), lambda i: (i, COL_G // RET_V)),
            pl.BlockSpec((t, SSM_INNER), lambda i: (i, 0)),
            pl.BlockSpec((t, SSM_INNER), lambda i: (i, 0)),
            pl.BlockSpec((t, SSM_INNER), lambda i: (i, COL_Z // SSM_INNER)),
            _full((1, SSM_INNER)), _full((1, SSM_INNER))]


def _post_fwd(y_ret, proj, y_ssm, xbc, dsk_e, ssm_norm, name):
    s = y_ret.shape[0]
    t = _pick(s, 256)

    def body(yr_ref, g_ref, ys_ref, xs_ref, z_ref, dsk_ref, nw_ref, or_ref, os_ref, ort_ref, ost_ref):
        for h in range(RET_V // POST_W):
            sl = slice(h * POST_W, (h + 1) * POST_W)
            o = _post_ret(yr_ref[:, sl], g_ref[:, sl].astype(F32))
            or_ref[:, sl] = o.astype(or_ref.dtype)
            ort_ref[sl, :] = o.T.astype(ort_ref.dtype)
        for g in range(SSM_INNER // POST_W):
            sl = slice(g * POST_W, (g + 1) * POST_W)
            o = _post_ssm(ys_ref[:, sl], xs_ref[:, sl], z_ref[:, sl].astype(F32), dsk_ref[:, sl], nw_ref[:, sl])
            os_ref[:, sl] = o.astype(os_ref.dtype)
            ost_ref[sl, :] = o.T.astype(ost_ref.dtype)

    return pl.pallas_call(
        body, grid=(s // t,), in_specs=_post_specs(t),
        out_specs=[pl.BlockSpec((t, RET_V), lambda i: (i, 0)), pl.BlockSpec((t, SSM_INNER), lambda i: (i, 0)),
                   pl.BlockSpec((RET_V, t), lambda i: (0, i)), pl.BlockSpec((SSM_INNER, t), lambda i: (0, i))],
        out_shape=[_sds((s, RET_V), MXU_DTYPE), _sds((s, SSM_INNER), MXU_DTYPE),
                   _sds((RET_V, s), MXU_DTYPE), _sds((SSM_INNER, s), MXU_DTYPE)],
        compiler_params=_cparams(("parallel",)), name=name,
    )(y_ret, proj, y_ssm, xbc, proj, dsk_e, ssm_norm.reshape(1, SSM_INNER))


def _post_bwd(y_ret, proj, y_ssm, xbc, dsk_e, ssm_norm, d_or, d_os, name, rider=None):
    s = y_ret.shape[0]
    t = _pick(s, 256)
    r_arrs, r_in, r_out, r_shape, r_scratch = _rider_args(rider)

    def body(yr_ref, g_ref, ys_ref, xs_ref, z_ref, dsk_ref, nw_ref, dor_ref, dos_ref,
             dyr_ref, dg_ref, dys_ref, dxs_ref, dz_ref, ddsk_ref, dnw_ref):
        @pl.when(pl.program_id(0) == 0)
        def _():
            ddsk_ref[...] = jnp.zeros_like(ddsk_ref)
            dnw_ref[...] = jnp.zeros_like(dnw_ref)

        for h in range(RET_V // POST_W):
            sl = slice(h * POST_W, (h + 1) * POST_W)
            _, vjp = jax.vjp(_post_ret, yr_ref[:, sl], g_ref[:, sl].astype(F32))
            dyr, dg = vjp(dor_ref[:, sl])
            dyr_ref[:, sl] = dyr
            dg_ref[:, sl] = dg.astype(dg_ref.dtype)
        for g in range(SSM_INNER // POST_W):
            sl = slice(g * POST_W, (g + 1) * POST_W)
            _, vjp = jax.vjp(_post_ssm, ys_ref[:, sl], xs_ref[:, sl], z_ref[:, sl].astype(F32), dsk_ref[:, sl], nw_ref[:, sl])
            dy, dxs, dz, ddsk, dnw = vjp(dos_ref[:, sl])
            dys_ref[:, sl] = dy
            dxs_ref[:, sl] = dxs
            dz_ref[:, sl] = dz.astype(dz_ref.dtype)
            ddsk_ref[:, sl] += ddsk
            dnw_ref[:, sl] += dnw

    rowv = pl.BlockSpec((t, RET_V), lambda i: (i, 0))
    rows = pl.BlockSpec((t, SSM_INNER), lambda i: (i, 0))
    in_specs = _post_specs(t) + [rowv, rows]
    return pl.pallas_call(
        _with_rider(body, len(in_specs), 7, 0, rider, (s // t,)), grid=(s // t,), in_specs=in_specs + r_in,
        out_specs=[rowv, rowv, rows, rows, rows, _full((1, SSM_INNER)), _full((1, SSM_INNER))] + r_out,
        out_shape=[_sds((s, RET_V), F32), _sds((s, RET_V), MXU_DTYPE), _sds((s, SSM_INNER), F32),
                   _sds((s, SSM_INNER), F32), _sds((s, SSM_INNER), MXU_DTYPE),
                   _sds((1, SSM_INNER), F32), _sds((1, SSM_INNER), F32)] + r_shape,
        scratch_shapes=r_scratch,
        compiler_params=_cparams(("arbitrary",)), name=name,
    )(y_ret, proj, y_ssm, xbc, proj, dsk_e, ssm_norm.reshape(1, SSM_INNER), d_or, d_os, *r_arrs)


def _merge_fn(gr, gs, br, bs, yr, ys):
    return jax.nn.sigmoid(gr + br) * yr + jax.nn.sigmoid(gs + bs) * ys


def _merge_specs(t):
    row = pl.BlockSpec((t, D_MODEL), lambda i: (i, 0))
    return [pl.BlockSpec((t, D_MODEL), lambda i: (i, COL_GATES // D_MODEL)),
            pl.BlockSpec((t, D_MODEL), lambda i: (i, COL_GATES // D_MODEL + 1)),
            pl.BlockSpec((1, D_MODEL), lambda i: (0, 0)), pl.BlockSpec((1, D_MODEL), lambda i: (0, 1)), row, row]


def _merge_fwd(proj, b_gate, br_ret, br_ssm, name):
    s = proj.shape[0]
    t = _pick(s, 512)

    def body(gr_ref, gs_ref, br_ref, bs_ref, yr_ref, ys_ref, o_ref, ot_ref):
        o = _merge_fn(gr_ref[...].astype(F32), gs_ref[...].astype(F32), br_ref[...], bs_ref[...], yr_ref[...], ys_ref[...])
        o_ref[...] = o.astype(o_ref.dtype)
        ot_ref[...] = o.T.astype(ot_ref.dtype)

    bg = b_gate.reshape(1, 2 * D_MODEL)
    return pl.pallas_call(
        body, grid=(s // t,), in_specs=_merge_specs(t),
        out_specs=[pl.BlockSpec((t, D_MODEL), lambda i: (i, 0)), pl.BlockSpec((D_MODEL, t), lambda i: (0, i))],
        out_shape=[_sds((s, D_MODEL), MXU_DTYPE), _sds((D_MODEL, s), MXU_DTYPE)],
        compiler_params=_cparams(("parallel",)), name=name,
    )(proj, proj, bg, bg, br_ret, br_ssm)


def _merge_bwd(proj, b_gate, br_ret, br_ssm, dm, name):
    s = proj.shape[0]
    t = _pick(s, 512)

    def body(gr_ref, gs_ref, br_ref, bs_ref, yr_ref, ys_ref, dm_ref, dgt_ref, db_ref, dyr_ref, dys_ref):
        @pl.when(pl.program_id(0) == 0)
        def _():
            db_ref[...] = jnp.zeros_like(db_ref)

        _, vjp = jax.vjp(_merge_fn, gr_ref[...].astype(F32), gs_ref[...].astype(F32), br_ref[...], bs_ref[...],
                         yr_ref[...], ys_ref[...])
        dgr, dgs, dbr, dbs, dyr, dys = vjp(dm_ref[...])
        dgt_ref[:, :D_MODEL] = dgr.astype(dgt_ref.dtype)
        dgt_ref[:, D_MODEL:] = dgs.astype(dgt_ref.dtype)
        db_ref[:, :D_MODEL] += dbr
        db_ref[:, D_MODEL:] += dbs
        dyr_ref[...] = dyr.astype(dyr_ref.dtype)
        dys_ref[...] = dys.astype(dys_ref.dtype)

    bg = b_gate.reshape(1, 2 * D_MODEL)
    row = pl.BlockSpec((t, D_MODEL), lambda i: (i, 0))
    return pl.pallas_call(
        body, grid=(s // t,), in_specs=_merge_specs(t) + [row],
        out_specs=[pl.BlockSpec((t, 2 * D_MODEL), lambda i: (i, 0)), _full((1, 2 * D_MODEL)), row, row],
        out_shape=[_sds((s, 2 * D_MODEL), MXU_DTYPE), _sds((1, 2 * D_MODEL), F32),
                   _sds((s, D_MODEL), MXU_DTYPE), _sds((s, D_MODEL), MXU_DTYPE)],
        compiler_params=_cparams(("arbitrary",)), name=name,
    )(proj, proj, bg, bg, br_ret, br_ssm, dm)


def _attn_head(q, k, v):
    sc = _mm(q, k, 'nt') * (XA_HEAD_DIM ** -0.5)
    e = jnp.exp(sc - lax.stop_gradient(jnp.max(sc, axis=-1, keepdims=True)))
    p = e / jnp.sum(e, axis=-1, keepdims=True)
    return _mm(p, v, 'nn')


def _attn_fwd(q, kv, name):
    s = q.shape[0]
    m = kv.shape[0]
    t = _pick(s, 512)

    def body(q_ref, kv_ref, o_ref, ot_ref):
        for h in range(XA_HEADS):
            sl = slice(h * XA_HEAD_DIM, (h + 1) * XA_HEAD_DIM)
            vl = slice(D_MODEL + h * XA_HEAD_DIM, D_MODEL + (h + 1) * XA_HEAD_DIM)
            o = _attn_head(q_ref[:, sl], kv_ref[:, sl], kv_ref[:, vl])
            o_ref[:, sl] = o.astype(o_ref.dtype)
            ot_ref[sl, :] = o.T.astype(ot_ref.dtype)

    return pl.pallas_call(
        body, grid=(s // t,),
        in_specs=[pl.BlockSpec((t, D_MODEL), lambda i: (i, 0)), _full((m, 2 * D_MODEL))],
        out_specs=[pl.BlockSpec((t, D_MODEL), lambda i: (i, 0)), pl.BlockSpec((D_MODEL, t), lambda i: (0, i))],
        out_shape=[_sds((s, D_MODEL), MXU_DTYPE), _sds((D_MODEL, s), MXU_DTYPE)],
        compiler_params=_cparams(("parallel",)), name=name,
    )(q, kv)


def _attn_bwd(q, kv, d_o, name):
    s = q.shape[0]
    m = kv.shape[0]
    t = _pick(s, 512)

    def body(q_ref, kv_ref, do_ref, dq_ref, dkv_ref):
        @pl.when(pl.program_id(0) == 0)
        def _():
            dkv_ref[...] = jnp.zeros_like(dkv_ref)

        for h in range(XA_HEADS):
            sl = slice(h * XA_HEAD_DIM, (h + 1) * XA_HEAD_DIM)
            vl = slice(D_MODEL + h * XA_HEAD_DIM, D_MODEL + (h + 1) * XA_HEAD_DIM)
            _, vjp = jax.vjp(_attn_head, q_ref[:, sl], kv_ref[:, sl], kv_ref[:, vl])
            dq, dk, dv = vjp(do_ref[:, sl])
            dq_ref[:, sl] = dq.astype(dq_ref.dtype)
            dkv_ref[:, sl] += dk
            dkv_ref[:, vl] += dv

    row = pl.BlockSpec((t, D_MODEL), lambda i: (i, 0))
    return pl.pallas_call(
        body, grid=(s // t,), in_specs=[row, _full((m, 2 * D_MODEL)), row],
        out_specs=[row, _full((m, 2 * D_MODEL))],
        out_shape=[_sds((s, D_MODEL), MXU_DTYPE), _sds((m, 2 * D_MODEL), F32)],
        compiler_params=_cparams(("arbitrary",)), name=name,
    )(q, kv, d_o)


def _loss_head(x, w, target, name):
    s, d = x.shape
    t = _pick(s, 512)

    def body(x_ref, w_ref, t_ref, loss_ref, dx_ref, dxb_ref, dw_ref):
        @pl.when(pl.program_id(0) == 0)
        def _():
            loss_ref[...] = jnp.zeros_like(loss_ref)
            dw_ref[...] = jnp.zeros_like(dw_ref)

        y, vjp = jax.vjp(_rmsnorm_fn, x_ref[...], w_ref[...])
        err = y - t_ref[...]
        loss_ref[...] += 0.5 * jnp.sum(jnp.sum(err * err, axis=-1, keepdims=True), axis=0, keepdims=True) / d
        dx, dw = vjp(err * (1.0 / d))
        dx_ref[...] = dx
        dxb_ref[...] = dx.astype(dxb_ref.dtype)
        dw_ref[...] += dw

    row = pl.BlockSpec((t, d), lambda i: (i, 0))
    return pl.pallas_call(
        body, grid=(s // t,), in_specs=[row, _full((1, d)), row],
        out_specs=[_full((1, LANES)), row, row, _full((1, d))],
        out_shape=[_sds((1, LANES), F32), _sds((s, d), F32), _sds((s, d), MXU_DTYPE), _sds((1, d), F32)],
        compiler_params=_cparams(("arbitrary",)), name=name,
    )(x, w.reshape(1, d), target)


def _epi_sqrelu(acc):
    r = jnp.maximum(acc, 0.0)
    return r * r, r * r


def _epi_sqrelu_bwd(acc, act):
    return (acc * (2.0 * jnp.sqrt(act.astype(F32))),)


def _rope_tables(positions):
    inv_freq = ROPE_THETA ** (-jnp.arange(0, RET_QK_DIM, 2, dtype=F32) / RET_QK_DIM)
    ang = positions.astype(F32)[:, None] * inv_freq
    cos, sin = jnp.cos(ang), jnp.sin(ang)
    return jnp.concatenate([cos, cos], axis=1), jnp.concatenate([-sin, sin], axis=1)


W_IN_ORIG = (('q', 0, 512), ('k', 512, 1024), ('v', 1024, 2048), ('g', 2048, 3072), ('z', 3072, 5120),
             ('xbc', 5120, 9216), ('dt', 9216, 9248), ('gates', 9248, 11296))
W_IN_MAIN_ORDER = ('z', 'xbc', 'gates', 'v', 'g', 'q', 'k')
W_IN_SHARD = IN_DIM // N_DEV


def _shard_segments(lo, hi):
    segs = []
    for j in range(lo // W_IN_SHARD, (hi - 1) // W_IN_SHARD + 1):
        segs.append((j, max(lo, j * W_IN_SHARD) - j * W_IN_SHARD, min(hi, (j + 1) * W_IN_SHARD) - j * W_IN_SHARD))
    return segs


def _w_in_from_shards(g):
    rng = {name: (lo, hi) for name, lo, hi in W_IN_ORIG}
    cols = [g[j][:, a:b] for name in W_IN_MAIN_ORDER for j, a, b in _shard_segments(*rng[name])]
    (j, a, b), = _shard_segments(*rng['dt'])
    return jnp.concatenate(cols, axis=1), jnp.pad(g[j][:, a:b], ((0, 0), (0, DT_PAD - SSM_HEADS)))


def _w_in_grad_blocks(d, d_dt):
    src_of = {'q': ('qk', 0), 'k': ('qk', RET_QK)}
    blocks = []
    for j in range(N_DEV):
        lo_j, hi_j = j * W_IN_SHARD, (j + 1) * W_IN_SHARD
        cols = []
        for name, lo, hi in W_IN_ORIG:
            a, b = max(lo, lo_j), min(hi, hi_j)
            if a >= b:
                continue
            if name == 'dt':
                cols.append(d_dt[:, a - lo:b - lo])
            else:
                key, off = src_of.get(name, (name, 0))
                cols.append(d[key][:, off + a - lo:off + b - lo])
        blocks.append(jnp.concatenate(cols, axis=1))
    return blocks


def _lanes_of_heads(v):
    return jnp.repeat(v, SSM_HEAD_DIM).reshape(1, SSM_INNER)


def _heads_of_lanes(v):
    return v.reshape(SSM_HEADS, SSM_HEAD_DIM).sum(axis=1)


def _layer_fwd(x, mem, cos2, sin2, p, l, later_blocks, rider_proj=None):
    n = lambda s: f"{s}_l{l}"
    sv = {'x0': x}
    u, u_t = _rmsnorm(x, p['norm_mix'], n("norm_mix"))
    s = x.shape[0]
    proj = _matmul(u, p['w_in_main'], 'nn', n("in_proj"), out_dtypes=(MXU_DTYPE,), tiles=(s, 512, D_MODEL), rider=rider_proj)
    if rider_proj is not None:
        proj, next_first = proj
    else:
        next_first = []
    dtraw = _matmul(u, p['w_in_dt'], 'nn', n("in_proj_dt"))
    xbc = _conv_fwd(proj, p['conv_w'], p['conv_b'], n("conv"))
    a_c = jnp.pad(-jnp.exp(p['a_log']), (0, DT_PAD - SSM_HEADS)).reshape(1, DT_PAD)
    dtb = jnp.pad(p['dt_bias'], (0, DT_PAD - SSM_HEADS)).reshape(1, DT_PAD)
    y_ret, y_ssm, s_hist, h_hist, *gathered = _scan_fwd(proj, xbc, dtraw, cos2, sin2, a_c, dtb, n("scan"),
                                                        _AllGather(later_blocks))
    p = {**p, **_full_weights(LATER, gathered)}
    dsk_e = _lanes_of_heads(p['d_skip'])
    o_ret, o_ssm, o_ret_t, o_ssm_t = _post_fwd(y_ret, proj, y_ssm, xbc, dsk_e, p['ssm_norm'], n("post"))
    br_ret = _matmul(o_ret, p['w_br_ret'], 'nn', n("br_ret"))
    br_ssm = _matmul(o_ssm, p['w_br_ssm'], 'nn', n("br_ssm"))
    merged, merged_t = _merge_fwd(proj, p['b_gate'], br_ret, br_ssm, n("merge"))
    x1 = _matmul(merged, p['w_out'], 'nn', n("w_out"), extras=(x,), epi=_epi_add)
    sv.update(u_t=u_t, proj=proj, dtraw=dtraw, xbc=xbc, a_c=a_c, dtb=dtb, y_ret=y_ret, y_ssm=y_ssm, s_hist=s_hist,
              h_hist=h_hist, dsk_e=dsk_e, o_ret_t=o_ret_t, o_ssm_t=o_ssm_t, br_ret=br_ret, br_ssm=br_ssm,
              merged_t=merged_t, x1=x1)
    hq, hq_t = _rmsnorm(x1, p['norm_xa'], n("norm_xa"))
    memn, _ = _rmsnorm(mem, p['norm_mem'], n("norm_mem"))
    q = _matmul(hq, p['xa_wq'], 'nn', n("xa_q"))
    kv = _matmul(memn, p['xa_wkv'], 'nn', n("xa_kv"))
    o, o_t = _attn_fwd(q, kv, n("attn"))
    x2 = _matmul(o, p['xa_wo'], 'nn', n("xa_o"), extras=(x1,), epi=_epi_add)
    sv.update(hq_t=hq_t, memn=memn, q=q, kv=kv, o_t=o_t, x2=x2)
    hm, hm_t = _rmsnorm(x2, p['norm_mlp'], n("norm_mlp"))
    act, act_t = _matmul(hm, p['mlp_w1'], 'nn', n("mlp_1"), epi=_epi_sqrelu, out_dtypes=(MXU_DTYPE, MXU_DTYPE),
                         out_t=(False, True))
    x3 = _matmul(act, p['mlp_w2'], 'nn', n("mlp_2"), extras=(x2,), epi=_epi_add)
    sv.update(hm_t=hm_t, act=act, act_t=act_t)
    return x3, sv, p, next_first


def _layer_bwd(dx, dxb, mem, cos2, sin2, p, sv, l, pending, c_idx):
    n = lambda s: f"{s}_bwd_l{l}"
    gd = (MXU_DTYPE,)
    g = {}
    g['mlp_w2'] = _matmul(sv['act_t'], dxb, 'nn', n("mlp_2_dw"), out_dtypes=gd)
    da = _matmul(dxb, p['mlp_w2'], 'nt', n("mlp_2_dx"), extras=(sv['act'],), epi=_epi_sqrelu_bwd, out_dtypes=(MXU_DTYPE,),
                 tiles=(_pick(dxb.shape[0], 512), D_FF, D_MODEL))
    g['mlp_w1'] = _matmul(sv['hm_t'], da, 'nn', n("mlp_1_dw"), out_dtypes=gd)
    dhm = _matmul(da, p['mlp_w1'], 'nt', n("mlp_1_dx"))
    dx2, dx2b, g['norm_mlp'] = _rmsnorm_bwd(sv['x2'], p['norm_mlp'], dhm, dx, n("norm_mlp"))
    g['xa_wo'] = _matmul(sv['o_t'], dx2b, 'nn', n("xa_o_dw"), out_dtypes=gd)
    d_o = _matmul(dx2b, p['xa_wo'], 'nt', n("xa_o_dx"))
    dq, dkv = _attn_bwd(sv['q'], sv['kv'], d_o, n("attn"))
    g['xa_wq'] = _matmul(sv['hq_t'], dq, 'nn', n("xa_q_dw"), out_dtypes=gd)
    dhq = _matmul(dq, p['xa_wq'], 'nt', n("xa_q_dx"))
    g['xa_wkv'] = _matmul(sv['memn'], dkv, 'tn', n("xa_kv_dw"), out_dtypes=gd)
    dmemn = _matmul(dkv, p['xa_wkv'], 'nt', n("xa_kv_dx"))
    _, _, g['norm_mem'] = _rmsnorm_bwd(mem, p['norm_mem'], dmemn, None, n("norm_mem"))
    dx1, dx1b, g['norm_xa'] = _rmsnorm_bwd(sv['x1'], p['norm_xa'], dhq, dx2, n("norm_xa"))
    g['w_out'] = _matmul(sv['merged_t'], dx1b, 'nn', n("w_out_dw"), out_dtypes=gd)
    dmerged = _matmul(dx1b, p['w_out'], 'nt', n("w_out_dx"))
    dgates, g['b_gate'], dbr_ret, dbr_ssm = _merge_bwd(sv['proj'], p['b_gate'], sv['br_ret'], sv['br_ssm'], dmerged, n("merge"))
    g['w_br_ret'] = _matmul(sv['o_ret_t'], dbr_ret, 'nn', n("br_ret_dw"), out_dtypes=gd)
    g['w_br_ssm'] = _matmul(sv['o_ssm_t'], dbr_ssm, 'nn', n("br_ssm_dw"), out_dtypes=gd)
    d_or = _matmul(dbr_ret, p['w_br_ret'], 'nt', n("br_ret_dx"))
    d_os = _matmul(dbr_ssm, p['w_br_ssm'], 'nt', n("br_ssm_dx"))
    later_by_core = [_grad_scatter(k, g[k]) for k in LATER]
    dyr, dg, dys, dxs_skip, dz, ddsk_e, g['ssm_norm'], *later_sib = _post_bwd(
        sv['y_ret'], sv['proj'], sv['y_ssm'], sv['xbc'], sv['dsk_e'], p['ssm_norm'], d_or, d_os, n("post"),
        _ExchangeCores(later_by_core))
    rider = _ExchangeChips(list(pending) + _core_sums(LATER, later_by_core, later_sib, c_idx, l))
    g['d_skip'] = _heads_of_lanes(ddsk_e)
    dqk_r, dv_r, dxbc_act, ddtraw, dac, ddtb, *delivered = _scan_bwd(
        sv['proj'], sv['xbc'], sv['dtraw'], cos2, sin2, sv['a_c'], sv['dtb'], sv['s_hist'], sv['h_hist'],
        dyr, dys, dxs_skip, n("scan"), rider)
    g['a_log'] = dac[0, :SSM_HEADS] * (-jnp.exp(p['a_log']))
    g['dt_bias'] = ddtb[0, :SSM_HEADS]
    dxbc_raw, g['conv_w'], g['conv_b'] = _conv_bwd(sv['proj'], p['conv_w'], p['conv_b'], dxbc_act, n("conv"))
    pieces = {'z': dz, 'xbc': dxbc_raw, 'gates': dgates, 'v': dv_r, 'g': dg, 'qk': dqk_r}
    d_w = {k: _matmul(sv['u_t'], pc, 'nn', n(f"in_proj_dw_{k}"), out_dtypes=gd) for k, pc in pieces.items()}
    d_dt = _matmul(sv['u_t'], ddtraw, 'nn', n("in_proj_dt_dw"), out_dtypes=gd)
    g['w_in'] = _w_in_grad_blocks(d_w, d_dt)
    du_dt = _matmul(ddtraw, p['w_in_dt'], 'nt', n("in_proj_dt_dx"))
    du = _matmul_nt_pieces(list(pieces.values()), p['w_in_main'], n("in_proj_dx"), extras=(du_dt,), epi=_epi_add)
    dx0, dx0b, g['norm_mix'] = _rmsnorm_bwd(sv['x0'], p['norm_mix'], du, dx1, n("norm_mix"))
    first_by_core = [_grad_scatter(k, g[k]) for k in FIRST]
    first_sib = _run_exchange(_ExchangeCores(first_by_core), n("grad_exchange_cores_first"))
    return (dx0, dx0b, g, delivered[:len(pending)], delivered[len(pending):],
            _core_sums(FIRST, first_by_core, first_sib, c_idx, l))


def _full_weights(names, gathered):
    p = {}
    for k, g in zip(names, gathered):
        if k == 'w_in':
            p['w_in_main'], p['w_in_dt'] = _w_in_from_shards(g)
        elif k in COL_SHARDED:
            p[k] = jnp.concatenate([g[j] for j in range(N_DEV)], axis=1)
        else:
            p[k] = g.reshape(-1, g.shape[-1])
    return p


def _grad_scatter(k, g):
    if k == 'w_in':
        blocks = g
    elif k in COL_SHARDED:
        c = g.shape[1] // N_DEV
        blocks = [g[:, j * c:(j + 1) * c] for j in range(N_DEV)]
    else:
        r = g.shape[0] // N_DEV
        blocks = [g[j * r:(j + 1) * r] for j in range(N_DEV)]
    return jnp.stack([jnp.stack([blocks[2 * chip + core] for chip in range(4)]) for core in range(2)])


def _core_sums(names, by_core, from_sibling, c_idx, l):
    return [_add_halves(a, o, c_idx, f"grad_add_cores_{k}_l{l}") for k, a, o in zip(names, by_core, from_sibling)]


def _step(x, mem, positions, small, blocks, loss_target):
    cos2, sin2 = _rope_tables(positions)
    first = _run_exchange(_AllGather([blocks[0][k] for k in FIRST]), "all_gather_first_l0")
    saved, layers = [], []
    for l in range(DEPTH):
        p = {k: small[k][l] for k in SMALL if k != 'norm_final'}
        p.update(_full_weights(FIRST, first))
        rider = _AllGather([blocks[l + 1][k] for k in FIRST]) if l + 1 < DEPTH else None
        x, sv, p, first = _layer_fwd(x, mem, cos2, sin2, p, l, [blocks[l][k] for k in LATER], rider)
        saved.append(sv)
        layers.append(p)
    loss, dx, dxb, dnf = _loss_head(x, small['norm_final'], loss_target, "loss_head")
    c_idx = lax.axis_index("c").astype(jnp.int32).reshape(1)
    grads, by_chip, pending = [None] * DEPTH, [dict() for _ in range(DEPTH)], []
    for l in reversed(range(DEPTH)):
        dx, dxb, grads[l], got_first, got_later, pending_next = _layer_bwd(
            dx, dxb, mem, cos2, sin2, layers[l], saved[l], l, pending, c_idx)
        if pending:
            by_chip[l + 1].update(zip(FIRST, got_first))
        by_chip[l].update(zip(LATER, got_later))
        pending = pending_next
    by_chip[0].update(zip(FIRST, _run_exchange(_ExchangeChips(pending), "grad_exchange_chips_first_l0")))
    small_g = {}
    for k in SMALL:
        small_g[k] = dnf.reshape(D_MODEL) if k == 'norm_final' else [grads[l][k].reshape(small[k].shape[1:]) for l in range(DEPTH)]
    return loss, dx, small_g, by_chip


MESH = pl.DeviceIdType.MESH
ANY_SPEC = pl.BlockSpec(memory_space=pl.ANY)


def _mesh_pos():
    return lax.axis_index("x"), lax.axis_index("y"), lax.axis_index("c")


def _other_chips(x, y):
    return [(1 - x, y), (x, 1 - y), (1 - x, 1 - y)]


class _AllGather:
    def __init__(self, arrs):
        self.arrs = list(arrs)
        na = self.n = len(self.arrs)
        self.out_shape = [_sds((N_DEV,) + a.shape, a.dtype) for a in self.arrs]
        self.scratch = [pltpu.SemaphoreType.DMA((na, 7)), pltpu.SemaphoreType.DMA((na, 7)), pltpu.SemaphoreType.DMA((na,))]

    def _copies(self, x_refs, o_refs, sems):
        send_sems, recv_sems, local_sems = sems
        x, y, c = _mesh_pos()
        me, sib = (x, y, c), (x, y, 1 - c)
        chips = _other_chips(x, y)

        def copy(a, k, block, to, src=None):
            dst = o_refs[a].at[4 * block[0] + 2 * block[1] + block[2]]
            return pltpu.make_async_remote_copy(src_ref=dst if src is None else src, dst_ref=dst,
                                                send_sem=send_sems.at[a, k], recv_sem=recv_sems.at[a, k],
                                                device_id=to, device_id_type=MESH)

        mine = [pltpu.make_async_copy(x_refs[a], o_refs[a].at[4 * x + 2 * y + c], local_sems.at[a]) for a in range(self.n)]
        first = []
        for a in range(self.n):
            first.append(copy(a, 0, me, sib, src=x_refs[a]))
            first += [copy(a, 1 + j, me, (*chip, c), src=x_refs[a]) for j, chip in enumerate(chips)]
        return copy, mine, first, me, sib, chips, c

    def start(self, x_refs, o_refs, sems):
        _, mine, first, *_ = self._copies(x_refs, o_refs, sems)
        for cp in mine + first:
            cp.start()

    def finish(self, x_refs, o_refs, sems):
        copy, mine, first, me, sib, chips, c = self._copies(x_refs, o_refs, sems)
        passed = []
        for a in range(self.n):
            for j, chip in enumerate(chips):
                copy(a, 1 + j, (*chip, c), me).wait_recv()
                cp = copy(a, 4 + j, (*chip, c), sib)
                cp.start()
                passed.append(cp)
        for a in range(self.n):
            copy(a, 0, sib, me).wait_recv()
            for j, chip in enumerate(chips):
                copy(a, 4 + j, (*chip, 1 - c), me).wait_recv()
        for cp in first + passed:
            cp.wait_send()
        for cp in mine:
            cp.wait()


class _ExchangeChips:
    def __init__(self, arrs):
        self.arrs = list(arrs)
        na = self.n = len(self.arrs)
        self.out_shape = [_sds(a.shape, a.dtype) for a in self.arrs]
        self.scratch = [pltpu.SemaphoreType.DMA((na, 3)), pltpu.SemaphoreType.DMA((na, 3)), pltpu.SemaphoreType.DMA((na,))]

    def _copies(self, a_refs, o_refs, sems):
        send_sems, recv_sems, local_sems = sems
        x, y, c = _mesh_pos()
        my_chip = 2 * x + y
        chips = _other_chips(x, y)
        mine = [pltpu.make_async_copy(a_refs[a].at[my_chip], o_refs[a].at[my_chip], local_sems.at[a]) for a in range(self.n)]
        sends = [pltpu.make_async_remote_copy(src_ref=a_refs[a].at[2 * px + py], dst_ref=o_refs[a].at[my_chip],
                                              send_sem=send_sems.at[a, j], recv_sem=recv_sems.at[a, j],
                                              device_id=(px, py, c), device_id_type=MESH)
                 for a in range(self.n) for j, (px, py) in enumerate(chips)]
        recvs = [pltpu.make_async_remote_copy(src_ref=a_refs[a].at[2 * px + py], dst_ref=o_refs[a].at[2 * px + py],
                                              send_sem=send_sems.at[a, j], recv_sem=recv_sems.at[a, j],
                                              device_id=(px, py, c), device_id_type=MESH)
                 for a in range(self.n) for j, (px, py) in enumerate(chips)]
        return mine, sends, recvs

    def start(self, a_refs, o_refs, sems):
        mine, sends, _ = self._copies(a_refs, o_refs, sems)
        for cp in mine + sends:
            cp.start()

    def finish(self, a_refs, o_refs, sems):
        mine, sends, recvs = self._copies(a_refs, o_refs, sems)
        for cp in recvs:
            cp.wait_recv()
        for cp in sends:
            cp.wait_send()
        for cp in mine:
            cp.wait()


def _run_exchange(ex, name):
    na = ex.n

    def body(*refs):
        i_refs, o_refs, sems = refs[:na], refs[na:2 * na], refs[2 * na:]
        ex.start(i_refs, o_refs, sems)
        ex.finish(i_refs, o_refs, sems)

    return pl.pallas_call(body, in_specs=[ANY_SPEC] * na, out_specs=[ANY_SPEC] * na, out_shape=ex.out_shape,
                          scratch_shapes=ex.scratch, name=name)(*ex.arrs)


class _ExchangeCores:
    def __init__(self, arrs):
        self.arrs = list(arrs)
        na = self.n = len(self.arrs)
        self.out_shape = [_sds(a.shape[1:], a.dtype) for a in self.arrs]
        self.scratch = [pltpu.SemaphoreType.DMA((na,)), pltpu.SemaphoreType.DMA((na,))]

    def _copies(self, a_refs, o_refs, sems):
        send_sems, recv_sems = sems
        x, y, c = _mesh_pos()
        return [pltpu.make_async_remote_copy(src_ref=a_refs[a].at[1 - c], dst_ref=o_refs[a], send_sem=send_sems.at[a],
                                             recv_sem=recv_sems.at[a], device_id=(x, y, 1 - c), device_id_type=MESH)
                for a in range(self.n)]

    def start(self, a_refs, o_refs, sems):
        for cp in self._copies(a_refs, o_refs, sems):
            cp.start()

    def finish(self, a_refs, o_refs, sems):
        for cp in self._copies(a_refs, o_refs, sems):
            cp.wait()


def _as_rows(a, lead):
    return a.reshape(a.shape[:lead] + (-1, a.shape[-1]))


def _add_halves(a, other, c_idx, name):
    a3, o2 = _as_rows(a, 1), _as_rows(other, 0)
    rows, cols = o2.shape
    tr = _pick(rows, 256)

    def body(c_ref, a_ref, o_ref, out_ref):
        out_ref[...] = (a_ref[0].astype(F32) + o_ref[...].astype(F32)).astype(out_ref.dtype)

    out = pl.pallas_call(
        body,
        grid_spec=pltpu.PrefetchScalarGridSpec(
            num_scalar_prefetch=1, grid=(rows // tr,),
            in_specs=[pl.BlockSpec((1, tr, cols), lambda i, c_ref: (c_ref[0], i, 0)),
                      pl.BlockSpec((tr, cols), lambda i, c_ref: (i, 0))],
            out_specs=pl.BlockSpec((tr, cols), lambda i, c_ref: (i, 0))),
        out_shape=_sds((rows, cols), a.dtype), compiler_params=_cparams(("parallel",)), name=name,
    )(c_idx, a3, o2)
    return out.reshape(other.shape)


def _all_reduce_small(v, name):
    r = v.shape[0]

    def body(v_ref, o_ref, slots, send_sems, recv_sems):
        x, y, c = _mesh_pos()
        me = 4 * x + 2 * y + c
        slots[me] = v_ref[...]
        cps = []
        for k in range(1, N_DEV):
            px = 1 - x if k & 4 else x
            py = 1 - y if k & 2 else y
            pc = 1 - c if k & 1 else c
            cps.append(pltpu.make_async_remote_copy(src_ref=v_ref, dst_ref=slots.at[me], send_sem=send_sems.at[k - 1],
                                                    recv_sem=recv_sems.at[k - 1], device_id=(px, py, pc), device_id_type=MESH))
        for cp in cps:
            cp.start()
        for cp in cps:
            cp.wait()
        acc = slots[0]
        for d in range(1, N_DEV):
            acc = acc + slots[d]
        o_ref[...] = acc

    vm = pl.BlockSpec(memory_space=pltpu.VMEM)
    return pl.pallas_call(
        body, in_specs=[vm], out_specs=vm, out_shape=_sds((r, LANES), F32),
        scratch_shapes=[pltpu.VMEM((N_DEV, r, LANES), F32), pltpu.SemaphoreType.DMA((N_DEV - 1,)),
                        pltpu.SemaphoreType.DMA((N_DEV - 1,))],
        compiler_params=pltpu.CompilerParams(vmem_limit_bytes=VMEM_LIMIT_BYTES), name=name,
    )(v)


def _adamw(w, g_slots, m, v, name):
    depth, rows, cols = w.shape
    ns = g_slots.shape[0]
    tr = _pick(rows, 256 if cols <= 1024 else 128)

    def body(w_ref, g_ref, m_ref, v_ref, go_ref, d_ref, mo_ref, vo_ref):
        g = g_ref[0, 0].astype(F32)
        for i in range(1, ns):
            g = g + g_ref[i, 0].astype(F32)
        m_new = ADAM_B1 * m_ref[0] + (1.0 - ADAM_B1) * g
        v_new = ADAM_B2 * v_ref[0] + (1.0 - ADAM_B2) * (g * g)
        m_hat = m_new / (1.0 - ADAM_B1 ** ADAM_STEP)
        v_hat = v_new / (1.0 - ADAM_B2 ** ADAM_STEP)
        go_ref[0] = g
        d_ref[0] = -ADAM_LR * (m_hat / (jnp.sqrt(v_hat) + ADAM_EPS) + ADAM_WD * w_ref[0])
        mo_ref[0] = m_new
        vo_ref[0] = v_new

    blk = pl.BlockSpec((1, tr, cols), lambda l, i: (l, i, 0))
    return pl.pallas_call(
        body, grid=(depth, rows // tr),
        in_specs=[blk, pl.BlockSpec((ns, 1, tr, cols), lambda l, i: (0, l, i, 0)), blk, blk],
        out_specs=[blk] * 4, out_shape=[_sds(w.shape, F32)] * 4,
        compiler_params=_cparams(("parallel", "parallel")), name=name,
    )(w, g_slots, m, v)


_ARG_NAMES = (['x', 'mem', 'positions'] + WEIGHTS + ['loss_target'] + ['m_' + n for n in WEIGHTS]
              + ['v_' + n for n in WEIGHTS])


PACK_TILE = 8 * LANES


def _pack_rows(parts):
    blocks = []
    for part in parts:
        flat = part.reshape(-1)
        pad = (-flat.shape[0]) % PACK_TILE
        blocks.append((jnp.pad(flat, (0, pad)) if pad else flat).reshape(-1, LANES))
    return jnp.concatenate(blocks, axis=0)


def _unpack_rows(packed, shapes):
    out, off = [], 0
    for shp in shapes:
        n = int(np.prod(shp))
        rows = -(-n // PACK_TILE) * 8
        out.append(packed[off:off + rows].reshape(-1)[:n].reshape(shp))
        off += rows
    return out


def kernel(x, mem, positions, norm_mix, w_in, b_gate, conv_w, conv_b, dt_bias, a_log, d_skip, ssm_norm, w_br_ret, w_br_ssm, w_out, norm_xa, norm_mem, xa_wq, xa_wkv, xa_wo, norm_mlp, mlp_w1, mlp_w2, norm_final, loss_target, m_norm_mix, m_w_in, m_b_gate, m_conv_w, m_conv_b, m_dt_bias, m_a_log, m_d_skip, m_ssm_norm, m_w_br_ret, m_w_br_ssm, m_w_out, m_norm_xa, m_norm_mem, m_xa_wq, m_xa_wkv, m_xa_wo, m_norm_mlp, m_mlp_w1, m_mlp_w2, m_norm_final, v_norm_mix, v_w_in, v_b_gate, v_conv_w, v_conv_b, v_dt_bias, v_a_log, v_d_skip, v_ssm_norm, v_w_br_ret, v_w_br_ssm, v_w_out, v_norm_xa, v_norm_mem, v_xa_wq, v_xa_wkv, v_xa_wo, v_norm_mlp, v_mlp_w1, v_mlp_w2, v_norm_final):
    d = dict(zip(_ARG_NAMES, (x, mem, positions, norm_mix, w_in, b_gate, conv_w, conv_b, dt_bias, a_log, d_skip, ssm_norm, w_br_ret, w_br_ssm, w_out, norm_xa, norm_mem, xa_wq, xa_wkv, xa_wo, norm_mlp, mlp_w1, mlp_w2, norm_final, loss_target, m_norm_mix, m_w_in, m_b_gate, m_conv_w, m_conv_b, m_dt_bias, m_a_log, m_d_skip, m_ssm_norm, m_w_br_ret, m_w_br_ssm, m_w_out, m_norm_xa, m_norm_mem, m_xa_wq, m_xa_wkv, m_xa_wo, m_norm_mlp, m_mlp_w1, m_mlp_w2, m_norm_final, v_norm_mix, v_w_in, v_b_gate, v_conv_w, v_conv_b, v_dt_bias, v_a_log, v_d_skip, v_ssm_norm, v_w_br_ret, v_w_br_ssm, v_w_out, v_norm_xa, v_norm_mem, v_xa_wq, v_xa_wkv, v_xa_wo, v_norm_mlp, v_mlp_w1, v_mlp_w2, v_norm_final)))
    blocks = [{k: d[k][l] if k == 'conv_w' else d[k][l].astype(MXU_DTYPE) for k in SHARDED} for l in range(DEPTH)]
    small = {k: d[k] for k in SMALL}
    loss, grad_x, grads, by_chip_l = _step(d['x'][0], d['mem'][0], d['positions'][0], small, blocks, d['loss_target'][0])
    by_chip = [jnp.stack([by_chip_l[l][k] for l in range(DEPTH)], axis=1) for k in SHARDED]
    small_g = [grads[k] if k == 'norm_final' else jnp.stack(grads[k]) for k in SMALL]
    total = _all_reduce_small(_pack_rows([loss] + small_g), "all_reduce_small")
    loss_out = total[0, 0]
    res = {}
    for k, g4 in zip(SHARDED, by_chip):
        res[k] = _adamw(d[k], g4, d['m_' + k], d['v_' + k], f"adamw_{k}")
    small_shapes = [d[k].shape for k in SMALL]
    pk = lambda pre: _pack_rows([d[pre + k] for k in SMALL])
    outs = _adamw(pk('')[None], total[8:][None, None], pk('m_')[None], pk('v_')[None], "adamw_small")
    unpacked = [_unpack_rows(o[0], small_shapes) for o in outs]
    for i, k in enumerate(SMALL):
        res[k] = [unpacked[j][i] for j in range(4)]
    return (loss_out, grad_x[None], *[res[k][0] for k in WEIGHTS], *[res[k][1] for k in WEIGHTS],
            *[res[k][2] for k in WEIGHTS], *[res[k][3] for k in WEIGHTS])
```

```python
import functools

import numpy as np
import jax
import jax.numpy as jnp
from jax import lax
from jax.experimental import pallas as pl
from jax.experimental.pallas import tpu as pltpu

F32 = jnp.float32
MXU_DTYPE = jnp.bfloat16
VMEM_LIMIT_BYTES = 56 * 1024 * 1024
LANES = 128
N_DEV = 8

D_MODEL = 1024
DEPTH = 4
CHUNK = 64
EPS = 1e-6
RET_HEADS, RET_QK_DIM, RET_V_DIM = 4, 128, 256
RET_QK, RET_V = 512, 1024
ROPE_THETA = 10000.0
SSM_INNER, SSM_HEAD_DIM, SSM_HEADS, SSM_GROUPS, SSM_STATE, SSM_CONV = 2048, 64, 32, 8, 128, 4
SSM_BC = 1024
SSM_CONV_DIM = 4096
IN_DIM = 11296
XA_HEADS, XA_HEAD_DIM = 4, 256
D_FF = 4096
ADAM_LR, ADAM_B1, ADAM_B2, ADAM_EPS, ADAM_WD, ADAM_STEP = 0.001, 0.9, 0.999, 1e-08, 0.01, 10

PROJ_W = 11264
COL_Z, COL_XBC, COL_GATES, COL_V, COL_G, COL_Q, COL_K = 0, 2048, 6144, 8192, 9216, 10240, 10752
DT_PAD = 128
N_LTILE = SSM_INNER // LANES

WEIGHTS = ['norm_mix', 'w_in', 'b_gate', 'conv_w', 'conv_b', 'dt_bias', 'a_log', 'd_skip', 'ssm_norm',
           'w_br_ret', 'w_br_ssm', 'w_out', 'norm_xa', 'norm_mem', 'xa_wq', 'xa_wkv', 'xa_wo', 'norm_mlp',
           'mlp_w1', 'mlp_w2', 'norm_final']
COL_SHARDED = ['w_in', 'conv_w', 'xa_wkv', 'mlp_w1']
ROW_SHARDED = ['w_br_ret', 'w_br_ssm', 'w_out', 'xa_wq', 'xa_wo', 'mlp_w2']
SHARDED = COL_SHARDED + ROW_SHARDED
FIRST = ['w_in', 'conv_w']
LATER = [n for n in SHARDED if n not in FIRST]
SMALL = [n for n in WEIGHTS if n not in SHARDED]


def _cparams(sem=None):
    return pltpu.CompilerParams(dimension_semantics=sem, vmem_limit_bytes=VMEM_LIMIT_BYTES)


def _sds(shape, dtype):
    return jax.ShapeDtypeStruct(shape, dtype)


def _full(shape):
    nd = len(shape)
    return pl.BlockSpec(shape, lambda *_: (0,) * nd)


_DIMS = {'nn': (((1,), (0,)), ((), ())), 'nt': (((1,), (1,)), ((), ())), 'tn': (((0,), (0,)), ((), ()))}


def _dot(a, b, mode='nn'):
    return lax.dot_general(a.astype(MXU_DTYPE), b.astype(MXU_DTYPE), _DIMS[mode], preferred_element_type=F32)


@functools.partial(jax.custom_vjp, nondiff_argnums=(2,))
def _mm(a, b, mode):
    return _dot(a, b, mode)


def _mm_fwd(a, b, mode):
    return _dot(a, b, mode), (a, b)


def _mm_bwd(mode, res, g):
    a, b = res
    if mode == 'nn':
        return _dot(g, b, 'nt'), _dot(a, g, 'tn')
    if mode == 'nt':
        return _dot(g, b, 'nn'), _dot(g, a, 'tn')
    return _dot(b, g, 'nt'), _dot(a, g, 'nn')


_mm.defvjp(_mm_fwd, _mm_bwd)


def _split3(x):
    hi = x.astype(jnp.bfloat16)
    r1 = x - hi.astype(F32)
    mid = r1.astype(jnp.bfloat16)
    lo = (r1 - mid.astype(F32)).astype(jnp.bfloat16)
    return hi, mid, lo


def _dot_sel(x, c, left=False):
    dims = _DIMS['nn']
    parts = _split3(x)
    if left:
        outs = [lax.dot_general(c, p, dims, preferred_element_type=F32) for p in parts]
    else:
        outs = [lax.dot_general(p, c, dims, preferred_element_type=F32) for p in parts]
    return (outs[0] + outs[1]) + outs[2]


def _silu(x):
    return x * jax.nn.sigmoid(x)


def _softplus(x):
    pos = x > 0.0
    return jnp.where(pos, x, 0.0) + jnp.log1p(jnp.exp(jnp.where(pos, -x, x)))


def _rms(x):
    return x * lax.rsqrt(jnp.mean(x * x, axis=-1, keepdims=True) + EPS)


def _pick(n, pref):
    t = min(n, pref)
    while n % t:
        t //= 2
    return t


def _with_rider(core, n_in, n_out, n_scratch, rider, grid):
    if rider is None:
        return core
    na, nrs = rider.n, len(rider.scratch)

    def at(step_of):
        cond = pl.program_id(0) == step_of(grid[0])
        for ax in range(1, len(grid)):
            cond = cond & (pl.program_id(ax) == step_of(grid[ax]))
        return cond

    def body(*refs):
        ci, ri = refs[:n_in], refs[n_in:n_in + na]
        co, ro = refs[n_in + na:n_in + na + n_out], refs[n_in + na + n_out:n_in + 2 * na + n_out]
        sc = refs[n_in + 2 * na + n_out:]
        cs, rs = sc[:n_scratch], sc[n_scratch:]
        assert len(rs) == nrs

        @pl.when(at(lambda n: 0))
        def _():
            rider.start(ri, ro, rs)

        core(*ci, *co, *cs)

        @pl.when(at(lambda n: n - 1))
        def _():
            rider.finish(ri, ro, rs)

    return body


def _rider_args(rider):
    if rider is None:
        return [], [], [], [], []
    return list(rider.arrs), [ANY_SPEC] * rider.n, [ANY_SPEC] * rider.n, list(rider.out_shape), list(rider.scratch)


MATMUL_TK_MAX = 4096


def _tiles(mode, m, n, k):
    tm, tn = (512, 1024) if mode == 'nt' else (1024, 512)
    tk = k
    while tk > MATMUL_TK_MAX or k % tk or tk % LANES:
        tk -= LANES
    return _pick(m, tm), _pick(n, tn), tk


def _matmul(a, b, mode, name, *, extras=(), epi=None, out_dtypes=(F32,), out_t=None, tiles=None, rider=None,
            scatter=None):
    if mode == 'nn':
        (m, k), (k2, n) = a.shape, b.shape
    elif mode == 'nt':
        (m, k), (n, k2) = a.shape, b.shape
    else:
        (k, m), (k2, n) = a.shape, b.shape
    assert k == k2, (a.shape, b.shape, mode)
    if scatter == 'rows':
        tiles = (m // N_DEV, n if n <= 1024 else 512, tiles[2] if tiles else _tiles(mode, m, n, k)[2])
    elif scatter == 'cols':
        tiles = (_pick(m, 1024), n // N_DEV, tiles[2] if tiles else _tiles(mode, m, n, k)[2])
    tm, tn, tk = tiles or _tiles(mode, m, n, k)
    nk = k // tk
    n_ex, n_out = len(extras), len(out_dtypes)
    out_t = out_t or (False,) * n_out

    def finish(acc, ex_refs, o_refs):
        outs = epi(acc, *[r[...] for r in ex_refs]) if epi is not None else (acc,)
        for o_ref, o, tr in zip(o_refs, outs, out_t):
            if scatter:
                o_ref[0, 0] = o.astype(o_ref.dtype)
            else:
                o_ref[...] = (o.T if tr else o).astype(o_ref.dtype)

    def body(*refs):
        a_ref, b_ref = refs[0], refs[1]
        ex_refs = refs[2:2 + n_ex]
        o_refs = refs[2 + n_ex:2 + n_ex + n_out]
        if nk == 1:
            finish(_dot(a_ref[...], b_ref[...], mode), ex_refs, o_refs)
            return
        acc_ref = refs[-1]
        kk = pl.program_id(2)

        @pl.when(kk == 0)
        def _():
            acc_ref[...] = jnp.zeros_like(acc_ref)

        acc_ref[...] += _dot(a_ref[...], b_ref[...], mode)

        @pl.when(kk == nk - 1)
        def _():
            finish(acc_ref[...], ex_refs, o_refs)

    if mode == 'nn':
        a_spec = pl.BlockSpec((tm, tk), lambda i, j, kk: (i, kk))
        b_spec = pl.BlockSpec((tk, tn), lambda i, j, kk: (kk, j))
    elif mode == 'nt':
        a_spec = pl.BlockSpec((tm, tk), lambda i, j, kk: (i, kk))
        b_spec = pl.BlockSpec((tn, tk), lambda i, j, kk: (j, kk))
    else:
        a_spec = pl.BlockSpec((tk, tm), lambda i, j, kk: (kk, i))
        b_spec = pl.BlockSpec((tk, tn), lambda i, j, kk: (kk, j))
    mn_spec = pl.BlockSpec((tm, tn), lambda i, j, kk: (i, j))
    nm_spec = pl.BlockSpec((tn, tm), lambda i, j, kk: (j, i))
    grid = (m // tm, n // tn, nk)
    r_arrs, r_in, r_out, r_shape, r_scratch = _rider_args(rider)
    o_specs = [nm_spec if tr else mn_spec for tr in out_t]
    o_shapes = [_sds((n, m) if tr else (m, n), dt) for dt, tr in zip(out_dtypes, out_t)]
    if scatter == 'rows':
        o_specs = [pl.BlockSpec((1, 1, tm, tn), lambda i, j, kk: (i % 2, i // 2, 0, j))]
        o_shapes = [_sds((2, 4, tm, n), out_dtypes[0])]
    elif scatter == 'cols':
        o_specs = [pl.BlockSpec((1, 1, tm, tn), lambda i, j, kk: (j % 2, j // 2, i, 0))]
        o_shapes = [_sds((2, 4, m, tn), out_dtypes[0])]
    outs = pl.pallas_call(
        _with_rider(body, 2 + n_ex, n_out, int(nk > 1), rider, grid), grid=grid,
        in_specs=[a_spec, b_spec] + [mn_spec] * n_ex + r_in,
        out_specs=o_specs + r_out,
        out_shape=o_shapes + r_shape,
        scratch_shapes=([pltpu.VMEM((tm, tn), F32)] if nk > 1 else []) + r_scratch,
        compiler_params=_cparams(("arbitrary",) * 3 if rider is not None else ("parallel", "parallel", "arbitrary")),
        name=name,
    )(a, b, *extras, *r_arrs)
    res = outs[0] if n_out == 1 else outs[:n_out]
    return (res, outs[n_out:]) if rider is not None else res


def _epi_add(acc, r):
    return (acc + r,)


PIECE_TK = 1024


def _matmul_nt_pieces(pieces, b, name, *, extras=(), epi=None, out_dtypes=(F32,)):
    m, n = pieces[0].shape[0], b.shape[0]
    tm, tn, tk = _pick(m, 1024), _pick(n, 1024), PIECE_TK
    steps = [pc.shape[1] // tk for pc in pieces]
    starts = [sum(steps[:i]) for i in range(len(pieces))]
    nk = sum(steps)
    assert b.shape[1] == nk * tk and all(pc.shape[1] % tk == 0 for pc in pieces)
    n_pc, n_ex, n_out = len(pieces), len(extras), len(out_dtypes)

    def body(*refs):
        pc_refs, b_ref = refs[:n_pc], refs[n_pc]
        ex_refs = refs[n_pc + 1:n_pc + 1 + n_ex]
        o_refs = refs[n_pc + 1 + n_ex:n_pc + 1 + n_ex + n_out]
        acc_ref = refs[-1]
        kk = pl.program_id(2)

        @pl.when(kk == 0)
        def _():
            acc_ref[...] = jnp.zeros_like(acc_ref)

        for pc_ref, st, ns in zip(pc_refs, starts, steps):
            @pl.when((kk >= st) & (kk < st + ns))
            def _(pc_ref=pc_ref):
                acc_ref[...] += _dot(pc_ref[...], b_ref[...], 'nt')

        @pl.when(kk == nk - 1)
        def _():
            acc = acc_ref[...]
            outs = epi(acc, *[r[...] for r in ex_refs]) if epi is not None else (acc,)
            for o_ref, o in zip(o_refs, outs):
                o_ref[...] = o.astype(o_ref.dtype)

    pc_specs = [pl.BlockSpec((tm, tk), lambda i, j, kk, st=st, ns=ns: (i, jnp.clip(kk - st, 0, ns - 1)))
                for st, ns in zip(starts, steps)]
    mn_spec = pl.BlockSpec((tm, tn), lambda i, j, kk: (i, j))
    outs = pl.pallas_call(
        body, grid=(m // tm, n // tn, nk),
        in_specs=pc_specs + [pl.BlockSpec((tn, tk), lambda i, j, kk: (j, kk))] + [mn_spec] * n_ex,
        out_specs=[mn_spec] * n_out, out_shape=[_sds((m, n), dt) for dt in out_dtypes],
        scratch_shapes=[pltpu.VMEM((tm, tn), F32)],
        compiler_params=_cparams(("parallel", "parallel", "arbitrary")), name=name,
    )(*pieces, b, *extras)
    return outs[0] if n_out == 1 else outs


def _rmsnorm_fn(x, w):
    return _rms(x) * w


def _rmsnorm(x, w, name):
    s, d = x.shape
    t = _pick(s, 512)

    def body(x_ref, w_ref, o_ref, ot_ref):
        y = _rmsnorm_fn(x_ref[...], w_ref[...])
        o_ref[...] = y.astype(o_ref.dtype)
        ot_ref[...] = y.T.astype(ot_ref.dtype)

    return pl.pallas_call(
        body, grid=(s // t,),
        in_specs=[pl.BlockSpec((t, d), lambda i: (i, 0)), _full((1, d))],
        out_specs=[pl.BlockSpec((t, d), lambda i: (i, 0)), pl.BlockSpec((d, t), lambda i: (0, i))],
        out_shape=[_sds((s, d), MXU_DTYPE), _sds((d, s), MXU_DTYPE)],
        compiler_params=_cparams(("parallel",)), name=name,
    )(x, w.reshape(1, d))


def _rmsnorm_bwd(x, w, du, dres, name):
    s, d = x.shape
    t = _pick(s, 512)
    has_res = dres is not None

    def body(*refs):
        if has_res:
            x_ref, w_ref, du_ref, dres_ref, dx_ref, dxb_ref, dw_ref = refs
        else:
            x_ref, w_ref, du_ref, dx_ref, dxb_ref, dw_ref = refs
        _, vjp = jax.vjp(_rmsnorm_fn, x_ref[...], w_ref[...])
        dx, dw = vjp(du_ref[...])
        dx = dx + dres_ref[...] if has_res else dx
        dx_ref[...] = dx
        dxb_ref[...] = dx.astype(dxb_ref.dtype)

        @pl.when(pl.program_id(0) == 0)
        def _():
            dw_ref[...] = jnp.zeros_like(dw_ref)

        dw_ref[...] += dw

    row = pl.BlockSpec((t, d), lambda i: (i, 0))
    return pl.pallas_call(
        body, grid=(s // t,),
        in_specs=[row, _full((1, d)), row] + ([row] if has_res else []),
        out_specs=[row, row, _full((1, d))],
        out_shape=[_sds((s, d), F32), _sds((s, d), MXU_DTYPE), _sds((1, d), F32)],
        compiler_params=_cparams(("arbitrary",)), name=name,
    )(x, w.reshape(1, d), du, *([dres] if has_res else []))


CONV_CW = 2048
CONV_HALO = 16


def _shifted(cat):
    return [cat] + [pltpu.roll(cat, sft, axis=0) for sft in (1, 2, 3)]


def _conv_taps(shifted, w, n_rows, off):
    acc = shifted[0][off:off + n_rows, :] * w[3:4, :]
    for sft in (1, 2, 3):
        acc = acc + shifted[sft][off:off + n_rows, :] * w[3 - sft:4 - sft, :]
    return acc


def _conv_fwd(proj, conv_w, conv_b, name):
    s = proj.shape[0]
    tr = _pick(s, 256)
    hb = tr // CONV_HALO
    col0 = COL_XBC // CONV_CW

    def body(prev_ref, x_ref, w_ref, b_ref, o_ref):
        i = pl.program_id(1)
        prev = jnp.where(i == 0, 0.0, prev_ref[...].astype(F32))
        cat = jnp.concatenate([prev, x_ref[...].astype(F32)], axis=0)
        o_ref[...] = _silu(_conv_taps(_shifted(cat), w_ref[...], tr, CONV_HALO) + b_ref[...])

    return pl.pallas_call(
        body, grid=(SSM_CONV_DIM // CONV_CW, s // tr),
        in_specs=[pl.BlockSpec((CONV_HALO, CONV_CW), lambda j, i: (jnp.maximum(i * hb - 1, 0), j + col0)),
                  pl.BlockSpec((tr, CONV_CW), lambda j, i: (i, j + col0)),
                  pl.BlockSpec((SSM_CONV, CONV_CW), lambda j, i: (0, j)),
                  pl.BlockSpec((1, CONV_CW), lambda j, i: (0, j))],
        out_specs=pl.BlockSpec((tr, CONV_CW), lambda j, i: (i, j)),
        out_shape=_sds((s, SSM_CONV_DIM), F32),
        compiler_params=_cparams(("parallel", "parallel")), name=name,
    )(proj, proj, conv_w, conv_b.reshape(1, SSM_CONV_DIM))


def _conv_bwd(proj, conv_w, conv_b, dact, name):
    s = proj.shape[0]
    tr = _pick(s, 256)
    hb = tr // CONV_HALO
    nb = s // CONV_HALO
    nt = s // tr
    col0 = COL_XBC // CONV_CW
    h = CONV_HALO

    def body(prev_ref, x_ref, next_ref, w_ref, b_ref, da_ref, dan_ref, dx_ref, dw_ref, db_ref):
        i = pl.program_id(1)
        w = w_ref[...]
        prev = jnp.where(i == 0, 0.0, prev_ref[...].astype(F32))
        cat = jnp.concatenate([prev, x_ref[...].astype(F32), next_ref[...].astype(F32)], axis=0)
        shifted = _shifted(cat)
        pre = _conv_taps(shifted, w, tr + h, h) + b_ref[...]
        dact_n = jnp.where(i == nt - 1, 0.0, dan_ref[...])
        dact_ext = jnp.concatenate([da_ref[...], dact_n], axis=0)
        sg = jax.nn.sigmoid(pre)
        dpre = dact_ext * (sg * (1.0 + pre * (1.0 - sg)))
        dx = dpre[:tr, :] * w[3:4, :]
        for sft in (1, 2, 3):
            dx = dx + pltpu.roll(dpre, tr + h - sft, axis=0)[:tr, :] * w[3 - sft:4 - sft, :]
        dx_ref[...] = dx.astype(dx_ref.dtype)

        @pl.when(i == 0)
        def _():
            dw_ref[...] = jnp.zeros_like(dw_ref)
            db_ref[...] = jnp.zeros_like(db_ref)

        dp = dpre[:tr, :]
        db_ref[...] += jnp.sum(dp, axis=0, keepdims=True)
        for r, sft in enumerate((3, 2, 1, 0)):
            dw_ref[r:r + 1, :] += jnp.sum(dp * shifted[sft][h:h + tr, :], axis=0, keepdims=True)

    return pl.pallas_call(
        body, grid=(SSM_CONV_DIM // CONV_CW, nt),
        in_specs=[pl.BlockSpec((h, CONV_CW), lambda j, i: (jnp.maximum(i * hb - 1, 0), j + col0)),
                  pl.BlockSpec((tr, CONV_CW), lambda j, i: (i, j + col0)),
                  pl.BlockSpec((h, CONV_CW), lambda j, i: (jnp.minimum((i + 1) * hb, nb - 1), j + col0)),
                  pl.BlockSpec((SSM_CONV, CONV_CW), lambda j, i: (0, j)),
                  pl.BlockSpec((1, CONV_CW), lambda j, i: (0, j)),
                  pl.BlockSpec((tr, CONV_CW), lambda j, i: (i, j)),
                  pl.BlockSpec((h, CONV_CW), lambda j, i: (jnp.minimum((i + 1) * hb, nb - 1), j))],
        out_specs=[pl.BlockSpec((tr, CONV_CW), lambda j, i: (i, j)),
                   pl.BlockSpec((SSM_CONV, CONV_CW), lambda j, i: (0, j)),
                   pl.BlockSpec((1, CONV_CW), lambda j, i: (0, j))],
        out_shape=[_sds((s, SSM_CONV_DIM), MXU_DTYPE), _sds((SSM_CONV, SSM_CONV_DIM), F32), _sds((1, SSM_CONV_DIM), F32)],
        compiler_params=_cparams(("parallel", "arbitrary")), name=name,
    )(proj, proj, proj, conv_w, conv_b.reshape(1, SSM_CONV_DIM), dact, dact)


def _scan_tables():
    idx = np.arange(CHUNK, dtype=np.float32)
    lg = np.log1p(-(2.0 ** (-5.0 - np.arange(RET_HEADS, dtype=np.float32)))).astype(np.float32)
    rel = np.abs(idx[:, None] - idx[None, :])
    r_intra = np.exp(lg[:, None, None] * rel).astype(np.float32)
    qd = np.exp(lg[None, :] * (idx[:, None] + 1.0)).astype(np.float32)
    kd = np.exp(lg[None, :] * (CHUNK - 1.0 - idx[:, None])).astype(np.float32)
    gam = [float(v) for v in np.exp(lg * CHUNK).astype(np.float32)]
    qd_e = np.repeat(qd, RET_QK_DIM, axis=1)
    kd_e = np.repeat(kd, RET_QK_DIM, axis=1)
    e = np.zeros((DT_PAD, SSM_INNER), np.float32)
    for hh in range(SSM_HEADS):
        e[hh, hh * SSM_HEAD_DIM:(hh + 1) * SSM_HEAD_DIM] = 1.0
    tri = np.tril(np.ones((CHUNK, CHUNK), np.float32))
    eye2 = np.concatenate([np.eye(CHUNK, dtype=np.float32)] * 2, axis=1)
    bdm = np.kron(np.eye(2, dtype=np.float32), np.ones((CHUNK, CHUNK), np.float32))
    last = np.zeros((CHUNK, LANES), np.float32)
    last[CHUNK - 1, :] = 1.0
    f32c = [jnp.asarray(c) for c in (r_intra, qd_e, kd_e, eye2, bdm, last)]
    sel = [jnp.asarray(c, jnp.bfloat16) for c in (e, e.T.copy(), tri, tri.T.copy())]
    return f32c + sel, gam


def _rope(t, cos2, sin2):
    return t * cos2 + pltpu.roll(t, RET_QK_DIM // 2, axis=1) * sin2


def _rope_t(d, cos2, sin2):
    return d * cos2 + pltpu.roll(d * sin2, RET_QK_DIM // 2, axis=1)


def _ret_step(q, k, v, st, r_intra, qd, kd, gamma):
    k = k * (RET_QK_DIM ** -0.5)
    sc = _mm(q, k, 'nt') * r_intra
    y = _mm(sc, v, 'nn') + _mm(q * qd, st, 'nn')
    st_new = st * gamma + _mm(k * kd, v, 'tn')
    return y, st_new


def _ssd_heads(dtraw, dtb, a_c, tri):
    dt = _softplus(dtraw + dtb)
    return dt, _dot_sel(dt * a_c, tri, left=True)


def _ssd_group(dte0, dte1, cum0, cum1, xs0, xs1, bm, cm, ht0, ht1, eye2, bdm, last):
    cbp = _mm(cm, jnp.concatenate([bm, bm], axis=0), 'nt')
    outs = []
    for dte, cum, xs, ht in ((dte0, cum0, xs0, ht0), (dte1, cum1, xs1, ht1)):
        r = jnp.sum(cum * eye2, axis=0, keepdims=True)
        dlt = cum - r
        seg = jnp.exp(jnp.where(dlt > 0.0, -dlt, dlt))
        xdt = xs * dte
        bd = jnp.concatenate([xdt, xdt], axis=0) * bdm
        clast = jnp.sum(cum * last, axis=0, keepdims=True)
        y = _mm(cbp * seg, bd, 'nn') + jnp.exp(cum) * _mm(cm, ht, 'nn')
        ht_new = jnp.exp(clast) * ht + _mm(bm, xdt * jnp.exp(clast - cum), 'tn')
        outs += [y, ht_new]
    return tuple(outs)


def _scan_in_specs(nc, rev):
    ch = (lambda c: nc - 1 - c) if rev else (lambda c: c)
    col = lambda w, blk: pl.BlockSpec((CHUNK, w), lambda c: (ch(c), blk))
    return [col(RET_QK, COL_Q // RET_QK), col(RET_QK, COL_K // RET_QK), col(RET_V, COL_V // RET_V),
            col(SSM_INNER, 0), col(SSM_BC, 2), col(SSM_BC, 3),
            col(DT_PAD, 0), col(LANES, 0), col(LANES, 0)]


def _const_specs(consts):
    return [_full(c.shape) for c in consts]


def _tile(t):
    return slice(t * LANES, (t + 1) * LANES)


def _scan_fwd(proj, xbc, dtraw, cos2, sin2, a_c, dtb, name, rider=None):
    s = proj.shape[0]
    nc = s // CHUNK
    consts, gam = _scan_tables()
    r_arrs, r_in, r_out, r_shape, r_scratch = _rider_args(rider)

    def body(q_ref, k_ref, v_ref, xs_ref, bm_ref, cm_ref, dt_ref, cos_ref, sin_ref, ac_ref, dtb_ref,
             ri_ref, qd_ref, kd_ref, eye_ref, bdm_ref, last_ref, e_ref, et_ref, tri_ref, trit_ref,
             yr_ref, ys_ref, sh_ref, hh_ref, st_sc, ht_sc):
        @pl.when(pl.program_id(0) == 0)
        def _():
            st_sc[...] = jnp.zeros_like(st_sc)
            ht_sc[...] = jnp.zeros_like(ht_sc)

        sh_ref[0] = st_sc[...]
        hh_ref[0] = ht_sc[...]
        cos2, sin2 = cos_ref[...], sin_ref[...]
        st_new = []
        for h in range(RET_HEADS):
            ql = slice(h * RET_QK_DIM, (h + 1) * RET_QK_DIM)
            vl = slice(h * RET_V_DIM, (h + 1) * RET_V_DIM)
            y, st_h = _ret_step(_rope(q_ref[:, ql].astype(F32), cos2, sin2), _rope(k_ref[:, ql].astype(F32), cos2, sin2),
                                v_ref[:, vl].astype(F32), st_sc[ql, :], ri_ref[h], qd_ref[:, ql], kd_ref[:, ql], gam[h])
            yr_ref[:, vl] = y
            st_new.append(st_h)
        dt, cum_c = _ssd_heads(dt_ref[...], dtb_ref[...], ac_ref[...], tri_ref[...])
        both = jnp.concatenate([dt, cum_c], axis=0)
        eye2, bdm, last = eye_ref[...], bdm_ref[...], last_ref[...]
        ht_new = []
        for g in range(SSM_GROUPS):
            t0, t1 = 2 * g, 2 * g + 1
            e0, e1 = _dot_sel(both, e_ref[:, _tile(t0)]), _dot_sel(both, e_ref[:, _tile(t1)])
            y0, h0, y1, h1 = _ssd_group(e0[:CHUNK], e1[:CHUNK], e0[CHUNK:], e1[CHUNK:],
                                        xs_ref[:, _tile(t0)], xs_ref[:, _tile(t1)], bm_ref[:, _tile(g)],
                                        cm_ref[:, _tile(g)], ht_sc[:, _tile(t0)], ht_sc[:, _tile(t1)], eye2, bdm, last)
            ys_ref[:, _tile(t0)] = y0
            ys_ref[:, _tile(t1)] = y1
            ht_new += [h0, h1]
        for h in range(RET_HEADS):
            st_sc[h * RET_QK_DIM:(h + 1) * RET_QK_DIM, :] = st_new[h]
        for t in range(N_LTILE):
            ht_sc[:, _tile(t)] = ht_new[t]

    in_specs = _scan_in_specs(nc, False) + [_full((1, DT_PAD)), _full((1, DT_PAD))] + _const_specs(consts)
    return pl.pallas_call(
        _with_rider(body, len(in_specs), 4, 2, rider, (nc,)), grid=(nc,),
        in_specs=in_specs + r_in,
        out_specs=[pl.BlockSpec((CHUNK, RET_V), lambda c: (c, 0)),
                   pl.BlockSpec((CHUNK, SSM_INNER), lambda c: (c, 0)),
                   pl.BlockSpec((1, RET_QK, RET_V_DIM), lambda c: (c, 0, 0)),
                   pl.BlockSpec((1, SSM_STATE, SSM_INNER), lambda c: (c, 0, 0))] + r_out,
        out_shape=[_sds((s, RET_V), F32), _sds((s, SSM_INNER), F32),
                   _sds((nc, RET_QK, RET_V_DIM), F32), _sds((nc, SSM_STATE, SSM_INNER), F32)] + r_shape,
        scratch_shapes=[pltpu.VMEM((RET_QK, RET_V_DIM), F32), pltpu.VMEM((SSM_STATE, SSM_INNER), F32)] + r_scratch,
        compiler_params=_cparams(("arbitrary",)), name=name,
    )(proj, proj, proj, xbc, xbc, xbc, dtraw, cos2, sin2, a_c, dtb, *consts, *r_arrs)


def _scan_bwd(proj, xbc, dtraw, cos2, sin2, a_c, dtb, s_hist, h_hist, dyr, dys, dxs_skip, name, rider=None):
    s = proj.shape[0]
    nc = s // CHUNK
    consts, gam = _scan_tables()
    rv = lambda c: nc - 1 - c
    r_arrs, r_in, r_out, r_shape, r_scratch = _rider_args(rider)

    def body(q_ref, k_ref, v_ref, xs_ref, bm_ref, cm_ref, dt_ref, cos_ref, sin_ref, ac_ref, dtb_ref,
             ri_ref, qd_ref, kd_ref, eye_ref, bdm_ref, last_ref, e_ref, et_ref, tri_ref, trit_ref,
             sh_ref, hh_ref, dyr_ref, dys_ref, dsk_ref,
             dqk_ref, dv_ref, dxbc_ref, ddt_ref, dac_ref, ddtb_ref, dst_sc, dht_sc):
        @pl.when(pl.program_id(0) == 0)
        def _():
            dst_sc[...] = jnp.zeros_like(dst_sc)
            dht_sc[...] = jnp.zeros_like(dht_sc)
            dac_ref[...] = jnp.zeros_like(dac_ref)
            ddtb_ref[...] = jnp.zeros_like(ddtb_ref)

        cos2, sin2 = cos_ref[...], sin_ref[...]
        dst_new = []
        for h in range(RET_HEADS):
            ql = slice(h * RET_QK_DIM, (h + 1) * RET_QK_DIM)
            vl = slice(h * RET_V_DIM, (h + 1) * RET_V_DIM)
            step = functools.partial(_ret_step, r_intra=ri_ref[h], qd=qd_ref[:, ql], kd=kd_ref[:, ql], gamma=gam[h])
            _, vjp = jax.vjp(step, _rope(q_ref[:, ql].astype(F32), cos2, sin2), _rope(k_ref[:, ql].astype(F32), cos2, sin2),
                             v_ref[:, vl].astype(F32), sh_ref[0, ql, :])
            dq, dk, dv, dst = vjp((dyr_ref[:, vl], dst_sc[ql, :]))
            dqk_ref[:, ql] = _rope_t(dq, cos2, sin2).astype(dqk_ref.dtype)
            dqk_ref[:, slice(RET_QK + ql.start, RET_QK + ql.stop)] = _rope_t(dk, cos2, sin2).astype(dqk_ref.dtype)
            dv_ref[:, vl] = dv.astype(dv_ref.dtype)
            dst_new.append(dst)
        dtraw_v, dtb_v, a_c, tri = dt_ref[...], dtb_ref[...], ac_ref[...], tri_ref[...]
        dt, cum_c = _ssd_heads(dtraw_v, dtb_v, a_c, tri)
        both = jnp.concatenate([dt, cum_c], axis=0)
        eye2, bdm, last = eye_ref[...], bdm_ref[...], last_ref[...]
        group = functools.partial(_ssd_group, eye2=eye2, bdm=bdm, last=last)
        d_both = jnp.zeros((2 * CHUNK, LANES), F32)
        dht_new = []
        for g in range(SSM_GROUPS):
            t0, t1 = 2 * g, 2 * g + 1
            e0, e1 = _dot_sel(both, e_ref[:, _tile(t0)]), _dot_sel(both, e_ref[:, _tile(t1)])
            _, vjp = jax.vjp(group, e0[:CHUNK], e1[:CHUNK], e0[CHUNK:], e1[CHUNK:],
                             xs_ref[:, _tile(t0)], xs_ref[:, _tile(t1)], bm_ref[:, _tile(g)], cm_ref[:, _tile(g)],
                             hh_ref[0, :, _tile(t0)], hh_ref[0, :, _tile(t1)])
            (d_dte0, d_dte1, d_cum0, d_cum1, d_xs0, d_xs1, d_bm, d_cm, d_ht0, d_ht1) = vjp(
                (dys_ref[:, _tile(t0)], dht_sc[:, _tile(t0)], dys_ref[:, _tile(t1)], dht_sc[:, _tile(t1)]))
            d_both = d_both + _dot_sel(jnp.concatenate([d_dte0, d_cum0], axis=0), et_ref[_tile(t0), :])
            d_both = d_both + _dot_sel(jnp.concatenate([d_dte1, d_cum1], axis=0), et_ref[_tile(t1), :])
            dxbc_ref[:, _tile(t0)] = d_xs0 + dsk_ref[:, _tile(t0)]
            dxbc_ref[:, _tile(t1)] = d_xs1 + dsk_ref[:, _tile(t1)]
            dxbc_ref[:, _tile(N_LTILE + g)] = d_bm
            dxbc_ref[:, _tile(N_LTILE + SSM_GROUPS + g)] = d_cm
            dht_new += [d_ht0, d_ht1]
        d_da = _dot_sel(d_both[CHUNK:], trit_ref[...], left=True)
        d_dt = d_both[:CHUNK] + d_da * a_c
        d_pre = d_dt * jax.nn.sigmoid(dtraw_v + dtb_v)
        ddt_ref[...] = d_pre
        ddtb_ref[...] += jnp.sum(d_pre, axis=0, keepdims=True)
        dac_ref[...] += jnp.sum(d_da * dt, axis=0, keepdims=True)
        for h in range(RET_HEADS):
            dst_sc[h * RET_QK_DIM:(h + 1) * RET_QK_DIM, :] = dst_new[h]
        for t in range(N_LTILE):
            dht_sc[:, _tile(t)] = dht_new[t]

    in_specs = (_scan_in_specs(nc, True) + [_full((1, DT_PAD)), _full((1, DT_PAD))] + _const_specs(consts)
                + [pl.BlockSpec((1, RET_QK, RET_V_DIM), lambda c: (rv(c), 0, 0)),
                   pl.BlockSpec((1, SSM_STATE, SSM_INNER), lambda c: (rv(c), 0, 0)),
                   pl.BlockSpec((CHUNK, RET_V), lambda c: (rv(c), 0)),
                   pl.BlockSpec((CHUNK, SSM_INNER), lambda c: (rv(c), 0)),
                   pl.BlockSpec((CHUNK, SSM_INNER), lambda c: (rv(c), 0))])
    return pl.pallas_call(
        _with_rider(body, len(in_specs), 6, 2, rider, (nc,)), grid=(nc,),
        in_specs=in_specs + r_in,
        out_specs=[pl.BlockSpec((CHUNK, 2 * RET_QK), lambda c: (rv(c), 0)),
                   pl.BlockSpec((CHUNK, RET_V), lambda c: (rv(c), 0)),
                   pl.BlockSpec((CHUNK, SSM_CONV_DIM), lambda c: (rv(c), 0)),
                   pl.BlockSpec((CHUNK, DT_PAD), lambda c: (rv(c), 0)),
                   _full((1, DT_PAD)), _full((1, DT_PAD))] + r_out,
        out_shape=[_sds((s, 2 * RET_QK), MXU_DTYPE), _sds((s, RET_V), MXU_DTYPE),
                   _sds((s, SSM_CONV_DIM), F32), _sds((s, DT_PAD), F32),
                   _sds((1, DT_PAD), F32), _sds((1, DT_PAD), F32)] + r_shape,
        scratch_shapes=[pltpu.VMEM((RET_QK, RET_V_DIM), F32), pltpu.VMEM((SSM_STATE, SSM_INNER), F32)] + r_scratch,
        compiler_params=_cparams(("arbitrary",)), name=name,
    )(proj, proj, proj, xbc, xbc, xbc, dtraw, cos2, sin2, a_c, dtb, *consts, s_hist, h_hist, dyr, dys, dxs_skip, *r_arrs)


POST_W = 256


def _post_ret(y, g):
    return _rms(y) * _silu(g)


def _post_ssm(y, xs, z, dsk, nw):
    return _rms((y + xs * dsk) * _silu(z)) * nw


def _post_specs(t):
    return [pl.BlockSpec((t, RET_V), lambda i: (i, 0)),
            pl.BlockSpec((t, RET_V), lambda i: (i, COL_G // RET_V)),
            pl.BlockSpec((t, SSM_INNER), lambda i: (i, 0)),
            pl.BlockSpec((t, SSM_INNER), lambda i: (i, 0)),
            pl.BlockSpec((t, SSM_INNER), lambda i: (i, COL_Z // SSM_INNER)),
            _full((1, SSM_INNER)), _full((1, SSM_INNER))]


def _post_fwd(y_ret, proj, y_ssm, xbc, dsk_e, ssm_norm, name):
    s = y_ret.shape[0]
    t = _pick(s, 256)

    def body(yr_ref, g_ref, ys_ref, xs_ref, z_ref, dsk_ref, nw_ref, or_ref, os_ref, ort_ref, ost_ref):
        for h in range(RET_V // POST_W):
            sl = slice(h * POST_W, (h + 1) * POST_W)
            o = _post_ret(yr_ref[:, sl], g_ref[:, sl].astype(F32))
            or_ref[:, sl] = o.astype(or_ref.dtype)
            ort_ref[sl, :] = o.T.astype(ort_ref.dtype)
        for g in range(SSM_INNER // POST_W):
            sl = slice(g * POST_W, (g + 1) * POST_W)
            o = _post_ssm(ys_ref[:, sl], xs_ref[:, sl], z_ref[:, sl].astype(F32), dsk_ref[:, sl], nw_ref[:, sl])
            os_ref[:, sl] = o.astype(os_ref.dtype)
            ost_ref[sl, :] = o.T.astype(ost_ref.dtype)

    return pl.pallas_call(
        body, grid=(s // t,), in_specs=_post_specs(t),
        out_specs=[pl.BlockSpec((t, RET_V), lambda i: (i, 0)), pl.BlockSpec((t, SSM_INNER), lambda i: (i, 0)),
                   pl.BlockSpec((RET_V, t), lambda i: (0, i)), pl.BlockSpec((SSM_INNER, t), lambda i: (0, i))],
        out_shape=[_sds((s, RET_V), MXU_DTYPE), _sds((s, SSM_INNER), MXU_DTYPE),
                   _sds((RET_V, s), MXU_DTYPE), _sds((SSM_INNER, s), MXU_DTYPE)],
        compiler_params=_cparams(("parallel",)), name=name,
    )(y_ret, proj, y_ssm, xbc, proj, dsk_e, ssm_norm.reshape(1, SSM_INNER))


def _post_bwd(y_ret, proj, y_ssm, xbc, dsk_e, ssm_norm, d_or, d_os, name, rider=None):
    s = y_ret.shape[0]
    t = _pick(s, 256)
    r_arrs, r_in, r_out, r_shape, r_scratch = _rider_args(rider)

    def body(yr_ref, g_ref, ys_ref, xs_ref, z_ref, dsk_ref, nw_ref, dor_ref, dos_ref,
             dyr_ref, dg_ref, dys_ref, dxs_ref, dz_ref, ddsk_ref, dnw_ref):
        @pl.when(pl.program_id(0) == 0)
        def _():
            ddsk_ref[...] = jnp.zeros_like(ddsk_ref)
            dnw_ref[...] = jnp.zeros_like(dnw_ref)

        for h in range(RET_V // POST_W):
            sl = slice(h * POST_W, (h + 1) * POST_W)
            _, vjp = jax.vjp(_post_ret, yr_ref[:, sl], g_ref[:, sl].astype(F32))
            dyr, dg = vjp(dor_ref[:, sl])
            dyr_ref[:, sl] = dyr
            dg_ref[:, sl] = dg.astype(dg_ref.dtype)
        for g in range(SSM_INNER // POST_W):
            sl = slice(g * POST_W, (g + 1) * POST_W)
            _, vjp = jax.vjp(_post_ssm, ys_ref[:, sl], xs_ref[:, sl], z_ref[:, sl].astype(F32), dsk_ref[:, sl], nw_ref[:, sl])
            dy, dxs, dz, ddsk, dnw = vjp(dos_ref[:, sl])
            dys_ref[:, sl] = dy
            dxs_ref[:, sl] = dxs
            dz_ref[:, sl] = dz.astype(dz_ref.dtype)
            ddsk_ref[:, sl] += ddsk
            dnw_ref[:, sl] += dnw

    rowv = pl.BlockSpec((t, RET_V), lambda i: (i, 0))
    rows = pl.BlockSpec((t, SSM_INNER), lambda i: (i, 0))
    in_specs = _post_specs(t) + [rowv, rows]
    return pl.pallas_call(
        _with_rider(body, len(in_specs), 7, 0, rider, (s // t,)), grid=(s // t,), in_specs=in_specs + r_in,
        out_specs=[rowv, rowv, rows, rows, rows, _full((1, SSM_INNER)), _full((1, SSM_INNER))] + r_out,
        out_shape=[_sds((s, RET_V), F32), _sds((s, RET_V), MXU_DTYPE), _sds((s, SSM_INNER), F32),
                   _sds((s, SSM_INNER), F32), _sds((s, SSM_INNER), MXU_DTYPE),
                   _sds((1, SSM_INNER), F32), _sds((1, SSM_INNER), F32)] + r_shape,
        scratch_shapes=r_scratch,
        compiler_params=_cparams(("arbitrary",)), name=name,
    )(y_ret, proj, y_ssm, xbc, proj, dsk_e, ssm_norm.reshape(1, SSM_INNER), d_or, d_os, *r_arrs)


def _merge_fn(gr, gs, br, bs, yr, ys):
    return jax.nn.sigmoid(gr + br) * yr + jax.nn.sigmoid(gs + bs) * ys


def _merge_specs(t):
    row = pl.BlockSpec((t, D_MODEL), lambda i: (i, 0))
    return [pl.BlockSpec((t, D_MODEL), lambda i: (i, COL_GATES // D_MODEL)),
            pl.BlockSpec((t, D_MODEL), lambda i: (i, COL_GATES // D_MODEL + 1)),
            pl.BlockSpec((1, D_MODEL), lambda i: (0, 0)), pl.BlockSpec((1, D_MODEL), lambda i: (0, 1)), row, row]


def _merge_fwd(proj, b_gate, br_ret, br_ssm, name):
    s = proj.shape[0]
    t = _pick(s, 512)

    def body(gr_ref, gs_ref, br_ref, bs_ref, yr_ref, ys_ref, o_ref, ot_ref):
        o = _merge_fn(gr_ref[...].astype(F32), gs_ref[...].astype(F32), br_ref[...], bs_ref[...], yr_ref[...], ys_ref[...])
        o_ref[...] = o.astype(o_ref.dtype)
        ot_ref[...] = o.T.astype(ot_ref.dtype)

    bg = b_gate.reshape(1, 2 * D_MODEL)
    return pl.pallas_call(
        body, grid=(s // t,), in_specs=_merge_specs(t),
        out_specs=[pl.BlockSpec((t, D_MODEL), lambda i: (i, 0)), pl.BlockSpec((D_MODEL, t), lambda i: (0, i))],
        out_shape=[_sds((s, D_MODEL), MXU_DTYPE), _sds((D_MODEL, s), MXU_DTYPE)],
        compiler_params=_cparams(("parallel",)), name=name,
    )(proj, proj, bg, bg, br_ret, br_ssm)


def _merge_bwd(proj, b_gate, br_ret, br_ssm, dm, name):
    s = proj.shape[0]
    t = _pick(s, 512)

    def body(gr_ref, gs_ref, br_ref, bs_ref, yr_ref, ys_ref, dm_ref, dgt_ref, db_ref, dyr_ref, dys_ref):
        @pl.when(pl.program_id(0) == 0)
        def _():
            db_ref[...] = jnp.zeros_like(db_ref)

        _, vjp = jax.vjp(_merge_fn, gr_ref[...].astype(F32), gs_ref[...].astype(F32), br_ref[...], bs_ref[...],
                         yr_ref[...], ys_ref[...])
        dgr, dgs, dbr, dbs, dyr, dys = vjp(dm_ref[...])
        dgt_ref[:, :D_MODEL] = dgr.astype(dgt_ref.dtype)
        dgt_ref[:, D_MODEL:] = dgs.astype(dgt_ref.dtype)
        db_ref[:, :D_MODEL] += dbr
        db_ref[:, D_MODEL:] += dbs
        dyr_ref[...] = dyr.astype(dyr_ref.dtype)
        dys_ref[...] = dys.astype(dys_ref.dtype)

    bg = b_gate.reshape(1, 2 * D_MODEL)
    row = pl.BlockSpec((t, D_MODEL), lambda i: (i, 0))
    return pl.pallas_call(
        body, grid=(s // t,), in_specs=_merge_specs(t) + [row],
        out_specs=[pl.BlockSpec((t, 2 * D_MODEL), lambda i: (i, 0)), _full((1, 2 * D_MODEL)), row, row],
        out_shape=[_sds((s, 2 * D_MODEL), MXU_DTYPE), _sds((1, 2 * D_MODEL), F32),
                   _sds((s, D_MODEL), MXU_DTYPE), _sds((s, D_MODEL), MXU_DTYPE)],
        compiler_params=_cparams(("arbitrary",)), name=name,
    )(proj, proj, bg, bg, br_ret, br_ssm, dm)


def _attn_head(q, k, v):
    sc = _mm(q, k, 'nt') * (XA_HEAD_DIM ** -0.5)
    e = jnp.exp(sc - lax.stop_gradient(jnp.max(sc, axis=-1, keepdims=True)))
    p = e / jnp.sum(e, axis=-1, keepdims=True)
    return _mm(p, v, 'nn')


def _attn_fwd(q, kv, name):
    s = q.shape[0]
    m = kv.shape[0]
    t = _pick(s, 512)

    def body(q_ref, kv_ref, o_ref, ot_ref):
        for h in range(XA_HEADS):
            sl = slice(h * XA_HEAD_DIM, (h + 1) * XA_HEAD_DIM)
            vl = slice(D_MODEL + h * XA_HEAD_DIM, D_MODEL + (h + 1) * XA_HEAD_DIM)
            o = _attn_head(q_ref[:, sl], kv_ref[:, sl], kv_ref[:, vl])
            o_ref[:, sl] = o.astype(o_ref.dtype)
            ot_ref[sl, :] = o.T.astype(ot_ref.dtype)

    return pl.pallas_call(
        body, grid=(s // t,),
        in_specs=[pl.BlockSpec((t, D_MODEL), lambda i: (i, 0)), _full((m, 2 * D_MODEL))],
        out_specs=[pl.BlockSpec((t, D_MODEL), lambda i: (i, 0)), pl.BlockSpec((D_MODEL, t), lambda i: (0, i))],
        out_shape=[_sds((s, D_MODEL), MXU_DTYPE), _sds((D_MODEL, s), MXU_DTYPE)],
        compiler_params=_cparams(("parallel",)), name=name,
    )(q, kv)


def _attn_bwd(q, kv, d_o, name):
    s = q.shape[0]
    m = kv.shape[0]
    t = _pick(s, 512)

    def body(q_ref, kv_ref, do_ref, dq_ref, dkv_ref):
        @pl.when(pl.program_id(0) == 0)
        def _():
            dkv_ref[...] = jnp.zeros_like(dkv_ref)

        for h in range(XA_HEADS):
            sl = slice(h * XA_HEAD_DIM, (h + 1) * XA_HEAD_DIM)
            vl = slice(D_MODEL + h * XA_HEAD_DIM, D_MODEL + (h + 1) * XA_HEAD_DIM)
            _, vjp = jax.vjp(_attn_head, q_ref[:, sl], kv_ref[:, sl], kv_ref[:, vl])
            dq, dk, dv = vjp(do_ref[:, sl])
            dq_ref[:, sl] = dq.astype(dq_ref.dtype)
            dkv_ref[:, sl] += dk
            dkv_ref[:, vl] += dv

    row = pl.BlockSpec((t, D_MODEL), lambda i: (i, 0))
    return pl.pallas_call(
        body, grid=(s // t,), in_specs=[row, _full((m, 2 * D_MODEL)), row],
        out_specs=[row, _full((m, 2 * D_MODEL))],
        out_shape=[_sds((s, D_MODEL), MXU_DTYPE), _sds((m, 2 * D_MODEL), F32)],
        compiler_params=_cparams(("arbitrary",)), name=name,
    )(q, kv, d_o)


def _loss_head(x, w, target, name):
    s, d = x.shape
    t = _pick(s, 512)

    def body(x_ref, w_ref, t_ref, loss_ref, dx_ref, dxb_ref, dw_ref):
        @pl.when(pl.program_id(0) == 0)
        def _():
            loss_ref[...] = jnp.zeros_like(loss_ref)
            dw_ref[...] = jnp.zeros_like(dw_ref)

        y, vjp = jax.vjp(_rmsnorm_fn, x_ref[...], w_ref[...])
        err = y - t_ref[...]
        loss_ref[...] += 0.5 * jnp.sum(jnp.sum(err * err, axis=-1, keepdims=True), axis=0, keepdims=True) / d
        dx, dw = vjp(err * (1.0 / d))
        dx_ref[...] = dx
        dxb_ref[...] = dx.astype(dxb_ref.dtype)
        dw_ref[...] += dw

    row = pl.BlockSpec((t, d), lambda i: (i, 0))
    return pl.pallas_call(
        body, grid=(s // t,), in_specs=[row, _full((1, d)), row],
        out_specs=[_full((1, LANES)), row, row, _full((1, d))],
        out_shape=[_sds((1, LANES), F32), _sds((s, d), F32), _sds((s, d), MXU_DTYPE), _sds((1, d), F32)],
        compiler_params=_cparams(("arbitrary",)), name=name,
    )(x, w.reshape(1, d), target)


def _epi_sqrelu(acc):
    r = jnp.maximum(acc, 0.0)
    return r * r, r * r


def _epi_sqrelu_bwd(acc, act):
    return (acc * (2.0 * jnp.sqrt(act.astype(F32))),)


def _rope_tables(positions):
    inv_freq = ROPE_THETA ** (-jnp.arange(0, RET_QK_DIM, 2, dtype=F32) / RET_QK_DIM)
    ang = positions.astype(F32)[:, None] * inv_freq
    cos, sin = jnp.cos(ang), jnp.sin(ang)
    return jnp.concatenate([cos, cos], axis=1), jnp.concatenate([-sin, sin], axis=1)


W_IN_ORIG = (('q', 0, 512), ('k', 512, 1024), ('v', 1024, 2048), ('g', 2048, 3072), ('z', 3072, 5120),
             ('xbc', 5120, 9216), ('dt', 9216, 9248), ('gates', 9248, 11296))
W_IN_MAIN_ORDER = ('z', 'xbc', 'gates', 'v', 'g', 'q', 'k')
W_IN_SHARD = IN_DIM // N_DEV


def _shard_segments(lo, hi):
    segs = []
    for j in range(lo // W_IN_SHARD, (hi - 1) // W_IN_SHARD + 1):
        segs.append((j, max(lo, j * W_IN_SHARD) - j * W_IN_SHARD, min(hi, (j + 1) * W_IN_SHARD) - j * W_IN_SHARD))
    return segs


def _w_in_from_shards(g):
    rng = {name: (lo, hi) for name, lo, hi in W_IN_ORIG}
    cols = [g[j][:, a:b] for name in W_IN_MAIN_ORDER for j, a, b in _shard_segments(*rng[name])]
    (j, a, b), = _shard_segments(*rng['dt'])
    return jnp.concatenate(cols, axis=1), jnp.pad(g[j][:, a:b], ((0, 0), (0, DT_PAD - SSM_HEADS)))


def _w_in_grad_blocks(d, d_dt):
    src_of = {'q': ('qk', 0), 'k': ('qk', RET_QK)}
    blocks = []
    for j in range(N_DEV):
        lo_j, hi_j = j * W_IN_SHARD, (j + 1) * W_IN_SHARD
        cols = []
        for name, lo, hi in W_IN_ORIG:
            a, b = max(lo, lo_j), min(hi, hi_j)
            if a >= b:
                continue
            if name == 'dt':
                cols.append(d_dt[:, a - lo:b - lo])
            else:
                key, off = src_of.get(name, (name, 0))
                cols.append(d[key][:, off + a - lo:off + b - lo])
        blocks.append(jnp.concatenate(cols, axis=1))
    return blocks


def _lanes_of_heads(v):
    return jnp.repeat(v, SSM_HEAD_DIM).reshape(1, SSM_INNER)


def _heads_of_lanes(v):
    return v.reshape(SSM_HEADS, SSM_HEAD_DIM).sum(axis=1)


def _layer_fwd(x, mem, cos2, sin2, p, l, later_blocks, rider_proj=None):
    n = lambda s: f"{s}_l{l}"
    sv = {'x0': x}
    u, u_t = _rmsnorm(x, p['norm_mix'], n("norm_mix"))
    s = x.shape[0]
    proj = _matmul(u, p['w_in_main'], 'nn', n("in_proj"), out_dtypes=(MXU_DTYPE,), tiles=(s, 512, D_MODEL), rider=rider_proj)
    if rider_proj is not None:
        proj, next_first = proj
    else:
        next_first = []
    dtraw = _matmul(u, p['w_in_dt'], 'nn', n("in_proj_dt"))
    xbc = _conv_fwd(proj, p['conv_w'], p['conv_b'], n("conv"))
    a_c = jnp.pad(-jnp.exp(p['a_log']), (0, DT_PAD - SSM_HEADS)).reshape(1, DT_PAD)
    dtb = jnp.pad(p['dt_bias'], (0, DT_PAD - SSM_HEADS)).reshape(1, DT_PAD)
    y_ret, y_ssm, s_hist, h_hist, *gathered = _scan_fwd(proj, xbc, dtraw, cos2, sin2, a_c, dtb, n("scan"),
                                                        _AllGather(later_blocks))
    p = {**p, **_full_weights(LATER, gathered)}
    dsk_e = _lanes_of_heads(p['d_skip'])
    o_ret, o_ssm, o_ret_t, o_ssm_t = _post_fwd(y_ret, proj, y_ssm, xbc, dsk_e, p['ssm_norm'], n("post"))
    br_ret = _matmul(o_ret, p['w_br_ret'], 'nn', n("br_ret"))
    br_ssm = _matmul(o_ssm, p['w_br_ssm'], 'nn', n("br_ssm"))
    merged, merged_t = _merge_fwd(proj, p['b_gate'], br_ret, br_ssm, n("merge"))
    x1 = _matmul(merged, p['w_out'], 'nn', n("w_out"), extras=(x,), epi=_epi_add)
    sv.update(u_t=u_t, proj=proj, dtraw=dtraw, xbc=xbc, a_c=a_c, dtb=dtb, y_ret=y_ret, y_ssm=y_ssm, s_hist=s_hist,
              h_hist=h_hist, dsk_e=dsk_e, o_ret_t=o_ret_t, o_ssm_t=o_ssm_t, br_ret=br_ret, br_ssm=br_ssm,
              merged_t=merged_t, x1=x1)
    hq, hq_t = _rmsnorm(x1, p['norm_xa'], n("norm_xa"))
    memn, _ = _rmsnorm(mem, p['norm_mem'], n("norm_mem"))
    q = _matmul(hq, p['xa_wq'], 'nn', n("xa_q"))
    kv = _matmul(memn, p['xa_wkv'], 'nn', n("xa_kv"))
    o, o_t = _attn_fwd(q, kv, n("attn"))
    x2 = _matmul(o, p['xa_wo'], 'nn', n("xa_o"), extras=(x1,), epi=_epi_add)
    sv.update(hq_t=hq_t, memn=memn, q=q, kv=kv, o_t=o_t, x2=x2)
    hm, hm_t = _rmsnorm(x2, p['norm_mlp'], n("norm_mlp"))
    act, act_t = _matmul(hm, p['mlp_w1'], 'nn', n("mlp_1"), epi=_epi_sqrelu, out_dtypes=(MXU_DTYPE, MXU_DTYPE),
                         out_t=(False, True))
    x3 = _matmul(act, p['mlp_w2'], 'nn', n("mlp_2"), extras=(x2,), epi=_epi_add)
    sv.update(hm_t=hm_t, act=act, act_t=act_t)
    return x3, sv, p, next_first


def _layer_bwd(dx, dxb, mem, cos2, sin2, p, sv, l, pending, c_idx):
    n = lambda s: f"{s}_bwd_l{l}"
    gd = (MXU_DTYPE,)
    g = {}
    g['mlp_w2'] = _matmul(sv['act_t'], dxb, 'nn', n("mlp_2_dw"), out_dtypes=gd, scatter='rows')
    da = _matmul(dxb, p['mlp_w2'], 'nt', n("mlp_2_dx"), extras=(sv['act'],), epi=_epi_sqrelu_bwd, out_dtypes=(MXU_DTYPE,),
                 tiles=(_pick(dxb.shape[0], 512), D_FF, D_MODEL))
    g['mlp_w1'] = _matmul(sv['hm_t'], da, 'nn', n("mlp_1_dw"), out_dtypes=gd, scatter='cols')
    dhm = _matmul(da, p['mlp_w1'], 'nt', n("mlp_1_dx"))
    dx2, dx2b, g['norm_mlp'] = _rmsnorm_bwd(sv['x2'], p['norm_mlp'], dhm, dx, n("norm_mlp"))
    g['xa_wo'] = _matmul(sv['o_t'], dx2b, 'nn', n("xa_o_dw"), out_dtypes=gd, scatter='rows')
    d_o = _matmul(dx2b, p['xa_wo'], 'nt', n("xa_o_dx"))
    dq, dkv = _attn_bwd(sv['q'], sv['kv'], d_o, n("attn"))
    g['xa_wq'] = _matmul(sv['hq_t'], dq, 'nn', n("xa_q_dw"), out_dtypes=gd, scatter='rows')
    dhq = _matmul(dq, p['xa_wq'], 'nt', n("xa_q_dx"))
    g['xa_wkv'] = _matmul(sv['memn'], dkv, 'tn', n("xa_kv_dw"), out_dtypes=gd, scatter='cols')
    dmemn = _matmul(dkv, p['xa_wkv'], 'nt', n("xa_kv_dx"))
    _, _, g['norm_mem'] = _rmsnorm_bwd(mem, p['norm_mem'], dmemn, None, n("norm_mem"))
    dx1, dx1b, g['norm_xa'] = _rmsnorm_bwd(sv['x1'], p['norm_xa'], dhq, dx2, n("norm_xa"))
    g['w_out'] = _matmul(sv['merged_t'], dx1b, 'nn', n("w_out_dw"), out_dtypes=gd, scatter='rows')
    dmerged = _matmul(dx1b, p['w_out'], 'nt', n("w_out_dx"))
    dgates, g['b_gate'], dbr_ret, dbr_ssm = _merge_bwd(sv['proj'], p['b_gate'], sv['br_ret'], sv['br_ssm'], dmerged, n("merge"))
    g['w_br_ret'] = _matmul(sv['o_ret_t'], dbr_ret, 'nn', n("br_ret_dw"), out_dtypes=gd, scatter='rows')
    g['w_br_ssm'] = _matmul(sv['o_ssm_t'], dbr_ssm, 'nn', n("br_ssm_dw"), out_dtypes=gd, scatter='rows')
    d_or = _matmul(dbr_ret, p['w_br_ret'], 'nt', n("br_ret_dx"))
    d_os = _matmul(dbr_ssm, p['w_br_ssm'], 'nt', n("br_ssm_dx"))
    later_by_core = [_grad_scatter(k, g[k]) for k in LATER]
    dyr, dg, dys, dxs_skip, dz, ddsk_e, g['ssm_norm'], *later_sib = _post_bwd(
        sv['y_ret'], sv['proj'], sv['y_ssm'], sv['xbc'], sv['dsk_e'], p['ssm_norm'], d_or, d_os, n("post"),
        _ExchangeCores(later_by_core))
    rider = _ExchangeChips(list(pending) + _core_sums(LATER, later_by_core, later_sib, c_idx, l))
    g['d_skip'] = _heads_of_lanes(ddsk_e)
    dqk_r, dv_r, dxbc_act, ddtraw, dac, ddtb, *delivered = _scan_bwd(
        sv['proj'], sv['xbc'], sv['dtraw'], cos2, sin2, sv['a_c'], sv['dtb'], sv['s_hist'], sv['h_hist'],
        dyr, dys, dxs_skip, n("scan"), rider)
    g['a_log'] = dac[0, :SSM_HEADS] * (-jnp.exp(p['a_log']))
    g['dt_bias'] = ddtb[0, :SSM_HEADS]
    dxbc_raw, g['conv_w'], g['conv_b'] = _conv_bwd(sv['proj'], p['conv_w'], p['conv_b'], dxbc_act, n("conv"))
    pieces = {'z': dz, 'xbc': dxbc_raw, 'gates': dgates, 'v': dv_r, 'g': dg, 'qk': dqk_r}
    d_w = {k: _matmul(sv['u_t'], pc, 'nn', n(f"in_proj_dw_{k}"), out_dtypes=gd) for k, pc in pieces.items()}
    d_dt = _matmul(sv['u_t'], ddtraw, 'nn', n("in_proj_dt_dw"), out_dtypes=gd)
    g['w_in'] = _w_in_grad_blocks(d_w, d_dt)
    du_dt = _matmul(ddtraw, p['w_in_dt'], 'nt', n("in_proj_dt_dx"))
    du = _matmul_nt_pieces(list(pieces.values()), p['w_in_main'], n("in_proj_dx"), extras=(du_dt,), epi=_epi_add)
    dx0, dx0b, g['norm_mix'] = _rmsnorm_bwd(sv['x0'], p['norm_mix'], du, dx1, n("norm_mix"))
    first_by_core = [_grad_scatter(k, g[k]) for k in FIRST]
    first_sib = _run_exchange(_ExchangeCores(first_by_core), n("grad_exchange_cores_first"))
    return (dx0, dx0b, g, delivered[:len(pending)], delivered[len(pending):],
            _core_sums(FIRST, first_by_core, first_sib, c_idx, l))


def _full_weights(names, gathered):
    p = {}
    for k, g in zip(names, gathered):
        if k == 'w_in':
            p['w_in_main'], p['w_in_dt'] = _w_in_from_shards(g)
        elif k in COL_SHARDED:
            p[k] = jnp.concatenate([g[j] for j in range(N_DEV)], axis=1)
        else:
            p[k] = g.reshape(-1, g.shape[-1])
    return p


def _grad_scatter(k, g):
    if k in LATER:
        return g
    if k == 'w_in':
        blocks = g
    elif k in COL_SHARDED:
        c = g.shape[1] // N_DEV
        blocks = [g[:, j * c:(j + 1) * c] for j in range(N_DEV)]
    else:
        r = g.shape[0] // N_DEV
        blocks = [g[j * r:(j + 1) * r] for j in range(N_DEV)]
    return jnp.stack([jnp.stack([blocks[2 * chip + core] for chip in range(4)]) for core in range(2)])


def _core_sums(names, by_core, from_sibling, c_idx, l):
    return [_add_halves(a, o, c_idx, f"grad_add_cores_{k}_l{l}") for k, a, o in zip(names, by_core, from_sibling)]


def _step(x, mem, positions, small, blocks, loss_target):
    cos2, sin2 = _rope_tables(positions)
    first = _run_exchange(_AllGather([blocks[0][k] for k in FIRST]), "all_gather_first_l0")
    saved, layers = [], []
    for l in range(DEPTH):
        p = {k: small[k][l] for k in SMALL if k != 'norm_final'}
        p.update(_full_weights(FIRST, first))
        rider = _AllGather([blocks[l + 1][k] for k in FIRST]) if l + 1 < DEPTH else None
        x, sv, p, first = _layer_fwd(x, mem, cos2, sin2, p, l, [blocks[l][k] for k in LATER], rider)
        saved.append(sv)
        layers.append(p)
    loss, dx, dxb, dnf = _loss_head(x, small['norm_final'], loss_target, "loss_head")
    c_idx = lax.axis_index("c").astype(jnp.int32).reshape(1)
    grads, by_chip, pending = [None] * DEPTH, [dict() for _ in range(DEPTH)], []
    for l in reversed(range(DEPTH)):
        dx, dxb, grads[l], got_first, got_later, pending_next = _layer_bwd(
            dx, dxb, mem, cos2, sin2, layers[l], saved[l], l, pending, c_idx)
        if pending:
            by_chip[l + 1].update(zip(FIRST, got_first))
        by_chip[l].update(zip(LATER, got_later))
        pending = pending_next
    by_chip[0].update(zip(FIRST, _run_exchange(_ExchangeChips(pending), "grad_exchange_chips_first_l0")))
    small_g = {}
    for k in SMALL:
        small_g[k] = dnf.reshape(D_MODEL) if k == 'norm_final' else [grads[l][k].reshape(small[k].shape[1:]) for l in range(DEPTH)]
    return loss, dx, small_g, by_chip


MESH = pl.DeviceIdType.MESH
ANY_SPEC = pl.BlockSpec(memory_space=pl.ANY)


def _mesh_pos():
    return lax.axis_index("x"), lax.axis_index("y"), lax.axis_index("c")


def _other_chips(x, y):
    return [(1 - x, y), (x, 1 - y), (1 - x, 1 - y)]


class _AllGather:
    def __init__(self, arrs):
        self.arrs = list(arrs)
        na = self.n = len(self.arrs)
        self.out_shape = [_sds((N_DEV,) + a.shape, a.dtype) for a in self.arrs]
        self.scratch = [pltpu.SemaphoreType.DMA((na, 7)), pltpu.SemaphoreType.DMA((na, 7)), pltpu.SemaphoreType.DMA((na,))]

    def _copies(self, x_refs, o_refs, sems):
        send_sems, recv_sems, local_sems = sems
        x, y, c = _mesh_pos()
        me, sib = (x, y, c), (x, y, 1 - c)
        chips = _other_chips(x, y)

        def copy(a, k, block, to, src=None):
            dst = o_refs[a].at[4 * block[0] + 2 * block[1] + block[2]]
            return pltpu.make_async_remote_copy(src_ref=dst if src is None else src, dst_ref=dst,
                                                send_sem=send_sems.at[a, k], recv_sem=recv_sems.at[a, k],
                                                device_id=to, device_id_type=MESH)

        mine = [pltpu.make_async_copy(x_refs[a], o_refs[a].at[4 * x + 2 * y + c], local_sems.at[a]) for a in range(self.n)]
        first = []
        for a in range(self.n):
            first.append(copy(a, 0, me, sib, src=x_refs[a]))
            first += [copy(a, 1 + j, me, (*chip, c), src=x_refs[a]) for j, chip in enumerate(chips)]
        return copy, mine, first, me, sib, chips, c

    def start(self, x_refs, o_refs, sems):
        _, mine, first, *_ = self._copies(x_refs, o_refs, sems)
        for cp in mine + first:
            cp.start()

    def finish(self, x_refs, o_refs, sems):
        copy, mine, first, me, sib, chips, c = self._copies(x_refs, o_refs, sems)
        passed = []
        for a in range(self.n):
            for j, chip in enumerate(chips):
                copy(a, 1 + j, (*chip, c), me).wait_recv()
                cp = copy(a, 4 + j, (*chip, c), sib)
                cp.start()
                passed.append(cp)
        for a in range(self.n):
            copy(a, 0, sib, me).wait_recv()
            for j, chip in enumerate(chips):
                copy(a, 4 + j, (*chip, 1 - c), me).wait_recv()
        for cp in first + passed:
            cp.wait_send()
        for cp in mine:
            cp.wait()


class _ExchangeChips:
    def __init__(self, arrs):
        self.arrs = list(arrs)
        na = self.n = len(self.arrs)
        self.out_shape = [_sds(a.shape, a.dtype) for a in self.arrs]
        self.scratch = [pltpu.SemaphoreType.DMA((na, 3)), pltpu.SemaphoreType.DMA((na, 3)), pltpu.SemaphoreType.DMA((na,))]

    def _copies(self, a_refs, o_refs, sems):
        send_sems, recv_sems, local_sems = sems
        x, y, c = _mesh_pos()
        my_chip = 2 * x + y
        chips = _other_chips(x, y)
        mine = [pltpu.make_async_copy(a_refs[a].at[my_chip], o_refs[a].at[my_chip], local_sems.at[a]) for a in range(self.n)]
        sends = [pltpu.make_async_remote_copy(src_ref=a_refs[a].at[2 * px + py], dst_ref=o_refs[a].at[my_chip],
                                              send_sem=send_sems.at[a, j], recv_sem=recv_sems.at[a, j],
                                              device_id=(px, py, c), device_id_type=MESH)
                 for a in range(self.n) for j, (px, py) in enumerate(chips)]
        recvs = [pltpu.make_async_remote_copy(src_ref=a_refs[a].at[2 * px + py], dst_ref=o_refs[a].at[2 * px + py],
                                              send_sem=send_sems.at[a, j], recv_sem=recv_sems.at[a, j],
                                              device_id=(px, py, c), device_id_type=MESH)
                 for a in range(self.n) for j, (px, py) in enumerate(chips)]
        return mine, sends, recvs

    def start(self, a_refs, o_refs, sems):
        mine, sends, _ = self._copies(a_refs, o_refs, sems)
        for cp in mine + sends:
            cp.start()

    def finish(self, a_refs, o_refs, sems):
        mine, sends, recvs = self._copies(a_refs, o_refs, sems)
        for cp in recvs:
            cp.wait_recv()
        for cp in sends:
            cp.wait_send()
        for cp in mine:
            cp.wait()


def _run_exchange(ex, name):
    na = ex.n

    def body(*refs):
        i_refs, o_refs, sems = refs[:na], refs[na:2 * na], refs[2 * na:]
        ex.start(i_refs, o_refs, sems)
        ex.finish(i_refs, o_refs, sems)

    return pl.pallas_call(body, in_specs=[ANY_SPEC] * na, out_specs=[ANY_SPEC] * na, out_shape=ex.out_shape,
                          scratch_shapes=ex.scratch, name=name)(*ex.arrs)


class _ExchangeCores:
    def __init__(self, arrs):
        self.arrs = list(arrs)
        na = self.n = len(self.arrs)
        self.out_shape = [_sds(a.shape[1:], a.dtype) for a in self.arrs]
        self.scratch = [pltpu.SemaphoreType.DMA((na,)), pltpu.SemaphoreType.DMA((na,))]

    def _copies(self, a_refs, o_refs, sems):
        send_sems, recv_sems = sems
        x, y, c = _mesh_pos()
        return [pltpu.make_async_remote_copy(src_ref=a_refs[a].at[1 - c], dst_ref=o_refs[a], send_sem=send_sems.at[a],
                                             recv_sem=recv_sems.at[a], device_id=(x, y, 1 - c), device_id_type=MESH)
                for a in range(self.n)]

    def start(self, a_refs, o_refs, sems):
        for cp in self._copies(a_refs, o_refs, sems):
            cp.start()

    def finish(self, a_refs, o_refs, sems):
        for cp in self._copies(a_refs, o_refs, sems):
            cp.wait()


def _as_rows(a, lead):
    return a.reshape(a.shape[:lead] + (-1, a.shape[-1]))


def _add_halves(a, other, c_idx, name):
    a3, o2 = _as_rows(a, 1), _as_rows(other, 0)
    rows, cols = o2.shape
    tr = _pick(rows, 256)

    def body(c_ref, a_ref, o_ref, out_ref):
        out_ref[...] = (a_ref[0].astype(F32) + o_ref[...].astype(F32)).astype(out_ref.dtype)

    out = pl.pallas_call(
        body,
        grid_spec=pltpu.PrefetchScalarGridSpec(
            num_scalar_prefetch=1, grid=(rows // tr,),
            in_specs=[pl.BlockSpec((1, tr, cols), lambda i, c_ref: (c_ref[0], i, 0)),
                      pl.BlockSpec((tr, cols), lambda i, c_ref: (i, 0))],
            out_specs=pl.BlockSpec((tr, cols), lambda i, c_ref: (i, 0))),
        out_shape=_sds((rows, cols), a.dtype), compiler_params=_cparams(("parallel",)), name=name,
    )(c_idx, a3, o2)
    return out.reshape(other.shape)


def _all_reduce_small(v, name):
    r = v.shape[0]

    def body(v_ref, o_ref, slots, send_sems, recv_sems):
        x, y, c = _mesh_pos()
        me = 4 * x + 2 * y + c
        slots[me] = v_ref[...]
        cps = []
        for k in range(1, N_DEV):
            px = 1 - x if k & 4 else x
            py = 1 - y if k & 2 else y
            pc = 1 - c if k & 1 else c
            cps.append(pltpu.make_async_remote_copy(src_ref=v_ref, dst_ref=slots.at[me], send_sem=send_sems.at[k - 1],
                                                    recv_sem=recv_sems.at[k - 1], device_id=(px, py, pc), device_id_type=MESH))
        for cp in cps:
            cp.start()
        for cp in cps:
            cp.wait()
        acc = slots[0]
        for d in range(1, N_DEV):
            acc = acc + slots[d]
        o_ref[...] = acc

    vm = pl.BlockSpec(memory_space=pltpu.VMEM)
    return pl.pallas_call(
        body, in_specs=[vm], out_specs=vm, out_shape=_sds((r, LANES), F32),
        scratch_shapes=[pltpu.VMEM((N_DEV, r, LANES), F32), pltpu.SemaphoreType.DMA((N_DEV - 1,)),
                        pltpu.SemaphoreType.DMA((N_DEV - 1,))],
        compiler_params=pltpu.CompilerParams(vmem_limit_bytes=VMEM_LIMIT_BYTES), name=name,
    )(v)


def _adamw(w, g_slots, m, v, name):
    depth, rows, cols = w.shape
    ns = g_slots.shape[0]
    tr = _pick(rows, 256 if cols <= 1024 else 128)

    def body(w_ref, g_ref, m_ref, v_ref, go_ref, d_ref, mo_ref, vo_ref):
        g = g_ref[0, 0].astype(F32)
        for i in range(1, ns):
            g = g + g_ref[i, 0].astype(F32)
        m_new = ADAM_B1 * m_ref[0] + (1.0 - ADAM_B1) * g
        v_new = ADAM_B2 * v_ref[0] + (1.0 - ADAM_B2) * (g * g)
        m_hat = m_new / (1.0 - ADAM_B1 ** ADAM_STEP)
        v_hat = v_new / (1.0 - ADAM_B2 ** ADAM_STEP)
        go_ref[0] = g
        d_ref[0] = -ADAM_LR * (m_hat / (jnp.sqrt(v_hat) + ADAM_EPS) + ADAM_WD * w_ref[0])
        mo_ref[0] = m_new
        vo_ref[0] = v_new

    blk = pl.BlockSpec((1, tr, cols), lambda l, i: (l, i, 0))
    return pl.pallas_call(
        body, grid=(depth, rows // tr),
        in_specs=[blk, pl.BlockSpec((ns, 1, tr, cols), lambda l, i: (0, l, i, 0)), blk, blk],
        out_specs=[blk] * 4, out_shape=[_sds(w.shape, F32)] * 4,
        compiler_params=_cparams(("parallel", "parallel")), name=name,
    )(w, g_slots, m, v)


_ARG_NAMES = (['x', 'mem', 'positions'] + WEIGHTS + ['loss_target'] + ['m_' + n for n in WEIGHTS]
              + ['v_' + n for n in WEIGHTS])


PACK_TILE = 8 * LANES


def _pack_rows(parts):
    blocks = []
    for part in parts:
        flat = part.reshape(-1)
        pad = (-flat.shape[0]) % PACK_TILE
        blocks.append((jnp.pad(flat, (0, pad)) if pad else flat).reshape(-1, LANES))
    return jnp.concatenate(blocks, axis=0)


def _unpack_rows(packed, shapes):
    out, off = [], 0
    for shp in shapes:
        n = int(np.prod(shp))
        rows = -(-n // PACK_TILE) * 8
        out.append(packed[off:off + rows].reshape(-1)[:n].reshape(shp))
        off += rows
    return out


def kernel(x, mem, positions, norm_mix, w_in, b_gate, conv_w, conv_b, dt_bias, a_log, d_skip, ssm_norm, w_br_ret, w_br_ssm, w_out, norm_xa, norm_mem, xa_wq, xa_wkv, xa_wo, norm_mlp, mlp_w1, mlp_w2, norm_final, loss_target, m_norm_mix, m_w_in, m_b_gate, m_conv_w, m_conv_b, m_dt_bias, m_a_log, m_d_skip, m_ssm_norm, m_w_br_ret, m_w_br_ssm, m_w_out, m_norm_xa, m_norm_mem, m_xa_wq, m_xa_wkv, m_xa_wo, m_norm_mlp, m_mlp_w1, m_mlp_w2, m_norm_final, v_norm_mix, v_w_in, v_b_gate, v_conv_w, v_conv_b, v_dt_bias, v_a_log, v_d_skip, v_ssm_norm, v_w_br_ret, v_w_br_ssm, v_w_out, v_norm_xa, v_norm_mem, v_xa_wq, v_xa_wkv, v_xa_wo, v_norm_mlp, v_mlp_w1, v_mlp_w2, v_norm_final):
    d = dict(zip(_ARG_NAMES, (x, mem, positions, norm_mix, w_in, b_gate, conv_w, conv_b, dt_bias, a_log, d_skip, ssm_norm, w_br_ret, w_br_ssm, w_out, norm_xa, norm_mem, xa_wq, xa_wkv, xa_wo, norm_mlp, mlp_w1, mlp_w2, norm_final, loss_target, m_norm_mix, m_w_in, m_b_gate, m_conv_w, m_conv_b, m_dt_bias, m_a_log, m_d_skip, m_ssm_norm, m_w_br_ret, m_w_br_ssm, m_w_out, m_norm_xa, m_norm_mem, m_xa_wq, m_xa_wkv, m_xa_wo, m_norm_mlp, m_mlp_w1, m_mlp_w2, m_norm_final, v_norm_mix, v_w_in, v_b_gate, v_conv_w, v_conv_b, v_dt_bias, v_a_log, v_d_skip, v_ssm_norm, v_w_br_ret, v_w_br_ssm, v_w_out, v_norm_xa, v_norm_mem, v_xa_wq, v_xa_wkv, v_xa_wo, v_norm_mlp, v_mlp_w1, v_mlp_w2, v_norm_final)))
    blocks = [{k: d[k][l] if k == 'conv_w' else d[k][l].astype(MXU_DTYPE) for k in SHARDED} for l in range(DEPTH)]
    small = {k: d[k] for k in SMALL}
    loss, grad_x, grads, by_chip_l = _step(d['x'][0], d['mem'][0], d['positions'][0], small, blocks, d['loss_target'][0])
    by_chip = [jnp.stack([by_chip_l[l][k] for l in range(DEPTH)], axis=1) for k in SHARDED]
    small_g = [grads[k] if k == 'norm_final' else jnp.stack(grads[k]) for k in SMALL]
    total = _all_reduce_small(_pack_rows([loss] + small_g), "all_reduce_small")
    loss_out = total[0, 0]
    res = {}
    for k, g4 in zip(SHARDED, by_chip):
        res[k] = _adamw(d[k], g4, d['m_' + k], d['v_' + k], f"adamw_{k}")
    small_shapes = [d[k].shape for k in SMALL]
    pk = lambda pre: _pack_rows([d[pre + k] for k in SMALL])
    outs = _adamw(pk('')[None], total[8:][None, None], pk('m_')[None], pk('v_')[None], "adamw_small")
    unpacked = [_unpack_rows(o[0], small_shapes) for o in outs]
    for i, k in enumerate(SMALL):
        res[k] = [unpacked[j][i] for j in range(4)]
    return (loss_out, grad_x[None], *[res[k][0] for k in WEIGHTS], *[res[k][1] for k in WEIGHTS],
            *[res[k][2] for k in WEIGHTS], *[res[k][3] for k in WEIGHTS])
```

```python
import functools

import numpy as np
import jax
import jax.numpy as jnp
from jax import lax
from jax.experimental import pallas as pl
from jax.experimental.pallas import tpu as pltpu

F32 = jnp.float32
MXU_DTYPE = jnp.bfloat16
VMEM_LIMIT_BYTES = 56 * 1024 * 1024
LANES = 128
N_DEV = 8

D_MODEL = 1024
DEPTH = 4
CHUNK = 64
EPS = 1e-6
RET_HEADS, RET_QK_DIM, RET_V_DIM = 4, 128, 256
RET_QK, RET_V = 512, 1024
ROPE_THETA = 10000.0
SSM_INNER, SSM_HEAD_DIM, SSM_HEADS, SSM_GROUPS, SSM_STATE, SSM_CONV = 2048, 64, 32, 8, 128, 4
SSM_BC = 1024
SSM_CONV_DIM = 4096
IN_DIM = 11296
XA_HEADS, XA_HEAD_DIM = 4, 256
D_FF = 4096
ADAM_LR, ADAM_B1, ADAM_B2, ADAM_EPS, ADAM_WD, ADAM_STEP = 0.001, 0.9, 0.999, 1e-08, 0.01, 10

PROJ_W = 11264
COL_Z, COL_XBC, COL_GATES, COL_V, COL_G, COL_Q, COL_K = 0, 2048, 6144, 8192, 9216, 10240, 10752
DT_PAD = 128
N_LTILE = SSM_INNER // LANES

WEIGHTS = ['norm_mix', 'w_in', 'b_gate', 'conv_w', 'conv_b', 'dt_bias', 'a_log', 'd_skip', 'ssm_norm',
           'w_br_ret', 'w_br_ssm', 'w_out', 'norm_xa', 'norm_mem', 'xa_wq', 'xa_wkv', 'xa_wo', 'norm_mlp',
           'mlp_w1', 'mlp_w2', 'norm_final']
COL_SHARDED = ['w_in', 'conv_w', 'xa_wkv', 'mlp_w1']
ROW_SHARDED = ['w_br_ret', 'w_br_ssm', 'w_out', 'xa_wq', 'xa_wo', 'mlp_w2']
SHARDED = COL_SHARDED + ROW_SHARDED
FIRST = ['w_in', 'conv_w']
LATER = [n for n in SHARDED if n not in FIRST]
SMALL = [n for n in WEIGHTS if n not in SHARDED]


def _cparams(sem=None):
    return pltpu.CompilerParams(dimension_semantics=sem, vmem_limit_bytes=VMEM_LIMIT_BYTES)


def _sds(shape, dtype):
    return jax.ShapeDtypeStruct(shape, dtype)


def _full(shape):
    nd = len(shape)
    return pl.BlockSpec(shape, lambda *_: (0,) * nd)


_DIMS = {'nn': (((1,), (0,)), ((), ())), 'nt': (((1,), (1,)), ((), ())), 'tn': (((0,), (0,)), ((), ()))}


def _dot(a, b, mode='nn'):
    return lax.dot_general(a.astype(MXU_DTYPE), b.astype(MXU_DTYPE), _DIMS[mode], preferred_element_type=F32)


@functools.partial(jax.custom_vjp, nondiff_argnums=(2,))
def _mm(a, b, mode):
    return _dot(a, b, mode)


def _mm_fwd(a, b, mode):
    return _dot(a, b, mode), (a, b)


def _mm_bwd(mode, res, g):
    a, b = res
    if mode == 'nn':
        return _dot(g, b, 'nt'), _dot(a, g, 'tn')
    if mode == 'nt':
        return _dot(g, b, 'nn'), _dot(g, a, 'tn')
    return _dot(b, g, 'nt'), _dot(a, g, 'nn')


_mm.defvjp(_mm_fwd, _mm_bwd)


def _split3(x):
    hi = x.astype(jnp.bfloat16)
    r1 = x - hi.astype(F32)
    mid = r1.astype(jnp.bfloat16)
    lo = (r1 - mid.astype(F32)).astype(jnp.bfloat16)
    return hi, mid, lo


def _dot_sel(x, c, left=False):
    dims = _DIMS['nn']
    parts = _split3(x)
    if left:
        outs = [lax.dot_general(c, p, dims, preferred_element_type=F32) for p in parts]
    else:
        outs = [lax.dot_general(p, c, dims, preferred_element_type=F32) for p in parts]
    return (outs[0] + outs[1]) + outs[2]


def _silu(x):
    return x * jax.nn.sigmoid(x)


def _softplus(x):
    pos = x > 0.0
    return jnp.where(pos, x, 0.0) + jnp.log1p(jnp.exp(jnp.where(pos, -x, x)))


def _rms(x):
    return x * lax.rsqrt(jnp.mean(x * x, axis=-1, keepdims=True) + EPS)


def _pick(n, pref):
    t = min(n, pref)
    while n % t:
        t //= 2
    return t


def _with_rider(core, n_in, n_out, n_scratch, rider, grid):
    if rider is None:
        return core
    na, nrs = rider.n, len(rider.scratch)

    def at(step_of):
        cond = pl.program_id(0) == step_of(grid[0])
        for ax in range(1, len(grid)):
            cond = cond & (pl.program_id(ax) == step_of(grid[ax]))
        return cond

    def body(*refs):
        ci, ri = refs[:n_in], refs[n_in:n_in + na]
        co, ro = refs[n_in + na:n_in + na + n_out], refs[n_in + na + n_out:n_in + 2 * na + n_out]
        sc = refs[n_in + 2 * na + n_out:]
        cs, rs = sc[:n_scratch], sc[n_scratch:]
        assert len(rs) == nrs

        @pl.when(at(lambda n: 0))
        def _():
            rider.start(ri, ro, rs)

        core(*ci, *co, *cs)

        @pl.when(at(lambda n: n - 1))
        def _():
            rider.finish(ri, ro, rs)

    return body


def _rider_args(rider):
    if rider is None:
        return [], [], [], [], []
    return list(rider.arrs), [ANY_SPEC] * rider.n, [ANY_SPEC] * rider.n, list(rider.out_shape), list(rider.scratch)


MATMUL_TK_MAX = 4096


def _tiles(mode, m, n, k):
    tm, tn = (512, 1024) if mode == 'nt' else (1024, 512)
    tk = k
    while tk > MATMUL_TK_MAX or k % tk or tk % LANES:
        tk -= LANES
    return _pick(m, tm), _pick(n, tn), tk


def _matmul(a, b, mode, name, *, extras=(), epi=None, out_dtypes=(F32,), out_t=None, tiles=None, rider=None,
            scatter=None):
    if mode == 'nn':
        (m, k), (k2, n) = a.shape, b.shape
    elif mode == 'nt':
        (m, k), (n, k2) = a.shape, b.shape
    else:
        (k, m), (k2, n) = a.shape, b.shape
    assert k == k2, (a.shape, b.shape, mode)
    if scatter == 'rows':
        tiles = (m // N_DEV, n if n <= 1024 else 512, tiles[2] if tiles else _tiles(mode, m, n, k)[2])
    elif scatter == 'cols':
        tiles = (_pick(m, 1024), n // N_DEV, tiles[2] if tiles else _tiles(mode, m, n, k)[2])
    tm, tn, tk = tiles or _tiles(mode, m, n, k)
    nk = k // tk
    n_ex, n_out = len(extras), len(out_dtypes)
    out_t = out_t or (False,) * n_out

    def finish(acc, ex_refs, o_refs):
        outs = epi(acc, *[r[...] for r in ex_refs]) if epi is not None else (acc,)
        for o_ref, o, tr in zip(o_refs, outs, out_t):
            if scatter:
                o_ref[0, 0] = o.astype(o_ref.dtype)
            else:
                o_ref[...] = (o.T if tr else o).astype(o_ref.dtype)

    def body(*refs):
        a_ref, b_ref = refs[0], refs[1]
        ex_refs = refs[2:2 + n_ex]
        o_refs = refs[2 + n_ex:2 + n_ex + n_out]
        if nk == 1:
            finish(_dot(a_ref[...], b_ref[...], mode), ex_refs, o_refs)
            return
        acc_ref = refs[-1]
        kk = pl.program_id(2)

        @pl.when(kk == 0)
        def _():
            acc_ref[...] = jnp.zeros_like(acc_ref)

        acc_ref[...] += _dot(a_ref[...], b_ref[...], mode)

        @pl.when(kk == nk - 1)
        def _():
            finish(acc_ref[...], ex_refs, o_refs)

    if mode == 'nn':
        a_spec = pl.BlockSpec((tm, tk), lambda i, j, kk: (i, kk))
        b_spec = pl.BlockSpec((tk, tn), lambda i, j, kk: (kk, j))
    elif mode == 'nt':
        a_spec = pl.BlockSpec((tm, tk), lambda i, j, kk: (i, kk))
        b_spec = pl.BlockSpec((tn, tk), lambda i, j, kk: (j, kk))
    else:
        a_spec = pl.BlockSpec((tk, tm), lambda i, j, kk: (kk, i))
        b_spec = pl.BlockSpec((tk, tn), lambda i, j, kk: (kk, j))
    mn_spec = pl.BlockSpec((tm, tn), lambda i, j, kk: (i, j))
    nm_spec = pl.BlockSpec((tn, tm), lambda i, j, kk: (j, i))
    grid = (m // tm, n // tn, nk)
    r_arrs, r_in, r_out, r_shape, r_scratch = _rider_args(rider)
    o_specs = [nm_spec if tr else mn_spec for tr in out_t]
    o_shapes = [_sds((n, m) if tr else (m, n), dt) for dt, tr in zip(out_dtypes, out_t)]
    if scatter == 'rows':
        o_specs = [pl.BlockSpec((1, 1, tm, tn), lambda i, j, kk: (i % 2, i // 2, 0, j))]
        o_shapes = [_sds((2, 4, tm, n), out_dtypes[0])]
    elif scatter == 'cols':
        o_specs = [pl.BlockSpec((1, 1, tm, tn), lambda i, j, kk: (j % 2, j // 2, i, 0))]
        o_shapes = [_sds((2, 4, m, tn), out_dtypes[0])]
    outs = pl.pallas_call(
        _with_rider(body, 2 + n_ex, n_out, int(nk > 1), rider, grid), grid=grid,
        in_specs=[a_spec, b_spec] + [mn_spec] * n_ex + r_in,
        out_specs=o_specs + r_out,
        out_shape=o_shapes + r_shape,
        scratch_shapes=([pltpu.VMEM((tm, tn), F32)] if nk > 1 else []) + r_scratch,
        compiler_params=_cparams(("arbitrary",) * 3 if rider is not None else ("parallel", "parallel", "arbitrary")),
        name=name,
    )(a, b, *extras, *r_arrs)
    res = outs[0] if n_out == 1 else outs[:n_out]
    return (res, outs[n_out:]) if rider is not None else res


def _epi_add(acc, r):
    return (acc + r,)


PIECE_TK = 1024


def _matmul_nt_pieces(pieces, b, name, *, extras=(), epi=None, out_dtypes=(F32,), rider=None):
    m, n = pieces[0].shape[0], b.shape[0]
    tm, tn, tk = _pick(m, 1024), _pick(n, 1024), PIECE_TK
    steps = [pc.shape[1] // tk for pc in pieces]
    starts = [sum(steps[:i]) for i in range(len(pieces))]
    nk = sum(steps)
    assert b.shape[1] == nk * tk and all(pc.shape[1] % tk == 0 for pc in pieces)
    n_pc, n_ex, n_out = len(pieces), len(extras), len(out_dtypes)

    def body(*refs):
        pc_refs, b_ref = refs[:n_pc], refs[n_pc]
        ex_refs = refs[n_pc + 1:n_pc + 1 + n_ex]
        o_refs = refs[n_pc + 1 + n_ex:n_pc + 1 + n_ex + n_out]
        acc_ref = refs[-1]
        kk = pl.program_id(2)

        @pl.when(kk == 0)
        def _():
            acc_ref[...] = jnp.zeros_like(acc_ref)

        for pc_ref, st, ns in zip(pc_refs, starts, steps):
            @pl.when((kk >= st) & (kk < st + ns))
            def _(pc_ref=pc_ref):
                acc_ref[...] += _dot(pc_ref[...], b_ref[...], 'nt')

        @pl.when(kk == nk - 1)
        def _():
            acc = acc_ref[...]
            outs = epi(acc, *[r[...] for r in ex_refs]) if epi is not None else (acc,)
            for o_ref, o in zip(o_refs, outs):
                o_ref[...] = o.astype(o_ref.dtype)

    pc_specs = [pl.BlockSpec((tm, tk), lambda i, j, kk, st=st, ns=ns: (i, jnp.clip(kk - st, 0, ns - 1)))
                for st, ns in zip(starts, steps)]
    mn_spec = pl.BlockSpec((tm, tn), lambda i, j, kk: (i, j))
    grid = (m // tm, n // tn, nk)
    r_arrs, r_in, r_out, r_shape, r_scratch = _rider_args(rider)
    outs = pl.pallas_call(
        _with_rider(body, n_pc + 1 + n_ex, n_out, 1, rider, grid), grid=grid,
        in_specs=pc_specs + [pl.BlockSpec((tn, tk), lambda i, j, kk: (j, kk))] + [mn_spec] * n_ex + r_in,
        out_specs=[mn_spec] * n_out + r_out, out_shape=[_sds((m, n), dt) for dt in out_dtypes] + r_shape,
        scratch_shapes=[pltpu.VMEM((tm, tn), F32)] + r_scratch,
        compiler_params=_cparams(("arbitrary",) * 3 if rider is not None else ("parallel", "parallel", "arbitrary")),
        name=name,
    )(*pieces, b, *extras, *r_arrs)
    res = outs[0] if n_out == 1 else outs[:n_out]
    return (res, outs[n_out:]) if rider is not None else res


def _rmsnorm_fn(x, w):
    return _rms(x) * w


def _rmsnorm(x, w, name):
    s, d = x.shape
    t = _pick(s, 512)

    def body(x_ref, w_ref, o_ref, ot_ref):
        y = _rmsnorm_fn(x_ref[...], w_ref[...])
        o_ref[...] = y.astype(o_ref.dtype)
        ot_ref[...] = y.T.astype(ot_ref.dtype)

    return pl.pallas_call(
        body, grid=(s // t,),
        in_specs=[pl.BlockSpec((t, d), lambda i: (i, 0)), _full((1, d))],
        out_specs=[pl.BlockSpec((t, d), lambda i: (i, 0)), pl.BlockSpec((d, t), lambda i: (0, i))],
        out_shape=[_sds((s, d), MXU_DTYPE), _sds((d, s), MXU_DTYPE)],
        compiler_params=_cparams(("parallel",)), name=name,
    )(x, w.reshape(1, d))


def _rmsnorm_bwd(x, w, du, dres, name):
    s, d = x.shape
    t = _pick(s, 512)
    has_res = dres is not None

    def body(*refs):
        if has_res:
            x_ref, w_ref, du_ref, dres_ref, dx_ref, dxb_ref, dw_ref = refs
        else:
            x_ref, w_ref, du_ref, dx_ref, dxb_ref, dw_ref = refs
        _, vjp = jax.vjp(_rmsnorm_fn, x_ref[...], w_ref[...])
        dx, dw = vjp(du_ref[...])
        dx = dx + dres_ref[...] if has_res else dx
        dx_ref[...] = dx
        dxb_ref[...] = dx.astype(dxb_ref.dtype)

        @pl.when(pl.program_id(0) == 0)
        def _():
            dw_ref[...] = jnp.zeros_like(dw_ref)

        dw_ref[...] += dw

    row = pl.BlockSpec((t, d), lambda i: (i, 0))
    return pl.pallas_call(
        body, grid=(s // t,),
        in_specs=[row, _full((1, d)), row] + ([row] if has_res else []),
        out_specs=[row, row, _full((1, d))],
        out_shape=[_sds((s, d), F32), _sds((s, d), MXU_DTYPE), _sds((1, d), F32)],
        compiler_params=_cparams(("arbitrary",)), name=name,
    )(x, w.reshape(1, d), du, *([dres] if has_res else []))


CONV_CW = 2048
CONV_HALO = 16


def _shifted(cat):
    return [cat] + [pltpu.roll(cat, sft, axis=0) for sft in (1, 2, 3)]


def _conv_taps(shifted, w, n_rows, off):
    acc = shifted[0][off:off + n_rows, :] * w[3:4, :]
    for sft in (1, 2, 3):
        acc = acc + shifted[sft][off:off + n_rows, :] * w[3 - sft:4 - sft, :]
    return acc


def _conv_fwd(proj, conv_w, conv_b, name):
    s = proj.shape[0]
    tr = _pick(s, 256)
    hb = tr // CONV_HALO
    col0 = COL_XBC // CONV_CW

    def body(prev_ref, x_ref, w_ref, b_ref, o_ref):
        i = pl.program_id(1)
        prev = jnp.where(i == 0, 0.0, prev_ref[...].astype(F32))
        cat = jnp.concatenate([prev, x_ref[...].astype(F32)], axis=0)
        o_ref[...] = _silu(_conv_taps(_shifted(cat), w_ref[...], tr, CONV_HALO) + b_ref[...])

    return pl.pallas_call(
        body, grid=(SSM_CONV_DIM // CONV_CW, s // tr),
        in_specs=[pl.BlockSpec((CONV_HALO, CONV_CW), lambda j, i: (jnp.maximum(i * hb - 1, 0), j + col0)),
                  pl.BlockSpec((tr, CONV_CW), lambda j, i: (i, j + col0)),
                  pl.BlockSpec((SSM_CONV, CONV_CW), lambda j, i: (0, j)),
                  pl.BlockSpec((1, CONV_CW), lambda j, i: (0, j))],
        out_specs=pl.BlockSpec((tr, CONV_CW), lambda j, i: (i, j)),
        out_shape=_sds((s, SSM_CONV_DIM), F32),
        compiler_params=_cparams(("parallel", "parallel")), name=name,
    )(proj, proj, conv_w, conv_b.reshape(1, SSM_CONV_DIM))


def _conv_bwd(proj, conv_w, conv_b, dact, name):
    s = proj.shape[0]
    tr = _pick(s, 256)
    hb = tr // CONV_HALO
    nb = s // CONV_HALO
    nt = s // tr
    col0 = COL_XBC // CONV_CW
    h = CONV_HALO

    def body(prev_ref, x_ref, next_ref, w_ref, b_ref, da_ref, dan_ref, dx_ref, dw_ref, db_ref):
        i = pl.program_id(1)
        w = w_ref[...]
        prev = jnp.where(i == 0, 0.0, prev_ref[...].astype(F32))
        cat = jnp.concatenate([prev, x_ref[...].astype(F32), next_ref[...].astype(F32)], axis=0)
        shifted = _shifted(cat)
        pre = _conv_taps(shifted, w, tr + h, h) + b_ref[...]
        dact_n = jnp.where(i == nt - 1, 0.0, dan_ref[...])
        dact_ext = jnp.concatenate([da_ref[...], dact_n], axis=0)
        sg = jax.nn.sigmoid(pre)
        dpre = dact_ext * (sg * (1.0 + pre * (1.0 - sg)))
        dx = dpre[:tr, :] * w[3:4, :]
        for sft in (1, 2, 3):
            dx = dx + pltpu.roll(dpre, tr + h - sft, axis=0)[:tr, :] * w[3 - sft:4 - sft, :]
        dx_ref[...] = dx.astype(dx_ref.dtype)

        @pl.when(i == 0)
        def _():
            dw_ref[...] = jnp.zeros_like(dw_ref)
            db_ref[...] = jnp.zeros_like(db_ref)

        dp = dpre[:tr, :]
        db_ref[...] += jnp.sum(dp, axis=0, keepdims=True)
        for r, sft in enumerate((3, 2, 1, 0)):
            dw_ref[r:r + 1, :] += jnp.sum(dp * shifted[sft][h:h + tr, :], axis=0, keepdims=True)

    return pl.pallas_call(
        body, grid=(SSM_CONV_DIM // CONV_CW, nt),
        in_specs=[pl.BlockSpec((h, CONV_CW), lambda j, i: (jnp.maximum(i * hb - 1, 0), j + col0)),
                  pl.BlockSpec((tr, CONV_CW), lambda j, i: (i, j + col0)),
                  pl.BlockSpec((h, CONV_CW), lambda j, i: (jnp.minimum((i + 1) * hb, nb - 1), j + col0)),
                  pl.BlockSpec((SSM_CONV, CONV_CW), lambda j, i: (0, j)),
                  pl.BlockSpec((1, CONV_CW), lambda j, i: (0, j)),
                  pl.BlockSpec((tr, CONV_CW), lambda j, i: (i, j)),
                  pl.BlockSpec((h, CONV_CW), lambda j, i: (jnp.minimum((i + 1) * hb, nb - 1), j))],
        out_specs=[pl.BlockSpec((tr, CONV_CW), lambda j, i: (i, j)),
                   pl.BlockSpec((SSM_CONV, CONV_CW), lambda j, i: (0, j)),
                   pl.BlockSpec((1, CONV_CW), lambda j, i: (0, j))],
        out_shape=[_sds((s, SSM_CONV_DIM), MXU_DTYPE), _sds((SSM_CONV, SSM_CONV_DIM), F32), _sds((1, SSM_CONV_DIM), F32)],
        compiler_params=_cparams(("parallel", "arbitrary")), name=name,
    )(proj, proj, proj, conv_w, conv_b.reshape(1, SSM_CONV_DIM), dact, dact)


def _scan_tables():
    idx = np.arange(CHUNK, dtype=np.float32)
    lg = np.log1p(-(2.0 ** (-5.0 - np.arange(RET_HEADS, dtype=np.float32)))).astype(np.float32)
    rel = np.abs(idx[:, None] - idx[None, :])
    r_intra = np.exp(lg[:, None, None] * rel).astype(np.float32)
    qd = np.exp(lg[None, :] * (idx[:, None] + 1.0)).astype(np.float32)
    kd = np.exp(lg[None, :] * (CHUNK - 1.0 - idx[:, None])).astype(np.float32)
    gam = [float(v) for v in np.exp(lg * CHUNK).astype(np.float32)]
    qd_e = np.repeat(qd, RET_QK_DIM, axis=1)
    kd_e = np.repeat(kd, RET_QK_DIM, axis=1)
    e = np.zeros((DT_PAD, SSM_INNER), np.float32)
    for hh in range(SSM_HEADS):
        e[hh, hh * SSM_HEAD_DIM:(hh + 1) * SSM_HEAD_DIM] = 1.0
    tri = np.tril(np.ones((CHUNK, CHUNK), np.float32))
    eye2 = np.concatenate([np.eye(CHUNK, dtype=np.float32)] * 2, axis=1)
    bdm = np.kron(np.eye(2, dtype=np.float32), np.ones((CHUNK, CHUNK), np.float32))
    last = np.zeros((CHUNK, LANES), np.float32)
    last[CHUNK - 1, :] = 1.0
    f32c = [jnp.asarray(c) for c in (r_intra, qd_e, kd_e, eye2, bdm, last)]
    sel = [jnp.asarray(c, jnp.bfloat16) for c in (e, e.T.copy(), tri, tri.T.copy())]
    return f32c + sel, gam


def _rope(t, cos2, sin2):
    return t * cos2 + pltpu.roll(t, RET_QK_DIM // 2, axis=1) * sin2


def _rope_t(d, cos2, sin2):
    return d * cos2 + pltpu.roll(d * sin2, RET_QK_DIM // 2, axis=1)


def _ret_step(q, k, v, st, r_intra, qd, kd, gamma):
    k = k * (RET_QK_DIM ** -0.5)
    sc = _mm(q, k, 'nt') * r_intra
    y = _mm(sc, v, 'nn') + _mm(q * qd, st, 'nn')
    st_new = st * gamma + _mm(k * kd, v, 'tn')
    return y, st_new


def _ssd_heads(dtraw, dtb, a_c, tri):
    dt = _softplus(dtraw + dtb)
    return dt, _dot_sel(dt * a_c, tri, left=True)


def _ssd_group(dte0, dte1, cum0, cum1, xs0, xs1, bm, cm, ht0, ht1, eye2, bdm, last):
    cbp = _mm(cm, jnp.concatenate([bm, bm], axis=0), 'nt')
    outs = []
    for dte, cum, xs, ht in ((dte0, cum0, xs0, ht0), (dte1, cum1, xs1, ht1)):
        r = jnp.sum(cum * eye2, axis=0, keepdims=True)
        dlt = cum - r
        seg = jnp.exp(jnp.where(dlt > 0.0, -dlt, dlt))
        xdt = xs * dte
        bd = jnp.concatenate([xdt, xdt], axis=0) * bdm
        clast = jnp.sum(cum * last, axis=0, keepdims=True)
        y = _mm(cbp * seg, bd, 'nn') + jnp.exp(cum) * _mm(cm, ht, 'nn')
        ht_new = jnp.exp(clast) * ht + _mm(bm, xdt * jnp.exp(clast - cum), 'tn')
        outs += [y, ht_new]
    return tuple(outs)


SCAN_SUB = 2


def _scan_in_specs(nb, rev, rows):
    ch = (lambda c: nb - 1 - c) if rev else (lambda c: c)
    col = lambda w, blk: pl.BlockSpec((rows, w), lambda c: (ch(c), blk))
    return [col(RET_QK, COL_Q // RET_QK), col(RET_QK, COL_K // RET_QK), col(RET_V, COL_V // RET_V),
            col(SSM_INNER, 0), col(SSM_BC, 2), col(SSM_BC, 3),
            col(DT_PAD, 0), col(LANES, 0), col(LANES, 0)]


def _const_specs(consts):
    return [_full(c.shape) for c in consts]


def _tile(t):
    return slice(t * LANES, (t + 1) * LANES)


def _scan_sub(s):
    return SCAN_SUB if (s // CHUNK) % SCAN_SUB == 0 else 1


def _scan_fwd(proj, xbc, dtraw, cos2, sin2, a_c, dtb, name, rider=None):
    s = proj.shape[0]
    nc = s // CHUNK
    sub = _scan_sub(s)
    nb, rows = nc // sub, CHUNK * sub
    consts, gam = _scan_tables()
    r_arrs, r_in, r_out, r_shape, r_scratch = _rider_args(rider)

    def body(q_ref, k_ref, v_ref, xs_ref, bm_ref, cm_ref, dt_ref, cos_ref, sin_ref, ac_ref, dtb_ref,
             ri_ref, qd_ref, kd_ref, eye_ref, bdm_ref, last_ref, e_ref, et_ref, tri_ref, trit_ref,
             yr_ref, ys_ref, sh_ref, hh_ref, st_sc, ht_sc):
        @pl.when(pl.program_id(0) == 0)
        def _():
            st_sc[...] = jnp.zeros_like(st_sc)
            ht_sc[...] = jnp.zeros_like(ht_sc)

        eye2, bdm, last = eye_ref[...], bdm_ref[...], last_ref[...]
        st = [st_sc[h * RET_QK_DIM:(h + 1) * RET_QK_DIM, :] for h in range(RET_HEADS)]
        ht = [ht_sc[:, _tile(t)] for t in range(N_LTILE)]
        for i in range(sub):
            r = slice(i * CHUNK, (i + 1) * CHUNK)
            cos2, sin2 = cos_ref[r, :], sin_ref[r, :]
            for h in range(RET_HEADS):
                ql = slice(h * RET_QK_DIM, (h + 1) * RET_QK_DIM)
                vl = slice(h * RET_V_DIM, (h + 1) * RET_V_DIM)
                sh_ref[i, ql, :] = st[h]
                y, st[h] = _ret_step(_rope(q_ref[r, ql].astype(F32), cos2, sin2), _rope(k_ref[r, ql].astype(F32), cos2, sin2),
                                     v_ref[r, vl].astype(F32), st[h], ri_ref[h], qd_ref[:, ql], kd_ref[:, ql], gam[h])
                yr_ref[r, vl] = y
            dt, cum_c = _ssd_heads(dt_ref[r, :], dtb_ref[...], ac_ref[...], tri_ref[...])
            both = jnp.concatenate([dt, cum_c], axis=0)
            for g in range(SSM_GROUPS):
                t0, t1 = 2 * g, 2 * g + 1
                hh_ref[i, :, _tile(t0)] = ht[t0]
                hh_ref[i, :, _tile(t1)] = ht[t1]
                e0, e1 = _dot_sel(both, e_ref[:, _tile(t0)]), _dot_sel(both, e_ref[:, _tile(t1)])
                y0, ht[t0], y1, ht[t1] = _ssd_group(e0[:CHUNK], e1[:CHUNK], e0[CHUNK:], e1[CHUNK:],
                                                    xs_ref[r, _tile(t0)], xs_ref[r, _tile(t1)], bm_ref[r, _tile(g)],
                                                    cm_ref[r, _tile(g)], ht[t0], ht[t1], eye2, bdm, last)
                ys_ref[r, _tile(t0)] = y0
                ys_ref[r, _tile(t1)] = y1
        for h in range(RET_HEADS):
            st_sc[h * RET_QK_DIM:(h + 1) * RET_QK_DIM, :] = st[h]
        for t in range(N_LTILE):
            ht_sc[:, _tile(t)] = ht[t]

    in_specs = _scan_in_specs(nb, False, rows) + [_full((1, DT_PAD)), _full((1, DT_PAD))] + _const_specs(consts)
    return pl.pallas_call(
        _with_rider(body, len(in_specs), 4, 2, rider, (nb,)), grid=(nb,),
        in_specs=in_specs + r_in,
        out_specs=[pl.BlockSpec((rows, RET_V), lambda c: (c, 0)),
                   pl.BlockSpec((rows, SSM_INNER), lambda c: (c, 0)),
                   pl.BlockSpec((sub, RET_QK, RET_V_DIM), lambda c: (c, 0, 0)),
                   pl.BlockSpec((sub, SSM_STATE, SSM_INNER), lambda c: (c, 0, 0))] + r_out,
        out_shape=[_sds((s, RET_V), F32), _sds((s, SSM_INNER), F32),
                   _sds((nc, RET_QK, RET_V_DIM), F32), _sds((nc, SSM_STATE, SSM_INNER), F32)] + r_shape,
        scratch_shapes=[pltpu.VMEM((RET_QK, RET_V_DIM), F32), pltpu.VMEM((SSM_STATE, SSM_INNER), F32)] + r_scratch,
        compiler_params=_cparams(("arbitrary",)), name=name,
    )(proj, proj, proj, xbc, xbc, xbc, dtraw, cos2, sin2, a_c, dtb, *consts, *r_arrs)


def _scan_bwd(proj, xbc, dtraw, cos2, sin2, a_c, dtb, s_hist, h_hist, dyr, dys, dxs_skip, name, rider=None):
    s = proj.shape[0]
    nc = s // CHUNK
    sub = _scan_sub(s)
    nb, rows = nc // sub, CHUNK * sub
    consts, gam = _scan_tables()
    rv = lambda c: nb - 1 - c
    r_arrs, r_in, r_out, r_shape, r_scratch = _rider_args(rider)

    def body(q_ref, k_ref, v_ref, xs_ref, bm_ref, cm_ref, dt_ref, cos_ref, sin_ref, ac_ref, dtb_ref,
             ri_ref, qd_ref, kd_ref, eye_ref, bdm_ref, last_ref, e_ref, et_ref, tri_ref, trit_ref,
             sh_ref, hh_ref, dyr_ref, dys_ref, dsk_ref,
             dqk_ref, dv_ref, dxbc_ref, ddt_ref, dac_ref, ddtb_ref, dst_sc, dht_sc):
        @pl.when(pl.program_id(0) == 0)
        def _():
            dst_sc[...] = jnp.zeros_like(dst_sc)
            dht_sc[...] = jnp.zeros_like(dht_sc)
            dac_ref[...] = jnp.zeros_like(dac_ref)
            ddtb_ref[...] = jnp.zeros_like(ddtb_ref)

        dtb_v, a_c, tri = dtb_ref[...], ac_ref[...], tri_ref[...]
        eye2, bdm, last = eye_ref[...], bdm_ref[...], last_ref[...]
        group = functools.partial(_ssd_group, eye2=eye2, bdm=bdm, last=last)
        dst = [dst_sc[h * RET_QK_DIM:(h + 1) * RET_QK_DIM, :] for h in range(RET_HEADS)]
        dht = [dht_sc[:, _tile(t)] for t in range(N_LTILE)]
        ddtb = jnp.zeros((1, DT_PAD), F32)
        dac = jnp.zeros((1, DT_PAD), F32)
        for i in reversed(range(sub)):
            r = slice(i * CHUNK, (i + 1) * CHUNK)
            cos2, sin2 = cos_ref[r, :], sin_ref[r, :]
            for h in range(RET_HEADS):
                ql = slice(h * RET_QK_DIM, (h + 1) * RET_QK_DIM)
                vl = slice(h * RET_V_DIM, (h + 1) * RET_V_DIM)
                step = functools.partial(_ret_step, r_intra=ri_ref[h], qd=qd_ref[:, ql], kd=kd_ref[:, ql], gamma=gam[h])
                _, vjp = jax.vjp(step, _rope(q_ref[r, ql].astype(F32), cos2, sin2), _rope(k_ref[r, ql].astype(F32), cos2, sin2),
                                 v_ref[r, vl].astype(F32), sh_ref[i, ql, :])
                dq, dk, dv, dst[h] = vjp((dyr_ref[r, vl], dst[h]))
                dqk_ref[r, ql] = _rope_t(dq, cos2, sin2).astype(dqk_ref.dtype)
                dqk_ref[r, slice(RET_QK + ql.start, RET_QK + ql.stop)] = _rope_t(dk, cos2, sin2).astype(dqk_ref.dtype)
                dv_ref[r, vl] = dv.astype(dv_ref.dtype)
            dtraw_v = dt_ref[r, :]
            dt, cum_c = _ssd_heads(dtraw_v, dtb_v, a_c, tri)
            both = jnp.concatenate([dt, cum_c], axis=0)
            d_both = jnp.zeros((2 * CHUNK, LANES), F32)
            for g in range(SSM_GROUPS):
                t0, t1 = 2 * g, 2 * g + 1
                e0, e1 = _dot_sel(both, e_ref[:, _tile(t0)]), _dot_sel(both, e_ref[:, _tile(t1)])
                _, vjp = jax.vjp(group, e0[:CHUNK], e1[:CHUNK], e0[CHUNK:], e1[CHUNK:],
                                 xs_ref[r, _tile(t0)], xs_ref[r, _tile(t1)], bm_ref[r, _tile(g)], cm_ref[r, _tile(g)],
                                 hh_ref[i, :, _tile(t0)], hh_ref[i, :, _tile(t1)])
                (d_dte0, d_dte1, d_cum0, d_cum1, d_xs0, d_xs1, d_bm, d_cm, dht[t0], dht[t1]) = vjp(
                    (dys_ref[r, _tile(t0)], dht[t0], dys_ref[r, _tile(t1)], dht[t1]))
                d_both = d_both + _dot_sel(jnp.concatenate([d_dte0, d_cum0], axis=0), et_ref[_tile(t0), :])
                d_both = d_both + _dot_sel(jnp.concatenate([d_dte1, d_cum1], axis=0), et_ref[_tile(t1), :])
                dxbc_ref[r, _tile(t0)] = d_xs0 + dsk_ref[r, _tile(t0)]
                dxbc_ref[r, _tile(t1)] = d_xs1 + dsk_ref[r, _tile(t1)]
                dxbc_ref[r, _tile(N_LTILE + g)] = d_bm
                dxbc_ref[r, _tile(N_LTILE + SSM_GROUPS + g)] = d_cm
            d_da = _dot_sel(d_both[CHUNK:], trit_ref[...], left=True)
            d_dt = d_both[:CHUNK] + d_da * a_c
            d_pre = d_dt * jax.nn.sigmoid(dtraw_v + dtb_v)
            ddt_ref[r, :] = d_pre
            ddtb = ddtb + jnp.sum(d_pre, axis=0, keepdims=True)
            dac = dac + jnp.sum(d_da * dt, axis=0, keepdims=True)
        ddtb_ref[...] += ddtb
        dac_ref[...] += dac
        for h in range(RET_HEADS):
            dst_sc[h * RET_QK_DIM:(h + 1) * RET_QK_DIM, :] = dst[h]
        for t in range(N_LTILE):
            dht_sc[:, _tile(t)] = dht[t]

    in_specs = (_scan_in_specs(nb, True, rows) + [_full((1, DT_PAD)), _full((1, DT_PAD))] + _const_specs(consts)
                + [pl.BlockSpec((sub, RET_QK, RET_V_DIM), lambda c: (rv(c), 0, 0)),
                   pl.BlockSpec((sub, SSM_STATE, SSM_INNER), lambda c: (rv(c), 0, 0)),
                   pl.BlockSpec((rows, RET_V), lambda c: (rv(c), 0)),
                   pl.BlockSpec((rows, SSM_INNER), lambda c: (rv(c), 0)),
                   pl.BlockSpec((rows, SSM_INNER), lambda c: (rv(c), 0))])
    return pl.pallas_call(
        _with_rider(body, len(in_specs), 6, 2, rider, (nb,)), grid=(nb,),
        in_specs=in_specs + r_in,
        out_specs=[pl.BlockSpec((rows, 2 * RET_QK), lambda c: (rv(c), 0)),
                   pl.BlockSpec((rows, RET_V), lambda c: (rv(c), 0)),
                   pl.BlockSpec((rows, SSM_CONV_DIM), lambda c: (rv(c), 0)),
                   pl.BlockSpec((rows, DT_PAD), lambda c: (rv(c), 0)),
                   _full((1, DT_PAD)), _full((1, DT_PAD))] + r_out,
        out_shape=[_sds((s, 2 * RET_QK), MXU_DTYPE), _sds((s, RET_V), MXU_DTYPE),
                   _sds((s, SSM_CONV_DIM), F32), _sds((s, DT_PAD), F32),
                   _sds((1, DT_PAD), F32), _sds((1, DT_PAD), F32)] + r_shape,
        scratch_shapes=[pltpu.VMEM((RET_QK, RET_V_DIM), F32), pltpu.VMEM((SSM_STATE, SSM_INNER), F32)] + r_scratch,
        compiler_params=_cparams(("arbitrary",)), name=name,
    )(proj, proj, proj, xbc, xbc, xbc, dtraw, cos2, sin2, a_c, dtb, *consts, s_hist, h_hist, dyr, dys, dxs_skip, *r_arrs)


POST_W = 256


def _post_ret(y, g):
    return _rms(y) * _silu(g)


def _post_ssm(y, xs, z, dsk, nw):
    return _rms((y + xs * dsk) * _silu(z)) * nw


def _post_specs(t):
    return [pl.BlockSpec((t, RET_V), lambda i: (i, 0)),
            pl.BlockSpec((t, RET_V), lambda i: (i, COL_G // RET_V)),
            pl.BlockSpec((t, SSM_INNER), lambda i: (i, 0)),
            pl.BlockSpec((t, SSM_INNER), lambda i: (i, 0)),
            pl.BlockSpec((t, SSM_INNER), lambda i: (i, COL_Z // SSM_INNER)),
            _full((1, SSM_INNER)), _full((1, SSM_INNER))]


def _post_fwd(y_ret, proj, y_ssm, xbc, dsk_e, ssm_norm, name):
    s = y_ret.shape[0]
    t = _pick(s, 256)

    def body(yr_ref, g_ref, ys_ref, xs_ref, z_ref, dsk_ref, nw_ref, or_ref, os_ref, ort_ref, ost_ref):
        for h in range(RET_V // POST_W):
            sl = slice(h * POST_W, (h + 1) * POST_W)
            o = _post_ret(yr_ref[:, sl], g_ref[:, sl].astype(F32))
            or_ref[:, sl] = o.astype(or_ref.dtype)
            ort_ref[sl, :] = o.T.astype(ort_ref.dtype)
        for g in range(SSM_INNER // POST_W):
            sl = slice(g * POST_W, (g + 1) * POST_W)
            o = _post_ssm(ys_ref[:, sl], xs_ref[:, sl], z_ref[:, sl].astype(F32), dsk_ref[:, sl], nw_ref[:, sl])
            os_ref[:, sl] = o.astype(os_ref.dtype)
            ost_ref[sl, :] = o.T.astype(ost_ref.dtype)

    return pl.pallas_call(
        body, grid=(s // t,), in_specs=_post_specs(t),
        out_specs=[pl.BlockSpec((t, RET_V), lambda i: (i, 0)), pl.BlockSpec((t, SSM_INNER), lambda i: (i, 0)),
                   pl.BlockSpec((RET_V, t), lambda i: (0, i)), pl.BlockSpec((SSM_INNER, t), lambda i: (0, i))],
        out_shape=[_sds((s, RET_V), MXU_DTYPE), _sds((s, SSM_INNER), MXU_DTYPE),
                   _sds((RET_V, s), MXU_DTYPE), _sds((SSM_INNER, s), MXU_DTYPE)],
        compiler_params=_cparams(("parallel",)), name=name,
    )(y_ret, proj, y_ssm, xbc, proj, dsk_e, ssm_norm.reshape(1, SSM_INNER))


def _post_bwd(y_ret, proj, y_ssm, xbc, dsk_e, ssm_norm, d_or, d_os, name, rider=None):
    s = y_ret.shape[0]
    t = _pick(s, 256)
    r_arrs, r_in, r_out, r_shape, r_scratch = _rider_args(rider)

    def body(yr_ref, g_ref, ys_ref, xs_ref, z_ref, dsk_ref, nw_ref, dor_ref, dos_ref,
             dyr_ref, dg_ref, dys_ref, dxs_ref, dz_ref, ddsk_ref, dnw_ref):
        @pl.when(pl.program_id(0) == 0)
        def _():
            ddsk_ref[...] = jnp.zeros_like(ddsk_ref)
            dnw_ref[...] = jnp.zeros_like(dnw_ref)

        for h in range(RET_V // POST_W):
            sl = slice(h * POST_W, (h + 1) * POST_W)
            _, vjp = jax.vjp(_post_ret, yr_ref[:, sl], g_ref[:, sl].astype(F32))
            dyr, dg = vjp(dor_ref[:, sl])
            dyr_ref[:, sl] = dyr
            dg_ref[:, sl] = dg.astype(dg_ref.dtype)
        for g in range(SSM_INNER // POST_W):
            sl = slice(g * POST_W, (g + 1) * POST_W)
            _, vjp = jax.vjp(_post_ssm, ys_ref[:, sl], xs_ref[:, sl], z_ref[:, sl].astype(F32), dsk_ref[:, sl], nw_ref[:, sl])
            dy, dxs, dz, ddsk, dnw = vjp(dos_ref[:, sl])
            dys_ref[:, sl] = dy
            dxs_ref[:, sl] = dxs
            dz_ref[:, sl] = dz.astype(dz_ref.dtype)
            ddsk_ref[:, sl] += ddsk
            dnw_ref[:, sl] += dnw

    rowv = pl.BlockSpec((t, RET_V), lambda i: (i, 0))
    rows = pl.BlockSpec((t, SSM_INNER), lambda i: (i, 0))
    in_specs = _post_specs(t) + [rowv, rows]
    return pl.pallas_call(
        _with_rider(body, len(in_specs), 7, 0, rider, (s // t,)), grid=(s // t,), in_specs=in_specs + r_in,
        out_specs=[rowv, rowv, rows, rows, rows, _full((1, SSM_INNER)), _full((1, SSM_INNER))] + r_out,
        out_shape=[_sds((s, RET_V), F32), _sds((s, RET_V), MXU_DTYPE), _sds((s, SSM_INNER), F32),
                   _sds((s, SSM_INNER), F32), _sds((s, SSM_INNER), MXU_DTYPE),
                   _sds((1, SSM_INNER), F32), _sds((1, SSM_INNER), F32)] + r_shape,
        scratch_shapes=r_scratch,
        compiler_params=_cparams(("arbitrary",)), name=name,
    )(y_ret, proj, y_ssm, xbc, proj, dsk_e, ssm_norm.reshape(1, SSM_INNER), d_or, d_os, *r_arrs)


def _merge_fn(gr, gs, br, bs, yr, ys):
    return jax.nn.sigmoid(gr + br) * yr + jax.nn.sigmoid(gs + bs) * ys


def _merge_specs(t):
    row = pl.BlockSpec((t, D_MODEL), lambda i: (i, 0))
    return [pl.BlockSpec((t, D_MODEL), lambda i: (i, COL_GATES // D_MODEL)),
            pl.BlockSpec((t, D_MODEL), lambda i: (i, COL_GATES // D_MODEL + 1)),
            pl.BlockSpec((1, D_MODEL), lambda i: (0, 0)), pl.BlockSpec((1, D_MODEL), lambda i: (0, 1)), row, row]


def _merge_fwd(proj, b_gate, br_ret, br_ssm, name):
    s = proj.shape[0]
    t = _pick(s, 512)

    def body(gr_ref, gs_ref, br_ref, bs_ref, yr_ref, ys_ref, o_ref, ot_ref):
        o = _merge_fn(gr_ref[...].astype(F32), gs_ref[...].astype(F32), br_ref[...], bs_ref[...], yr_ref[...], ys_ref[...])
        o_ref[...] = o.astype(o_ref.dtype)
        ot_ref[...] = o.T.astype(ot_ref.dtype)

    bg = b_gate.reshape(1, 2 * D_MODEL)
    return pl.pallas_call(
        body, grid=(s // t,), in_specs=_merge_specs(t),
        out_specs=[pl.BlockSpec((t, D_MODEL), lambda i: (i, 0)), pl.BlockSpec((D_MODEL, t), lambda i: (0, i))],
        out_shape=[_sds((s, D_MODEL), MXU_DTYPE), _sds((D_MODEL, s), MXU_DTYPE)],
        compiler_params=_cparams(("parallel",)), name=name,
    )(proj, proj, bg, bg, br_ret, br_ssm)


def _merge_bwd(proj, b_gate, br_ret, br_ssm, dm, name):
    s = proj.shape[0]
    t = _pick(s, 512)

    def body(gr_ref, gs_ref, br_ref, bs_ref, yr_ref, ys_ref, dm_ref, dgt_ref, db_ref, dyr_ref, dys_ref):
        @pl.when(pl.program_id(0) == 0)
        def _():
            db_ref[...] = jnp.zeros_like(db_ref)

        _, vjp = jax.vjp(_merge_fn, gr_ref[...].astype(F32), gs_ref[...].astype(F32), br_ref[...], bs_ref[...],
                         yr_ref[...], ys_ref[...])
        dgr, dgs, dbr, dbs, dyr, dys = vjp(dm_ref[...])
        dgt_ref[:, :D_MODEL] = dgr.astype(dgt_ref.dtype)
        dgt_ref[:, D_MODEL:] = dgs.astype(dgt_ref.dtype)
        db_ref[:, :D_MODEL] += dbr
        db_ref[:, D_MODEL:] += dbs
        dyr_ref[...] = dyr.astype(dyr_ref.dtype)
        dys_ref[...] = dys.astype(dys_ref.dtype)

    bg = b_gate.reshape(1, 2 * D_MODEL)
    row = pl.BlockSpec((t, D_MODEL), lambda i: (i, 0))
    return pl.pallas_call(
        body, grid=(s // t,), in_specs=_merge_specs(t) + [row],
        out_specs=[pl.BlockSpec((t, 2 * D_MODEL), lambda i: (i, 0)), _full((1, 2 * D_MODEL)), row, row],
        out_shape=[_sds((s, 2 * D_MODEL), MXU_DTYPE), _sds((1, 2 * D_MODEL), F32),
                   _sds((s, D_MODEL), MXU_DTYPE), _sds((s, D_MODEL), MXU_DTYPE)],
        compiler_params=_cparams(("arbitrary",)), name=name,
    )(proj, proj, bg, bg, br_ret, br_ssm, dm)


def _attn_head(q, k, v):
    sc = _mm(q, k, 'nt') * (XA_HEAD_DIM ** -0.5)
    e = jnp.exp(sc - lax.stop_gradient(jnp.max(sc, axis=-1, keepdims=True)))
    p = e / jnp.sum(e, axis=-1, keepdims=True)
    return _mm(p, v, 'nn')


def _attn_fwd(q, kv, name):
    s = q.shape[0]
    m = kv.shape[0]
    t = _pick(s, 512)

    def body(q_ref, kv_ref, o_ref, ot_ref):
        for h in range(XA_HEADS):
            sl = slice(h * XA_HEAD_DIM, (h + 1) * XA_HEAD_DIM)
            vl = slice(D_MODEL + h * XA_HEAD_DIM, D_MODEL + (h + 1) * XA_HEAD_DIM)
            o = _attn_head(q_ref[:, sl], kv_ref[:, sl], kv_ref[:, vl])
            o_ref[:, sl] = o.astype(o_ref.dtype)
            ot_ref[sl, :] = o.T.astype(ot_ref.dtype)

    return pl.pallas_call(
        body, grid=(s // t,),
        in_specs=[pl.BlockSpec((t, D_MODEL), lambda i: (i, 0)), _full((m, 2 * D_MODEL))],
        out_specs=[pl.BlockSpec((t, D_MODEL), lambda i: (i, 0)), pl.BlockSpec((D_MODEL, t), lambda i: (0, i))],
        out_shape=[_sds((s, D_MODEL), MXU_DTYPE), _sds((D_MODEL, s), MXU_DTYPE)],
        compiler_params=_cparams(("parallel",)), name=name,
    )(q, kv)


def _attn_bwd(q, kv, d_o, name):
    s = q.shape[0]
    m = kv.shape[0]
    t = _pick(s, 512)

    def body(q_ref, kv_ref, do_ref, dq_ref, dkv_ref):
        @pl.when(pl.program_id(0) == 0)
        def _():
            dkv_ref[...] = jnp.zeros_like(dkv_ref)

        for h in range(XA_HEADS):
            sl = slice(h * XA_HEAD_DIM, (h + 1) * XA_HEAD_DIM)
            vl = slice(D_MODEL + h * XA_HEAD_DIM, D_MODEL + (h + 1) * XA_HEAD_DIM)
            _, vjp = jax.vjp(_attn_head, q_ref[:, sl], kv_ref[:, sl], kv_ref[:, vl])
            dq, dk, dv = vjp(do_ref[:, sl])
            dq_ref[:, sl] = dq.astype(dq_ref.dtype)
            dkv_ref[:, sl] += dk
            dkv_ref[:, vl] += dv

    row = pl.BlockSpec((t, D_MODEL), lambda i: (i, 0))
    return pl.pallas_call(
        body, grid=(s // t,), in_specs=[row, _full((m, 2 * D_MODEL)), row],
        out_specs=[row, _full((m, 2 * D_MODEL))],
        out_shape=[_sds((s, D_MODEL), MXU_DTYPE), _sds((m, 2 * D_MODEL), F32)],
        compiler_params=_cparams(("arbitrary",)), name=name,
    )(q, kv, d_o)


def _loss_head(x, w, target, name):
    s, d = x.shape
    t = _pick(s, 512)

    def body(x_ref, w_ref, t_ref, loss_ref, dx_ref, dxb_ref, dw_ref):
        @pl.when(pl.program_id(0) == 0)
        def _():
            loss_ref[...] = jnp.zeros_like(loss_ref)
            dw_ref[...] = jnp.zeros_like(dw_ref)

        y, vjp = jax.vjp(_rmsnorm_fn, x_ref[...], w_ref[...])
        err = y - t_ref[...]
        loss_ref[...] += 0.5 * jnp.sum(jnp.sum(err * err, axis=-1, keepdims=True), axis=0, keepdims=True) / d
        dx, dw = vjp(err * (1.0 / d))
        dx_ref[...] = dx
        dxb_ref[...] = dx.astype(dxb_ref.dtype)
        dw_ref[...] += dw

    row = pl.BlockSpec((t, d), lambda i: (i, 0))
    return pl.pallas_call(
        body, grid=(s // t,), in_specs=[row, _full((1, d)), row],
        out_specs=[_full((1, LANES)), row, row, _full((1, d))],
        out_shape=[_sds((1, LANES), F32), _sds((s, d), F32), _sds((s, d), MXU_DTYPE), _sds((1, d), F32)],
        compiler_params=_cparams(("arbitrary",)), name=name,
    )(x, w.reshape(1, d), target)


def _epi_sqrelu(acc):
    r = jnp.maximum(acc, 0.0)
    return r * r, r * r


def _epi_sqrelu_bwd(acc, act):
    return (acc * (2.0 * jnp.sqrt(act.astype(F32))),)


def _rope_tables(positions):
    inv_freq = ROPE_THETA ** (-jnp.arange(0, RET_QK_DIM, 2, dtype=F32) / RET_QK_DIM)
    ang = positions.astype(F32)[:, None] * inv_freq
    cos, sin = jnp.cos(ang), jnp.sin(ang)
    return jnp.concatenate([cos, cos], axis=1), jnp.concatenate([-sin, sin], axis=1)


W_IN_ORIG = (('q', 0, 512), ('k', 512, 1024), ('v', 1024, 2048), ('g', 2048, 3072), ('z', 3072, 5120),
             ('xbc', 5120, 9216), ('dt', 9216, 9248), ('gates', 9248, 11296))
W_IN_MAIN_ORDER = ('z', 'xbc', 'gates', 'v', 'g', 'q', 'k')
W_IN_SHARD = IN_DIM // N_DEV


def _shard_segments(lo, hi):
    segs = []
    for j in range(lo // W_IN_SHARD, (hi - 1) // W_IN_SHARD + 1):
        segs.append((j, max(lo, j * W_IN_SHARD) - j * W_IN_SHARD, min(hi, (j + 1) * W_IN_SHARD) - j * W_IN_SHARD))
    return segs


def _w_in_from_shards(g):
    rng = {name: (lo, hi) for name, lo, hi in W_IN_ORIG}
    cols = [g[j][:, a:b] for name in W_IN_MAIN_ORDER for j, a, b in _shard_segments(*rng[name])]
    (j, a, b), = _shard_segments(*rng['dt'])
    return jnp.concatenate(cols, axis=1), jnp.pad(g[j][:, a:b], ((0, 0), (0, DT_PAD - SSM_HEADS)))


def _w_in_grad_blocks(d, d_dt):
    src_of = {'q': ('qk', 0), 'k': ('qk', RET_QK)}
    blocks = []
    for j in range(N_DEV):
        lo_j, hi_j = j * W_IN_SHARD, (j + 1) * W_IN_SHARD
        cols = []
        for name, lo, hi in W_IN_ORIG:
            a, b = max(lo, lo_j), min(hi, hi_j)
            if a >= b:
                continue
            if name == 'dt':
                cols.append(d_dt[:, a - lo:b - lo])
            else:
                key, off = src_of.get(name, (name, 0))
                cols.append(d[key][:, off + a - lo:off + b - lo])
        blocks.append(jnp.concatenate(cols, axis=1))
    return blocks


def _lanes_of_heads(v):
    return jnp.repeat(v, SSM_HEAD_DIM).reshape(1, SSM_INNER)


def _heads_of_lanes(v):
    return v.reshape(SSM_HEADS, SSM_HEAD_DIM).sum(axis=1)


def _layer_fwd(x, mem, cos2, sin2, p, l, later_blocks, rider_proj=None):
    n = lambda s: f"{s}_l{l}"
    sv = {'x0': x}
    u, u_t = _rmsnorm(x, p['norm_mix'], n("norm_mix"))
    s = x.shape[0]
    proj = _matmul(u, p['w_in_main'], 'nn', n("in_proj"), out_dtypes=(MXU_DTYPE,), tiles=(s, 512, D_MODEL), rider=rider_proj)
    if rider_proj is not None:
        proj, next_first = proj
    else:
        next_first = []
    dtraw = _matmul(u, p['w_in_dt'], 'nn', n("in_proj_dt"))
    xbc = _conv_fwd(proj, p['conv_w'], p['conv_b'], n("conv"))
    a_c = jnp.pad(-jnp.exp(p['a_log']), (0, DT_PAD - SSM_HEADS)).reshape(1, DT_PAD)
    dtb = jnp.pad(p['dt_bias'], (0, DT_PAD - SSM_HEADS)).reshape(1, DT_PAD)
    y_ret, y_ssm, s_hist, h_hist, *gathered = _scan_fwd(proj, xbc, dtraw, cos2, sin2, a_c, dtb, n("scan"),
                                                        _AllGather(later_blocks))
    p = {**p, **_full_weights(LATER, gathered)}
    dsk_e = _lanes_of_heads(p['d_skip'])
    o_ret, o_ssm, o_ret_t, o_ssm_t = _post_fwd(y_ret, proj, y_ssm, xbc, dsk_e, p['ssm_norm'], n("post"))
    br_ret = _matmul(o_ret, p['w_br_ret'], 'nn', n("br_ret"))
    br_ssm = _matmul(o_ssm, p['w_br_ssm'], 'nn', n("br_ssm"))
    merged, merged_t = _merge_fwd(proj, p['b_gate'], br_ret, br_ssm, n("merge"))
    x1 = _matmul(merged, p['w_out'], 'nn', n("w_out"), extras=(x,), epi=_epi_add)
    sv.update(u_t=u_t, proj=proj, dtraw=dtraw, xbc=xbc, a_c=a_c, dtb=dtb, y_ret=y_ret, y_ssm=y_ssm, s_hist=s_hist,
              h_hist=h_hist, dsk_e=dsk_e, o_ret_t=o_ret_t, o_ssm_t=o_ssm_t, br_ret=br_ret, br_ssm=br_ssm,
              merged_t=merged_t, x1=x1)
    hq, hq_t = _rmsnorm(x1, p['norm_xa'], n("norm_xa"))
    memn, _ = _rmsnorm(mem, p['norm_mem'], n("norm_mem"))
    q = _matmul(hq, p['xa_wq'], 'nn', n("xa_q"))
    kv = _matmul(memn, p['xa_wkv'], 'nn', n("xa_kv"))
    o, o_t = _attn_fwd(q, kv, n("attn"))
    x2 = _matmul(o, p['xa_wo'], 'nn', n("xa_o"), extras=(x1,), epi=_epi_add)
    sv.update(hq_t=hq_t, memn=memn, q=q, kv=kv, o_t=o_t, x2=x2)
    hm, hm_t = _rmsnorm(x2, p['norm_mlp'], n("norm_mlp"))
    act, act_t = _matmul(hm, p['mlp_w1'], 'nn', n("mlp_1"), epi=_epi_sqrelu, out_dtypes=(MXU_DTYPE, MXU_DTYPE),
                         out_t=(False, True), tiles=(s, 512, D_MODEL))
    x3 = _matmul(act, p['mlp_w2'], 'nn', n("mlp_2"), extras=(x2,), epi=_epi_add, tiles=(_pick(s, 1024), D_MODEL, D_FF))
    sv.update(hm_t=hm_t, act=act, act_t=act_t)
    return x3, sv, p, next_first


def _layer_bwd(dx, dxb, mem, cos2, sin2, p, sv, l, pending, c_idx):
    n = lambda s: f"{s}_bwd_l{l}"
    gd = (MXU_DTYPE,)
    g = {}
    g['mlp_w2'] = _matmul(sv['act_t'], dxb, 'nn', n("mlp_2_dw"), out_dtypes=gd, scatter='rows')
    da = _matmul(dxb, p['mlp_w2'], 'nt', n("mlp_2_dx"), extras=(sv['act'],), epi=_epi_sqrelu_bwd, out_dtypes=(MXU_DTYPE,),
                 tiles=(_pick(dxb.shape[0], 512), D_FF, D_MODEL))
    g['mlp_w1'] = _matmul(sv['hm_t'], da, 'nn', n("mlp_1_dw"), out_dtypes=gd, scatter='cols')
    dhm = _matmul(da, p['mlp_w1'], 'nt', n("mlp_1_dx"))
    dx2, dx2b, g['norm_mlp'] = _rmsnorm_bwd(sv['x2'], p['norm_mlp'], dhm, dx, n("norm_mlp"))
    g['xa_wo'] = _matmul(sv['o_t'], dx2b, 'nn', n("xa_o_dw"), out_dtypes=gd, scatter='rows')
    d_o = _matmul(dx2b, p['xa_wo'], 'nt', n("xa_o_dx"))
    dq, dkv = _attn_bwd(sv['q'], sv['kv'], d_o, n("attn"))
    g['xa_wq'] = _matmul(sv['hq_t'], dq, 'nn', n("xa_q_dw"), out_dtypes=gd, scatter='rows')
    dhq = _matmul(dq, p['xa_wq'], 'nt', n("xa_q_dx"))
    g['xa_wkv'] = _matmul(sv['memn'], dkv, 'tn', n("xa_kv_dw"), out_dtypes=gd, scatter='cols')
    dmemn = _matmul(dkv, p['xa_wkv'], 'nt', n("xa_kv_dx"))
    _, _, g['norm_mem'] = _rmsnorm_bwd(mem, p['norm_mem'], dmemn, None, n("norm_mem"))
    dx1, dx1b, g['norm_xa'] = _rmsnorm_bwd(sv['x1'], p['norm_xa'], dhq, dx2, n("norm_xa"))
    g['w_out'] = _matmul(sv['merged_t'], dx1b, 'nn', n("w_out_dw"), out_dtypes=gd, scatter='rows')
    dmerged = _matmul(dx1b, p['w_out'], 'nt', n("w_out_dx"))
    dgates, g['b_gate'], dbr_ret, dbr_ssm = _merge_bwd(sv['proj'], p['b_gate'], sv['br_ret'], sv['br_ssm'], dmerged, n("merge"))
    g['w_br_ret'] = _matmul(sv['o_ret_t'], dbr_ret, 'nn', n("br_ret_dw"), out_dtypes=gd, scatter='rows')
    g['w_br_ssm'] = _matmul(sv['o_ssm_t'], dbr_ssm, 'nn', n("br_ssm_dw"), out_dtypes=gd, scatter='rows')
    d_or = _matmul(dbr_ret, p['w_br_ret'], 'nt', n("br_ret_dx"))
    d_os = _matmul(dbr_ssm, p['w_br_ssm'], 'nt', n("br_ssm_dx"))
    later_by_core = [_grad_scatter(k, g[k]) for k in LATER]
    dyr, dg, dys, dxs_skip, dz, ddsk_e, g['ssm_norm'], *later_sib = _post_bwd(
        sv['y_ret'], sv['proj'], sv['y_ssm'], sv['xbc'], sv['dsk_e'], p['ssm_norm'], d_or, d_os, n("post"),
        _ExchangeCores(later_by_core))
    rider = _ExchangeChips(list(pending) + _core_sums(LATER, later_by_core, later_sib, c_idx, l))
    g['d_skip'] = _heads_of_lanes(ddsk_e)
    dqk_r, dv_r, dxbc_act, ddtraw, dac, ddtb, *delivered = _scan_bwd(
        sv['proj'], sv['xbc'], sv['dtraw'], cos2, sin2, sv['a_c'], sv['dtb'], sv['s_hist'], sv['h_hist'],
        dyr, dys, dxs_skip, n("scan"), rider)
    g['a_log'] = dac[0, :SSM_HEADS] * (-jnp.exp(p['a_log']))
    g['dt_bias'] = ddtb[0, :SSM_HEADS]
    dxbc_raw, g['conv_w'], g['conv_b'] = _conv_bwd(sv['proj'], p['conv_w'], p['conv_b'], dxbc_act, n("conv"))
    pieces = {'z': dz, 'xbc': dxbc_raw, 'gates': dgates, 'v': dv_r, 'g': dg, 'qk': dqk_r}
    d_w = {k: _matmul(sv['u_t'], pc, 'nn', n(f"in_proj_dw_{k}"), out_dtypes=gd) for k, pc in pieces.items()}
    d_dt = _matmul(sv['u_t'], ddtraw, 'nn', n("in_proj_dt_dw"), out_dtypes=gd)
    g['w_in'] = _w_in_grad_blocks(d_w, d_dt)
    du_dt = _matmul(ddtraw, p['w_in_dt'], 'nt', n("in_proj_dt_dx"))
    first_by_core = [_grad_scatter(k, g[k]) for k in FIRST]
    du, first_sib = _matmul_nt_pieces(list(pieces.values()), p['w_in_main'], n("in_proj_dx"), extras=(du_dt,), epi=_epi_add,
                                      rider=_ExchangeCores(first_by_core))
    dx0, dx0b, g['norm_mix'] = _rmsnorm_bwd(sv['x0'], p['norm_mix'], du, dx1, n("norm_mix"))
    return (dx0, dx0b, g, delivered[:len(pending)], delivered[len(pending):],
            _core_sums(FIRST, first_by_core, first_sib, c_idx, l))


def _full_weights(names, gathered):
    p = {}
    for k, g in zip(names, gathered):
        if k == 'w_in':
            p['w_in_main'], p['w_in_dt'] = _w_in_from_shards(g)
        elif k in COL_SHARDED:
            p[k] = jnp.concatenate([g[j] for j in range(N_DEV)], axis=1)
        else:
            p[k] = g.reshape(-1, g.shape[-1])
    return p


def _grad_scatter(k, g):
    if k in LATER:
        return g
    if k == 'w_in':
        blocks = g
    elif k in COL_SHARDED:
        c = g.shape[1] // N_DEV
        blocks = [g[:, j * c:(j + 1) * c] for j in range(N_DEV)]
    else:
        r = g.shape[0] // N_DEV
        blocks = [g[j * r:(j + 1) * r] for j in range(N_DEV)]
    return jnp.stack([jnp.stack([blocks[2 * chip + core] for chip in range(4)]) for core in range(2)])


def _core_sums(names, by_core, from_sibling, c_idx, l):
    return [_add_halves(a, o, c_idx, f"grad_add_cores_{k}_l{l}") for k, a, o in zip(names, by_core, from_sibling)]


def _step(x, mem, positions, small, blocks, loss_target):
    cos2, sin2 = _rope_tables(positions)
    first = _run_exchange(_AllGather([blocks[0][k] for k in FIRST]), "all_gather_first_l0")
    saved, layers = [], []
    for l in range(DEPTH):
        p = {k: small[k][l] for k in SMALL if k != 'norm_final'}
        p.update(_full_weights(FIRST, first))
        rider = _AllGather([blocks[l + 1][k] for k in FIRST]) if l + 1 < DEPTH else None
        x, sv, p, first = _layer_fwd(x, mem, cos2, sin2, p, l, [blocks[l][k] for k in LATER], rider)
        saved.append(sv)
        layers.append(p)
    loss, dx, dxb, dnf = _loss_head(x, small['norm_final'], loss_target, "loss_head")
    c_idx = lax.axis_index("c").astype(jnp.int32).reshape(1)
    grads, by_chip, pending = [None] * DEPTH, [dict() for _ in range(DEPTH)], []
    for l in reversed(range(DEPTH)):
        dx, dxb, grads[l], got_first, got_later, pending_next = _layer_bwd(
            dx, dxb, mem, cos2, sin2, layers[l], saved[l], l, pending, c_idx)
        if pending:
            by_chip[l + 1].update(zip(FIRST, got_first))
        by_chip[l].update(zip(LATER, got_later))
        pending = pending_next
    by_chip[0].update(zip(FIRST, _run_exchange(_ExchangeChips(pending), "grad_exchange_chips_first_l0")))
    small_g = {}
    for k in SMALL:
        small_g[k] = dnf.reshape(D_MODEL) if k == 'norm_final' else [grads[l][k].reshape(small[k].shape[1:]) for l in range(DEPTH)]
    return loss, dx, small_g, by_chip


MESH = pl.DeviceIdType.MESH
ANY_SPEC = pl.BlockSpec(memory_space=pl.ANY)


def _mesh_pos():
    return lax.axis_index("x"), lax.axis_index("y"), lax.axis_index("c")


def _other_chips(x, y):
    return [(1 - x, y), (x, 1 - y), (1 - x, 1 - y)]


class _AllGather:
    def __init__(self, arrs):
        self.arrs = list(arrs)
        na = self.n = len(self.arrs)
        self.out_shape = [_sds((N_DEV,) + a.shape, a.dtype) for a in self.arrs]
        self.scratch = [pltpu.SemaphoreType.DMA((na, 7)), pltpu.SemaphoreType.DMA((na, 7)), pltpu.SemaphoreType.DMA((na,))]

    def _copies(self, x_refs, o_refs, sems):
        send_sems, recv_sems, local_sems = sems
        x, y, c = _mesh_pos()
        me, sib = (x, y, c), (x, y, 1 - c)
        chips = _other_chips(x, y)

        def copy(a, k, block, to, src=None):
            dst = o_refs[a].at[4 * block[0] + 2 * block[1] + block[2]]
            return pltpu.make_async_remote_copy(src_ref=dst if src is None else src, dst_ref=dst,
                                                send_sem=send_sems.at[a, k], recv_sem=recv_sems.at[a, k],
                                                device_id=to, device_id_type=MESH)

        mine = [pltpu.make_async_copy(x_refs[a], o_refs[a].at[4 * x + 2 * y + c], local_sems.at[a]) for a in range(self.n)]
        first = []
        for a in range(self.n):
            first.append(copy(a, 0, me, sib, src=x_refs[a]))
            first += [copy(a, 1 + j, me, (*chip, c), src=x_refs[a]) for j, chip in enumerate(chips)]
        return copy, mine, first, me, sib, chips, c

    def start(self, x_refs, o_refs, sems):
        _, mine, first, *_ = self._copies(x_refs, o_refs, sems)
        for cp in mine + first:
            cp.start()

    def finish(self, x_refs, o_refs, sems):
        copy, mine, first, me, sib, chips, c = self._copies(x_refs, o_refs, sems)
        passed = []
        for a in range(self.n):
            for j, chip in enumerate(chips):
                copy(a, 1 + j, (*chip, c), me).wait_recv()
                cp = copy(a, 4 + j, (*chip, c), sib)
                cp.start()
                passed.append(cp)
        for a in range(self.n):
            copy(a, 0, sib, me).wait_recv()
            for j, chip in enumerate(chips):
                copy(a, 4 + j, (*chip, 1 - c), me).wait_recv()
        for cp in first + passed:
            cp.wait_send()
        for cp in mine:
            cp.wait()


class _ExchangeChips:
    def __init__(self, arrs):
        self.arrs = list(arrs)
        na = self.n = len(self.arrs)
        self.out_shape = [_sds(a.shape, a.dtype) for a in self.arrs]
        self.scratch = [pltpu.SemaphoreType.DMA((na, 3)), pltpu.SemaphoreType.DMA((na, 3)), pltpu.SemaphoreType.DMA((na,))]

    def _copies(self, a_refs, o_refs, sems):
        send_sems, recv_sems, local_sems = sems
        x, y, c = _mesh_pos()
        my_chip = 2 * x + y
        chips = _other_chips(x, y)
        mine = [pltpu.make_async_copy(a_refs[a].at[my_chip], o_refs[a].at[my_chip], local_sems.at[a]) for a in range(self.n)]
        sends = [pltpu.make_async_remote_copy(src_ref=a_refs[a].at[2 * px + py], dst_ref=o_refs[a].at[my_chip],
                                              send_sem=send_sems.at[a, j], recv_sem=recv_sems.at[a, j],
                                              device_id=(px, py, c), device_id_type=MESH)
                 for a in range(self.n) for j, (px, py) in enumerate(chips)]
        recvs = [pltpu.make_async_remote_copy(src_ref=a_refs[a].at[2 * px + py], dst_ref=o_refs[a].at[2 * px + py],
                                              send_sem=send_sems.at[a, j], recv_sem=recv_sems.at[a, j],
                                              device_id=(px, py, c), device_id_type=MESH)
                 for a in range(self.n) for j, (px, py) in enumerate(chips)]
        return mine, sends, recvs

    def start(self, a_refs, o_refs, sems):
        mine, sends, _ = self._copies(a_refs, o_refs, sems)
        for cp in mine + sends:
            cp.start()

    def finish(self, a_refs, o_refs, sems):
        mine, sends, recvs = self._copies(a_refs, o_refs, sems)
        for cp in recvs:
            cp.wait_recv()
        for cp in sends:
            cp.wait_send()
        for cp in mine:
            cp.wait()


def _run_exchange(ex, name):
    na = ex.n

    def body(*refs):
        i_refs, o_refs, sems = refs[:na], refs[na:2 * na], refs[2 * na:]
        ex.start(i_refs, o_refs, sems)
        ex.finish(i_refs, o_refs, sems)

    return pl.pallas_call(body, in_specs=[ANY_SPEC] * na, out_specs=[ANY_SPEC] * na, out_shape=ex.out_shape,
                          scratch_shapes=ex.scratch, name=name)(*ex.arrs)


class _ExchangeCores:
    def __init__(self, arrs):
        self.arrs = list(arrs)
        na = self.n = len(self.arrs)
        self.out_shape = [_sds(a.shape[1:], a.dtype) for a in self.arrs]
        self.scratch = [pltpu.SemaphoreType.DMA((na,)), pltpu.SemaphoreType.DMA((na,))]

    def _copies(self, a_refs, o_refs, sems):
        send_sems, recv_sems = sems
        x, y, c = _mesh_pos()
        return [pltpu.make_async_remote_copy(src_ref=a_refs[a].at[1 - c], dst_ref=o_refs[a], send_sem=send_sems.at[a],
                                             recv_sem=recv_sems.at[a], device_id=(x, y, 1 - c), device_id_type=MESH)
                for a in range(self.n)]

    def start(self, a_refs, o_refs, sems):
        for cp in self._copies(a_refs, o_refs, sems):
            cp.start()

    def finish(self, a_refs, o_refs, sems):
        for cp in self._copies(a_refs, o_refs, sems):
            cp.wait()


def _as_rows(a, lead):
    return a.reshape(a.shape[:lead] + (-1, a.shape[-1]))


def _add_halves(a, other, c_idx, name):
    a3, o2 = _as_rows(a, 1), _as_rows(other, 0)
    rows, cols = o2.shape
    tr = _pick(rows, 256)

    def body(c_ref, a_ref, o_ref, out_ref):
        out_ref[...] = (a_ref[0].astype(F32) + o_ref[...].astype(F32)).astype(out_ref.dtype)

    out = pl.pallas_call(
        body,
        grid_spec=pltpu.PrefetchScalarGridSpec(
            num_scalar_prefetch=1, grid=(rows // tr,),
            in_specs=[pl.BlockSpec((1, tr, cols), lambda i, c_ref: (c_ref[0], i, 0)),
                      pl.BlockSpec((tr, cols), lambda i, c_ref: (i, 0))],
            out_specs=pl.BlockSpec((tr, cols), lambda i, c_ref: (i, 0))),
        out_shape=_sds((rows, cols), a.dtype), compiler_params=_cparams(("parallel",)), name=name,
    )(c_idx, a3, o2)
    return out.reshape(other.shape)


def _all_reduce_small(v, name):
    r = v.shape[0]

    def body(v_ref, o_ref, slots, send_sems, recv_sems):
        x, y, c = _mesh_pos()
        me = 4 * x + 2 * y + c
        slots[me] = v_ref[...]
        cps = []
        for k in range(1, N_DEV):
            px = 1 - x if k & 4 else x
            py = 1 - y if k & 2 else y
            pc = 1 - c if k & 1 else c
            cps.append(pltpu.make_async_remote_copy(src_ref=v_ref, dst_ref=slots.at[me], send_sem=send_sems.at[k - 1],
                                                    recv_sem=recv_sems.at[k - 1], device_id=(px, py, pc), device_id_type=MESH))
        for cp in cps:
            cp.start()
        for cp in cps:
            cp.wait()
        acc = slots[0]
        for d in range(1, N_DEV):
            acc = acc + slots[d]
        o_ref[...] = acc

    vm = pl.BlockSpec(memory_space=pltpu.VMEM)
    return pl.pallas_call(
        body, in_specs=[vm], out_specs=vm, out_shape=_sds((r, LANES), F32),
        scratch_shapes=[pltpu.VMEM((N_DEV, r, LANES), F32), pltpu.SemaphoreType.DMA((N_DEV - 1,)),
                        pltpu.SemaphoreType.DMA((N_DEV - 1,))],
        compiler_params=pltpu.CompilerParams(vmem_limit_bytes=VMEM_LIMIT_BYTES), name=name,
    )(v)


def _adamw(w, g_slots, m, v, name):
    depth, rows, cols = w.shape
    ns = g_slots.shape[0]
    tr = _pick(rows, 256 if cols <= 1024 else 128)

    def body(w_ref, g_ref, m_ref, v_ref, go_ref, d_ref, mo_ref, vo_ref):
        g = g_ref[0, 0].astype(F32)
        for i in range(1, ns):
            g = g + g_ref[i, 0].astype(F32)
        m_new = ADAM_B1 * m_ref[0] + (1.0 - ADAM_B1) * g
        v_new = ADAM_B2 * v_ref[0] + (1.0 - ADAM_B2) * (g * g)
        m_hat = m_new / (1.0 - ADAM_B1 ** ADAM_STEP)
        v_hat = v_new / (1.0 - ADAM_B2 ** ADAM_STEP)
        go_ref[0] = g
        d_ref[0] = -ADAM_LR * (m_hat / (jnp.sqrt(v_hat) + ADAM_EPS) + ADAM_WD * w_ref[0])
        mo_ref[0] = m_new
        vo_ref[0] = v_new

    blk = pl.BlockSpec((1, tr, cols), lambda l, i: (l, i, 0))
    return pl.pallas_call(
        body, grid=(depth, rows // tr),
        in_specs=[blk, pl.BlockSpec((ns, 1, tr, cols), lambda l, i: (0, l, i, 0)), blk, blk],
        out_specs=[blk] * 4, out_shape=[_sds(w.shape, F32)] * 4,
        compiler_params=_cparams(("parallel", "parallel")), name=name,
    )(w, g_slots, m, v)


_ARG_NAMES = (['x', 'mem', 'positions'] + WEIGHTS + ['loss_target'] + ['m_' + n for n in WEIGHTS]
              + ['v_' + n for n in WEIGHTS])


PACK_TILE = 8 * LANES


def _pack_rows(parts):
    blocks = []
    for part in parts:
        flat = part.reshape(-1)
        pad = (-flat.shape[0]) % PACK_TILE
        blocks.append((jnp.pad(flat, (0, pad)) if pad else flat).reshape(-1, LANES))
    return jnp.concatenate(blocks, axis=0)


def _unpack_rows(packed, shapes):
    out, off = [], 0
    for shp in shapes:
        n = int(np.prod(shp))
        rows = -(-n // PACK_TILE) * 8
        out.append(packed[off:off + rows].reshape(-1)[:n].reshape(shp))
        off += rows
    return out


def kernel(x, mem, positions, norm_mix, w_in, b_gate, conv_w, conv_b, dt_bias, a_log, d_skip, ssm_norm, w_br_ret, w_br_ssm, w_out, norm_xa, norm_mem, xa_wq, xa_wkv, xa_wo, norm_mlp, mlp_w1, mlp_w2, norm_final, loss_target, m_norm_mix, m_w_in, m_b_gate, m_conv_w, m_conv_b, m_dt_bias, m_a_log, m_d_skip, m_ssm_norm, m_w_br_ret, m_w_br_ssm, m_w_out, m_norm_xa, m_norm_mem, m_xa_wq, m_xa_wkv, m_xa_wo, m_norm_mlp, m_mlp_w1, m_mlp_w2, m_norm_final, v_norm_mix, v_w_in, v_b_gate, v_conv_w, v_conv_b, v_dt_bias, v_a_log, v_d_skip, v_ssm_norm, v_w_br_ret, v_w_br_ssm, v_w_out, v_norm_xa, v_norm_mem, v_xa_wq, v_xa_wkv, v_xa_wo, v_norm_mlp, v_mlp_w1, v_mlp_w2, v_norm_final):
    d = dict(zip(_ARG_NAMES, (x, mem, positions, norm_mix, w_in, b_gate, conv_w, conv_b, dt_bias, a_log, d_skip, ssm_norm, w_br_ret, w_br_ssm, w_out, norm_xa, norm_mem, xa_wq, xa_wkv, xa_wo, norm_mlp, mlp_w1, mlp_w2, norm_final, loss_target, m_norm_mix, m_w_in, m_b_gate, m_conv_w, m_conv_b, m_dt_bias, m_a_log, m_d_skip, m_ssm_norm, m_w_br_ret, m_w_br_ssm, m_w_out, m_norm_xa, m_norm_mem, m_xa_wq, m_xa_wkv, m_xa_wo, m_norm_mlp, m_mlp_w1, m_mlp_w2, m_norm_final, v_norm_mix, v_w_in, v_b_gate, v_conv_w, v_conv_b, v_dt_bias, v_a_log, v_d_skip, v_ssm_norm, v_w_br_ret, v_w_br_ssm, v_w_out, v_norm_xa, v_norm_mem, v_xa_wq, v_xa_wkv, v_xa_wo, v_norm_mlp, v_mlp_w1, v_mlp_w2, v_norm_final)))
    blocks = [{k: d[k][l] if k == 'conv_w' else d[k][l].astype(MXU_DTYPE) for k in SHARDED} for l in range(DEPTH)]
    small = {k: d[k] for k in SMALL}
    loss, grad_x, grads, by_chip_l = _step(d['x'][0], d['mem'][0], d['positions'][0], small, blocks, d['loss_target'][0])
    by_chip = [jnp.stack([by_chip_l[l][k] for l in range(DEPTH)], axis=1) for k in SHARDED]
    small_g = [grads[k] if k == 'norm_final' else jnp.stack(grads[k]) for k in SMALL]
    total = _all_reduce_small(_pack_rows([loss] + small_g), "all_reduce_small")
    loss_out = total[0, 0]
    res = {}
    for k, g4 in zip(SHARDED, by_chip):
        res[k] = _adamw(d[k], g4, d['m_' + k], d['v_' + k], f"adamw_{k}")
    small_shapes = [d[k].shape for k in SMALL]
    pk = lambda pre: _pack_rows([d[pre + k] for k in SMALL])
    outs = _adamw(pk('')[None], total[8:][None, None], pk('m_')[None], pk('v_')[None], "adamw_small")
    unpacked = [_unpack_rows(o[0], small_shapes) for o in outs]
    for i, k in enumerate(SMALL):
        res[k] = [unpacked[j][i] for j in range(4)]
    return (loss_out, grad_x[None], *[res[k][0] for k in WEIGHTS], *[res[k][1] for k in WEIGHTS],
            *[res[k][2] for k in WEIGHTS], *[res[k][3] for k in WEIGHTS])
```

```python
import functools

import numpy as np
import jax
import jax.numpy as jnp
from jax import lax
from jax.experimental import pallas as pl
from jax.experimental.pallas import tpu as pltpu

F32 = jnp.float32
MXU_DTYPE = jnp.bfloat16
VMEM_LIMIT_BYTES = 56 * 1024 * 1024
LANES = 128
N_DEV = 8

D_MODEL = 1024
DEPTH = 4
CHUNK = 64
EPS = 1e-6
RET_HEADS, RET_QK_DIM, RET_V_DIM = 4, 128, 256
RET_QK, RET_V = 512, 1024
ROPE_THETA = 10000.0
SSM_INNER, SSM_HEAD_DIM, SSM_HEADS, SSM_GROUPS, SSM_STATE, SSM_CONV = 2048, 64, 32, 8, 128, 4
SSM_BC = 1024
SSM_CONV_DIM = 4096
IN_DIM = 11296
XA_HEADS, XA_HEAD_DIM = 4, 256
D_FF = 4096
ADAM_LR, ADAM_B1, ADAM_B2, ADAM_EPS, ADAM_WD, ADAM_STEP = 0.001, 0.9, 0.999, 1e-08, 0.01, 10

PROJ_W = 11264
COL_Z, COL_XBC, COL_GATES, COL_V, COL_G, COL_Q, COL_K = 0, 2048, 6144, 8192, 9216, 10240, 10752
DT_PAD = 128
N_LTILE = SSM_INNER // LANES

WEIGHTS = ['norm_mix', 'w_in', 'b_gate', 'conv_w', 'conv_b', 'dt_bias', 'a_log', 'd_skip', 'ssm_norm',
           'w_br_ret', 'w_br_ssm', 'w_out', 'norm_xa', 'norm_mem', 'xa_wq', 'xa_wkv', 'xa_wo', 'norm_mlp',
           'mlp_w1', 'mlp_w2', 'norm_final']
COL_SHARDED = ['w_in', 'conv_w', 'xa_wkv', 'mlp_w1']
ROW_SHARDED = ['w_br_ret', 'w_br_ssm', 'w_out', 'xa_wq', 'xa_wo', 'mlp_w2']
SHARDED = COL_SHARDED + ROW_SHARDED
FIRST = ['w_in', 'conv_w']
LATER = [n for n in SHARDED if n not in FIRST]
SMALL = [n for n in WEIGHTS if n not in SHARDED]


def _cparams(sem=None):
    return pltpu.CompilerParams(dimension_semantics=sem, vmem_limit_bytes=VMEM_LIMIT_BYTES)


def _sds(shape, dtype):
    return jax.ShapeDtypeStruct(shape, dtype)


def _full(shape):
    nd = len(shape)
    return pl.BlockSpec(shape, lambda *_: (0,) * nd)


_DIMS = {'nn': (((1,), (0,)), ((), ())), 'nt': (((1,), (1,)), ((), ())), 'tn': (((0,), (0,)), ((), ()))}


def _dot(a, b, mode='nn'):
    return lax.dot_general(a.astype(MXU_DTYPE), b.astype(MXU_DTYPE), _DIMS[mode], preferred_element_type=F32)


@functools.partial(jax.custom_vjp, nondiff_argnums=(2,))
def _mm(a, b, mode):
    return _dot(a, b, mode)


def _mm_fwd(a, b, mode):
    return _dot(a, b, mode), (a, b)


def _mm_bwd(mode, res, g):
    a, b = res
    if mode == 'nn':
        return _dot(g, b, 'nt'), _dot(a, g, 'tn')
    if mode == 'nt':
        return _dot(g, b, 'nn'), _dot(g, a, 'tn')
    return _dot(b, g, 'nt'), _dot(a, g, 'nn')


_mm.defvjp(_mm_fwd, _mm_bwd)


def _split3(x):
    hi = x.astype(jnp.bfloat16)
    r1 = x - hi.astype(F32)
    mid = r1.astype(jnp.bfloat16)
    lo = (r1 - mid.astype(F32)).astype(jnp.bfloat16)
    return hi, mid, lo


def _dot_sel(x, c, left=False):
    dims = _DIMS['nn']
    parts = _split3(x)
    if left:
        outs = [lax.dot_general(c, p, dims, preferred_element_type=F32) for p in parts]
    else:
        outs = [lax.dot_general(p, c, dims, preferred_element_type=F32) for p in parts]
    return (outs[0] + outs[1]) + outs[2]


def _silu(x):
    return x * jax.nn.sigmoid(x)


def _softplus(x):
    pos = x > 0.0
    return jnp.where(pos, x, 0.0) + jnp.log1p(jnp.exp(jnp.where(pos, -x, x)))


def _rms(x):
    return x * lax.rsqrt(jnp.mean(x * x, axis=-1, keepdims=True) + EPS)


def _pick(n, pref):
    t = min(n, pref)
    while n % t:
        t //= 2
    return t


def _with_rider(core, n_in, n_out, n_scratch, rider, grid):
    if rider is None:
        return core
    na, nrs = rider.n, len(rider.scratch)

    def at(step_of):
        cond = pl.program_id(0) == step_of(grid[0])
        for ax in range(1, len(grid)):
            cond = cond & (pl.program_id(ax) == step_of(grid[ax]))
        return cond

    def body(*refs):
        ci, ri = refs[:n_in], refs[n_in:n_in + na]
        co, ro = refs[n_in + na:n_in + na + n_out], refs[n_in + na + n_out:n_in + 2 * na + n_out]
        sc = refs[n_in + 2 * na + n_out:]
        cs, rs = sc[:n_scratch], sc[n_scratch:]
        assert len(rs) == nrs

        @pl.when(at(lambda n: 0))
        def _():
            rider.start(ri, ro, rs)

        core(*ci, *co, *cs)

        @pl.when(at(lambda n: n - 1))
        def _():
            rider.finish(ri, ro, rs)

    return body


def _rider_args(rider):
    if rider is None:
        return [], [], [], [], []
    return list(rider.arrs), [ANY_SPEC] * rider.n, [ANY_SPEC] * rider.n, list(rider.out_shape), list(rider.scratch)


MATMUL_TK_MAX = 4096


def _tiles(mode, m, n, k):
    tm, tn = (512, 1024) if mode == 'nt' else (1024, 512)
    tk = k
    while tk > MATMUL_TK_MAX or k % tk or tk % LANES:
        tk -= LANES
    return _pick(m, tm), _pick(n, tn), tk


def _matmul(a, b, mode, name, *, extras=(), epi=None, out_dtypes=(F32,), out_t=None, tiles=None, rider=None,
            scatter=None):
    if mode == 'nn':
        (m, k), (k2, n) = a.shape, b.shape
    elif mode == 'nt':
        (m, k), (n, k2) = a.shape, b.shape
    else:
        (k, m), (k2, n) = a.shape, b.shape
    assert k == k2, (a.shape, b.shape, mode)
    if scatter == 'rows':
        tiles = (m // N_DEV, n if n <= 1024 else 512, tiles[2] if tiles else _tiles(mode, m, n, k)[2])
    elif scatter == 'cols':
        tiles = (_pick(m, 1024), n // N_DEV, tiles[2] if tiles else _tiles(mode, m, n, k)[2])
    tm, tn, tk = tiles or _tiles(mode, m, n, k)
    nk = k // tk
    n_ex, n_out = len(extras), len(out_dtypes)
    out_t = out_t or (False,) * n_out

    def finish(acc, ex_refs, o_refs):
        outs = epi(acc, *[r[...] for r in ex_refs]) if epi is not None else (acc,)
        for o_ref, o, tr in zip(o_refs, outs, out_t):
            if scatter:
                o_ref[0, 0] = o.astype(o_ref.dtype)
            else:
                o_ref[...] = (o.T if tr else o).astype(o_ref.dtype)

    def body(*refs):
        a_ref, b_ref = refs[0], refs[1]
        ex_refs = refs[2:2 + n_ex]
        o_refs = refs[2 + n_ex:2 + n_ex + n_out]
        if nk == 1:
            finish(_dot(a_ref[...], b_ref[...], mode), ex_refs, o_refs)
            return
        acc_ref = refs[-1]
        kk = pl.program_id(2)

        @pl.when(kk == 0)
        def _():
            acc_ref[...] = jnp.zeros_like(acc_ref)

        acc_ref[...] += _dot(a_ref[...], b_ref[...], mode)

        @pl.when(kk == nk - 1)
        def _():
            finish(acc_ref[...], ex_refs, o_refs)

    if mode == 'nn':
        a_spec = pl.BlockSpec((tm, tk), lambda i, j, kk: (i, kk))
        b_spec = pl.BlockSpec((tk, tn), lambda i, j, kk: (kk, j))
    elif mode == 'nt':
        a_spec = pl.BlockSpec((tm, tk), lambda i, j, kk: (i, kk))
        b_spec = pl.BlockSpec((tn, tk), lambda i, j, kk: (j, kk))
    else:
        a_spec = pl.BlockSpec((tk, tm), lambda i, j, kk: (kk, i))
        b_spec = pl.BlockSpec((tk, tn), lambda i, j, kk: (kk, j))
    mn_spec = pl.BlockSpec((tm, tn), lambda i, j, kk: (i, j))
    nm_spec = pl.BlockSpec((tn, tm), lambda i, j, kk: (j, i))
    grid = (m // tm, n // tn, nk)
    r_arrs, r_in, r_out, r_shape, r_scratch = _rider_args(rider)
    o_specs = [nm_spec if tr else mn_spec for tr in out_t]
    o_shapes = [_sds((n, m) if tr else (m, n), dt) for dt, tr in zip(out_dtypes, out_t)]
    if scatter == 'rows':
        o_specs = [pl.BlockSpec((1, 1, tm, tn), lambda i, j, kk: (i % 2, i // 2, 0, j))]
        o_shapes = [_sds((2, 4, tm, n), out_dtypes[0])]
    elif scatter == 'cols':
        o_specs = [pl.BlockSpec((1, 1, tm, tn), lambda i, j, kk: (j % 2, j // 2, i, 0))]
        o_shapes = [_sds((2, 4, m, tn), out_dtypes[0])]
    outs = pl.pallas_call(
        _with_rider(body, 2 + n_ex, n_out, int(nk > 1), rider, grid), grid=grid,
        in_specs=[a_spec, b_spec] + [mn_spec] * n_ex + r_in,
        out_specs=o_specs + r_out,
        out_shape=o_shapes + r_shape,
        scratch_shapes=([pltpu.VMEM((tm, tn), F32)] if nk > 1 else []) + r_scratch,
        compiler_params=_cparams(("arbitrary",) * 3 if rider is not None else ("parallel", "parallel", "arbitrary")),
        name=name,
    )(a, b, *extras, *r_arrs)
    res = outs[0] if n_out == 1 else outs[:n_out]
    return (res, outs[n_out:]) if rider is not None else res


def _epi_add(acc, r):
    return (acc + r,)


PIECE_TK = 1024


def _matmul_nt_pieces(pieces, b, name, *, extras=(), epi=None, out_dtypes=(F32,), rider=None):
    m, n = pieces[0].shape[0], b.shape[0]
    tm, tn, tk = _pick(m, 1024), _pick(n, 1024), PIECE_TK
    steps = [pc.shape[1] // tk for pc in pieces]
    starts = [sum(steps[:i]) for i in range(len(pieces))]
    nk = sum(steps)
    assert b.shape[1] == nk * tk and all(pc.shape[1] % tk == 0 for pc in pieces)
    n_pc, n_ex, n_out = len(pieces), len(extras), len(out_dtypes)

    def body(*refs):
        pc_refs, b_ref = refs[:n_pc], refs[n_pc]
        ex_refs = refs[n_pc + 1:n_pc + 1 + n_ex]
        o_refs = refs[n_pc + 1 + n_ex:n_pc + 1 + n_ex + n_out]
        acc_ref = refs[-1]
        kk = pl.program_id(2)

        @pl.when(kk == 0)
        def _():
            acc_ref[...] = jnp.zeros_like(acc_ref)

        for pc_ref, st, ns in zip(pc_refs, starts, steps):
            @pl.when((kk >= st) & (kk < st + ns))
            def _(pc_ref=pc_ref):
                acc_ref[...] += _dot(pc_ref[...], b_ref[...], 'nt')

        @pl.when(kk == nk - 1)
        def _():
            acc = acc_ref[...]
            outs = epi(acc, *[r[...] for r in ex_refs]) if epi is not None else (acc,)
            for o_ref, o in zip(o_refs, outs):
                o_ref[...] = o.astype(o_ref.dtype)

    pc_specs = [pl.BlockSpec((tm, tk), lambda i, j, kk, st=st, ns=ns: (i, jnp.clip(kk - st, 0, ns - 1)))
                for st, ns in zip(starts, steps)]
    mn_spec = pl.BlockSpec((tm, tn), lambda i, j, kk: (i, j))
    grid = (m // tm, n // tn, nk)
    r_arrs, r_in, r_out, r_shape, r_scratch = _rider_args(rider)
    outs = pl.pallas_call(
        _with_rider(body, n_pc + 1 + n_ex, n_out, 1, rider, grid), grid=grid,
        in_specs=pc_specs + [pl.BlockSpec((tn, tk), lambda i, j, kk: (j, kk))] + [mn_spec] * n_ex + r_in,
        out_specs=[mn_spec] * n_out + r_out, out_shape=[_sds((m, n), dt) for dt in out_dtypes] + r_shape,
        scratch_shapes=[pltpu.VMEM((tm, tn), F32)] + r_scratch,
        compiler_params=_cparams(("arbitrary",) * 3 if rider is not None else ("parallel", "parallel", "arbitrary")),
        name=name,
    )(*pieces, b, *extras, *r_arrs)
    res = outs[0] if n_out == 1 else outs[:n_out]
    return (res, outs[n_out:]) if rider is not None else res


def _rmsnorm_fn(x, w):
    return _rms(x) * w


def _rmsnorm(x, w, name):
    s, d = x.shape
    t = _pick(s, 512)

    def body(x_ref, w_ref, o_ref, ot_ref):
        y = _rmsnorm_fn(x_ref[...], w_ref[...])
        o_ref[...] = y.astype(o_ref.dtype)
        ot_ref[...] = y.T.astype(ot_ref.dtype)

    return pl.pallas_call(
        body, grid=(s // t,),
        in_specs=[pl.BlockSpec((t, d), lambda i: (i, 0)), _full((1, d))],
        out_specs=[pl.BlockSpec((t, d), lambda i: (i, 0)), pl.BlockSpec((d, t), lambda i: (0, i))],
        out_shape=[_sds((s, d), MXU_DTYPE), _sds((d, s), MXU_DTYPE)],
        compiler_params=_cparams(("parallel",)), name=name,
    )(x, w.reshape(1, d))


def _rmsnorm_bwd(x, w, du, dres, name):
    s, d = x.shape
    t = _pick(s, 512)
    has_res = dres is not None

    def body(*refs):
        if has_res:
            x_ref, w_ref, du_ref, dres_ref, dx_ref, dxb_ref, dw_ref = refs
        else:
            x_ref, w_ref, du_ref, dx_ref, dxb_ref, dw_ref = refs
        _, vjp = jax.vjp(_rmsnorm_fn, x_ref[...], w_ref[...])
        dx, dw = vjp(du_ref[...])
        dx = dx + dres_ref[...] if has_res else dx
        dx_ref[...] = dx
        dxb_ref[...] = dx.astype(dxb_ref.dtype)

        @pl.when(pl.program_id(0) == 0)
        def _():
            dw_ref[...] = jnp.zeros_like(dw_ref)

        dw_ref[...] += dw

    row = pl.BlockSpec((t, d), lambda i: (i, 0))
    return pl.pallas_call(
        body, grid=(s // t,),
        in_specs=[row, _full((1, d)), row] + ([row] if has_res else []),
        out_specs=[row, row, _full((1, d))],
        out_shape=[_sds((s, d), F32), _sds((s, d), MXU_DTYPE), _sds((1, d), F32)],
        compiler_params=_cparams(("arbitrary",)), name=name,
    )(x, w.reshape(1, d), du, *([dres] if has_res else []))


CONV_CW = 2048
CONV_HALO = 16


def _shifted(cat):
    return [cat] + [pltpu.roll(cat, sft, axis=0) for sft in (1, 2, 3)]


def _conv_taps(shifted, w, n_rows, off):
    acc = shifted[0][off:off + n_rows, :] * w[3:4, :]
    for sft in (1, 2, 3):
        acc = acc + shifted[sft][off:off + n_rows, :] * w[3 - sft:4 - sft, :]
    return acc


def _conv_fwd(proj, conv_w, conv_b, name):
    s = proj.shape[0]
    tr = _pick(s, 512)
    hb = tr // CONV_HALO
    col0 = COL_XBC // CONV_CW

    def body(prev_ref, x_ref, w_ref, b_ref, o_ref):
        i = pl.program_id(1)
        prev = jnp.where(i == 0, 0.0, prev_ref[...].astype(F32))
        cat = jnp.concatenate([prev, x_ref[...].astype(F32)], axis=0)
        o_ref[...] = _silu(_conv_taps(_shifted(cat), w_ref[...], tr, CONV_HALO) + b_ref[...])

    return pl.pallas_call(
        body, grid=(SSM_CONV_DIM // CONV_CW, s // tr),
        in_specs=[pl.BlockSpec((CONV_HALO, CONV_CW), lambda j, i: (jnp.maximum(i * hb - 1, 0), j + col0)),
                  pl.BlockSpec((tr, CONV_CW), lambda j, i: (i, j + col0)),
                  pl.BlockSpec((SSM_CONV, CONV_CW), lambda j, i: (0, j)),
                  pl.BlockSpec((1, CONV_CW), lambda j, i: (0, j))],
        out_specs=pl.BlockSpec((tr, CONV_CW), lambda j, i: (i, j)),
        out_shape=_sds((s, SSM_CONV_DIM), F32),
        compiler_params=_cparams(("parallel", "parallel")), name=name,
    )(proj, proj, conv_w, conv_b.reshape(1, SSM_CONV_DIM))


def _conv_bwd(proj, conv_w, conv_b, dact, name):
    s = proj.shape[0]
    tr = _pick(s, 512)
    hb = tr // CONV_HALO
    nb = s // CONV_HALO
    nt = s // tr
    col0 = COL_XBC // CONV_CW
    h = CONV_HALO

    def body(prev_ref, x_ref, next_ref, w_ref, b_ref, da_ref, dan_ref, dx_ref, dw_ref, db_ref):
        i = pl.program_id(1)
        w = w_ref[...]
        prev = jnp.where(i == 0, 0.0, prev_ref[...].astype(F32))
        cat = jnp.concatenate([prev, x_ref[...].astype(F32), next_ref[...].astype(F32)], axis=0)
        shifted = _shifted(cat)
        pre = _conv_taps(shifted, w, tr + h, h) + b_ref[...]
        dact_n = jnp.where(i == nt - 1, 0.0, dan_ref[...])
        dact_ext = jnp.concatenate([da_ref[...], dact_n], axis=0)
        sg = jax.nn.sigmoid(pre)
        dpre = dact_ext * (sg * (1.0 + pre * (1.0 - sg)))
        dx = dpre[:tr, :] * w[3:4, :]
        for sft in (1, 2, 3):
            dx = dx + pltpu.roll(dpre, tr + h - sft, axis=0)[:tr, :] * w[3 - sft:4 - sft, :]
        dx_ref[...] = dx.astype(dx_ref.dtype)

        @pl.when(i == 0)
        def _():
            dw_ref[...] = jnp.zeros_like(dw_ref)
            db_ref[...] = jnp.zeros_like(db_ref)

        dp = dpre[:tr, :]
        db_ref[...] += jnp.sum(dp, axis=0, keepdims=True)
        for r, sft in enumerate((3, 2, 1, 0)):
            dw_ref[r:r + 1, :] += jnp.sum(dp * shifted[sft][h:h + tr, :], axis=0, keepdims=True)

    return pl.pallas_call(
        body, grid=(SSM_CONV_DIM // CONV_CW, nt),
        in_specs=[pl.BlockSpec((h, CONV_CW), lambda j, i: (jnp.maximum(i * hb - 1, 0), j + col0)),
                  pl.BlockSpec((tr, CONV_CW), lambda j, i: (i, j + col0)),
                  pl.BlockSpec((h, CONV_CW), lambda j, i: (jnp.minimum((i + 1) * hb, nb - 1), j + col0)),
                  pl.BlockSpec((SSM_CONV, CONV_CW), lambda j, i: (0, j)),
                  pl.BlockSpec((1, CONV_CW), lambda j, i: (0, j)),
                  pl.BlockSpec((tr, CONV_CW), lambda j, i: (i, j)),
                  pl.BlockSpec((h, CONV_CW), lambda j, i: (jnp.minimum((i + 1) * hb, nb - 1), j))],
        out_specs=[pl.BlockSpec((tr, CONV_CW), lambda j, i: (i, j)),
                   pl.BlockSpec((SSM_CONV, CONV_CW), lambda j, i: (0, j)),
                   pl.BlockSpec((1, CONV_CW), lambda j, i: (0, j))],
        out_shape=[_sds((s, SSM_CONV_DIM), MXU_DTYPE), _sds((SSM_CONV, SSM_CONV_DIM), F32), _sds((1, SSM_CONV_DIM), F32)],
        compiler_params=_cparams(("parallel", "arbitrary")), name=name,
    )(proj, proj, proj, conv_w, conv_b.reshape(1, SSM_CONV_DIM), dact, dact)


def _scan_tables():
    idx = np.arange(CHUNK, dtype=np.float32)
    lg = np.log1p(-(2.0 ** (-5.0 - np.arange(RET_HEADS, dtype=np.float32)))).astype(np.float32)
    rel = np.abs(idx[:, None] - idx[None, :])
    r_intra = np.exp(lg[:, None, None] * rel).astype(np.float32)
    qd = np.exp(lg[None, :] * (idx[:, None] + 1.0)).astype(np.float32)
    kd = np.exp(lg[None, :] * (CHUNK - 1.0 - idx[:, None])).astype(np.float32)
    gam = [float(v) for v in np.exp(lg * CHUNK).astype(np.float32)]
    qd_e = np.repeat(qd, RET_QK_DIM, axis=1)
    kd_e = np.repeat(kd, RET_QK_DIM, axis=1)
    e = np.zeros((DT_PAD, SSM_INNER), np.float32)
    for hh in range(SSM_HEADS):
        e[hh, hh * SSM_HEAD_DIM:(hh + 1) * SSM_HEAD_DIM] = 1.0
    tri = np.tril(np.ones((CHUNK, CHUNK), np.float32))
    eye2 = np.concatenate([np.eye(CHUNK, dtype=np.float32)] * 2, axis=1)
    bdm = np.kron(np.eye(2, dtype=np.float32), np.ones((CHUNK, CHUNK), np.float32))
    last = np.zeros((CHUNK, LANES), np.float32)
    last[CHUNK - 1, :] = 1.0
    f32c = [jnp.asarray(c) for c in (r_intra, qd_e, kd_e, eye2, bdm, last)]
    sel = [jnp.asarray(c, jnp.bfloat16) for c in (e, e.T.copy(), tri, tri.T.copy())]
    return f32c + sel, gam


def _rope(t, cos2, sin2):
    return t * cos2 + pltpu.roll(t, RET_QK_DIM // 2, axis=1) * sin2


def _rope_t(d, cos2, sin2):
    return d * cos2 + pltpu.roll(d * sin2, RET_QK_DIM // 2, axis=1)


def _ret_step(q, k, v, st, r_intra, qd, kd, gamma):
    k = k * (RET_QK_DIM ** -0.5)
    sc = _mm(q, k, 'nt') * r_intra
    y = _mm(sc, v, 'nn') + _mm(q * qd, st, 'nn')
    st_new = st * gamma + _mm(k * kd, v, 'tn')
    return y, st_new


def _ssd_heads(dtraw, dtb, a_c, tri):
    dt = _softplus(dtraw + dtb)
    return dt, _dot_sel(dt * a_c, tri, left=True)


def _ssd_group(dte0, dte1, cum0, cum1, xs0, xs1, bm, cm, ht0, ht1, eye2, bdm, last):
    cbp = _mm(cm, jnp.concatenate([bm, bm], axis=0), 'nt')
    outs = []
    for dte, cum, xs, ht in ((dte0, cum0, xs0, ht0), (dte1, cum1, xs1, ht1)):
        r = jnp.sum(cum * eye2, axis=0, keepdims=True)
        dlt = cum - r
        seg = jnp.exp(jnp.where(dlt > 0.0, -dlt, dlt))
        xdt = xs * dte
        bd = jnp.concatenate([xdt, xdt], axis=0) * bdm
        clast = jnp.sum(cum * last, axis=0, keepdims=True)
        y = _mm(cbp * seg, bd, 'nn') + jnp.exp(cum) * _mm(cm, ht, 'nn')
        ht_new = jnp.exp(clast) * ht + _mm(bm, xdt * jnp.exp(clast - cum), 'tn')
        outs += [y, ht_new]
    return tuple(outs)


SCAN_SUB = 2


def _scan_in_specs(nb, rev, rows):
    ch = (lambda c: nb - 1 - c) if rev else (lambda c: c)
    col = lambda w, blk: pl.BlockSpec((rows, w), lambda c: (ch(c), blk))
    return [col(RET_QK, COL_Q // RET_QK), col(RET_QK, COL_K // RET_QK), col(RET_V, COL_V // RET_V),
            col(SSM_INNER, 0), col(SSM_BC, 2), col(SSM_BC, 3),
            col(DT_PAD, 0), col(LANES, 0), col(LANES, 0)]


def _const_specs(consts):
    return [_full(c.shape) for c in consts]


def _tile(t):
    return slice(t * LANES, (t + 1) * LANES)


def _scan_sub(s):
    return SCAN_SUB if (s // CHUNK) % SCAN_SUB == 0 else 1


def _scan_fwd(proj, xbc, dtraw, cos2, sin2, a_c, dtb, name, rider=None):
    s = proj.shape[0]
    nc = s // CHUNK
    sub = _scan_sub(s)
    nb, rows = nc // sub, CHUNK * sub
    consts, gam = _scan_tables()
    r_arrs, r_in, r_out, r_shape, r_scratch = _rider_args(rider)

    def body(q_ref, k_ref, v_ref, xs_ref, bm_ref, cm_ref, dt_ref, cos_ref, sin_ref, ac_ref, dtb_ref,
             ri_ref, qd_ref, kd_ref, eye_ref, bdm_ref, last_ref, e_ref, et_ref, tri_ref, trit_ref,
             yr_ref, ys_ref, sh_ref, hh_ref, st_sc, ht_sc):
        @pl.when(pl.program_id(0) == 0)
        def _():
            st_sc[...] = jnp.zeros_like(st_sc)
            ht_sc[...] = jnp.zeros_like(ht_sc)

        eye2, bdm, last = eye_ref[...], bdm_ref[...], last_ref[...]
        st = [st_sc[h * RET_QK_DIM:(h + 1) * RET_QK_DIM, :] for h in range(RET_HEADS)]
        ht = [ht_sc[:, _tile(t)] for t in range(N_LTILE)]
        for i in range(sub):
            r = slice(i * CHUNK, (i + 1) * CHUNK)
            cos2, sin2 = cos_ref[r, :], sin_ref[r, :]
            for h in range(RET_HEADS):
                ql = slice(h * RET_QK_DIM, (h + 1) * RET_QK_DIM)
                vl = slice(h * RET_V_DIM, (h + 1) * RET_V_DIM)
                sh_ref[i, ql, :] = st[h]
                y, st[h] = _ret_step(_rope(q_ref[r, ql].astype(F32), cos2, sin2), _rope(k_ref[r, ql].astype(F32), cos2, sin2),
                                     v_ref[r, vl].astype(F32), st[h], ri_ref[h], qd_ref[:, ql], kd_ref[:, ql], gam[h])
                yr_ref[r, vl] = y.astype(yr_ref.dtype)
            dt, cum_c = _ssd_heads(dt_ref[r, :], dtb_ref[...], ac_ref[...], tri_ref[...])
            both = jnp.concatenate([dt, cum_c], axis=0)
            for g in range(SSM_GROUPS):
                t0, t1 = 2 * g, 2 * g + 1
                hh_ref[i, :, _tile(t0)] = ht[t0]
                hh_ref[i, :, _tile(t1)] = ht[t1]
                e0, e1 = _dot_sel(both, e_ref[:, _tile(t0)]), _dot_sel(both, e_ref[:, _tile(t1)])
                y0, ht[t0], y1, ht[t1] = _ssd_group(e0[:CHUNK], e1[:CHUNK], e0[CHUNK:], e1[CHUNK:],
                                                    xs_ref[r, _tile(t0)], xs_ref[r, _tile(t1)], bm_ref[r, _tile(g)],
                                                    cm_ref[r, _tile(g)], ht[t0], ht[t1], eye2, bdm, last)
                ys_ref[r, _tile(t0)] = y0.astype(ys_ref.dtype)
                ys_ref[r, _tile(t1)] = y1.astype(ys_ref.dtype)
        for h in range(RET_HEADS):
            st_sc[h * RET_QK_DIM:(h + 1) * RET_QK_DIM, :] = st[h]
        for t in range(N_LTILE):
            ht_sc[:, _tile(t)] = ht[t]

    in_specs = _scan_in_specs(nb, False, rows) + [_full((1, DT_PAD)), _full((1, DT_PAD))] + _const_specs(consts)
    return pl.pallas_call(
        _with_rider(body, len(in_specs), 4, 2, rider, (nb,)), grid=(nb,),
        in_specs=in_specs + r_in,
        out_specs=[pl.BlockSpec((rows, RET_V), lambda c: (c, 0)),
                   pl.BlockSpec((rows, SSM_INNER), lambda c: (c, 0)),
                   pl.BlockSpec((sub, RET_QK, RET_V_DIM), lambda c: (c, 0, 0)),
                   pl.BlockSpec((sub, SSM_STATE, SSM_INNER), lambda c: (c, 0, 0))] + r_out,
        out_shape=[_sds((s, RET_V), MXU_DTYPE), _sds((s, SSM_INNER), MXU_DTYPE),
                   _sds((nc, RET_QK, RET_V_DIM), F32), _sds((nc, SSM_STATE, SSM_INNER), F32)] + r_shape,
        scratch_shapes=[pltpu.VMEM((RET_QK, RET_V_DIM), F32), pltpu.VMEM((SSM_STATE, SSM_INNER), F32)] + r_scratch,
        compiler_params=_cparams(("arbitrary",)), name=name,
    )(proj, proj, proj, xbc, xbc, xbc, dtraw, cos2, sin2, a_c, dtb, *consts, *r_arrs)


def _scan_bwd(proj, xbc, dtraw, cos2, sin2, a_c, dtb, s_hist, h_hist, dyr, dys, dxs_skip, name, rider=None):
    s = proj.shape[0]
    nc = s // CHUNK
    sub = _scan_sub(s)
    nb, rows = nc // sub, CHUNK * sub
    consts, gam = _scan_tables()
    rv = lambda c: nb - 1 - c
    r_arrs, r_in, r_out, r_shape, r_scratch = _rider_args(rider)

    def body(q_ref, k_ref, v_ref, xs_ref, bm_ref, cm_ref, dt_ref, cos_ref, sin_ref, ac_ref, dtb_ref,
             ri_ref, qd_ref, kd_ref, eye_ref, bdm_ref, last_ref, e_ref, et_ref, tri_ref, trit_ref,
             sh_ref, hh_ref, dyr_ref, dys_ref, dsk_ref,
             dqk_ref, dv_ref, dxbc_ref, ddt_ref, dac_ref, ddtb_ref, dst_sc, dht_sc):
        @pl.when(pl.program_id(0) == 0)
        def _():
            dst_sc[...] = jnp.zeros_like(dst_sc)
            dht_sc[...] = jnp.zeros_like(dht_sc)
            dac_ref[...] = jnp.zeros_like(dac_ref)
            ddtb_ref[...] = jnp.zeros_like(ddtb_ref)

        dtb_v, a_c, tri = dtb_ref[...], ac_ref[...], tri_ref[...]
        eye2, bdm, last = eye_ref[...], bdm_ref[...], last_ref[...]
        group = functools.partial(_ssd_group, eye2=eye2, bdm=bdm, last=last)
        dst = [dst_sc[h * RET_QK_DIM:(h + 1) * RET_QK_DIM, :] for h in range(RET_HEADS)]
        dht = [dht_sc[:, _tile(t)] for t in range(N_LTILE)]
        ddtb = jnp.zeros((1, DT_PAD), F32)
        dac = jnp.zeros((1, DT_PAD), F32)
        for i in reversed(range(sub)):
            r = slice(i * CHUNK, (i + 1) * CHUNK)
            cos2, sin2 = cos_ref[r, :], sin_ref[r, :]
            for h in range(RET_HEADS):
                ql = slice(h * RET_QK_DIM, (h + 1) * RET_QK_DIM)
                vl = slice(h * RET_V_DIM, (h + 1) * RET_V_DIM)
                step = functools.partial(_ret_step, r_intra=ri_ref[h], qd=qd_ref[:, ql], kd=kd_ref[:, ql], gamma=gam[h])
                _, vjp = jax.vjp(step, _rope(q_ref[r, ql].astype(F32), cos2, sin2), _rope(k_ref[r, ql].astype(F32), cos2, sin2),
                                 v_ref[r, vl].astype(F32), sh_ref[i, ql, :])
                dq, dk, dv, dst[h] = vjp((dyr_ref[r, vl].astype(F32), dst[h]))
                dqk_ref[r, ql] = _rope_t(dq, cos2, sin2).astype(dqk_ref.dtype)
                dqk_ref[r, slice(RET_QK + ql.start, RET_QK + ql.stop)] = _rope_t(dk, cos2, sin2).astype(dqk_ref.dtype)
                dv_ref[r, vl] = dv.astype(dv_ref.dtype)
            dtraw_v = dt_ref[r, :]
            dt, cum_c = _ssd_heads(dtraw_v, dtb_v, a_c, tri)
            both = jnp.concatenate([dt, cum_c], axis=0)
            d_both = jnp.zeros((2 * CHUNK, LANES), F32)
            for g in range(SSM_GROUPS):
                t0, t1 = 2 * g, 2 * g + 1
                e0, e1 = _dot_sel(both, e_ref[:, _tile(t0)]), _dot_sel(both, e_ref[:, _tile(t1)])
                _, vjp = jax.vjp(group, e0[:CHUNK], e1[:CHUNK], e0[CHUNK:], e1[CHUNK:],
                                 xs_ref[r, _tile(t0)], xs_ref[r, _tile(t1)], bm_ref[r, _tile(g)], cm_ref[r, _tile(g)],
                                 hh_ref[i, :, _tile(t0)], hh_ref[i, :, _tile(t1)])
                (d_dte0, d_dte1, d_cum0, d_cum1, d_xs0, d_xs1, d_bm, d_cm, dht[t0], dht[t1]) = vjp(
                    (dys_ref[r, _tile(t0)].astype(F32), dht[t0], dys_ref[r, _tile(t1)].astype(F32), dht[t1]))
                d_both = d_both + _dot_sel(jnp.concatenate([d_dte0, d_cum0], axis=0), et_ref[_tile(t0), :])
                d_both = d_both + _dot_sel(jnp.concatenate([d_dte1, d_cum1], axis=0), et_ref[_tile(t1), :])
                dxbc_ref[r, _tile(t0)] = d_xs0 + dsk_ref[r, _tile(t0)].astype(F32)
                dxbc_ref[r, _tile(t1)] = d_xs1 + dsk_ref[r, _tile(t1)].astype(F32)
                dxbc_ref[r, _tile(N_LTILE + g)] = d_bm
                dxbc_ref[r, _tile(N_LTILE + SSM_GROUPS + g)] = d_cm
            d_da = _dot_sel(d_both[CHUNK:], trit_ref[...], left=True)
            d_dt = d_both[:CHUNK] + d_da * a_c
            d_pre = d_dt * jax.nn.sigmoid(dtraw_v + dtb_v)
            ddt_ref[r, :] = d_pre
            ddtb = ddtb + jnp.sum(d_pre, axis=0, keepdims=True)
            dac = dac + jnp.sum(d_da * dt, axis=0, keepdims=True)
        ddtb_ref[...] += ddtb
        dac_ref[...] += dac
        for h in range(RET_HEADS):
            dst_sc[h * RET_QK_DIM:(h + 1) * RET_QK_DIM, :] = dst[h]
        for t in range(N_LTILE):
            dht_sc[:, _tile(t)] = dht[t]

    in_specs = (_scan_in_specs(nb, True, rows) + [_full((1, DT_PAD)), _full((1, DT_PAD))] + _const_specs(consts)
                + [pl.BlockSpec((sub, RET_QK, RET_V_DIM), lambda c: (rv(c), 0, 0)),
                   pl.BlockSpec((sub, SSM_STATE, SSM_INNER), lambda c: (rv(c), 0, 0)),
                   pl.BlockSpec((rows, RET_V), lambda c: (rv(c), 0)),
                   pl.BlockSpec((rows, SSM_INNER), lambda c: (rv(c), 0)),
                   pl.BlockSpec((rows, SSM_INNER), lambda c: (rv(c), 0))])
    return pl.pallas_call(
        _with_rider(body, len(in_specs), 6, 2, rider, (nb,)), grid=(nb,),
        in_specs=in_specs + r_in,
        out_specs=[pl.BlockSpec((rows, 2 * RET_QK), lambda c: (rv(c), 0)),
                   pl.BlockSpec((rows, RET_V), lambda c: (rv(c), 0)),
                   pl.BlockSpec((rows, SSM_CONV_DIM), lambda c: (rv(c), 0)),
                   pl.BlockSpec((rows, DT_PAD), lambda c: (rv(c), 0)),
                   _full((1, DT_PAD)), _full((1, DT_PAD))] + r_out,
        out_shape=[_sds((s, 2 * RET_QK), MXU_DTYPE), _sds((s, RET_V), MXU_DTYPE),
                   _sds((s, SSM_CONV_DIM), F32), _sds((s, DT_PAD), F32),
                   _sds((1, DT_PAD), F32), _sds((1, DT_PAD), F32)] + r_shape,
        scratch_shapes=[pltpu.VMEM((RET_QK, RET_V_DIM), F32), pltpu.VMEM((SSM_STATE, SSM_INNER), F32)] + r_scratch,
        compiler_params=_cparams(("arbitrary",)), name=name,
    )(proj, proj, proj, xbc, xbc, xbc, dtraw, cos2, sin2, a_c, dtb, *consts, s_hist, h_hist, dyr, dys, dxs_skip, *r_arrs)


POST_W = 256


def _post_ret(y, g):
    return _rms(y) * _silu(g)


def _post_ssm(y, xs, z, dsk, nw):
    return _rms((y + xs * dsk) * _silu(z)) * nw


def _post_specs(t):
    return [pl.BlockSpec((t, RET_V), lambda i: (i, 0)),
            pl.BlockSpec((t, RET_V), lambda i: (i, COL_G // RET_V)),
            pl.BlockSpec((t, SSM_INNER), lambda i: (i, 0)),
            pl.BlockSpec((t, SSM_INNER), lambda i: (i, 0)),
            pl.BlockSpec((t, SSM_INNER), lambda i: (i, COL_Z // SSM_INNER)),
            _full((1, SSM_INNER)), _full((1, SSM_INNER))]


def _post_fwd(y_ret, proj, y_ssm, xbc, dsk_e, ssm_norm, name):
    s = y_ret.shape[0]
    t = _pick(s, 256)

    def body(yr_ref, g_ref, ys_ref, xs_ref, z_ref, dsk_ref, nw_ref, or_ref, os_ref, ort_ref, ost_ref):
        for h in range(RET_V // POST_W):
            sl = slice(h * POST_W, (h + 1) * POST_W)
            o = _post_ret(yr_ref[:, sl].astype(F32), g_ref[:, sl].astype(F32))
            or_ref[:, sl] = o.astype(or_ref.dtype)
            ort_ref[sl, :] = o.T.astype(ort_ref.dtype)
        for g in range(SSM_INNER // POST_W):
            sl = slice(g * POST_W, (g + 1) * POST_W)
            o = _post_ssm(ys_ref[:, sl].astype(F32), xs_ref[:, sl], z_ref[:, sl].astype(F32), dsk_ref[:, sl], nw_ref[:, sl])
            os_ref[:, sl] = o.astype(os_ref.dtype)
            ost_ref[sl, :] = o.T.astype(ost_ref.dtype)

    return pl.pallas_call(
        body, grid=(s // t,), in_specs=_post_specs(t),
        out_specs=[pl.BlockSpec((t, RET_V), lambda i: (i, 0)), pl.BlockSpec((t, SSM_INNER), lambda i: (i, 0)),
                   pl.BlockSpec((RET_V, t), lambda i: (0, i)), pl.BlockSpec((SSM_INNER, t), lambda i: (0, i))],
        out_shape=[_sds((s, RET_V), MXU_DTYPE), _sds((s, SSM_INNER), MXU_DTYPE),
                   _sds((RET_V, s), MXU_DTYPE), _sds((SSM_INNER, s), MXU_DTYPE)],
        compiler_params=_cparams(("parallel",)), name=name,
    )(y_ret, proj, y_ssm, xbc, proj, dsk_e, ssm_norm.reshape(1, SSM_INNER))


def _post_bwd(y_ret, proj, y_ssm, xbc, dsk_e, ssm_norm, d_or, d_os, name, rider=None):
    s = y_ret.shape[0]
    t = _pick(s, 256)
    r_arrs, r_in, r_out, r_shape, r_scratch = _rider_args(rider)

    def body(yr_ref, g_ref, ys_ref, xs_ref, z_ref, dsk_ref, nw_ref, dor_ref, dos_ref,
             dyr_ref, dg_ref, dys_ref, dxs_ref, dz_ref, ddsk_ref, dnw_ref):
        @pl.when(pl.program_id(0) == 0)
        def _():
            ddsk_ref[...] = jnp.zeros_like(ddsk_ref)
            dnw_ref[...] = jnp.zeros_like(dnw_ref)

        for h in range(RET_V // POST_W):
            sl = slice(h * POST_W, (h + 1) * POST_W)
            _, vjp = jax.vjp(_post_ret, yr_ref[:, sl].astype(F32), g_ref[:, sl].astype(F32))
            dyr, dg = vjp(dor_ref[:, sl].astype(F32))
            dyr_ref[:, sl] = dyr.astype(dyr_ref.dtype)
            dg_ref[:, sl] = dg.astype(dg_ref.dtype)
        for g in range(SSM_INNER // POST_W):
            sl = slice(g * POST_W, (g + 1) * POST_W)
            _, vjp = jax.vjp(_post_ssm, ys_ref[:, sl].astype(F32), xs_ref[:, sl], z_ref[:, sl].astype(F32), dsk_ref[:, sl],
                             nw_ref[:, sl])
            dy, dxs, dz, ddsk, dnw = vjp(dos_ref[:, sl].astype(F32))
            dys_ref[:, sl] = dy.astype(dys_ref.dtype)
            dxs_ref[:, sl] = dxs.astype(dxs_ref.dtype)
            dz_ref[:, sl] = dz.astype(dz_ref.dtype)
            ddsk_ref[:, sl] += ddsk
            dnw_ref[:, sl] += dnw

    rowv = pl.BlockSpec((t, RET_V), lambda i: (i, 0))
    rows = pl.BlockSpec((t, SSM_INNER), lambda i: (i, 0))
    in_specs = _post_specs(t) + [rowv, rows]
    return pl.pallas_call(
        _with_rider(body, len(in_specs), 7, 0, rider, (s // t,)), grid=(s // t,), in_specs=in_specs + r_in,
        out_specs=[rowv, rowv, rows, rows, rows, _full((1, SSM_INNER)), _full((1, SSM_INNER))] + r_out,
        out_shape=[_sds((s, RET_V), MXU_DTYPE), _sds((s, RET_V), MXU_DTYPE), _sds((s, SSM_INNER), MXU_DTYPE),
                   _sds((s, SSM_INNER), MXU_DTYPE), _sds((s, SSM_INNER), MXU_DTYPE),
                   _sds((1, SSM_INNER), F32), _sds((1, SSM_INNER), F32)] + r_shape,
        scratch_shapes=r_scratch,
        compiler_params=_cparams(("arbitrary",)), name=name,
    )(y_ret, proj, y_ssm, xbc, proj, dsk_e, ssm_norm.reshape(1, SSM_INNER), d_or, d_os, *r_arrs)


def _merge_fn(gr, gs, br, bs, yr, ys):
    return jax.nn.sigmoid(gr + br) * yr + jax.nn.sigmoid(gs + bs) * ys


def _merge_specs(t):
    row = pl.BlockSpec((t, D_MODEL), lambda i: (i, 0))
    return [pl.BlockSpec((t, D_MODEL), lambda i: (i, COL_GATES // D_MODEL)),
            pl.BlockSpec((t, D_MODEL), lambda i: (i, COL_GATES // D_MODEL + 1)),
            pl.BlockSpec((1, D_MODEL), lambda i: (0, 0)), pl.BlockSpec((1, D_MODEL), lambda i: (0, 1)), row, row]


def _merge_fwd(proj, b_gate, br_ret, br_ssm, name):
    s = proj.shape[0]
    t = _pick(s, 512)

    def body(gr_ref, gs_ref, br_ref, bs_ref, yr_ref, ys_ref, o_ref, ot_ref):
        o = _merge_fn(gr_ref[...].astype(F32), gs_ref[...].astype(F32), br_ref[...], bs_ref[...], yr_ref[...], ys_ref[...])
        o_ref[...] = o.astype(o_ref.dtype)
        ot_ref[...] = o.T.astype(ot_ref.dtype)

    bg = b_gate.reshape(1, 2 * D_MODEL)
    return pl.pallas_call(
        body, grid=(s // t,), in_specs=_merge_specs(t),
        out_specs=[pl.BlockSpec((t, D_MODEL), lambda i: (i, 0)), pl.BlockSpec((D_MODEL, t), lambda i: (0, i))],
        out_shape=[_sds((s, D_MODEL), MXU_DTYPE), _sds((D_MODEL, s), MXU_DTYPE)],
        compiler_params=_cparams(("parallel",)), name=name,
    )(proj, proj, bg, bg, br_ret, br_ssm)


def _merge_bwd(proj, b_gate, br_ret, br_ssm, dm, name):
    s = proj.shape[0]
    t = _pick(s, 512)

    def body(gr_ref, gs_ref, br_ref, bs_ref, yr_ref, ys_ref, dm_ref, dgt_ref, db_ref, dyr_ref, dys_ref):
        @pl.when(pl.program_id(0) == 0)
        def _():
            db_ref[...] = jnp.zeros_like(db_ref)

        _, vjp = jax.vjp(_merge_fn, gr_ref[...].astype(F32), gs_ref[...].astype(F32), br_ref[...], bs_ref[...],
                         yr_ref[...], ys_ref[...])
        dgr, dgs, dbr, dbs, dyr, dys = vjp(dm_ref[...])
        dgt_ref[:, :D_MODEL] = dgr.astype(dgt_ref.dtype)
        dgt_ref[:, D_MODEL:] = dgs.astype(dgt_ref.dtype)
        db_ref[:, :D_MODEL] += dbr
        db_ref[:, D_MODEL:] += dbs
        dyr_ref[...] = dyr.astype(dyr_ref.dtype)
        dys_ref[...] = dys.astype(dys_ref.dtype)

    bg = b_gate.reshape(1, 2 * D_MODEL)
    row = pl.BlockSpec((t, D_MODEL), lambda i: (i, 0))
    return pl.pallas_call(
        body, grid=(s // t,), in_specs=_merge_specs(t) + [row],
        out_specs=[pl.BlockSpec((t, 2 * D_MODEL), lambda i: (i, 0)), _full((1, 2 * D_MODEL)), row, row],
        out_shape=[_sds((s, 2 * D_MODEL), MXU_DTYPE), _sds((1, 2 * D_MODEL), F32),
                   _sds((s, D_MODEL), MXU_DTYPE), _sds((s, D_MODEL), MXU_DTYPE)],
        compiler_params=_cparams(("arbitrary",)), name=name,
    )(proj, proj, bg, bg, br_ret, br_ssm, dm)


def _attn_head(q, k, v):
    sc = _mm(q, k, 'nt') * (XA_HEAD_DIM ** -0.5)
    e = jnp.exp(sc - lax.stop_gradient(jnp.max(sc, axis=-1, keepdims=True)))
    p = e / jnp.sum(e, axis=-1, keepdims=True)
    return _mm(p, v, 'nn')


def _attn_fwd(q, kv, name):
    s = q.shape[0]
    m = kv.shape[0]
    t = _pick(s, 512)

    def body(q_ref, kv_ref, o_ref, ot_ref):
        for h in range(XA_HEADS):
            sl = slice(h * XA_HEAD_DIM, (h + 1) * XA_HEAD_DIM)
            vl = slice(D_MODEL + h * XA_HEAD_DIM, D_MODEL + (h + 1) * XA_HEAD_DIM)
            o = _attn_head(q_ref[:, sl], kv_ref[:, sl], kv_ref[:, vl])
            o_ref[:, sl] = o.astype(o_ref.dtype)
            ot_ref[sl, :] = o.T.astype(ot_ref.dtype)

    return pl.pallas_call(
        body, grid=(s // t,),
        in_specs=[pl.BlockSpec((t, D_MODEL), lambda i: (i, 0)), _full((m, 2 * D_MODEL))],
        out_specs=[pl.BlockSpec((t, D_MODEL), lambda i: (i, 0)), pl.BlockSpec((D_MODEL, t), lambda i: (0, i))],
        out_shape=[_sds((s, D_MODEL), MXU_DTYPE), _sds((D_MODEL, s), MXU_DTYPE)],
        compiler_params=_cparams(("parallel",)), name=name,
    )(q, kv)


def _attn_bwd(q, kv, d_o, name):
    s = q.shape[0]
    m = kv.shape[0]
    t = _pick(s, 512)

    def body(q_ref, kv_ref, do_ref, dq_ref, dkv_ref):
        @pl.when(pl.program_id(0) == 0)
        def _():
            dkv_ref[...] = jnp.zeros_like(dkv_ref)

        for h in range(XA_HEADS):
            sl = slice(h * XA_HEAD_DIM, (h + 1) * XA_HEAD_DIM)
            vl = slice(D_MODEL + h * XA_HEAD_DIM, D_MODEL + (h + 1) * XA_HEAD_DIM)
            _, vjp = jax.vjp(_attn_head, q_ref[:, sl], kv_ref[:, sl], kv_ref[:, vl])
            dq, dk, dv = vjp(do_ref[:, sl])
            dq_ref[:, sl] = dq.astype(dq_ref.dtype)
            dkv_ref[:, sl] += dk
            dkv_ref[:, vl] += dv

    row = pl.BlockSpec((t, D_MODEL), lambda i: (i, 0))
    return pl.pallas_call(
        body, grid=(s // t,), in_specs=[row, _full((m, 2 * D_MODEL)), row],
        out_specs=[row, _full((m, 2 * D_MODEL))],
        out_shape=[_sds((s, D_MODEL), MXU_DTYPE), _sds((m, 2 * D_MODEL), F32)],
        compiler_params=_cparams(("arbitrary",)), name=name,
    )(q, kv, d_o)


def _loss_head(x, w, target, name):
    s, d = x.shape
    t = _pick(s, 512)

    def body(x_ref, w_ref, t_ref, loss_ref, dx_ref, dxb_ref, dw_ref):
        @pl.when(pl.program_id(0) == 0)
        def _():
            loss_ref[...] = jnp.zeros_like(loss_ref)
            dw_ref[...] = jnp.zeros_like(dw_ref)

        y, vjp = jax.vjp(_rmsnorm_fn, x_ref[...], w_ref[...])
        err = y - t_ref[...]
        loss_ref[...] += 0.5 * jnp.sum(jnp.sum(err * err, axis=-1, keepdims=True), axis=0, keepdims=True) / d
        dx, dw = vjp(err * (1.0 / d))
        dx_ref[...] = dx
        dxb_ref[...] = dx.astype(dxb_ref.dtype)
        dw_ref[...] += dw

    row = pl.BlockSpec((t, d), lambda i: (i, 0))
    return pl.pallas_call(
        body, grid=(s // t,), in_specs=[row, _full((1, d)), row],
        out_specs=[_full((1, LANES)), row, row, _full((1, d))],
        out_shape=[_sds((1, LANES), F32), _sds((s, d), F32), _sds((s, d), MXU_DTYPE), _sds((1, d), F32)],
        compiler_params=_cparams(("arbitrary",)), name=name,
    )(x, w.reshape(1, d), target)


def _epi_sqrelu(acc):
    r = jnp.maximum(acc, 0.0)
    return r * r, r * r


def _epi_sqrelu_bwd(acc, act):
    return (acc * (2.0 * jnp.sqrt(act.astype(F32))),)


def _rope_tables(positions):
    inv_freq = ROPE_THETA ** (-jnp.arange(0, RET_QK_DIM, 2, dtype=F32) / RET_QK_DIM)
    ang = positions.astype(F32)[:, None] * inv_freq
    cos, sin = jnp.cos(ang), jnp.sin(ang)
    return jnp.concatenate([cos, cos], axis=1), jnp.concatenate([-sin, sin], axis=1)


W_IN_ORIG = (('q', 0, 512), ('k', 512, 1024), ('v', 1024, 2048), ('g', 2048, 3072), ('z', 3072, 5120),
             ('xbc', 5120, 9216), ('dt', 9216, 9248), ('gates', 9248, 11296))
W_IN_MAIN_ORDER = ('z', 'xbc', 'gates', 'v', 'g', 'q', 'k')
W_IN_SHARD = IN_DIM // N_DEV


def _shard_segments(lo, hi):
    segs = []
    for j in range(lo // W_IN_SHARD, (hi - 1) // W_IN_SHARD + 1):
        segs.append((j, max(lo, j * W_IN_SHARD) - j * W_IN_SHARD, min(hi, (j + 1) * W_IN_SHARD) - j * W_IN_SHARD))
    return segs


def _w_in_from_shards(g):
    rng = {name: (lo, hi) for name, lo, hi in W_IN_ORIG}
    cols = [g[j][:, a:b] for name in W_IN_MAIN_ORDER for j, a, b in _shard_segments(*rng[name])]
    (j, a, b), = _shard_segments(*rng['dt'])
    return jnp.concatenate(cols, axis=1), jnp.pad(g[j][:, a:b], ((0, 0), (0, DT_PAD - SSM_HEADS)))


def _w_in_grad_blocks(d, d_dt):
    src_of = {'q': ('qk', 0), 'k': ('qk', RET_QK)}
    blocks = []
    for j in range(N_DEV):
        lo_j, hi_j = j * W_IN_SHARD, (j + 1) * W_IN_SHARD
        cols = []
        for name, lo, hi in W_IN_ORIG:
            a, b = max(lo, lo_j), min(hi, hi_j)
            if a >= b:
                continue
            if name == 'dt':
                cols.append(d_dt[:, a - lo:b - lo])
            else:
                key, off = src_of.get(name, (name, 0))
                cols.append(d[key][:, off + a - lo:off + b - lo])
        blocks.append(jnp.concatenate(cols, axis=1))
    return blocks


def _lanes_of_heads(v):
    return jnp.repeat(v, SSM_HEAD_DIM).reshape(1, SSM_INNER)


def _heads_of_lanes(v):
    return v.reshape(SSM_HEADS, SSM_HEAD_DIM).sum(axis=1)


def _layer_fwd(x, mem, cos2, sin2, p, l, later_blocks, rider_proj=None):
    n = lambda s: f"{s}_l{l}"
    sv = {'x0': x}
    u, u_t = _rmsnorm(x, p['norm_mix'], n("norm_mix"))
    s = x.shape[0]
    proj = _matmul(u, p['w_in_main'], 'nn', n("in_proj"), out_dtypes=(MXU_DTYPE,), tiles=(s, 512, D_MODEL), rider=rider_proj)
    if rider_proj is not None:
        proj, next_first = proj
    else:
        next_first = []
    dtraw = _matmul(u, p['w_in_dt'], 'nn', n("in_proj_dt"))
    xbc = _conv_fwd(proj, p['conv_w'], p['conv_b'], n("conv"))
    a_c = jnp.pad(-jnp.exp(p['a_log']), (0, DT_PAD - SSM_HEADS)).reshape(1, DT_PAD)
    dtb = jnp.pad(p['dt_bias'], (0, DT_PAD - SSM_HEADS)).reshape(1, DT_PAD)
    y_ret, y_ssm, s_hist, h_hist, *gathered = _scan_fwd(proj, xbc, dtraw, cos2, sin2, a_c, dtb, n("scan"),
                                                        _AllGather(later_blocks))
    p = {**p, **_full_weights(LATER, gathered)}
    dsk_e = _lanes_of_heads(p['d_skip'])
    o_ret, o_ssm, o_ret_t, o_ssm_t = _post_fwd(y_ret, proj, y_ssm, xbc, dsk_e, p['ssm_norm'], n("post"))
    br_ret = _matmul(o_ret, p['w_br_ret'], 'nn', n("br_ret"))
    br_ssm = _matmul(o_ssm, p['w_br_ssm'], 'nn', n("br_ssm"))
    merged, merged_t = _merge_fwd(proj, p['b_gate'], br_ret, br_ssm, n("merge"))
    x1 = _matmul(merged, p['w_out'], 'nn', n("w_out"), extras=(x,), epi=_epi_add)
    sv.update(u_t=u_t, proj=proj, dtraw=dtraw, xbc=xbc, a_c=a_c, dtb=dtb, y_ret=y_ret, y_ssm=y_ssm, s_hist=s_hist,
              h_hist=h_hist, dsk_e=dsk_e, o_ret_t=o_ret_t, o_ssm_t=o_ssm_t, br_ret=br_ret, br_ssm=br_ssm,
              merged_t=merged_t, x1=x1)
    hq, hq_t = _rmsnorm(x1, p['norm_xa'], n("norm_xa"))
    memn, _ = _rmsnorm(mem, p['norm_mem'], n("norm_mem"))
    q = _matmul(hq, p['xa_wq'], 'nn', n("xa_q"))
    kv = _matmul(memn, p['xa_wkv'], 'nn', n("xa_kv"))
    o, o_t = _attn_fwd(q, kv, n("attn"))
    x2 = _matmul(o, p['xa_wo'], 'nn', n("xa_o"), extras=(x1,), epi=_epi_add)
    sv.update(hq_t=hq_t, memn=memn, q=q, kv=kv, o_t=o_t, x2=x2)
    hm, hm_t = _rmsnorm(x2, p['norm_mlp'], n("norm_mlp"))
    act, act_t = _matmul(hm, p['mlp_w1'], 'nn', n("mlp_1"), epi=_epi_sqrelu, out_dtypes=(MXU_DTYPE, MXU_DTYPE),
                         out_t=(False, True), tiles=(s, 512, D_MODEL))
    x3 = _matmul(act, p['mlp_w2'], 'nn', n("mlp_2"), extras=(x2,), epi=_epi_add, tiles=(_pick(s, 1024), D_MODEL, D_FF))
    sv.update(hm_t=hm_t, act=act, act_t=act_t)
    return x3, sv, p, next_first


def _layer_bwd(dx, dxb, mem, cos2, sin2, p, sv, l, pending, c_idx):
    n = lambda s: f"{s}_bwd_l{l}"
    gd = (MXU_DTYPE,)
    g = {}
    g['mlp_w2'] = _matmul(sv['act_t'], dxb, 'nn', n("mlp_2_dw"), out_dtypes=gd, scatter='rows')
    da = _matmul(dxb, p['mlp_w2'], 'nt', n("mlp_2_dx"), extras=(sv['act'],), epi=_epi_sqrelu_bwd, out_dtypes=(MXU_DTYPE,),
                 tiles=(_pick(dxb.shape[0], 512), D_FF, D_MODEL))
    g['mlp_w1'] = _matmul(sv['hm_t'], da, 'nn', n("mlp_1_dw"), out_dtypes=gd, scatter='cols')
    dhm = _matmul(da, p['mlp_w1'], 'nt', n("mlp_1_dx"))
    dx2, dx2b, g['norm_mlp'] = _rmsnorm_bwd(sv['x2'], p['norm_mlp'], dhm, dx, n("norm_mlp"))
    g['xa_wo'] = _matmul(sv['o_t'], dx2b, 'nn', n("xa_o_dw"), out_dtypes=gd, scatter='rows')
    d_o = _matmul(dx2b, p['xa_wo'], 'nt', n("xa_o_dx"))
    dq, dkv = _attn_bwd(sv['q'], sv['kv'], d_o, n("attn"))
    g['xa_wq'] = _matmul(sv['hq_t'], dq, 'nn', n("xa_q_dw"), out_dtypes=gd, scatter='rows')
    dhq = _matmul(dq, p['xa_wq'], 'nt', n("xa_q_dx"))
    g['xa_wkv'] = _matmul(sv['memn'], dkv, 'tn', n("xa_kv_dw"), out_dtypes=gd, scatter='cols')
    dmemn = _matmul(dkv, p['xa_wkv'], 'nt', n("xa_kv_dx"))
    _, _, g['norm_mem'] = _rmsnorm_bwd(mem, p['norm_mem'], dmemn, None, n("norm_mem"))
    dx1, dx1b, g['norm_xa'] = _rmsnorm_bwd(sv['x1'], p['norm_xa'], dhq, dx2, n("norm_xa"))
    g['w_out'] = _matmul(sv['merged_t'], dx1b, 'nn', n("w_out_dw"), out_dtypes=gd, scatter='rows')
    dmerged = _matmul(dx1b, p['w_out'], 'nt', n("w_out_dx"))
    dgates, g['b_gate'], dbr_ret, dbr_ssm = _merge_bwd(sv['proj'], p['b_gate'], sv['br_ret'], sv['br_ssm'], dmerged, n("merge"))
    g['w_br_ret'] = _matmul(sv['o_ret_t'], dbr_ret, 'nn', n("br_ret_dw"), out_dtypes=gd, scatter='rows')
    g['w_br_ssm'] = _matmul(sv['o_ssm_t'], dbr_ssm, 'nn', n("br_ssm_dw"), out_dtypes=gd, scatter='rows')
    d_or = _matmul(dbr_ret, p['w_br_ret'], 'nt', n("br_ret_dx"), out_dtypes=(MXU_DTYPE,))
    d_os = _matmul(dbr_ssm, p['w_br_ssm'], 'nt', n("br_ssm_dx"), out_dtypes=(MXU_DTYPE,))
    later_by_core = [_grad_scatter(k, g[k]) for k in LATER]
    dyr, dg, dys, dxs_skip, dz, ddsk_e, g['ssm_norm'], *later_sib = _post_bwd(
        sv['y_ret'], sv['proj'], sv['y_ssm'], sv['xbc'], sv['dsk_e'], p['ssm_norm'], d_or, d_os, n("post"),
        _ExchangeCores(later_by_core))
    rider = _ExchangeChips(list(pending) + _core_sums(LATER, later_by_core, later_sib, c_idx, l))
    g['d_skip'] = _heads_of_lanes(ddsk_e)
    dqk_r, dv_r, dxbc_act, ddtraw, dac, ddtb, *delivered = _scan_bwd(
        sv['proj'], sv['xbc'], sv['dtraw'], cos2, sin2, sv['a_c'], sv['dtb'], sv['s_hist'], sv['h_hist'],
        dyr, dys, dxs_skip, n("scan"), rider)
    g['a_log'] = dac[0, :SSM_HEADS] * (-jnp.exp(p['a_log']))
    g['dt_bias'] = ddtb[0, :SSM_HEADS]
    dxbc_raw, g['conv_w'], g['conv_b'] = _conv_bwd(sv['proj'], p['conv_w'], p['conv_b'], dxbc_act, n("conv"))
    pieces = {'z': dz, 'xbc': dxbc_raw, 'gates': dgates, 'v': dv_r, 'g': dg, 'qk': dqk_r}
    d_w = {k: _matmul(sv['u_t'], pc, 'nn', n(f"in_proj_dw_{k}"), out_dtypes=gd) for k, pc in pieces.items()}
    d_dt = _matmul(sv['u_t'], ddtraw, 'nn', n("in_proj_dt_dw"), out_dtypes=gd)
    g['w_in'] = _w_in_grad_blocks(d_w, d_dt)
    du_dt = _matmul(ddtraw, p['w_in_dt'], 'nt', n("in_proj_dt_dx"))
    first_by_core = [_grad_scatter(k, g[k]) for k in FIRST]
    du, first_sib = _matmul_nt_pieces(list(pieces.values()), p['w_in_main'], n("in_proj_dx"), extras=(du_dt,), epi=_epi_add,
                                      rider=_ExchangeCores(first_by_core))
    dx0, dx0b, g['norm_mix'] = _rmsnorm_bwd(sv['x0'], p['norm_mix'], du, dx1, n("norm_mix"))
    return (dx0, dx0b, g, delivered[:len(pending)], delivered[len(pending):],
            _core_sums(FIRST, first_by_core, first_sib, c_idx, l))


def _full_weights(names, gathered):
    p = {}
    for k, g in zip(names, gathered):
        if k == 'w_in':
            p['w_in_main'], p['w_in_dt'] = _w_in_from_shards(g)
        elif k in COL_SHARDED:
            p[k] = jnp.concatenate([g[j] for j in range(N_DEV)], axis=1)
        else:
            p[k] = g.reshape(-1, g.shape[-1])
    return p


def _grad_scatter(k, g):
    if k in LATER:
        return g
    if k == 'w_in':
        blocks = g
    elif k in COL_SHARDED:
        c = g.shape[1] // N_DEV
        blocks = [g[:, j * c:(j + 1) * c] for j in range(N_DEV)]
    else:
        r = g.shape[0] // N_DEV
        blocks = [g[j * r:(j + 1) * r] for j in range(N_DEV)]
    return jnp.stack([jnp.stack([blocks[2 * chip + core] for chip in range(4)]) for core in range(2)])


def _core_sums(names, by_core, from_sibling, c_idx, l):
    return [_add_halves(a, o, c_idx, f"grad_add_cores_{k}_l{l}") for k, a, o in zip(names, by_core, from_sibling)]


def _step(x, mem, positions, small, blocks, loss_target):
    cos2, sin2 = _rope_tables(positions)
    first = _run_exchange(_AllGather([blocks[0][k] for k in FIRST]), "all_gather_first_l0")
    saved, layers = [], []
    for l in range(DEPTH):
        p = {k: small[k][l] for k in SMALL if k != 'norm_final'}
        p.update(_full_weights(FIRST, first))
        rider = _AllGather([blocks[l + 1][k] for k in FIRST]) if l + 1 < DEPTH else None
        x, sv, p, first = _layer_fwd(x, mem, cos2, sin2, p, l, [blocks[l][k] for k in LATER], rider)
        saved.append(sv)
        layers.append(p)
    loss, dx, dxb, dnf = _loss_head(x, small['norm_final'], loss_target, "loss_head")
    c_idx = lax.axis_index("c").astype(jnp.int32).reshape(1)
    grads, by_chip, pending = [None] * DEPTH, [dict() for _ in range(DEPTH)], []
    for l in reversed(range(DEPTH)):
        dx, dxb, grads[l], got_first, got_later, pending_next = _layer_bwd(
            dx, dxb, mem, cos2, sin2, layers[l], saved[l], l, pending, c_idx)
        if pending:
            by_chip[l + 1].update(zip(FIRST, got_first))
        by_chip[l].update(zip(LATER, got_later))
        pending = pending_next
    by_chip[0].update(zip(FIRST, _run_exchange(_ExchangeChips(pending), "grad_exchange_chips_first_l0")))
    small_g = {}
    for k in SMALL:
        small_g[k] = dnf.reshape(D_MODEL) if k == 'norm_final' else [grads[l][k].reshape(small[k].shape[1:]) for l in range(DEPTH)]
    return loss, dx, small_g, by_chip


MESH = pl.DeviceIdType.MESH
ANY_SPEC = pl.BlockSpec(memory_space=pl.ANY)


def _mesh_pos():
    return lax.axis_index("x"), lax.axis_index("y"), lax.axis_index("c")


def _other_chips(x, y):
    return [(1 - x, y), (x, 1 - y), (1 - x, 1 - y)]


class _AllGather:
    def __init__(self, arrs):
        self.arrs = list(arrs)
        na = self.n = len(self.arrs)
        self.out_shape = [_sds((N_DEV,) + a.shape, a.dtype) for a in self.arrs]
        self.scratch = [pltpu.SemaphoreType.DMA((na, 7)), pltpu.SemaphoreType.DMA((na, 7)), pltpu.SemaphoreType.DMA((na,))]

    def _copies(self, x_refs, o_refs, sems):
        send_sems, recv_sems, local_sems = sems
        x, y, c = _mesh_pos()
        me, sib = (x, y, c), (x, y, 1 - c)
        chips = _other_chips(x, y)

        def copy(a, k, block, to, src=None):
            dst = o_refs[a].at[4 * block[0] + 2 * block[1] + block[2]]
            return pltpu.make_async_remote_copy(src_ref=dst if src is None else src, dst_ref=dst,
                                                send_sem=send_sems.at[a, k], recv_sem=recv_sems.at[a, k],
                                                device_id=to, device_id_type=MESH)

        mine = [pltpu.make_async_copy(x_refs[a], o_refs[a].at[4 * x + 2 * y + c], local_sems.at[a]) for a in range(self.n)]
        first = []
        for a in range(self.n):
            first.append(copy(a, 0, me, sib, src=x_refs[a]))
            first += [copy(a, 1 + j, me, (*chip, c), src=x_refs[a]) for j, chip in enumerate(chips)]
        return copy, mine, first, me, sib, chips, c

    def start(self, x_refs, o_refs, sems):
        _, mine, first, *_ = self._copies(x_refs, o_refs, sems)
        for cp in mine + first:
            cp.start()

    def finish(self, x_refs, o_refs, sems):
        copy, mine, first, me, sib, chips, c = self._copies(x_refs, o_refs, sems)
        passed = []
        for a in range(self.n):
            for j, chip in enumerate(chips):
                copy(a, 1 + j, (*chip, c), me).wait_recv()
                cp = copy(a, 4 + j, (*chip, c), sib)
                cp.start()
                passed.append(cp)
        for a in range(self.n):
            copy(a, 0, sib, me).wait_recv()
            for j, chip in enumerate(chips):
                copy(a, 4 + j, (*chip, 1 - c), me).wait_recv()
        for cp in first + passed:
            cp.wait_send()
        for cp in mine:
            cp.wait()


class _ExchangeChips:
    def __init__(self, arrs):
        self.arrs = list(arrs)
        na = self.n = len(self.arrs)
        self.out_shape = [_sds(a.shape, a.dtype) for a in self.arrs]
        self.scratch = [pltpu.SemaphoreType.DMA((na, 3)), pltpu.SemaphoreType.DMA((na, 3)), pltpu.SemaphoreType.DMA((na,))]

    def _copies(self, a_refs, o_refs, sems):
        send_sems, recv_sems, local_sems = sems
        x, y, c = _mesh_pos()
        my_chip = 2 * x + y
        chips = _other_chips(x, y)
        mine = [pltpu.make_async_copy(a_refs[a].at[my_chip], o_refs[a].at[my_chip], local_sems.at[a]) for a in range(self.n)]
        sends = [pltpu.make_async_remote_copy(src_ref=a_refs[a].at[2 * px + py], dst_ref=o_refs[a].at[my_chip],
                                              send_sem=send_sems.at[a, j], recv_sem=recv_sems.at[a, j],
                                              device_id=(px, py, c), device_id_type=MESH)
                 for a in range(self.n) for j, (px, py) in enumerate(chips)]
        recvs = [pltpu.make_async_remote_copy(src_ref=a_refs[a].at[2 * px + py], dst_ref=o_refs[a].at[2 * px + py],
                                              send_sem=send_sems.at[a, j], recv_sem=recv_sems.at[a, j],
                                              device_id=(px, py, c), device_id_type=MESH)
                 for a in range(self.n) for j, (px, py) in enumerate(chips)]
        return mine, sends, recvs

    def start(self, a_refs, o_refs, sems):
        mine, sends, _ = self._copies(a_refs, o_refs, sems)
        for cp in mine + sends:
            cp.start()

    def finish(self, a_refs, o_refs, sems):
        mine, sends, recvs = self._copies(a_refs, o_refs, sems)
        for cp in recvs:
            cp.wait_recv()
        for cp in sends:
            cp.wait_send()
        for cp in mine:
            cp.wait()


def _run_exchange(ex, name):
    na = ex.n

    def body(*refs):
        i_refs, o_refs, sems = refs[:na], refs[na:2 * na], refs[2 * na:]
        ex.start(i_refs, o_refs, sems)
        ex.finish(i_refs, o_refs, sems)

    return pl.pallas_call(body, in_specs=[ANY_SPEC] * na, out_specs=[ANY_SPEC] * na, out_shape=ex.out_shape,
                          scratch_shapes=ex.scratch, name=name)(*ex.arrs)


class _ExchangeCores:
    def __init__(self, arrs):
        self.arrs = list(arrs)
        na = self.n = len(self.arrs)
        self.out_shape = [_sds(a.shape[1:], a.dtype) for a in self.arrs]
        self.scratch = [pltpu.SemaphoreType.DMA((na,)), pltpu.SemaphoreType.DMA((na,))]

    def _copies(self, a_refs, o_refs, sems):
        send_sems, recv_sems = sems
        x, y, c = _mesh_pos()
        return [pltpu.make_async_remote_copy(src_ref=a_refs[a].at[1 - c], dst_ref=o_refs[a], send_sem=send_sems.at[a],
                                             recv_sem=recv_sems.at[a], device_id=(x, y, 1 - c), device_id_type=MESH)
                for a in range(self.n)]

    def start(self, a_refs, o_refs, sems):
        for cp in self._copies(a_refs, o_refs, sems):
            cp.start()

    def finish(self, a_refs, o_refs, sems):
        for cp in self._copies(a_refs, o_refs, sems):
            cp.wait()


def _as_rows(a, lead):
    return a.reshape(a.shape[:lead] + (-1, a.shape[-1]))


def _add_halves(a, other, c_idx, name):
    a3, o2 = _as_rows(a, 1), _as_rows(other, 0)
    rows, cols = o2.shape
    tr = _pick(rows, 256)

    def body(c_ref, a_ref, o_ref, out_ref):
        out_ref[...] = (a_ref[0].astype(F32) + o_ref[...].astype(F32)).astype(out_ref.dtype)

    out = pl.pallas_call(
        body,
        grid_spec=pltpu.PrefetchScalarGridSpec(
            num_scalar_prefetch=1, grid=(rows // tr,),
            in_specs=[pl.BlockSpec((1, tr, cols), lambda i, c_ref: (c_ref[0], i, 0)),
                      pl.BlockSpec((tr, cols), lambda i, c_ref: (i, 0))],
            out_specs=pl.BlockSpec((tr, cols), lambda i, c_ref: (i, 0))),
        out_shape=_sds((rows, cols), a.dtype), compiler_params=_cparams(("parallel",)), name=name,
    )(c_idx, a3, o2)
    return out.reshape(other.shape)


def _all_reduce_small(v, name):
    r = v.shape[0]

    def body(v_ref, o_ref, slots, send_sems, recv_sems):
        x, y, c = _mesh_pos()
        me = 4 * x + 2 * y + c
        slots[me] = v_ref[...]
        cps = []
        for k in range(1, N_DEV):
            px = 1 - x if k & 4 else x
            py = 1 - y if k & 2 else y
            pc = 1 - c if k & 1 else c
            cps.append(pltpu.make_async_remote_copy(src_ref=v_ref, dst_ref=slots.at[me], send_sem=send_sems.at[k - 1],
                                                    recv_sem=recv_sems.at[k - 1], device_id=(px, py, pc), device_id_type=MESH))
        for cp in cps:
            cp.start()
        for cp in cps:
            cp.wait()
        acc = slots[0]
        for d in range(1, N_DEV):
            acc = acc + slots[d]
        o_ref[...] = acc

    vm = pl.BlockSpec(memory_space=pltpu.VMEM)
    return pl.pallas_call(
        body, in_specs=[vm], out_specs=vm, out_shape=_sds((r, LANES), F32),
        scratch_shapes=[pltpu.VMEM((N_DEV, r, LANES), F32), pltpu.SemaphoreType.DMA((N_DEV - 1,)),
                        pltpu.SemaphoreType.DMA((N_DEV - 1,))],
        compiler_params=pltpu.CompilerParams(vmem_limit_bytes=VMEM_LIMIT_BYTES), name=name,
    )(v)


def _adamw(w, g_slots, m, v, name):
    depth, rows, cols = w.shape
    ns = g_slots.shape[0]
    tr = _pick(rows, 256 if cols <= 1024 else 128)

    def body(w_ref, g_ref, m_ref, v_ref, go_ref, d_ref, mo_ref, vo_ref):
        g = g_ref[0, 0].astype(F32)
        for i in range(1, ns):
            g = g + g_ref[i, 0].astype(F32)
        m_new = ADAM_B1 * m_ref[0] + (1.0 - ADAM_B1) * g
        v_new = ADAM_B2 * v_ref[0] + (1.0 - ADAM_B2) * (g * g)
        m_hat = m_new / (1.0 - ADAM_B1 ** ADAM_STEP)
        v_hat = v_new / (1.0 - ADAM_B2 ** ADAM_STEP)
        go_ref[0] = g
        d_ref[0] = -ADAM_LR * (m_hat / (jnp.sqrt(v_hat) + ADAM_EPS) + ADAM_WD * w_ref[0])
        mo_ref[0] = m_new
        vo_ref[0] = v_new

    blk = pl.BlockSpec((1, tr, cols), lambda l, i: (l, i, 0))
    return pl.pallas_call(
        body, grid=(depth, rows // tr),
        in_specs=[blk, pl.BlockSpec((ns, 1, tr, cols), lambda l, i: (0, l, i, 0)), blk, blk],
        out_specs=[blk] * 4, out_shape=[_sds(w.shape, F32)] * 4,
        compiler_params=_cparams(("parallel", "parallel")), name=name,
    )(w, g_slots, m, v)


_ARG_NAMES = (['x', 'mem', 'positions'] + WEIGHTS + ['loss_target'] + ['m_' + n for n in WEIGHTS]
              + ['v_' + n for n in WEIGHTS])


PACK_TILE = 8 * LANES


def _pack_rows(parts):
    blocks = []
    for part in parts:
        flat = part.reshape(-1)
        pad = (-flat.shape[0]) % PACK_TILE
        blocks.append((jnp.pad(flat, (0, pad)) if pad else flat).reshape(-1, LANES))
    return jnp.concatenate(blocks, axis=0)


def _unpack_rows(packed, shapes):
    out, off = [], 0
    for shp in shapes:
        n = int(np.prod(shp))
        rows = -(-n // PACK_TILE) * 8
        out.append(packed[off:off + rows].reshape(-1)[:n].reshape(shp))
        off += rows
    return out


def kernel(x, mem, positions, norm_mix, w_in, b_gate, conv_w, conv_b, dt_bias, a_log, d_skip, ssm_norm, w_br_ret, w_br_ssm, w_out, norm_xa, norm_mem, xa_wq, xa_wkv, xa_wo, norm_mlp, mlp_w1, mlp_w2, norm_final, loss_target, m_norm_mix, m_w_in, m_b_gate, m_conv_w, m_conv_b, m_dt_bias, m_a_log, m_d_skip, m_ssm_norm, m_w_br_ret, m_w_br_ssm, m_w_out, m_norm_xa, m_norm_mem, m_xa_wq, m_xa_wkv, m_xa_wo, m_norm_mlp, m_mlp_w1, m_mlp_w2, m_norm_final, v_norm_mix, v_w_in, v_b_gate, v_conv_w, v_conv_b, v_dt_bias, v_a_log, v_d_skip, v_ssm_norm, v_w_br_ret, v_w_br_ssm, v_w_out, v_norm_xa, v_norm_mem, v_xa_wq, v_xa_wkv, v_xa_wo, v_norm_mlp, v_mlp_w1, v_mlp_w2, v_norm_final):
    d = dict(zip(_ARG_NAMES, (x, mem, positions, norm_mix, w_in, b_gate, conv_w, conv_b, dt_bias, a_log, d_skip, ssm_norm, w_br_ret, w_br_ssm, w_out, norm_xa, norm_mem, xa_wq, xa_wkv, xa_wo, norm_mlp, mlp_w1, mlp_w2, norm_final, loss_target, m_norm_mix, m_w_in, m_b_gate, m_conv_w, m_conv_b, m_dt_bias, m_a_log, m_d_skip, m_ssm_norm, m_w_br_ret, m_w_br_ssm, m_w_out, m_norm_xa, m_norm_mem, m_xa_wq, m_xa_wkv, m_xa_wo, m_norm_mlp, m_mlp_w1, m_mlp_w2, m_norm_final, v_norm_mix, v_w_in, v_b_gate, v_conv_w, v_conv_b, v_dt_bias, v_a_log, v_d_skip, v_ssm_norm, v_w_br_ret, v_w_br_ssm, v_w_out, v_norm_xa, v_norm_mem, v_xa_wq, v_xa_wkv, v_xa_wo, v_norm_mlp, v_mlp_w1, v_mlp_w2, v_norm_final)))
    blocks = [{k: d[k][l] if k == 'conv_w' else d[k][l].astype(MXU_DTYPE) for k in SHARDED} for l in range(DEPTH)]
    small = {k: d[k] for k in SMALL}
    loss, grad_x, grads, by_chip_l = _step(d['x'][0], d['mem'][0], d['positions'][0], small, blocks, d['loss_target'][0])
    by_chip = [jnp.stack([by_chip_l[l][k] for l in range(DEPTH)], axis=1) for k in SHARDED]
    small_g = [grads[k] if k == 'norm_final' else jnp.stack(grads[k]) for k in SMALL]
    total = _all_reduce_small(_pack_rows([loss] + small_g), "all_reduce_small")
    loss_out = total[0, 0]
    res = {}
    for k, g4 in zip(SHARDED, by_chip):
        res[k] = _adamw(d[k], g4, d['m_' + k], d['v_' + k], f"adamw_{k}")
    small_shapes = [d[k].shape for k in SMALL]
    pk = lambda pre: _pack_rows([d[pre + k] for k in SMALL])
    outs = _adamw(pk('')[None], total[8:][None, None], pk('m_')[None], pk('v_')[None], "adamw_small")
    unpacked = [_unpack_rows(o[0], small_shapes) for o in outs]
    for i, k in enumerate(SMALL):
        res[k] = [unpacked[j][i] for j in range(4)]
    return (loss_out, grad_x[None], *[res[k][0] for k in WEIGHTS], *[res[k][1] for k in WEIGHTS],
            *[res[k][2] for k in WEIGHTS], *[res[k][3] for k in WEIGHTS])
```

```python
import functools

import numpy as np
import jax
import jax.numpy as jnp
from jax import lax
from jax.experimental import pallas as pl
from jax.experimental.pallas import tpu as pltpu

F32 = jnp.float32
MXU_DTYPE = jnp.bfloat16
VMEM_LIMIT_BYTES = 56 * 1024 * 1024
LANES = 128
N_DEV = 8

D_MODEL = 1024
DEPTH = 4
CHUNK = 64
EPS = 1e-6
RET_HEADS, RET_QK_DIM, RET_V_DIM = 4, 128, 256
RET_QK, RET_V = 512, 1024
ROPE_THETA = 10000.0
SSM_INNER, SSM_HEAD_DIM, SSM_HEADS, SSM_GROUPS, SSM_STATE, SSM_CONV = 2048, 64, 32, 8, 128, 4
SSM_BC = 1024
SSM_CONV_DIM = 4096
IN_DIM = 11296
XA_HEADS, XA_HEAD_DIM = 4, 256
D_FF = 4096
ADAM_LR, ADAM_B1, ADAM_B2, ADAM_EPS, ADAM_WD, ADAM_STEP = 0.001, 0.9, 0.999, 1e-08, 0.01, 10

PROJ_W = 11264
COL_Z, COL_XBC, COL_GATES, COL_V, COL_G, COL_Q, COL_K = 0, 2048, 6144, 8192, 9216, 10240, 10752
DT_PAD = 128
N_LTILE = SSM_INNER // LANES

WEIGHTS = ['norm_mix', 'w_in', 'b_gate', 'conv_w', 'conv_b', 'dt_bias', 'a_log', 'd_skip', 'ssm_norm',
           'w_br_ret', 'w_br_ssm', 'w_out', 'norm_xa', 'norm_mem', 'xa_wq', 'xa_wkv', 'xa_wo', 'norm_mlp',
           'mlp_w1', 'mlp_w2', 'norm_final']
COL_SHARDED = ['w_in', 'conv_w', 'xa_wkv', 'mlp_w1']
ROW_SHARDED = ['w_br_ret', 'w_br_ssm', 'w_out', 'xa_wq', 'xa_wo', 'mlp_w2']
SHARDED = COL_SHARDED + ROW_SHARDED
FIRST = ['w_in', 'conv_w']
LATER = [n for n in SHARDED if n not in FIRST]
SMALL = [n for n in WEIGHTS if n not in SHARDED]


def _cparams(sem=None):
    return pltpu.CompilerParams(dimension_semantics=sem, vmem_limit_bytes=VMEM_LIMIT_BYTES)


def _sds(shape, dtype):
    return jax.ShapeDtypeStruct(shape, dtype)


def _full(shape):
    nd = len(shape)
    return pl.BlockSpec(shape, lambda *_: (0,) * nd)


_DIMS = {'nn': (((1,), (0,)), ((), ())), 'nt': (((1,), (1,)), ((), ())), 'tn': (((0,), (0,)), ((), ()))}


def _dot(a, b, mode='nn'):
    return lax.dot_general(a.astype(MXU_DTYPE), b.astype(MXU_DTYPE), _DIMS[mode], preferred_element_type=F32)


@functools.partial(jax.custom_vjp, nondiff_argnums=(2,))
def _mm(a, b, mode):
    return _dot(a, b, mode)


def _mm_fwd(a, b, mode):
    return _dot(a, b, mode), (a, b)


def _mm_bwd(mode, res, g):
    a, b = res
    if mode == 'nn':
        return _dot(g, b, 'nt'), _dot(a, g, 'tn')
    if mode == 'nt':
        return _dot(g, b, 'nn'), _dot(g, a, 'tn')
    return _dot(b, g, 'nt'), _dot(a, g, 'nn')


_mm.defvjp(_mm_fwd, _mm_bwd)


def _split3(x):
    hi = x.astype(jnp.bfloat16)
    r1 = x - hi.astype(F32)
    mid = r1.astype(jnp.bfloat16)
    lo = (r1 - mid.astype(F32)).astype(jnp.bfloat16)
    return hi, mid, lo


def _dot_sel(x, c, left=False):
    dims = _DIMS['nn']
    parts = _split3(x)
    if left:
        outs = [lax.dot_general(c, p, dims, preferred_element_type=F32) for p in parts]
    else:
        outs = [lax.dot_general(p, c, dims, preferred_element_type=F32) for p in parts]
    return (outs[0] + outs[1]) + outs[2]


def _silu(x):
    return x * jax.nn.sigmoid(x)


def _softplus(x):
    pos = x > 0.0
    return jnp.where(pos, x, 0.0) + jnp.log1p(jnp.exp(jnp.where(pos, -x, x)))


def _rms(x):
    return x * lax.rsqrt(jnp.mean(x * x, axis=-1, keepdims=True) + EPS)


def _pick(n, pref):
    t = min(n, pref)
    while n % t:
        t //= 2
    return t


def _with_rider(core, n_in, n_out, n_scratch, rider, grid):
    if rider is None:
        return core
    na, nrs = rider.n, len(rider.scratch)

    def at(step_of):
        cond = pl.program_id(0) == step_of(grid[0])
        for ax in range(1, len(grid)):
            cond = cond & (pl.program_id(ax) == step_of(grid[ax]))
        return cond

    def body(*refs):
        ci, ri = refs[:n_in], refs[n_in:n_in + na]
        co, ro = refs[n_in + na:n_in + na + n_out], refs[n_in + na + n_out:n_in + 2 * na + n_out]
        sc = refs[n_in + 2 * na + n_out:]
        cs, rs = sc[:n_scratch], sc[n_scratch:]
        assert len(rs) == nrs

        @pl.when(at(lambda n: 0))
        def _():
            rider.start(ri, ro, rs)

        core(*ci, *co, *cs)

        @pl.when(at(lambda n: n - 1))
        def _():
            rider.finish(ri, ro, rs)

    return body


def _rider_args(rider):
    if rider is None:
        return [], [], [], [], []
    return list(rider.arrs), [ANY_SPEC] * rider.n, [ANY_SPEC] * rider.n, list(rider.out_shape), list(rider.scratch)


MATMUL_TK_MAX = 4096


def _tiles(mode, m, n, k):
    tm, tn = (512, 1024) if mode == 'nt' else (1024, 512)
    tk = k
    while tk > MATMUL_TK_MAX or k % tk or tk % LANES:
        tk -= LANES
    return _pick(m, tm), _pick(n, tn), tk


def _matmul(a, b, mode, name, *, extras=(), epi=None, out_dtypes=(F32,), out_t=None, tiles=None, rider=None,
            scatter=None):
    if mode == 'nn':
        (m, k), (k2, n) = a.shape, b.shape
    elif mode == 'nt':
        (m, k), (n, k2) = a.shape, b.shape
    else:
        (k, m), (k2, n) = a.shape, b.shape
    assert k == k2, (a.shape, b.shape, mode)
    if scatter == 'rows':
        tiles = (m // N_DEV, n if n <= 1024 else 512, tiles[2] if tiles else _tiles(mode, m, n, k)[2])
    elif scatter == 'cols':
        tiles = (_pick(m, 1024), n // N_DEV, tiles[2] if tiles else _tiles(mode, m, n, k)[2])
    tm, tn, tk = tiles or _tiles(mode, m, n, k)
    nk = k // tk
    n_ex, n_out = len(extras), len(out_dtypes)
    out_t = out_t or (False,) * n_out

    def finish(acc, ex_refs, o_refs):
        outs = epi(acc, *[r[...] for r in ex_refs]) if epi is not None else (acc,)
        for o_ref, o, tr in zip(o_refs, outs, out_t):
            if scatter:
                o_ref[0, 0] = o.astype(o_ref.dtype)
            else:
                o_ref[...] = (o.T if tr else o).astype(o_ref.dtype)

    def body(*refs):
        a_ref, b_ref = refs[0], refs[1]
        ex_refs = refs[2:2 + n_ex]
        o_refs = refs[2 + n_ex:2 + n_ex + n_out]
        if nk == 1:
            finish(_dot(a_ref[...], b_ref[...], mode), ex_refs, o_refs)
            return
        acc_ref = refs[-1]
        kk = pl.program_id(2)

        @pl.when(kk == 0)
        def _():
            acc_ref[...] = jnp.zeros_like(acc_ref)

        acc_ref[...] += _dot(a_ref[...], b_ref[...], mode)

        @pl.when(kk == nk - 1)
        def _():
            finish(acc_ref[...], ex_refs, o_refs)

    if mode == 'nn':
        a_spec = pl.BlockSpec((tm, tk), lambda i, j, kk: (i, kk))
        b_spec = pl.BlockSpec((tk, tn), lambda i, j, kk: (kk, j))
    elif mode == 'nt':
        a_spec = pl.BlockSpec((tm, tk), lambda i, j, kk: (i, kk))
        b_spec = pl.BlockSpec((tn, tk), lambda i, j, kk: (j, kk))
    else:
        a_spec = pl.BlockSpec((tk, tm), lambda i, j, kk: (kk, i))
        b_spec = pl.BlockSpec((tk, tn), lambda i, j, kk: (kk, j))
    mn_spec = pl.BlockSpec((tm, tn), lambda i, j, kk: (i, j))
    nm_spec = pl.BlockSpec((tn, tm), lambda i, j, kk: (j, i))
    grid = (m // tm, n // tn, nk)
    r_arrs, r_in, r_out, r_shape, r_scratch = _rider_args(rider)
    o_specs = [nm_spec if tr else mn_spec for tr in out_t]
    o_shapes = [_sds((n, m) if tr else (m, n), dt) for dt, tr in zip(out_dtypes, out_t)]
    if scatter == 'rows':
        o_specs = [pl.BlockSpec((1, 1, tm, tn), lambda i, j, kk: (i % 2, i // 2, 0, j))]
        o_shapes = [_sds((2, 4, tm, n), out_dtypes[0])]
    elif scatter == 'cols':
        o_specs = [pl.BlockSpec((1, 1, tm, tn), lambda i, j, kk: (j % 2, j // 2, i, 0))]
        o_shapes = [_sds((2, 4, m, tn), out_dtypes[0])]
    outs = pl.pallas_call(
        _with_rider(body, 2 + n_ex, n_out, int(nk > 1), rider, grid), grid=grid,
        in_specs=[a_spec, b_spec] + [mn_spec] * n_ex + r_in,
        out_specs=o_specs + r_out,
        out_shape=o_shapes + r_shape,
        scratch_shapes=([pltpu.VMEM((tm, tn), F32)] if nk > 1 else []) + r_scratch,
        compiler_params=_cparams(("arbitrary",) * 3 if rider is not None else ("parallel", "parallel", "arbitrary")),
        name=name,
    )(a, b, *extras, *r_arrs)
    res = outs[0] if n_out == 1 else outs[:n_out]
    return (res, outs[n_out:]) if rider is not None else res


def _epi_add(acc, r):
    return (acc + r,)


PIECE_TK = 1024


def _matmul_nt_pieces(pieces, b, name, *, extras=(), epi=None, out_dtypes=(F32,), rider=None):
    m, n = pieces[0].shape[0], b.shape[0]
    tm, tn, tk = _pick(m, 1024), _pick(n, 1024), PIECE_TK
    steps = [pc.shape[1] // tk for pc in pieces]
    starts = [sum(steps[:i]) for i in range(len(pieces))]
    nk = sum(steps)
    assert b.shape[1] == nk * tk and all(pc.shape[1] % tk == 0 for pc in pieces)
    n_pc, n_ex, n_out = len(pieces), len(extras), len(out_dtypes)

    def body(*refs):
        pc_refs, b_ref = refs[:n_pc], refs[n_pc]
        ex_refs = refs[n_pc + 1:n_pc + 1 + n_ex]
        o_refs = refs[n_pc + 1 + n_ex:n_pc + 1 + n_ex + n_out]
        acc_ref = refs[-1]
        kk = pl.program_id(2)

        @pl.when(kk == 0)
        def _():
            acc_ref[...] = jnp.zeros_like(acc_ref)

        for pc_ref, st, ns in zip(pc_refs, starts, steps):
            @pl.when((kk >= st) & (kk < st + ns))
            def _(pc_ref=pc_ref):
                acc_ref[...] += _dot(pc_ref[...], b_ref[...], 'nt')

        @pl.when(kk == nk - 1)
        def _():
            acc = acc_ref[...]
            outs = epi(acc, *[r[...] for r in ex_refs]) if epi is not None else (acc,)
            for o_ref, o in zip(o_refs, outs):
                o_ref[...] = o.astype(o_ref.dtype)

    pc_specs = [pl.BlockSpec((tm, tk), lambda i, j, kk, st=st, ns=ns: (i, jnp.clip(kk - st, 0, ns - 1)))
                for st, ns in zip(starts, steps)]
    mn_spec = pl.BlockSpec((tm, tn), lambda i, j, kk: (i, j))
    grid = (m // tm, n // tn, nk)
    r_arrs, r_in, r_out, r_shape, r_scratch = _rider_args(rider)
    outs = pl.pallas_call(
        _with_rider(body, n_pc + 1 + n_ex, n_out, 1, rider, grid), grid=grid,
        in_specs=pc_specs + [pl.BlockSpec((tn, tk), lambda i, j, kk: (j, kk))] + [mn_spec] * n_ex + r_in,
        out_specs=[mn_spec] * n_out + r_out, out_shape=[_sds((m, n), dt) for dt in out_dtypes] + r_shape,
        scratch_shapes=[pltpu.VMEM((tm, tn), F32)] + r_scratch,
        compiler_params=_cparams(("arbitrary",) * 3 if rider is not None else ("parallel", "parallel", "arbitrary")),
        name=name,
    )(*pieces, b, *extras, *r_arrs)
    res = outs[0] if n_out == 1 else outs[:n_out]
    return (res, outs[n_out:]) if rider is not None else res


def _rmsnorm_fn(x, w):
    return _rms(x) * w


def _rmsnorm(x, w, name):
    s, d = x.shape
    t = _pick(s, 512)

    def body(x_ref, w_ref, o_ref, ot_ref):
        y = _rmsnorm_fn(x_ref[...], w_ref[...])
        o_ref[...] = y.astype(o_ref.dtype)
        ot_ref[...] = y.T.astype(ot_ref.dtype)

    return pl.pallas_call(
        body, grid=(s // t,),
        in_specs=[pl.BlockSpec((t, d), lambda i: (i, 0)), _full((1, d))],
        out_specs=[pl.BlockSpec((t, d), lambda i: (i, 0)), pl.BlockSpec((d, t), lambda i: (0, i))],
        out_shape=[_sds((s, d), MXU_DTYPE), _sds((d, s), MXU_DTYPE)],
        compiler_params=_cparams(("parallel",)), name=name,
    )(x, w.reshape(1, d))


def _rmsnorm_bwd(x, w, du, dres, name):
    s, d = x.shape
    t = _pick(s, 512)
    has_res = dres is not None

    def body(*refs):
        if has_res:
            x_ref, w_ref, du_ref, dres_ref, dx_ref, dxb_ref, dw_ref = refs
        else:
            x_ref, w_ref, du_ref, dx_ref, dxb_ref, dw_ref = refs
        _, vjp = jax.vjp(_rmsnorm_fn, x_ref[...], w_ref[...])
        dx, dw = vjp(du_ref[...])
        dx = dx + dres_ref[...] if has_res else dx
        dx_ref[...] = dx
        dxb_ref[...] = dx.astype(dxb_ref.dtype)

        @pl.when(pl.program_id(0) == 0)
        def _():
            dw_ref[...] = jnp.zeros_like(dw_ref)

        dw_ref[...] += dw

    row = pl.BlockSpec((t, d), lambda i: (i, 0))
    return pl.pallas_call(
        body, grid=(s // t,),
        in_specs=[row, _full((1, d)), row] + ([row] if has_res else []),
        out_specs=[row, row, _full((1, d))],
        out_shape=[_sds((s, d), F32), _sds((s, d), MXU_DTYPE), _sds((1, d), F32)],
        compiler_params=_cparams(("arbitrary",)), name=name,
    )(x, w.reshape(1, d), du, *([dres] if has_res else []))


CONV_CW = 2048
CONV_HALO = 16


def _shifted(cat):
    return [cat] + [pltpu.roll(cat, sft, axis=0) for sft in (1, 2, 3)]


def _conv_taps(shifted, w, n_rows, off):
    acc = shifted[0][off:off + n_rows, :] * w[3:4, :]
    for sft in (1, 2, 3):
        acc = acc + shifted[sft][off:off + n_rows, :] * w[3 - sft:4 - sft, :]
    return acc


def _conv_fwd(proj, conv_w, conv_b, name):
    s = proj.shape[0]
    tr = _pick(s, 512)
    hb = tr // CONV_HALO
    col0 = COL_XBC // CONV_CW

    def body(prev_ref, x_ref, w_ref, b_ref, o_ref):
        i = pl.program_id(1)
        prev = jnp.where(i == 0, 0.0, prev_ref[...].astype(F32))
        cat = jnp.concatenate([prev, x_ref[...].astype(F32)], axis=0)
        o_ref[...] = _silu(_conv_taps(_shifted(cat), w_ref[...], tr, CONV_HALO) + b_ref[...]).astype(o_ref.dtype)

    return pl.pallas_call(
        body, grid=(SSM_CONV_DIM // CONV_CW, s // tr),
        in_specs=[pl.BlockSpec((CONV_HALO, CONV_CW), lambda j, i: (jnp.maximum(i * hb - 1, 0), j + col0)),
                  pl.BlockSpec((tr, CONV_CW), lambda j, i: (i, j + col0)),
                  pl.BlockSpec((SSM_CONV, CONV_CW), lambda j, i: (0, j)),
                  pl.BlockSpec((1, CONV_CW), lambda j, i: (0, j))],
        out_specs=pl.BlockSpec((tr, CONV_CW), lambda j, i: (i, j)),
        out_shape=_sds((s, SSM_CONV_DIM), MXU_DTYPE),
        compiler_params=_cparams(("parallel", "parallel")), name=name,
    )(proj, proj, conv_w, conv_b.reshape(1, SSM_CONV_DIM))


def _conv_bwd(proj, conv_w, conv_b, dact, name):
    s = proj.shape[0]
    tr = _pick(s, 512)
    hb = tr // CONV_HALO
    nb = s // CONV_HALO
    nt = s // tr
    col0 = COL_XBC // CONV_CW
    h = CONV_HALO

    def body(prev_ref, x_ref, next_ref, w_ref, b_ref, da_ref, dan_ref, dx_ref, dw_ref, db_ref):
        i = pl.program_id(1)
        w = w_ref[...]
        prev = jnp.where(i == 0, 0.0, prev_ref[...].astype(F32))
        cat = jnp.concatenate([prev, x_ref[...].astype(F32), next_ref[...].astype(F32)], axis=0)
        shifted = _shifted(cat)
        pre = _conv_taps(shifted, w, tr + h, h) + b_ref[...]
        dact_n = jnp.where(i == nt - 1, 0.0, dan_ref[...])
        dact_ext = jnp.concatenate([da_ref[...], dact_n], axis=0)
        sg = jax.nn.sigmoid(pre)
        dpre = dact_ext * (sg * (1.0 + pre * (1.0 - sg)))
        dx = dpre[:tr, :] * w[3:4, :]
        for sft in (1, 2, 3):
            dx = dx + pltpu.roll(dpre, tr + h - sft, axis=0)[:tr, :] * w[3 - sft:4 - sft, :]
        dx_ref[...] = dx.astype(dx_ref.dtype)

        @pl.when(i == 0)
        def _():
            dw_ref[...] = jnp.zeros_like(dw_ref)
            db_ref[...] = jnp.zeros_like(db_ref)

        dp = dpre[:tr, :]
        db_ref[...] += jnp.sum(dp, axis=0, keepdims=True)
        for r, sft in enumerate((3, 2, 1, 0)):
            dw_ref[r:r + 1, :] += jnp.sum(dp * shifted[sft][h:h + tr, :], axis=0, keepdims=True)

    return pl.pallas_call(
        body, grid=(SSM_CONV_DIM // CONV_CW, nt),
        in_specs=[pl.BlockSpec((h, CONV_CW), lambda j, i: (jnp.maximum(i * hb - 1, 0), j + col0)),
                  pl.BlockSpec((tr, CONV_CW), lambda j, i: (i, j + col0)),
                  pl.BlockSpec((h, CONV_CW), lambda j, i: (jnp.minimum((i + 1) * hb, nb - 1), j + col0)),
                  pl.BlockSpec((SSM_CONV, CONV_CW), lambda j, i: (0, j)),
                  pl.BlockSpec((1, CONV_CW), lambda j, i: (0, j)),
                  pl.BlockSpec((tr, CONV_CW), lambda j, i: (i, j)),
                  pl.BlockSpec((h, CONV_CW), lambda j, i: (jnp.minimum((i + 1) * hb, nb - 1), j))],
        out_specs=[pl.BlockSpec((tr, CONV_CW), lambda j, i: (i, j)),
                   pl.BlockSpec((SSM_CONV, CONV_CW), lambda j, i: (0, j)),
                   pl.BlockSpec((1, CONV_CW), lambda j, i: (0, j))],
        out_shape=[_sds((s, SSM_CONV_DIM), MXU_DTYPE), _sds((SSM_CONV, SSM_CONV_DIM), F32), _sds((1, SSM_CONV_DIM), F32)],
        compiler_params=_cparams(("parallel", "arbitrary")), name=name,
    )(proj, proj, proj, conv_w, conv_b.reshape(1, SSM_CONV_DIM), dact, dact)


def _scan_tables():
    idx = np.arange(CHUNK, dtype=np.float32)
    lg = np.log1p(-(2.0 ** (-5.0 - np.arange(RET_HEADS, dtype=np.float32)))).astype(np.float32)
    rel = np.abs(idx[:, None] - idx[None, :])
    r_intra = np.exp(lg[:, None, None] * rel).astype(np.float32)
    qd = np.exp(lg[None, :] * (idx[:, None] + 1.0)).astype(np.float32)
    kd = np.exp(lg[None, :] * (CHUNK - 1.0 - idx[:, None])).astype(np.float32)
    gam = [float(v) for v in np.exp(lg * CHUNK).astype(np.float32)]
    qd_e = np.repeat(qd, RET_QK_DIM, axis=1)
    kd_e = np.repeat(kd, RET_QK_DIM, axis=1)
    e = np.zeros((DT_PAD, SSM_INNER), np.float32)
    for hh in range(SSM_HEADS):
        e[hh, hh * SSM_HEAD_DIM:(hh + 1) * SSM_HEAD_DIM] = 1.0
    tri = np.tril(np.ones((CHUNK, CHUNK), np.float32))
    eye2 = np.concatenate([np.eye(CHUNK, dtype=np.float32)] * 2, axis=1)
    bdm = np.kron(np.eye(2, dtype=np.float32), np.ones((CHUNK, CHUNK), np.float32))
    last = np.zeros((CHUNK, LANES), np.float32)
    last[CHUNK - 1, :] = 1.0
    f32c = [jnp.asarray(c) for c in (r_intra, qd_e, kd_e, eye2, bdm, last)]
    sel = [jnp.asarray(c, jnp.bfloat16) for c in (e, e.T.copy(), tri, tri.T.copy())]
    return f32c + sel, gam


def _rope(t, cos2, sin2):
    return t * cos2 + pltpu.roll(t, RET_QK_DIM // 2, axis=1) * sin2


def _rope_t(d, cos2, sin2):
    return d * cos2 + pltpu.roll(d * sin2, RET_QK_DIM // 2, axis=1)


def _ret_step(q, k, v, st, r_intra, qd, kd, gamma):
    k = k * (RET_QK_DIM ** -0.5)
    sc = _mm(q, k, 'nt') * r_intra
    y = _mm(sc, v, 'nn') + _mm(q * qd, st, 'nn')
    st_new = st * gamma + _mm(k * kd, v, 'tn')
    return y, st_new


def _ssd_heads(dtraw, dtb, a_c, tri):
    dt = _softplus(dtraw + dtb)
    return dt, _dot_sel(dt * a_c, tri, left=True)


def _ssd_group(dte0, dte1, cum0, cum1, xs0, xs1, bm, cm, ht0, ht1, eye2, bdm, last):
    cbp = _mm(cm, jnp.concatenate([bm, bm], axis=0), 'nt')
    outs = []
    for dte, cum, xs, ht in ((dte0, cum0, xs0, ht0), (dte1, cum1, xs1, ht1)):
        r = jnp.sum(cum * eye2, axis=0, keepdims=True)
        dlt = cum - r
        seg = jnp.exp(jnp.where(dlt > 0.0, -dlt, dlt))
        xdt = xs * dte
        bd = jnp.concatenate([xdt, xdt], axis=0) * bdm
        clast = jnp.sum(cum * last, axis=0, keepdims=True)
        y = _mm(cbp * seg, bd, 'nn') + jnp.exp(cum) * _mm(cm, ht, 'nn')
        ht_new = jnp.exp(clast) * ht + _mm(bm, xdt * jnp.exp(clast - cum), 'tn')
        outs += [y, ht_new]
    return tuple(outs)


SCAN_SUB = 2


def _scan_in_specs(nb, rev, rows):
    ch = (lambda c: nb - 1 - c) if rev else (lambda c: c)
    col = lambda w, blk: pl.BlockSpec((rows, w), lambda c: (ch(c), blk))
    return [col(RET_QK, COL_Q // RET_QK), col(RET_QK, COL_K // RET_QK), col(RET_V, COL_V // RET_V),
            col(SSM_INNER, 0), col(SSM_BC, 2), col(SSM_BC, 3),
            col(DT_PAD, 0), col(LANES, 0), col(LANES, 0)]


def _const_specs(consts):
    return [_full(c.shape) for c in consts]


def _tile(t):
    return slice(t * LANES, (t + 1) * LANES)


def _scan_sub(s):
    return SCAN_SUB if (s // CHUNK) % SCAN_SUB == 0 else 1


def _scan_fwd(proj, xbc, dtraw, cos2, sin2, a_c, dtb, name, rider=None):
    s = proj.shape[0]
    nc = s // CHUNK
    sub = _scan_sub(s)
    nb, rows = nc // sub, CHUNK * sub
    consts, gam = _scan_tables()
    r_arrs, r_in, r_out, r_shape, r_scratch = _rider_args(rider)

    def body(q_ref, k_ref, v_ref, xs_ref, bm_ref, cm_ref, dt_ref, cos_ref, sin_ref, ac_ref, dtb_ref,
             ri_ref, qd_ref, kd_ref, eye_ref, bdm_ref, last_ref, e_ref, et_ref, tri_ref, trit_ref,
             yr_ref, ys_ref, sh_ref, hh_ref, st_sc, ht_sc):
        @pl.when(pl.program_id(0) == 0)
        def _():
            st_sc[...] = jnp.zeros_like(st_sc)
            ht_sc[...] = jnp.zeros_like(ht_sc)

        eye2, bdm, last = eye_ref[...], bdm_ref[...], last_ref[...]
        st = [st_sc[h * RET_QK_DIM:(h + 1) * RET_QK_DIM, :] for h in range(RET_HEADS)]
        ht = [ht_sc[:, _tile(t)] for t in range(N_LTILE)]
        for i in range(sub):
            r = slice(i * CHUNK, (i + 1) * CHUNK)
            cos2, sin2 = cos_ref[r, :], sin_ref[r, :]
            for h in range(RET_HEADS):
                ql = slice(h * RET_QK_DIM, (h + 1) * RET_QK_DIM)
                vl = slice(h * RET_V_DIM, (h + 1) * RET_V_DIM)
                sh_ref[i, ql, :] = st[h]
                y, st[h] = _ret_step(_rope(q_ref[r, ql].astype(F32), cos2, sin2), _rope(k_ref[r, ql].astype(F32), cos2, sin2),
                                     v_ref[r, vl].astype(F32), st[h], ri_ref[h], qd_ref[:, ql], kd_ref[:, ql], gam[h])
                yr_ref[r, vl] = y.astype(yr_ref.dtype)
            dt, cum_c = _ssd_heads(dt_ref[r, :], dtb_ref[...], ac_ref[...], tri_ref[...])
            both = jnp.concatenate([dt, cum_c], axis=0)
            for g in range(SSM_GROUPS):
                t0, t1 = 2 * g, 2 * g + 1
                hh_ref[i, :, _tile(t0)] = ht[t0]
                hh_ref[i, :, _tile(t1)] = ht[t1]
                e0, e1 = _dot_sel(both, e_ref[:, _tile(t0)]), _dot_sel(both, e_ref[:, _tile(t1)])
                y0, ht[t0], y1, ht[t1] = _ssd_group(e0[:CHUNK], e1[:CHUNK], e0[CHUNK:], e1[CHUNK:],
                                                    xs_ref[r, _tile(t0)].astype(F32), xs_ref[r, _tile(t1)].astype(F32),
                                                    bm_ref[r, _tile(g)].astype(F32), cm_ref[r, _tile(g)].astype(F32),
                                                    ht[t0], ht[t1], eye2, bdm, last)
                ys_ref[r, _tile(t0)] = y0.astype(ys_ref.dtype)
                ys_ref[r, _tile(t1)] = y1.astype(ys_ref.dtype)
        for h in range(RET_HEADS):
            st_sc[h * RET_QK_DIM:(h + 1) * RET_QK_DIM, :] = st[h]
        for t in range(N_LTILE):
            ht_sc[:, _tile(t)] = ht[t]

    in_specs = _scan_in_specs(nb, False, rows) + [_full((1, DT_PAD)), _full((1, DT_PAD))] + _const_specs(consts)
    return pl.pallas_call(
        _with_rider(body, len(in_specs), 4, 2, rider, (nb,)), grid=(nb,),
        in_specs=in_specs + r_in,
        out_specs=[pl.BlockSpec((rows, RET_V), lambda c: (c, 0)),
                   pl.BlockSpec((rows, SSM_INNER), lambda c: (c, 0)),
                   pl.BlockSpec((sub, RET_QK, RET_V_DIM), lambda c: (c, 0, 0)),
                   pl.BlockSpec((sub, SSM_STATE, SSM_INNER), lambda c: (c, 0, 0))] + r_out,
        out_shape=[_sds((s, RET_V), MXU_DTYPE), _sds((s, SSM_INNER), MXU_DTYPE),
                   _sds((nc, RET_QK, RET_V_DIM), F32), _sds((nc, SSM_STATE, SSM_INNER), F32)] + r_shape,
        scratch_shapes=[pltpu.VMEM((RET_QK, RET_V_DIM), F32), pltpu.VMEM((SSM_STATE, SSM_INNER), F32)] + r_scratch,
        compiler_params=_cparams(("arbitrary",)), name=name,
    )(proj, proj, proj, xbc, xbc, xbc, dtraw, cos2, sin2, a_c, dtb, *consts, *r_arrs)


def _scan_bwd(proj, xbc, dtraw, cos2, sin2, a_c, dtb, s_hist, h_hist, dyr, dys, dxs_skip, name, rider=None):
    s = proj.shape[0]
    nc = s // CHUNK
    sub = _scan_sub(s)
    nb, rows = nc // sub, CHUNK * sub
    consts, gam = _scan_tables()
    rv = lambda c: nb - 1 - c
    r_arrs, r_in, r_out, r_shape, r_scratch = _rider_args(rider)

    def body(q_ref, k_ref, v_ref, xs_ref, bm_ref, cm_ref, dt_ref, cos_ref, sin_ref, ac_ref, dtb_ref,
             ri_ref, qd_ref, kd_ref, eye_ref, bdm_ref, last_ref, e_ref, et_ref, tri_ref, trit_ref,
             sh_ref, hh_ref, dyr_ref, dys_ref, dsk_ref,
             dqk_ref, dv_ref, dxbc_ref, ddt_ref, dac_ref, ddtb_ref, dst_sc, dht_sc):
        @pl.when(pl.program_id(0) == 0)
        def _():
            dst_sc[...] = jnp.zeros_like(dst_sc)
            dht_sc[...] = jnp.zeros_like(dht_sc)
            dac_ref[...] = jnp.zeros_like(dac_ref)
            ddtb_ref[...] = jnp.zeros_like(ddtb_ref)

        dtb_v, a_c, tri = dtb_ref[...], ac_ref[...], tri_ref[...]
        eye2, bdm, last = eye_ref[...], bdm_ref[...], last_ref[...]
        group = functools.partial(_ssd_group, eye2=eye2, bdm=bdm, last=last)
        dst = [dst_sc[h * RET_QK_DIM:(h + 1) * RET_QK_DIM, :] for h in range(RET_HEADS)]
        dht = [dht_sc[:, _tile(t)] for t in range(N_LTILE)]
        ddtb = jnp.zeros((1, DT_PAD), F32)
        dac = jnp.zeros((1, DT_PAD), F32)
        for i in reversed(range(sub)):
            r = slice(i * CHUNK, (i + 1) * CHUNK)
            cos2, sin2 = cos_ref[r, :], sin_ref[r, :]
            for h in range(RET_HEADS):
                ql = slice(h * RET_QK_DIM, (h + 1) * RET_QK_DIM)
                vl = slice(h * RET_V_DIM, (h + 1) * RET_V_DIM)
                step = functools.partial(_ret_step, r_intra=ri_ref[h], qd=qd_ref[:, ql], kd=kd_ref[:, ql], gamma=gam[h])
                _, vjp = jax.vjp(step, _rope(q_ref[r, ql].astype(F32), cos2, sin2), _rope(k_ref[r, ql].astype(F32), cos2, sin2),
                                 v_ref[r, vl].astype(F32), sh_ref[i, ql, :])
                dq, dk, dv, dst[h] = vjp((dyr_ref[r, vl].astype(F32), dst[h]))
                dqk_ref[r, ql] = _rope_t(dq, cos2, sin2).astype(dqk_ref.dtype)
                dqk_ref[r, slice(RET_QK + ql.start, RET_QK + ql.stop)] = _rope_t(dk, cos2, sin2).astype(dqk_ref.dtype)
                dv_ref[r, vl] = dv.astype(dv_ref.dtype)
            dtraw_v = dt_ref[r, :]
            dt, cum_c = _ssd_heads(dtraw_v, dtb_v, a_c, tri)
            both = jnp.concatenate([dt, cum_c], axis=0)
            d_both = jnp.zeros((2 * CHUNK, LANES), F32)
            for g in range(SSM_GROUPS):
                t0, t1 = 2 * g, 2 * g + 1
                e0, e1 = _dot_sel(both, e_ref[:, _tile(t0)]), _dot_sel(both, e_ref[:, _tile(t1)])
                _, vjp = jax.vjp(group, e0[:CHUNK], e1[:CHUNK], e0[CHUNK:], e1[CHUNK:],
                                 xs_ref[r, _tile(t0)].astype(F32), xs_ref[r, _tile(t1)].astype(F32),
                                 bm_ref[r, _tile(g)].astype(F32), cm_ref[r, _tile(g)].astype(F32),
                                 hh_ref[i, :, _tile(t0)], hh_ref[i, :, _tile(t1)])
                (d_dte0, d_dte1, d_cum0, d_cum1, d_xs0, d_xs1, d_bm, d_cm, dht[t0], dht[t1]) = vjp(
                    (dys_ref[r, _tile(t0)].astype(F32), dht[t0], dys_ref[r, _tile(t1)].astype(F32), dht[t1]))
                d_both = d_both + _dot_sel(jnp.concatenate([d_dte0, d_cum0], axis=0), et_ref[_tile(t0), :])
                d_both = d_both + _dot_sel(jnp.concatenate([d_dte1, d_cum1], axis=0), et_ref[_tile(t1), :])
                dxbc_ref[r, _tile(t0)] = d_xs0 + dsk_ref[r, _tile(t0)].astype(F32)
                dxbc_ref[r, _tile(t1)] = d_xs1 + dsk_ref[r, _tile(t1)].astype(F32)
                dxbc_ref[r, _tile(N_LTILE + g)] = d_bm
                dxbc_ref[r, _tile(N_LTILE + SSM_GROUPS + g)] = d_cm
            d_da = _dot_sel(d_both[CHUNK:], trit_ref[...], left=True)
            d_dt = d_both[:CHUNK] + d_da * a_c
            d_pre = d_dt * jax.nn.sigmoid(dtraw_v + dtb_v)
            ddt_ref[r, :] = d_pre
            ddtb = ddtb + jnp.sum(d_pre, axis=0, keepdims=True)
            dac = dac + jnp.sum(d_da * dt, axis=0, keepdims=True)
        ddtb_ref[...] += ddtb
        dac_ref[...] += dac
        for h in range(RET_HEADS):
            dst_sc[h * RET_QK_DIM:(h + 1) * RET_QK_DIM, :] = dst[h]
        for t in range(N_LTILE):
            dht_sc[:, _tile(t)] = dht[t]

    in_specs = (_scan_in_specs(nb, True, rows) + [_full((1, DT_PAD)), _full((1, DT_PAD))] + _const_specs(consts)
                + [pl.BlockSpec((sub, RET_QK, RET_V_DIM), lambda c: (rv(c), 0, 0)),
                   pl.BlockSpec((sub, SSM_STATE, SSM_INNER), lambda c: (rv(c), 0, 0)),
                   pl.BlockSpec((rows, RET_V), lambda c: (rv(c), 0)),
                   pl.BlockSpec((rows, SSM_INNER), lambda c: (rv(c), 0)),
                   pl.BlockSpec((rows, SSM_INNER), lambda c: (rv(c), 0))])
    return pl.pallas_call(
        _with_rider(body, len(in_specs), 6, 2, rider, (nb,)), grid=(nb,),
        in_specs=in_specs + r_in,
        out_specs=[pl.BlockSpec((rows, 2 * RET_QK), lambda c: (rv(c), 0)),
                   pl.BlockSpec((rows, RET_V), lambda c: (rv(c), 0)),
                   pl.BlockSpec((rows, SSM_CONV_DIM), lambda c: (rv(c), 0)),
                   pl.BlockSpec((rows, DT_PAD), lambda c: (rv(c), 0)),
                   _full((1, DT_PAD)), _full((1, DT_PAD))] + r_out,
        out_shape=[_sds((s, 2 * RET_QK), MXU_DTYPE), _sds((s, RET_V), MXU_DTYPE),
                   _sds((s, SSM_CONV_DIM), F32), _sds((s, DT_PAD), F32),
                   _sds((1, DT_PAD), F32), _sds((1, DT_PAD), F32)] + r_shape,
        scratch_shapes=[pltpu.VMEM((RET_QK, RET_V_DIM), F32), pltpu.VMEM((SSM_STATE, SSM_INNER), F32)] + r_scratch,
        compiler_params=_cparams(("arbitrary",)), name=name,
    )(proj, proj, proj, xbc, xbc, xbc, dtraw, cos2, sin2, a_c, dtb, *consts, s_hist, h_hist, dyr, dys, dxs_skip, *r_arrs)


POST_W = 256


def _post_ret(y, g):
    return _rms(y) * _silu(g)


def _post_ssm(y, xs, z, dsk, nw):
    return _rms((y + xs * dsk) * _silu(z)) * nw


def _post_specs(t):
    return [pl.BlockSpec((t, RET_V), lambda i: (i, 0)),
            pl.BlockSpec((t, RET_V), lambda i: (i, COL_G // RET_V)),
            pl.BlockSpec((t, SSM_INNER), lambda i: (i, 0)),
            pl.BlockSpec((t, SSM_INNER), lambda i: (i, 0)),
            pl.BlockSpec((t, SSM_INNER), lambda i: (i, COL_Z // SSM_INNER)),
            _full((1, SSM_INNER)), _full((1, SSM_INNER))]


def _post_fwd(y_ret, proj, y_ssm, xbc, dsk_e, ssm_norm, name):
    s = y_ret.shape[0]
    t = _pick(s, 256)

    def body(yr_ref, g_ref, ys_ref, xs_ref, z_ref, dsk_ref, nw_ref, or_ref, os_ref, ort_ref, ost_ref):
        for h in range(RET_V // POST_W):
            sl = slice(h * POST_W, (h + 1) * POST_W)
            o = _post_ret(yr_ref[:, sl].astype(F32), g_ref[:, sl].astype(F32))
            or_ref[:, sl] = o.astype(or_ref.dtype)
            ort_ref[sl, :] = o.T.astype(ort_ref.dtype)
        for g in range(SSM_INNER // POST_W):
            sl = slice(g * POST_W, (g + 1) * POST_W)
            o = _post_ssm(ys_ref[:, sl].astype(F32), xs_ref[:, sl].astype(F32), z_ref[:, sl].astype(F32), dsk_ref[:, sl],
                          nw_ref[:, sl])
            os_ref[:, sl] = o.astype(os_ref.dtype)
            ost_ref[sl, :] = o.T.astype(ost_ref.dtype)

    return pl.pallas_call(
        body, grid=(s // t,), in_specs=_post_specs(t),
        out_specs=[pl.BlockSpec((t, RET_V), lambda i: (i, 0)), pl.BlockSpec((t, SSM_INNER), lambda i: (i, 0)),
                   pl.BlockSpec((RET_V, t), lambda i: (0, i)), pl.BlockSpec((SSM_INNER, t), lambda i: (0, i))],
        out_shape=[_sds((s, RET_V), MXU_DTYPE), _sds((s, SSM_INNER), MXU_DTYPE),
                   _sds((RET_V, s), MXU_DTYPE), _sds((SSM_INNER, s), MXU_DTYPE)],
        compiler_params=_cparams(("parallel",)), name=name,
    )(y_ret, proj, y_ssm, xbc, proj, dsk_e, ssm_norm.reshape(1, SSM_INNER))


def _post_bwd(y_ret, proj, y_ssm, xbc, dsk_e, ssm_norm, d_or, d_os, name, rider=None):
    s = y_ret.shape[0]
    t = _pick(s, 256)
    r_arrs, r_in, r_out, r_shape, r_scratch = _rider_args(rider)

    def body(yr_ref, g_ref, ys_ref, xs_ref, z_ref, dsk_ref, nw_ref, dor_ref, dos_ref,
             dyr_ref, dg_ref, dys_ref, dxs_ref, dz_ref, ddsk_ref, dnw_ref):
        @pl.when(pl.program_id(0) == 0)
        def _():
            ddsk_ref[...] = jnp.zeros_like(ddsk_ref)
            dnw_ref[...] = jnp.zeros_like(dnw_ref)

        for h in range(RET_V // POST_W):
            sl = slice(h * POST_W, (h + 1) * POST_W)
            _, vjp = jax.vjp(_post_ret, yr_ref[:, sl].astype(F32), g_ref[:, sl].astype(F32))
            dyr, dg = vjp(dor_ref[:, sl].astype(F32))
            dyr_ref[:, sl] = dyr.astype(dyr_ref.dtype)
            dg_ref[:, sl] = dg.astype(dg_ref.dtype)
        for g in range(SSM_INNER // POST_W):
            sl = slice(g * POST_W, (g + 1) * POST_W)
            _, vjp = jax.vjp(_post_ssm, ys_ref[:, sl].astype(F32), xs_ref[:, sl].astype(F32), z_ref[:, sl].astype(F32),
                             dsk_ref[:, sl], nw_ref[:, sl])
            dy, dxs, dz, ddsk, dnw = vjp(dos_ref[:, sl].astype(F32))
            dys_ref[:, sl] = dy.astype(dys_ref.dtype)
            dxs_ref[:, sl] = dxs.astype(dxs_ref.dtype)
            dz_ref[:, sl] = dz.astype(dz_ref.dtype)
            ddsk_ref[:, sl] += ddsk
            dnw_ref[:, sl] += dnw

    rowv = pl.BlockSpec((t, RET_V), lambda i: (i, 0))
    rows = pl.BlockSpec((t, SSM_INNER), lambda i: (i, 0))
    in_specs = _post_specs(t) + [rowv, rows]
    return pl.pallas_call(
        _with_rider(body, len(in_specs), 7, 0, rider, (s // t,)), grid=(s // t,), in_specs=in_specs + r_in,
        out_specs=[rowv, rowv, rows, rows, rows, _full((1, SSM_INNER)), _full((1, SSM_INNER))] + r_out,
        out_shape=[_sds((s, RET_V), MXU_DTYPE), _sds((s, RET_V), MXU_DTYPE), _sds((s, SSM_INNER), MXU_DTYPE),
                   _sds((s, SSM_INNER), MXU_DTYPE), _sds((s, SSM_INNER), MXU_DTYPE),
                   _sds((1, SSM_INNER), F32), _sds((1, SSM_INNER), F32)] + r_shape,
        scratch_shapes=r_scratch,
        compiler_params=_cparams(("arbitrary",)), name=name,
    )(y_ret, proj, y_ssm, xbc, proj, dsk_e, ssm_norm.reshape(1, SSM_INNER), d_or, d_os, *r_arrs)


def _merge_fn(gr, gs, br, bs, yr, ys):
    return jax.nn.sigmoid(gr + br) * yr + jax.nn.sigmoid(gs + bs) * ys


def _merge_specs(t):
    row = pl.BlockSpec((t, D_MODEL), lambda i: (i, 0))
    return [pl.BlockSpec((t, D_MODEL), lambda i: (i, COL_GATES // D_MODEL)),
            pl.BlockSpec((t, D_MODEL), lambda i: (i, COL_GATES // D_MODEL + 1)),
            pl.BlockSpec((1, D_MODEL), lambda i: (0, 0)), pl.BlockSpec((1, D_MODEL), lambda i: (0, 1)), row, row]


def _merge_fwd(proj, b_gate, br_ret, br_ssm, name):
    s = proj.shape[0]
    t = _pick(s, 512)

    def body(gr_ref, gs_ref, br_ref, bs_ref, yr_ref, ys_ref, o_ref, ot_ref):
        o = _merge_fn(gr_ref[...].astype(F32), gs_ref[...].astype(F32), br_ref[...], bs_ref[...], yr_ref[...], ys_ref[...])
        o_ref[...] = o.astype(o_ref.dtype)
        ot_ref[...] = o.T.astype(ot_ref.dtype)

    bg = b_gate.reshape(1, 2 * D_MODEL)
    return pl.pallas_call(
        body, grid=(s // t,), in_specs=_merge_specs(t),
        out_specs=[pl.BlockSpec((t, D_MODEL), lambda i: (i, 0)), pl.BlockSpec((D_MODEL, t), lambda i: (0, i))],
        out_shape=[_sds((s, D_MODEL), MXU_DTYPE), _sds((D_MODEL, s), MXU_DTYPE)],
        compiler_params=_cparams(("parallel",)), name=name,
    )(proj, proj, bg, bg, br_ret, br_ssm)


def _merge_bwd(proj, b_gate, br_ret, br_ssm, dm, name):
    s = proj.shape[0]
    t = _pick(s, 512)

    def body(gr_ref, gs_ref, br_ref, bs_ref, yr_ref, ys_ref, dm_ref, dgt_ref, db_ref, dyr_ref, dys_ref):
        @pl.when(pl.program_id(0) == 0)
        def _():
            db_ref[...] = jnp.zeros_like(db_ref)

        _, vjp = jax.vjp(_merge_fn, gr_ref[...].astype(F32), gs_ref[...].astype(F32), br_ref[...], bs_ref[...],
                         yr_ref[...], ys_ref[...])
        dgr, dgs, dbr, dbs, dyr, dys = vjp(dm_ref[...])
        dgt_ref[:, :D_MODEL] = dgr.astype(dgt_ref.dtype)
        dgt_ref[:, D_MODEL:] = dgs.astype(dgt_ref.dtype)
        db_ref[:, :D_MODEL] += dbr
        db_ref[:, D_MODEL:] += dbs
        dyr_ref[...] = dyr.astype(dyr_ref.dtype)
        dys_ref[...] = dys.astype(dys_ref.dtype)

    bg = b_gate.reshape(1, 2 * D_MODEL)
    row = pl.BlockSpec((t, D_MODEL), lambda i: (i, 0))
    return pl.pallas_call(
        body, grid=(s // t,), in_specs=_merge_specs(t) + [row],
        out_specs=[pl.BlockSpec((t, 2 * D_MODEL), lambda i: (i, 0)), _full((1, 2 * D_MODEL)), row, row],
        out_shape=[_sds((s, 2 * D_MODEL), MXU_DTYPE), _sds((1, 2 * D_MODEL), F32),
                   _sds((s, D_MODEL), MXU_DTYPE), _sds((s, D_MODEL), MXU_DTYPE)],
        compiler_params=_cparams(("arbitrary",)), name=name,
    )(proj, proj, bg, bg, br_ret, br_ssm, dm)


def _attn_head(q, k, v):
    sc = _mm(q, k, 'nt') * (XA_HEAD_DIM ** -0.5)
    e = jnp.exp(sc - lax.stop_gradient(jnp.max(sc, axis=-1, keepdims=True)))
    p = e / jnp.sum(e, axis=-1, keepdims=True)
    return _mm(p, v, 'nn')


def _attn_fwd(q, kv, name):
    s = q.shape[0]
    m = kv.shape[0]
    t = _pick(s, 512)

    def body(q_ref, kv_ref, o_ref, ot_ref):
        for h in range(XA_HEADS):
            sl = slice(h * XA_HEAD_DIM, (h + 1) * XA_HEAD_DIM)
            vl = slice(D_MODEL + h * XA_HEAD_DIM, D_MODEL + (h + 1) * XA_HEAD_DIM)
            o = _attn_head(q_ref[:, sl], kv_ref[:, sl], kv_ref[:, vl])
            o_ref[:, sl] = o.astype(o_ref.dtype)
            ot_ref[sl, :] = o.T.astype(ot_ref.dtype)

    return pl.pallas_call(
        body, grid=(s // t,),
        in_specs=[pl.BlockSpec((t, D_MODEL), lambda i: (i, 0)), _full((m, 2 * D_MODEL))],
        out_specs=[pl.BlockSpec((t, D_MODEL), lambda i: (i, 0)), pl.BlockSpec((D_MODEL, t), lambda i: (0, i))],
        out_shape=[_sds((s, D_MODEL), MXU_DTYPE), _sds((D_MODEL, s), MXU_DTYPE)],
        compiler_params=_cparams(("parallel",)), name=name,
    )(q, kv)


def _attn_bwd(q, kv, d_o, name):
    s = q.shape[0]
    m = kv.shape[0]
    t = _pick(s, 512)

    def body(q_ref, kv_ref, do_ref, dq_ref, dkv_ref):
        @pl.when(pl.program_id(0) == 0)
        def _():
            dkv_ref[...] = jnp.zeros_like(dkv_ref)

        for h in range(XA_HEADS):
            sl = slice(h * XA_HEAD_DIM, (h + 1) * XA_HEAD_DIM)
            vl = slice(D_MODEL + h * XA_HEAD_DIM, D_MODEL + (h + 1) * XA_HEAD_DIM)
            _, vjp = jax.vjp(_attn_head, q_ref[:, sl], kv_ref[:, sl], kv_ref[:, vl])
            dq, dk, dv = vjp(do_ref[:, sl])
            dq_ref[:, sl] = dq.astype(dq_ref.dtype)
            dkv_ref[:, sl] += dk
            dkv_ref[:, vl] += dv

    row = pl.BlockSpec((t, D_MODEL), lambda i: (i, 0))
    return pl.pallas_call(
        body, grid=(s // t,), in_specs=[row, _full((m, 2 * D_MODEL)), row],
        out_specs=[row, _full((m, 2 * D_MODEL))],
        out_shape=[_sds((s, D_MODEL), MXU_DTYPE), _sds((m, 2 * D_MODEL), F32)],
        compiler_params=_cparams(("arbitrary",)), name=name,
    )(q, kv, d_o)


def _loss_head(x, w, target, name):
    s, d = x.shape
    t = _pick(s, 512)

    def body(x_ref, w_ref, t_ref, loss_ref, dx_ref, dxb_ref, dw_ref):
        @pl.when(pl.program_id(0) == 0)
        def _():
            loss_ref[...] = jnp.zeros_like(loss_ref)
            dw_ref[...] = jnp.zeros_like(dw_ref)

        y, vjp = jax.vjp(_rmsnorm_fn, x_ref[...], w_ref[...])
        err = y - t_ref[...]
        loss_ref[...] += 0.5 * jnp.sum(jnp.sum(err * err, axis=-1, keepdims=True), axis=0, keepdims=True) / d
        dx, dw = vjp(err * (1.0 / d))
        dx_ref[...] = dx
        dxb_ref[...] = dx.astype(dxb_ref.dtype)
        dw_ref[...] += dw

    row = pl.BlockSpec((t, d), lambda i: (i, 0))
    return pl.pallas_call(
        body, grid=(s // t,), in_specs=[row, _full((1, d)), row],
        out_specs=[_full((1, LANES)), row, row, _full((1, d))],
        out_shape=[_sds((1, LANES), F32), _sds((s, d), F32), _sds((s, d), MXU_DTYPE), _sds((1, d), F32)],
        compiler_params=_cparams(("arbitrary",)), name=name,
    )(x, w.reshape(1, d), target)


def _epi_sqrelu(acc):
    r = jnp.maximum(acc, 0.0)
    return r * r, r * r


def _epi_sqrelu_bwd(acc, act):
    return (acc * (2.0 * jnp.sqrt(act.astype(F32))),)


def _rope_tables(positions):
    inv_freq = ROPE_THETA ** (-jnp.arange(0, RET_QK_DIM, 2, dtype=F32) / RET_QK_DIM)
    ang = positions.astype(F32)[:, None] * inv_freq
    cos, sin = jnp.cos(ang), jnp.sin(ang)
    return jnp.concatenate([cos, cos], axis=1), jnp.concatenate([-sin, sin], axis=1)


W_IN_ORIG = (('q', 0, 512), ('k', 512, 1024), ('v', 1024, 2048), ('g', 2048, 3072), ('z', 3072, 5120),
             ('xbc', 5120, 9216), ('dt', 9216, 9248), ('gates', 9248, 11296))
W_IN_MAIN_ORDER = ('z', 'xbc', 'gates', 'v', 'g', 'q', 'k')
W_IN_SHARD = IN_DIM // N_DEV


def _shard_segments(lo, hi):
    segs = []
    for j in range(lo // W_IN_SHARD, (hi - 1) // W_IN_SHARD + 1):
        segs.append((j, max(lo, j * W_IN_SHARD) - j * W_IN_SHARD, min(hi, (j + 1) * W_IN_SHARD) - j * W_IN_SHARD))
    return segs


def _w_in_from_shards(g):
    rng = {name: (lo, hi) for name, lo, hi in W_IN_ORIG}
    cols = [g[j][:, a:b] for name in W_IN_MAIN_ORDER for j, a, b in _shard_segments(*rng[name])]
    (j, a, b), = _shard_segments(*rng['dt'])
    return jnp.concatenate(cols, axis=1), jnp.pad(g[j][:, a:b], ((0, 0), (0, DT_PAD - SSM_HEADS)))


def _w_in_grad_blocks(d, d_dt):
    src_of = {'q': ('qk', 0), 'k': ('qk', RET_QK)}
    blocks = []
    for j in range(N_DEV):
        lo_j, hi_j = j * W_IN_SHARD, (j + 1) * W_IN_SHARD
        cols = []
        for name, lo, hi in W_IN_ORIG:
            a, b = max(lo, lo_j), min(hi, hi_j)
            if a >= b:
                continue
            if name == 'dt':
                cols.append(d_dt[:, a - lo:b - lo])
            else:
                key, off = src_of.get(name, (name, 0))
                cols.append(d[key][:, off + a - lo:off + b - lo])
        blocks.append(jnp.concatenate(cols, axis=1))
    return blocks


def _lanes_of_heads(v):
    return jnp.repeat(v, SSM_HEAD_DIM).reshape(1, SSM_INNER)


def _heads_of_lanes(v):
    return v.reshape(SSM_HEADS, SSM_HEAD_DIM).sum(axis=1)


def _layer_fwd(x, mem, cos2, sin2, p, l, later_blocks, rider_proj=None):
    n = lambda s: f"{s}_l{l}"
    sv = {'x0': x}
    u, u_t = _rmsnorm(x, p['norm_mix'], n("norm_mix"))
    s = x.shape[0]
    proj = _matmul(u, p['w_in_main'], 'nn', n("in_proj"), out_dtypes=(MXU_DTYPE,), tiles=(s, 512, D_MODEL), rider=rider_proj)
    if rider_proj is not None:
        proj, next_first = proj
    else:
        next_first = []
    dtraw = _matmul(u, p['w_in_dt'], 'nn', n("in_proj_dt"))
    xbc = _conv_fwd(proj, p['conv_w'], p['conv_b'], n("conv"))
    a_c = jnp.pad(-jnp.exp(p['a_log']), (0, DT_PAD - SSM_HEADS)).reshape(1, DT_PAD)
    dtb = jnp.pad(p['dt_bias'], (0, DT_PAD - SSM_HEADS)).reshape(1, DT_PAD)
    y_ret, y_ssm, s_hist, h_hist, *gathered = _scan_fwd(proj, xbc, dtraw, cos2, sin2, a_c, dtb, n("scan"),
                                                        _AllGather(later_blocks))
    p = {**p, **_full_weights(LATER, gathered)}
    dsk_e = _lanes_of_heads(p['d_skip'])
    o_ret, o_ssm, o_ret_t, o_ssm_t = _post_fwd(y_ret, proj, y_ssm, xbc, dsk_e, p['ssm_norm'], n("post"))
    br_ret = _matmul(o_ret, p['w_br_ret'], 'nn', n("br_ret"))
    br_ssm = _matmul(o_ssm, p['w_br_ssm'], 'nn', n("br_ssm"))
    merged, merged_t = _merge_fwd(proj, p['b_gate'], br_ret, br_ssm, n("merge"))
    x1 = _matmul(merged, p['w_out'], 'nn', n("w_out"), extras=(x,), epi=_epi_add)
    sv.update(u_t=u_t, proj=proj, dtraw=dtraw, xbc=xbc, a_c=a_c, dtb=dtb, y_ret=y_ret, y_ssm=y_ssm, s_hist=s_hist,
              h_hist=h_hist, dsk_e=dsk_e, o_ret_t=o_ret_t, o_ssm_t=o_ssm_t, br_ret=br_ret, br_ssm=br_ssm,
              merged_t=merged_t, x1=x1)
    hq, hq_t = _rmsnorm(x1, p['norm_xa'], n("norm_xa"))
    memn, _ = _rmsnorm(mem, p['norm_mem'], n("norm_mem"))
    q = _matmul(hq, p['xa_wq'], 'nn', n("xa_q"))
    kv = _matmul(memn, p['xa_wkv'], 'nn', n("xa_kv"))
    o, o_t = _attn_fwd(q, kv, n("attn"))
    x2 = _matmul(o, p['xa_wo'], 'nn', n("xa_o"), extras=(x1,), epi=_epi_add)
    sv.update(hq_t=hq_t, memn=memn, q=q, kv=kv, o_t=o_t, x2=x2)
    hm, hm_t = _rmsnorm(x2, p['norm_mlp'], n("norm_mlp"))
    act, act_t = _matmul(hm, p['mlp_w1'], 'nn', n("mlp_1"), epi=_epi_sqrelu, out_dtypes=(MXU_DTYPE, MXU_DTYPE),
                         out_t=(False, True), tiles=(s, 512, D_MODEL))
    x3 = _matmul(act, p['mlp_w2'], 'nn', n("mlp_2"), extras=(x2,), epi=_epi_add, tiles=(_pick(s, 1024), D_MODEL, D_FF))
    sv.update(hm_t=hm_t, act=act, act_t=act_t)
    return x3, sv, p, next_first


def _layer_bwd(dx, dxb, mem, cos2, sin2, p, sv, l, pending, c_idx):
    n = lambda s: f"{s}_bwd_l{l}"
    gd = (MXU_DTYPE,)
    g = {}
    g['mlp_w2'] = _matmul(sv['act_t'], dxb, 'nn', n("mlp_2_dw"), out_dtypes=gd, scatter='rows')
    da = _matmul(dxb, p['mlp_w2'], 'nt', n("mlp_2_dx"), extras=(sv['act'],), epi=_epi_sqrelu_bwd, out_dtypes=(MXU_DTYPE,),
                 tiles=(_pick(dxb.shape[0], 512), D_FF, D_MODEL))
    g['mlp_w1'] = _matmul(sv['hm_t'], da, 'nn', n("mlp_1_dw"), out_dtypes=gd, scatter='cols')
    dhm = _matmul(da, p['mlp_w1'], 'nt', n("mlp_1_dx"))
    dx2, dx2b, g['norm_mlp'] = _rmsnorm_bwd(sv['x2'], p['norm_mlp'], dhm, dx, n("norm_mlp"))
    g['xa_wo'] = _matmul(sv['o_t'], dx2b, 'nn', n("xa_o_dw"), out_dtypes=gd, scatter='rows')
    d_o = _matmul(dx2b, p['xa_wo'], 'nt', n("xa_o_dx"))
    dq, dkv = _attn_bwd(sv['q'], sv['kv'], d_o, n("attn"))
    g['xa_wq'] = _matmul(sv['hq_t'], dq, 'nn', n("xa_q_dw"), out_dtypes=gd, scatter='rows')
    dhq = _matmul(dq, p['xa_wq'], 'nt', n("xa_q_dx"))
    g['xa_wkv'] = _matmul(sv['memn'], dkv, 'tn', n("xa_kv_dw"), out_dtypes=gd, scatter='cols')
    dmemn = _matmul(dkv, p['xa_wkv'], 'nt', n("xa_kv_dx"))
    _, _, g['norm_mem'] = _rmsnorm_bwd(mem, p['norm_mem'], dmemn, None, n("norm_mem"))
    dx1, dx1b, g['norm_xa'] = _rmsnorm_bwd(sv['x1'], p['norm_xa'], dhq, dx2, n("norm_xa"))
    g['w_out'] = _matmul(sv['merged_t'], dx1b, 'nn', n("w_out_dw"), out_dtypes=gd, scatter='rows')
    dmerged = _matmul(dx1b, p['w_out'], 'nt', n("w_out_dx"))
    dgates, g['b_gate'], dbr_ret, dbr_ssm = _merge_bwd(sv['proj'], p['b_gate'], sv['br_ret'], sv['br_ssm'], dmerged, n("merge"))
    g['w_br_ret'] = _matmul(sv['o_ret_t'], dbr_ret, 'nn', n("br_ret_dw"), out_dtypes=gd, scatter='rows')
    g['w_br_ssm'] = _matmul(sv['o_ssm_t'], dbr_ssm, 'nn', n("br_ssm_dw"), out_dtypes=gd, scatter='rows')
    d_or = _matmul(dbr_ret, p['w_br_ret'], 'nt', n("br_ret_dx"), out_dtypes=(MXU_DTYPE,))
    d_os = _matmul(dbr_ssm, p['w_br_ssm'], 'nt', n("br_ssm_dx"), out_dtypes=(MXU_DTYPE,))
    later_by_core = [_grad_scatter(k, g[k]) for k in LATER]
    dyr, dg, dys, dxs_skip, dz, ddsk_e, g['ssm_norm'], *later_sib = _post_bwd(
        sv['y_ret'], sv['proj'], sv['y_ssm'], sv['xbc'], sv['dsk_e'], p['ssm_norm'], d_or, d_os, n("post"),
        _ExchangeCores(later_by_core))
    rider = _ExchangeChips(list(pending) + _core_sums(LATER, later_by_core, later_sib, c_idx, l))
    g['d_skip'] = _heads_of_lanes(ddsk_e)
    dqk_r, dv_r, dxbc_act, ddtraw, dac, ddtb, *delivered = _scan_bwd(
        sv['proj'], sv['xbc'], sv['dtraw'], cos2, sin2, sv['a_c'], sv['dtb'], sv['s_hist'], sv['h_hist'],
        dyr, dys, dxs_skip, n("scan"), rider)
    g['a_log'] = dac[0, :SSM_HEADS] * (-jnp.exp(p['a_log']))
    g['dt_bias'] = ddtb[0, :SSM_HEADS]
    dxbc_raw, g['conv_w'], g['conv_b'] = _conv_bwd(sv['proj'], p['conv_w'], p['conv_b'], dxbc_act, n("conv"))
    pieces = {'z': dz, 'xbc': dxbc_raw, 'gates': dgates, 'v': dv_r, 'g': dg, 'qk': dqk_r}
    d_w = {k: _matmul(sv['u_t'], pc, 'nn', n(f"in_proj_dw_{k}"), out_dtypes=gd) for k, pc in pieces.items()}
    d_dt = _matmul(sv['u_t'], ddtraw, 'nn', n("in_proj_dt_dw"), out_dtypes=gd)
    g['w_in'] = _w_in_grad_blocks(d_w, d_dt)
    du_dt = _matmul(ddtraw, p['w_in_dt'], 'nt', n("in_proj_dt_dx"))
    first_by_core = [_grad_scatter(k, g[k]) for k in FIRST]
    du, first_sib = _matmul_nt_pieces(list(pieces.values()), p['w_in_main'], n("in_proj_dx"), extras=(du_dt,), epi=_epi_add,
                                      rider=_ExchangeCores(first_by_core))
    dx0, dx0b, g['norm_mix'] = _rmsnorm_bwd(sv['x0'], p['norm_mix'], du, dx1, n("norm_mix"))
    return (dx0, dx0b, g, delivered[:len(pending)], delivered[len(pending):],
            _core_sums(FIRST, first_by_core, first_sib, c_idx, l))


def _full_weights(names, gathered):
    p = {}
    for k, g in zip(names, gathered):
        if k == 'w_in':
            p['w_in_main'], p['w_in_dt'] = _w_in_from_shards(g)
        elif k in COL_SHARDED:
            p[k] = jnp.concatenate([g[j] for j in range(N_DEV)], axis=1)
        else:
            p[k] = g.reshape(-1, g.shape[-1])
    return p


def _grad_scatter(k, g):
    if k in LATER:
        return g
    if k == 'w_in':
        blocks = g
    elif k in COL_SHARDED:
        c = g.shape[1] // N_DEV
        blocks = [g[:, j * c:(j + 1) * c] for j in range(N_DEV)]
    else:
        r = g.shape[0] // N_DEV
        blocks = [g[j * r:(j + 1) * r] for j in range(N_DEV)]
    return jnp.stack([jnp.stack([blocks[2 * chip + core] for chip in range(4)]) for core in range(2)])


def _core_sums(names, by_core, from_sibling, c_idx, l):
    return [_add_halves(a, o, c_idx, f"grad_add_cores_{k}_l{l}") for k, a, o in zip(names, by_core, from_sibling)]


def _step(x, mem, positions, small, blocks, loss_target):
    cos2, sin2 = _rope_tables(positions)
    first = _run_exchange(_AllGather([blocks[0][k] for k in FIRST]), "all_gather_first_l0")
    saved, layers = [], []
    for l in range(DEPTH):
        p = {k: small[k][l] for k in SMALL if k != 'norm_final'}
        p.update(_full_weights(FIRST, first))
        rider = _AllGather([blocks[l + 1][k] for k in FIRST]) if l + 1 < DEPTH else None
        x, sv, p, first = _layer_fwd(x, mem, cos2, sin2, p, l, [blocks[l][k] for k in LATER], rider)
        saved.append(sv)
        layers.append(p)
    loss, dx, dxb, dnf = _loss_head(x, small['norm_final'], loss_target, "loss_head")
    c_idx = lax.axis_index("c").astype(jnp.int32).reshape(1)
    grads, by_chip, pending = [None] * DEPTH, [dict() for _ in range(DEPTH)], []
    for l in reversed(range(DEPTH)):
        dx, dxb, grads[l], got_first, got_later, pending_next = _layer_bwd(
            dx, dxb, mem, cos2, sin2, layers[l], saved[l], l, pending, c_idx)
        if pending:
            by_chip[l + 1].update(zip(FIRST, got_first))
        by_chip[l].update(zip(LATER, got_later))
        pending = pending_next
    by_chip[0].update(zip(FIRST, _run_exchange(_ExchangeChips(pending), "grad_exchange_chips_first_l0")))
    small_g = {}
    for k in SMALL:
        small_g[k] = dnf.reshape(D_MODEL) if k == 'norm_final' else [grads[l][k].reshape(small[k].shape[1:]) for l in range(DEPTH)]
    return loss, dx, small_g, by_chip


MESH = pl.DeviceIdType.MESH
ANY_SPEC = pl.BlockSpec(memory_space=pl.ANY)


def _mesh_pos():
    return lax.axis_index("x"), lax.axis_index("y"), lax.axis_index("c")


def _other_chips(x, y):
    return [(1 - x, y), (x, 1 - y), (1 - x, 1 - y)]


class _AllGather:
    def __init__(self, arrs):
        self.arrs = list(arrs)
        na = self.n = len(self.arrs)
        self.out_shape = [_sds((N_DEV,) + a.shape, a.dtype) for a in self.arrs]
        self.scratch = [pltpu.SemaphoreType.DMA((na, 7)), pltpu.SemaphoreType.DMA((na, 7)), pltpu.SemaphoreType.DMA((na,))]

    def _copies(self, x_refs, o_refs, sems):
        send_sems, recv_sems, local_sems = sems
        x, y, c = _mesh_pos()
        me, sib = (x, y, c), (x, y, 1 - c)
        chips = _other_chips(x, y)

        def copy(a, k, block, to, src=None):
            dst = o_refs[a].at[4 * block[0] + 2 * block[1] + block[2]]
            return pltpu.make_async_remote_copy(src_ref=dst if src is None else src, dst_ref=dst,
                                                send_sem=send_sems.at[a, k], recv_sem=recv_sems.at[a, k],
                                                device_id=to, device_id_type=MESH)

        mine = [pltpu.make_async_copy(x_refs[a], o_refs[a].at[4 * x + 2 * y + c], local_sems.at[a]) for a in range(self.n)]
        first = []
        for a in range(self.n):
            first.append(copy(a, 0, me, sib, src=x_refs[a]))
            first += [copy(a, 1 + j, me, (*chip, c), src=x_refs[a]) for j, chip in enumerate(chips)]
        return copy, mine, first, me, sib, chips, c

    def start(self, x_refs, o_refs, sems):
        _, mine, first, *_ = self._copies(x_refs, o_refs, sems)
        for cp in mine + first:
            cp.start()

    def finish(self, x_refs, o_refs, sems):
        copy, mine, first, me, sib, chips, c = self._copies(x_refs, o_refs, sems)
        passed = []
        for a in range(self.n):
            for j, chip in enumerate(chips):
                copy(a, 1 + j, (*chip, c), me).wait_recv()
                cp = copy(a, 4 + j, (*chip, c), sib)
                cp.start()
                passed.append(cp)
        for a in range(self.n):
            copy(a, 0, sib, me).wait_recv()
            for j, chip in enumerate(chips):
                copy(a, 4 + j, (*chip, 1 - c), me).wait_recv()
        for cp in first + passed:
            cp.wait_send()
        for cp in mine:
            cp.wait()


class _ExchangeChips:
    def __init__(self, arrs):
        self.arrs = list(arrs)
        na = self.n = len(self.arrs)
        self.out_shape = [_sds(a.shape, a.dtype) for a in self.arrs]
        self.scratch = [pltpu.SemaphoreType.DMA((na, 3)), pltpu.SemaphoreType.DMA((na, 3)), pltpu.SemaphoreType.DMA((na,))]

    def _copies(self, a_refs, o_refs, sems):
        send_sems, recv_sems, local_sems = sems
        x, y, c = _mesh_pos()
        my_chip = 2 * x + y
        chips = _other_chips(x, y)
        mine = [pltpu.make_async_copy(a_refs[a].at[my_chip], o_refs[a].at[my_chip], local_sems.at[a]) for a in range(self.n)]
        sends = [pltpu.make_async_remote_copy(src_ref=a_refs[a].at[2 * px + py], dst_ref=o_refs[a].at[my_chip],
                                              send_sem=send_sems.at[a, j], recv_sem=recv_sems.at[a, j],
                                              device_id=(px, py, c), device_id_type=MESH)
                 for a in range(self.n) for j, (px, py) in enumerate(chips)]
        recvs = [pltpu.make_async_remote_copy(src_ref=a_refs[a].at[2 * px + py], dst_ref=o_refs[a].at[2 * px + py],
                                              send_sem=send_sems.at[a, j], recv_sem=recv_sems.at[a, j],
                                              device_id=(px, py, c), device_id_type=MESH)
                 for a in range(self.n) for j, (px, py) in enumerate(chips)]
        return mine, sends, recvs

    def start(self, a_refs, o_refs, sems):
        mine, sends, _ = self._copies(a_refs, o_refs, sems)
        for cp in mine + sends:
            cp.start()

    def finish(self, a_refs, o_refs, sems):
        mine, sends, recvs = self._copies(a_refs, o_refs, sems)
        for cp in recvs:
            cp.wait_recv()
        for cp in sends:
            cp.wait_send()
        for cp in mine:
            cp.wait()


def _run_exchange(ex, name):
    na = ex.n

    def body(*refs):
        i_refs, o_refs, sems = refs[:na], refs[na:2 * na], refs[2 * na:]
        ex.start(i_refs, o_refs, sems)
        ex.finish(i_refs, o_refs, sems)

    return pl.pallas_call(body, in_specs=[ANY_SPEC] * na, out_specs=[ANY_SPEC] * na, out_shape=ex.out_shape,
                          scratch_shapes=ex.scratch, name=name)(*ex.arrs)


class _ExchangeCores:
    def __init__(self, arrs):
        self.arrs = list(arrs)
        na = self.n = len(self.arrs)
        self.out_shape = [_sds(a.shape[1:], a.dtype) for a in self.arrs]
        self.scratch = [pltpu.SemaphoreType.DMA((na,)), pltpu.SemaphoreType.DMA((na,))]

    def _copies(self, a_refs, o_refs, sems):
        send_sems, recv_sems = sems
        x, y, c = _mesh_pos()
        return [pltpu.make_async_remote_copy(src_ref=a_refs[a].at[1 - c], dst_ref=o_refs[a], send_sem=send_sems.at[a],
                                             recv_sem=recv_sems.at[a], device_id=(x, y, 1 - c), device_id_type=MESH)
                for a in range(self.n)]

    def start(self, a_refs, o_refs, sems):
        for cp in self._copies(a_refs, o_refs, sems):
            cp.start()

    def finish(self, a_refs, o_refs, sems):
        for cp in self._copies(a_refs, o_refs, sems):
            cp.wait()


def _as_rows(a, lead):
    return a.reshape(a.shape[:lead] + (-1, a.shape[-1]))


def _add_halves(a, other, c_idx, name):
    a3, o2 = _as_rows(a, 1), _as_rows(other, 0)
    rows, cols = o2.shape
    tr = _pick(rows, 256)

    def body(c_ref, a_ref, o_ref, out_ref):
        out_ref[...] = (a_ref[0].astype(F32) + o_ref[...].astype(F32)).astype(out_ref.dtype)

    out = pl.pallas_call(
        body,
        grid_spec=pltpu.PrefetchScalarGridSpec(
            num_scalar_prefetch=1, grid=(rows // tr,),
            in_specs=[pl.BlockSpec((1, tr, cols), lambda i, c_ref: (c_ref[0], i, 0)),
                      pl.BlockSpec((tr, cols), lambda i, c_ref: (i, 0))],
            out_specs=pl.BlockSpec((tr, cols), lambda i, c_ref: (i, 0))),
        out_shape=_sds((rows, cols), a.dtype), compiler_params=_cparams(("parallel",)), name=name,
    )(c_idx, a3, o2)
    return out.reshape(other.shape)


def _all_reduce_small(v, name):
    r = v.shape[0]

    def body(v_ref, o_ref, slots, send_sems, recv_sems):
        x, y, c = _mesh_pos()
        me = 4 * x + 2 * y + c
        slots[me] = v_ref[...]
        cps = []
        for k in range(1, N_DEV):
            px = 1 - x if k & 4 else x
            py = 1 - y if k & 2 else y
            pc = 1 - c if k & 1 else c
            cps.append(pltpu.make_async_remote_copy(src_ref=v_ref, dst_ref=slots.at[me], send_sem=send_sems.at[k - 1],
                                                    recv_sem=recv_sems.at[k - 1], device_id=(px, py, pc), device_id_type=MESH))
        for cp in cps:
            cp.start()
        for cp in cps:
            cp.wait()
        acc = slots[0]
        for d in range(1, N_DEV):
            acc = acc + slots[d]
        o_ref[...] = acc

    vm = pl.BlockSpec(memory_space=pltpu.VMEM)
    return pl.pallas_call(
        body, in_specs=[vm], out_specs=vm, out_shape=_sds((r, LANES), F32),
        scratch_shapes=[pltpu.VMEM((N_DEV, r, LANES), F32), pltpu.SemaphoreType.DMA((N_DEV - 1,)),
                        pltpu.SemaphoreType.DMA((N_DEV - 1,))],
        compiler_params=pltpu.CompilerParams(vmem_limit_bytes=VMEM_LIMIT_BYTES), name=name,
    )(v)


def _adamw(w, g_slots, m, v, name):
    depth, rows, cols = w.shape
    ns = g_slots.shape[0]
    tr = _pick(rows, 256 if cols <= 1024 else 128)

    def body(w_ref, g_ref, m_ref, v_ref, go_ref, d_ref, mo_ref, vo_ref):
        g = g_ref[0, 0].astype(F32)
        for i in range(1, ns):
            g = g + g_ref[i, 0].astype(F32)
        m_new = ADAM_B1 * m_ref[0] + (1.0 - ADAM_B1) * g
        v_new = ADAM_B2 * v_ref[0] + (1.0 - ADAM_B2) * (g * g)
        m_hat = m_new / (1.0 - ADAM_B1 ** ADAM_STEP)
        v_hat = v_new / (1.0 - ADAM_B2 ** ADAM_STEP)
        go_ref[0] = g
        d_ref[0] = -ADAM_LR * (m_hat / (jnp.sqrt(v_hat) + ADAM_EPS) + ADAM_WD * w_ref[0])
        mo_ref[0] = m_new
        vo_ref[0] = v_new

    blk = pl.BlockSpec((1, tr, cols), lambda l, i: (l, i, 0))
    return pl.pallas_call(
        body, grid=(depth, rows // tr),
        in_specs=[blk, pl.BlockSpec((ns, 1, tr, cols), lambda l, i: (0, l, i, 0)), blk, blk],
        out_specs=[blk] * 4, out_shape=[_sds(w.shape, F32)] * 4,
        compiler_params=_cparams(("parallel", "parallel")), name=name,
    )(w, g_slots, m, v)


_ARG_NAMES = (['x', 'mem', 'positions'] + WEIGHTS + ['loss_target'] + ['m_' + n for n in WEIGHTS]
              + ['v_' + n for n in WEIGHTS])


PACK_TILE = 8 * LANES


def _pack_rows(parts):
    blocks = []
    for part in parts:
        flat = part.reshape(-1)
        pad = (-flat.shape[0]) % PACK_TILE
        blocks.append((jnp.pad(flat, (0, pad)) if pad else flat).reshape(-1, LANES))
    return jnp.concatenate(blocks, axis=0)


def _unpack_rows(packed, shapes):
    out, off = [], 0
    for shp in shapes:
        n = int(np.prod(shp))
        rows = -(-n // PACK_TILE) * 8
        out.append(packed[off:off + rows].reshape(-1)[:n].reshape(shp))
        off += rows
    return out


def kernel(x, mem, positions, norm_mix, w_in, b_gate, conv_w, conv_b, dt_bias, a_log, d_skip, ssm_norm, w_br_ret, w_br_ssm, w_out, norm_xa, norm_mem, xa_wq, xa_wkv, xa_wo, norm_mlp, mlp_w1, mlp_w2, norm_final, loss_target, m_norm_mix, m_w_in, m_b_gate, m_conv_w, m_conv_b, m_dt_bias, m_a_log, m_d_skip, m_ssm_norm, m_w_br_ret, m_w_br_ssm, m_w_out, m_norm_xa, m_norm_mem, m_xa_wq, m_xa_wkv, m_xa_wo, m_norm_mlp, m_mlp_w1, m_mlp_w2, m_norm_final, v_norm_mix, v_w_in, v_b_gate, v_conv_w, v_conv_b, v_dt_bias, v_a_log, v_d_skip, v_ssm_norm, v_w_br_ret, v_w_br_ssm, v_w_out, v_norm_xa, v_norm_mem, v_xa_wq, v_xa_wkv, v_xa_wo, v_norm_mlp, v_mlp_w1, v_mlp_w2, v_norm_final):
    d = dict(zip(_ARG_NAMES, (x, mem, positions, norm_mix, w_in, b_gate, conv_w, conv_b, dt_bias, a_log, d_skip, ssm_norm, w_br_ret, w_br_ssm, w_out, norm_xa, norm_mem, xa_wq, xa_wkv, xa_wo, norm_mlp, mlp_w1, mlp_w2, norm_final, loss_target, m_norm_mix, m_w_in, m_b_gate, m_conv_w, m_conv_b, m_dt_bias, m_a_log, m_d_skip, m_ssm_norm, m_w_br_ret, m_w_br_ssm, m_w_out, m_norm_xa, m_norm_mem, m_xa_wq, m_xa_wkv, m_xa_wo, m_norm_mlp, m_mlp_w1, m_mlp_w2, m_norm_final, v_norm_mix, v_w_in, v_b_gate, v_conv_w, v_conv_b, v_dt_bias, v_a_log, v_d_skip, v_ssm_norm, v_w_br_ret, v_w_br_ssm, v_w_out, v_norm_xa, v_norm_mem, v_xa_wq, v_xa_wkv, v_xa_wo, v_norm_mlp, v_mlp_w1, v_mlp_w2, v_norm_final)))
    blocks = [{k: d[k][l] if k == 'conv_w' else d[k][l].astype(MXU_DTYPE) for k in SHARDED} for l in range(DEPTH)]
    small = {k: d[k] for k in SMALL}
    loss, grad_x, grads, by_chip_l = _step(d['x'][0], d['mem'][0], d['positions'][0], small, blocks, d['loss_target'][0])
    by_chip = [jnp.stack([by_chip_l[l][k] for l in range(DEPTH)], axis=1) for k in SHARDED]
    small_g = [grads[k] if k == 'norm_final' else jnp.stack(grads[k]) for k in SMALL]
    total = _all_reduce_small(_pack_rows([loss] + small_g), "all_reduce_small")
    loss_out = total[0, 0]
    res = {}
    for k, g4 in zip(SHARDED, by_chip):
        res[k] = _adamw(d[k], g4, d['m_' + k], d['v_' + k], f"adamw_{k}")
    small_shapes = [d[k].shape for k in SMALL]
    pk = lambda pre: _pack_rows([d[pre + k] for k in SMALL])
    outs = _adamw(pk('')[None], total[8:][None, None], pk('m_')[None], pk('v_')[None], "adamw_small")
    unpacked = [_unpack_rows(o[0], small_shapes) for o in outs]
    for i, k in enumerate(SMALL):
        res[k] = [unpacked[j][i] for j in range(4)]
    return (loss_out, grad_x[None], *[res[k][0] for k in WEIGHTS], *[res[k][1] for k in WEIGHTS],
            *[res[k][2] for k in WEIGHTS], *[res[k][3] for k in WEIGHTS])
```

```python
import functools

import numpy as np
import jax
import jax.numpy as jnp
from jax import lax
from jax.experimental import pallas as pl
from jax.experimental.pallas import tpu as pltpu

F32 = jnp.float32
MXU_DTYPE = jnp.bfloat16
VMEM_LIMIT_BYTES = 56 * 1024 * 1024
LANES = 128
N_DEV = 8

D_MODEL = 1024
DEPTH = 4
CHUNK = 64
EPS = 1e-6
RET_HEADS, RET_QK_DIM, RET_V_DIM = 4, 128, 256
RET_QK, RET_V = 512, 1024
ROPE_THETA = 10000.0
SSM_INNER, SSM_HEAD_DIM, SSM_HEADS, SSM_GROUPS, SSM_STATE, SSM_CONV = 2048, 64, 32, 8, 128, 4
SSM_BC = 1024
SSM_CONV_DIM = 4096
IN_DIM = 11296
XA_HEADS, XA_HEAD_DIM = 4, 256
D_FF = 4096
ADAM_LR, ADAM_B1, ADAM_B2, ADAM_EPS, ADAM_WD, ADAM_STEP = 0.001, 0.9, 0.999, 1e-08, 0.01, 10

PROJ_W = 11264
COL_Z, COL_XBC, COL_GATES, COL_V, COL_G, COL_Q, COL_K = 0, 2048, 6144, 8192, 9216, 10240, 10752
DT_PAD = 128
N_LTILE = SSM_INNER // LANES

WEIGHTS = ['norm_mix', 'w_in', 'b_gate', 'conv_w', 'conv_b', 'dt_bias', 'a_log', 'd_skip', 'ssm_norm',
           'w_br_ret', 'w_br_ssm', 'w_out', 'norm_xa', 'norm_mem', 'xa_wq', 'xa_wkv', 'xa_wo', 'norm_mlp',
           'mlp_w1', 'mlp_w2', 'norm_final']
COL_SHARDED = ['w_in', 'conv_w', 'xa_wkv', 'mlp_w1']
ROW_SHARDED = ['w_br_ret', 'w_br_ssm', 'w_out', 'xa_wq', 'xa_wo', 'mlp_w2']
SHARDED = COL_SHARDED + ROW_SHARDED
FIRST = ['w_in', 'conv_w']
LATER = [n for n in SHARDED if n not in FIRST]
SMALL = [n for n in WEIGHTS if n not in SHARDED]


def _cparams(sem=None):
    return pltpu.CompilerParams(dimension_semantics=sem, vmem_limit_bytes=VMEM_LIMIT_BYTES)


def _sds(shape, dtype):
    return jax.ShapeDtypeStruct(shape, dtype)


def _full(shape):
    nd = len(shape)
    return pl.BlockSpec(shape, lambda *_: (0,) * nd)


_DIMS = {'nn': (((1,), (0,)), ((), ())), 'nt': (((1,), (1,)), ((), ())), 'tn': (((0,), (0,)), ((), ()))}


def _dot(a, b, mode='nn'):
    return lax.dot_general(a.astype(MXU_DTYPE), b.astype(MXU_DTYPE), _DIMS[mode], preferred_element_type=F32)


@functools.partial(jax.custom_vjp, nondiff_argnums=(2,))
def _mm(a, b, mode):
    return _dot(a, b, mode)


def _mm_fwd(a, b, mode):
    return _dot(a, b, mode), (a, b)


def _mm_bwd(mode, res, g):
    a, b = res
    if mode == 'nn':
        return _dot(g, b, 'nt'), _dot(a, g, 'tn')
    if mode == 'nt':
        return _dot(g, b, 'nn'), _dot(g, a, 'tn')
    return _dot(b, g, 'nt'), _dot(a, g, 'nn')


_mm.defvjp(_mm_fwd, _mm_bwd)


def _split3(x):
    hi = x.astype(jnp.bfloat16)
    r1 = x - hi.astype(F32)
    mid = r1.astype(jnp.bfloat16)
    lo = (r1 - mid.astype(F32)).astype(jnp.bfloat16)
    return hi, mid, lo


def _dot_sel(x, c, left=False):
    dims = _DIMS['nn']
    parts = _split3(x)
    if left:
        outs = [lax.dot_general(c, p, dims, preferred_element_type=F32) for p in parts]
    else:
        outs = [lax.dot_general(p, c, dims, preferred_element_type=F32) for p in parts]
    return (outs[0] + outs[1]) + outs[2]


def _silu(x):
    return x * jax.nn.sigmoid(x)


def _softplus(x):
    pos = x > 0.0
    return jnp.where(pos, x, 0.0) + jnp.log1p(jnp.exp(jnp.where(pos, -x, x)))


def _rms(x):
    return x * lax.rsqrt(jnp.mean(x * x, axis=-1, keepdims=True) + EPS)


def _pick(n, pref):
    t = min(n, pref)
    while n % t:
        t //= 2
    return t


def _with_rider(core, n_in, n_out, n_scratch, rider, grid):
    if rider is None:
        return core
    na, nrs = rider.n, len(rider.scratch)

    def at(step_of):
        cond = pl.program_id(0) == step_of(grid[0])
        for ax in range(1, len(grid)):
            cond = cond & (pl.program_id(ax) == step_of(grid[ax]))
        return cond

    def body(*refs):
        ci, ri = refs[:n_in], refs[n_in:n_in + na]
        co, ro = refs[n_in + na:n_in + na + n_out], refs[n_in + na + n_out:n_in + 2 * na + n_out]
        sc = refs[n_in + 2 * na + n_out:]
        cs, rs = sc[:n_scratch], sc[n_scratch:]
        assert len(rs) == nrs

        @pl.when(at(lambda n: 0))
        def _():
            rider.start(ri, ro, rs)

        core(*ci, *co, *cs)

        @pl.when(at(lambda n: n - 1))
        def _():
            rider.finish(ri, ro, rs)

    return body


def _rider_args(rider):
    if rider is None:
        return [], [], [], [], []
    return list(rider.arrs), [ANY_SPEC] * rider.n, [ANY_SPEC] * rider.n, list(rider.out_shape), list(rider.scratch)


MATMUL_TK_MAX = 4096


def _tiles(mode, m, n, k):
    tm, tn = (512, 1024) if mode == 'nt' else (1024, 512)
    tk = k
    while tk > MATMUL_TK_MAX or k % tk or tk % LANES:
        tk -= LANES
    return _pick(m, tm), _pick(n, tn), tk


def _matmul(a, b, mode, name, *, extras=(), epi=None, out_dtypes=(F32,), out_t=None, tiles=None, rider=None,
            scatter=None):
    if mode == 'nn':
        (m, k), (k2, n) = a.shape, b.shape
    elif mode == 'nt':
        (m, k), (n, k2) = a.shape, b.shape
    else:
        (k, m), (k2, n) = a.shape, b.shape
    assert k == k2, (a.shape, b.shape, mode)
    if scatter == 'rows':
        tiles = (m // N_DEV, n if n <= 1024 else 512, tiles[2] if tiles else _tiles(mode, m, n, k)[2])
    elif scatter == 'cols':
        tiles = (_pick(m, 1024), n // N_DEV, tiles[2] if tiles else _tiles(mode, m, n, k)[2])
    tm, tn, tk = tiles or _tiles(mode, m, n, k)
    nk = k // tk
    n_ex, n_out = len(extras), len(out_dtypes)
    out_t = out_t or (False,) * n_out

    def finish(acc, ex_refs, o_refs):
        outs = epi(acc, *[r[...] for r in ex_refs]) if epi is not None else (acc,)
        for o_ref, o, tr in zip(o_refs, outs, out_t):
            if scatter:
                o_ref[0, 0] = o.astype(o_ref.dtype)
            else:
                o_ref[...] = (o.T if tr else o).astype(o_ref.dtype)

    def body(*refs):
        a_ref, b_ref = refs[0], refs[1]
        ex_refs = refs[2:2 + n_ex]
        o_refs = refs[2 + n_ex:2 + n_ex + n_out]
        if nk == 1:
            finish(_dot(a_ref[...], b_ref[...], mode), ex_refs, o_refs)
            return
        acc_ref = refs[-1]
        kk = pl.program_id(2)

        @pl.when(kk == 0)
        def _():
            acc_ref[...] = jnp.zeros_like(acc_ref)

        acc_ref[...] += _dot(a_ref[...], b_ref[...], mode)

        @pl.when(kk == nk - 1)
        def _():
            finish(acc_ref[...], ex_refs, o_refs)

    if mode == 'nn':
        a_spec = pl.BlockSpec((tm, tk), lambda i, j, kk: (i, kk))
        b_spec = pl.BlockSpec((tk, tn), lambda i, j, kk: (kk, j))
    elif mode == 'nt':
        a_spec = pl.BlockSpec((tm, tk), lambda i, j, kk: (i, kk))
        b_spec = pl.BlockSpec((tn, tk), lambda i, j, kk: (j, kk))
    else:
        a_spec = pl.BlockSpec((tk, tm), lambda i, j, kk: (kk, i))
        b_spec = pl.BlockSpec((tk, tn), lambda i, j, kk: (kk, j))
    mn_spec = pl.BlockSpec((tm, tn), lambda i, j, kk: (i, j))
    nm_spec = pl.BlockSpec((tn, tm), lambda i, j, kk: (j, i))
    grid = (m // tm, n // tn, nk)
    r_arrs, r_in, r_out, r_shape, r_scratch = _rider_args(rider)
    o_specs = [nm_spec if tr else mn_spec for tr in out_t]
    o_shapes = [_sds((n, m) if tr else (m, n), dt) for dt, tr in zip(out_dtypes, out_t)]
    if scatter == 'rows':
        o_specs = [pl.BlockSpec((1, 1, tm, tn), lambda i, j, kk: (i % 2, i // 2, 0, j))]
        o_shapes = [_sds((2, 4, tm, n), out_dtypes[0])]
    elif scatter == 'cols':
        o_specs = [pl.BlockSpec((1, 1, tm, tn), lambda i, j, kk: (j % 2, j // 2, i, 0))]
        o_shapes = [_sds((2, 4, m, tn), out_dtypes[0])]
    outs = pl.pallas_call(
        _with_rider(body, 2 + n_ex, n_out, int(nk > 1), rider, grid), grid=grid,
        in_specs=[a_spec, b_spec] + [mn_spec] * n_ex + r_in,
        out_specs=o_specs + r_out,
        out_shape=o_shapes + r_shape,
        scratch_shapes=([pltpu.VMEM((tm, tn), F32)] if nk > 1 else []) + r_scratch,
        compiler_params=_cparams(("arbitrary",) * 3 if rider is not None else ("parallel", "parallel", "arbitrary")),
        name=name,
    )(a, b, *extras, *r_arrs)
    res = outs[0] if n_out == 1 else outs[:n_out]
    return (res, outs[n_out:]) if rider is not None else res


def _epi_add(acc, r):
    return (acc + r,)


PIECE_TK = 1024


def _matmul_nt_pieces(pieces, b, name, *, extras=(), epi=None, out_dtypes=(F32,), rider=None):
    m, n = pieces[0].shape[0], b.shape[0]
    tm, tn, tk = _pick(m, 1024), _pick(n, 1024), PIECE_TK
    steps = [pc.shape[1] // tk for pc in pieces]
    starts = [sum(steps[:i]) for i in range(len(pieces))]
    nk = sum(steps)
    assert b.shape[1] == nk * tk and all(pc.shape[1] % tk == 0 for pc in pieces)
    n_pc, n_ex, n_out = len(pieces), len(extras), len(out_dtypes)

    def body(*refs):
        pc_refs, b_ref = refs[:n_pc], refs[n_pc]
        ex_refs = refs[n_pc + 1:n_pc + 1 + n_ex]
        o_refs = refs[n_pc + 1 + n_ex:n_pc + 1 + n_ex + n_out]
        acc_ref = refs[-1]
        kk = pl.program_id(2)

        @pl.when(kk == 0)
        def _():
            acc_ref[...] = jnp.zeros_like(acc_ref)

        for pc_ref, st, ns in zip(pc_refs, starts, steps):
            @pl.when((kk >= st) & (kk < st + ns))
            def _(pc_ref=pc_ref):
                acc_ref[...] += _dot(pc_ref[...], b_ref[...], 'nt')

        @pl.when(kk == nk - 1)
        def _():
            acc = acc_ref[...]
            outs = epi(acc, *[r[...] for r in ex_refs]) if epi is not None else (acc,)
            for o_ref, o in zip(o_refs, outs):
                o_ref[...] = o.astype(o_ref.dtype)

    pc_specs = [pl.BlockSpec((tm, tk), lambda i, j, kk, st=st, ns=ns: (i, jnp.clip(kk - st, 0, ns - 1)))
                for st, ns in zip(starts, steps)]
    mn_spec = pl.BlockSpec((tm, tn), lambda i, j, kk: (i, j))
    grid = (m // tm, n // tn, nk)
    r_arrs, r_in, r_out, r_shape, r_scratch = _rider_args(rider)
    outs = pl.pallas_call(
        _with_rider(body, n_pc + 1 + n_ex, n_out, 1, rider, grid), grid=grid,
        in_specs=pc_specs + [pl.BlockSpec((tn, tk), lambda i, j, kk: (j, kk))] + [mn_spec] * n_ex + r_in,
        out_specs=[mn_spec] * n_out + r_out, out_shape=[_sds((m, n), dt) for dt in out_dtypes] + r_shape,
        scratch_shapes=[pltpu.VMEM((tm, tn), F32)] + r_scratch,
        compiler_params=_cparams(("arbitrary",) * 3 if rider is not None else ("parallel", "parallel", "arbitrary")),
        name=name,
    )(*pieces, b, *extras, *r_arrs)
    res = outs[0] if n_out == 1 else outs[:n_out]
    return (res, outs[n_out:]) if rider is not None else res


def _rmsnorm_fn(x, w):
    return _rms(x) * w


def _rmsnorm(x, w, name):
    s, d = x.shape
    t = _pick(s, 512)

    def body(x_ref, w_ref, o_ref, ot_ref):
        y = _rmsnorm_fn(x_ref[...], w_ref[...])
        o_ref[...] = y.astype(o_ref.dtype)
        ot_ref[...] = y.T.astype(ot_ref.dtype)

    return pl.pallas_call(
        body, grid=(s // t,),
        in_specs=[pl.BlockSpec((t, d), lambda i: (i, 0)), _full((1, d))],
        out_specs=[pl.BlockSpec((t, d), lambda i: (i, 0)), pl.BlockSpec((d, t), lambda i: (0, i))],
        out_shape=[_sds((s, d), MXU_DTYPE), _sds((d, s), MXU_DTYPE)],
        compiler_params=_cparams(("parallel",)), name=name,
    )(x, w.reshape(1, d))


def _rmsnorm_bwd(x, w, du, dres, name):
    s, d = x.shape
    t = _pick(s, 512)
    has_res = dres is not None

    def body(*refs):
        if has_res:
            x_ref, w_ref, du_ref, dres_ref, dx_ref, dxb_ref, dw_ref = refs
        else:
            x_ref, w_ref, du_ref, dx_ref, dxb_ref, dw_ref = refs
        _, vjp = jax.vjp(_rmsnorm_fn, x_ref[...], w_ref[...])
        dx, dw = vjp(du_ref[...])
        dx = dx + dres_ref[...] if has_res else dx
        dx_ref[...] = dx
        dxb_ref[...] = dx.astype(dxb_ref.dtype)

        @pl.when(pl.program_id(0) == 0)
        def _():
            dw_ref[...] = jnp.zeros_like(dw_ref)

        dw_ref[...] += dw

    row = pl.BlockSpec((t, d), lambda i: (i, 0))
    return pl.pallas_call(
        body, grid=(s // t,),
        in_specs=[row, _full((1, d)), row] + ([row] if has_res else []),
        out_specs=[row, row, _full((1, d))],
        out_shape=[_sds((s, d), F32), _sds((s, d), MXU_DTYPE), _sds((1, d), F32)],
        compiler_params=_cparams(("arbitrary",)), name=name,
    )(x, w.reshape(1, d), du, *([dres] if has_res else []))


CONV_CW = 2048
CONV_HALO = 16


def _shifted(cat):
    return [cat] + [pltpu.roll(cat, sft, axis=0) for sft in (1, 2, 3)]


def _conv_taps(shifted, w, n_rows, off):
    acc = shifted[0][off:off + n_rows, :] * w[3:4, :]
    for sft in (1, 2, 3):
        acc = acc + shifted[sft][off:off + n_rows, :] * w[3 - sft:4 - sft, :]
    return acc


def _conv_fwd(proj, conv_w, conv_b, name):
    s = proj.shape[0]
    tr = _pick(s, 512)
    hb = tr // CONV_HALO
    col0 = COL_XBC // CONV_CW

    def body(prev_ref, x_ref, w_ref, b_ref, o_ref):
        i = pl.program_id(1)
        prev = jnp.where(i == 0, 0.0, prev_ref[...].astype(F32))
        cat = jnp.concatenate([prev, x_ref[...].astype(F32)], axis=0)
        o_ref[...] = _silu(_conv_taps(_shifted(cat), w_ref[...], tr, CONV_HALO) + b_ref[...])

    return pl.pallas_call(
        body, grid=(SSM_CONV_DIM // CONV_CW, s // tr),
        in_specs=[pl.BlockSpec((CONV_HALO, CONV_CW), lambda j, i: (jnp.maximum(i * hb - 1, 0), j + col0)),
                  pl.BlockSpec((tr, CONV_CW), lambda j, i: (i, j + col0)),
                  pl.BlockSpec((SSM_CONV, CONV_CW), lambda j, i: (0, j)),
                  pl.BlockSpec((1, CONV_CW), lambda j, i: (0, j))],
        out_specs=pl.BlockSpec((tr, CONV_CW), lambda j, i: (i, j)),
        out_shape=_sds((s, SSM_CONV_DIM), F32),
        compiler_params=_cparams(("parallel", "parallel")), name=name,
    )(proj, proj, conv_w, conv_b.reshape(1, SSM_CONV_DIM))


def _conv_bwd(proj, conv_w, conv_b, dact, name):
    s = proj.shape[0]
    tr = _pick(s, 512)
    hb = tr // CONV_HALO
    nb = s // CONV_HALO
    nt = s // tr
    col0 = COL_XBC // CONV_CW
    h = CONV_HALO

    def body(prev_ref, x_ref, next_ref, w_ref, b_ref, da_ref, dan_ref, dx_ref, dw_ref, db_ref):
        i = pl.program_id(1)
        w = w_ref[...]
        prev = jnp.where(i == 0, 0.0, prev_ref[...].astype(F32))
        cat = jnp.concatenate([prev, x_ref[...].astype(F32), next_ref[...].astype(F32)], axis=0)
        shifted = _shifted(cat)
        pre = _conv_taps(shifted, w, tr + h, h) + b_ref[...]
        dact_n = jnp.where(i == nt - 1, 0.0, dan_ref[...])
        dact_ext = jnp.concatenate([da_ref[...], dact_n], axis=0)
        sg = jax.nn.sigmoid(pre)
        dpre = dact_ext * (sg * (1.0 + pre * (1.0 - sg)))
        dx = dpre[:tr, :] * w[3:4, :]
        for sft in (1, 2, 3):
            dx = dx + pltpu.roll(dpre, tr + h - sft, axis=0)[:tr, :] * w[3 - sft:4 - sft, :]
        dx_ref[...] = dx.astype(dx_ref.dtype)

        @pl.when(i == 0)
        def _():
            dw_ref[...] = jnp.zeros_like(dw_ref)
            db_ref[...] = jnp.zeros_like(db_ref)

        dp = dpre[:tr, :]
        db_ref[...] += jnp.sum(dp, axis=0, keepdims=True)
        for r, sft in enumerate((3, 2, 1, 0)):
            dw_ref[r:r + 1, :] += jnp.sum(dp * shifted[sft][h:h + tr, :], axis=0, keepdims=True)

    return pl.pallas_call(
        body, grid=(SSM_CONV_DIM // CONV_CW, nt),
        in_specs=[pl.BlockSpec((h, CONV_CW), lambda j, i: (jnp.maximum(i * hb - 1, 0), j + col0)),
                  pl.BlockSpec((tr, CONV_CW), lambda j, i: (i, j + col0)),
                  pl.BlockSpec((h, CONV_CW), lambda j, i: (jnp.minimum((i + 1) * hb, nb - 1), j + col0)),
                  pl.BlockSpec((SSM_CONV, CONV_CW), lambda j, i: (0, j)),
                  pl.BlockSpec((1, CONV_CW), lambda j, i: (0, j)),
                  pl.BlockSpec((tr, CONV_CW), lambda j, i: (i, j)),
                  pl.BlockSpec((h, CONV_CW), lambda j, i: (jnp.minimum((i + 1) * hb, nb - 1), j))],
        out_specs=[pl.BlockSpec((tr, CONV_CW), lambda j, i: (i, j)),
                   pl.BlockSpec((SSM_CONV, CONV_CW), lambda j, i: (0, j)),
                   pl.BlockSpec((1, CONV_CW), lambda j, i: (0, j))],
        out_shape=[_sds((s, SSM_CONV_DIM), MXU_DTYPE), _sds((SSM_CONV, SSM_CONV_DIM), F32), _sds((1, SSM_CONV_DIM), F32)],
        compiler_params=_cparams(("parallel", "arbitrary")), name=name,
    )(proj, proj, proj, conv_w, conv_b.reshape(1, SSM_CONV_DIM), dact, dact)


def _scan_tables():
    idx = np.arange(CHUNK, dtype=np.float32)
    lg = np.log1p(-(2.0 ** (-5.0 - np.arange(RET_HEADS, dtype=np.float32)))).astype(np.float32)
    rel = np.abs(idx[:, None] - idx[None, :])
    r_intra = np.exp(lg[:, None, None] * rel).astype(np.float32)
    qd = np.exp(lg[None, :] * (idx[:, None] + 1.0)).astype(np.float32)
    kd = np.exp(lg[None, :] * (CHUNK - 1.0 - idx[:, None])).astype(np.float32)
    gam = [float(v) for v in np.exp(lg * CHUNK).astype(np.float32)]
    qd_e = np.repeat(qd, RET_QK_DIM, axis=1)
    kd_e = np.repeat(kd, RET_QK_DIM, axis=1)
    e = np.zeros((DT_PAD, SSM_INNER), np.float32)
    for hh in range(SSM_HEADS):
        e[hh, hh * SSM_HEAD_DIM:(hh + 1) * SSM_HEAD_DIM] = 1.0
    tri = np.tril(np.ones((CHUNK, CHUNK), np.float32))
    eye2 = np.concatenate([np.eye(CHUNK, dtype=np.float32)] * 2, axis=1)
    bdm = np.kron(np.eye(2, dtype=np.float32), np.ones((CHUNK, CHUNK), np.float32))
    last = np.zeros((CHUNK, LANES), np.float32)
    last[CHUNK - 1, :] = 1.0
    f32c = [jnp.asarray(c) for c in (r_intra, qd_e, kd_e, eye2, bdm, last)]
    sel = [jnp.asarray(c, jnp.bfloat16) for c in (e, e.T.copy(), tri, tri.T.copy())]
    return f32c + sel, gam


def _rope(t, cos2, sin2):
    return t * cos2 + pltpu.roll(t, RET_QK_DIM // 2, axis=1) * sin2


def _rope_t(d, cos2, sin2):
    return d * cos2 + pltpu.roll(d * sin2, RET_QK_DIM // 2, axis=1)


def _ret_step(q, k, v, st, r_intra, qd, kd, gamma):
    k = k * (RET_QK_DIM ** -0.5)
    sc = _mm(q, k, 'nt') * r_intra
    y = _mm(sc, v, 'nn') + _mm(q * qd, st, 'nn')
    st_new = st * gamma + _mm(k * kd, v, 'tn')
    return y, st_new


def _ssd_heads(dtraw, dtb, a_c, tri):
    dt = _softplus(dtraw + dtb)
    return dt, _dot_sel(dt * a_c, tri, left=True)


def _ssd_group(dte0, dte1, cum0, cum1, xs0, xs1, bm, cm, ht0, ht1, eye2, bdm, last):
    cbp = _mm(cm, jnp.concatenate([bm, bm], axis=0), 'nt')
    outs = []
    for dte, cum, xs, ht in ((dte0, cum0, xs0, ht0), (dte1, cum1, xs1, ht1)):
        r = jnp.sum(cum * eye2, axis=0, keepdims=True)
        dlt = cum - r
        seg = jnp.exp(jnp.where(dlt > 0.0, -dlt, dlt))
        xdt = xs * dte
        bd = jnp.concatenate([xdt, xdt], axis=0) * bdm
        clast = jnp.sum(cum * last, axis=0, keepdims=True)
        y = _mm(cbp * seg, bd, 'nn') + jnp.exp(cum) * _mm(cm, ht, 'nn')
        ht_new = jnp.exp(clast) * ht + _mm(bm, xdt * jnp.exp(clast - cum), 'tn')
        outs += [y, ht_new]
    return tuple(outs)


SCAN_SUB = 4


def _scan_in_specs(nb, rev, rows):
    ch = (lambda c: nb - 1 - c) if rev else (lambda c: c)
    col = lambda w, blk: pl.BlockSpec((rows, w), lambda c: (ch(c), blk))
    return [col(RET_QK, COL_Q // RET_QK), col(RET_QK, COL_K // RET_QK), col(RET_V, COL_V // RET_V),
            col(SSM_INNER, 0), col(SSM_BC, 2), col(SSM_BC, 3),
            col(DT_PAD, 0), col(LANES, 0), col(LANES, 0)]


def _const_specs(consts):
    return [_full(c.shape) for c in consts]


def _tile(t):
    return slice(t * LANES, (t + 1) * LANES)


def _scan_sub(s):
    return SCAN_SUB if (s // CHUNK) % SCAN_SUB == 0 else 1


def _scan_fwd(proj, xbc, dtraw, cos2, sin2, a_c, dtb, name, rider=None):
    s = proj.shape[0]
    nc = s // CHUNK
    sub = _scan_sub(s)
    nb, rows = nc // sub, CHUNK * sub
    consts, gam = _scan_tables()
    r_arrs, r_in, r_out, r_shape, r_scratch = _rider_args(rider)

    def body(q_ref, k_ref, v_ref, xs_ref, bm_ref, cm_ref, dt_ref, cos_ref, sin_ref, ac_ref, dtb_ref,
             ri_ref, qd_ref, kd_ref, eye_ref, bdm_ref, last_ref, e_ref, et_ref, tri_ref, trit_ref,
             yr_ref, ys_ref, sh_ref, hh_ref, st_sc, ht_sc):
        @pl.when(pl.program_id(0) == 0)
        def _():
            st_sc[...] = jnp.zeros_like(st_sc)
            ht_sc[...] = jnp.zeros_like(ht_sc)

        eye2, bdm, last = eye_ref[...], bdm_ref[...], last_ref[...]
        st = [st_sc[h * RET_QK_DIM:(h + 1) * RET_QK_DIM, :] for h in range(RET_HEADS)]
        ht = [ht_sc[:, _tile(t)] for t in range(N_LTILE)]
        for i in range(sub):
            r = slice(i * CHUNK, (i + 1) * CHUNK)
            cos2, sin2 = cos_ref[r, :], sin_ref[r, :]
            for h in range(RET_HEADS):
                ql = slice(h * RET_QK_DIM, (h + 1) * RET_QK_DIM)
                vl = slice(h * RET_V_DIM, (h + 1) * RET_V_DIM)
                sh_ref[i, ql, :] = st[h]
                y, st[h] = _ret_step(_rope(q_ref[r, ql].astype(F32), cos2, sin2), _rope(k_ref[r, ql].astype(F32), cos2, sin2),
                                     v_ref[r, vl].astype(F32), st[h], ri_ref[h], qd_ref[:, ql], kd_ref[:, ql], gam[h])
                yr_ref[r, vl] = y.astype(yr_ref.dtype)
            dt, cum_c = _ssd_heads(dt_ref[r, :], dtb_ref[...], ac_ref[...], tri_ref[...])
            both = jnp.concatenate([dt, cum_c], axis=0)
            for g in range(SSM_GROUPS):
                t0, t1 = 2 * g, 2 * g + 1
                hh_ref[i, :, _tile(t0)] = ht[t0]
                hh_ref[i, :, _tile(t1)] = ht[t1]
                e0, e1 = _dot_sel(both, e_ref[:, _tile(t0)]), _dot_sel(both, e_ref[:, _tile(t1)])
                y0, ht[t0], y1, ht[t1] = _ssd_group(e0[:CHUNK], e1[:CHUNK], e0[CHUNK:], e1[CHUNK:],
                                                    xs_ref[r, _tile(t0)], xs_ref[r, _tile(t1)], bm_ref[r, _tile(g)],
                                                    cm_ref[r, _tile(g)], ht[t0], ht[t1], eye2, bdm, last)
                ys_ref[r, _tile(t0)] = y0.astype(ys_ref.dtype)
                ys_ref[r, _tile(t1)] = y1.astype(ys_ref.dtype)
        for h in range(RET_HEADS):
            st_sc[h * RET_QK_DIM:(h + 1) * RET_QK_DIM, :] = st[h]
        for t in range(N_LTILE):
            ht_sc[:, _tile(t)] = ht[t]

    in_specs = _scan_in_specs(nb, False, rows) + [_full((1, DT_PAD)), _full((1, DT_PAD))] + _const_specs(consts)
    return pl.pallas_call(
        _with_rider(body, len(in_specs), 4, 2, rider, (nb,)), grid=(nb,),
        in_specs=in_specs + r_in,
        out_specs=[pl.BlockSpec((rows, RET_V), lambda c: (c, 0)),
                   pl.BlockSpec((rows, SSM_INNER), lambda c: (c, 0)),
                   pl.BlockSpec((sub, RET_QK, RET_V_DIM), lambda c: (c, 0, 0)),
                   pl.BlockSpec((sub, SSM_STATE, SSM_INNER), lambda c: (c, 0, 0))] + r_out,
        out_shape=[_sds((s, RET_V), MXU_DTYPE), _sds((s, SSM_INNER), MXU_DTYPE),
                   _sds((nc, RET_QK, RET_V_DIM), F32), _sds((nc, SSM_STATE, SSM_INNER), F32)] + r_shape,
        scratch_shapes=[pltpu.VMEM((RET_QK, RET_V_DIM), F32), pltpu.VMEM((SSM_STATE, SSM_INNER), F32)] + r_scratch,
        compiler_params=_cparams(("arbitrary",)), name=name,
    )(proj, proj, proj, xbc, xbc, xbc, dtraw, cos2, sin2, a_c, dtb, *consts, *r_arrs)


def _scan_bwd(proj, xbc, dtraw, cos2, sin2, a_c, dtb, s_hist, h_hist, dyr, dys, dxs_skip, name, rider=None):
    s = proj.shape[0]
    nc = s // CHUNK
    sub = _scan_sub(s)
    nb, rows = nc // sub, CHUNK * sub
    consts, gam = _scan_tables()
    rv = lambda c: nb - 1 - c
    r_arrs, r_in, r_out, r_shape, r_scratch = _rider_args(rider)

    def body(q_ref, k_ref, v_ref, xs_ref, bm_ref, cm_ref, dt_ref, cos_ref, sin_ref, ac_ref, dtb_ref,
             ri_ref, qd_ref, kd_ref, eye_ref, bdm_ref, last_ref, e_ref, et_ref, tri_ref, trit_ref,
             sh_ref, hh_ref, dyr_ref, dys_ref, dsk_ref,
             dqk_ref, dv_ref, dxbc_ref, ddt_ref, dac_ref, ddtb_ref, dst_sc, dht_sc):
        @pl.when(pl.program_id(0) == 0)
        def _():
            dst_sc[...] = jnp.zeros_like(dst_sc)
            dht_sc[...] = jnp.zeros_like(dht_sc)
            dac_ref[...] = jnp.zeros_like(dac_ref)
            ddtb_ref[...] = jnp.zeros_like(ddtb_ref)

        dtb_v, a_c, tri = dtb_ref[...], ac_ref[...], tri_ref[...]
        eye2, bdm, last = eye_ref[...], bdm_ref[...], last_ref[...]
        group = functools.partial(_ssd_group, eye2=eye2, bdm=bdm, last=last)
        dst = [dst_sc[h * RET_QK_DIM:(h + 1) * RET_QK_DIM, :] for h in range(RET_HEADS)]
        dht = [dht_sc[:, _tile(t)] for t in range(N_LTILE)]
        ddtb = jnp.zeros((1, DT_PAD), F32)
        dac = jnp.zeros((1, DT_PAD), F32)
        for i in reversed(range(sub)):
            r = slice(i * CHUNK, (i + 1) * CHUNK)
            cos2, sin2 = cos_ref[r, :], sin_ref[r, :]
            for h in range(RET_HEADS):
                ql = slice(h * RET_QK_DIM, (h + 1) * RET_QK_DIM)
                vl = slice(h * RET_V_DIM, (h + 1) * RET_V_DIM)
                step = functools.partial(_ret_step, r_intra=ri_ref[h], qd=qd_ref[:, ql], kd=kd_ref[:, ql], gamma=gam[h])
                _, vjp = jax.vjp(step, _rope(q_ref[r, ql].astype(F32), cos2, sin2), _rope(k_ref[r, ql].astype(F32), cos2, sin2),
                                 v_ref[r, vl].astype(F32), sh_ref[i, ql, :])
                dq, dk, dv, dst[h] = vjp((dyr_ref[r, vl].astype(F32), dst[h]))
                dqk_ref[r, ql] = _rope_t(dq, cos2, sin2).astype(dqk_ref.dtype)
                dqk_ref[r, slice(RET_QK + ql.start, RET_QK + ql.stop)] = _rope_t(dk, cos2, sin2).astype(dqk_ref.dtype)
                dv_ref[r, vl] = dv.astype(dv_ref.dtype)
            dtraw_v = dt_ref[r, :]
            dt, cum_c = _ssd_heads(dtraw_v, dtb_v, a_c, tri)
            both = jnp.concatenate([dt, cum_c], axis=0)
            d_both = jnp.zeros((2 * CHUNK, LANES), F32)
            for g in range(SSM_GROUPS):
                t0, t1 = 2 * g, 2 * g + 1
                e0, e1 = _dot_sel(both, e_ref[:, _tile(t0)]), _dot_sel(both, e_ref[:, _tile(t1)])
                _, vjp = jax.vjp(group, e0[:CHUNK], e1[:CHUNK], e0[CHUNK:], e1[CHUNK:],
                                 xs_ref[r, _tile(t0)], xs_ref[r, _tile(t1)], bm_ref[r, _tile(g)], cm_ref[r, _tile(g)],
                                 hh_ref[i, :, _tile(t0)], hh_ref[i, :, _tile(t1)])
                (d_dte0, d_dte1, d_cum0, d_cum1, d_xs0, d_xs1, d_bm, d_cm, dht[t0], dht[t1]) = vjp(
                    (dys_ref[r, _tile(t0)].astype(F32), dht[t0], dys_ref[r, _tile(t1)].astype(F32), dht[t1]))
                d_both = d_both + _dot_sel(jnp.concatenate([d_dte0, d_cum0], axis=0), et_ref[_tile(t0), :])
                d_both = d_both + _dot_sel(jnp.concatenate([d_dte1, d_cum1], axis=0), et_ref[_tile(t1), :])
                dxbc_ref[r, _tile(t0)] = d_xs0 + dsk_ref[r, _tile(t0)].astype(F32)
                dxbc_ref[r, _tile(t1)] = d_xs1 + dsk_ref[r, _tile(t1)].astype(F32)
                dxbc_ref[r, _tile(N_LTILE + g)] = d_bm
                dxbc_ref[r, _tile(N_LTILE + SSM_GROUPS + g)] = d_cm
            d_da = _dot_sel(d_both[CHUNK:], trit_ref[...], left=True)
            d_dt = d_both[:CHUNK] + d_da * a_c
            d_pre = d_dt * jax.nn.sigmoid(dtraw_v + dtb_v)
            ddt_ref[r, :] = d_pre
            ddtb = ddtb + jnp.sum(d_pre, axis=0, keepdims=True)
            dac = dac + jnp.sum(d_da * dt, axis=0, keepdims=True)
        ddtb_ref[...] += ddtb
        dac_ref[...] += dac
        for h in range(RET_HEADS):
            dst_sc[h * RET_QK_DIM:(h + 1) * RET_QK_DIM, :] = dst[h]
        for t in range(N_LTILE):
            dht_sc[:, _tile(t)] = dht[t]

    in_specs = (_scan_in_specs(nb, True, rows) + [_full((1, DT_PAD)), _full((1, DT_PAD))] + _const_specs(consts)
                + [pl.BlockSpec((sub, RET_QK, RET_V_DIM), lambda c: (rv(c), 0, 0)),
                   pl.BlockSpec((sub, SSM_STATE, SSM_INNER), lambda c: (rv(c), 0, 0)),
                   pl.BlockSpec((rows, RET_V), lambda c: (rv(c), 0)),
                   pl.BlockSpec((rows, SSM_INNER), lambda c: (rv(c), 0)),
                   pl.BlockSpec((rows, SSM_INNER), lambda c: (rv(c), 0))])
    return pl.pallas_call(
        _with_rider(body, len(in_specs), 6, 2, rider, (nb,)), grid=(nb,),
        in_specs=in_specs + r_in,
        out_specs=[pl.BlockSpec((rows, 2 * RET_QK), lambda c: (rv(c), 0)),
                   pl.BlockSpec((rows, RET_V), lambda c: (rv(c), 0)),
                   pl.BlockSpec((rows, SSM_CONV_DIM), lambda c: (rv(c), 0)),
                   pl.BlockSpec((rows, DT_PAD), lambda c: (rv(c), 0)),
                   _full((1, DT_PAD)), _full((1, DT_PAD))] + r_out,
        out_shape=[_sds((s, 2 * RET_QK), MXU_DTYPE), _sds((s, RET_V), MXU_DTYPE),
                   _sds((s, SSM_CONV_DIM), F32), _sds((s, DT_PAD), F32),
                   _sds((1, DT_PAD), F32), _sds((1, DT_PAD), F32)] + r_shape,
        scratch_shapes=[pltpu.VMEM((RET_QK, RET_V_DIM), F32), pltpu.VMEM((SSM_STATE, SSM_INNER), F32)] + r_scratch,
        compiler_params=_cparams(("arbitrary",)), name=name,
    )(proj, proj, proj, xbc, xbc, xbc, dtraw, cos2, sin2, a_c, dtb, *consts, s_hist, h_hist, dyr, dys, dxs_skip, *r_arrs)


POST_W = 256


def _post_ret(y, g):
    return _rms(y) * _silu(g)


def _post_ssm(y, xs, z, dsk, nw):
    return _rms((y + xs * dsk) * _silu(z)) * nw


def _post_specs(t):
    return [pl.BlockSpec((t, RET_V), lambda i: (i, 0)),
            pl.BlockSpec((t, RET_V), lambda i: (i, COL_G // RET_V)),
            pl.BlockSpec((t, SSM_INNER), lambda i: (i, 0)),
            pl.BlockSpec((t, SSM_INNER), lambda i: (i, 0)),
            pl.BlockSpec((t, SSM_INNER), lambda i: (i, COL_Z // SSM_INNER)),
            _full((1, SSM_INNER)), _full((1, SSM_INNER))]


def _post_fwd(y_ret, proj, y_ssm, xbc, dsk_e, ssm_norm, name):
    s = y_ret.shape[0]
    t = _pick(s, 256)

    def body(yr_ref, g_ref, ys_ref, xs_ref, z_ref, dsk_ref, nw_ref, or_ref, os_ref, ort_ref, ost_ref):
        for h in range(RET_V // POST_W):
            sl = slice(h * POST_W, (h + 1) * POST_W)
            o = _post_ret(yr_ref[:, sl].astype(F32), g_ref[:, sl].astype(F32))
            or_ref[:, sl] = o.astype(or_ref.dtype)
            ort_ref[sl, :] = o.T.astype(ort_ref.dtype)
        for g in range(SSM_INNER // POST_W):
            sl = slice(g * POST_W, (g + 1) * POST_W)
            o = _post_ssm(ys_ref[:, sl].astype(F32), xs_ref[:, sl], z_ref[:, sl].astype(F32), dsk_ref[:, sl], nw_ref[:, sl])
            os_ref[:, sl] = o.astype(os_ref.dtype)
            ost_ref[sl, :] = o.T.astype(ost_ref.dtype)

    return pl.pallas_call(
        body, grid=(s // t,), in_specs=_post_specs(t),
        out_specs=[pl.BlockSpec((t, RET_V), lambda i: (i, 0)), pl.BlockSpec((t, SSM_INNER), lambda i: (i, 0)),
                   pl.BlockSpec((RET_V, t), lambda i: (0, i)), pl.BlockSpec((SSM_INNER, t), lambda i: (0, i))],
        out_shape=[_sds((s, RET_V), MXU_DTYPE), _sds((s, SSM_INNER), MXU_DTYPE),
                   _sds((RET_V, s), MXU_DTYPE), _sds((SSM_INNER, s), MXU_DTYPE)],
        compiler_params=_cparams(("parallel",)), name=name,
    )(y_ret, proj, y_ssm, xbc, proj, dsk_e, ssm_norm.reshape(1, SSM_INNER))


def _post_bwd(y_ret, proj, y_ssm, xbc, dsk_e, ssm_norm, d_or, d_os, name, rider=None):
    s = y_ret.shape[0]
    t = _pick(s, 256)
    r_arrs, r_in, r_out, r_shape, r_scratch = _rider_args(rider)

    def body(yr_ref, g_ref, ys_ref, xs_ref, z_ref, dsk_ref, nw_ref, dor_ref, dos_ref,
             dyr_ref, dg_ref, dys_ref, dxs_ref, dz_ref, ddsk_ref, dnw_ref):
        @pl.when(pl.program_id(0) == 0)
        def _():
            ddsk_ref[...] = jnp.zeros_like(ddsk_ref)
            dnw_ref[...] = jnp.zeros_like(dnw_ref)

        for h in range(RET_V // POST_W):
            sl = slice(h * POST_W, (h + 1) * POST_W)
            _, vjp = jax.vjp(_post_ret, yr_ref[:, sl].astype(F32), g_ref[:, sl].astype(F32))
            dyr, dg = vjp(dor_ref[:, sl].astype(F32))
            dyr_ref[:, sl] = dyr.astype(dyr_ref.dtype)
            dg_ref[:, sl] = dg.astype(dg_ref.dtype)
        for g in range(SSM_INNER // POST_W):
            sl = slice(g * POST_W, (g + 1) * POST_W)
            _, vjp = jax.vjp(_post_ssm, ys_ref[:, sl].astype(F32), xs_ref[:, sl], z_ref[:, sl].astype(F32), dsk_ref[:, sl],
                             nw_ref[:, sl])
            dy, dxs, dz, ddsk, dnw = vjp(dos_ref[:, sl].astype(F32))
            dys_ref[:, sl] = dy.astype(dys_ref.dtype)
            dxs_ref[:, sl] = dxs.astype(dxs_ref.dtype)
            dz_ref[:, sl] = dz.astype(dz_ref.dtype)
            ddsk_ref[:, sl] += ddsk
            dnw_ref[:, sl] += dnw

    rowv = pl.BlockSpec((t, RET_V), lambda i: (i, 0))
    rows = pl.BlockSpec((t, SSM_INNER), lambda i: (i, 0))
    in_specs = _post_specs(t) + [rowv, rows]
    return pl.pallas_call(
        _with_rider(body, len(in_specs), 7, 0, rider, (s // t,)), grid=(s // t,), in_specs=in_specs + r_in,
        out_specs=[rowv, rowv, rows, rows, rows, _full((1, SSM_INNER)), _full((1, SSM_INNER))] + r_out,
        out_shape=[_sds((s, RET_V), MXU_DTYPE), _sds((s, RET_V), MXU_DTYPE), _sds((s, SSM_INNER), MXU_DTYPE),
                   _sds((s, SSM_INNER), MXU_DTYPE), _sds((s, SSM_INNER), MXU_DTYPE),
                   _sds((1, SSM_INNER), F32), _sds((1, SSM_INNER), F32)] + r_shape,
        scratch_shapes=r_scratch,
        compiler_params=_cparams(("arbitrary",)), name=name,
    )(y_ret, proj, y_ssm, xbc, proj, dsk_e, ssm_norm.reshape(1, SSM_INNER), d_or, d_os, *r_arrs)


def _merge_fn(gr, gs, br, bs, yr, ys):
    return jax.nn.sigmoid(gr + br) * yr + jax.nn.sigmoid(gs + bs) * ys


def _merge_specs(t):
    row = pl.BlockSpec((t, D_MODEL), lambda i: (i, 0))
    return [pl.BlockSpec((t, D_MODEL), lambda i: (i, COL_GATES // D_MODEL)),
            pl.BlockSpec((t, D_MODEL), lambda i: (i, COL_GATES // D_MODEL + 1)),
            pl.BlockSpec((1, D_MODEL), lambda i: (0, 0)), pl.BlockSpec((1, D_MODEL), lambda i: (0, 1)), row, row]


def _merge_fwd(proj, b_gate, br_ret, br_ssm, name):
    s = proj.shape[0]
    t = _pick(s, 512)

    def body(gr_ref, gs_ref, br_ref, bs_ref, yr_ref, ys_ref, o_ref, ot_ref):
        o = _merge_fn(gr_ref[...].astype(F32), gs_ref[...].astype(F32), br_ref[...], bs_ref[...], yr_ref[...], ys_ref[...])
        o_ref[...] = o.astype(o_ref.dtype)
        ot_ref[...] = o.T.astype(ot_ref.dtype)

    bg = b_gate.reshape(1, 2 * D_MODEL)
    return pl.pallas_call(
        body, grid=(s // t,), in_specs=_merge_specs(t),
        out_specs=[pl.BlockSpec((t, D_MODEL), lambda i: (i, 0)), pl.BlockSpec((D_MODEL, t), lambda i: (0, i))],
        out_shape=[_sds((s, D_MODEL), MXU_DTYPE), _sds((D_MODEL, s), MXU_DTYPE)],
        compiler_params=_cparams(("parallel",)), name=name,
    )(proj, proj, bg, bg, br_ret, br_ssm)


def _merge_bwd(proj, b_gate, br_ret, br_ssm, dm, name):
    s = proj.shape[0]
    t = _pick(s, 512)

    def body(gr_ref, gs_ref, br_ref, bs_ref, yr_ref, ys_ref, dm_ref, dgt_ref, db_ref, dyr_ref, dys_ref):
        @pl.when(pl.program_id(0) == 0)
        def _():
            db_ref[...] = jnp.zeros_like(db_ref)

        _, vjp = jax.vjp(_merge_fn, gr_ref[...].astype(F32), gs_ref[...].astype(F32), br_ref[...], bs_ref[...],
                         yr_ref[...], ys_ref[...])
        dgr, dgs, dbr, dbs, dyr, dys = vjp(dm_ref[...])
        dgt_ref[:, :D_MODEL] = dgr.astype(dgt_ref.dtype)
        dgt_ref[:, D_MODEL:] = dgs.astype(dgt_ref.dtype)
        db_ref[:, :D_MODEL] += dbr
        db_ref[:, D_MODEL:] += dbs
        dyr_ref[...] = dyr.astype(dyr_ref.dtype)
        dys_ref[...] = dys.astype(dys_ref.dtype)

    bg = b_gate.reshape(1, 2 * D_MODEL)
    row = pl.BlockSpec((t, D_MODEL), lambda i: (i, 0))
    return pl.pallas_call(
        body, grid=(s // t,), in_specs=_merge_specs(t) + [row],
        out_specs=[pl.BlockSpec((t, 2 * D_MODEL), lambda i: (i, 0)), _full((1, 2 * D_MODEL)), row, row],
        out_shape=[_sds((s, 2 * D_MODEL), MXU_DTYPE), _sds((1, 2 * D_MODEL), F32),
                   _sds((s, D_MODEL), MXU_DTYPE), _sds((s, D_MODEL), MXU_DTYPE)],
        compiler_params=_cparams(("arbitrary",)), name=name,
    )(proj, proj, bg, bg, br_ret, br_ssm, dm)


def _attn_head(q, k, v):
    sc = _mm(q, k, 'nt') * (XA_HEAD_DIM ** -0.5)
    e = jnp.exp(sc - lax.stop_gradient(jnp.max(sc, axis=-1, keepdims=True)))
    p = e / jnp.sum(e, axis=-1, keepdims=True)
    return _mm(p, v, 'nn')


def _attn_fwd(q, kv, name):
    s = q.shape[0]
    m = kv.shape[0]
    t = _pick(s, 512)

    def body(q_ref, kv_ref, o_ref, ot_ref):
        for h in range(XA_HEADS):
            sl = slice(h * XA_HEAD_DIM, (h + 1) * XA_HEAD_DIM)
            vl = slice(D_MODEL + h * XA_HEAD_DIM, D_MODEL + (h + 1) * XA_HEAD_DIM)
            o = _attn_head(q_ref[:, sl], kv_ref[:, sl], kv_ref[:, vl])
            o_ref[:, sl] = o.astype(o_ref.dtype)
            ot_ref[sl, :] = o.T.astype(ot_ref.dtype)

    return pl.pallas_call(
        body, grid=(s // t,),
        in_specs=[pl.BlockSpec((t, D_MODEL), lambda i: (i, 0)), _full((m, 2 * D_MODEL))],
        out_specs=[pl.BlockSpec((t, D_MODEL), lambda i: (i, 0)), pl.BlockSpec((D_MODEL, t), lambda i: (0, i))],
        out_shape=[_sds((s, D_MODEL), MXU_DTYPE), _sds((D_MODEL, s), MXU_DTYPE)],
        compiler_params=_cparams(("parallel",)), name=name,
    )(q, kv)


def _attn_bwd(q, kv, d_o, name):
    s = q.shape[0]
    m = kv.shape[0]
    t = _pick(s, 512)

    def body(q_ref, kv_ref, do_ref, dq_ref, dkv_ref):
        @pl.when(pl.program_id(0) == 0)
        def _():
            dkv_ref[...] = jnp.zeros_like(dkv_ref)

        for h in range(XA_HEADS):
            sl = slice(h * XA_HEAD_DIM, (h + 1) * XA_HEAD_DIM)
            vl = slice(D_MODEL + h * XA_HEAD_DIM, D_MODEL + (h + 1) * XA_HEAD_DIM)
            _, vjp = jax.vjp(_attn_head, q_ref[:, sl], kv_ref[:, sl], kv_ref[:, vl])
            dq, dk, dv = vjp(do_ref[:, sl])
            dq_ref[:, sl] = dq.astype(dq_ref.dtype)
            dkv_ref[:, sl] += dk
            dkv_ref[:, vl] += dv

    row = pl.BlockSpec((t, D_MODEL), lambda i: (i, 0))
    return pl.pallas_call(
        body, grid=(s // t,), in_specs=[row, _full((m, 2 * D_MODEL)), row],
        out_specs=[row, _full((m, 2 * D_MODEL))],
        out_shape=[_sds((s, D_MODEL), MXU_DTYPE), _sds((m, 2 * D_MODEL), F32)],
        compiler_params=_cparams(("arbitrary",)), name=name,
    )(q, kv, d_o)


def _loss_head(x, w, target, name):
    s, d = x.shape
    t = _pick(s, 512)

    def body(x_ref, w_ref, t_ref, loss_ref, dx_ref, dxb_ref, dw_ref):
        @pl.when(pl.program_id(0) == 0)
        def _():
            loss_ref[...] = jnp.zeros_like(loss_ref)
            dw_ref[...] = jnp.zeros_like(dw_ref)

        y, vjp = jax.vjp(_rmsnorm_fn, x_ref[...], w_ref[...])
        err = y - t_ref[...]
        loss_ref[...] += 0.5 * jnp.sum(jnp.sum(err * err, axis=-1, keepdims=True), axis=0, keepdims=True) / d
        dx, dw = vjp(err * (1.0 / d))
        dx_ref[...] = dx
        dxb_ref[...] = dx.astype(dxb_ref.dtype)
        dw_ref[...] += dw

    row = pl.BlockSpec((t, d), lambda i: (i, 0))
    return pl.pallas_call(
        body, grid=(s // t,), in_specs=[row, _full((1, d)), row],
        out_specs=[_full((1, LANES)), row, row, _full((1, d))],
        out_shape=[_sds((1, LANES), F32), _sds((s, d), F32), _sds((s, d), MXU_DTYPE), _sds((1, d), F32)],
        compiler_params=_cparams(("arbitrary",)), name=name,
    )(x, w.reshape(1, d), target)


def _epi_sqrelu(acc):
    r = jnp.maximum(acc, 0.0)
    return r * r, r * r


def _epi_sqrelu_bwd(acc, act):
    return (acc * (2.0 * jnp.sqrt(act.astype(F32))),)


def _rope_tables(positions):
    inv_freq = ROPE_THETA ** (-jnp.arange(0, RET_QK_DIM, 2, dtype=F32) / RET_QK_DIM)
    ang = positions.astype(F32)[:, None] * inv_freq
    cos, sin = jnp.cos(ang), jnp.sin(ang)
    return jnp.concatenate([cos, cos], axis=1), jnp.concatenate([-sin, sin], axis=1)


W_IN_ORIG = (('q', 0, 512), ('k', 512, 1024), ('v', 1024, 2048), ('g', 2048, 3072), ('z', 3072, 5120),
             ('xbc', 5120, 9216), ('dt', 9216, 9248), ('gates', 9248, 11296))
W_IN_MAIN_ORDER = ('z', 'xbc', 'gates', 'v', 'g', 'q', 'k')
W_IN_SHARD = IN_DIM // N_DEV


def _shard_segments(lo, hi):
    segs = []
    for j in range(lo // W_IN_SHARD, (hi - 1) // W_IN_SHARD + 1):
        segs.append((j, max(lo, j * W_IN_SHARD) - j * W_IN_SHARD, min(hi, (j + 1) * W_IN_SHARD) - j * W_IN_SHARD))
    return segs


def _w_in_from_shards(g):
    rng = {name: (lo, hi) for name, lo, hi in W_IN_ORIG}
    cols = [g[j][:, a:b] for name in W_IN_MAIN_ORDER for j, a, b in _shard_segments(*rng[name])]
    (j, a, b), = _shard_segments(*rng['dt'])
    return jnp.concatenate(cols, axis=1), jnp.pad(g[j][:, a:b], ((0, 0), (0, DT_PAD - SSM_HEADS)))


def _w_in_grad_blocks(d, d_dt):
    src_of = {'q': ('qk', 0), 'k': ('qk', RET_QK)}
    blocks = []
    for j in range(N_DEV):
        lo_j, hi_j = j * W_IN_SHARD, (j + 1) * W_IN_SHARD
        cols = []
        for name, lo, hi in W_IN_ORIG:
            a, b = max(lo, lo_j), min(hi, hi_j)
            if a >= b:
                continue
            if name == 'dt':
                cols.append(d_dt[:, a - lo:b - lo])
            else:
                key, off = src_of.get(name, (name, 0))
                cols.append(d[key][:, off + a - lo:off + b - lo])
        blocks.append(jnp.concatenate(cols, axis=1))
    return blocks


def _lanes_of_heads(v):
    return jnp.repeat(v, SSM_HEAD_DIM).reshape(1, SSM_INNER)


def _heads_of_lanes(v):
    return v.reshape(SSM_HEADS, SSM_HEAD_DIM).sum(axis=1)


def _layer_fwd(x, mem, cos2, sin2, p, l, later_blocks, rider_proj=None):
    n = lambda s: f"{s}_l{l}"
    sv = {'x0': x}
    u, u_t = _rmsnorm(x, p['norm_mix'], n("norm_mix"))
    s = x.shape[0]
    proj = _matmul(u, p['w_in_main'], 'nn', n("in_proj"), out_dtypes=(MXU_DTYPE,), tiles=(s, 512, D_MODEL), rider=rider_proj)
    if rider_proj is not None:
        proj, next_first = proj
    else:
        next_first = []
    dtraw = _matmul(u, p['w_in_dt'], 'nn', n("in_proj_dt"))
    xbc = _conv_fwd(proj, p['conv_w'], p['conv_b'], n("conv"))
    a_c = jnp.pad(-jnp.exp(p['a_log']), (0, DT_PAD - SSM_HEADS)).reshape(1, DT_PAD)
    dtb = jnp.pad(p['dt_bias'], (0, DT_PAD - SSM_HEADS)).reshape(1, DT_PAD)
    y_ret, y_ssm, s_hist, h_hist, *gathered = _scan_fwd(proj, xbc, dtraw, cos2, sin2, a_c, dtb, n("scan"),
                                                        _AllGather(later_blocks))
    p = {**p, **_full_weights(LATER, gathered)}
    dsk_e = _lanes_of_heads(p['d_skip'])
    o_ret, o_ssm, o_ret_t, o_ssm_t = _post_fwd(y_ret, proj, y_ssm, xbc, dsk_e, p['ssm_norm'], n("post"))
    br_ret = _matmul(o_ret, p['w_br_ret'], 'nn', n("br_ret"))
    br_ssm = _matmul(o_ssm, p['w_br_ssm'], 'nn', n("br_ssm"))
    merged, merged_t = _merge_fwd(proj, p['b_gate'], br_ret, br_ssm, n("merge"))
    x1 = _matmul(merged, p['w_out'], 'nn', n("w_out"), extras=(x,), epi=_epi_add)
    sv.update(u_t=u_t, proj=proj, dtraw=dtraw, xbc=xbc, a_c=a_c, dtb=dtb, y_ret=y_ret, y_ssm=y_ssm, s_hist=s_hist,
              h_hist=h_hist, dsk_e=dsk_e, o_ret_t=o_ret_t, o_ssm_t=o_ssm_t, br_ret=br_ret, br_ssm=br_ssm,
              merged_t=merged_t, x1=x1)
    hq, hq_t = _rmsnorm(x1, p['norm_xa'], n("norm_xa"))
    memn, _ = _rmsnorm(mem, p['norm_mem'], n("norm_mem"))
    q = _matmul(hq, p['xa_wq'], 'nn', n("xa_q"))
    kv = _matmul(memn, p['xa_wkv'], 'nn', n("xa_kv"))
    o, o_t = _attn_fwd(q, kv, n("attn"))
    x2 = _matmul(o, p['xa_wo'], 'nn', n("xa_o"), extras=(x1,), epi=_epi_add)
    sv.update(hq_t=hq_t, memn=memn, q=q, kv=kv, o_t=o_t, x2=x2)
    hm, hm_t = _rmsnorm(x2, p['norm_mlp'], n("norm_mlp"))
    act, act_t = _matmul(hm, p['mlp_w1'], 'nn', n("mlp_1"), epi=_epi_sqrelu, out_dtypes=(MXU_DTYPE, MXU_DTYPE),
                         out_t=(False, True), tiles=(s, 512, D_MODEL))
    x3 = _matmul(act, p['mlp_w2'], 'nn', n("mlp_2"), extras=(x2,), epi=_epi_add, tiles=(_pick(s, 1024), D_MODEL, D_FF))
    sv.update(hm_t=hm_t, act=act, act_t=act_t)
    return x3, sv, p, next_first


def _layer_bwd(dx, dxb, mem, cos2, sin2, p, sv, l, pending, c_idx):
    n = lambda s: f"{s}_bwd_l{l}"
    gd = (MXU_DTYPE,)
    g = {}
    g['mlp_w2'] = _matmul(sv['act_t'], dxb, 'nn', n("mlp_2_dw"), out_dtypes=gd, scatter='rows')
    da = _matmul(dxb, p['mlp_w2'], 'nt', n("mlp_2_dx"), extras=(sv['act'],), epi=_epi_sqrelu_bwd, out_dtypes=(MXU_DTYPE,),
                 tiles=(_pick(dxb.shape[0], 512), D_FF, D_MODEL))
    g['mlp_w1'] = _matmul(sv['hm_t'], da, 'nn', n("mlp_1_dw"), out_dtypes=gd, scatter='cols')
    dhm = _matmul(da, p['mlp_w1'], 'nt', n("mlp_1_dx"))
    dx2, dx2b, g['norm_mlp'] = _rmsnorm_bwd(sv['x2'], p['norm_mlp'], dhm, dx, n("norm_mlp"))
    g['xa_wo'] = _matmul(sv['o_t'], dx2b, 'nn', n("xa_o_dw"), out_dtypes=gd, scatter='rows')
    d_o = _matmul(dx2b, p['xa_wo'], 'nt', n("xa_o_dx"))
    dq, dkv = _attn_bwd(sv['q'], sv['kv'], d_o, n("attn"))
    g['xa_wq'] = _matmul(sv['hq_t'], dq, 'nn', n("xa_q_dw"), out_dtypes=gd, scatter='rows')
    dhq = _matmul(dq, p['xa_wq'], 'nt', n("xa_q_dx"))
    g['xa_wkv'] = _matmul(sv['memn'], dkv, 'tn', n("xa_kv_dw"), out_dtypes=gd, scatter='cols')
    dmemn = _matmul(dkv, p['xa_wkv'], 'nt', n("xa_kv_dx"))
    _, _, g['norm_mem'] = _rmsnorm_bwd(mem, p['norm_mem'], dmemn, None, n("norm_mem"))
    dx1, dx1b, g['norm_xa'] = _rmsnorm_bwd(sv['x1'], p['norm_xa'], dhq, dx2, n("norm_xa"))
    g['w_out'] = _matmul(sv['merged_t'], dx1b, 'nn', n("w_out_dw"), out_dtypes=gd, scatter='rows')
    dmerged = _matmul(dx1b, p['w_out'], 'nt', n("w_out_dx"))
    dgates, g['b_gate'], dbr_ret, dbr_ssm = _merge_bwd(sv['proj'], p['b_gate'], sv['br_ret'], sv['br_ssm'], dmerged, n("merge"))
    g['w_br_ret'] = _matmul(sv['o_ret_t'], dbr_ret, 'nn', n("br_ret_dw"), out_dtypes=gd, scatter='rows')
    g['w_br_ssm'] = _matmul(sv['o_ssm_t'], dbr_ssm, 'nn', n("br_ssm_dw"), out_dtypes=gd, scatter='rows')
    d_or = _matmul(dbr_ret, p['w_br_ret'], 'nt', n("br_ret_dx"), out_dtypes=(MXU_DTYPE,))
    d_os = _matmul(dbr_ssm, p['w_br_ssm'], 'nt', n("br_ssm_dx"), out_dtypes=(MXU_DTYPE,))
    later_by_core = [_grad_scatter(k, g[k]) for k in LATER]
    dyr, dg, dys, dxs_skip, dz, ddsk_e, g['ssm_norm'], *later_sib = _post_bwd(
        sv['y_ret'], sv['proj'], sv['y_ssm'], sv['xbc'], sv['dsk_e'], p['ssm_norm'], d_or, d_os, n("post"),
        _ExchangeCores(later_by_core))
    rider = _ExchangeChips(list(pending) + _core_sums(LATER, later_by_core, later_sib, c_idx, l))
    g['d_skip'] = _heads_of_lanes(ddsk_e)
    dqk_r, dv_r, dxbc_act, ddtraw, dac, ddtb, *delivered = _scan_bwd(
        sv['proj'], sv['xbc'], sv['dtraw'], cos2, sin2, sv['a_c'], sv['dtb'], sv['s_hist'], sv['h_hist'],
        dyr, dys, dxs_skip, n("scan"), rider)
    g['a_log'] = dac[0, :SSM_HEADS] * (-jnp.exp(p['a_log']))
    g['dt_bias'] = ddtb[0, :SSM_HEADS]
    dxbc_raw, g['conv_w'], g['conv_b'] = _conv_bwd(sv['proj'], p['conv_w'], p['conv_b'], dxbc_act, n("conv"))
    pieces = {'z': dz, 'xbc': dxbc_raw, 'gates': dgates, 'v': dv_r, 'g': dg, 'qk': dqk_r}
    d_w = {k: _matmul(sv['u_t'], pc, 'nn', n(f"in_proj_dw_{k}"), out_dtypes=gd) for k, pc in pieces.items()}
    d_dt = _matmul(sv['u_t'], ddtraw, 'nn', n("in_proj_dt_dw"), out_dtypes=gd)
    g['w_in'] = _w_in_grad_blocks(d_w, d_dt)
    du_dt = _matmul(ddtraw, p['w_in_dt'], 'nt', n("in_proj_dt_dx"))
    first_by_core = [_grad_scatter(k, g[k]) for k in FIRST]
    du, first_sib = _matmul_nt_pieces(list(pieces.values()), p['w_in_main'], n("in_proj_dx"), extras=(du_dt,), epi=_epi_add,
                                      rider=_ExchangeCores(first_by_core))
    dx0, dx0b, g['norm_mix'] = _rmsnorm_bwd(sv['x0'], p['norm_mix'], du, dx1, n("norm_mix"))
    return (dx0, dx0b, g, delivered[:len(pending)], delivered[len(pending):],
            _core_sums(FIRST, first_by_core, first_sib, c_idx, l))


def _full_weights(names, gathered):
    p = {}
    for k, g in zip(names, gathered):
        if k == 'w_in':
            p['w_in_main'], p['w_in_dt'] = _w_in_from_shards(g)
        elif k in COL_SHARDED:
            p[k] = jnp.concatenate([g[j] for j in range(N_DEV)], axis=1)
        else:
            p[k] = g.reshape(-1, g.shape[-1])
    return p


def _grad_scatter(k, g):
    if k in LATER:
        return g
    if k == 'w_in':
        blocks = g
    elif k in COL_SHARDED:
        c = g.shape[1] // N_DEV
        blocks = [g[:, j * c:(j + 1) * c] for j in range(N_DEV)]
    else:
        r = g.shape[0] // N_DEV
        blocks = [g[j * r:(j + 1) * r] for j in range(N_DEV)]
    return jnp.stack([jnp.stack([blocks[2 * chip + core] for chip in range(4)]) for core in range(2)])


def _core_sums(names, by_core, from_sibling, c_idx, l):
    return [_add_halves(a, o, c_idx, f"grad_add_cores_{k}_l{l}") for k, a, o in zip(names, by_core, from_sibling)]


def _step(x, mem, positions, small, blocks, loss_target):
    cos2, sin2 = _rope_tables(positions)
    first = _run_exchange(_AllGather([blocks[0][k] for k in FIRST]), "all_gather_first_l0")
    saved, layers = [], []
    for l in range(DEPTH):
        p = {k: small[k][l] for k in SMALL if k != 'norm_final'}
        p.update(_full_weights(FIRST, first))
        rider = _AllGather([blocks[l + 1][k] for k in FIRST]) if l + 1 < DEPTH else None
        x, sv, p, first = _layer_fwd(x, mem, cos2, sin2, p, l, [blocks[l][k] for k in LATER], rider)
        saved.append(sv)
        layers.append(p)
    loss, dx, dxb, dnf = _loss_head(x, small['norm_final'], loss_target, "loss_head")
    c_idx = lax.axis_index("c").astype(jnp.int32).reshape(1)
    grads, by_chip, pending = [None] * DEPTH, [dict() for _ in range(DEPTH)], []
    for l in reversed(range(DEPTH)):
        dx, dxb, grads[l], got_first, got_later, pending_next = _layer_bwd(
            dx, dxb, mem, cos2, sin2, layers[l], saved[l], l, pending, c_idx)
        if pending:
            by_chip[l + 1].update(zip(FIRST, got_first))
        by_chip[l].update(zip(LATER, got_later))
        pending = pending_next
    by_chip[0].update(zip(FIRST, _run_exchange(_ExchangeChips(pending), "grad_exchange_chips_first_l0")))
    small_g = {}
    for k in SMALL:
        small_g[k] = dnf.reshape(D_MODEL) if k == 'norm_final' else [grads[l][k].reshape(small[k].shape[1:]) for l in range(DEPTH)]
    return loss, dx, small_g, by_chip


MESH = pl.DeviceIdType.MESH
ANY_SPEC = pl.BlockSpec(memory_space=pl.ANY)


def _mesh_pos():
    return lax.axis_index("x"), lax.axis_index("y"), lax.axis_index("c")


def _other_chips(x, y):
    return [(1 - x, y), (x, 1 - y), (1 - x, 1 - y)]


class _AllGather:
    def __init__(self, arrs):
        self.arrs = list(arrs)
        na = self.n = len(self.arrs)
        self.out_shape = [_sds((N_DEV,) + a.shape, a.dtype) for a in self.arrs]
        self.scratch = [pltpu.SemaphoreType.DMA((na, 7)), pltpu.SemaphoreType.DMA((na, 7)), pltpu.SemaphoreType.DMA((na,))]

    def _copies(self, x_refs, o_refs, sems):
        send_sems, recv_sems, local_sems = sems
        x, y, c = _mesh_pos()
        me, sib = (x, y, c), (x, y, 1 - c)
        chips = _other_chips(x, y)

        def copy(a, k, block, to, src=None):
            dst = o_refs[a].at[4 * block[0] + 2 * block[1] + block[2]]
            return pltpu.make_async_remote_copy(src_ref=dst if src is None else src, dst_ref=dst,
                                                send_sem=send_sems.at[a, k], recv_sem=recv_sems.at[a, k],
                                                device_id=to, device_id_type=MESH)

        mine = [pltpu.make_async_copy(x_refs[a], o_refs[a].at[4 * x + 2 * y + c], local_sems.at[a]) for a in range(self.n)]
        first = []
        for a in range(self.n):
            first.append(copy(a, 0, me, sib, src=x_refs[a]))
            first += [copy(a, 1 + j, me, (*chip, c), src=x_refs[a]) for j, chip in enumerate(chips)]
        return copy, mine, first, me, sib, chips, c

    def start(self, x_refs, o_refs, sems):
        _, mine, first, *_ = self._copies(x_refs, o_refs, sems)
        for cp in mine + first:
            cp.start()

    def finish(self, x_refs, o_refs, sems):
        copy, mine, first, me, sib, chips, c = self._copies(x_refs, o_refs, sems)
        passed = []
        for a in range(self.n):
            for j, chip in enumerate(chips):
                copy(a, 1 + j, (*chip, c), me).wait_recv()
                cp = copy(a, 4 + j, (*chip, c), sib)
                cp.start()
                passed.append(cp)
        for a in range(self.n):
            copy(a, 0, sib, me).wait_recv()
            for j, chip in enumerate(chips):
                copy(a, 4 + j, (*chip, 1 - c), me).wait_recv()
        for cp in first + passed:
            cp.wait_send()
        for cp in mine:
            cp.wait()


class _ExchangeChips:
    def __init__(self, arrs):
        self.arrs = list(arrs)
        na = self.n = len(self.arrs)
        self.out_shape = [_sds(a.shape, a.dtype) for a in self.arrs]
        self.scratch = [pltpu.SemaphoreType.DMA((na, 3)), pltpu.SemaphoreType.DMA((na, 3)), pltpu.SemaphoreType.DMA((na,))]

    def _copies(self, a_refs, o_refs, sems):
        send_sems, recv_sems, local_sems = sems
        x, y, c = _mesh_pos()
        my_chip = 2 * x + y
        chips = _other_chips(x, y)
        mine = [pltpu.make_async_copy(a_refs[a].at[my_chip], o_refs[a].at[my_chip], local_sems.at[a]) for a in range(self.n)]
        sends = [pltpu.make_async_remote_copy(src_ref=a_refs[a].at[2 * px + py], dst_ref=o_refs[a].at[my_chip],
                                              send_sem=send_sems.at[a, j], recv_sem=recv_sems.at[a, j],
                                              device_id=(px, py, c), device_id_type=MESH)
                 for a in range(self.n) for j, (px, py) in enumerate(chips)]
        recvs = [pltpu.make_async_remote_copy(src_ref=a_refs[a].at[2 * px + py], dst_ref=o_refs[a].at[2 * px + py],
                                              send_sem=send_sems.at[a, j], recv_sem=recv_sems.at[a, j],
                                              device_id=(px, py, c), device_id_type=MESH)
                 for a in range(self.n) for j, (px, py) in enumerate(chips)]
        return mine, sends, recvs

    def start(self, a_refs, o_refs, sems):
        mine, sends, _ = self._copies(a_refs, o_refs, sems)
        for cp in mine + sends:
            cp.start()

    def finish(self, a_refs, o_refs, sems):
        mine, sends, recvs = self._copies(a_refs, o_refs, sems)
        for cp in recvs:
            cp.wait_recv()
        for cp in sends:
            cp.wait_send()
        for cp in mine:
            cp.wait()


def _run_exchange(ex, name):
    na = ex.n

    def body(*refs):
        i_refs, o_refs, sems = refs[:na], refs[na:2 * na], refs[2 * na:]
        ex.start(i_refs, o_refs, sems)
        ex.finish(i_refs, o_refs, sems)

    return pl.pallas_call(body, in_specs=[ANY_SPEC] * na, out_specs=[ANY_SPEC] * na, out_shape=ex.out_shape,
                          scratch_shapes=ex.scratch, name=name)(*ex.arrs)


class _ExchangeCores:
    def __init__(self, arrs):
        self.arrs = list(arrs)
        na = self.n = len(self.arrs)
        self.out_shape = [_sds(a.shape[1:], a.dtype) for a in self.arrs]
        self.scratch = [pltpu.SemaphoreType.DMA((na,)), pltpu.SemaphoreType.DMA((na,))]

    def _copies(self, a_refs, o_refs, sems):
        send_sems, recv_sems = sems
        x, y, c = _mesh_pos()
        return [pltpu.make_async_remote_copy(src_ref=a_refs[a].at[1 - c], dst_ref=o_refs[a], send_sem=send_sems.at[a],
                                             recv_sem=recv_sems.at[a], device_id=(x, y, 1 - c), device_id_type=MESH)
                for a in range(self.n)]

    def start(self, a_refs, o_refs, sems):
        for cp in self._copies(a_refs, o_refs, sems):
            cp.start()

    def finish(self, a_refs, o_refs, sems):
        for cp in self._copies(a_refs, o_refs, sems):
            cp.wait()


def _as_rows(a, lead):
    return a.reshape(a.shape[:lead] + (-1, a.shape[-1]))


def _add_halves(a, other, c_idx, name):
    a3, o2 = _as_rows(a, 1), _as_rows(other, 0)
    rows, cols = o2.shape
    tr = _pick(rows, 256)

    def body(c_ref, a_ref, o_ref, out_ref):
        out_ref[...] = (a_ref[0].astype(F32) + o_ref[...].astype(F32)).astype(out_ref.dtype)

    out = pl.pallas_call(
        body,
        grid_spec=pltpu.PrefetchScalarGridSpec(
            num_scalar_prefetch=1, grid=(rows // tr,),
            in_specs=[pl.BlockSpec((1, tr, cols), lambda i, c_ref: (c_ref[0], i, 0)),
                      pl.BlockSpec((tr, cols), lambda i, c_ref: (i, 0))],
            out_specs=pl.BlockSpec((tr, cols), lambda i, c_ref: (i, 0))),
        out_shape=_sds((rows, cols), a.dtype), compiler_params=_cparams(("parallel",)), name=name,
    )(c_idx, a3, o2)
    return out.reshape(other.shape)


def _all_reduce_small(v, name):
    r = v.shape[0]

    def body(v_ref, o_ref, slots, send_sems, recv_sems):
        x, y, c = _mesh_pos()
        me = 4 * x + 2 * y + c
        slots[me] = v_ref[...]
        cps = []
        for k in range(1, N_DEV):
            px = 1 - x if k & 4 else x
            py = 1 - y if k & 2 else y
            pc = 1 - c if k & 1 else c
            cps.append(pltpu.make_async_remote_copy(src_ref=v_ref, dst_ref=slots.at[me], send_sem=send_sems.at[k - 1],
                                                    recv_sem=recv_sems.at[k - 1], device_id=(px, py, pc), device_id_type=MESH))
        for cp in cps:
            cp.start()
        for cp in cps:
            cp.wait()
        acc = slots[0]
        for d in range(1, N_DEV):
            acc = acc + slots[d]
        o_ref[...] = acc

    vm = pl.BlockSpec(memory_space=pltpu.VMEM)
    return pl.pallas_call(
        body, in_specs=[vm], out_specs=vm, out_shape=_sds((r, LANES), F32),
        scratch_shapes=[pltpu.VMEM((N_DEV, r, LANES), F32), pltpu.SemaphoreType.DMA((N_DEV - 1,)),
                        pltpu.SemaphoreType.DMA((N_DEV - 1,))],
        compiler_params=pltpu.CompilerParams(vmem_limit_bytes=VMEM_LIMIT_BYTES), name=name,
    )(v)


def _adamw(w, g_slots, m, v, name):
    depth, rows, cols = w.shape
    ns = g_slots.shape[0]
    tr = _pick(rows, 256 if cols <= 1024 else 128)

    def body(w_ref, g_ref, m_ref, v_ref, go_ref, d_ref, mo_ref, vo_ref):
        g = g_ref[0, 0].astype(F32)
        for i in range(1, ns):
            g = g + g_ref[i, 0].astype(F32)
        m_new = ADAM_B1 * m_ref[0] + (1.0 - ADAM_B1) * g
        v_new = ADAM_B2 * v_ref[0] + (1.0 - ADAM_B2) * (g * g)
        m_hat = m_new / (1.0 - ADAM_B1 ** ADAM_STEP)
        v_hat = v_new / (1.0 - ADAM_B2 ** ADAM_STEP)
        go_ref[0] = g
        d_ref[0] = -ADAM_LR * (m_hat / (jnp.sqrt(v_hat) + ADAM_EPS) + ADAM_WD * w_ref[0])
        mo_ref[0] = m_new
        vo_ref[0] = v_new

    blk = pl.BlockSpec((1, tr, cols), lambda l, i: (l, i, 0))
    return pl.pallas_call(
        body, grid=(depth, rows // tr),
        in_specs=[blk, pl.BlockSpec((ns, 1, tr, cols), lambda l, i: (0, l, i, 0)), blk, blk],
        out_specs=[blk] * 4, out_shape=[_sds(w.shape, F32)] * 4,
        compiler_params=_cparams(("parallel", "parallel")), name=name,
    )(w, g_slots, m, v)


_ARG_NAMES = (['x', 'mem', 'positions'] + WEIGHTS + ['loss_target'] + ['m_' + n for n in WEIGHTS]
              + ['v_' + n for n in WEIGHTS])


PACK_TILE = 8 * LANES


def _pack_rows(parts):
    blocks = []
    for part in parts:
        flat = part.reshape(-1)
        pad = (-flat.shape[0]) % PACK_TILE
        blocks.append((jnp.pad(flat, (0, pad)) if pad else flat).reshape(-1, LANES))
    return jnp.concatenate(blocks, axis=0)


def _unpack_rows(packed, shapes):
    out, off = [], 0
    for shp in shapes:
        n = int(np.prod(shp))
        rows = -(-n // PACK_TILE) * 8
        out.append(packed[off:off + rows].reshape(-1)[:n].reshape(shp))
        off += rows
    return out


def kernel(x, mem, positions, norm_mix, w_in, b_gate, conv_w, conv_b, dt_bias, a_log, d_skip, ssm_norm, w_br_ret, w_br_ssm, w_out, norm_xa, norm_mem, xa_wq, xa_wkv, xa_wo, norm_mlp, mlp_w1, mlp_w2, norm_final, loss_target, m_norm_mix, m_w_in, m_b_gate, m_conv_w, m_conv_b, m_dt_bias, m_a_log, m_d_skip, m_ssm_norm, m_w_br_ret, m_w_br_ssm, m_w_out, m_norm_xa, m_norm_mem, m_xa_wq, m_xa_wkv, m_xa_wo, m_norm_mlp, m_mlp_w1, m_mlp_w2, m_norm_final, v_norm_mix, v_w_in, v_b_gate, v_conv_w, v_conv_b, v_dt_bias, v_a_log, v_d_skip, v_ssm_norm, v_w_br_ret, v_w_br_ssm, v_w_out, v_norm_xa, v_norm_mem, v_xa_wq, v_xa_wkv, v_xa_wo, v_norm_mlp, v_mlp_w1, v_mlp_w2, v_norm_final):
    d = dict(zip(_ARG_NAMES, (x, mem, positions, norm_mix, w_in, b_gate, conv_w, conv_b, dt_bias, a_log, d_skip, ssm_norm, w_br_ret, w_br_ssm, w_out, norm_xa, norm_mem, xa_wq, xa_wkv, xa_wo, norm_mlp, mlp_w1, mlp_w2, norm_final, loss_target, m_norm_mix, m_w_in, m_b_gate, m_conv_w, m_conv_b, m_dt_bias, m_a_log, m_d_skip, m_ssm_norm, m_w_br_ret, m_w_br_ssm, m_w_out, m_norm_xa, m_norm_mem, m_xa_wq, m_xa_wkv, m_xa_wo, m_norm_mlp, m_mlp_w1, m_mlp_w2, m_norm_final, v_norm_mix, v_w_in, v_b_gate, v_conv_w, v_conv_b, v_dt_bias, v_a_log, v_d_skip, v_ssm_norm, v_w_br_ret, v_w_br_ssm, v_w_out, v_norm_xa, v_norm_mem, v_xa_wq, v_xa_wkv, v_xa_wo, v_norm_mlp, v_mlp_w1, v_mlp_w2, v_norm_final)))
    blocks = [{k: d[k][l] if k == 'conv_w' else d[k][l].astype(MXU_DTYPE) for k in SHARDED} for l in range(DEPTH)]
    small = {k: d[k] for k in SMALL}
    loss, grad_x, grads, by_chip_l = _step(d['x'][0], d['mem'][0], d['positions'][0], small, blocks, d['loss_target'][0])
    by_chip = [jnp.stack([by_chip_l[l][k] for l in range(DEPTH)], axis=1) for k in SHARDED]
    small_g = [grads[k] if k == 'norm_final' else jnp.stack(grads[k]) for k in SMALL]
    total = _all_reduce_small(_pack_rows([loss] + small_g), "all_reduce_small")
    loss_out = total[0, 0]
    res = {}
    for k, g4 in zip(SHARDED, by_chip):
        res[k] = _adamw(d[k], g4, d['m_' + k], d['v_' + k], f"adamw_{k}")
    small_shapes = [d[k].shape for k in SMALL]
    pk = lambda pre: _pack_rows([d[pre + k] for k in SMALL])
    outs = _adamw(pk('')[None], total[8:][None, None], pk('m_')[None], pk('v_')[None], "adamw_small")
    unpacked = [_unpack_rows(o[0], small_shapes) for o in outs]
    for i, k in enumerate(SMALL):
        res[k] = [unpacked[j][i] for j in range(4)]
    return (loss_out, grad_x[None], *[res[k][0] for k in WEIGHTS], *[res[k][1] for k in WEIGHTS],
            *[res[k][2] for k in WEIGHTS], *[res[k][3] for k in WEIGHTS])
```
